```python
import math
import jax, jax.numpy as jnp
from jax import lax
import numpy as np

D_MODEL = 1024
BATCH = 8
SEQ = 4096
DEPTH = 2

HEAD_DIM = 64
MIX_WIDTH = D_MODEL
N_GROUPS = 4
GROUP_WIDTH = MIX_WIDTH // N_GROUPS
HEADS_PER_GROUP = GROUP_WIDTH // HEAD_DIM
FOX_HEADS = HEADS_PER_GROUP
MOBA_HEADS = HEADS_PER_GROUP
SB_HEADS = HEADS_PER_GROUP
SWA_HEADS = HEADS_PER_GROUP
SWA_KV_HEADS = SWA_HEADS // 2
SWA_KV_WIDTH = SWA_KV_HEADS * HEAD_DIM
QUERY_BLOCK = 128
MOBA_BLOCK = 256
MOBA_TOPK = 3
MOBA_QUERY_CHUNK = 32
SWA_WINDOW = 128
NUM_BUCKETS = 32
REL_MAX_DISTANCE = 1024
RMS_EPS = 1e-6
FORGET_BIAS_INIT = 2.0
NEG_INF = -1e30
ATTN_SCALE = HEAD_DIM ** -0.5
SPLIT_SIZES = (
    GROUP_WIDTH, GROUP_WIDTH, GROUP_WIDTH, FOX_HEADS, GROUP_WIDTH,
    GROUP_WIDTH, GROUP_WIDTH, GROUP_WIDTH, GROUP_WIDTH,
    GROUP_WIDTH, GROUP_WIDTH, GROUP_WIDTH, GROUP_WIDTH,
    GROUP_WIDTH, SWA_KV_WIDTH, SWA_KV_WIDTH, GROUP_WIDTH,
)
IN_WIDTH = sum(SPLIT_SIZES)

kernel_name = 'hybrid_fox_moba_stickbreak_swa_block'


def rms_norm(x, gain):
    xf = x.astype(jnp.float32)
    y = xf * lax.rsqrt(jnp.mean(xf * xf, axis=-1, keepdims=True) + RMS_EPS)
    return (y * gain.astype(jnp.float32)).astype(x.dtype)


def split_columns(proj):
    parts, off = [], 0
    for w in SPLIT_SIZES:
        parts.append(proj[..., off:off + w])
        off += w
    return parts


def to_heads(t, n):
    b, s, _ = t.shape
    return t.reshape(b, s, n, HEAD_DIM).transpose(0, 2, 1, 3)


def from_heads(o):
    b, h, s, d = o.shape
    return o.transpose(0, 2, 1, 3).reshape(b, s, h * d)


def unblock(out):
    nb, b, h, qb, d = out.shape
    return out.transpose(1, 2, 0, 3, 4).reshape(b, h, nb * qb, d)


def rel_bucket(dist):
    max_exact = NUM_BUCKETS // 2
    d = jnp.maximum(dist, 0)
    log_ratio = jnp.log(jnp.maximum(d, 1).astype(jnp.float32) / max_exact) / math.log(REL_MAX_DISTANCE / max_exact)
    large = max_exact + (log_ratio * (NUM_BUCKETS - max_exact)).astype(jnp.int32)
    large = jnp.minimum(large, NUM_BUCKETS - 1)
    return jnp.where(d < max_exact, d, large)


def forgetting_attention(q, k, v, log_f):
    b, h, t_len, d = q.shape
    c = jnp.cumsum(log_f, axis=-1)
    pos = jnp.arange(t_len)

    def block(i):
        start = i * QUERY_BLOCK
        q_i = lax.dynamic_slice_in_dim(q, start, QUERY_BLOCK, axis=2)
        c_i = lax.dynamic_slice_in_dim(c, start, QUERY_BLOCK, axis=2)
        t = start + jnp.arange(QUERY_BLOCK)
        s = jnp.einsum('bhqd,bhkd->bhqk', q_i, k).astype(jnp.float32) * ATTN_SCALE
        s = s + c_i[..., :, None] - c[..., None, :]
        s = jnp.where(t[:, None] >= pos[None, :], s, NEG_INF)
        p = jax.nn.softmax(s, axis=-1).astype(v.dtype)
        return jnp.einsum('bhqk,bhkd->bhqd', p, v)

    return unblock(lax.map(block, jnp.arange(t_len // QUERY_BLOCK)))


def moba_attention(q, k, v, rel_bias_h):
    b, h, t_len, d = q.shape
    t_pad = -(-t_len // MOBA_BLOCK) * MOBA_BLOCK
    if t_pad != t_len:
        padw = ((0, 0), (0, 0), (0, t_pad - t_len), (0, 0))
        q, k, v = jnp.pad(q, padw), jnp.pad(k, padw), jnp.pad(v, padw)
    nblk = t_pad // MOBA_BLOCK
    k_eff = min(MOBA_TOPK, nblk)
    k_blocks = k.reshape(b, h, nblk, MOBA_BLOCK, d)
    v_blocks = v.reshape(b, h, nblk, MOBA_BLOCK, d)
    k_mean = jnp.mean(k_blocks.astype(jnp.float32), axis=3)
    gate = jnp.einsum('bhtd,bhnd->bhtn', q.astype(jnp.float32), k_mean)
    q_blk = jnp.arange(t_pad) // MOBA_BLOCK
    past = jnp.arange(nblk)[None, :] < q_blk[:, None]
    gate = jnp.where(past, gate, NEG_INF)
    _, sel = lax.top_k(gate, k_eff)
    n_valid = jnp.minimum(q_blk, k_eff)
    sel_valid = jnp.arange(k_eff)[None, :] < n_valid[:, None]
    b_idx = jnp.arange(b)[:, None, None, None]
    h_idx = jnp.arange(h)[None, :, None, None]
    h_idx5 = h_idx[..., None]
    blk_off = jnp.arange(MOBA_BLOCK)

    def chunk(ci):
        start = ci * MOBA_QUERY_CHUNK
        q_c = lax.dynamic_slice_in_dim(q, start, MOBA_QUERY_CHUNK, axis=2)
        sel_c = lax.dynamic_slice_in_dim(sel, start, MOBA_QUERY_CHUNK, axis=2)
        valid_c = lax.dynamic_slice_in_dim(sel_valid, start, MOBA_QUERY_CHUNK, axis=0)
        t = start + jnp.arange(MOBA_QUERY_CHUNK)
        k_sel = k_blocks[b_idx, h_idx, sel_c]
        v_sel = v_blocks[b_idx, h_idx, sel_c]
        s_sel = jnp.einsum('bhqd,bhqnkd->bhqnk', q_c, k_sel).astype(jnp.float32) * ATTN_SCALE
        key_pos = sel_c[..., None] * MOBA_BLOCK + blk_off
        dist = t[None, None, :, None, None] - key_pos
        s_sel = s_sel + rel_bias_h[h_idx5, rel_bucket(dist)].astype(jnp.float32)
        s_sel = jnp.where(valid_c[None, None, :, :, None], s_sel, NEG_INF)
        own_start = (start // MOBA_BLOCK) * MOBA_BLOCK
        k_own = lax.dynamic_slice_in_dim(k, own_start, MOBA_BLOCK, axis=2)
        v_own = lax.dynamic_slice_in_dim(v, own_start, MOBA_BLOCK, axis=2)
        s_own = jnp.einsum('bhqd,bhkd->bhqk', q_c, k_own).astype(jnp.float32) * ATTN_SCALE
        dist_own = t[:, None] - (own_start + blk_off)[None, :]
        s_own = s_own + rel_bias_h[:, rel_bucket(dist_own)][None].astype(jnp.float32)
        s_own = jnp.where(dist_own >= 0, s_own, NEG_INF)
        n_sel = k_eff * MOBA_BLOCK
        s_all = jnp.concatenate([s_sel.reshape(b, h, MOBA_QUERY_CHUNK, n_sel), s_own], axis=-1)
        p = jax.nn.softmax(s_all, axis=-1).astype(v.dtype)
        p_sel = p[..., :n_sel].reshape(b, h, MOBA_QUERY_CHUNK, k_eff, MOBA_BLOCK)
        p_own = p[..., n_sel:]
        return (jnp.einsum('bhqnk,bhqnkd->bhqd', p_sel, v_sel)
                + jnp.einsum('bhqk,bhkd->bhqd', p_own, v_own))

    out = unblock(lax.map(chunk, jnp.arange(t_pad // MOBA_QUERY_CHUNK)))
    return out[:, :, :t_len]


def stick_breaking_attention(q, k, v):
    b, h, t_len, d = q.shape
    pos = jnp.arange(t_len)

    def block(i):
        start = i * QUERY_BLOCK
        q_i = lax.dynamic_slice_in_dim(q, start, QUERY_BLOCK, axis=2)
        t = start + jnp.arange(QUERY_BLOCK)
        z = jnp.einsum('bhqd,bhkd->bhqk', q_i, k).astype(jnp.float32) * ATTN_SCALE
        strict = pos[None, :] < t[:, None]
        log_keep = jnp.where(strict, jax.nn.log_sigmoid(-z), 0.0)
        suffix = lax.cumsum(log_keep, axis=3, reverse=True) - log_keep
        a = jnp.where(strict, jnp.exp(jax.nn.log_sigmoid(z) + suffix), 0.0)
        return jnp.einsum('bhqk,bhkd->bhqd', a.astype(v.dtype), v)

    return unblock(lax.map(block, jnp.arange(t_len // QUERY_BLOCK)))


def swa_sink_attention(q, k, v, sinks, rel_bias_h):
    b, h, t_len, d = q.shape
    hkv = k.shape[1]
    g = h // hkv
    w = SWA_WINDOW
    nb = t_len // w
    qb = q.reshape(b, hkv, g, nb, w, d)

    def band(a):
        ab = a.reshape(b, hkv, nb, w, d)
        prev = jnp.pad(ab, ((0, 0), (0, 0), (1, 0), (0, 0), (0, 0)))[:, :, :-1]
        return jnp.concatenate([prev, ab], axis=3)

    kb, vb = band(k), band(v)
    s = jnp.einsum('bkgnqd,bknsd->bkgnqs', qb, kb).astype(jnp.float32) * ATTN_SCALE
    qi = jnp.arange(w)
    kj = jnp.arange(2 * w)
    dist = qi[:, None] + w - kj[None, :]
    bias = rel_bias_h[:, rel_bucket(dist)].astype(jnp.float32).reshape(hkv, g, 1, w, 2 * w)
    key_pos = jnp.arange(nb)[:, None] * w - w + kj[None, :]
    allowed = ((dist >= 0) & (dist < w))[None] & (key_pos >= 0)[:, None, :]
    s = jnp.where(allowed, s + bias, NEG_INF)
    sink = jnp.broadcast_to(sinks.astype(jnp.float32).reshape(1, hkv, g, 1, 1, 1), s.shape[:-1] + (1,))
    p = jax.nn.softmax(jnp.concatenate([s, sink], axis=-1), axis=-1)[..., :-1]
    o = jnp.einsum('bkgnqs,bknsd->bkgnqd', p.astype(v.dtype), vb)
    return o.reshape(b, h, t_len, d)


def hybrid_layer(x, norm_gain, w_in, b_forget, fox_qk_gain, moba_qk_gain, swa_qk_gain, sinks, w_out, rel_bias):
    hn = rms_norm(x, norm_gain)
    proj = jnp.einsum('btd,de->bte', hn, w_in)
    (fq, fk, fv, ff, fg, mq, mk, mv, mg, sq, sk, sv, sg, wq, wk, wv, wg) = split_columns(proj)
    log_f = jax.nn.log_sigmoid((ff + b_forget).astype(jnp.float32)).transpose(0, 2, 1)
    o_fox = forgetting_attention(rms_norm(to_heads(fq, FOX_HEADS), fox_qk_gain[0]),
                                 rms_norm(to_heads(fk, FOX_HEADS), fox_qk_gain[1]),
                                 to_heads(fv, FOX_HEADS), log_f)
    o_moba = moba_attention(rms_norm(to_heads(mq, MOBA_HEADS), moba_qk_gain[0]),
                            rms_norm(to_heads(mk, MOBA_HEADS), moba_qk_gain[1]),
                            to_heads(mv, MOBA_HEADS), rel_bias[:, :MOBA_HEADS].T)
    o_sb = stick_breaking_attention(to_heads(sq, SB_HEADS), to_heads(sk, SB_HEADS), to_heads(sv, SB_HEADS))
    o_swa = swa_sink_attention(rms_norm(to_heads(wq, SWA_HEADS), swa_qk_gain[0]),
                               rms_norm(to_heads(wk, SWA_KV_HEADS), swa_qk_gain[1]),
                               to_heads(wv, SWA_KV_HEADS), sinks, rel_bias[:, MOBA_HEADS:].T)
    y = jnp.concatenate([from_heads(o_fox) * jax.nn.silu(fg),
                         from_heads(o_moba) * jax.nn.silu(mg),
                         from_heads(o_sb) * jax.nn.silu(sg),
                         from_heads(o_swa) * jax.nn.silu(wg)], axis=-1)
    return x + jnp.einsum('bte,ed->btd', y, w_out)


def setup_inputs(seed: int = 0) -> dict:
    key = jax.random.key(seed)
    ks = jax.random.split(key, 10)
    x = jax.random.normal(ks[0], (BATCH, SEQ, D_MODEL), jnp.float32)
    norm_gain = 1.0 + 0.02 * jax.random.normal(ks[1], (DEPTH, D_MODEL), jnp.float32)
    w_in = jax.random.normal(ks[2], (DEPTH, D_MODEL, IN_WIDTH), jnp.float32) * D_MODEL ** -0.5
    b_forget = FORGET_BIAS_INIT + 0.1 * jax.random.normal(ks[3], (DEPTH, FOX_HEADS), jnp.float32)
    fox_qk_gain = 1.0 + 0.02 * jax.random.normal(ks[4], (DEPTH, 2, HEAD_DIM), jnp.float32)
    moba_qk_gain = 1.0 + 0.02 * jax.random.normal(ks[5], (DEPTH, 2, HEAD_DIM), jnp.float32)
    swa_qk_gain = 1.0 + 0.02 * jax.random.normal(ks[6], (DEPTH, 2, HEAD_DIM), jnp.float32)
    sinks = 0.5 * jax.random.normal(ks[7], (DEPTH, SWA_HEADS), jnp.float32)
    w_out = jax.random.normal(ks[8], (DEPTH, MIX_WIDTH, D_MODEL), jnp.float32) * MIX_WIDTH ** -0.5
    rel_bias = 0.1 * jax.random.normal(ks[9], (NUM_BUCKETS, MOBA_HEADS + SWA_HEADS), jnp.float32)
    return {'x': x, 'norm_gain': norm_gain, 'w_in': w_in, 'b_forget': b_forget,
            'fox_qk_gain': fox_qk_gain, 'moba_qk_gain': moba_qk_gain, 'swa_qk_gain': swa_qk_gain,
            'sinks': sinks, 'w_out': w_out, 'rel_bias': rel_bias}


def reference(x, norm_gain, w_in, b_forget, fox_qk_gain, moba_qk_gain, swa_qk_gain, sinks, w_out, rel_bias):
    for layer in range(DEPTH):
        x = hybrid_layer(x, norm_gain[layer], w_in[layer], b_forget[layer], fox_qk_gain[layer],
                         moba_qk_gain[layer], swa_qk_gain[layer], sinks[layer], w_out[layer], rel_bias)
    return x
```

```python
import functools
import math

import jax
import jax.numpy as jnp
from jax import lax
from jax.experimental import pallas as pl
from jax.experimental.pallas import tpu as pltpu

HEAD_DIM = 64
LANES = 128
GROUP_WIDTH = 256
MOBA_BLOCK = 256
MOBA_TOPK = 3
SWA_WINDOW = 128
NUM_BUCKETS = 32
REL_MAX_DISTANCE = 1024
RMS_EPS = 1e-6
NEG_INF = -1e30
ATTN_SCALE = HEAD_DIM ** -0.5
MOBA_NEAR_TILES = REL_MAX_DISTANCE // MOBA_BLOCK + 1
FOX_FORGET_COLS = 4

COL_FQ, COL_FK, COL_FV, COL_FG = 0, 2, 4, 6
COL_MQ, COL_MK, COL_MV, COL_MG = 8, 10, 12, 14
COL_SQ, COL_SK, COL_SV, COL_SG = 16, 18, 20, 22
COL_WQ, COL_WK, COL_WV, COL_WG = 24, 26, 27, 28
COL_FF = 30
PROJ_WIDTH = 32 * LANES

_MXU_DTYPE = jnp.bfloat16
_F32 = jnp.float32
_VMEM_LIMIT = 48 * 1024 * 1024

_NT = (((1,), (1,)), ((), ()))


def _lane_iota(shape):
    return lax.broadcasted_iota(jnp.int32, shape, len(shape) - 1)


def _pair_rms(x, gain_row):
    low = _lane_iota(x.shape) < HEAD_DIM
    sq = x * x
    ms_lo = jnp.sum(jnp.where(low, sq, 0.0), axis=-1, keepdims=True)
    ms_hi = jnp.sum(jnp.where(low, 0.0, sq), axis=-1, keepdims=True)
    ms = jnp.where(low, ms_lo, ms_hi) * (1.0 / HEAD_DIM)
    return x * lax.rsqrt(ms + RMS_EPS) * gain_row


def _silu(g):
    return g * (1.0 / (1.0 + jnp.exp(-g)))


def _softplus(z):
    return jnp.maximum(z, 0.0) + jnp.log1p(jnp.exp(-jnp.abs(z)))


def _split3(x):
    hi = x.astype(_MXU_DTYPE).astype(_F32)
    r = x - hi
    mid = r.astype(_MXU_DTYPE).astype(_F32)
    return hi, mid, r - mid


def _own_half(hh, shape):
    lane = _lane_iota(shape)
    return (lane < HEAD_DIM) if hh == 0 else (lane >= HEAD_DIM)


def _flash_update(s, v_tile, m, l, acc):
    m_new = jnp.maximum(m, jnp.max(s, axis=-1, keepdims=True))
    alpha = jnp.exp(m - m_new)
    p = jnp.exp(s - m_new)
    l = alpha * l + jnp.sum(p, axis=-1, keepdims=True)
    acc = alpha * acc + jnp.dot(p.astype(_MXU_DTYPE), v_tile, preferred_element_type=_F32)
    return m_new, l, acc


def _causal(tq):
    return (lax.broadcasted_iota(jnp.int32, (tq, tq), 0)
            >= lax.broadcasted_iota(jnp.int32, (tq, tq), 1))


def _inproj_kernel(x_ref, gain_ref, w_ref, o_ref, *, tn):
    x = x_ref[...]
    ms = jnp.mean(x * x, axis=-1, keepdims=True)
    hn = (x * lax.rsqrt(ms + RMS_EPS) * gain_ref[...]).astype(_MXU_DTYPE)
    for n in range(o_ref.shape[1] // tn):
        o_ref[:, n * tn:(n + 1) * tn] = jnp.dot(
            hn, w_ref[:, n * tn:(n + 1) * tn], preferred_element_type=_F32)


def _inproj(x2d, gain_row, w):
    m, d = x2d.shape
    n = w.shape[1]
    tm = 256
    return pl.pallas_call(
        functools.partial(_inproj_kernel, tn=1024),
        grid=(m // tm,),
        in_specs=[pl.BlockSpec((tm, d), lambda i: (i, 0)),
                  pl.BlockSpec((1, d), lambda i: (0, 0)),
                  pl.BlockSpec((d, n), lambda i: (0, 0))],
        out_specs=pl.BlockSpec((tm, n), lambda i: (i, 0)),
        out_shape=jax.ShapeDtypeStruct((m, n), _F32),
        compiler_params=pltpu.CompilerParams(
            dimension_semantics=("arbitrary",), vmem_limit_bytes=_VMEM_LIMIT),
        name="inproj",
    )(x2d, gain_row, w)


def _rel_bucket(dist):
    max_exact = NUM_BUCKETS // 2
    d = jnp.maximum(dist, 0)
    log_ratio = (jnp.log(jnp.maximum(d, 1).astype(_F32) / max_exact)
                 / math.log(REL_MAX_DISTANCE / max_exact))
    large = max_exact + (log_ratio * (NUM_BUCKETS - max_exact)).astype(jnp.int32)
    large = jnp.minimum(large, NUM_BUCKETS - 1)
    return jnp.where(d < max_exact, d, large)


def _bias_kernel(rb_ref, bm_ref, bs_ref, om_ref, os_ref, *, n_heads):
    h = pl.program_id(0)
    for d in range(bm_ref.shape[0]):
        b = bm_ref[d]
        acc = jnp.zeros(b.shape, _F32)
        for k in range(NUM_BUCKETS):
            acc = jnp.where(b == k, rb_ref[k, h], acc)
        om_ref[0, d] = acc
    b = bs_ref[...]
    acc = jnp.zeros(b.shape, _F32)
    for k in range(NUM_BUCKETS):
        acc = jnp.where(b == k, rb_ref[k, n_heads + h], acc)
    os_ref[0] = acc


def _bias_tiles(rel_bias):
    n_heads = rel_bias.shape[1] // 2
    blk = MOBA_BLOCK
    i = jnp.arange(blk)[:, None]
    j = jnp.arange(blk)[None, :]
    bm = jnp.stack([_rel_bucket(d * blk + i - j) for d in range(MOBA_NEAR_TILES)])
    w = SWA_WINDOW
    bs = _rel_bucket(jnp.arange(w)[:, None] + w - jnp.arange(2 * w)[None, :])
    return pl.pallas_call(
        functools.partial(_bias_kernel, n_heads=n_heads),
        grid=(n_heads,),
        in_specs=[pl.BlockSpec(memory_space=pltpu.SMEM),
                  pl.BlockSpec(bm.shape, lambda h: (0, 0, 0)),
                  pl.BlockSpec(bs.shape, lambda h: (0, 0))],
        out_specs=[pl.BlockSpec((1,) + bm.shape, lambda h: (h, 0, 0, 0)),
                   pl.BlockSpec((1,) + bs.shape, lambda h: (h, 0, 0))],
        out_shape=[jax.ShapeDtypeStruct((n_heads,) + bm.shape, _F32),
                   jax.ShapeDtypeStruct((n_heads,) + bs.shape, _F32)],
        compiler_params=pltpu.CompilerParams(dimension_semantics=("arbitrary",)),
        name="bias_tiles",
    )(rel_bias, bm, bs)


def _fox_kernel(q_ref, k_ref, v_ref, ff_ref, g_ref, qkg_ref, bf_ref, o_ref,
                kaug_sc, v_sc, c_sc, *, tq):
    p = pl.program_id(1)
    i = pl.program_id(2)
    t_len = k_ref.shape[1]

    def head_column(x, hh):
        return jnp.sum(jnp.where(_lane_iota(x.shape) == 2 * p + hh, x, 0.0),
                       axis=-1, keepdims=True)

    @pl.when(i == 0)
    def _prep():
        ff = ff_ref[0] + bf_ref[...]
        log_f = jnp.minimum(ff, 0.0) - jnp.log1p(jnp.exp(-jnp.abs(ff)))
        ch = 256
        tri = (lax.broadcasted_iota(jnp.int32, (ch, ch), 0)
               >= lax.broadcasted_iota(jnp.int32, (ch, ch), 1)).astype(_F32)
        carry = jnp.zeros((1, LANES), _F32)
        for r in range(t_len // ch):
            inc = jnp.dot(tri, log_f[r * ch:(r + 1) * ch], precision=lax.Precision.HIGHEST,
                          preferred_element_type=_F32) + carry
            c_sc[r * ch:(r + 1) * ch, :] = inc
            carry = inc[ch - 1:ch, :]
        kn = _pair_rms(k_ref[0], qkg_ref[1:2, :])
        c_all = c_sc[...]
        lane = _lane_iota(kn.shape)
        for hh in range(2):
            a0 = HEAD_DIM if hh == 0 else 0
            hi, mid, lo = _split3(-head_column(c_all, hh))
            aug = jnp.where(lane == a0, hi,
                            jnp.where(lane == a0 + 1, mid,
                                      jnp.where(lane == a0 + 2, lo,
                                                jnp.where((lane >= a0 + 3) & (lane < a0 + 6), 1.0, 0.0))))
            kaug_sc[hh] = jnp.where(_own_half(hh, kn.shape), kn, aug).astype(_MXU_DTYPE)
        v_sc[...] = v_ref[0].astype(_MXU_DTYPE)

    q = _pair_rms(q_ref[0], qkg_ref[0:1, :]) * ATTN_SCALE
    row0 = pl.multiple_of(i * tq, tq)
    c_t = c_sc[pl.ds(row0, tq), :]
    lane = _lane_iota(q.shape)
    outs = []
    for hh in range(2):
        a0 = HEAD_DIM if hh == 0 else 0
        hi, mid, lo = _split3(head_column(c_t, hh))
        aug = jnp.where((lane >= a0) & (lane < a0 + 3), 1.0,
                        jnp.where(lane == a0 + 3, hi,
                                  jnp.where(lane == a0 + 4, mid,
                                            jnp.where(lane == a0 + 5, lo, 0.0))))
        q_aug = jnp.where(_own_half(hh, q.shape), q, aug).astype(_MXU_DTYPE)

        s = lax.dot_general(q_aug, kaug_sc[hh, pl.ds(row0, tq), :], _NT,
                            preferred_element_type=_F32)
        s = jnp.where(_causal(tq), s, NEG_INF)
        m = jnp.max(s, axis=-1, keepdims=True)
        pr = jnp.exp(s - m)
        l = jnp.sum(pr, axis=-1, keepdims=True)
        acc = jnp.dot(pr.astype(_MXU_DTYPE), v_sc[pl.ds(row0, tq), :], preferred_element_type=_F32)

        def body(j, carry, q_aug=q_aug, hh=hh):
            r = pl.multiple_of(j * tq, tq)
            s = lax.dot_general(q_aug, kaug_sc[hh, pl.ds(r, tq), :], _NT,
                                preferred_element_type=_F32)
            return _flash_update(s, v_sc[pl.ds(r, tq), :], *carry)

        m, l, acc = lax.fori_loop(0, i, body, (m, l, acc))
        outs.append(acc / l)
    o = jnp.where(_own_half(0, outs[0].shape), outs[0], outs[1])
    o_ref[0] = (o * _silu(g_ref[0])).astype(o_ref.dtype)


def _fox(proj, bf_row, qk_gain, *, tq=256):
    b, t, _ = proj.shape
    return pl.pallas_call(
        functools.partial(_fox_kernel, tq=tq),
        grid=(b, 2, t // tq),
        in_specs=[pl.BlockSpec((1, tq, LANES), lambda bb, p, i: (bb, i, COL_FQ + p)),
                  pl.BlockSpec((1, t, LANES), lambda bb, p, i: (bb, 0, COL_FK + p)),
                  pl.BlockSpec((1, t, LANES), lambda bb, p, i: (bb, 0, COL_FV + p)),
                  pl.BlockSpec((1, t, LANES), lambda bb, p, i: (bb, 0, COL_FF)),
                  pl.BlockSpec((1, tq, LANES), lambda bb, p, i: (bb, i, COL_FG + p)),
                  pl.BlockSpec((2, LANES), lambda bb, p, i: (0, 0)),
                  pl.BlockSpec((1, LANES), lambda bb, p, i: (0, 0))],
        out_specs=pl.BlockSpec((1, tq, LANES), lambda bb, p, i: (bb, i, p)),
        out_shape=jax.ShapeDtypeStruct((b, t, GROUP_WIDTH), _MXU_DTYPE),
        scratch_shapes=[pltpu.VMEM((2, t, LANES), _MXU_DTYPE),
                        pltpu.VMEM((t, LANES), _MXU_DTYPE),
                        pltpu.VMEM((t, LANES), _F32)],
        compiler_params=pltpu.CompilerParams(
            dimension_semantics=("arbitrary", "arbitrary", "arbitrary"),
            vmem_limit_bytes=_VMEM_LIMIT),
        name="fox",
    )(proj, proj, proj, proj, proj, qk_gain, bf_row)


def _moba_kernel(rb_ref, q_ref, k_ref, v_ref, g_ref, qkg_ref, bias_ref, o_ref,
                 kaug_sc, v_sc, kmean_sc):
    p = pl.program_id(1)
    i = pl.program_id(2)
    t_len = k_ref.shape[1]
    blk = MOBA_BLOCK
    nblk = t_len // blk
    near = MOBA_NEAR_TILES

    @pl.when(i == 0)
    def _prep():
        kn = _pair_rms(k_ref[0], qkg_ref[1:2, :])
        kmean_sc[...] = jnp.zeros(kmean_sc.shape, _F32)
        for n in range(nblk):
            kmean_sc[n:n + 1, :] = jnp.mean(kn[n * blk:(n + 1) * blk], axis=0, keepdims=True)
        lane = _lane_iota(kn.shape)
        row_blk = lax.broadcasted_iota(jnp.int32, kn.shape, 0) // blk
        for hh in range(2):
            a0 = HEAD_DIM if hh == 0 else 0
            onehot = jnp.where((lane - a0 == row_blk) | (lane - a0 - 16 == row_blk), 1.0, 0.0)
            kaug_sc[hh] = jnp.where(_own_half(hh, kn.shape), kn, onehot).astype(_MXU_DTYPE)
        v_sc[...] = v_ref[0].astype(_MXU_DTYPE)

    qn = _pair_rms(q_ref[0], qkg_ref[0:1, :])
    row0 = pl.multiple_of(i * blk, blk)
    lane = _lane_iota(qn.shape)
    lane_f = lane.astype(_F32)
    past = lane < i
    outs = []
    for hh in range(2):
        own = _own_half(hh, qn.shape)
        gate = lax.dot_general(jnp.where(own, qn, 0.0), kmean_sc[...], _NT,
                               precision=lax.Precision.HIGHEST, preferred_element_type=_F32)
        cand = jnp.where(past, gate, -jnp.inf)
        sel = jnp.zeros(qn.shape, _F32)
        for _ in range(MOBA_TOPK):
            mx = jnp.max(cand, axis=-1, keepdims=True)
            is_max = (cand == mx) & (mx > -jnp.inf)
            first = jnp.min(jnp.where(is_max, lane_f, float(LANES)), axis=-1, keepdims=True)
            pick = lane_f == first
            sel = jnp.where(pick, 1.0, sel)
            cand = jnp.where(pick, -jnp.inf, cand)
        c_far = jnp.full((1, LANES), rb_ref[NUM_BUCKETS - 1, 2 * p + hh], _F32)
        far_hi = c_far.astype(_MXU_DTYPE).astype(_F32)
        far_lo = c_far - far_hi
        pen = jnp.where(past & (sel == 0.0), NEG_INF, 0.0)
        aug = jnp.where(lane < 16, pen + jnp.where(i - lane >= near, far_hi, 0.0),
                        jnp.where((lane < 32) & (i - (lane - 16) >= near), far_lo, 0.0))
        if hh == 0:
            aug = pltpu.roll(aug, HEAD_DIM, axis=1)
        q_aug = jnp.where(own, qn * ATTN_SCALE, aug).astype(_MXU_DTYPE)

        def scores(r, q_aug=q_aug, hh=hh):
            return lax.dot_general(q_aug, kaug_sc[hh, pl.ds(r, blk), :], _NT,
                                   preferred_element_type=_F32)

        s = scores(row0) + bias_ref[hh, 0]
        s = jnp.where(_causal(blk), s, NEG_INF)
        m = jnp.max(s, axis=-1, keepdims=True)
        pr = jnp.exp(s - m)
        l = jnp.sum(pr, axis=-1, keepdims=True)
        acc = jnp.dot(pr.astype(_MXU_DTYPE), v_sc[pl.ds(row0, blk), :], preferred_element_type=_F32)

        def near_body(j, carry, hh=hh, scores=scores):
            r = pl.multiple_of(j * blk, blk)
            return _flash_update(scores(r) + bias_ref[hh, i - j], v_sc[pl.ds(r, blk), :], *carry)

        def far_body(j, carry, scores=scores):
            r = pl.multiple_of(j * blk, blk)
            return _flash_update(scores(r), v_sc[pl.ds(r, blk), :], *carry)

        near_lo = jnp.maximum(i - (near - 1), 0)
        carry = lax.fori_loop(near_lo, i, near_body, (m, l, acc))
        m, l, acc = lax.fori_loop(0, near_lo, far_body, carry)
        outs.append(acc / l)
    o = jnp.where(_own_half(0, outs[0].shape), outs[0], outs[1])
    o_ref[0] = (o * _silu(g_ref[0])).astype(o_ref.dtype)


def _moba(proj, rel_bias, qk_gain, bias_tiles):
    b, t, _ = proj.shape
    blk = MOBA_BLOCK
    assert t % blk == 0 and t // blk <= 16
    return pl.pallas_call(
        _moba_kernel,
        grid=(b, 2, t // blk),
        in_specs=[pl.BlockSpec(memory_space=pltpu.SMEM),
                  pl.BlockSpec((1, blk, LANES), lambda bb, p, i: (bb, i, COL_MQ + p)),
                  pl.BlockSpec((1, t, LANES), lambda bb, p, i: (bb, 0, COL_MK + p)),
                  pl.BlockSpec((1, t, LANES), lambda bb, p, i: (bb, 0, COL_MV + p)),
                  pl.BlockSpec((1, blk, LANES), lambda bb, p, i: (bb, i, COL_MG + p)),
                  pl.BlockSpec((2, LANES), lambda bb, p, i: (0, 0)),
                  pl.BlockSpec((2, MOBA_NEAR_TILES, blk, blk), lambda bb, p, i: (p, 0, 0, 0))],
        out_specs=pl.BlockSpec((1, blk, LANES), lambda bb, p, i: (bb, i, p)),
        out_shape=jax.ShapeDtypeStruct((b, t, GROUP_WIDTH), _MXU_DTYPE),
        scratch_shapes=[pltpu.VMEM((2, t, LANES), _MXU_DTYPE),
                        pltpu.VMEM((t, LANES), _MXU_DTYPE),
                        pltpu.VMEM((LANES, LANES), _F32)],
        compiler_params=pltpu.CompilerParams(
            dimension_semantics=("arbitrary", "arbitrary", "arbitrary"),
            vmem_limit_bytes=_VMEM_LIMIT),
        name="moba",
    )(rel_bias, proj, proj, proj, proj, qk_gain, bias_tiles)


def _sb_kernel(q_ref, k_ref, v_ref, g_ref, o_ref, k_sc, v_sc, *, tq):
    i = pl.program_id(2)

    @pl.when(i == 0)
    def _prep():
        k_sc[...] = k_ref[0].astype(_MXU_DTYPE)
        v_sc[...] = v_ref[0].astype(_MXU_DTYPE)

    q = q_ref[0] * ATTN_SCALE
    row0 = pl.multiple_of(i * tq, tq)
    after = (lax.broadcasted_iota(jnp.int32, (tq, tq), 0)
             > lax.broadcasted_iota(jnp.int32, (tq, tq), 1)).astype(_MXU_DTYPE)
    strict = (lax.broadcasted_iota(jnp.int32, (tq, tq), 0)
              > lax.broadcasted_iota(jnp.int32, (tq, tq), 1))

    def suffix(log_keep):
        hi = log_keep.astype(_MXU_DTYPE)
        lo = (log_keep - hi.astype(_F32)).astype(_MXU_DTYPE)
        return (jnp.dot(hi, after, preferred_element_type=_F32)
                + jnp.dot(lo, after, preferred_element_type=_F32))

    outs = []
    for hh in range(2):
        q_h = jnp.where(_own_half(hh, q.shape), q, 0.0).astype(_MXU_DTYPE)

        def logits(r, q_h=q_h):
            return lax.dot_general(q_h, k_sc[pl.ds(r, tq), :], _NT, preferred_element_type=_F32)

        z = logits(row0)
        log_keep = jnp.where(strict, -_softplus(z), 0.0)
        a = jnp.where(strict, jnp.exp(z + log_keep + suffix(log_keep)), 0.0)
        acc = jnp.dot(a.astype(_MXU_DTYPE), v_sc[pl.ds(row0, tq), :], preferred_element_type=_F32)
        run = jnp.sum(log_keep, axis=-1, keepdims=True)

        def body(jj, carry, logits=logits):
            run, acc = carry
            r = pl.multiple_of((i - 1 - jj) * tq, tq)
            z = logits(r)
            log_keep = -_softplus(z)
            a = jnp.exp(z + log_keep + suffix(log_keep) + run)
            acc = acc + jnp.dot(a.astype(_MXU_DTYPE), v_sc[pl.ds(r, tq), :],
                                preferred_element_type=_F32)
            return run + jnp.sum(log_keep, axis=-1, keepdims=True), acc

        _, acc = lax.fori_loop(0, i, body, (run, acc))
        outs.append(acc)
    o = jnp.where(_own_half(0, outs[0].shape), outs[0], outs[1])
    o_ref[0] = (o * _silu(g_ref[0])).astype(o_ref.dtype)


def _sb(proj, *, tq=256):
    b, t, _ = proj.shape
    return pl.pallas_call(
        functools.partial(_sb_kernel, tq=tq),
        grid=(b, 2, t // tq),
        in_specs=[pl.BlockSpec((1, tq, LANES), lambda bb, p, i: (bb, i, COL_SQ + p)),
                  pl.BlockSpec((1, t, LANES), lambda bb, p, i: (bb, 0, COL_SK + p)),
                  pl.BlockSpec((1, t, LANES), lambda bb, p, i: (bb, 0, COL_SV + p)),
                  pl.BlockSpec((1, tq, LANES), lambda bb, p, i: (bb, i, COL_SG + p))],
        out_specs=pl.BlockSpec((1, tq, LANES), lambda bb, p, i: (bb, i, p)),
        out_shape=jax.ShapeDtypeStruct((b, t, GROUP_WIDTH), _MXU_DTYPE),
        scratch_shapes=[pltpu.VMEM((t, LANES), _MXU_DTYPE),
                        pltpu.VMEM((t, LANES), _MXU_DTYPE)],
        compiler_params=pltpu.CompilerParams(
            dimension_semantics=("arbitrary", "arbitrary", "arbitrary"),
            vmem_limit_bytes=_VMEM_LIMIT),
        name="stickbreak",
    )(proj, proj, proj, proj)


def _swa_kernel(sink_ref, q_ref, k_ref, v_ref, g_ref, qkg_ref, bias_ref, o_ref,
                k_sc, v_sc, *, tq):
    kv = pl.program_id(1)
    i = pl.program_id(2)
    t_len = k_ref.shape[1]
    w = SWA_WINDOW

    @pl.when(i == 0)
    def _prep():
        kn = _pair_rms(k_ref[0], qkg_ref[1:2, :])
        v = v_ref[0]
        keep = (_lane_iota(kn.shape) < HEAD_DIM) == (kv == 0)
        k_sc[0:w, :] = jnp.zeros((w, LANES), _MXU_DTYPE)
        v_sc[0:w, :] = jnp.zeros((w, LANES), _MXU_DTYPE)
        k_sc[w:w + t_len, :] = jnp.where(keep, kn, pltpu.roll(kn, HEAD_DIM, axis=1)).astype(_MXU_DTYPE)
        v_sc[w:w + t_len, :] = jnp.where(keep, v, pltpu.roll(v, HEAD_DIM, axis=1)).astype(_MXU_DTYPE)

    q = _pair_rms(q_ref[0], qkg_ref[0:1, :]) * ATTN_SCALE
    qi = lax.broadcasted_iota(jnp.int32, (w, 2 * w), 0)
    kj = lax.broadcasted_iota(jnp.int32, (w, 2 * w), 1)
    dist = qi + w - kj
    in_window = (dist >= 0) & (dist < w)
    for u in range(tq // w):
        nb = i * (tq // w) + u
        r = pl.multiple_of(nb * w, w)
        k_t = k_sc[pl.ds(r, 2 * w), :]
        v_t = v_sc[pl.ds(r, 2 * w), :]
        allowed = in_window & (kj + (nb - 1) * w >= 0)
        q_u = q[u * w:(u + 1) * w]
        outs = []
        for g in range(2):
            q_h = jnp.where(_own_half(g, q_u.shape), q_u, 0.0).astype(_MXU_DTYPE)
            s = lax.dot_general(q_h, k_t, _NT, preferred_element_type=_F32)
            s = jnp.where(allowed, s + bias_ref[g], NEG_INF)
            sink = sink_ref[2 * kv + g]
            m = jnp.maximum(jnp.max(s, axis=-1, keepdims=True), sink)
            e = jnp.exp(s - m)
            den = jnp.sum(e, axis=-1, keepdims=True) + jnp.exp(sink - m)
            outs.append(jnp.dot(e.astype(_MXU_DTYPE), v_t, preferred_element_type=_F32) / den)
        o = jnp.where(_own_half(0, outs[0].shape), outs[0], outs[1])
        o_ref[0, u * w:(u + 1) * w, :] = (o * _silu(g_ref[0, u * w:(u + 1) * w, :])).astype(o_ref.dtype)


def _swa(proj, sinks, qk_gain, bias_tiles, *, tq=512):
    b, t, _ = proj.shape
    w = SWA_WINDOW
    return pl.pallas_call(
        functools.partial(_swa_kernel, tq=tq),
        grid=(b, 2, t // tq),
        in_specs=[pl.BlockSpec(memory_space=pltpu.SMEM),
                  pl.BlockSpec((1, tq, LANES), lambda bb, kv, i: (bb, i, COL_WQ + kv)),
                  pl.BlockSpec((1, t, LANES), lambda bb, kv, i: (bb, 0, COL_WK)),
                  pl.BlockSpec((1, t, LANES), lambda bb, kv, i: (bb, 0, COL_WV)),
                  pl.BlockSpec((1, tq, LANES), lambda bb, kv, i: (bb, i, COL_WG + kv)),
                  pl.BlockSpec((2, LANES), lambda bb, kv, i: (0, 0)),
                  pl.BlockSpec((2, w, 2 * w), lambda bb, kv, i: (kv, 0, 0))],
        out_specs=pl.BlockSpec((1, tq, LANES), lambda bb, kv, i: (bb, i, kv)),
        out_shape=jax.ShapeDtypeStruct((b, t, GROUP_WIDTH), _MXU_DTYPE),
        scratch_shapes=[pltpu.VMEM((t + w, LANES), _MXU_DTYPE),
                        pltpu.VMEM((t + w, LANES), _MXU_DTYPE)],
        compiler_params=pltpu.CompilerParams(
            dimension_semantics=("arbitrary", "arbitrary", "arbitrary"),
            vmem_limit_bytes=_VMEM_LIMIT),
        name="swa",
    )(sinks, proj, proj, proj, proj, qk_gain, bias_tiles)


def _outproj_kernel(x_ref, ya_ref, yb_ref, yc_ref, yd_ref, w_ref, o_ref):
    acc = x_ref[...]
    for g, y_ref in enumerate((ya_ref, yb_ref, yc_ref, yd_ref)):
        acc = acc + jnp.dot(y_ref[...], w_ref[g * GROUP_WIDTH:(g + 1) * GROUP_WIDTH, :],
                            preferred_element_type=_F32)
    o_ref[...] = acc


def _outproj(x2d, ys, w):
    m, d = x2d.shape
    tm = 512
    y_spec = pl.BlockSpec((tm, GROUP_WIDTH), lambda i: (i, 0))
    return pl.pallas_call(
        _outproj_kernel,
        grid=(m // tm,),
        in_specs=[pl.BlockSpec((tm, d), lambda i: (i, 0)), y_spec, y_spec, y_spec, y_spec,
                  pl.BlockSpec(w.shape, lambda i: (0, 0))],
        out_specs=pl.BlockSpec((tm, d), lambda i: (i, 0)),
        out_shape=jax.ShapeDtypeStruct((m, d), _F32),
        compiler_params=pltpu.CompilerParams(
            dimension_semantics=("arbitrary",), vmem_limit_bytes=_VMEM_LIMIT),
        name="outproj",
    )(x2d, *ys, w)


def _rearrange_w_in(w_in):
    off = 3 * GROUP_WIDTH
    main = jnp.concatenate([w_in[..., :off], w_in[..., off + FOX_FORGET_COLS:]], axis=-1)
    ff = w_in[..., off:off + FOX_FORGET_COLS]
    pad = PROJ_WIDTH - main.shape[-1] - FOX_FORGET_COLS
    return jnp.concatenate([main, ff, jnp.zeros(w_in.shape[:-1] + (pad,), w_in.dtype)], axis=-1)


def _pair_gain(g):
    return jnp.tile(g.astype(_F32), (1, 2))


def kernel(x, norm_gain, w_in, b_forget, fox_qk_gain, moba_qk_gain, swa_qk_gain, sinks, w_out, rel_bias):
    b, t, d = x.shape
    depth = w_in.shape[0]
    w_in_r = _rearrange_w_in(w_in).astype(_MXU_DTYPE)
    w_out_c = w_out.astype(_MXU_DTYPE)
    moba_bias, swa_bias = _bias_tiles(rel_bias)
    x2d = x.reshape(b * t, d)
    for layer in range(depth):
        proj = _inproj(x2d, norm_gain[layer][None, :], w_in_r[layer]).reshape(b, t, PROJ_WIDTH)
        bf_row = jnp.pad(b_forget[layer], (0, LANES - FOX_FORGET_COLS))[None, :]
        y_fox = _fox(proj, bf_row, _pair_gain(fox_qk_gain[layer]))
        y_moba = _moba(proj, rel_bias, _pair_gain(moba_qk_gain[layer]), moba_bias)
        y_sb = _sb(proj)
        y_swa = _swa(proj, sinks[layer], _pair_gain(swa_qk_gain[layer]), swa_bias)
        ys = [y.reshape(b * t, GROUP_WIDTH) for y in (y_fox, y_moba, y_sb, y_swa)]
        x2d = _outproj(x2d, ys, w_out_c[layer])
    return x2d.reshape(b, t, d)
```

```python
import functools
import math

import jax
import jax.numpy as jnp
from jax import lax
from jax.experimental import pallas as pl
from jax.experimental.pallas import tpu as pltpu

HEAD_DIM = 64
LANES = 128
GROUP_WIDTH = 256
MOBA_BLOCK = 256
MOBA_TOPK = 3
SWA_WINDOW = 128
NUM_BUCKETS = 32
REL_MAX_DISTANCE = 1024
RMS_EPS = 1e-6
NEG_INF = -1e30
ATTN_SCALE = HEAD_DIM ** -0.5
LOG2E = math.log2(math.e)
BOUNDED_SOFTMAX_RANGE = 60.0
MOBA_NEAR_TILES = REL_MAX_DISTANCE // MOBA_BLOCK + 1
MOBA_WIDE = 2 * MOBA_BLOCK
MOBA_WIDE_NEAR_TILES = REL_MAX_DISTANCE // MOBA_WIDE + 1
FOX_FORGET_COLS = 4
SB_LOG_CUTOFF = -104.0

COL_FQ, COL_FK, COL_FV, COL_FG = 0, 2, 4, 6
COL_MQ, COL_MK, COL_MV, COL_MG = 8, 10, 12, 14
COL_SQ, COL_SK, COL_SV, COL_SG = 16, 18, 20, 22
COL_WQ, COL_WK, COL_WV, COL_WG = 24, 26, 27, 28
COL_FF = 30
PROJ_WIDTH = 32 * LANES

_MXU_DTYPE = jnp.bfloat16
_F32 = jnp.float32
_VMEM_LIMIT = 48 * 1024 * 1024

_NT = (((1,), (1,)), ((), ()))


def _lane_iota(shape):
    return lax.broadcasted_iota(jnp.int32, shape, len(shape) - 1)


def _pair_rms(x, gain_row):
    low = _lane_iota(x.shape) < HEAD_DIM
    sq = x * x
    ms_lo = jnp.sum(jnp.where(low, sq, 0.0), axis=-1, keepdims=True)
    ms_hi = jnp.sum(jnp.where(low, 0.0, sq), axis=-1, keepdims=True)
    ms = jnp.where(low, ms_lo, ms_hi) * (1.0 / HEAD_DIM)
    return x * lax.rsqrt(ms + RMS_EPS) * gain_row


def _silu(g):
    return g * (1.0 / (1.0 + jnp.exp(-g)))


def _softplus(z):
    return jnp.maximum(z, 0.0) + jnp.log1p(jnp.exp(-jnp.abs(z)))


def _split3(x):
    hi = x.astype(_MXU_DTYPE).astype(_F32)
    r = x - hi
    mid = r.astype(_MXU_DTYPE).astype(_F32)
    return hi, mid, r - mid


def _own_half(hh, shape):
    lane = _lane_iota(shape)
    return (lane < HEAD_DIM) if hh == 0 else (lane >= HEAD_DIM)


def _flash_update(s, v_tile, m, l, acc):
    m_new = jnp.maximum(m, jnp.max(s, axis=-1, keepdims=True))
    alpha = jnp.exp(m - m_new)
    p = jnp.exp(s - m_new)
    l = alpha * l + jnp.sum(p, axis=-1, keepdims=True)
    acc = alpha * acc + jnp.dot(p.astype(_MXU_DTYPE), v_tile, preferred_element_type=_F32)
    return m_new, l, acc


def _causal(tq):
    return (lax.broadcasted_iota(jnp.int32, (tq, tq), 0)
            >= lax.broadcasted_iota(jnp.int32, (tq, tq), 1))


def _inproj_kernel(x_ref, gain_ref, w_ref, o_ref, *, tn):
    x = x_ref[...]
    ms = jnp.mean(x * x, axis=-1, keepdims=True)
    hn = (x * lax.rsqrt(ms + RMS_EPS) * gain_ref[...]).astype(_MXU_DTYPE)
    for n in range(o_ref.shape[1] // tn):
        o_ref[:, n * tn:(n + 1) * tn] = jnp.dot(
            hn, w_ref[:, n * tn:(n + 1) * tn], preferred_element_type=_F32)


def _inproj(x2d, gain_row, w):
    m, d = x2d.shape
    n = w.shape[1]
    tm = 256
    return pl.pallas_call(
        functools.partial(_inproj_kernel, tn=1024),
        grid=(m // tm,),
        in_specs=[pl.BlockSpec((tm, d), lambda i: (i, 0)),
                  pl.BlockSpec((1, d), lambda i: (0, 0)),
                  pl.BlockSpec((d, n), lambda i: (0, 0))],
        out_specs=pl.BlockSpec((tm, n), lambda i: (i, 0)),
        out_shape=jax.ShapeDtypeStruct((m, n), _F32),
        compiler_params=pltpu.CompilerParams(
            dimension_semantics=("arbitrary",), vmem_limit_bytes=_VMEM_LIMIT),
        name="inproj",
    )(x2d, gain_row, w)


def _rel_bucket(dist):
    max_exact = NUM_BUCKETS // 2
    d = jnp.maximum(dist, 0)
    log_ratio = (jnp.log(jnp.maximum(d, 1).astype(_F32) / max_exact)
                 / math.log(REL_MAX_DISTANCE / max_exact))
    large = max_exact + (log_ratio * (NUM_BUCKETS - max_exact)).astype(jnp.int32)
    large = jnp.minimum(large, NUM_BUCKETS - 1)
    return jnp.where(d < max_exact, d, large)


def _bias_kernel(rb_ref, bm_ref, bw_ref, bs_ref, om_ref, ow_ref, os_ref, *, n_heads):
    h = pl.program_id(0)

    def lookup(buckets, col):
        acc = jnp.zeros(buckets.shape, _F32)
        for k in range(NUM_BUCKETS):
            acc = jnp.where(buckets == k, rb_ref[k, col], acc)
        return acc

    for d in range(bm_ref.shape[0]):
        om_ref[0, d] = lookup(bm_ref[d], h)
    for d in range(bw_ref.shape[0]):
        ow_ref[0, d] = lookup(bw_ref[d], h) * LOG2E
    os_ref[0] = lookup(bs_ref[...], n_heads + h)


def _bias_tiles(rel_bias):
    n_heads = rel_bias.shape[1] // 2

    def toeplitz(size, count):
        i = jnp.arange(size)[:, None]
        j = jnp.arange(size)[None, :]
        return jnp.stack([_rel_bucket(d * size + i - j) for d in range(count)])

    bm = toeplitz(MOBA_BLOCK, MOBA_NEAR_TILES)
    bw = toeplitz(MOBA_WIDE, MOBA_WIDE_NEAR_TILES)
    w = SWA_WINDOW
    bs = _rel_bucket(jnp.arange(w)[:, None] + w - jnp.arange(2 * w)[None, :])
    return pl.pallas_call(
        functools.partial(_bias_kernel, n_heads=n_heads),
        grid=(n_heads,),
        in_specs=[pl.BlockSpec(memory_space=pltpu.SMEM),
                  pl.BlockSpec(bm.shape, lambda h: (0, 0, 0)),
                  pl.BlockSpec(bw.shape, lambda h: (0, 0, 0)),
                  pl.BlockSpec(bs.shape, lambda h: (0, 0))],
        out_specs=[pl.BlockSpec((1,) + bm.shape, lambda h: (h, 0, 0, 0)),
                   pl.BlockSpec((1,) + bw.shape, lambda h: (h, 0, 0, 0)),
                   pl.BlockSpec((1,) + bs.shape, lambda h: (h, 0, 0))],
        out_shape=[jax.ShapeDtypeStruct((n_heads,) + bm.shape, _F32),
                   jax.ShapeDtypeStruct((n_heads,) + bw.shape, _F32),
                   jax.ShapeDtypeStruct((n_heads,) + bs.shape, _F32)],
        compiler_params=pltpu.CompilerParams(
            dimension_semantics=("arbitrary",), vmem_limit_bytes=_VMEM_LIMIT),
        name="bias_tiles",
    )(rel_bias, bm, bw, bs)


def _forget_cumsum(ff_ref, bf_ref, c_sc):
    t_len = c_sc.shape[0]
    ff = ff_ref[0] + bf_ref[...]
    log_f = jnp.minimum(ff, 0.0) - jnp.log1p(jnp.exp(-jnp.abs(ff)))
    ch = 256
    tri = (lax.broadcasted_iota(jnp.int32, (ch, ch), 0)
           >= lax.broadcasted_iota(jnp.int32, (ch, ch), 1)).astype(_F32)
    carry = jnp.zeros((1, LANES), _F32)
    for r in range(t_len // ch):
        inc = jnp.dot(tri, log_f[r * ch:(r + 1) * ch], precision=lax.Precision.HIGHEST,
                      preferred_element_type=_F32) + carry
        c_sc[r * ch:(r + 1) * ch, :] = inc
        carry = inc[ch - 1:ch, :]


def _lane_column(x, col):
    return jnp.sum(jnp.where(_lane_iota(x.shape) == col, x, 0.0), axis=-1, keepdims=True)


def _lane_fields(shape, a0, fields):
    lane = _lane_iota(shape)
    out = jnp.zeros(shape, _F32)
    for n, f in enumerate(fields):
        out = jnp.where(lane == a0 + n, f, out)
    return out


def _fox_bounded_kernel(bound_ref, q_ref, k_ref, v_ref, ff_ref, g_ref, qkg_ref, bf_ref, o_ref,
                        kaug_sc, vext_sc, c_sc):
    p = pl.program_id(1)
    i = pl.program_id(2)
    tq = q_ref.shape[1]

    @pl.when(i == 0)
    def _prep():
        _forget_cumsum(ff_ref, bf_ref, c_sc)
        kn = _pair_rms(k_ref[0], qkg_ref[1:2, :])
        c_all = c_sc[...] * LOG2E
        v = v_ref[0]
        lane = _lane_iota(v.shape)
        for hh in range(2):
            a0 = HEAD_DIM if hh == 0 else 0
            hi, mid, lo = _split3(-_lane_column(c_all, 2 * p + hh))
            aug = _lane_fields(kn.shape, a0, [hi, mid, lo, 1.0, 1.0, 1.0, 1.0, 1.0])
            own = _own_half(hh, kn.shape)
            kaug_sc[hh] = jnp.where(own, kn, aug).astype(_MXU_DTYPE)
            vext_sc[hh, :, 0:LANES] = jnp.where(own, v, 0.0).astype(_MXU_DTYPE)
            vext_sc[hh, :, LANES:2 * LANES] = jnp.where(lane == hh, 1.0, 0.0).astype(_MXU_DTYPE)

    q = _pair_rms(q_ref[0], qkg_ref[0:1, :]) * (ATTN_SCALE * LOG2E)
    row0 = pl.multiple_of(i * tq, tq)
    c_t = c_sc[pl.ds(row0, tq), :] * LOG2E
    off = jnp.full((1, 1), -LOG2E, _F32) * bound_ref[0]
    off_hi = off.astype(_MXU_DTYPE).astype(_F32)
    q_augs = []
    for hh in range(2):
        a0 = HEAD_DIM if hh == 0 else 0
        hi, mid, lo = _split3(_lane_column(c_t, 2 * p + hh))
        aug = _lane_fields(q.shape, a0, [1.0, 1.0, 1.0, hi, mid, lo, off_hi, off - off_hi])
        q_augs.append(jnp.where(_own_half(hh, q.shape), q, aug).astype(_MXU_DTYPE))

    def step(r, acc, diagonal):
        for hh in range(2):
            s = lax.dot_general(q_augs[hh], kaug_sc[hh, pl.ds(r, tq), :], _NT,
                                preferred_element_type=_F32)
            if diagonal:
                s = jnp.where(_causal(tq), s, NEG_INF)
            acc = acc + jnp.dot(jnp.exp2(s).astype(_MXU_DTYPE), vext_sc[hh, pl.ds(r, tq), :],
                                preferred_element_type=_F32)
        return acc

    acc = step(row0, jnp.zeros((tq, 2 * LANES), _F32), True)
    acc = lax.fori_loop(0, i, lambda j, a: step(pl.multiple_of(j * tq, tq), a, False), acc)
    den = acc[:, LANES:]
    l = jnp.where(_own_half(0, (tq, LANES)), _lane_column(den, 0), _lane_column(den, 1))
    o_ref[0] = (acc[:, :LANES] / l * _silu(g_ref[0])).astype(o_ref.dtype)


def _fox_bounded(proj, bf_row, qk_gain, bound, *, tq=512):
    b, t, _ = proj.shape
    return pl.pallas_call(
        _fox_bounded_kernel,
        grid=(b, 2, t // tq),
        in_specs=[pl.BlockSpec(memory_space=pltpu.SMEM),
                  pl.BlockSpec((1, tq, LANES), lambda bb, p, i: (bb, i, COL_FQ + p)),
                  pl.BlockSpec((1, t, LANES), lambda bb, p, i: (bb, 0, COL_FK + p)),
                  pl.BlockSpec((1, t, LANES), lambda bb, p, i: (bb, 0, COL_FV + p)),
                  pl.BlockSpec((1, t, LANES), lambda bb, p, i: (bb, 0, COL_FF)),
                  pl.BlockSpec((1, tq, LANES), lambda bb, p, i: (bb, i, COL_FG + p)),
                  pl.BlockSpec((2, LANES), lambda bb, p, i: (0, 0)),
                  pl.BlockSpec((1, LANES), lambda bb, p, i: (0, 0))],
        out_specs=pl.BlockSpec((1, tq, LANES), lambda bb, p, i: (bb, i, p)),
        out_shape=jax.ShapeDtypeStruct((b, t, GROUP_WIDTH), _MXU_DTYPE),
        scratch_shapes=[pltpu.VMEM((2, t, LANES), _MXU_DTYPE),
                        pltpu.VMEM((2, t, 2 * LANES), _MXU_DTYPE),
                        pltpu.VMEM((t, LANES), _F32)],
        compiler_params=pltpu.CompilerParams(
            dimension_semantics=("arbitrary", "arbitrary", "arbitrary"),
            vmem_limit_bytes=_VMEM_LIMIT),
        name="fox_bounded",
    )(bound, proj, proj, proj, proj, proj, qk_gain, bf_row)


def _fox_kernel(q_ref, k_ref, v_ref, ff_ref, g_ref, qkg_ref, bf_ref, o_ref,
                kaug_sc, v_sc, c_sc, *, tq):
    p = pl.program_id(1)
    i = pl.program_id(2)

    def head_column(x, hh):
        return _lane_column(x, 2 * p + hh)

    @pl.when(i == 0)
    def _prep():
        _forget_cumsum(ff_ref, bf_ref, c_sc)
        kn = _pair_rms(k_ref[0], qkg_ref[1:2, :])
        c_all = c_sc[...]
        lane = _lane_iota(kn.shape)
        for hh in range(2):
            a0 = HEAD_DIM if hh == 0 else 0
            hi, mid, lo = _split3(-head_column(c_all, hh))
            aug = jnp.where(lane == a0, hi,
                            jnp.where(lane == a0 + 1, mid,
                                      jnp.where(lane == a0 + 2, lo,
                                                jnp.where((lane >= a0 + 3) & (lane < a0 + 6), 1.0, 0.0))))
            kaug_sc[hh] = jnp.where(_own_half(hh, kn.shape), kn, aug).astype(_MXU_DTYPE)
        v_sc[...] = v_ref[0].astype(_MXU_DTYPE)

    q = _pair_rms(q_ref[0], qkg_ref[0:1, :]) * ATTN_SCALE
    row0 = pl.multiple_of(i * tq, tq)
    c_t = c_sc[pl.ds(row0, tq), :]
    lane = _lane_iota(q.shape)
    outs = []
    for hh in range(2):
        a0 = HEAD_DIM if hh == 0 else 0
        hi, mid, lo = _split3(head_column(c_t, hh))
        aug = jnp.where((lane >= a0) & (lane < a0 + 3), 1.0,
                        jnp.where(lane == a0 + 3, hi,
                                  jnp.where(lane == a0 + 4, mid,
                                            jnp.where(lane == a0 + 5, lo, 0.0))))
        q_aug = jnp.where(_own_half(hh, q.shape), q, aug).astype(_MXU_DTYPE)

        s = lax.dot_general(q_aug, kaug_sc[hh, pl.ds(row0, tq), :], _NT,
                            preferred_element_type=_F32)
        s = jnp.where(_causal(tq), s, NEG_INF)
        m = jnp.max(s, axis=-1, keepdims=True)
        pr = jnp.exp(s - m)
        l = jnp.sum(pr, axis=-1, keepdims=True)
        acc = jnp.dot(pr.astype(_MXU_DTYPE), v_sc[pl.ds(row0, tq), :], preferred_element_type=_F32)

        def body(j, carry, q_aug=q_aug, hh=hh):
            r = pl.multiple_of(j * tq, tq)
            s = lax.dot_general(q_aug, kaug_sc[hh, pl.ds(r, tq), :], _NT,
                                preferred_element_type=_F32)
            return _flash_update(s, v_sc[pl.ds(r, tq), :], *carry)

        m, l, acc = lax.fori_loop(0, i, body, (m, l, acc))
        outs.append(acc / l)
    o = jnp.where(_own_half(0, outs[0].shape), outs[0], outs[1])
    o_ref[0] = (o * _silu(g_ref[0])).astype(o_ref.dtype)


def _fox(proj, bf_row, qk_gain, *, tq=256):
    b, t, _ = proj.shape
    return pl.pallas_call(
        functools.partial(_fox_kernel, tq=tq),
        grid=(b, 2, t // tq),
        in_specs=[pl.BlockSpec((1, tq, LANES), lambda bb, p, i: (bb, i, COL_FQ + p)),
                  pl.BlockSpec((1, t, LANES), lambda bb, p, i: (bb, 0, COL_FK + p)),
                  pl.BlockSpec((1, t, LANES), lambda bb, p, i: (bb, 0, COL_FV + p)),
                  pl.BlockSpec((1, t, LANES), lambda bb, p, i: (bb, 0, COL_FF)),
                  pl.BlockSpec((1, tq, LANES), lambda bb, p, i: (bb, i, COL_FG + p)),
                  pl.BlockSpec((2, LANES), lambda bb, p, i: (0, 0)),
                  pl.BlockSpec((1, LANES), lambda bb, p, i: (0, 0))],
        out_specs=pl.BlockSpec((1, tq, LANES), lambda bb, p, i: (bb, i, p)),
        out_shape=jax.ShapeDtypeStruct((b, t, GROUP_WIDTH), _MXU_DTYPE),
        scratch_shapes=[pltpu.VMEM((2, t, LANES), _MXU_DTYPE),
                        pltpu.VMEM((t, LANES), _MXU_DTYPE),
                        pltpu.VMEM((t, LANES), _F32)],
        compiler_params=pltpu.CompilerParams(
            dimension_semantics=("arbitrary", "arbitrary", "arbitrary"),
            vmem_limit_bytes=_VMEM_LIMIT),
        name="fox",
    )(proj, proj, proj, proj, proj, qk_gain, bf_row)


def _moba_block_means(kn, kmean_sc):
    kmean_sc[...] = jnp.zeros(kmean_sc.shape, _F32)
    for n in range(kn.shape[0] // MOBA_BLOCK):
        kmean_sc[n:n + 1, :] = jnp.mean(kn[n * MOBA_BLOCK:(n + 1) * MOBA_BLOCK], axis=0, keepdims=True)


def _moba_select(q_head, kmean, past):
    gate = lax.dot_general(q_head, kmean, _NT, precision=lax.Precision.HIGHEST,
                           preferred_element_type=_F32)
    lane_f = _lane_iota(gate.shape).astype(_F32)
    cand = jnp.where(past, gate, -jnp.inf)
    sel = jnp.zeros(gate.shape, _F32)
    for _ in range(MOBA_TOPK):
        mx = jnp.max(cand, axis=-1, keepdims=True)
        is_max = (cand == mx) & (mx > -jnp.inf)
        first = jnp.min(jnp.where(is_max, lane_f, float(LANES)), axis=-1, keepdims=True)
        pick = lane_f == first
        sel = jnp.where(pick, 1.0, sel)
        cand = jnp.where(pick, -jnp.inf, cand)
    return sel


def _moba_bounded_kernel(rb_ref, bound_ref, q_ref, k_ref, v_ref, g_ref, qkg_ref, bias_ref, o_ref,
                         kaug_sc, vext_sc, kmean_sc):
    p = pl.program_id(1)
    i = pl.program_id(2)
    tq = q_ref.shape[1]
    blk = MOBA_BLOCK
    near = MOBA_WIDE_NEAR_TILES

    @pl.when(i == 0)
    def _prep():
        kn = _pair_rms(k_ref[0], qkg_ref[1:2, :])
        _moba_block_means(kn, kmean_sc)
        v = v_ref[0]
        lane = _lane_iota(kn.shape)
        row_blk = lax.broadcasted_iota(jnp.int32, kn.shape, 0) // blk
        for hh in range(2):
            a0 = HEAD_DIM if hh == 0 else 0
            aug = jnp.where((lane - a0 == row_blk) | (lane - a0 - 16 == row_blk)
                            | (lane - a0 == 32) | (lane - a0 == 33), 1.0, 0.0)
            own = _own_half(hh, kn.shape)
            kaug_sc[hh] = jnp.where(own, kn, aug).astype(_MXU_DTYPE)
            vext_sc[hh, :, 0:LANES] = jnp.where(own, v, 0.0).astype(_MXU_DTYPE)
            vext_sc[hh, :, LANES:2 * LANES] = jnp.where(lane == hh, 1.0, 0.0).astype(_MXU_DTYPE)

    qn = _pair_rms(q_ref[0], qkg_ref[0:1, :])
    row0 = pl.multiple_of(i * tq, tq)
    lane = _lane_iota(qn.shape)
    q_blk = i * (tq // blk) + lax.broadcasted_iota(jnp.int32, qn.shape, 0) // blk
    past = lane < q_blk
    off = jnp.full((1, 1), -LOG2E, _F32) * bound_ref[0]
    off_hi = off.astype(_MXU_DTYPE).astype(_F32)
    q_augs = []
    for hh in range(2):
        own = _own_half(hh, qn.shape)
        sel = _moba_select(jnp.where(own, qn, 0.0), kmean_sc[...], past)
        c_far = jnp.full((1, 1), LOG2E, _F32) * rb_ref[NUM_BUCKETS - 1, 2 * p + hh]
        far_hi = c_far.astype(_MXU_DTYPE).astype(_F32)
        far_lo = c_far - far_hi
        pen = jnp.where((lane == q_blk) | (past & (sel != 0.0)), 0.0, NEG_INF)
        aug = jnp.where(lane < 16, pen + jnp.where(i - lane // 2 >= near, far_hi, 0.0),
                        jnp.where(lane < 32, jnp.where(i - (lane - 16) // 2 >= near, far_lo, 0.0),
                                  jnp.where(lane == 32, off_hi,
                                            jnp.where(lane == 33, off - off_hi, 0.0))))
        if hh == 0:
            aug = pltpu.roll(aug, HEAD_DIM, axis=1)
        q_augs.append(jnp.where(own, qn * (ATTN_SCALE * LOG2E), aug).astype(_MXU_DTYPE))

    def step(r, acc, bias_idx, diagonal):
        for hh in range(2):
            s = lax.dot_general(q_augs[hh], kaug_sc[hh, pl.ds(r, tq), :], _NT,
                                preferred_element_type=_F32)
            if bias_idx is not None:
                s = s + bias_ref[hh, bias_idx]
            if diagonal:
                s = jnp.where(_causal(tq), s, NEG_INF)
            acc = acc + jnp.dot(jnp.exp2(s).astype(_MXU_DTYPE), vext_sc[hh, pl.ds(r, tq), :],
                                preferred_element_type=_F32)
        return acc

    acc = step(row0, jnp.zeros((tq, 2 * LANES), _F32), 0, True)
    near_lo = jnp.maximum(i - (near - 1), 0)
    acc = lax.fori_loop(near_lo, i,
                        lambda j, a: step(pl.multiple_of(j * tq, tq), a, i - j, False), acc)
    acc = lax.fori_loop(0, near_lo,
                        lambda j, a: step(pl.multiple_of(j * tq, tq), a, None, False), acc)
    den = acc[:, LANES:]
    l = jnp.where(_own_half(0, (tq, LANES)), _lane_column(den, 0), _lane_column(den, 1))
    o_ref[0] = (acc[:, :LANES] / l * _silu(g_ref[0])).astype(o_ref.dtype)


def _moba_bounded(proj, rel_bias, qk_gain, bias_tiles, bound):
    b, t, _ = proj.shape
    tq = MOBA_WIDE
    assert t % tq == 0 and t // MOBA_BLOCK <= 16
    return pl.pallas_call(
        _moba_bounded_kernel,
        grid=(b, 2, t // tq),
        in_specs=[pl.BlockSpec(memory_space=pltpu.SMEM),
                  pl.BlockSpec(memory_space=pltpu.SMEM),
                  pl.BlockSpec((1, tq, LANES), lambda bb, p, i: (bb, i, COL_MQ + p)),
                  pl.BlockSpec((1, t, LANES), lambda bb, p, i: (bb, 0, COL_MK + p)),
                  pl.BlockSpec((1, t, LANES), lambda bb, p, i: (bb, 0, COL_MV + p)),
                  pl.BlockSpec((1, tq, LANES), lambda bb, p, i: (bb, i, COL_MG + p)),
                  pl.BlockSpec((2, LANES), lambda bb, p, i: (0, 0)),
                  pl.BlockSpec((2, MOBA_WIDE_NEAR_TILES, tq, tq), lambda bb, p, i: (p, 0, 0, 0))],
        out_specs=pl.BlockSpec((1, tq, LANES), lambda bb, p, i: (bb, i, p)),
        out_shape=jax.ShapeDtypeStruct((b, t, GROUP_WIDTH), _MXU_DTYPE),
        scratch_shapes=[pltpu.VMEM((2, t, LANES), _MXU_DTYPE),
                        pltpu.VMEM((2, t, 2 * LANES), _MXU_DTYPE),
                        pltpu.VMEM((LANES, LANES), _F32)],
        compiler_params=pltpu.CompilerParams(
            dimension_semantics=("arbitrary", "arbitrary", "arbitrary"),
            vmem_limit_bytes=_VMEM_LIMIT),
        name="moba_bounded",
    )(rel_bias, bound, proj, proj, proj, proj, qk_gain, bias_tiles)


def _moba_kernel(rb_ref, q_ref, k_ref, v_ref, g_ref, qkg_ref, bias_ref, o_ref,
                 kaug_sc, v_sc, kmean_sc):
    p = pl.program_id(1)
    i = pl.program_id(2)
    t_len = k_ref.shape[1]
    blk = MOBA_BLOCK
    nblk = t_len // blk
    near = MOBA_NEAR_TILES

    @pl.when(i == 0)
    def _prep():
        kn = _pair_rms(k_ref[0], qkg_ref[1:2, :])
        kmean_sc[...] = jnp.zeros(kmean_sc.shape, _F32)
        for n in range(nblk):
            kmean_sc[n:n + 1, :] = jnp.mean(kn[n * blk:(n + 1) * blk], axis=0, keepdims=True)
        lane = _lane_iota(kn.shape)
        row_blk = lax.broadcasted_iota(jnp.int32, kn.shape, 0) // blk
        for hh in range(2):
            a0 = HEAD_DIM if hh == 0 else 0
            onehot = jnp.where((lane - a0 == row_blk) | (lane - a0 - 16 == row_blk), 1.0, 0.0)
            kaug_sc[hh] = jnp.where(_own_half(hh, kn.shape), kn, onehot).astype(_MXU_DTYPE)
        v_sc[...] = v_ref[0].astype(_MXU_DTYPE)

    qn = _pair_rms(q_ref[0], qkg_ref[0:1, :])
    row0 = pl.multiple_of(i * blk, blk)
    lane = _lane_iota(qn.shape)
    lane_f = lane.astype(_F32)
    past = lane < i
    outs = []
    for hh in range(2):
        own = _own_half(hh, qn.shape)
        gate = lax.dot_general(jnp.where(own, qn, 0.0), kmean_sc[...], _NT,
                               precision=lax.Precision.HIGHEST, preferred_element_type=_F32)
        cand = jnp.where(past, gate, -jnp.inf)
        sel = jnp.zeros(qn.shape, _F32)
        for _ in range(MOBA_TOPK):
            mx = jnp.max(cand, axis=-1, keepdims=True)
            is_max = (cand == mx) & (mx > -jnp.inf)
            first = jnp.min(jnp.where(is_max, lane_f, float(LANES)), axis=-1, keepdims=True)
            pick = lane_f == first
            sel = jnp.where(pick, 1.0, sel)
            cand = jnp.where(pick, -jnp.inf, cand)
        c_far = jnp.full((1, LANES), rb_ref[NUM_BUCKETS - 1, 2 * p + hh], _F32)
        far_hi = c_far.astype(_MXU_DTYPE).astype(_F32)
        far_lo = c_far - far_hi
        pen = jnp.where(past & (sel == 0.0), NEG_INF, 0.0)
        aug = jnp.where(lane < 16, pen + jnp.where(i - lane >= near, far_hi, 0.0),
                        jnp.where((lane < 32) & (i - (lane - 16) >= near), far_lo, 0.0))
        if hh == 0:
            aug = pltpu.roll(aug, HEAD_DIM, axis=1)
        q_aug = jnp.where(own, qn * ATTN_SCALE, aug).astype(_MXU_DTYPE)

        def scores(r, q_aug=q_aug, hh=hh):
            return lax.dot_general(q_aug, kaug_sc[hh, pl.ds(r, blk), :], _NT,
                                   preferred_element_type=_F32)

        s = scores(row0) + bias_ref[hh, 0]
        s = jnp.where(_causal(blk), s, NEG_INF)
        m = jnp.max(s, axis=-1, keepdims=True)
        pr = jnp.exp(s - m)
        l = jnp.sum(pr, axis=-1, keepdims=True)
        acc = jnp.dot(pr.astype(_MXU_DTYPE), v_sc[pl.ds(row0, blk), :], preferred_element_type=_F32)

        def near_body(j, carry, hh=hh, scores=scores):
            r = pl.multiple_of(j * blk, blk)
            return _flash_update(scores(r) + bias_ref[hh, i - j], v_sc[pl.ds(r, blk), :], *carry)

        def far_body(j, carry, scores=scores):
            r = pl.multiple_of(j * blk, blk)
            return _flash_update(scores(r), v_sc[pl.ds(r, blk), :], *carry)

        near_lo = jnp.maximum(i - (near - 1), 0)
        carry = lax.fori_loop(near_lo, i, near_body, (m, l, acc))
        m, l, acc = lax.fori_loop(0, near_lo, far_body, carry)
        outs.append(acc / l)
    o = jnp.where(_own_half(0, outs[0].shape), outs[0], outs[1])
    o_ref[0] = (o * _silu(g_ref[0])).astype(o_ref.dtype)


def _moba(proj, rel_bias, qk_gain, bias_tiles):
    b, t, _ = proj.shape
    blk = MOBA_BLOCK
    assert t % blk == 0 and t // blk <= 16
    return pl.pallas_call(
        _moba_kernel,
        grid=(b, 2, t // blk),
        in_specs=[pl.BlockSpec(memory_space=pltpu.SMEM),
                  pl.BlockSpec((1, blk, LANES), lambda bb, p, i: (bb, i, COL_MQ + p)),
                  pl.BlockSpec((1, t, LANES), lambda bb, p, i: (bb, 0, COL_MK + p)),
                  pl.BlockSpec((1, t, LANES), lambda bb, p, i: (bb, 0, COL_MV + p)),
                  pl.BlockSpec((1, blk, LANES), lambda bb, p, i: (bb, i, COL_MG + p)),
                  pl.BlockSpec((2, LANES), lambda bb, p, i: (0, 0)),
                  pl.BlockSpec((2, MOBA_NEAR_TILES, blk, blk), lambda bb, p, i: (p, 0, 0, 0))],
        out_specs=pl.BlockSpec((1, blk, LANES), lambda bb, p, i: (bb, i, p)),
        out_shape=jax.ShapeDtypeStruct((b, t, GROUP_WIDTH), _MXU_DTYPE),
        scratch_shapes=[pltpu.VMEM((2, t, LANES), _MXU_DTYPE),
                        pltpu.VMEM((t, LANES), _MXU_DTYPE),
                        pltpu.VMEM((LANES, LANES), _F32)],
        compiler_params=pltpu.CompilerParams(
            dimension_semantics=("arbitrary", "arbitrary", "arbitrary"),
            vmem_limit_bytes=_VMEM_LIMIT),
        name="moba",
    )(rel_bias, proj, proj, proj, proj, qk_gain, bias_tiles)


def _sb_kernel(q_ref, k_ref, v_ref, g_ref, o_ref, k_sc, v_sc, *, tq):
    i = pl.program_id(2)

    @pl.when(i == 0)
    def _prep():
        k_sc[...] = k_ref[0].astype(_MXU_DTYPE)
        v = v_ref[0]
        for hh in range(2):
            v_sc[hh] = jnp.where(_own_half(hh, v.shape), v, 0.0).astype(_MXU_DTYPE)

    q = q_ref[0] * ATTN_SCALE
    q_heads = [jnp.where(_own_half(hh, q.shape), q, 0.0).astype(_MXU_DTYPE) for hh in range(2)]
    row0 = pl.multiple_of(i * tq, tq)
    strict = (lax.broadcasted_iota(jnp.int32, (tq, tq), 0)
              > lax.broadcasted_iota(jnp.int32, (tq, tq), 1))
    after = strict.astype(_MXU_DTYPE)

    def suffix(log_keep):
        hi = log_keep.astype(_MXU_DTYPE)
        lo = (log_keep - hi.astype(_F32)).astype(_MXU_DTYPE)
        return (jnp.dot(hi, after, preferred_element_type=_F32)
                + jnp.dot(lo, after, preferred_element_type=_F32))

    def tile(r, runs, diagonal):
        k_t = k_sc[pl.ds(r, tq), :]
        new_runs, out = [], None
        for hh in range(2):
            z = lax.dot_general(q_heads[hh], k_t, _NT, preferred_element_type=_F32)
            log_keep = -_softplus(z)
            if diagonal:
                log_keep = jnp.where(strict, log_keep, 0.0)
            a = jnp.exp(z + log_keep + suffix(log_keep) + runs[hh])
            if diagonal:
                a = jnp.where(strict, a, 0.0)
            pv = jnp.dot(a.astype(_MXU_DTYPE), v_sc[hh, pl.ds(r, tq), :], preferred_element_type=_F32)
            out = pv if out is None else out + pv
            new_runs.append(runs[hh] + jnp.sum(log_keep, axis=-1, keepdims=True))
        return new_runs, out

    def alive(runs):
        return jnp.max(jnp.maximum(runs[0], runs[1])) >= SB_LOG_CUTOFF

    zero = jnp.zeros((tq, 1), _F32)
    runs, acc = tile(row0, [zero, zero], True)

    def cond(state):
        return (state[0] < i) & state[1]

    def body(state):
        jj, _, run0, run1, acc = state
        r = pl.multiple_of((i - 1 - jj) * tq, tq)
        runs, out = tile(r, [run0, run1], False)
        return jj + 1, alive(runs), runs[0], runs[1], acc + out

    state = lax.while_loop(cond, body, (jnp.int32(0), alive(runs), runs[0], runs[1], acc))
    o_ref[0] = (state[4] * _silu(g_ref[0])).astype(o_ref.dtype)


def _sb(proj, *, tq=256):
    b, t, _ = proj.shape
    return pl.pallas_call(
        functools.partial(_sb_kernel, tq=tq),
        grid=(b, 2, t // tq),
        in_specs=[pl.BlockSpec((1, tq, LANES), lambda bb, p, i: (bb, i, COL_SQ + p)),
                  pl.BlockSpec((1, t, LANES), lambda bb, p, i: (bb, 0, COL_SK + p)),
                  pl.BlockSpec((1, t, LANES), lambda bb, p, i: (bb, 0, COL_SV + p)),
                  pl.BlockSpec((1, tq, LANES), lambda bb, p, i: (bb, i, COL_SG + p))],
        out_specs=pl.BlockSpec((1, tq, LANES), lambda bb, p, i: (bb, i, p)),
        out_shape=jax.ShapeDtypeStruct((b, t, GROUP_WIDTH), _MXU_DTYPE),
        scratch_shapes=[pltpu.VMEM((t, LANES), _MXU_DTYPE),
                        pltpu.VMEM((2, t, LANES), _MXU_DTYPE)],
        compiler_params=pltpu.CompilerParams(
            dimension_semantics=("arbitrary", "arbitrary", "arbitrary"),
            vmem_limit_bytes=_VMEM_LIMIT),
        name="stickbreak",
    )(proj, proj, proj, proj)


def _swa_kernel(sink_ref, q_ref, k_ref, v_ref, g_ref, qkg_ref, bias_ref, o_ref,
                k_sc, v_sc, *, tq):
    kv = pl.program_id(1)
    i = pl.program_id(2)
    t_len = k_ref.shape[1]
    w = SWA_WINDOW

    @pl.when(i == 0)
    def _prep():
        kn = _pair_rms(k_ref[0], qkg_ref[1:2, :])
        v = v_ref[0]
        keep = (_lane_iota(kn.shape) < HEAD_DIM) == (kv == 0)
        k_sc[0:w, :] = jnp.zeros((w, LANES), _MXU_DTYPE)
        v_sc[0:w, :] = jnp.zeros((w, LANES), _MXU_DTYPE)
        k_sc[w:w + t_len, :] = jnp.where(keep, kn, pltpu.roll(kn, HEAD_DIM, axis=1)).astype(_MXU_DTYPE)
        v_sc[w:w + t_len, :] = jnp.where(keep, v, pltpu.roll(v, HEAD_DIM, axis=1)).astype(_MXU_DTYPE)

    q = _pair_rms(q_ref[0], qkg_ref[0:1, :]) * ATTN_SCALE
    qi = lax.broadcasted_iota(jnp.int32, (w, 2 * w), 0)
    kj = lax.broadcasted_iota(jnp.int32, (w, 2 * w), 1)
    dist = qi + w - kj
    in_window = (dist >= 0) & (dist < w)
    for u in range(tq // w):
        nb = i * (tq // w) + u
        r = pl.multiple_of(nb * w, w)
        k_t = k_sc[pl.ds(r, 2 * w), :]
        v_t = v_sc[pl.ds(r, 2 * w), :]
        allowed = in_window & (kj + (nb - 1) * w >= 0)
        q_u = q[u * w:(u + 1) * w]
        outs = []
        for g in range(2):
            q_h = jnp.where(_own_half(g, q_u.shape), q_u, 0.0).astype(_MXU_DTYPE)
            s = lax.dot_general(q_h, k_t, _NT, preferred_element_type=_F32)
            s = jnp.where(allowed, s + bias_ref[g], NEG_INF)
            sink = sink_ref[2 * kv + g]
            m = jnp.maximum(jnp.max(s, axis=-1, keepdims=True), sink)
            e = jnp.exp(s - m)
            den = jnp.sum(e, axis=-1, keepdims=True) + jnp.exp(sink - m)
            outs.append(jnp.dot(e.astype(_MXU_DTYPE), v_t, preferred_element_type=_F32) / den)
        o = jnp.where(_own_half(0, outs[0].shape), outs[0], outs[1])
        o_ref[0, u * w:(u + 1) * w, :] = (o * _silu(g_ref[0, u * w:(u + 1) * w, :])).astype(o_ref.dtype)


def _swa(proj, sinks, qk_gain, bias_tiles, *, tq=512):
    b, t, _ = proj.shape
    w = SWA_WINDOW
    return pl.pallas_call(
        functools.partial(_swa_kernel, tq=tq),
        grid=(b, 2, t // tq),
        in_specs=[pl.BlockSpec(memory_space=pltpu.SMEM),
                  pl.BlockSpec((1, tq, LANES), lambda bb, kv, i: (bb, i, COL_WQ + kv)),
                  pl.BlockSpec((1, t, LANES), lambda bb, kv, i: (bb, 0, COL_WK)),
                  pl.BlockSpec((1, t, LANES), lambda bb, kv, i: (bb, 0, COL_WV)),
                  pl.BlockSpec((1, tq, LANES), lambda bb, kv, i: (bb, i, COL_WG + kv)),
                  pl.BlockSpec((2, LANES), lambda bb, kv, i: (0, 0)),
                  pl.BlockSpec((2, w, 2 * w), lambda bb, kv, i: (kv, 0, 0))],
        out_specs=pl.BlockSpec((1, tq, LANES), lambda bb, kv, i: (bb, i, kv)),
        out_shape=jax.ShapeDtypeStruct((b, t, GROUP_WIDTH), _MXU_DTYPE),
        scratch_shapes=[pltpu.VMEM((t + w, LANES), _MXU_DTYPE),
                        pltpu.VMEM((t + w, LANES), _MXU_DTYPE)],
        compiler_params=pltpu.CompilerParams(
            dimension_semantics=("arbitrary", "arbitrary", "arbitrary"),
            vmem_limit_bytes=_VMEM_LIMIT),
        name="swa",
    )(sinks, proj, proj, proj, proj, qk_gain, bias_tiles)


def _outproj_kernel(x_ref, ya_ref, yb_ref, yc_ref, yd_ref, w_ref, o_ref):
    acc = x_ref[...]
    for g, y_ref in enumerate((ya_ref, yb_ref, yc_ref, yd_ref)):
        acc = acc + jnp.dot(y_ref[...], w_ref[g * GROUP_WIDTH:(g + 1) * GROUP_WIDTH, :],
                            preferred_element_type=_F32)
    o_ref[...] = acc


def _outproj(x2d, ys, w):
    m, d = x2d.shape
    tm = 512
    y_spec = pl.BlockSpec((tm, GROUP_WIDTH), lambda i: (i, 0))
    return pl.pallas_call(
        _outproj_kernel,
        grid=(m // tm,),
        in_specs=[pl.BlockSpec((tm, d), lambda i: (i, 0)), y_spec, y_spec, y_spec, y_spec,
                  pl.BlockSpec(w.shape, lambda i: (0, 0))],
        out_specs=pl.BlockSpec((tm, d), lambda i: (i, 0)),
        out_shape=jax.ShapeDtypeStruct((m, d), _F32),
        compiler_params=pltpu.CompilerParams(
            dimension_semantics=("arbitrary",), vmem_limit_bytes=_VMEM_LIMIT),
        name="outproj",
    )(x2d, *ys, w)


def _rearrange_w_in(w_in):
    off = 3 * GROUP_WIDTH
    main = jnp.concatenate([w_in[..., :off], w_in[..., off + FOX_FORGET_COLS:]], axis=-1)
    ff = w_in[..., off:off + FOX_FORGET_COLS]
    pad = PROJ_WIDTH - main.shape[-1] - FOX_FORGET_COLS
    return jnp.concatenate([main, ff, jnp.zeros(w_in.shape[:-1] + (pad,), w_in.dtype)], axis=-1)


def _qk_logit_bound(qk_gain):
    g = jnp.abs(qk_gain.astype(_F32))
    return ATTN_SCALE * HEAD_DIM * jnp.max(g[0]) * jnp.max(g[1])


def _pair_gain(g):
    return jnp.tile(g.astype(_F32), (1, 2))


def kernel(x, norm_gain, w_in, b_forget, fox_qk_gain, moba_qk_gain, swa_qk_gain, sinks, w_out, rel_bias):
    b, t, d = x.shape
    depth = w_in.shape[0]
    w_in_r = _rearrange_w_in(w_in).astype(_MXU_DTYPE)
    w_out_c = w_out.astype(_MXU_DTYPE)
    moba_bias, moba_wide_bias, swa_bias = _bias_tiles(rel_bias)
    moba_bias_max = jnp.max(jnp.abs(rel_bias[:, :rel_bias.shape[1] // 2].astype(_F32)))
    x2d = x.reshape(b * t, d)
    for layer in range(depth):
        proj = _inproj(x2d, norm_gain[layer][None, :], w_in_r[layer]).reshape(b, t, PROJ_WIDTH)
        bf_row = jnp.pad(b_forget[layer], (0, LANES - FOX_FORGET_COLS))[None, :]
        fox_gain = _pair_gain(fox_qk_gain[layer])
        fox_bound = _qk_logit_bound(fox_qk_gain[layer])
        y_fox = lax.cond(2.0 * fox_bound <= BOUNDED_SOFTMAX_RANGE,
                         lambda: _fox_bounded(proj, bf_row, fox_gain, fox_bound.reshape(1)),
                         lambda: _fox(proj, bf_row, fox_gain))
        moba_gain = _pair_gain(moba_qk_gain[layer])
        moba_bound = _qk_logit_bound(moba_qk_gain[layer]) + moba_bias_max
        y_moba = lax.cond(2.0 * moba_bound <= BOUNDED_SOFTMAX_RANGE,
                          lambda: _moba_bounded(proj, rel_bias, moba_gain, moba_wide_bias,
                                                moba_bound.reshape(1)),
                          lambda: _moba(proj, rel_bias, moba_gain, moba_bias))
        y_sb = _sb(proj)
        y_swa = _swa(proj, sinks[layer], _pair_gain(swa_qk_gain[layer]), swa_bias)
        ys = [y.reshape(b * t, GROUP_WIDTH) for y in (y_fox, y_moba, y_sb, y_swa)]
        x2d = _outproj(x2d, ys, w_out_c[layer])
    return x2d.reshape(b, t, d)
```

```python
import functools
import math

import jax
import jax.numpy as jnp
from jax import lax
from jax.experimental import pallas as pl
from jax.experimental.pallas import tpu as pltpu

HEAD_DIM = 64
LANES = 128
GROUP_WIDTH = 256
MOBA_BLOCK = 256
MOBA_TOPK = 3
SWA_WINDOW = 128
NUM_BUCKETS = 32
REL_MAX_DISTANCE = 1024
RMS_EPS = 1e-6
NEG_INF = -1e30
ATTN_SCALE = HEAD_DIM ** -0.5
LOG2E = math.log2(math.e)
BOUNDED_SOFTMAX_RANGE = 60.0
MOBA_NEAR_TILES = REL_MAX_DISTANCE // MOBA_BLOCK + 1
MOBA_WIDE = 2 * MOBA_BLOCK
MOBA_WIDE_NEAR_TILES = REL_MAX_DISTANCE // MOBA_WIDE + 1
FOX_FORGET_COLS = 4
EXP_ZERO_CUTOFF = -104.0

COL_FQ, COL_FK, COL_FV, COL_FG = 0, 2, 4, 6
COL_MQ, COL_MK, COL_MV, COL_MG = 8, 10, 12, 14
COL_SQ, COL_SK, COL_SV, COL_SG = 16, 18, 20, 22
COL_WQ, COL_WK, COL_WV, COL_WG = 24, 26, 27, 28
COL_FF = 30
PROJ_WIDTH = 32 * LANES

_MXU_DTYPE = jnp.bfloat16
_F32 = jnp.float32
_VMEM_LIMIT = 48 * 1024 * 1024

_NT = (((1,), (1,)), ((), ()))


def _lane_iota(shape):
    return lax.broadcasted_iota(jnp.int32, shape, len(shape) - 1)


def _pair_rms(x, gain_row):
    low = _lane_iota(x.shape) < HEAD_DIM
    sq = x * x
    ms_lo = jnp.sum(jnp.where(low, sq, 0.0), axis=-1, keepdims=True)
    ms_hi = jnp.sum(jnp.where(low, 0.0, sq), axis=-1, keepdims=True)
    ms = jnp.where(low, ms_lo, ms_hi) * (1.0 / HEAD_DIM)
    return x * lax.rsqrt(ms + RMS_EPS) * gain_row


def _silu(g):
    return g * (1.0 / (1.0 + jnp.exp(-g)))


def _split3(x):
    hi = x.astype(_MXU_DTYPE).astype(_F32)
    r = x - hi
    mid = r.astype(_MXU_DTYPE).astype(_F32)
    return hi, mid, r - mid


def _own_half(hh, shape):
    lane = _lane_iota(shape)
    return (lane < HEAD_DIM) if hh == 0 else (lane >= HEAD_DIM)


def _flash_update(s, v_tile, m, l, acc):
    m_new = jnp.maximum(m, jnp.max(s, axis=-1, keepdims=True))
    alpha = jnp.exp(m - m_new)
    p = jnp.exp(s - m_new)
    l = alpha * l + jnp.sum(p, axis=-1, keepdims=True)
    acc = alpha * acc + jnp.dot(p.astype(_MXU_DTYPE), v_tile, preferred_element_type=_F32)
    return m_new, l, acc


def _causal(tq):
    return (lax.broadcasted_iota(jnp.int32, (tq, tq), 0)
            >= lax.broadcasted_iota(jnp.int32, (tq, tq), 1))


def _inproj_kernel(x_ref, gain_ref, w_ref, o_ref, *, tn):
    x = x_ref[...]
    ms = jnp.mean(x * x, axis=-1, keepdims=True)
    hn = (x * lax.rsqrt(ms + RMS_EPS) * gain_ref[...]).astype(_MXU_DTYPE)
    for n in range(o_ref.shape[1] // tn):
        o_ref[:, n * tn:(n + 1) * tn] = jnp.dot(
            hn, w_ref[:, n * tn:(n + 1) * tn], preferred_element_type=_F32)


def _inproj(x2d, gain_row, w):
    m, d = x2d.shape
    n = w.shape[1]
    tm = 256
    return pl.pallas_call(
        functools.partial(_inproj_kernel, tn=1024),
        grid=(m // tm,),
        in_specs=[pl.BlockSpec((tm, d), lambda i: (i, 0)),
                  pl.BlockSpec((1, d), lambda i: (0, 0)),
                  pl.BlockSpec((d, n), lambda i: (0, 0))],
        out_specs=pl.BlockSpec((tm, n), lambda i: (i, 0)),
        out_shape=jax.ShapeDtypeStruct((m, n), _F32),
        compiler_params=pltpu.CompilerParams(
            dimension_semantics=("arbitrary",), vmem_limit_bytes=_VMEM_LIMIT),
        name="inproj",
    )(x2d, gain_row, w)


def _rel_bucket(dist):
    max_exact = NUM_BUCKETS // 2
    d = jnp.maximum(dist, 0)
    log_ratio = (jnp.log(jnp.maximum(d, 1).astype(_F32) / max_exact)
                 / math.log(REL_MAX_DISTANCE / max_exact))
    large = max_exact + (log_ratio * (NUM_BUCKETS - max_exact)).astype(jnp.int32)
    large = jnp.minimum(large, NUM_BUCKETS - 1)
    return jnp.where(d < max_exact, d, large)


def _bias_kernel(rb_ref, bm_ref, bw_ref, bs_ref, om_ref, ow_ref, os_ref, *, n_heads):
    h = pl.program_id(0)

    def lookup(buckets, col):
        acc = jnp.zeros(buckets.shape, _F32)
        for k in range(NUM_BUCKETS):
            acc = jnp.where(buckets == k, rb_ref[k, col], acc)
        return acc

    for d in range(bm_ref.shape[0]):
        om_ref[0, d] = lookup(bm_ref[d], h)
    for d in range(bw_ref.shape[0]):
        ow_ref[0, d] = lookup(bw_ref[d], h) * LOG2E
    os_ref[0] = lookup(bs_ref[...], n_heads + h)


def _bias_tiles(rel_bias):
    n_heads = rel_bias.shape[1] // 2

    def toeplitz(size, count):
        i = jnp.arange(size)[:, None]
        j = jnp.arange(size)[None, :]
        return jnp.stack([_rel_bucket(d * size + i - j) for d in range(count)])

    bm = toeplitz(MOBA_BLOCK, MOBA_NEAR_TILES)
    bw = toeplitz(MOBA_WIDE, MOBA_WIDE_NEAR_TILES)
    w = SWA_WINDOW
    bs = _rel_bucket(jnp.arange(w)[:, None] + w - jnp.arange(2 * w)[None, :])
    return pl.pallas_call(
        functools.partial(_bias_kernel, n_heads=n_heads),
        grid=(n_heads,),
        in_specs=[pl.BlockSpec(memory_space=pltpu.SMEM),
                  pl.BlockSpec(bm.shape, lambda h: (0, 0, 0)),
                  pl.BlockSpec(bw.shape, lambda h: (0, 0, 0)),
                  pl.BlockSpec(bs.shape, lambda h: (0, 0))],
        out_specs=[pl.BlockSpec((1,) + bm.shape, lambda h: (h, 0, 0, 0)),
                   pl.BlockSpec((1,) + bw.shape, lambda h: (h, 0, 0, 0)),
                   pl.BlockSpec((1,) + bs.shape, lambda h: (h, 0, 0))],
        out_shape=[jax.ShapeDtypeStruct((n_heads,) + bm.shape, _F32),
                   jax.ShapeDtypeStruct((n_heads,) + bw.shape, _F32),
                   jax.ShapeDtypeStruct((n_heads,) + bs.shape, _F32)],
        compiler_params=pltpu.CompilerParams(
            dimension_semantics=("arbitrary",), vmem_limit_bytes=_VMEM_LIMIT),
        name="bias_tiles",
    )(rel_bias, bm, bw, bs)


def _forget_cumsum(ff_ref, bf_ref, c_sc):
    t_len = c_sc.shape[0]
    ff = ff_ref[0] + bf_ref[...]
    log_f = jnp.minimum(ff, 0.0) - jnp.log1p(jnp.exp(-jnp.abs(ff)))
    ch = 256
    tri = (lax.broadcasted_iota(jnp.int32, (ch, ch), 0)
           >= lax.broadcasted_iota(jnp.int32, (ch, ch), 1)).astype(_F32)
    carry = jnp.zeros((1, LANES), _F32)
    for r in range(t_len // ch):
        inc = jnp.dot(tri, log_f[r * ch:(r + 1) * ch], precision=lax.Precision.HIGHEST,
                      preferred_element_type=_F32) + carry
        c_sc[r * ch:(r + 1) * ch, :] = inc
        carry = inc[ch - 1:ch, :]


def _lane_column(x, col):
    return jnp.sum(jnp.where(_lane_iota(x.shape) == col, x, 0.0), axis=-1, keepdims=True)


def _lane_fields(shape, a0, fields):
    lane = _lane_iota(shape)
    out = jnp.zeros(shape, _F32)
    for n, f in enumerate(fields):
        out = jnp.where(lane == a0 + n, f, out)
    return out


def _bounded_weights(q_augs, kaug_sc, r, bias, diagonal):
    tq = q_augs[0].shape[0]
    ps = []
    for hh in range(2):
        s = lax.dot_general(q_augs[hh], kaug_sc[hh, pl.ds(r, tq), :], _NT,
                            preferred_element_type=_F32)
        if bias is not None:
            s = s + bias(hh)
        if diagonal:
            s = jnp.where(_causal(tq), s, NEG_INF)
        ps.append(jnp.exp2(s).astype(_MXU_DTYPE))
    return ps


def _bounded_values(ps, vext_sc, r, acc):
    tq = ps[0].shape[1]
    for hh in range(2):
        acc = acc + jnp.dot(ps[hh], vext_sc[hh, pl.ds(r, tq), :], preferred_element_type=_F32)
    return acc


def _fox_bounded_kernel(bound_ref, q_ref, k_ref, v_ref, ff_ref, g_ref, qkg_ref, bf_ref, o_ref,
                        kaug_sc, vext_sc, c_sc):
    p = pl.program_id(1)
    i = pl.program_id(2)
    tq = q_ref.shape[1]

    @pl.when(i == 0)
    def _prep():
        _forget_cumsum(ff_ref, bf_ref, c_sc)
        kn = _pair_rms(k_ref[0], qkg_ref[1:2, :])
        c_all = c_sc[...] * LOG2E
        v = v_ref[0]
        lane = _lane_iota(v.shape)
        for hh in range(2):
            a0 = HEAD_DIM if hh == 0 else 0
            hi, mid, lo = _split3(-_lane_column(c_all, 2 * p + hh))
            aug = _lane_fields(kn.shape, a0, [hi, mid, lo, 1.0, 1.0, 1.0, 1.0, 1.0])
            own = _own_half(hh, kn.shape)
            kaug_sc[hh] = jnp.where(own, kn, aug).astype(_MXU_DTYPE)
            vext_sc[hh, :, 0:LANES] = jnp.where(own, v, 0.0).astype(_MXU_DTYPE)
            vext_sc[hh, :, LANES:2 * LANES] = jnp.where(lane == hh, 1.0, 0.0).astype(_MXU_DTYPE)

    q = _pair_rms(q_ref[0], qkg_ref[0:1, :]) * (ATTN_SCALE * LOG2E)
    row0 = pl.multiple_of(i * tq, tq)
    c_t = c_sc[pl.ds(row0, tq), :] * LOG2E
    off = jnp.full((1, 1), -LOG2E, _F32) * bound_ref[0]
    off_hi = off.astype(_MXU_DTYPE).astype(_F32)
    q_augs = []
    for hh in range(2):
        a0 = HEAD_DIM if hh == 0 else 0
        hi, mid, lo = _split3(_lane_column(c_t, 2 * p + hh))
        aug = _lane_fields(q.shape, a0, [1.0, 1.0, 1.0, hi, mid, lo, off_hi, off - off_hi])
        q_augs.append(jnp.where(_own_half(hh, q.shape), q, aug).astype(_MXU_DTYPE))

    lane = _lane_iota((1, LANES))
    pair_lanes = (lane == 2 * p) | (lane == 2 * p + 1)
    c_first = c_sc[pl.ds(row0, 1), :]
    cutoff = EXP_ZERO_CUTOFF - 2.0 * bound_ref[0]

    def live(j):
        c_last = c_sc[pl.ds(jnp.maximum(j, 0) * tq + (tq - 1), 1), :]
        gap = jnp.max(jnp.where(pair_lanes, c_first - c_last, -jnp.inf))
        return (j >= 0) & (gap >= cutoff)

    def body(state):
        j, _, r_prev, ps0, ps1, acc = state
        r = pl.multiple_of(j * tq, tq)
        ps = _bounded_weights(q_augs, kaug_sc, r, None, False)
        acc = _bounded_values([ps0, ps1], vext_sc, pl.multiple_of(r_prev, tq), acc)
        return j - 1, live(j - 1), r, ps[0], ps[1], acc

    ps = _bounded_weights(q_augs, kaug_sc, row0, None, True)
    state = lax.while_loop(lambda state: state[1], body,
                           (i - 1, live(i - 1), row0, ps[0], ps[1],
                            jnp.zeros((tq, 2 * LANES), _F32)))
    acc = _bounded_values([state[3], state[4]], vext_sc, pl.multiple_of(state[2], tq), state[5])
    den = acc[:, LANES:]
    l = jnp.where(_own_half(0, (tq, LANES)), _lane_column(den, 0), _lane_column(den, 1))
    o_ref[0] = (acc[:, :LANES] / l * _silu(g_ref[0])).astype(o_ref.dtype)


def _fox_bounded(proj, bf_row, qk_gain, bound, *, tq=512):
    b, t, _ = proj.shape
    return pl.pallas_call(
        _fox_bounded_kernel,
        grid=(b, 2, t // tq),
        in_specs=[pl.BlockSpec(memory_space=pltpu.SMEM),
                  pl.BlockSpec((1, tq, LANES), lambda bb, p, i: (bb, i, COL_FQ + p)),
                  pl.BlockSpec((1, t, LANES), lambda bb, p, i: (bb, 0, COL_FK + p)),
                  pl.BlockSpec((1, t, LANES), lambda bb, p, i: (bb, 0, COL_FV + p)),
                  pl.BlockSpec((1, t, LANES), lambda bb, p, i: (bb, 0, COL_FF)),
                  pl.BlockSpec((1, tq, LANES), lambda bb, p, i: (bb, i, COL_FG + p)),
                  pl.BlockSpec((2, LANES), lambda bb, p, i: (0, 0)),
                  pl.BlockSpec((1, LANES), lambda bb, p, i: (0, 0))],
        out_specs=pl.BlockSpec((1, tq, LANES), lambda bb, p, i: (bb, i, p)),
        out_shape=jax.ShapeDtypeStruct((b, t, GROUP_WIDTH), _MXU_DTYPE),
        scratch_shapes=[pltpu.VMEM((2, t, LANES), _MXU_DTYPE),
                        pltpu.VMEM((2, t, 2 * LANES), _MXU_DTYPE),
                        pltpu.VMEM((t, LANES), _F32)],
        compiler_params=pltpu.CompilerParams(
            dimension_semantics=("arbitrary", "arbitrary", "arbitrary"),
            vmem_limit_bytes=_VMEM_LIMIT),
        name="fox_bounded",
    )(bound, proj, proj, proj, proj, proj, qk_gain, bf_row)


def _fox_kernel(q_ref, k_ref, v_ref, ff_ref, g_ref, qkg_ref, bf_ref, o_ref,
                kaug_sc, v_sc, c_sc, *, tq):
    p = pl.program_id(1)
    i = pl.program_id(2)

    def head_column(x, hh):
        return _lane_column(x, 2 * p + hh)

    @pl.when(i == 0)
    def _prep():
        _forget_cumsum(ff_ref, bf_ref, c_sc)
        kn = _pair_rms(k_ref[0], qkg_ref[1:2, :])
        c_all = c_sc[...]
        lane = _lane_iota(kn.shape)
        for hh in range(2):
            a0 = HEAD_DIM if hh == 0 else 0
            hi, mid, lo = _split3(-head_column(c_all, hh))
            aug = jnp.where(lane == a0, hi,
                            jnp.where(lane == a0 + 1, mid,
                                      jnp.where(lane == a0 + 2, lo,
                                                jnp.where((lane >= a0 + 3) & (lane < a0 + 6), 1.0, 0.0))))
            kaug_sc[hh] = jnp.where(_own_half(hh, kn.shape), kn, aug).astype(_MXU_DTYPE)
        v_sc[...] = v_ref[0].astype(_MXU_DTYPE)

    q = _pair_rms(q_ref[0], qkg_ref[0:1, :]) * ATTN_SCALE
    row0 = pl.multiple_of(i * tq, tq)
    c_t = c_sc[pl.ds(row0, tq), :]
    lane = _lane_iota(q.shape)
    outs = []
    for hh in range(2):
        a0 = HEAD_DIM if hh == 0 else 0
        hi, mid, lo = _split3(head_column(c_t, hh))
        aug = jnp.where((lane >= a0) & (lane < a0 + 3), 1.0,
                        jnp.where(lane == a0 + 3, hi,
                                  jnp.where(lane == a0 + 4, mid,
                                            jnp.where(lane == a0 + 5, lo, 0.0))))
        q_aug = jnp.where(_own_half(hh, q.shape), q, aug).astype(_MXU_DTYPE)

        s = lax.dot_general(q_aug, kaug_sc[hh, pl.ds(row0, tq), :], _NT,
                            preferred_element_type=_F32)
        s = jnp.where(_causal(tq), s, NEG_INF)
        m = jnp.max(s, axis=-1, keepdims=True)
        pr = jnp.exp(s - m)
        l = jnp.sum(pr, axis=-1, keepdims=True)
        acc = jnp.dot(pr.astype(_MXU_DTYPE), v_sc[pl.ds(row0, tq), :], preferred_element_type=_F32)

        def body(j, carry, q_aug=q_aug, hh=hh):
            r = pl.multiple_of(j * tq, tq)
            s = lax.dot_general(q_aug, kaug_sc[hh, pl.ds(r, tq), :], _NT,
                                preferred_element_type=_F32)
            return _flash_update(s, v_sc[pl.ds(r, tq), :], *carry)

        m, l, acc = lax.fori_loop(0, i, body, (m, l, acc))
        outs.append(acc / l)
    o = jnp.where(_own_half(0, outs[0].shape), outs[0], outs[1])
    o_ref[0] = (o * _silu(g_ref[0])).astype(o_ref.dtype)


def _fox(proj, bf_row, qk_gain, *, tq=256):
    b, t, _ = proj.shape
    return pl.pallas_call(
        functools.partial(_fox_kernel, tq=tq),
        grid=(b, 2, t // tq),
        in_specs=[pl.BlockSpec((1, tq, LANES), lambda bb, p, i: (bb, i, COL_FQ + p)),
                  pl.BlockSpec((1, t, LANES), lambda bb, p, i: (bb, 0, COL_FK + p)),
                  pl.BlockSpec((1, t, LANES), lambda bb, p, i: (bb, 0, COL_FV + p)),
                  pl.BlockSpec((1, t, LANES), lambda bb, p, i: (bb, 0, COL_FF)),
                  pl.BlockSpec((1, tq, LANES), lambda bb, p, i: (bb, i, COL_FG + p)),
                  pl.BlockSpec((2, LANES), lambda bb, p, i: (0, 0)),
                  pl.BlockSpec((1, LANES), lambda bb, p, i: (0, 0))],
        out_specs=pl.BlockSpec((1, tq, LANES), lambda bb, p, i: (bb, i, p)),
        out_shape=jax.ShapeDtypeStruct((b, t, GROUP_WIDTH), _MXU_DTYPE),
        scratch_shapes=[pltpu.VMEM((2, t, LANES), _MXU_DTYPE),
                        pltpu.VMEM((t, LANES), _MXU_DTYPE),
                        pltpu.VMEM((t, LANES), _F32)],
        compiler_params=pltpu.CompilerParams(
            dimension_semantics=("arbitrary", "arbitrary", "arbitrary"),
            vmem_limit_bytes=_VMEM_LIMIT),
        name="fox",
    )(proj, proj, proj, proj, proj, qk_gain, bf_row)


def _moba_block_means(kn, kmean_sc):
    kmean_sc[...] = jnp.zeros(kmean_sc.shape, _F32)
    for n in range(kn.shape[0] // MOBA_BLOCK):
        kmean_sc[n:n + 1, :] = jnp.mean(kn[n * MOBA_BLOCK:(n + 1) * MOBA_BLOCK], axis=0, keepdims=True)


def _moba_select(q_head, kmean, past):
    gate = lax.dot_general(q_head, kmean, _NT, precision=lax.Precision.HIGHEST,
                           preferred_element_type=_F32)
    lane_f = _lane_iota(gate.shape).astype(_F32)
    cand = jnp.where(past, gate, -jnp.inf)
    sel = jnp.zeros(gate.shape, _F32)
    for _ in range(MOBA_TOPK):
        mx = jnp.max(cand, axis=-1, keepdims=True)
        is_max = (cand == mx) & (mx > -jnp.inf)
        first = jnp.min(jnp.where(is_max, lane_f, float(LANES)), axis=-1, keepdims=True)
        pick = lane_f == first
        sel = jnp.where(pick, 1.0, sel)
        cand = jnp.where(pick, -jnp.inf, cand)
    return sel


def _moba_bounded_kernel(rb_ref, bound_ref, q_ref, k_ref, v_ref, g_ref, qkg_ref, bias_ref, o_ref,
                         kaug_sc, vext_sc, kmean_sc):
    p = pl.program_id(1)
    i = pl.program_id(2)
    tq = q_ref.shape[1]
    blk = MOBA_BLOCK
    near = MOBA_WIDE_NEAR_TILES

    @pl.when(i == 0)
    def _prep():
        kn = _pair_rms(k_ref[0], qkg_ref[1:2, :])
        _moba_block_means(kn, kmean_sc)
        v = v_ref[0]
        lane = _lane_iota(kn.shape)
        row_blk = lax.broadcasted_iota(jnp.int32, kn.shape, 0) // blk
        for hh in range(2):
            a0 = HEAD_DIM if hh == 0 else 0
            aug = jnp.where((lane - a0 == row_blk) | (lane - a0 - 16 == row_blk)
                            | (lane - a0 == 32) | (lane - a0 == 33), 1.0, 0.0)
            own = _own_half(hh, kn.shape)
            kaug_sc[hh] = jnp.where(own, kn, aug).astype(_MXU_DTYPE)
            vext_sc[hh, :, 0:LANES] = jnp.where(own, v, 0.0).astype(_MXU_DTYPE)
            vext_sc[hh, :, LANES:2 * LANES] = jnp.where(lane == hh, 1.0, 0.0).astype(_MXU_DTYPE)

    qn = _pair_rms(q_ref[0], qkg_ref[0:1, :])
    row0 = pl.multiple_of(i * tq, tq)
    off = jnp.full((1, 1), -LOG2E, _F32) * bound_ref[0]
    off_hi = off.astype(_MXU_DTYPE).astype(_F32)
    blk_n = lax.broadcasted_iota(jnp.int32, (16, tq), 0)
    blk_f = blk_n.astype(_F32)
    q_blk = i * (tq // blk) + lax.broadcasted_iota(jnp.int32, (16, tq), 1) // blk
    past = blk_n < q_blk
    far = i - blk_n // 2 >= near
    sub8 = lax.broadcasted_iota(jnp.int32, (8, tq), 0)
    off_rows = jnp.where(sub8 == 0, off_hi, jnp.where(sub8 == 1, off - off_hi, 0.0))
    q_augs = []
    for hh in range(2):
        own = _own_half(hh, qn.shape)
        gate = lax.dot_general(kmean_sc[...], jnp.where(own, qn, 0.0), _NT,
                               precision=lax.Precision.HIGHEST, preferred_element_type=_F32)
        cand = jnp.where(past, gate[0:16, :], -jnp.inf)
        sel = jnp.zeros((16, tq), _F32)
        for _ in range(MOBA_TOPK):
            mx = jnp.max(cand, axis=0, keepdims=True)
            is_max = (cand == mx) & (mx > -jnp.inf)
            first = jnp.min(jnp.where(is_max, blk_f, float(LANES)), axis=0, keepdims=True)
            pick = blk_f == first
            sel = jnp.where(pick, 1.0, sel)
            cand = jnp.where(pick, -jnp.inf, cand)
        c_far = jnp.full((1, 1), LOG2E, _F32) * rb_ref[NUM_BUCKETS - 1, 2 * p + hh]
        far_hi = c_far.astype(_MXU_DTYPE).astype(_F32)
        pen = jnp.where((blk_n == q_blk) | (past & (sel != 0.0)), 0.0, NEG_INF)
        fields = jnp.concatenate([pen + jnp.where(far, far_hi, 0.0),
                                  jnp.where(far, c_far - far_hi, 0.0),
                                  off_rows,
                                  jnp.zeros((HEAD_DIM - 40, tq), _F32)], axis=0)
        blank = jnp.zeros((HEAD_DIM, tq), _F32)
        aug = jnp.concatenate([blank, fields] if hh == 0 else [fields, blank], axis=0).T
        q_augs.append(jnp.where(own, qn * (ATTN_SCALE * LOG2E), aug).astype(_MXU_DTYPE))

    def body(j, state, with_bias):
        r_prev, ps0, ps1, acc = state
        r = pl.multiple_of(j * tq, tq)
        bias = (lambda hh: bias_ref[hh, i - j]) if with_bias else None
        ps = _bounded_weights(q_augs, kaug_sc, r, bias, False)
        acc = _bounded_values([ps0, ps1], vext_sc, pl.multiple_of(r_prev, tq), acc)
        return r, ps[0], ps[1], acc

    ps = _bounded_weights(q_augs, kaug_sc, row0, lambda hh: bias_ref[hh, 0], True)
    state = (row0, ps[0], ps[1], jnp.zeros((tq, 2 * LANES), _F32))
    near_lo = jnp.maximum(i - (near - 1), 0)
    state = lax.fori_loop(near_lo, i, functools.partial(body, with_bias=True), state)
    state = lax.fori_loop(0, near_lo, functools.partial(body, with_bias=False), state)
    acc = _bounded_values([state[1], state[2]], vext_sc, pl.multiple_of(state[0], tq), state[3])
    den = acc[:, LANES:]
    l = jnp.where(_own_half(0, (tq, LANES)), _lane_column(den, 0), _lane_column(den, 1))
    o_ref[0] = (acc[:, :LANES] / l * _silu(g_ref[0])).astype(o_ref.dtype)


def _moba_bounded(proj, rel_bias, qk_gain, bias_tiles, bound):
    b, t, _ = proj.shape
    tq = MOBA_WIDE
    assert t % tq == 0 and t // MOBA_BLOCK <= 16
    return pl.pallas_call(
        _moba_bounded_kernel,
        grid=(b, 2, t // tq),
        in_specs=[pl.BlockSpec(memory_space=pltpu.SMEM),
                  pl.BlockSpec(memory_space=pltpu.SMEM),
                  pl.BlockSpec((1, tq, LANES), lambda bb, p, i: (bb, i, COL_MQ + p)),
                  pl.BlockSpec((1, t, LANES), lambda bb, p, i: (bb, 0, COL_MK + p)),
                  pl.BlockSpec((1, t, LANES), lambda bb, p, i: (bb, 0, COL_MV + p)),
                  pl.BlockSpec((1, tq, LANES), lambda bb, p, i: (bb, i, COL_MG + p)),
                  pl.BlockSpec((2, LANES), lambda bb, p, i: (0, 0)),
                  pl.BlockSpec((2, MOBA_WIDE_NEAR_TILES, tq, tq), lambda bb, p, i: (p, 0, 0, 0))],
        out_specs=pl.BlockSpec((1, tq, LANES), lambda bb, p, i: (bb, i, p)),
        out_shape=jax.ShapeDtypeStruct((b, t, GROUP_WIDTH), _MXU_DTYPE),
        scratch_shapes=[pltpu.VMEM((2, t, LANES), _MXU_DTYPE),
                        pltpu.VMEM((2, t, 2 * LANES), _MXU_DTYPE),
                        pltpu.VMEM((LANES, LANES), _F32)],
        compiler_params=pltpu.CompilerParams(
            dimension_semantics=("arbitrary", "arbitrary", "arbitrary"),
            vmem_limit_bytes=_VMEM_LIMIT),
        name="moba_bounded",
    )(rel_bias, bound, proj, proj, proj, proj, qk_gain, bias_tiles)


def _moba_kernel(rb_ref, q_ref, k_ref, v_ref, g_ref, qkg_ref, bias_ref, o_ref,
                 kaug_sc, v_sc, kmean_sc):
    p = pl.program_id(1)
    i = pl.program_id(2)
    t_len = k_ref.shape[1]
    blk = MOBA_BLOCK
    nblk = t_len // blk
    near = MOBA_NEAR_TILES

    @pl.when(i == 0)
    def _prep():
        kn = _pair_rms(k_ref[0], qkg_ref[1:2, :])
        _moba_block_means(kn, kmean_sc)
        lane = _lane_iota(kn.shape)
        row_blk = lax.broadcasted_iota(jnp.int32, kn.shape, 0) // blk
        for hh in range(2):
            a0 = HEAD_DIM if hh == 0 else 0
            onehot = jnp.where((lane - a0 == row_blk) | (lane - a0 - 16 == row_blk), 1.0, 0.0)
            kaug_sc[hh] = jnp.where(_own_half(hh, kn.shape), kn, onehot).astype(_MXU_DTYPE)
        v_sc[...] = v_ref[0].astype(_MXU_DTYPE)

    qn = _pair_rms(q_ref[0], qkg_ref[0:1, :])
    row0 = pl.multiple_of(i * blk, blk)
    lane = _lane_iota(qn.shape)
    lane_f = lane.astype(_F32)
    past = lane < i
    outs = []
    for hh in range(2):
        own = _own_half(hh, qn.shape)
        sel = _moba_select(jnp.where(own, qn, 0.0), kmean_sc[...], past)
        c_far = jnp.full((1, LANES), rb_ref[NUM_BUCKETS - 1, 2 * p + hh], _F32)
        far_hi = c_far.astype(_MXU_DTYPE).astype(_F32)
        far_lo = c_far - far_hi
        pen = jnp.where(past & (sel == 0.0), NEG_INF, 0.0)
        aug = jnp.where(lane < 16, pen + jnp.where(i - lane >= near, far_hi, 0.0),
                        jnp.where((lane < 32) & (i - (lane - 16) >= near), far_lo, 0.0))
        if hh == 0:
            aug = pltpu.roll(aug, HEAD_DIM, axis=1)
        q_aug = jnp.where(own, qn * ATTN_SCALE, aug).astype(_MXU_DTYPE)

        def scores(r, q_aug=q_aug, hh=hh):
            return lax.dot_general(q_aug, kaug_sc[hh, pl.ds(r, blk), :], _NT,
                                   preferred_element_type=_F32)

        s = scores(row0) + bias_ref[hh, 0]
        s = jnp.where(_causal(blk), s, NEG_INF)
        m = jnp.max(s, axis=-1, keepdims=True)
        pr = jnp.exp(s - m)
        l = jnp.sum(pr, axis=-1, keepdims=True)
        acc = jnp.dot(pr.astype(_MXU_DTYPE), v_sc[pl.ds(row0, blk), :], preferred_element_type=_F32)

        def near_body(j, carry, hh=hh, scores=scores):
            r = pl.multiple_of(j * blk, blk)
            return _flash_update(scores(r) + bias_ref[hh, i - j], v_sc[pl.ds(r, blk), :], *carry)

        def far_body(j, carry, scores=scores):
            r = pl.multiple_of(j * blk, blk)
            return _flash_update(scores(r), v_sc[pl.ds(r, blk), :], *carry)

        near_lo = jnp.maximum(i - (near - 1), 0)
        carry = lax.fori_loop(near_lo, i, near_body, (m, l, acc))
        m, l, acc = lax.fori_loop(0, near_lo, far_body, carry)
        outs.append(acc / l)
    o = jnp.where(_own_half(0, outs[0].shape), outs[0], outs[1])
    o_ref[0] = (o * _silu(g_ref[0])).astype(o_ref.dtype)


def _moba(proj, rel_bias, qk_gain, bias_tiles):
    b, t, _ = proj.shape
    blk = MOBA_BLOCK
    assert t % blk == 0 and t // blk <= 16
    return pl.pallas_call(
        _moba_kernel,
        grid=(b, 2, t // blk),
        in_specs=[pl.BlockSpec(memory_space=pltpu.SMEM),
                  pl.BlockSpec((1, blk, LANES), lambda bb, p, i: (bb, i, COL_MQ + p)),
                  pl.BlockSpec((1, t, LANES), lambda bb, p, i: (bb, 0, COL_MK + p)),
                  pl.BlockSpec((1, t, LANES), lambda bb, p, i: (bb, 0, COL_MV + p)),
                  pl.BlockSpec((1, blk, LANES), lambda bb, p, i: (bb, i, COL_MG + p)),
                  pl.BlockSpec((2, LANES), lambda bb, p, i: (0, 0)),
                  pl.BlockSpec((2, MOBA_NEAR_TILES, blk, blk), lambda bb, p, i: (p, 0, 0, 0))],
        out_specs=pl.BlockSpec((1, blk, LANES), lambda bb, p, i: (bb, i, p)),
        out_shape=jax.ShapeDtypeStruct((b, t, GROUP_WIDTH), _MXU_DTYPE),
        scratch_shapes=[pltpu.VMEM((2, t, LANES), _MXU_DTYPE),
                        pltpu.VMEM((t, LANES), _MXU_DTYPE),
                        pltpu.VMEM((LANES, LANES), _F32)],
        compiler_params=pltpu.CompilerParams(
            dimension_semantics=("arbitrary", "arbitrary", "arbitrary"),
            vmem_limit_bytes=_VMEM_LIMIT),
        name="moba",
    )(rel_bias, proj, proj, proj, proj, qk_gain, bias_tiles)


def _sb_kernel(q_ref, k_ref, v_ref, g_ref, o_ref, k_sc, v_sc, *, tq):
    i = pl.program_id(2)

    @pl.when(i == 0)
    def _prep():
        k_sc[...] = k_ref[0].astype(_MXU_DTYPE)
        v = v_ref[0]
        for hh in range(2):
            v_sc[hh] = jnp.where(_own_half(hh, v.shape), v, 0.0).astype(_MXU_DTYPE)

    q = q_ref[0] * (ATTN_SCALE * LOG2E)
    q_heads = [jnp.where(_own_half(hh, q.shape), q, 0.0).astype(_MXU_DTYPE) for hh in range(2)]
    row0 = pl.multiple_of(i * tq, tq)
    strict = (lax.broadcasted_iota(jnp.int32, (tq, tq), 0)
              > lax.broadcasted_iota(jnp.int32, (tq, tq), 1))
    after = strict.astype(_MXU_DTYPE)
    sign_bit = jnp.uint32(0x80000000)

    def pair(j, runs, diagonal):
        has_prev = (j >= 1).astype(_F32)
        rows = [pl.multiple_of(j * tq, tq), pl.multiple_of(jnp.maximum(j - 1, 0) * tq, tq)]
        chains = [(t, hh) for t in range(2) for hh in range(2)]
        masked = [diagonal and t == 0 for t, _ in chains]
        zs = [lax.dot_general(q_heads[hh], k_sc[pl.ds(rows[t], tq), :], _NT,
                              preferred_element_type=_F32) for t, hh in chains]
        drops = []
        for z, msk in zip(zs, masked):
            neg_abs = lax.bitcast_convert_type(lax.bitcast_convert_type(z, jnp.uint32) | sign_bit, _F32)
            drop = jnp.maximum(z, 0.0) + jnp.log(1.0 + jnp.exp2(neg_abs)) * LOG2E
            drops.append(jnp.where(strict, drop, 0.0) if msk else drop)
        laters = []
        for drop in drops:
            hi = drop.astype(_MXU_DTYPE)
            lo = (drop - hi.astype(_F32)).astype(_MXU_DTYPE)
            laters.append(jnp.dot(hi, after, preferred_element_type=_F32)
                          + jnp.dot(lo, after, preferred_element_type=_F32))
        pvs = []
        for (t, hh), z, drop, later, msk in zip(chains, zs, drops, laters, masked):
            w = jnp.exp2(z - drop - later)
            if msk:
                w = jnp.where(strict, w, 0.0)
            pvs.append(jnp.dot(w.astype(_MXU_DTYPE), v_sc[hh, pl.ds(rows[t], tq), :],
                               preferred_element_type=_F32))
        sums = [jnp.sum(drop, axis=-1, keepdims=True) for drop in drops]
        new_runs, out = [], None
        for hh in range(2):
            run_b = runs[hh] - sums[hh]
            pv = pvs[hh] * jnp.exp2(runs[hh]) + pvs[2 + hh] * (jnp.exp2(run_b) * has_prev)
            out = pv if out is None else out + pv
            new_runs.append(run_b - sums[2 + hh] * has_prev)
        return new_runs, out

    def alive(runs):
        return jnp.max(jnp.maximum(runs[0], runs[1])) >= EXP_ZERO_CUTOFF * LOG2E

    zero = jnp.zeros((tq, 1), _F32)
    runs, acc = pair(i, [zero, zero], True)

    def body(state):
        j, _, run0, run1, acc = state
        runs, out = pair(j, [run0, run1], False)
        return j - 2, alive(runs), runs[0], runs[1], acc + out

    state = lax.while_loop(lambda state: (state[0] >= 0) & state[1], body,
                           (i - 2, alive(runs), runs[0], runs[1], acc))
    o_ref[0] = (state[4] * _silu(g_ref[0])).astype(o_ref.dtype)


def _sb(proj, *, tq=256):
    b, t, _ = proj.shape
    return pl.pallas_call(
        functools.partial(_sb_kernel, tq=tq),
        grid=(b, 2, t // tq),
        in_specs=[pl.BlockSpec((1, tq, LANES), lambda bb, p, i: (bb, i, COL_SQ + p)),
                  pl.BlockSpec((1, t, LANES), lambda bb, p, i: (bb, 0, COL_SK + p)),
                  pl.BlockSpec((1, t, LANES), lambda bb, p, i: (bb, 0, COL_SV + p)),
                  pl.BlockSpec((1, tq, LANES), lambda bb, p, i: (bb, i, COL_SG + p))],
        out_specs=pl.BlockSpec((1, tq, LANES), lambda bb, p, i: (bb, i, p)),
        out_shape=jax.ShapeDtypeStruct((b, t, GROUP_WIDTH), _MXU_DTYPE),
        scratch_shapes=[pltpu.VMEM((t, LANES), _MXU_DTYPE),
                        pltpu.VMEM((2, t, LANES), _MXU_DTYPE)],
        compiler_params=pltpu.CompilerParams(
            dimension_semantics=("arbitrary", "arbitrary", "arbitrary"),
            vmem_limit_bytes=_VMEM_LIMIT),
        name="stickbreak",
    )(proj, proj, proj, proj)


def _swa_kernel(sink_ref, q_ref, k_ref, v_ref, g_ref, qkg_ref, bias_ref, o_ref,
                k_sc, v_sc, *, tq):
    kv = pl.program_id(1)
    i = pl.program_id(2)
    t_len = k_ref.shape[1]
    w = SWA_WINDOW

    @pl.when(i == 0)
    def _prep():
        kn = _pair_rms(k_ref[0], qkg_ref[1:2, :])
        v = v_ref[0]
        keep = (_lane_iota(kn.shape) < HEAD_DIM) == (kv == 0)
        k_sc[0:w, :] = jnp.zeros((w, LANES), _MXU_DTYPE)
        v_sc[0:w, :] = jnp.zeros((w, LANES), _MXU_DTYPE)
        k_sc[w:w + t_len, :] = jnp.where(keep, kn, pltpu.roll(kn, HEAD_DIM, axis=1)).astype(_MXU_DTYPE)
        v_sc[w:w + t_len, :] = jnp.where(keep, v, pltpu.roll(v, HEAD_DIM, axis=1)).astype(_MXU_DTYPE)

    q = _pair_rms(q_ref[0], qkg_ref[0:1, :]) * ATTN_SCALE
    qi = lax.broadcasted_iota(jnp.int32, (w, 2 * w), 0)
    kj = lax.broadcasted_iota(jnp.int32, (w, 2 * w), 1)
    dist = qi + w - kj
    in_window = (dist >= 0) & (dist < w)
    for u in range(tq // w):
        nb = i * (tq // w) + u
        r = pl.multiple_of(nb * w, w)
        k_t = k_sc[pl.ds(r, 2 * w), :]
        v_t = v_sc[pl.ds(r, 2 * w), :]
        allowed = in_window & (kj + (nb - 1) * w >= 0)
        q_u = q[u * w:(u + 1) * w]
        outs = []
        for g in range(2):
            q_h = jnp.where(_own_half(g, q_u.shape), q_u, 0.0).astype(_MXU_DTYPE)
            s = lax.dot_general(q_h, k_t, _NT, preferred_element_type=_F32)
            s = jnp.where(allowed, s + bias_ref[g], NEG_INF)
            sink = sink_ref[2 * kv + g]
            m = jnp.maximum(jnp.max(s, axis=-1, keepdims=True), sink)
            e = jnp.exp(s - m)
            den = jnp.sum(e, axis=-1, keepdims=True) + jnp.exp(sink - m)
            outs.append(jnp.dot(e.astype(_MXU_DTYPE), v_t, preferred_element_type=_F32) / den)
        o = jnp.where(_own_half(0, outs[0].shape), outs[0], outs[1])
        o_ref[0, u * w:(u + 1) * w, :] = (o * _silu(g_ref[0, u * w:(u + 1) * w, :])).astype(o_ref.dtype)


def _swa(proj, sinks, qk_gain, bias_tiles, *, tq=512):
    b, t, _ = proj.shape
    w = SWA_WINDOW
    return pl.pallas_call(
        functools.partial(_swa_kernel, tq=tq),
        grid=(b, 2, t // tq),
        in_specs=[pl.BlockSpec(memory_space=pltpu.SMEM),
                  pl.BlockSpec((1, tq, LANES), lambda bb, kv, i: (bb, i, COL_WQ + kv)),
                  pl.BlockSpec((1, t, LANES), lambda bb, kv, i: (bb, 0, COL_WK)),
                  pl.BlockSpec((1, t, LANES), lambda bb, kv, i: (bb, 0, COL_WV)),
                  pl.BlockSpec((1, tq, LANES), lambda bb, kv, i: (bb, i, COL_WG + kv)),
                  pl.BlockSpec((2, LANES), lambda bb, kv, i: (0, 0)),
                  pl.BlockSpec((2, w, 2 * w), lambda bb, kv, i: (kv, 0, 0))],
        out_specs=pl.BlockSpec((1, tq, LANES), lambda bb, kv, i: (bb, i, kv)),
        out_shape=jax.ShapeDtypeStruct((b, t, GROUP_WIDTH), _MXU_DTYPE),
        scratch_shapes=[pltpu.VMEM((t + w, LANES), _MXU_DTYPE),
                        pltpu.VMEM((t + w, LANES), _MXU_DTYPE)],
        compiler_params=pltpu.CompilerParams(
            dimension_semantics=("arbitrary", "arbitrary", "arbitrary"),
            vmem_limit_bytes=_VMEM_LIMIT),
        name="swa",
    )(sinks, proj, proj, proj, proj, qk_gain, bias_tiles)


def _outproj_kernel(x_ref, ya_ref, yb_ref, yc_ref, yd_ref, w_ref, o_ref):
    acc = x_ref[...]
    for g, y_ref in enumerate((ya_ref, yb_ref, yc_ref, yd_ref)):
        acc = acc + jnp.dot(y_ref[...], w_ref[g * GROUP_WIDTH:(g + 1) * GROUP_WIDTH, :],
                            preferred_element_type=_F32)
    o_ref[...] = acc


def _outproj(x2d, ys, w):
    m, d = x2d.shape
    tm = 512
    y_spec = pl.BlockSpec((tm, GROUP_WIDTH), lambda i: (i, 0))
    return pl.pallas_call(
        _outproj_kernel,
        grid=(m // tm,),
        in_specs=[pl.BlockSpec((tm, d), lambda i: (i, 0)), y_spec, y_spec, y_spec, y_spec,
                  pl.BlockSpec(w.shape, lambda i: (0, 0))],
        out_specs=pl.BlockSpec((tm, d), lambda i: (i, 0)),
        out_shape=jax.ShapeDtypeStruct((m, d), _F32),
        compiler_params=pltpu.CompilerParams(
            dimension_semantics=("arbitrary",), vmem_limit_bytes=_VMEM_LIMIT),
        name="outproj",
    )(x2d, *ys, w)


def _rearrange_w_in(w_in):
    off = 3 * GROUP_WIDTH
    main = jnp.concatenate([w_in[..., :off], w_in[..., off + FOX_FORGET_COLS:]], axis=-1)
    ff = w_in[..., off:off + FOX_FORGET_COLS]
    pad = PROJ_WIDTH - main.shape[-1] - FOX_FORGET_COLS
    return jnp.concatenate([main, ff, jnp.zeros(w_in.shape[:-1] + (pad,), w_in.dtype)], axis=-1)


def _qk_logit_bound(qk_gain):
    g = jnp.abs(qk_gain.astype(_F32))
    return ATTN_SCALE * HEAD_DIM * jnp.max(g[0]) * jnp.max(g[1])


def _pair_gain(g):
    return jnp.tile(g.astype(_F32), (1, 2))


def kernel(x, norm_gain, w_in, b_forget, fox_qk_gain, moba_qk_gain, swa_qk_gain, sinks, w_out, rel_bias):
    b, t, d = x.shape
    depth = w_in.shape[0]
    w_in_r = _rearrange_w_in(w_in).astype(_MXU_DTYPE)
    w_out_c = w_out.astype(_MXU_DTYPE)
    moba_bias, moba_wide_bias, swa_bias = _bias_tiles(rel_bias)
    moba_bias_max = jnp.max(jnp.abs(rel_bias[:, :rel_bias.shape[1] // 2].astype(_F32)))
    x2d = x.reshape(b * t, d)
    for layer in range(depth):
        proj = _inproj(x2d, norm_gain[layer][None, :], w_in_r[layer]).reshape(b, t, PROJ_WIDTH)
        bf_row = jnp.pad(b_forget[layer], (0, LANES - FOX_FORGET_COLS))[None, :]
        fox_gain = _pair_gain(fox_qk_gain[layer])
        fox_bound = _qk_logit_bound(fox_qk_gain[layer])
        y_fox = lax.cond(2.0 * fox_bound <= BOUNDED_SOFTMAX_RANGE,
                         lambda: _fox_bounded(proj, bf_row, fox_gain, fox_bound.reshape(1)),
                         lambda: _fox(proj, bf_row, fox_gain))
        moba_gain = _pair_gain(moba_qk_gain[layer])
        moba_bound = _qk_logit_bound(moba_qk_gain[layer]) + moba_bias_max
        y_moba = lax.cond(2.0 * moba_bound <= BOUNDED_SOFTMAX_RANGE,
                          lambda: _moba_bounded(proj, rel_bias, moba_gain, moba_wide_bias,
                                                moba_bound.reshape(1)),
                          lambda: _moba(proj, rel_bias, moba_gain, moba_bias))
        y_sb = _sb(proj)
        y_swa = _swa(proj, sinks[layer], _pair_gain(swa_qk_gain[layer]), swa_bias)
        ys = [y.reshape(b * t, GROUP_WIDTH) for y in (y_fox, y_moba, y_sb, y_swa)]
        x2d = _outproj(x2d, ys, w_out_c[layer])
    return x2d.reshape(b, t, d)
```

```python
import functools
import math

import jax
import jax.numpy as jnp
from jax import lax
from jax.experimental import pallas as pl
from jax.experimental.pallas import tpu as pltpu

HEAD_DIM = 64
LANES = 128
GROUP_WIDTH = 256
MOBA_BLOCK = 256
MOBA_TOPK = 3
SWA_WINDOW = 128
NUM_BUCKETS = 32
REL_MAX_DISTANCE = 1024
RMS_EPS = 1e-6
NEG_INF = -1e30
ATTN_SCALE = HEAD_DIM ** -0.5
LOG2E = math.log2(math.e)
BOUNDED_SOFTMAX_RANGE = 60.0
MOBA_NEAR_TILES = REL_MAX_DISTANCE // MOBA_BLOCK + 1
MOBA_WIDE = 2 * MOBA_BLOCK
MOBA_WIDE_NEAR_TILES = REL_MAX_DISTANCE // MOBA_WIDE + 1
FOX_FORGET_COLS = 4
EXP_ZERO_CUTOFF = -104.0

COL_FQ, COL_FK, COL_FV, COL_FG = 0, 2, 4, 6
COL_MQ, COL_MK, COL_MV, COL_MG = 8, 10, 12, 14
COL_SQ, COL_SK, COL_SV, COL_SG = 16, 18, 20, 22
COL_WQ, COL_WK, COL_WV, COL_WG = 24, 26, 27, 28
COL_FF = 30
PROJ_WIDTH = 32 * LANES
QK_NORM_COLS = (COL_FQ, COL_FQ + 1, COL_FK, COL_FK + 1, COL_MQ, COL_MQ + 1, COL_MK, COL_MK + 1,
                COL_WQ, COL_WQ + 1, COL_WK)

_MXU_DTYPE = jnp.bfloat16
_F32 = jnp.float32
_VMEM_LIMIT = 48 * 1024 * 1024

_NT = (((1,), (1,)), ((), ()))


def _lane_iota(shape):
    return lax.broadcasted_iota(jnp.int32, shape, len(shape) - 1)


def _pair_rms(x, gain_row):
    low = _lane_iota(x.shape) < HEAD_DIM
    sq = x * x
    ms_lo = jnp.sum(jnp.where(low, sq, 0.0), axis=-1, keepdims=True)
    ms_hi = jnp.sum(jnp.where(low, 0.0, sq), axis=-1, keepdims=True)
    ms = jnp.where(low, ms_lo, ms_hi) * (1.0 / HEAD_DIM)
    return x * lax.rsqrt(ms + RMS_EPS) * gain_row


def _silu(g):
    return g * (1.0 / (1.0 + jnp.exp(-g)))


def _split3(x):
    hi = x.astype(_MXU_DTYPE).astype(_F32)
    r = x - hi
    mid = r.astype(_MXU_DTYPE).astype(_F32)
    return hi, mid, r - mid


def _own_half(hh, shape):
    lane = _lane_iota(shape)
    return (lane < HEAD_DIM) if hh == 0 else (lane >= HEAD_DIM)


def _flash_update(s, v_tile, m, l, acc):
    m_new = jnp.maximum(m, jnp.max(s, axis=-1, keepdims=True))
    alpha = jnp.exp(m - m_new)
    p = jnp.exp(s - m_new)
    l = alpha * l + jnp.sum(p, axis=-1, keepdims=True)
    acc = alpha * acc + jnp.dot(p.astype(_MXU_DTYPE), v_tile, preferred_element_type=_F32)
    return m_new, l, acc


def _causal(tq):
    return (lax.broadcasted_iota(jnp.int32, (tq, tq), 0)
            >= lax.broadcasted_iota(jnp.int32, (tq, tq), 1))


def _inproj_kernel(x_ref, gain_ref, w_ref, qkg_ref, o_ref, *, tn):
    x = x_ref[...]
    ms = jnp.mean(x * x, axis=-1, keepdims=True)
    hn = (x * lax.rsqrt(ms + RMS_EPS) * gain_ref[...]).astype(_MXU_DTYPE)
    for n in range(o_ref.shape[1] // tn):
        acc = jnp.dot(hn, w_ref[:, n * tn:(n + 1) * tn], preferred_element_type=_F32)
        for c in range(n * tn // LANES, (n + 1) * tn // LANES):
            tile = acc[:, c * LANES - n * tn:(c + 1) * LANES - n * tn]
            if c in QK_NORM_COLS:
                tile = _pair_rms(tile, qkg_ref[:, c * LANES:(c + 1) * LANES])
            o_ref[:, c * LANES:(c + 1) * LANES] = tile


def _inproj(x2d, gain_row, w, qk_gain_row):
    m, d = x2d.shape
    n = w.shape[1]
    tm = 256
    return pl.pallas_call(
        functools.partial(_inproj_kernel, tn=1024),
        grid=(m // tm,),
        in_specs=[pl.BlockSpec((tm, d), lambda i: (i, 0)),
                  pl.BlockSpec((1, d), lambda i: (0, 0)),
                  pl.BlockSpec((d, n), lambda i: (0, 0)),
                  pl.BlockSpec((1, n), lambda i: (0, 0))],
        out_specs=pl.BlockSpec((tm, n), lambda i: (i, 0)),
        out_shape=jax.ShapeDtypeStruct((m, n), _F32),
        compiler_params=pltpu.CompilerParams(
            dimension_semantics=("arbitrary",), vmem_limit_bytes=_VMEM_LIMIT),
        name="inproj",
    )(x2d, gain_row, w, qk_gain_row)


def _rel_bucket(dist):
    max_exact = NUM_BUCKETS // 2
    d = jnp.maximum(dist, 0)
    log_ratio = (jnp.log(jnp.maximum(d, 1).astype(_F32) / max_exact)
                 / math.log(REL_MAX_DISTANCE / max_exact))
    large = max_exact + (log_ratio * (NUM_BUCKETS - max_exact)).astype(jnp.int32)
    large = jnp.minimum(large, NUM_BUCKETS - 1)
    return jnp.where(d < max_exact, d, large)


def _bias_kernel(rb_ref, bm_ref, bw_ref, bs_ref, om_ref, ow_ref, os_ref, *, n_heads):
    h = pl.program_id(0)

    def lookup(buckets, col):
        acc = jnp.zeros(buckets.shape, _F32)
        for k in range(NUM_BUCKETS):
            acc = jnp.where(buckets == k, rb_ref[k, col], acc)
        return acc

    for d in range(bm_ref.shape[0]):
        om_ref[0, d] = lookup(bm_ref[d], h)
    for d in range(bw_ref.shape[0]):
        ow_ref[0, d] = lookup(bw_ref[d], h) * LOG2E
    os_ref[0] = lookup(bs_ref[...], n_heads + h)


def _bias_tiles(rel_bias):
    n_heads = rel_bias.shape[1] // 2

    def toeplitz(size, count):
        i = jnp.arange(size)[:, None]
        j = jnp.arange(size)[None, :]
        return jnp.stack([_rel_bucket(d * size + i - j) for d in range(count)])

    bm = toeplitz(MOBA_BLOCK, MOBA_NEAR_TILES)
    bw = toeplitz(MOBA_WIDE, MOBA_WIDE_NEAR_TILES)
    w = SWA_WINDOW
    bs = _rel_bucket(jnp.arange(w)[:, None] + w - jnp.arange(2 * w)[None, :])
    return pl.pallas_call(
        functools.partial(_bias_kernel, n_heads=n_heads),
        grid=(n_heads,),
        in_specs=[pl.BlockSpec(memory_space=pltpu.SMEM),
                  pl.BlockSpec(bm.shape, lambda h: (0, 0, 0)),
                  pl.BlockSpec(bw.shape, lambda h: (0, 0, 0)),
                  pl.BlockSpec(bs.shape, lambda h: (0, 0))],
        out_specs=[pl.BlockSpec((1,) + bm.shape, lambda h: (h, 0, 0, 0)),
                   pl.BlockSpec((1,) + bw.shape, lambda h: (h, 0, 0, 0)),
                   pl.BlockSpec((1,) + bs.shape, lambda h: (h, 0, 0))],
        out_shape=[jax.ShapeDtypeStruct((n_heads,) + bm.shape, _F32),
                   jax.ShapeDtypeStruct((n_heads,) + bw.shape, _F32),
                   jax.ShapeDtypeStruct((n_heads,) + bs.shape, _F32)],
        compiler_params=pltpu.CompilerParams(
            dimension_semantics=("arbitrary",), vmem_limit_bytes=_VMEM_LIMIT),
        name="bias_tiles",
    )(rel_bias, bm, bw, bs)


def _forget_cumsum(ff_ref, bf_ref, c_sc):
    t_len = c_sc.shape[0]
    ff = ff_ref[0] + bf_ref[...]
    log_f = jnp.minimum(ff, 0.0) - jnp.log1p(jnp.exp(-jnp.abs(ff)))
    ch = 256
    tri = (lax.broadcasted_iota(jnp.int32, (ch, ch), 0)
           >= lax.broadcasted_iota(jnp.int32, (ch, ch), 1)).astype(_F32)
    carry = jnp.zeros((1, LANES), _F32)
    for r in range(t_len // ch):
        inc = jnp.dot(tri, log_f[r * ch:(r + 1) * ch], precision=lax.Precision.HIGHEST,
                      preferred_element_type=_F32) + carry
        c_sc[r * ch:(r + 1) * ch, :] = inc
        carry = inc[ch - 1:ch, :]


def _lane_column(x, col):
    return jnp.sum(jnp.where(_lane_iota(x.shape) == col, x, 0.0), axis=-1, keepdims=True)


def _lane_fields(shape, a0, fields):
    lane = _lane_iota(shape)
    out = jnp.zeros(shape, _F32)
    for n, f in enumerate(fields):
        out = jnp.where(lane == a0 + n, f, out)
    return out


def _bounded_weights(q_augs, kaug_sc, r, bias, diagonal):
    tq = q_augs[0].shape[0]
    ps = []
    for hh in range(2):
        s = lax.dot_general(q_augs[hh], kaug_sc[hh, pl.ds(r, tq), :], _NT,
                            preferred_element_type=_F32)
        if bias is not None:
            s = s + bias(hh)
        if diagonal:
            s = jnp.where(_causal(tq), s, NEG_INF)
        ps.append(jnp.exp2(s).astype(_MXU_DTYPE))
    return ps


def _bounded_values(ps, vext_sc, r, acc):
    tq = ps[0].shape[1]
    for hh in range(2):
        acc = acc + jnp.dot(ps[hh], vext_sc[hh, pl.ds(r, tq), :], preferred_element_type=_F32)
    return acc


def _fox_bounded_kernel(bound_ref, q_ref, k_ref, v_ref, ff_ref, g_ref, bf_ref, o_ref,
                        kaug_sc, vext_sc, c_sc):
    p = pl.program_id(1)
    i = pl.program_id(2)
    tq = q_ref.shape[1]

    @pl.when(i == 0)
    def _prep():
        _forget_cumsum(ff_ref, bf_ref, c_sc)
        kn = k_ref[0]
        c_all = c_sc[...] * LOG2E
        v = v_ref[0]
        lane = _lane_iota(v.shape)
        for hh in range(2):
            a0 = HEAD_DIM if hh == 0 else 0
            hi, mid, lo = _split3(-_lane_column(c_all, 2 * p + hh))
            aug = _lane_fields(kn.shape, a0, [hi, mid, lo, 1.0, 1.0, 1.0, 1.0, 1.0])
            own = _own_half(hh, kn.shape)
            kaug_sc[hh] = jnp.where(own, kn, aug).astype(_MXU_DTYPE)
            vext_sc[hh, :, 0:LANES] = jnp.where(own, v, 0.0).astype(_MXU_DTYPE)
            vext_sc[hh, :, LANES:2 * LANES] = jnp.where(lane == hh, 1.0, 0.0).astype(_MXU_DTYPE)

    q = q_ref[0] * (ATTN_SCALE * LOG2E)
    row0 = pl.multiple_of(i * tq, tq)
    c_t = c_sc[pl.ds(row0, tq), :] * LOG2E
    off = jnp.full((1, 1), -LOG2E, _F32) * bound_ref[0]
    off_hi = off.astype(_MXU_DTYPE).astype(_F32)
    q_augs = []
    for hh in range(2):
        a0 = HEAD_DIM if hh == 0 else 0
        hi, mid, lo = _split3(_lane_column(c_t, 2 * p + hh))
        aug = _lane_fields(q.shape, a0, [1.0, 1.0, 1.0, hi, mid, lo, off_hi, off - off_hi])
        q_augs.append(jnp.where(_own_half(hh, q.shape), q, aug).astype(_MXU_DTYPE))

    lane = _lane_iota((1, LANES))
    pair_lanes = (lane == 2 * p) | (lane == 2 * p + 1)
    c_first = c_sc[pl.ds(row0, 1), :]
    cutoff = EXP_ZERO_CUTOFF - 2.0 * bound_ref[0]

    def live(j):
        c_last = c_sc[pl.ds(jnp.maximum(j, 0) * tq + (tq - 1), 1), :]
        gap = jnp.max(jnp.where(pair_lanes, c_first - c_last, -jnp.inf))
        return (j >= 0) & (gap >= cutoff)

    def body(state):
        j, _, r_prev, ps0, ps1, acc = state
        r = pl.multiple_of(j * tq, tq)
        ps = _bounded_weights(q_augs, kaug_sc, r, None, False)
        acc = _bounded_values([ps0, ps1], vext_sc, pl.multiple_of(r_prev, tq), acc)
        return j - 1, live(j - 1), r, ps[0], ps[1], acc

    ps = _bounded_weights(q_augs, kaug_sc, row0, None, True)
    state = lax.while_loop(lambda state: state[1], body,
                           (i - 1, live(i - 1), row0, ps[0], ps[1],
                            jnp.zeros((tq, 2 * LANES), _F32)))
    acc = _bounded_values([state[3], state[4]], vext_sc, pl.multiple_of(state[2], tq), state[5])
    den = acc[:, LANES:]
    l = jnp.where(_own_half(0, (tq, LANES)), _lane_column(den, 0), _lane_column(den, 1))
    o_ref[0] = (acc[:, :LANES] / l * _silu(g_ref[0])).astype(o_ref.dtype)


def _fox_bounded(proj, bf_row, bound, *, tq=512):
    b, t, _ = proj.shape
    return pl.pallas_call(
        _fox_bounded_kernel,
        grid=(b, 2, t // tq),
        in_specs=[pl.BlockSpec(memory_space=pltpu.SMEM),
                  pl.BlockSpec((1, tq, LANES), lambda bb, p, i: (bb, i, COL_FQ + p)),
                  pl.BlockSpec((1, t, LANES), lambda bb, p, i: (bb, 0, COL_FK + p)),
                  pl.BlockSpec((1, t, LANES), lambda bb, p, i: (bb, 0, COL_FV + p)),
                  pl.BlockSpec((1, t, LANES), lambda bb, p, i: (bb, 0, COL_FF)),
                  pl.BlockSpec((1, tq, LANES), lambda bb, p, i: (bb, i, COL_FG + p)),
                  pl.BlockSpec((1, LANES), lambda bb, p, i: (0, 0))],
        out_specs=pl.BlockSpec((1, tq, LANES), lambda bb, p, i: (bb, i, p)),
        out_shape=jax.ShapeDtypeStruct((b, t, GROUP_WIDTH), _MXU_DTYPE),
        scratch_shapes=[pltpu.VMEM((2, t, LANES), _MXU_DTYPE),
                        pltpu.VMEM((2, t, 2 * LANES), _MXU_DTYPE),
                        pltpu.VMEM((t, LANES), _F32)],
        compiler_params=pltpu.CompilerParams(
            dimension_semantics=("arbitrary", "arbitrary", "arbitrary"),
            vmem_limit_bytes=_VMEM_LIMIT),
        name="fox_bounded",
    )(bound, proj, proj, proj, proj, proj, bf_row)


def _fox_kernel(q_ref, k_ref, v_ref, ff_ref, g_ref, bf_ref, o_ref,
                kaug_sc, v_sc, c_sc, *, tq):
    p = pl.program_id(1)
    i = pl.program_id(2)

    def head_column(x, hh):
        return _lane_column(x, 2 * p + hh)

    @pl.when(i == 0)
    def _prep():
        _forget_cumsum(ff_ref, bf_ref, c_sc)
        kn = k_ref[0]
        c_all = c_sc[...]
        lane = _lane_iota(kn.shape)
        for hh in range(2):
            a0 = HEAD_DIM if hh == 0 else 0
            hi, mid, lo = _split3(-head_column(c_all, hh))
            aug = jnp.where(lane == a0, hi,
                            jnp.where(lane == a0 + 1, mid,
                                      jnp.where(lane == a0 + 2, lo,
                                                jnp.where((lane >= a0 + 3) & (lane < a0 + 6), 1.0, 0.0))))
            kaug_sc[hh] = jnp.where(_own_half(hh, kn.shape), kn, aug).astype(_MXU_DTYPE)
        v_sc[...] = v_ref[0].astype(_MXU_DTYPE)

    q = q_ref[0] * ATTN_SCALE
    row0 = pl.multiple_of(i * tq, tq)
    c_t = c_sc[pl.ds(row0, tq), :]
    lane = _lane_iota(q.shape)
    outs = []
    for hh in range(2):
        a0 = HEAD_DIM if hh == 0 else 0
        hi, mid, lo = _split3(head_column(c_t, hh))
        aug = jnp.where((lane >= a0) & (lane < a0 + 3), 1.0,
                        jnp.where(lane == a0 + 3, hi,
                                  jnp.where(lane == a0 + 4, mid,
                                            jnp.where(lane == a0 + 5, lo, 0.0))))
        q_aug = jnp.where(_own_half(hh, q.shape), q, aug).astype(_MXU_DTYPE)

        s = lax.dot_general(q_aug, kaug_sc[hh, pl.ds(row0, tq), :], _NT,
                            preferred_element_type=_F32)
        s = jnp.where(_causal(tq), s, NEG_INF)
        m = jnp.max(s, axis=-1, keepdims=True)
        pr = jnp.exp(s - m)
        l = jnp.sum(pr, axis=-1, keepdims=True)
        acc = jnp.dot(pr.astype(_MXU_DTYPE), v_sc[pl.ds(row0, tq), :], preferred_element_type=_F32)

        def body(j, carry, q_aug=q_aug, hh=hh):
            r = pl.multiple_of(j * tq, tq)
            s = lax.dot_general(q_aug, kaug_sc[hh, pl.ds(r, tq), :], _NT,
                                preferred_element_type=_F32)
            return _flash_update(s, v_sc[pl.ds(r, tq), :], *carry)

        m, l, acc = lax.fori_loop(0, i, body, (m, l, acc))
        outs.append(acc / l)
    o = jnp.where(_own_half(0, outs[0].shape), outs[0], outs[1])
    o_ref[0] = (o * _silu(g_ref[0])).astype(o_ref.dtype)


def _fox(proj, bf_row, *, tq=256):
    b, t, _ = proj.shape
    return pl.pallas_call(
        functools.partial(_fox_kernel, tq=tq),
        grid=(b, 2, t // tq),
        in_specs=[pl.BlockSpec((1, tq, LANES), lambda bb, p, i: (bb, i, COL_FQ + p)),
                  pl.BlockSpec((1, t, LANES), lambda bb, p, i: (bb, 0, COL_FK + p)),
                  pl.BlockSpec((1, t, LANES), lambda bb, p, i: (bb, 0, COL_FV + p)),
                  pl.BlockSpec((1, t, LANES), lambda bb, p, i: (bb, 0, COL_FF)),
                  pl.BlockSpec((1, tq, LANES), lambda bb, p, i: (bb, i, COL_FG + p)),
                  pl.BlockSpec((1, LANES), lambda bb, p, i: (0, 0))],
        out_specs=pl.BlockSpec((1, tq, LANES), lambda bb, p, i: (bb, i, p)),
        out_shape=jax.ShapeDtypeStruct((b, t, GROUP_WIDTH), _MXU_DTYPE),
        scratch_shapes=[pltpu.VMEM((2, t, LANES), _MXU_DTYPE),
                        pltpu.VMEM((t, LANES), _MXU_DTYPE),
                        pltpu.VMEM((t, LANES), _F32)],
        compiler_params=pltpu.CompilerParams(
            dimension_semantics=("arbitrary", "arbitrary", "arbitrary"),
            vmem_limit_bytes=_VMEM_LIMIT),
        name="fox",
    )(proj, proj, proj, proj, proj, bf_row)


def _moba_block_means(kn, kmean_sc):
    kmean_sc[...] = jnp.zeros(kmean_sc.shape, _F32)
    for n in range(kn.shape[0] // MOBA_BLOCK):
        kmean_sc[n:n + 1, :] = jnp.mean(kn[n * MOBA_BLOCK:(n + 1) * MOBA_BLOCK], axis=0, keepdims=True)


def _moba_select(q_head, kmean, past):
    gate = lax.dot_general(q_head, kmean, _NT, precision=lax.Precision.HIGHEST,
                           preferred_element_type=_F32)
    lane_f = _lane_iota(gate.shape).astype(_F32)
    cand = jnp.where(past, gate, -jnp.inf)
    sel = jnp.zeros(gate.shape, _F32)
    for _ in range(MOBA_TOPK):
        mx = jnp.max(cand, axis=-1, keepdims=True)
        is_max = (cand == mx) & (mx > -jnp.inf)
        first = jnp.min(jnp.where(is_max, lane_f, float(LANES)), axis=-1, keepdims=True)
        pick = lane_f == first
        sel = jnp.where(pick, 1.0, sel)
        cand = jnp.where(pick, -jnp.inf, cand)
    return sel


def _moba_bounded_kernel(rb_ref, bound_ref, q_ref, k_ref, v_ref, g_ref, bias_ref, o_ref,
                         kaug_sc, vext_sc, kmean_sc):
    p = pl.program_id(1)
    i = pl.program_id(2)
    tq = q_ref.shape[1]
    blk = MOBA_BLOCK
    near = MOBA_WIDE_NEAR_TILES

    @pl.when(i == 0)
    def _prep():
        kn = k_ref[0]
        _moba_block_means(kn, kmean_sc)
        v = v_ref[0]
        lane = _lane_iota(kn.shape)
        row_blk = lax.broadcasted_iota(jnp.int32, kn.shape, 0) // blk
        for hh in range(2):
            a0 = HEAD_DIM if hh == 0 else 0
            aug = jnp.where((lane - a0 == row_blk) | (lane - a0 - 16 == row_blk)
                            | (lane - a0 == 32) | (lane - a0 == 33), 1.0, 0.0)
            own = _own_half(hh, kn.shape)
            kaug_sc[hh] = jnp.where(own, kn, aug).astype(_MXU_DTYPE)
            vext_sc[hh, :, 0:LANES] = jnp.where(own, v, 0.0).astype(_MXU_DTYPE)
            vext_sc[hh, :, LANES:2 * LANES] = jnp.where(lane == hh, 1.0, 0.0).astype(_MXU_DTYPE)

    qn = q_ref[0]
    row0 = pl.multiple_of(i * tq, tq)
    off = jnp.full((1, 1), -LOG2E, _F32) * bound_ref[0]
    off_hi = off.astype(_MXU_DTYPE).astype(_F32)
    blk_n = lax.broadcasted_iota(jnp.int32, (16, tq), 0)
    blk_f = blk_n.astype(_F32)
    q_blk = i * (tq // blk) + lax.broadcasted_iota(jnp.int32, (16, tq), 1) // blk
    past = blk_n < q_blk
    far = i - blk_n // 2 >= near
    sub8 = lax.broadcasted_iota(jnp.int32, (8, tq), 0)
    off_rows = jnp.where(sub8 == 0, off_hi, jnp.where(sub8 == 1, off - off_hi, 0.0))
    q_augs = []
    for hh in range(2):
        own = _own_half(hh, qn.shape)
        gate = lax.dot_general(kmean_sc[0:16, :], jnp.where(own, qn, 0.0), _NT,
                               precision=lax.Precision.HIGHEST, preferred_element_type=_F32)
        cand = jnp.where(past, gate, -jnp.inf)
        sel = jnp.zeros((16, tq), _F32)
        for _ in range(MOBA_TOPK):
            mx = jnp.max(cand, axis=0, keepdims=True)
            is_max = (cand == mx) & (mx > -jnp.inf)
            first = jnp.min(jnp.where(is_max, blk_f, float(LANES)), axis=0, keepdims=True)
            pick = blk_f == first
            sel = jnp.where(pick, 1.0, sel)
            cand = jnp.where(pick, -jnp.inf, cand)
        c_far = jnp.full((1, 1), LOG2E, _F32) * rb_ref[NUM_BUCKETS - 1, 2 * p + hh]
        far_hi = c_far.astype(_MXU_DTYPE).astype(_F32)
        pen = jnp.where((blk_n == q_blk) | (past & (sel != 0.0)), 0.0, NEG_INF)
        fields = jnp.concatenate([pen + jnp.where(far, far_hi, 0.0),
                                  jnp.where(far, c_far - far_hi, 0.0),
                                  off_rows,
                                  jnp.zeros((HEAD_DIM - 40, tq), _F32)], axis=0)
        blank = jnp.zeros((HEAD_DIM, tq), _F32)
        aug = jnp.concatenate([blank, fields] if hh == 0 else [fields, blank], axis=0).T
        q_augs.append(jnp.where(own, qn * (ATTN_SCALE * LOG2E), aug).astype(_MXU_DTYPE))

    def body(j, state, with_bias):
        r_prev, ps0, ps1, acc = state
        r = pl.multiple_of(j * tq, tq)
        bias = (lambda hh: bias_ref[hh, i - j]) if with_bias else None
        ps = _bounded_weights(q_augs, kaug_sc, r, bias, False)
        acc = _bounded_values([ps0, ps1], vext_sc, pl.multiple_of(r_prev, tq), acc)
        return r, ps[0], ps[1], acc

    ps = _bounded_weights(q_augs, kaug_sc, row0, lambda hh: bias_ref[hh, 0], True)
    state = (row0, ps[0], ps[1], jnp.zeros((tq, 2 * LANES), _F32))
    near_lo = jnp.maximum(i - (near - 1), 0)
    state = lax.fori_loop(near_lo, i, functools.partial(body, with_bias=True), state)
    state = lax.fori_loop(0, near_lo, functools.partial(body, with_bias=False), state)
    acc = _bounded_values([state[1], state[2]], vext_sc, pl.multiple_of(state[0], tq), state[3])
    den = acc[:, LANES:]
    l = jnp.where(_own_half(0, (tq, LANES)), _lane_column(den, 0), _lane_column(den, 1))
    o_ref[0] = (acc[:, :LANES] / l * _silu(g_ref[0])).astype(o_ref.dtype)


def _moba_bounded(proj, rel_bias, bias_tiles, bound):
    b, t, _ = proj.shape
    tq = MOBA_WIDE
    assert t % tq == 0 and t // MOBA_BLOCK <= 16
    return pl.pallas_call(
        _moba_bounded_kernel,
        grid=(b, 2, t // tq),
        in_specs=[pl.BlockSpec(memory_space=pltpu.SMEM),
                  pl.BlockSpec(memory_space=pltpu.SMEM),
                  pl.BlockSpec((1, tq, LANES), lambda bb, p, i: (bb, i, COL_MQ + p)),
                  pl.BlockSpec((1, t, LANES), lambda bb, p, i: (bb, 0, COL_MK + p)),
                  pl.BlockSpec((1, t, LANES), lambda bb, p, i: (bb, 0, COL_MV + p)),
                  pl.BlockSpec((1, tq, LANES), lambda bb, p, i: (bb, i, COL_MG + p)),
                  pl.BlockSpec((2, MOBA_WIDE_NEAR_TILES, tq, tq), lambda bb, p, i: (p, 0, 0, 0))],
        out_specs=pl.BlockSpec((1, tq, LANES), lambda bb, p, i: (bb, i, p)),
        out_shape=jax.ShapeDtypeStruct((b, t, GROUP_WIDTH), _MXU_DTYPE),
        scratch_shapes=[pltpu.VMEM((2, t, LANES), _MXU_DTYPE),
                        pltpu.VMEM((2, t, 2 * LANES), _MXU_DTYPE),
                        pltpu.VMEM((LANES, LANES), _F32)],
        compiler_params=pltpu.CompilerParams(
            dimension_semantics=("arbitrary", "arbitrary", "arbitrary"),
            vmem_limit_bytes=_VMEM_LIMIT),
        name="moba_bounded",
    )(rel_bias, bound, proj, proj, proj, proj, bias_tiles)


def _moba_kernel(rb_ref, q_ref, k_ref, v_ref, g_ref, bias_ref, o_ref,
                 kaug_sc, v_sc, kmean_sc):
    p = pl.program_id(1)
    i = pl.program_id(2)
    t_len = k_ref.shape[1]
    blk = MOBA_BLOCK
    nblk = t_len // blk
    near = MOBA_NEAR_TILES

    @pl.when(i == 0)
    def _prep():
        kn = k_ref[0]
        _moba_block_means(kn, kmean_sc)
        lane = _lane_iota(kn.shape)
        row_blk = lax.broadcasted_iota(jnp.int32, kn.shape, 0) // blk
        for hh in range(2):
            a0 = HEAD_DIM if hh == 0 else 0
            onehot = jnp.where((lane - a0 == row_blk) | (lane - a0 - 16 == row_blk), 1.0, 0.0)
            kaug_sc[hh] = jnp.where(_own_half(hh, kn.shape), kn, onehot).astype(_MXU_DTYPE)
        v_sc[...] = v_ref[0].astype(_MXU_DTYPE)

    qn = q_ref[0]
    row0 = pl.multiple_of(i * blk, blk)
    lane = _lane_iota(qn.shape)
    lane_f = lane.astype(_F32)
    past = lane < i
    outs = []
    for hh in range(2):
        own = _own_half(hh, qn.shape)
        sel = _moba_select(jnp.where(own, qn, 0.0), kmean_sc[...], past)
        c_far = jnp.full((1, LANES), rb_ref[NUM_BUCKETS - 1, 2 * p + hh], _F32)
        far_hi = c_far.astype(_MXU_DTYPE).astype(_F32)
        far_lo = c_far - far_hi
        pen = jnp.where(past & (sel == 0.0), NEG_INF, 0.0)
        aug = jnp.where(lane < 16, pen + jnp.where(i - lane >= near, far_hi, 0.0),
                        jnp.where((lane < 32) & (i - (lane - 16) >= near), far_lo, 0.0))
        if hh == 0:
            aug = pltpu.roll(aug, HEAD_DIM, axis=1)
        q_aug = jnp.where(own, qn * ATTN_SCALE, aug).astype(_MXU_DTYPE)

        def scores(r, q_aug=q_aug, hh=hh):
            return lax.dot_general(q_aug, kaug_sc[hh, pl.ds(r, blk), :], _NT,
                                   preferred_element_type=_F32)

        s = scores(row0) + bias_ref[hh, 0]
        s = jnp.where(_causal(blk), s, NEG_INF)
        m = jnp.max(s, axis=-1, keepdims=True)
        pr = jnp.exp(s - m)
        l = jnp.sum(pr, axis=-1, keepdims=True)
        acc = jnp.dot(pr.astype(_MXU_DTYPE), v_sc[pl.ds(row0, blk), :], preferred_element_type=_F32)

        def near_body(j, carry, hh=hh, scores=scores):
            r = pl.multiple_of(j * blk, blk)
            return _flash_update(scores(r) + bias_ref[hh, i - j], v_sc[pl.ds(r, blk), :], *carry)

        def far_body(j, carry, scores=scores):
            r = pl.multiple_of(j * blk, blk)
            return _flash_update(scores(r), v_sc[pl.ds(r, blk), :], *carry)

        near_lo = jnp.maximum(i - (near - 1), 0)
        carry = lax.fori_loop(near_lo, i, near_body, (m, l, acc))
        m, l, acc = lax.fori_loop(0, near_lo, far_body, carry)
        outs.append(acc / l)
    o = jnp.where(_own_half(0, outs[0].shape), outs[0], outs[1])
    o_ref[0] = (o * _silu(g_ref[0])).astype(o_ref.dtype)


def _moba(proj, rel_bias, bias_tiles):
    b, t, _ = proj.shape
    blk = MOBA_BLOCK
    assert t % blk == 0 and t // blk <= 16
    return pl.pallas_call(
        _moba_kernel,
        grid=(b, 2, t // blk),
        in_specs=[pl.BlockSpec(memory_space=pltpu.SMEM),
                  pl.BlockSpec((1, blk, LANES), lambda bb, p, i: (bb, i, COL_MQ + p)),
                  pl.BlockSpec((1, t, LANES), lambda bb, p, i: (bb, 0, COL_MK + p)),
                  pl.BlockSpec((1, t, LANES), lambda bb, p, i: (bb, 0, COL_MV + p)),
                  pl.BlockSpec((1, blk, LANES), lambda bb, p, i: (bb, i, COL_MG + p)),
                  pl.BlockSpec((2, MOBA_NEAR_TILES, blk, blk), lambda bb, p, i: (p, 0, 0, 0))],
        out_specs=pl.BlockSpec((1, blk, LANES), lambda bb, p, i: (bb, i, p)),
        out_shape=jax.ShapeDtypeStruct((b, t, GROUP_WIDTH), _MXU_DTYPE),
        scratch_shapes=[pltpu.VMEM((2, t, LANES), _MXU_DTYPE),
                        pltpu.VMEM((t, LANES), _MXU_DTYPE),
                        pltpu.VMEM((LANES, LANES), _F32)],
        compiler_params=pltpu.CompilerParams(
            dimension_semantics=("arbitrary", "arbitrary", "arbitrary"),
            vmem_limit_bytes=_VMEM_LIMIT),
        name="moba",
    )(rel_bias, proj, proj, proj, proj, bias_tiles)


def _sb_kernel(q_ref, k_ref, v_ref, g_ref, o_ref, k_sc, v_sc, *, tq):
    i = pl.program_id(2)

    @pl.when(i == 0)
    def _prep():
        k_sc[...] = k_ref[0].astype(_MXU_DTYPE)
        v = v_ref[0]
        for hh in range(2):
            v_sc[hh] = jnp.where(_own_half(hh, v.shape), v, 0.0).astype(_MXU_DTYPE)

    q = q_ref[0] * (ATTN_SCALE * LOG2E)
    q_heads = [jnp.where(_own_half(hh, q.shape), q, 0.0).astype(_MXU_DTYPE) for hh in range(2)]
    row0 = pl.multiple_of(i * tq, tq)
    strict = (lax.broadcasted_iota(jnp.int32, (tq, tq), 0)
              > lax.broadcasted_iota(jnp.int32, (tq, tq), 1))
    after = strict.astype(_MXU_DTYPE)
    sign_bit = jnp.uint32(0x80000000)

    def pair(j, runs, diagonal):
        has_prev = (j >= 1).astype(_F32)
        rows = [pl.multiple_of(j * tq, tq), pl.multiple_of(jnp.maximum(j - 1, 0) * tq, tq)]
        chains = [(t, hh) for t in range(2) for hh in range(2)]
        masked = [diagonal and t == 0 for t, _ in chains]
        zs = [lax.dot_general(q_heads[hh], k_sc[pl.ds(rows[t], tq), :], _NT,
                              preferred_element_type=_F32) for t, hh in chains]
        drops = []
        for z, msk in zip(zs, masked):
            neg_abs = lax.bitcast_convert_type(lax.bitcast_convert_type(z, jnp.uint32) | sign_bit, _F32)
            drop = jnp.maximum(z, 0.0) + jnp.log(1.0 + jnp.exp2(neg_abs)) * LOG2E
            drops.append(jnp.where(strict, drop, 0.0) if msk else drop)
        laters = []
        for drop in drops:
            hi = drop.astype(_MXU_DTYPE)
            lo = (drop - hi.astype(_F32)).astype(_MXU_DTYPE)
            laters.append(jnp.dot(hi, after, preferred_element_type=_F32)
                          + jnp.dot(lo, after, preferred_element_type=_F32))
        pvs = []
        for (t, hh), z, drop, later, msk in zip(chains, zs, drops, laters, masked):
            w = jnp.exp2(z - drop - later)
            if msk:
                w = jnp.where(strict, w, 0.0)
            pvs.append(jnp.dot(w.astype(_MXU_DTYPE), v_sc[hh, pl.ds(rows[t], tq), :],
                               preferred_element_type=_F32))
        sums = [jnp.sum(drop, axis=-1, keepdims=True) for drop in drops]
        new_runs, out = [], None
        for hh in range(2):
            run_b = runs[hh] - sums[hh]
            pv = pvs[hh] * jnp.exp2(runs[hh]) + pvs[2 + hh] * (jnp.exp2(run_b) * has_prev)
            out = pv if out is None else out + pv
            new_runs.append(run_b - sums[2 + hh] * has_prev)
        return new_runs, out

    def alive(runs):
        return jnp.max(jnp.maximum(runs[0], runs[1])) >= EXP_ZERO_CUTOFF * LOG2E

    zero = jnp.zeros((tq, 1), _F32)
    runs, acc = pair(i, [zero, zero], True)

    def body(state):
        j, _, run0, run1, acc = state
        runs, out = pair(j, [run0, run1], False)
        return j - 2, alive(runs), runs[0], runs[1], acc + out

    state = lax.while_loop(lambda state: (state[0] >= 0) & state[1], body,
                           (i - 2, alive(runs), runs[0], runs[1], acc))
    o_ref[0] = (state[4] * _silu(g_ref[0])).astype(o_ref.dtype)


def _sb(proj, *, tq=256):
    b, t, _ = proj.shape
    return pl.pallas_call(
        functools.partial(_sb_kernel, tq=tq),
        grid=(b, 2, t // tq),
        in_specs=[pl.BlockSpec((1, tq, LANES), lambda bb, p, i: (bb, i, COL_SQ + p)),
                  pl.BlockSpec((1, t, LANES), lambda bb, p, i: (bb, 0, COL_SK + p)),
                  pl.BlockSpec((1, t, LANES), lambda bb, p, i: (bb, 0, COL_SV + p)),
                  pl.BlockSpec((1, tq, LANES), lambda bb, p, i: (bb, i, COL_SG + p))],
        out_specs=pl.BlockSpec((1, tq, LANES), lambda bb, p, i: (bb, i, p)),
        out_shape=jax.ShapeDtypeStruct((b, t, GROUP_WIDTH), _MXU_DTYPE),
        scratch_shapes=[pltpu.VMEM((t, LANES), _MXU_DTYPE),
                        pltpu.VMEM((2, t, LANES), _MXU_DTYPE)],
        compiler_params=pltpu.CompilerParams(
            dimension_semantics=("arbitrary", "arbitrary", "arbitrary"),
            vmem_limit_bytes=_VMEM_LIMIT),
        name="stickbreak",
    )(proj, proj, proj, proj)


def _swa_kernel(sink_ref, q_ref, k_ref, v_ref, g_ref, bias_ref, o_ref,
                k_sc, v_sc, *, tq):
    kv = pl.program_id(1)
    i = pl.program_id(2)
    t_len = k_ref.shape[1]
    w = SWA_WINDOW

    @pl.when(i == 0)
    def _prep():
        kn = k_ref[0]
        v = v_ref[0]
        keep = (_lane_iota(kn.shape) < HEAD_DIM) == (kv == 0)
        k_sc[0:w, :] = jnp.zeros((w, LANES), _MXU_DTYPE)
        v_sc[0:w, :] = jnp.zeros((w, LANES), _MXU_DTYPE)
        k_sc[w:w + t_len, :] = jnp.where(keep, kn, pltpu.roll(kn, HEAD_DIM, axis=1)).astype(_MXU_DTYPE)
        v_sc[w:w + t_len, :] = jnp.where(keep, v, pltpu.roll(v, HEAD_DIM, axis=1)).astype(_MXU_DTYPE)

    q = q_ref[0] * ATTN_SCALE
    qi = lax.broadcasted_iota(jnp.int32, (w, 2 * w), 0)
    kj = lax.broadcasted_iota(jnp.int32, (w, 2 * w), 1)
    dist = qi + w - kj
    in_window = (dist >= 0) & (dist < w)
    for u in range(tq // w):
        nb = i * (tq // w) + u
        r = pl.multiple_of(nb * w, w)
        k_t = k_sc[pl.ds(r, 2 * w), :]
        v_t = v_sc[pl.ds(r, 2 * w), :]
        allowed = in_window & (kj + (nb - 1) * w >= 0)
        q_u = q[u * w:(u + 1) * w]
        outs = []
        for g in range(2):
            q_h = jnp.where(_own_half(g, q_u.shape), q_u, 0.0).astype(_MXU_DTYPE)
            s = lax.dot_general(q_h, k_t, _NT, preferred_element_type=_F32)
            s = jnp.where(allowed, s + bias_ref[g], NEG_INF)
            sink = sink_ref[2 * kv + g]
            m = jnp.maximum(jnp.max(s, axis=-1, keepdims=True), sink)
            e = jnp.exp(s - m)
            den = jnp.sum(e, axis=-1, keepdims=True) + jnp.exp(sink - m)
            outs.append(jnp.dot(e.astype(_MXU_DTYPE), v_t, preferred_element_type=_F32) / den)
        o = jnp.where(_own_half(0, outs[0].shape), outs[0], outs[1])
        o_ref[0, u * w:(u + 1) * w, :] = (o * _silu(g_ref[0, u * w:(u + 1) * w, :])).astype(o_ref.dtype)


def _swa(proj, sinks, bias_tiles, *, tq=512):
    b, t, _ = proj.shape
    w = SWA_WINDOW
    return pl.pallas_call(
        functools.partial(_swa_kernel, tq=tq),
        grid=(b, 2, t // tq),
        in_specs=[pl.BlockSpec(memory_space=pltpu.SMEM),
                  pl.BlockSpec((1, tq, LANES), lambda bb, kv, i: (bb, i, COL_WQ + kv)),
                  pl.BlockSpec((1, t, LANES), lambda bb, kv, i: (bb, 0, COL_WK)),
                  pl.BlockSpec((1, t, LANES), lambda bb, kv, i: (bb, 0, COL_WV)),
                  pl.BlockSpec((1, tq, LANES), lambda bb, kv, i: (bb, i, COL_WG + kv)),
                  pl.BlockSpec((2, w, 2 * w), lambda bb, kv, i: (kv, 0, 0))],
        out_specs=pl.BlockSpec((1, tq, LANES), lambda bb, kv, i: (bb, i, kv)),
        out_shape=jax.ShapeDtypeStruct((b, t, GROUP_WIDTH), _MXU_DTYPE),
        scratch_shapes=[pltpu.VMEM((t + w, LANES), _MXU_DTYPE),
                        pltpu.VMEM((t + w, LANES), _MXU_DTYPE)],
        compiler_params=pltpu.CompilerParams(
            dimension_semantics=("arbitrary", "arbitrary", "arbitrary"),
            vmem_limit_bytes=_VMEM_LIMIT),
        name="swa",
    )(sinks, proj, proj, proj, proj, bias_tiles)


def _outproj_kernel(x_ref, ya_ref, yb_ref, yc_ref, yd_ref, w_ref, o_ref):
    acc = x_ref[...]
    for g, y_ref in enumerate((ya_ref, yb_ref, yc_ref, yd_ref)):
        acc = acc + jnp.dot(y_ref[...], w_ref[g * GROUP_WIDTH:(g + 1) * GROUP_WIDTH, :],
                            preferred_element_type=_F32)
    o_ref[...] = acc


def _outproj(x2d, ys, w):
    m, d = x2d.shape
    tm = 512
    y_spec = pl.BlockSpec((tm, GROUP_WIDTH), lambda i: (i, 0))
    return pl.pallas_call(
        _outproj_kernel,
        grid=(m // tm,),
        in_specs=[pl.BlockSpec((tm, d), lambda i: (i, 0)), y_spec, y_spec, y_spec, y_spec,
                  pl.BlockSpec(w.shape, lambda i: (0, 0))],
        out_specs=pl.BlockSpec((tm, d), lambda i: (i, 0)),
        out_shape=jax.ShapeDtypeStruct((m, d), _F32),
        compiler_params=pltpu.CompilerParams(
            dimension_semantics=("arbitrary",), vmem_limit_bytes=_VMEM_LIMIT),
        name="outproj",
    )(x2d, *ys, w)


def _rearrange_w_in(w_in):
    off = 3 * GROUP_WIDTH
    main = jnp.concatenate([w_in[..., :off], w_in[..., off + FOX_FORGET_COLS:]], axis=-1)
    ff = w_in[..., off:off + FOX_FORGET_COLS]
    pad = PROJ_WIDTH - main.shape[-1] - FOX_FORGET_COLS
    return jnp.concatenate([main, ff, jnp.zeros(w_in.shape[:-1] + (pad,), w_in.dtype)], axis=-1)


def _qk_logit_bound(qk_gain):
    g = jnp.abs(qk_gain.astype(_F32))
    return ATTN_SCALE * HEAD_DIM * jnp.max(g[0]) * jnp.max(g[1])


def _qk_gain_row(fox_g, moba_g, swa_g):
    def tiles(g, n):
        return jnp.tile(g.astype(_F32), 2 * n)
    row = jnp.zeros((PROJ_WIDTH,), _F32)
    for col, g, n in ((COL_FQ, fox_g[0], 2), (COL_FK, fox_g[1], 2),
                      (COL_MQ, moba_g[0], 2), (COL_MK, moba_g[1], 2),
                      (COL_WQ, swa_g[0], 2), (COL_WK, swa_g[1], 1)):
        row = lax.dynamic_update_slice(row, tiles(g, n), (col * LANES,))
    return row[None, :]


def kernel(x, norm_gain, w_in, b_forget, fox_qk_gain, moba_qk_gain, swa_qk_gain, sinks, w_out, rel_bias):
    b, t, d = x.shape
    depth = w_in.shape[0]
    w_in_r = _rearrange_w_in(w_in).astype(_MXU_DTYPE)
    w_out_c = w_out.astype(_MXU_DTYPE)
    moba_bias, moba_wide_bias, swa_bias = _bias_tiles(rel_bias)
    moba_bias_max = jnp.max(jnp.abs(rel_bias[:, :rel_bias.shape[1] // 2].astype(_F32)))
    x2d = x.reshape(b * t, d)
    for layer in range(depth):
        qk_gain_row = _qk_gain_row(fox_qk_gain[layer], moba_qk_gain[layer], swa_qk_gain[layer])
        proj = _inproj(x2d, norm_gain[layer][None, :], w_in_r[layer], qk_gain_row)
        proj = proj.reshape(b, t, PROJ_WIDTH)
        bf_row = jnp.pad(b_forget[layer], (0, LANES - FOX_FORGET_COLS))[None, :]
        fox_bound = _qk_logit_bound(fox_qk_gain[layer])
        y_fox = lax.cond(2.0 * fox_bound <= BOUNDED_SOFTMAX_RANGE,
                         lambda: _fox_bounded(proj, bf_row, fox_bound.reshape(1)),
                         lambda: _fox(proj, bf_row))
        moba_bound = _qk_logit_bound(moba_qk_gain[layer]) + moba_bias_max
        y_moba = lax.cond(2.0 * moba_bound <= BOUNDED_SOFTMAX_RANGE,
                          lambda: _moba_bounded(proj, rel_bias, moba_wide_bias,
                                                moba_bound.reshape(1)),
                          lambda: _moba(proj, rel_bias, moba_bias))
        y_sb = _sb(proj)
        y_swa = _swa(proj, sinks[layer], swa_bias)
        ys = [y.reshape(b * t, GROUP_WIDTH) for y in (y_fox, y_moba, y_sb, y_swa)]
        x2d = _outproj(x2d, ys, w_out_c[layer])
    return x2d.reshape(b, t, d)
```

```python
import functools
import math

import jax
import jax.numpy as jnp
from jax import lax
from jax.experimental import pallas as pl
from jax.experimental.pallas import tpu as pltpu

HEAD_DIM = 64
LANES = 128
GROUP_WIDTH = 256
MOBA_BLOCK = 256
MOBA_TOPK = 3
SWA_WINDOW = 128
NUM_BUCKETS = 32
REL_MAX_DISTANCE = 1024
RMS_EPS = 1e-6
NEG_INF = -1e30
ATTN_SCALE = HEAD_DIM ** -0.5
LOG2E = math.log2(math.e)
BOUNDED_SOFTMAX_RANGE = 60.0
MOBA_NEAR_TILES = REL_MAX_DISTANCE // MOBA_BLOCK + 1
MOBA_WIDE = 2 * MOBA_BLOCK
MOBA_WIDE_NEAR_TILES = REL_MAX_DISTANCE // MOBA_WIDE + 1
FOX_FORGET_COLS = 4
EXP_ZERO_CUTOFF = -104.0

COL_FQ, COL_FK, COL_FV, COL_FG = 0, 2, 4, 6
COL_MQ, COL_MK, COL_MV, COL_MG = 8, 10, 12, 14
COL_SQ, COL_SK, COL_SV, COL_SG = 16, 18, 20, 22
COL_WQ, COL_WK, COL_WV, COL_WG = 24, 26, 27, 28
COL_FF = 30
PROJ_WIDTH = 32 * LANES
QK_NORM_COLS = (COL_FQ, COL_FQ + 1, COL_FK, COL_FK + 1, COL_MQ, COL_MQ + 1, COL_MK, COL_MK + 1,
                COL_WQ, COL_WQ + 1, COL_WK)

_MXU_DTYPE = jnp.bfloat16
_F32 = jnp.float32
_VMEM_LIMIT = 48 * 1024 * 1024

_NT = (((1,), (1,)), ((), ()))


def _lane_iota(shape):
    return lax.broadcasted_iota(jnp.int32, shape, len(shape) - 1)


def _pair_rms(x, gain_row):
    low = _lane_iota(x.shape) < HEAD_DIM
    sq = x * x
    ms_lo = jnp.sum(jnp.where(low, sq, 0.0), axis=-1, keepdims=True)
    ms_hi = jnp.sum(jnp.where(low, 0.0, sq), axis=-1, keepdims=True)
    ms = jnp.where(low, ms_lo, ms_hi) * (1.0 / HEAD_DIM)
    return x * lax.rsqrt(ms + RMS_EPS) * gain_row


def _silu(g):
    return g * (1.0 / (1.0 + jnp.exp(-g)))


def _split3(x):
    hi = x.astype(_MXU_DTYPE).astype(_F32)
    r = x - hi
    mid = r.astype(_MXU_DTYPE).astype(_F32)
    return hi, mid, r - mid


def _own_half(hh, shape):
    lane = _lane_iota(shape)
    return (lane < HEAD_DIM) if hh == 0 else (lane >= HEAD_DIM)


def _flash_update(s, v_tile, m, l, acc):
    m_new = jnp.maximum(m, jnp.max(s, axis=-1, keepdims=True))
    alpha = jnp.exp(m - m_new)
    p = jnp.exp(s - m_new)
    l = alpha * l + jnp.sum(p, axis=-1, keepdims=True)
    acc = alpha * acc + jnp.dot(p.astype(_MXU_DTYPE), v_tile, preferred_element_type=_F32)
    return m_new, l, acc


def _causal(tq):
    return (lax.broadcasted_iota(jnp.int32, (tq, tq), 0)
            >= lax.broadcasted_iota(jnp.int32, (tq, tq), 1))


def _inproj_kernel(x_ref, gain_ref, w_ref, qkg_ref, o_ref, *, tn):
    x = x_ref[...]
    ms = jnp.mean(x * x, axis=-1, keepdims=True)
    hn = (x * lax.rsqrt(ms + RMS_EPS) * gain_ref[...]).astype(_MXU_DTYPE)
    for n in range(o_ref.shape[1] // tn):
        acc = jnp.dot(hn, w_ref[:, n * tn:(n + 1) * tn], preferred_element_type=_F32)
        for c in range(n * tn // LANES, (n + 1) * tn // LANES):
            tile = acc[:, c * LANES - n * tn:(c + 1) * LANES - n * tn]
            if c in QK_NORM_COLS:
                tile = _pair_rms(tile, qkg_ref[:, c * LANES:(c + 1) * LANES])
            o_ref[:, c * LANES:(c + 1) * LANES] = tile


def _inproj(x2d, gain_row, w_all, layer, qk_gain_row):
    m, d = x2d.shape
    n = w_all.shape[2]
    tm = 256
    return pl.pallas_call(
        functools.partial(_inproj_kernel, tn=1024),
        grid=(m // tm,),
        in_specs=[pl.BlockSpec((tm, d), lambda i: (i, 0)),
                  pl.BlockSpec((1, d), lambda i: (0, 0)),
                  pl.BlockSpec((None, d, n), lambda i: (layer, 0, 0)),
                  pl.BlockSpec((1, n), lambda i: (0, 0))],
        out_specs=pl.BlockSpec((tm, n), lambda i: (i, 0)),
        out_shape=jax.ShapeDtypeStruct((m, n), _F32),
        compiler_params=pltpu.CompilerParams(
            dimension_semantics=("arbitrary",), vmem_limit_bytes=_VMEM_LIMIT),
        name="inproj",
    )(x2d, gain_row, w_all, qk_gain_row)


def _rel_bucket(dist):
    max_exact = NUM_BUCKETS // 2
    d = jnp.maximum(dist, 0)
    log_ratio = (jnp.log(jnp.maximum(d, 1).astype(_F32) / max_exact)
                 / math.log(REL_MAX_DISTANCE / max_exact))
    large = max_exact + (log_ratio * (NUM_BUCKETS - max_exact)).astype(jnp.int32)
    large = jnp.minimum(large, NUM_BUCKETS - 1)
    return jnp.where(d < max_exact, d, large)


def _bias_kernel(rb_ref, bm_ref, bw_ref, bs_ref, om_ref, ow_ref, os_ref, *, n_heads):
    h = pl.program_id(0)

    def lookup(buckets, col):
        acc = jnp.zeros(buckets.shape, _F32)
        for k in range(NUM_BUCKETS):
            acc = jnp.where(buckets == k, rb_ref[k, col], acc)
        return acc

    for d in range(bm_ref.shape[0]):
        om_ref[0, d] = lookup(bm_ref[d], h)
    for d in range(bw_ref.shape[0]):
        ow_ref[0, d] = lookup(bw_ref[d], h) * LOG2E
    os_ref[0] = lookup(bs_ref[...], n_heads + h)


def _bias_tiles(rel_bias):
    n_heads = rel_bias.shape[1] // 2

    def toeplitz(size, count):
        i = jnp.arange(size)[:, None]
        j = jnp.arange(size)[None, :]
        return jnp.stack([_rel_bucket(d * size + i - j) for d in range(count)])

    bm = toeplitz(MOBA_BLOCK, MOBA_NEAR_TILES)
    bw = toeplitz(MOBA_WIDE, MOBA_WIDE_NEAR_TILES)
    w = SWA_WINDOW
    bs = _rel_bucket(jnp.arange(w)[:, None] + w - jnp.arange(2 * w)[None, :])
    return pl.pallas_call(
        functools.partial(_bias_kernel, n_heads=n_heads),
        grid=(n_heads,),
        in_specs=[pl.BlockSpec(memory_space=pltpu.SMEM),
                  pl.BlockSpec(bm.shape, lambda h: (0, 0, 0)),
                  pl.BlockSpec(bw.shape, lambda h: (0, 0, 0)),
                  pl.BlockSpec(bs.shape, lambda h: (0, 0))],
        out_specs=[pl.BlockSpec((1,) + bm.shape, lambda h: (h, 0, 0, 0)),
                   pl.BlockSpec((1,) + bw.shape, lambda h: (h, 0, 0, 0)),
                   pl.BlockSpec((1,) + bs.shape, lambda h: (h, 0, 0))],
        out_shape=[jax.ShapeDtypeStruct((n_heads,) + bm.shape, _F32),
                   jax.ShapeDtypeStruct((n_heads,) + bw.shape, _F32),
                   jax.ShapeDtypeStruct((n_heads,) + bs.shape, _F32)],
        compiler_params=pltpu.CompilerParams(
            dimension_semantics=("arbitrary",), vmem_limit_bytes=_VMEM_LIMIT),
        name="bias_tiles",
    )(rel_bias, bm, bw, bs)


def _forget_cumsum(ff_ref, bf_ref, c_sc):
    t_len = c_sc.shape[0]
    ff = ff_ref[0] + bf_ref[...]
    log_f = jnp.minimum(ff, 0.0) - jnp.log(1.0 + jnp.exp(-jnp.abs(ff)))
    parts = [part.astype(_MXU_DTYPE) for part in _split3(log_f)]
    ch = 256
    tri = (lax.broadcasted_iota(jnp.int32, (ch, ch), 0)
           >= lax.broadcasted_iota(jnp.int32, (ch, ch), 1)).astype(_MXU_DTYPE)
    carry = jnp.zeros((1, LANES), _F32)
    for r in range(t_len // ch):
        inc = carry
        for part in reversed(parts):
            inc = inc + jnp.dot(tri, part[r * ch:(r + 1) * ch], preferred_element_type=_F32)
        c_sc[r * ch:(r + 1) * ch, :] = inc
        carry = inc[ch - 1:ch, :]


def _lane_column(x, col):
    return jnp.sum(jnp.where(_lane_iota(x.shape) == col, x, 0.0), axis=-1, keepdims=True)


def _lane_fields(shape, a0, fields):
    lane = _lane_iota(shape)
    out = jnp.zeros(shape, _F32)
    for n, f in enumerate(fields):
        out = jnp.where(lane == a0 + n, f, out)
    return out


def _bounded_weights(q_augs, kaug_sc, r, bias, diagonal):
    tq = q_augs[0].shape[0]
    ps = []
    for hh in range(2):
        s = lax.dot_general(q_augs[hh], kaug_sc[hh, pl.ds(r, tq), :], _NT,
                            preferred_element_type=_F32)
        if bias is not None:
            s = s + bias(hh)
        if diagonal:
            s = jnp.where(_causal(tq), s, NEG_INF)
        ps.append(jnp.exp2(s).astype(_MXU_DTYPE))
    return ps


def _bounded_values(ps, vext_sc, r, acc):
    tq = ps[0].shape[1]
    for hh in range(2):
        acc = acc + jnp.dot(ps[hh], vext_sc[hh, pl.ds(r, tq), :], preferred_element_type=_F32)
    return acc


def _fox_bounded_kernel(bound_ref, q_ref, k_ref, v_ref, ff_ref, g_ref, bf_ref, o_ref,
                        kaug_sc, vext_sc, c_sc):
    p = pl.program_id(1)
    i = pl.program_id(2)
    tq = q_ref.shape[1]

    @pl.when(i == 0)
    def _prep():
        pl.when(p == 0)(lambda: _forget_cumsum(ff_ref, bf_ref, c_sc))
        kn = k_ref[0]
        c_all = c_sc[...] * LOG2E
        v = v_ref[0]
        lane = _lane_iota(v.shape)
        for hh in range(2):
            a0 = HEAD_DIM if hh == 0 else 0
            hi, mid, lo = _split3(-_lane_column(c_all, 2 * p + hh))
            aug = _lane_fields(kn.shape, a0, [hi, mid, lo, 1.0, 1.0, 1.0, 1.0, 1.0])
            own = _own_half(hh, kn.shape)
            kaug_sc[hh] = jnp.where(own, kn, aug).astype(_MXU_DTYPE)
            vext_sc[hh, :, 0:LANES] = jnp.where(own, v, 0.0).astype(_MXU_DTYPE)
            vext_sc[hh, :, LANES:2 * LANES] = jnp.where(lane == hh, 1.0, 0.0).astype(_MXU_DTYPE)

    q = q_ref[0] * (ATTN_SCALE * LOG2E)
    row0 = pl.multiple_of(i * tq, tq)
    c_t = c_sc[pl.ds(row0, tq), :] * LOG2E
    off = jnp.full((1, 1), -LOG2E, _F32) * bound_ref[0]
    off_hi = off.astype(_MXU_DTYPE).astype(_F32)
    q_augs = []
    for hh in range(2):
        a0 = HEAD_DIM if hh == 0 else 0
        hi, mid, lo = _split3(_lane_column(c_t, 2 * p + hh))
        aug = _lane_fields(q.shape, a0, [1.0, 1.0, 1.0, hi, mid, lo, off_hi, off - off_hi])
        q_augs.append(jnp.where(_own_half(hh, q.shape), q, aug).astype(_MXU_DTYPE))

    lane = _lane_iota((1, LANES))
    pair_lanes = (lane == 2 * p) | (lane == 2 * p + 1)
    c_first = c_sc[pl.ds(row0, 1), :]
    cutoff = EXP_ZERO_CUTOFF - 2.0 * bound_ref[0]

    def live(j):
        c_last = c_sc[pl.ds(jnp.maximum(j, 0) * tq + (tq - 1), 1), :]
        gap = jnp.max(jnp.where(pair_lanes, c_first - c_last, -jnp.inf))
        return (j >= 0) & (gap >= cutoff)

    def body(state):
        j, _, r_prev, ps0, ps1, acc = state
        r = pl.multiple_of(j * tq, tq)
        ps = _bounded_weights(q_augs, kaug_sc, r, None, False)
        acc = _bounded_values([ps0, ps1], vext_sc, pl.multiple_of(r_prev, tq), acc)
        return j - 1, live(j - 1), r, ps[0], ps[1], acc

    ps = _bounded_weights(q_augs, kaug_sc, row0, None, True)
    state = lax.while_loop(lambda state: state[1], body,
                           (i - 1, live(i - 1), row0, ps[0], ps[1],
                            jnp.zeros((tq, 2 * LANES), _F32)))
    acc = _bounded_values([state[3], state[4]], vext_sc, pl.multiple_of(state[2], tq), state[5])
    den = acc[:, LANES:]
    l = jnp.where(_own_half(0, (tq, LANES)), _lane_column(den, 0), _lane_column(den, 1))
    o_ref[0] = (acc[:, :LANES] / l * _silu(g_ref[0])).astype(o_ref.dtype)


def _fox_bounded(proj, bf_row, bound, *, tq=512):
    b, t, _ = proj.shape
    return pl.pallas_call(
        _fox_bounded_kernel,
        grid=(b, 2, t // tq),
        in_specs=[pl.BlockSpec(memory_space=pltpu.SMEM),
                  pl.BlockSpec((1, tq, LANES), lambda bb, p, i: (bb, i, COL_FQ + p)),
                  pl.BlockSpec((1, t, LANES), lambda bb, p, i: (bb, 0, COL_FK + p)),
                  pl.BlockSpec((1, t, LANES), lambda bb, p, i: (bb, 0, COL_FV + p)),
                  pl.BlockSpec((1, t, LANES), lambda bb, p, i: (bb, 0, COL_FF)),
                  pl.BlockSpec((1, tq, LANES), lambda bb, p, i: (bb, i, COL_FG + p)),
                  pl.BlockSpec((1, LANES), lambda bb, p, i: (0, 0))],
        out_specs=pl.BlockSpec((1, tq, LANES), lambda bb, p, i: (bb, i, p)),
        out_shape=jax.ShapeDtypeStruct((b, t, GROUP_WIDTH), _MXU_DTYPE),
        scratch_shapes=[pltpu.VMEM((2, t, LANES), _MXU_DTYPE),
                        pltpu.VMEM((2, t, 2 * LANES), _MXU_DTYPE),
                        pltpu.VMEM((t, LANES), _F32)],
        compiler_params=pltpu.CompilerParams(
            dimension_semantics=("arbitrary", "arbitrary", "arbitrary"),
            vmem_limit_bytes=_VMEM_LIMIT),
        name="fox_bounded",
    )(bound, proj, proj, proj, proj, proj, bf_row)


def _fox_kernel(q_ref, k_ref, v_ref, ff_ref, g_ref, bf_ref, o_ref,
                kaug_sc, v_sc, c_sc, *, tq):
    p = pl.program_id(1)
    i = pl.program_id(2)

    def head_column(x, hh):
        return _lane_column(x, 2 * p + hh)

    @pl.when(i == 0)
    def _prep():
        pl.when(p == 0)(lambda: _forget_cumsum(ff_ref, bf_ref, c_sc))
        kn = k_ref[0]
        c_all = c_sc[...]
        lane = _lane_iota(kn.shape)
        for hh in range(2):
            a0 = HEAD_DIM if hh == 0 else 0
            hi, mid, lo = _split3(-head_column(c_all, hh))
            aug = jnp.where(lane == a0, hi,
                            jnp.where(lane == a0 + 1, mid,
                                      jnp.where(lane == a0 + 2, lo,
                                                jnp.where((lane >= a0 + 3) & (lane < a0 + 6), 1.0, 0.0))))
            kaug_sc[hh] = jnp.where(_own_half(hh, kn.shape), kn, aug).astype(_MXU_DTYPE)
        v_sc[...] = v_ref[0].astype(_MXU_DTYPE)

    q = q_ref[0] * ATTN_SCALE
    row0 = pl.multiple_of(i * tq, tq)
    c_t = c_sc[pl.ds(row0, tq), :]
    lane = _lane_iota(q.shape)
    outs = []
    for hh in range(2):
        a0 = HEAD_DIM if hh == 0 else 0
        hi, mid, lo = _split3(head_column(c_t, hh))
        aug = jnp.where((lane >= a0) & (lane < a0 + 3), 1.0,
                        jnp.where(lane == a0 + 3, hi,
                                  jnp.where(lane == a0 + 4, mid,
                                            jnp.where(lane == a0 + 5, lo, 0.0))))
        q_aug = jnp.where(_own_half(hh, q.shape), q, aug).astype(_MXU_DTYPE)

        s = lax.dot_general(q_aug, kaug_sc[hh, pl.ds(row0, tq), :], _NT,
                            preferred_element_type=_F32)
        s = jnp.where(_causal(tq), s, NEG_INF)
        m = jnp.max(s, axis=-1, keepdims=True)
        pr = jnp.exp(s - m)
        l = jnp.sum(pr, axis=-1, keepdims=True)
        acc = jnp.dot(pr.astype(_MXU_DTYPE), v_sc[pl.ds(row0, tq), :], preferred_element_type=_F32)

        def body(j, carry, q_aug=q_aug, hh=hh):
            r = pl.multiple_of(j * tq, tq)
            s = lax.dot_general(q_aug, kaug_sc[hh, pl.ds(r, tq), :], _NT,
                                preferred_element_type=_F32)
            return _flash_update(s, v_sc[pl.ds(r, tq), :], *carry)

        m, l, acc = lax.fori_loop(0, i, body, (m, l, acc))
        outs.append(acc / l)
    o = jnp.where(_own_half(0, outs[0].shape), outs[0], outs[1])
    o_ref[0] = (o * _silu(g_ref[0])).astype(o_ref.dtype)


def _fox(proj, bf_row, *, tq=256):
    b, t, _ = proj.shape
    return pl.pallas_call(
        functools.partial(_fox_kernel, tq=tq),
        grid=(b, 2, t // tq),
        in_specs=[pl.BlockSpec((1, tq, LANES), lambda bb, p, i: (bb, i, COL_FQ + p)),
                  pl.BlockSpec((1, t, LANES), lambda bb, p, i: (bb, 0, COL_FK + p)),
                  pl.BlockSpec((1, t, LANES), lambda bb, p, i: (bb, 0, COL_FV + p)),
                  pl.BlockSpec((1, t, LANES), lambda bb, p, i: (bb, 0, COL_FF)),
                  pl.BlockSpec((1, tq, LANES), lambda bb, p, i: (bb, i, COL_FG + p)),
                  pl.BlockSpec((1, LANES), lambda bb, p, i: (0, 0))],
        out_specs=pl.BlockSpec((1, tq, LANES), lambda bb, p, i: (bb, i, p)),
        out_shape=jax.ShapeDtypeStruct((b, t, GROUP_WIDTH), _MXU_DTYPE),
        scratch_shapes=[pltpu.VMEM((2, t, LANES), _MXU_DTYPE),
                        pltpu.VMEM((t, LANES), _MXU_DTYPE),
                        pltpu.VMEM((t, LANES), _F32)],
        compiler_params=pltpu.CompilerParams(
            dimension_semantics=("arbitrary", "arbitrary", "arbitrary"),
            vmem_limit_bytes=_VMEM_LIMIT),
        name="fox",
    )(proj, proj, proj, proj, proj, bf_row)


def _moba_block_means(kn, kmean_sc):
    kmean_sc[...] = jnp.zeros(kmean_sc.shape, _F32)
    for n in range(kn.shape[0] // MOBA_BLOCK):
        kmean_sc[n:n + 1, :] = jnp.mean(kn[n * MOBA_BLOCK:(n + 1) * MOBA_BLOCK], axis=0, keepdims=True)


def _moba_select(q_head, kmean, past):
    gate = lax.dot_general(q_head, kmean, _NT, precision=lax.Precision.HIGHEST,
                           preferred_element_type=_F32)
    lane_f = _lane_iota(gate.shape).astype(_F32)
    cand = jnp.where(past, gate, -jnp.inf)
    sel = jnp.zeros(gate.shape, _F32)
    for _ in range(MOBA_TOPK):
        mx = jnp.max(cand, axis=-1, keepdims=True)
        is_max = (cand == mx) & (mx > -jnp.inf)
        first = jnp.min(jnp.where(is_max, lane_f, float(LANES)), axis=-1, keepdims=True)
        pick = lane_f == first
        sel = jnp.where(pick, 1.0, sel)
        cand = jnp.where(pick, -jnp.inf, cand)
    return sel


def _moba_bounded_kernel(rb_ref, bound_ref, q_ref, k_ref, v_ref, g_ref, bias_ref, o_ref,
                         kaug_sc, vext_sc, kmean_sc):
    p = pl.program_id(1)
    i = pl.program_id(2)
    tq = q_ref.shape[1]
    blk = MOBA_BLOCK
    near = MOBA_WIDE_NEAR_TILES

    @pl.when(i == 0)
    def _prep():
        kn = k_ref[0]
        _moba_block_means(kn, kmean_sc)
        v = v_ref[0]
        lane = _lane_iota(kn.shape)
        row_blk = lax.broadcasted_iota(jnp.int32, kn.shape, 0) // blk
        for hh in range(2):
            a0 = HEAD_DIM if hh == 0 else 0
            aug = jnp.where((lane - a0 == row_blk) | (lane - a0 - 16 == row_blk)
                            | (lane - a0 == 32) | (lane - a0 == 33), 1.0, 0.0)
            own = _own_half(hh, kn.shape)
            kaug_sc[hh] = jnp.where(own, kn, aug).astype(_MXU_DTYPE)
            vext_sc[hh, :, 0:LANES] = jnp.where(own, v, 0.0).astype(_MXU_DTYPE)
            vext_sc[hh, :, LANES:2 * LANES] = jnp.where(lane == hh, 1.0, 0.0).astype(_MXU_DTYPE)

    qn = q_ref[0]
    row0 = pl.multiple_of(i * tq, tq)
    off = jnp.full((1, 1), -LOG2E, _F32) * bound_ref[0]
    off_hi = off.astype(_MXU_DTYPE).astype(_F32)
    blk_n = lax.broadcasted_iota(jnp.int32, (16, tq), 0)
    blk_f = blk_n.astype(_F32)
    q_blk = i * (tq // blk) + lax.broadcasted_iota(jnp.int32, (16, tq), 1) // blk
    past = blk_n < q_blk
    far = i - blk_n // 2 >= near
    sub8 = lax.broadcasted_iota(jnp.int32, (8, tq), 0)
    off_rows = jnp.where(sub8 == 0, off_hi, jnp.where(sub8 == 1, off - off_hi, 0.0))
    q_augs = []
    for hh in range(2):
        own = _own_half(hh, qn.shape)
        gate = lax.dot_general(kmean_sc[0:16, :], jnp.where(own, qn, 0.0), _NT,
                               precision=lax.Precision.HIGHEST, preferred_element_type=_F32)
        cand = jnp.where(past, gate, -jnp.inf)
        sel = jnp.zeros((16, tq), _F32)
        for _ in range(MOBA_TOPK):
            mx = jnp.max(cand, axis=0, keepdims=True)
            is_max = (cand == mx) & (mx > -jnp.inf)
            first = jnp.min(jnp.where(is_max, blk_f, float(LANES)), axis=0, keepdims=True)
            pick = blk_f == first
            sel = jnp.where(pick, 1.0, sel)
            cand = jnp.where(pick, -jnp.inf, cand)
        c_far = jnp.full((1, 1), LOG2E, _F32) * rb_ref[NUM_BUCKETS - 1, 2 * p + hh]
        far_hi = c_far.astype(_MXU_DTYPE).astype(_F32)
        pen = jnp.where((blk_n == q_blk) | (past & (sel != 0.0)), 0.0, NEG_INF)
        fields = jnp.concatenate([pen + jnp.where(far, far_hi, 0.0),
                                  jnp.where(far, c_far - far_hi, 0.0),
                                  off_rows,
                                  jnp.zeros((HEAD_DIM - 40, tq), _F32)], axis=0)
        blank = jnp.zeros((HEAD_DIM, tq), _F32)
        aug = jnp.concatenate([blank, fields] if hh == 0 else [fields, blank], axis=0).T
        q_augs.append(jnp.where(own, qn * (ATTN_SCALE * LOG2E), aug).astype(_MXU_DTYPE))

    def body(j, state, with_bias):
        r_prev, ps0, ps1, acc = state
        r = pl.multiple_of(j * tq, tq)
        bias = (lambda hh: bias_ref[hh, i - j]) if with_bias else None
        ps = _bounded_weights(q_augs, kaug_sc, r, bias, False)
        acc = _bounded_values([ps0, ps1], vext_sc, pl.multiple_of(r_prev, tq), acc)
        return r, ps[0], ps[1], acc

    ps = _bounded_weights(q_augs, kaug_sc, row0, lambda hh: bias_ref[hh, 0], True)
    state = (row0, ps[0], ps[1], jnp.zeros((tq, 2 * LANES), _F32))
    near_lo = jnp.maximum(i - (near - 1), 0)
    state = lax.fori_loop(near_lo, i, functools.partial(body, with_bias=True), state)
    state = lax.fori_loop(0, near_lo, functools.partial(body, with_bias=False), state)
    acc = _bounded_values([state[1], state[2]], vext_sc, pl.multiple_of(state[0], tq), state[3])
    den = acc[:, LANES:]
    l = jnp.where(_own_half(0, (tq, LANES)), _lane_column(den, 0), _lane_column(den, 1))
    o_ref[0] = (acc[:, :LANES] / l * _silu(g_ref[0])).astype(o_ref.dtype)


def _moba_bounded(proj, rel_bias, bias_tiles, bound):
    b, t, _ = proj.shape
    tq = MOBA_WIDE
    assert t % tq == 0 and t // MOBA_BLOCK <= 16
    return pl.pallas_call(
        _moba_bounded_kernel,
        grid=(b, 2, t // tq),
        in_specs=[pl.BlockSpec(memory_space=pltpu.SMEM),
                  pl.BlockSpec(memory_space=pltpu.SMEM),
                  pl.BlockSpec((1, tq, LANES), lambda bb, p, i: (bb, i, COL_MQ + p)),
                  pl.BlockSpec((1, t, LANES), lambda bb, p, i: (bb, 0, COL_MK + p)),
                  pl.BlockSpec((1, t, LANES), lambda bb, p, i: (bb, 0, COL_MV + p)),
                  pl.BlockSpec((1, tq, LANES), lambda bb, p, i: (bb, i, COL_MG + p)),
                  pl.BlockSpec((2, MOBA_WIDE_NEAR_TILES, tq, tq), lambda bb, p, i: (p, 0, 0, 0))],
        out_specs=pl.BlockSpec((1, tq, LANES), lambda bb, p, i: (bb, i, p)),
        out_shape=jax.ShapeDtypeStruct((b, t, GROUP_WIDTH), _MXU_DTYPE),
        scratch_shapes=[pltpu.VMEM((2, t, LANES), _MXU_DTYPE),
                        pltpu.VMEM((2, t, 2 * LANES), _MXU_DTYPE),
                        pltpu.VMEM((LANES, LANES), _F32)],
        compiler_params=pltpu.CompilerParams(
            dimension_semantics=("arbitrary", "arbitrary", "arbitrary"),
            vmem_limit_bytes=_VMEM_LIMIT),
        name="moba_bounded",
    )(rel_bias, bound, proj, proj, proj, proj, bias_tiles)


def _moba_kernel(rb_ref, q_ref, k_ref, v_ref, g_ref, bias_ref, o_ref,
                 kaug_sc, v_sc, kmean_sc):
    p = pl.program_id(1)
    i = pl.program_id(2)
    t_len = k_ref.shape[1]
    blk = MOBA_BLOCK
    nblk = t_len // blk
    near = MOBA_NEAR_TILES

    @pl.when(i == 0)
    def _prep():
        kn = k_ref[0]
        _moba_block_means(kn, kmean_sc)
        lane = _lane_iota(kn.shape)
        row_blk = lax.broadcasted_iota(jnp.int32, kn.shape, 0) // blk
        for hh in range(2):
            a0 = HEAD_DIM if hh == 0 else 0
            onehot = jnp.where((lane - a0 == row_blk) | (lane - a0 - 16 == row_blk), 1.0, 0.0)
            kaug_sc[hh] = jnp.where(_own_half(hh, kn.shape), kn, onehot).astype(_MXU_DTYPE)
        v_sc[...] = v_ref[0].astype(_MXU_DTYPE)

    qn = q_ref[0]
    row0 = pl.multiple_of(i * blk, blk)
    lane = _lane_iota(qn.shape)
    lane_f = lane.astype(_F32)
    past = lane < i
    outs = []
    for hh in range(2):
        own = _own_half(hh, qn.shape)
        sel = _moba_select(jnp.where(own, qn, 0.0), kmean_sc[...], past)
        c_far = jnp.full((1, LANES), rb_ref[NUM_BUCKETS - 1, 2 * p + hh], _F32)
        far_hi = c_far.astype(_MXU_DTYPE).astype(_F32)
        far_lo = c_far - far_hi
        pen = jnp.where(past & (sel == 0.0), NEG_INF, 0.0)
        aug = jnp.where(lane < 16, pen + jnp.where(i - lane >= near, far_hi, 0.0),
                        jnp.where((lane < 32) & (i - (lane - 16) >= near), far_lo, 0.0))
        if hh == 0:
            aug = pltpu.roll(aug, HEAD_DIM, axis=1)
        q_aug = jnp.where(own, qn * ATTN_SCALE, aug).astype(_MXU_DTYPE)

        def scores(r, q_aug=q_aug, hh=hh):
            return lax.dot_general(q_aug, kaug_sc[hh, pl.ds(r, blk), :], _NT,
                                   preferred_element_type=_F32)

        s = scores(row0) + bias_ref[hh, 0]
        s = jnp.where(_causal(blk), s, NEG_INF)
        m = jnp.max(s, axis=-1, keepdims=True)
        pr = jnp.exp(s - m)
        l = jnp.sum(pr, axis=-1, keepdims=True)
        acc = jnp.dot(pr.astype(_MXU_DTYPE), v_sc[pl.ds(row0, blk), :], preferred_element_type=_F32)

        def near_body(j, carry, hh=hh, scores=scores):
            r = pl.multiple_of(j * blk, blk)
            return _flash_update(scores(r) + bias_ref[hh, i - j], v_sc[pl.ds(r, blk), :], *carry)

        def far_body(j, carry, scores=scores):
            r = pl.multiple_of(j * blk, blk)
            return _flash_update(scores(r), v_sc[pl.ds(r, blk), :], *carry)

        near_lo = jnp.maximum(i - (near - 1), 0)
        carry = lax.fori_loop(near_lo, i, near_body, (m, l, acc))
        m, l, acc = lax.fori_loop(0, near_lo, far_body, carry)
        outs.append(acc / l)
    o = jnp.where(_own_half(0, outs[0].shape), outs[0], outs[1])
    o_ref[0] = (o * _silu(g_ref[0])).astype(o_ref.dtype)


def _moba(proj, rel_bias, bias_tiles):
    b, t, _ = proj.shape
    blk = MOBA_BLOCK
    assert t % blk == 0 and t // blk <= 16
    return pl.pallas_call(
        _moba_kernel,
        grid=(b, 2, t // blk),
        in_specs=[pl.BlockSpec(memory_space=pltpu.SMEM),
                  pl.BlockSpec((1, blk, LANES), lambda bb, p, i: (bb, i, COL_MQ + p)),
                  pl.BlockSpec((1, t, LANES), lambda bb, p, i: (bb, 0, COL_MK + p)),
                  pl.BlockSpec((1, t, LANES), lambda bb, p, i: (bb, 0, COL_MV + p)),
                  pl.BlockSpec((1, blk, LANES), lambda bb, p, i: (bb, i, COL_MG + p)),
                  pl.BlockSpec((2, MOBA_NEAR_TILES, blk, blk), lambda bb, p, i: (p, 0, 0, 0))],
        out_specs=pl.BlockSpec((1, blk, LANES), lambda bb, p, i: (bb, i, p)),
        out_shape=jax.ShapeDtypeStruct((b, t, GROUP_WIDTH), _MXU_DTYPE),
        scratch_shapes=[pltpu.VMEM((2, t, LANES), _MXU_DTYPE),
                        pltpu.VMEM((t, LANES), _MXU_DTYPE),
                        pltpu.VMEM((LANES, LANES), _F32)],
        compiler_params=pltpu.CompilerParams(
            dimension_semantics=("arbitrary", "arbitrary", "arbitrary"),
            vmem_limit_bytes=_VMEM_LIMIT),
        name="moba",
    )(rel_bias, proj, proj, proj, proj, bias_tiles)


def _sb_kernel(q_ref, k_ref, v_ref, g_ref, o_ref, k_sc, v_sc, *, tq):
    i = pl.program_id(2)

    @pl.when(i == 0)
    def _prep():
        k_sc[...] = k_ref[0].astype(_MXU_DTYPE)
        v = v_ref[0]
        for hh in range(2):
            v_sc[hh] = jnp.where(_own_half(hh, v.shape), v, 0.0).astype(_MXU_DTYPE)

    subs = q_ref.shape[1] // tq
    q_heads = []
    for u in range(subs):
        q = q_ref[0, u * tq:(u + 1) * tq, :] * (ATTN_SCALE * LOG2E)
        q_heads.append([jnp.where(_own_half(hh, q.shape), q, 0.0).astype(_MXU_DTYPE)
                        for hh in range(2)])
    strict =(lax.broadcasted_iota(jnp.int32, (tq, tq), 0)
              > lax.broadcasted_iota(jnp.int32, (tq, tq), 1))
    after = strict.astype(_MXU_DTYPE)
    sign_bit = jnp.uint32(0x80000000)

    def pair(back, runs, diagonal):
        chains = [(u, t, hh) for u in range(subs) for t in range(2) for hh in range(2)]
        tile_idx = [[subs * i + u - back - t for t in range(2)] for u in range(subs)]
        valid = [[(j >= 0).astype(_F32) for j in js] for js in tile_idx]
        rows = [[pl.multiple_of(jnp.maximum(j, 0) * tq, tq) for j in js] for js in tile_idx]
        masked = [diagonal and t == 0 for _, t, _ in chains]
        zs = [lax.dot_general(q_heads[u][hh], k_sc[pl.ds(rows[u][t], tq), :], _NT,
                              preferred_element_type=_F32) for u, t, hh in chains]
        drops = []
        for z, msk in zip(zs, masked):
            neg_abs = lax.bitcast_convert_type(lax.bitcast_convert_type(z, jnp.uint32) | sign_bit, _F32)
            drop = jnp.maximum(z, 0.0) + jnp.log(1.0 + jnp.exp2(neg_abs)) * LOG2E
            drops.append(jnp.where(strict, drop, 0.0) if msk else drop)
        laters = []
        for drop in drops:
            hi = drop.astype(_MXU_DTYPE)
            lo = (drop - hi.astype(_F32)).astype(_MXU_DTYPE)
            laters.append(jnp.dot(hi, after, preferred_element_type=_F32)
                          + jnp.dot(lo, after, preferred_element_type=_F32))
        pvs = []
        for (u, t, hh), z, drop, later, msk in zip(chains, zs, drops, laters, masked):
            w = jnp.exp2(z - drop - later)
            if msk:
                w = jnp.where(strict, w, 0.0)
            pvs.append(jnp.dot(w.astype(_MXU_DTYPE), v_sc[hh, pl.ds(rows[u][t], tq), :],
                               preferred_element_type=_F32))
        sums = [jnp.sum(drop, axis=-1, keepdims=True) for drop in drops]
        new_runs, outs = [], []
        for u in range(subs):
            new_runs.append([])
            out = None
            for hh in range(2):
                run = runs[u][hh]
                for t in range(2):
                    c = chains.index((u, t, hh))
                    pv = pvs[c] * (jnp.exp2(run) * valid[u][t])
                    out = pv if out is None else out + pv
                    run = run - sums[c] * valid[u][t]
                new_runs[u].append(run)
            outs.append(out)
        return new_runs, outs

    def alive(runs):
        top = functools.reduce(jnp.maximum, [r for rs in runs for r in rs])
        return jnp.max(top) >= EXP_ZERO_CUTOFF * LOG2E

    def flat(runs):
        return [r for rs in runs for r in rs]

    def nested(flat_runs):
        return [flat_runs[2 * u:2 * u + 2] for u in range(subs)]

    zero = jnp.zeros((tq, 1), _F32)
    runs, accs = pair(0, [[zero, zero] for _ in range(subs)], True)

    def body(state):
        back = state[0]
        runs, outs = pair(back, nested(list(state[2:2 + 2 * subs])), False)
        accs = [a + o for a, o in zip(state[2 + 2 * subs:], outs)]
        return (back + 2, alive(runs), *flat(runs), *accs)

    state = lax.while_loop(lambda state: (subs * i + subs - 1 - state[0] >= 0) & state[1], body,
                           (jnp.int32(2), alive(runs), *flat(runs), *accs))
    for u in range(subs):
        o_ref[0, u * tq:(u + 1) * tq, :] = (
            state[2 + 2 * subs + u] * _silu(g_ref[0, u * tq:(u + 1) * tq, :])).astype(o_ref.dtype)


def _sb(proj, *, tq=256, subs=2):
    b, t, _ = proj.shape
    rows = tq * subs
    return pl.pallas_call(
        functools.partial(_sb_kernel, tq=tq),
        grid=(b, 2, t // rows),
        in_specs=[pl.BlockSpec((1, rows, LANES), lambda bb, p, i: (bb, i, COL_SQ + p)),
                  pl.BlockSpec((1, t, LANES), lambda bb, p, i: (bb, 0, COL_SK + p)),
                  pl.BlockSpec((1, t, LANES), lambda bb, p, i: (bb, 0, COL_SV + p)),
                  pl.BlockSpec((1, rows, LANES), lambda bb, p, i: (bb, i, COL_SG + p))],
        out_specs=pl.BlockSpec((1, rows, LANES), lambda bb, p, i: (bb, i, p)),
        out_shape=jax.ShapeDtypeStruct((b, t, GROUP_WIDTH), _MXU_DTYPE),
        scratch_shapes=[pltpu.VMEM((t, LANES), _MXU_DTYPE),
                        pltpu.VMEM((2, t, LANES), _MXU_DTYPE)],
        compiler_params=pltpu.CompilerParams(
            dimension_semantics=("arbitrary", "arbitrary", "arbitrary"),
            vmem_limit_bytes=_VMEM_LIMIT),
        name="stickbreak",
    )(proj, proj, proj, proj)


def _swa_kernel(sink_ref, q_ref, k_ref, v_ref, g_ref, bias_ref, o_ref,
                k_sc, v_sc, *, tq):
    kv = pl.program_id(1)
    i = pl.program_id(2)
    t_len = k_ref.shape[1]
    w = SWA_WINDOW

    @pl.when(i == 0)
    def _prep():
        kn = k_ref[0]
        v = v_ref[0]
        keep = (_lane_iota(kn.shape) < HEAD_DIM) == (kv == 0)
        k_sc[0:w, :] = jnp.zeros((w, LANES), _MXU_DTYPE)
        v_sc[0:w, :] = jnp.zeros((w, LANES), _MXU_DTYPE)
        k_sc[w:w + t_len, :] = jnp.where(keep, kn, pltpu.roll(kn, HEAD_DIM, axis=1)).astype(_MXU_DTYPE)
        v_sc[w:w + t_len, :] = jnp.where(keep, v, pltpu.roll(v, HEAD_DIM, axis=1)).astype(_MXU_DTYPE)

    q = q_ref[0] * ATTN_SCALE
    qi = lax.broadcasted_iota(jnp.int32, (w, 2 * w), 0)
    kj = lax.broadcasted_iota(jnp.int32, (w, 2 * w), 1)
    dist = qi + w - kj
    in_window = (dist >= 0) & (dist < w)
    for u in range(tq // w):
        nb = i * (tq // w) + u
        r = pl.multiple_of(nb * w, w)
        k_t = k_sc[pl.ds(r, 2 * w), :]
        v_t = v_sc[pl.ds(r, 2 * w), :]
        allowed = in_window & (kj + (nb - 1) * w >= 0)
        q_u = q[u * w:(u + 1) * w]
        outs = []
        for g in range(2):
            q_h = jnp.where(_own_half(g, q_u.shape), q_u, 0.0).astype(_MXU_DTYPE)
            s = lax.dot_general(q_h, k_t, _NT, preferred_element_type=_F32)
            s = jnp.where(allowed, s + bias_ref[g], NEG_INF)
            sink = sink_ref[2 * kv + g]
            m = jnp.maximum(jnp.max(s, axis=-1, keepdims=True), sink)
            e = jnp.exp(s - m)
            den = jnp.sum(e, axis=-1, keepdims=True) + jnp.exp(sink - m)
            outs.append(jnp.dot(e.astype(_MXU_DTYPE), v_t, preferred_element_type=_F32) / den)
        o = jnp.where(_own_half(0, outs[0].shape), outs[0], outs[1])
        o_ref[0, u * w:(u + 1) * w, :] = (o * _silu(g_ref[0, u * w:(u + 1) * w, :])).astype(o_ref.dtype)


def _swa(proj, sinks, bias_tiles, *, tq=512):
    b, t, _ = proj.shape
    w = SWA_WINDOW
    return pl.pallas_call(
        functools.partial(_swa_kernel, tq=tq),
        grid=(b, 2, t // tq),
        in_specs=[pl.BlockSpec(memory_space=pltpu.SMEM),
                  pl.BlockSpec((1, tq, LANES), lambda bb, kv, i: (bb, i, COL_WQ + kv)),
                  pl.BlockSpec((1, t, LANES), lambda bb, kv, i: (bb, 0, COL_WK)),
                  pl.BlockSpec((1, t, LANES), lambda bb, kv, i: (bb, 0, COL_WV)),
                  pl.BlockSpec((1, tq, LANES), lambda bb, kv, i: (bb, i, COL_WG + kv)),
                  pl.BlockSpec((2, w, 2 * w), lambda bb, kv, i: (kv, 0, 0))],
        out_specs=pl.BlockSpec((1, tq, LANES), lambda bb, kv, i: (bb, i, kv)),
        out_shape=jax.ShapeDtypeStruct((b, t, GROUP_WIDTH), _MXU_DTYPE),
        scratch_shapes=[pltpu.VMEM((t + w, LANES), _MXU_DTYPE),
                        pltpu.VMEM((t + w, LANES), _MXU_DTYPE)],
        compiler_params=pltpu.CompilerParams(
            dimension_semantics=("arbitrary", "arbitrary", "arbitrary"),
            vmem_limit_bytes=_VMEM_LIMIT),
        name="swa",
    )(sinks, proj, proj, proj, proj, bias_tiles)


def _outproj_kernel(x_ref, ya_ref, yb_ref, yc_ref, yd_ref, w_ref, o_ref):
    acc = x_ref[...]
    for g, y_ref in enumerate((ya_ref, yb_ref, yc_ref, yd_ref)):
        acc = acc + jnp.dot(y_ref[...], w_ref[g * GROUP_WIDTH:(g + 1) * GROUP_WIDTH, :],
                            preferred_element_type=_F32)
    o_ref[...] = acc


def _outproj(x2d, ys, w_all, layer):
    m, d = x2d.shape
    tm = 512
    y_spec = pl.BlockSpec((tm, GROUP_WIDTH), lambda i: (i, 0))
    return pl.pallas_call(
        _outproj_kernel,
        grid=(m // tm,),
        in_specs=[pl.BlockSpec((tm, d), lambda i: (i, 0)), y_spec, y_spec, y_spec, y_spec,
                  pl.BlockSpec((None,) + w_all.shape[1:], lambda i: (layer, 0, 0))],
        out_specs=pl.BlockSpec((tm, d), lambda i: (i, 0)),
        out_shape=jax.ShapeDtypeStruct((m, d), _F32),
        compiler_params=pltpu.CompilerParams(
            dimension_semantics=("arbitrary",), vmem_limit_bytes=_VMEM_LIMIT),
        name="outproj",
    )(x2d, *ys, w_all)


def _rearrange_w_in(w_in):
    off = 3 * GROUP_WIDTH
    main = jnp.concatenate([w_in[..., :off], w_in[..., off + FOX_FORGET_COLS:]], axis=-1)
    ff = w_in[..., off:off + FOX_FORGET_COLS]
    pad = PROJ_WIDTH - main.shape[-1] - FOX_FORGET_COLS
    return jnp.concatenate([main, ff, jnp.zeros(w_in.shape[:-1] + (pad,), w_in.dtype)], axis=-1)


def _qk_logit_bound(qk_gain):
    g = jnp.abs(qk_gain.astype(_F32))
    return ATTN_SCALE * HEAD_DIM * jnp.max(g[0]) * jnp.max(g[1])


def _qk_gain_row(fox_g, moba_g, swa_g):
    def tiles(g, n):
        return jnp.tile(g.astype(_F32), 2 * n)
    row = jnp.zeros((PROJ_WIDTH,), _F32)
    for col, g, n in ((COL_FQ, fox_g[0], 2), (COL_FK, fox_g[1], 2),
                      (COL_MQ, moba_g[0], 2), (COL_MK, moba_g[1], 2),
                      (COL_WQ, swa_g[0], 2), (COL_WK, swa_g[1], 1)):
        row = lax.dynamic_update_slice(row, tiles(g, n), (col * LANES,))
    return row[None, :]


def kernel(x, norm_gain, w_in, b_forget, fox_qk_gain, moba_qk_gain, swa_qk_gain, sinks, w_out, rel_bias):
    b, t, d = x.shape
    depth = w_in.shape[0]
    w_in_r = _rearrange_w_in(w_in).astype(_MXU_DTYPE)
    w_out_c = w_out.astype(_MXU_DTYPE)
    moba_bias, moba_wide_bias, swa_bias = _bias_tiles(rel_bias)
    moba_bias_max = jnp.max(jnp.abs(rel_bias[:, :rel_bias.shape[1] // 2].astype(_F32)))
    x2d = x.reshape(b * t, d)
    for layer in range(depth):
        qk_gain_row = _qk_gain_row(fox_qk_gain[layer], moba_qk_gain[layer], swa_qk_gain[layer])
        proj = _inproj(x2d, norm_gain[layer][None, :], w_in_r, layer, qk_gain_row)
        proj = proj.reshape(b, t, PROJ_WIDTH)
        bf_row = jnp.pad(b_forget[layer], (0, LANES - FOX_FORGET_COLS))[None, :]
        fox_bound = _qk_logit_bound(fox_qk_gain[layer])
        y_fox = lax.cond(2.0 * fox_bound <= BOUNDED_SOFTMAX_RANGE,
                         lambda: _fox_bounded(proj, bf_row, fox_bound.reshape(1)),
                         lambda: _fox(proj, bf_row))
        moba_bound = _qk_logit_bound(moba_qk_gain[layer]) + moba_bias_max
        y_moba = lax.cond(2.0 * moba_bound <= BOUNDED_SOFTMAX_RANGE,
                          lambda: _moba_bounded(proj, rel_bias, moba_wide_bias,
                                                moba_bound.reshape(1)),
                          lambda: _moba(proj, rel_bias, moba_bias))
        y_sb = _sb(proj)
        y_swa = _swa(proj, sinks[layer], swa_bias)
        ys = [y.reshape(b * t, GROUP_WIDTH) for y in (y_fox, y_moba, y_sb, y_swa)]
        x2d = _outproj(x2d, ys, w_out_c, layer)
    return x2d.reshape(b, t, d)
```

```python
import functools
import math

import jax
import jax.numpy as jnp
from jax import lax
from jax.experimental import pallas as pl
from jax.experimental.pallas import tpu as pltpu

HEAD_DIM = 64
LANES = 128
GROUP_WIDTH = 256
MOBA_BLOCK = 256
MOBA_TOPK = 3
SWA_WINDOW = 128
NUM_BUCKETS = 32
REL_MAX_DISTANCE = 1024
RMS_EPS = 1e-6
NEG_INF = -1e30
ATTN_SCALE = HEAD_DIM ** -0.5
LOG2E = math.log2(math.e)
BOUNDED_SOFTMAX_RANGE = 60.0
MOBA_NEAR_TILES = REL_MAX_DISTANCE // MOBA_BLOCK + 1
MOBA_WIDE = 2 * MOBA_BLOCK
MOBA_WIDE_NEAR_TILES = REL_MAX_DISTANCE // MOBA_WIDE + 1
FOX_FORGET_COLS = 4
EXP_ZERO_CUTOFF = -104.0

COL_FQ, COL_FK, COL_FV, COL_FG = 0, 2, 4, 6
COL_MQ, COL_MK, COL_MV, COL_MG = 8, 10, 12, 14
COL_SQ, COL_SK, COL_SV, COL_SG = 16, 18, 20, 22
COL_WQ, COL_WK, COL_WV, COL_WG = 24, 26, 27, 28
COL_FF = 30
PROJ_WIDTH = 32 * LANES
QK_NORM_COLS = (COL_FQ, COL_FQ + 1, COL_FK, COL_FK + 1, COL_MQ, COL_MQ + 1, COL_MK, COL_MK + 1,
                COL_WQ, COL_WQ + 1, COL_WK)

_MXU_DTYPE = jnp.bfloat16
_F32 = jnp.float32
_VMEM_LIMIT = 48 * 1024 * 1024

_NT = (((1,), (1,)), ((), ()))


def _lane_iota(shape):
    return lax.broadcasted_iota(jnp.int32, shape, len(shape) - 1)


def _pair_rms(x, gain_row):
    low = _lane_iota(x.shape) < HEAD_DIM
    sq = x * x
    ms_lo = jnp.sum(jnp.where(low, sq, 0.0), axis=-1, keepdims=True)
    ms_hi = jnp.sum(jnp.where(low, 0.0, sq), axis=-1, keepdims=True)
    ms = jnp.where(low, ms_lo, ms_hi) * (1.0 / HEAD_DIM)
    return x * lax.rsqrt(ms + RMS_EPS) * gain_row


def _silu(g):
    return g * (1.0 / (1.0 + jnp.exp(-g)))


def _split3(x):
    hi = x.astype(_MXU_DTYPE).astype(_F32)
    r = x - hi
    mid = r.astype(_MXU_DTYPE).astype(_F32)
    return hi, mid, r - mid


def _own_half(hh, shape):
    lane = _lane_iota(shape)
    return (lane < HEAD_DIM) if hh == 0 else (lane >= HEAD_DIM)


def _flash_update(s, v_tile, m, l, acc):
    m_new = jnp.maximum(m, jnp.max(s, axis=-1, keepdims=True))
    alpha = jnp.exp(m - m_new)
    p = jnp.exp(s - m_new)
    l = alpha * l + jnp.sum(p, axis=-1, keepdims=True)
    acc = alpha * acc + jnp.dot(p.astype(_MXU_DTYPE), v_tile, preferred_element_type=_F32)
    return m_new, l, acc


def _causal(tq):
    return (lax.broadcasted_iota(jnp.int32, (tq, tq), 0)
            >= lax.broadcasted_iota(jnp.int32, (tq, tq), 1))


def _inproj_kernel(x_ref, gain_ref, w_ref, qkg_ref, o_ref, *, tn):
    x = x_ref[...]
    ms = jnp.mean(x * x, axis=-1, keepdims=True)
    hn = (x * lax.rsqrt(ms + RMS_EPS) * gain_ref[...]).astype(_MXU_DTYPE)
    for n in range(o_ref.shape[1] // tn):
        acc = jnp.dot(hn, w_ref[:, n * tn:(n + 1) * tn], preferred_element_type=_F32)
        for c in range(n * tn // LANES, (n + 1) * tn // LANES):
            tile = acc[:, c * LANES - n * tn:(c + 1) * LANES - n * tn]
            if c in QK_NORM_COLS:
                tile = _pair_rms(tile, qkg_ref[:, c * LANES:(c + 1) * LANES])
            o_ref[:, c * LANES:(c + 1) * LANES] = tile


def _inproj(x2d, gain_row, w_all, layer, qk_gain_row):
    m, d = x2d.shape
    n = w_all.shape[2]
    tm = 256
    return pl.pallas_call(
        functools.partial(_inproj_kernel, tn=1024),
        grid=(m // tm,),
        in_specs=[pl.BlockSpec((tm, d), lambda i: (i, 0)),
                  pl.BlockSpec((1, d), lambda i: (0, 0)),
                  pl.BlockSpec((None, d, n), lambda i: (layer, 0, 0)),
                  pl.BlockSpec((1, n), lambda i: (0, 0))],
        out_specs=pl.BlockSpec((tm, n), lambda i: (i, 0)),
        out_shape=jax.ShapeDtypeStruct((m, n), _F32),
        compiler_params=pltpu.CompilerParams(
            dimension_semantics=("arbitrary",), vmem_limit_bytes=_VMEM_LIMIT),
        name="inproj",
    )(x2d, gain_row, w_all, qk_gain_row)


def _rel_bucket(dist):
    max_exact = NUM_BUCKETS // 2
    d = jnp.maximum(dist, 0)
    log_ratio = (jnp.log(jnp.maximum(d, 1).astype(_F32) / max_exact)
                 / math.log(REL_MAX_DISTANCE / max_exact))
    large = max_exact + (log_ratio * (NUM_BUCKETS - max_exact)).astype(jnp.int32)
    large = jnp.minimum(large, NUM_BUCKETS - 1)
    return jnp.where(d < max_exact, d, large)


def _bias_kernel(rb_ref, bm_ref, bw_ref, bs_ref, om_ref, ow_ref, os_ref, *, n_heads):
    h = pl.program_id(0)

    def lookup(buckets, col):
        acc = jnp.zeros(buckets.shape, _F32)
        for k in range(NUM_BUCKETS):
            acc = jnp.where(buckets == k, rb_ref[k, col], acc)
        return acc

    for d in range(bm_ref.shape[0]):
        om_ref[0, d] = lookup(bm_ref[d], h)
    for d in range(bw_ref.shape[0]):
        ow_ref[0, d] = lookup(bw_ref[d], h) * LOG2E
    os_ref[0] = lookup(bs_ref[...], n_heads + h)


def _bias_tiles(rel_bias):
    n_heads = rel_bias.shape[1] // 2

    def toeplitz(size, count):
        i = jnp.arange(size)[:, None]
        j = jnp.arange(size)[None, :]
        return jnp.stack([_rel_bucket(d * size + i - j) for d in range(count)])

    bm = toeplitz(MOBA_BLOCK, MOBA_NEAR_TILES)
    bw = toeplitz(MOBA_WIDE, MOBA_WIDE_NEAR_TILES)
    w = SWA_WINDOW
    bs = _rel_bucket(jnp.arange(w)[:, None] + w - jnp.arange(2 * w)[None, :])
    return pl.pallas_call(
        functools.partial(_bias_kernel, n_heads=n_heads),
        grid=(n_heads,),
        in_specs=[pl.BlockSpec(memory_space=pltpu.SMEM),
                  pl.BlockSpec(bm.shape, lambda h: (0, 0, 0)),
                  pl.BlockSpec(bw.shape, lambda h: (0, 0, 0)),
                  pl.BlockSpec(bs.shape, lambda h: (0, 0))],
        out_specs=[pl.BlockSpec((1,) + bm.shape, lambda h: (h, 0, 0, 0)),
                   pl.BlockSpec((1,) + bw.shape, lambda h: (h, 0, 0, 0)),
                   pl.BlockSpec((1,) + bs.shape, lambda h: (h, 0, 0))],
        out_shape=[jax.ShapeDtypeStruct((n_heads,) + bm.shape, _F32),
                   jax.ShapeDtypeStruct((n_heads,) + bw.shape, _F32),
                   jax.ShapeDtypeStruct((n_heads,) + bs.shape, _F32)],
        compiler_params=pltpu.CompilerParams(
            dimension_semantics=("arbitrary",), vmem_limit_bytes=_VMEM_LIMIT),
        name="bias_tiles",
    )(rel_bias, bm, bw, bs)


def _forget_cumsum(ff_ref, bf_ref, c_sc):
    t_len = c_sc.shape[0]
    ff = ff_ref[0] + bf_ref[...]
    log_f = jnp.minimum(ff, 0.0) - jnp.log(1.0 + jnp.exp(-jnp.abs(ff)))
    parts = [part.astype(_MXU_DTYPE) for part in _split3(log_f)]
    ch = 256
    tri = (lax.broadcasted_iota(jnp.int32, (ch, ch), 0)
           >= lax.broadcasted_iota(jnp.int32, (ch, ch), 1)).astype(_MXU_DTYPE)
    carry = jnp.zeros((1, LANES), _F32)
    for r in range(t_len // ch):
        inc = carry
        for part in reversed(parts):
            inc = inc + jnp.dot(tri, part[r * ch:(r + 1) * ch], preferred_element_type=_F32)
        c_sc[r * ch:(r + 1) * ch, :] = inc
        carry = inc[ch - 1:ch, :]


def _lane_column(x, col):
    return jnp.sum(jnp.where(_lane_iota(x.shape) == col, x, 0.0), axis=-1, keepdims=True)


def _lane_fields(shape, a0, fields):
    lane = _lane_iota(shape)
    out = jnp.zeros(shape, _F32)
    for n, f in enumerate(fields):
        out = jnp.where(lane == a0 + n, f, out)
    return out


def _bounded_weights(ps_ref, q_augs, kaug_sc, r, bias, diagonal):
    tq = q_augs[0].shape[0]
    for hh in range(2):
        s = lax.dot_general(q_augs[hh], kaug_sc[hh, pl.ds(r, tq), :], _NT,
                            preferred_element_type=_F32)
        if bias is not None:
            s = s + bias(hh)
        if diagonal:
            s = jnp.where(_causal(tq), s, NEG_INF)
        ps_ref[hh] = jnp.exp2(s).astype(_MXU_DTYPE)


def _bounded_values(ps_ref, vext_sc, r, acc_sc):
    tq = ps_ref.shape[1]
    acc_sc[...] += (jnp.dot(ps_ref[0], vext_sc[0, pl.ds(r, tq), :], preferred_element_type=_F32)
                    + jnp.dot(ps_ref[1], vext_sc[1, pl.ds(r, tq), :], preferred_element_type=_F32))


def _bounded_pipeline_step(n, ps_bufs, weights, values):
    def new_in(slot):
        def run():
            weights(ps_bufs[slot])
            values(ps_bufs[1 - slot])
        return run
    lax.cond(n % 2 == 0, new_in(0), new_in(1))


def _bounded_finish(n_done, ps_bufs, values):
    lax.cond(n_done % 2 == 0, lambda: values(ps_bufs[0]), lambda: values(ps_bufs[1]))


def _fox_bounded_kernel(bound_ref, q_ref, k_ref, v_ref, ff_ref, g_ref, bf_ref, o_ref,
                        kaug_sc, vext_sc, c_sc, ps_a, ps_b, acc_sc):
    p = pl.program_id(1)
    i = pl.program_id(2)
    tq = q_ref.shape[1]

    @pl.when(i == 0)
    def _prep():
        pl.when(p == 0)(lambda: _forget_cumsum(ff_ref, bf_ref, c_sc))
        kn = k_ref[0]
        c_all = c_sc[...] * LOG2E
        v = v_ref[0]
        lane = _lane_iota(v.shape)
        for hh in range(2):
            a0 = HEAD_DIM if hh == 0 else 0
            hi, mid, lo = _split3(-_lane_column(c_all, 2 * p + hh))
            aug = _lane_fields(kn.shape, a0, [hi, mid, lo, 1.0, 1.0, 1.0, 1.0, 1.0])
            own = _own_half(hh, kn.shape)
            kaug_sc[hh] = jnp.where(own, kn, aug).astype(_MXU_DTYPE)
            vext_sc[hh, :, 0:LANES] = jnp.where(own, v, 0.0).astype(_MXU_DTYPE)
            vext_sc[hh, :, LANES:2 * LANES] = jnp.where(lane == hh, 1.0, 0.0).astype(_MXU_DTYPE)

    q = q_ref[0] * (ATTN_SCALE * LOG2E)
    row0 = pl.multiple_of(i * tq, tq)
    c_t = c_sc[pl.ds(row0, tq), :] * LOG2E
    off = jnp.full((1, 1), -LOG2E, _F32) * bound_ref[0]
    off_hi = off.astype(_MXU_DTYPE).astype(_F32)
    q_augs = []
    for hh in range(2):
        a0 = HEAD_DIM if hh == 0 else 0
        hi, mid, lo = _split3(_lane_column(c_t, 2 * p + hh))
        aug = _lane_fields(q.shape, a0, [1.0, 1.0, 1.0, hi, mid, lo, off_hi, off - off_hi])
        q_augs.append(jnp.where(_own_half(hh, q.shape), q, aug).astype(_MXU_DTYPE))

    lane = _lane_iota((1, LANES))
    pair_lanes = (lane == 2 * p) | (lane == 2 * p + 1)
    c_first = c_sc[pl.ds(row0, 1), :]
    cutoff = EXP_ZERO_CUTOFF - 2.0 * bound_ref[0]

    def live(j):
        c_last = c_sc[pl.ds(jnp.maximum(j, 0) * tq + (tq - 1), 1), :]
        gap = jnp.max(jnp.where(pair_lanes, c_first - c_last, -jnp.inf))
        return (j >= 0) & (gap >= cutoff)

    ps_bufs = (ps_a, ps_b)

    def values_at(r):
        return lambda ps_ref: _bounded_values(ps_ref, vext_sc, pl.multiple_of(r, tq), acc_sc)

    def body(state):
        j, _, r_prev, n = state
        r = pl.multiple_of(j * tq, tq)
        _bounded_pipeline_step(
            n, ps_bufs,
            lambda ps_ref: _bounded_weights(ps_ref, q_augs, kaug_sc, r, None, False),
            values_at(r_prev))
        return j - 1, live(j - 1), r, n + 1

    acc_sc[...] = jnp.zeros(acc_sc.shape, _F32)
    _bounded_weights(ps_a, q_augs, kaug_sc, row0, None, True)
    state = lax.while_loop(lambda state: state[1], body,
                           (i - 1, live(i - 1), row0, jnp.int32(1)))
    _bounded_finish(state[3] - 1, ps_bufs, values_at(state[2]))
    acc = acc_sc[...]
    den = acc[:, LANES:]
    l = jnp.where(_own_half(0, (tq, LANES)), _lane_column(den, 0), _lane_column(den, 1))
    o_ref[0] = (acc[:, :LANES] / l * _silu(g_ref[0])).astype(o_ref.dtype)


def _fox_bounded(proj, bf_row, bound, *, tq=512):
    b, t, _ = proj.shape
    return pl.pallas_call(
        _fox_bounded_kernel,
        grid=(b, 2, t // tq),
        in_specs=[pl.BlockSpec(memory_space=pltpu.SMEM),
                  pl.BlockSpec((1, tq, LANES), lambda bb, p, i: (bb, i, COL_FQ + p)),
                  pl.BlockSpec((1, t, LANES), lambda bb, p, i: (bb, 0, COL_FK + p)),
                  pl.BlockSpec((1, t, LANES), lambda bb, p, i: (bb, 0, COL_FV + p)),
                  pl.BlockSpec((1, t, LANES), lambda bb, p, i: (bb, 0, COL_FF)),
                  pl.BlockSpec((1, tq, LANES), lambda bb, p, i: (bb, i, COL_FG + p)),
                  pl.BlockSpec((1, LANES), lambda bb, p, i: (0, 0))],
        out_specs=pl.BlockSpec((1, tq, LANES), lambda bb, p, i: (bb, i, p)),
        out_shape=jax.ShapeDtypeStruct((b, t, GROUP_WIDTH), _MXU_DTYPE),
        scratch_shapes=[pltpu.VMEM((2, t, LANES), _MXU_DTYPE),
                        pltpu.VMEM((2, t, 2 * LANES), _MXU_DTYPE),
                        pltpu.VMEM((t, LANES), _F32),
                        pltpu.VMEM((2, tq, tq), _MXU_DTYPE),
                        pltpu.VMEM((2, tq, tq), _MXU_DTYPE),
                        pltpu.VMEM((tq, 2 * LANES), _F32)],
        compiler_params=pltpu.CompilerParams(
            dimension_semantics=("arbitrary", "arbitrary", "arbitrary"),
            vmem_limit_bytes=_VMEM_LIMIT),
        name="fox_bounded",
    )(bound, proj, proj, proj, proj, proj, bf_row)


def _fox_kernel(q_ref, k_ref, v_ref, ff_ref, g_ref, bf_ref, o_ref,
                kaug_sc, v_sc, c_sc, *, tq):
    p = pl.program_id(1)
    i = pl.program_id(2)

    def head_column(x, hh):
        return _lane_column(x, 2 * p + hh)

    @pl.when(i == 0)
    def _prep():
        pl.when(p == 0)(lambda: _forget_cumsum(ff_ref, bf_ref, c_sc))
        kn = k_ref[0]
        c_all = c_sc[...]
        lane = _lane_iota(kn.shape)
        for hh in range(2):
            a0 = HEAD_DIM if hh == 0 else 0
            hi, mid, lo = _split3(-head_column(c_all, hh))
            aug = jnp.where(lane == a0, hi,
                            jnp.where(lane == a0 + 1, mid,
                                      jnp.where(lane == a0 + 2, lo,
                                                jnp.where((lane >= a0 + 3) & (lane < a0 + 6), 1.0, 0.0))))
            kaug_sc[hh] = jnp.where(_own_half(hh, kn.shape), kn, aug).astype(_MXU_DTYPE)
        v_sc[...] = v_ref[0].astype(_MXU_DTYPE)

    q = q_ref[0] * ATTN_SCALE
    row0 = pl.multiple_of(i * tq, tq)
    c_t = c_sc[pl.ds(row0, tq), :]
    lane = _lane_iota(q.shape)
    outs = []
    for hh in range(2):
        a0 = HEAD_DIM if hh == 0 else 0
        hi, mid, lo = _split3(head_column(c_t, hh))
        aug = jnp.where((lane >= a0) & (lane < a0 + 3), 1.0,
                        jnp.where(lane == a0 + 3, hi,
                                  jnp.where(lane == a0 + 4, mid,
                                            jnp.where(lane == a0 + 5, lo, 0.0))))
        q_aug = jnp.where(_own_half(hh, q.shape), q, aug).astype(_MXU_DTYPE)

        s = lax.dot_general(q_aug, kaug_sc[hh, pl.ds(row0, tq), :], _NT,
                            preferred_element_type=_F32)
        s = jnp.where(_causal(tq), s, NEG_INF)
        m = jnp.max(s, axis=-1, keepdims=True)
        pr = jnp.exp(s - m)
        l = jnp.sum(pr, axis=-1, keepdims=True)
        acc = jnp.dot(pr.astype(_MXU_DTYPE), v_sc[pl.ds(row0, tq), :], preferred_element_type=_F32)

        def body(j, carry, q_aug=q_aug, hh=hh):
            r = pl.multiple_of(j * tq, tq)
            s = lax.dot_general(q_aug, kaug_sc[hh, pl.ds(r, tq), :], _NT,
                                preferred_element_type=_F32)
            return _flash_update(s, v_sc[pl.ds(r, tq), :], *carry)

        m, l, acc = lax.fori_loop(0, i, body, (m, l, acc))
        outs.append(acc / l)
    o = jnp.where(_own_half(0, outs[0].shape), outs[0], outs[1])
    o_ref[0] = (o * _silu(g_ref[0])).astype(o_ref.dtype)


def _fox(proj, bf_row, *, tq=256):
    b, t, _ = proj.shape
    return pl.pallas_call(
        functools.partial(_fox_kernel, tq=tq),
        grid=(b, 2, t // tq),
        in_specs=[pl.BlockSpec((1, tq, LANES), lambda bb, p, i: (bb, i, COL_FQ + p)),
                  pl.BlockSpec((1, t, LANES), lambda bb, p, i: (bb, 0, COL_FK + p)),
                  pl.BlockSpec((1, t, LANES), lambda bb, p, i: (bb, 0, COL_FV + p)),
                  pl.BlockSpec((1, t, LANES), lambda bb, p, i: (bb, 0, COL_FF)),
                  pl.BlockSpec((1, tq, LANES), lambda bb, p, i: (bb, i, COL_FG + p)),
                  pl.BlockSpec((1, LANES), lambda bb, p, i: (0, 0))],
        out_specs=pl.BlockSpec((1, tq, LANES), lambda bb, p, i: (bb, i, p)),
        out_shape=jax.ShapeDtypeStruct((b, t, GROUP_WIDTH), _MXU_DTYPE),
        scratch_shapes=[pltpu.VMEM((2, t, LANES), _MXU_DTYPE),
                        pltpu.VMEM((t, LANES), _MXU_DTYPE),
                        pltpu.VMEM((t, LANES), _F32)],
        compiler_params=pltpu.CompilerParams(
            dimension_semantics=("arbitrary", "arbitrary", "arbitrary"),
            vmem_limit_bytes=_VMEM_LIMIT),
        name="fox",
    )(proj, proj, proj, proj, proj, bf_row)


def _moba_block_means(kn, kmean_sc):
    kmean_sc[...] = jnp.zeros(kmean_sc.shape, _F32)
    for n in range(kn.shape[0] // MOBA_BLOCK):
        kmean_sc[n:n + 1, :] = jnp.mean(kn[n * MOBA_BLOCK:(n + 1) * MOBA_BLOCK], axis=0, keepdims=True)


def _moba_select(q_head, kmean, past):
    gate = lax.dot_general(q_head, kmean, _NT, precision=lax.Precision.HIGHEST,
                           preferred_element_type=_F32)
    lane_f = _lane_iota(gate.shape).astype(_F32)
    cand = jnp.where(past, gate, -jnp.inf)
    sel = jnp.zeros(gate.shape, _F32)
    for _ in range(MOBA_TOPK):
        mx = jnp.max(cand, axis=-1, keepdims=True)
        is_max = (cand == mx) & (mx > -jnp.inf)
        first = jnp.min(jnp.where(is_max, lane_f, float(LANES)), axis=-1, keepdims=True)
        pick = lane_f == first
        sel = jnp.where(pick, 1.0, sel)
        cand = jnp.where(pick, -jnp.inf, cand)
    return sel


def _moba_bounded_kernel(rb_ref, bound_ref, q_ref, k_ref, v_ref, g_ref, bias_ref, o_ref,
                         kaug_sc, vext_sc, kmean_sc, ps_a, ps_b, acc_sc):
    p = pl.program_id(1)
    i = pl.program_id(2)
    tq = q_ref.shape[1]
    blk = MOBA_BLOCK
    near = MOBA_WIDE_NEAR_TILES

    @pl.when(i == 0)
    def _prep():
        kn = k_ref[0]
        _moba_block_means(kn, kmean_sc)
        v = v_ref[0]
        lane = _lane_iota(kn.shape)
        row_blk = lax.broadcasted_iota(jnp.int32, kn.shape, 0) // blk
        for hh in range(2):
            a0 = HEAD_DIM if hh == 0 else 0
            aug = jnp.where((lane - a0 == row_blk) | (lane - a0 - 16 == row_blk)
                            | (lane - a0 == 32) | (lane - a0 == 33), 1.0, 0.0)
            own = _own_half(hh, kn.shape)
            kaug_sc[hh] = jnp.where(own, kn, aug).astype(_MXU_DTYPE)
            vext_sc[hh, :, 0:LANES] = jnp.where(own, v, 0.0).astype(_MXU_DTYPE)
            vext_sc[hh, :, LANES:2 * LANES] = jnp.where(lane == hh, 1.0, 0.0).astype(_MXU_DTYPE)

    qn = q_ref[0]
    row0 = pl.multiple_of(i * tq, tq)
    off = jnp.full((1, 1), -LOG2E, _F32) * bound_ref[0]
    off_hi = off.astype(_MXU_DTYPE).astype(_F32)
    blk_n = lax.broadcasted_iota(jnp.int32, (16, tq), 0)
    blk_f = blk_n.astype(_F32)
    q_blk = i * (tq // blk) + lax.broadcasted_iota(jnp.int32, (16, tq), 1) // blk
    past = blk_n < q_blk
    far = i - blk_n // 2 >= near
    sub8 = lax.broadcasted_iota(jnp.int32, (8, tq), 0)
    off_rows = jnp.where(sub8 == 0, off_hi, jnp.where(sub8 == 1, off - off_hi, 0.0))
    q_augs = []
    for hh in range(2):
        own = _own_half(hh, qn.shape)
        gate = lax.dot_general(kmean_sc[0:16, :], jnp.where(own, qn, 0.0), _NT,
                               precision=lax.Precision.HIGHEST, preferred_element_type=_F32)
        cand = jnp.where(past, gate, -jnp.inf)
        sel = jnp.zeros((16, tq), _F32)
        for _ in range(MOBA_TOPK):
            mx = jnp.max(cand, axis=0, keepdims=True)
            is_max = (cand == mx) & (mx > -jnp.inf)
            first = jnp.min(jnp.where(is_max, blk_f, float(LANES)), axis=0, keepdims=True)
            pick = blk_f == first
            sel = jnp.where(pick, 1.0, sel)
            cand = jnp.where(pick, -jnp.inf, cand)
        c_far = jnp.full((1, 1), LOG2E, _F32) * rb_ref[NUM_BUCKETS - 1, 2 * p + hh]
        far_hi = c_far.astype(_MXU_DTYPE).astype(_F32)
        pen = jnp.where((blk_n == q_blk) | (past & (sel != 0.0)), 0.0, NEG_INF)
        fields = jnp.concatenate([pen + jnp.where(far, far_hi, 0.0),
                                  jnp.where(far, c_far - far_hi, 0.0),
                                  off_rows,
                                  jnp.zeros((HEAD_DIM - 40, tq), _F32)], axis=0)
        blank = jnp.zeros((HEAD_DIM, tq), _F32)
        aug = jnp.concatenate([blank, fields] if hh == 0 else [fields, blank], axis=0).T
        q_augs.append(jnp.where(own, qn * (ATTN_SCALE * LOG2E), aug).astype(_MXU_DTYPE))

    ps_bufs = (ps_a, ps_b)

    def values_at(r):
        return lambda ps_ref: _bounded_values(ps_ref, vext_sc, pl.multiple_of(r, tq), acc_sc)

    def body(j, state, with_bias):
        r_prev, n = state
        r = pl.multiple_of(j * tq, tq)
        bias = (lambda hh: bias_ref[hh, i - j]) if with_bias else None
        _bounded_pipeline_step(
            n, ps_bufs,
            lambda ps_ref: _bounded_weights(ps_ref, q_augs, kaug_sc, r, bias, False),
            values_at(r_prev))
        return r, n + 1

    acc_sc[...] = jnp.zeros(acc_sc.shape, _F32)
    _bounded_weights(ps_a, q_augs, kaug_sc, row0, lambda hh: bias_ref[hh, 0], True)
    state = (row0, jnp.int32(1))
    near_lo = jnp.maximum(i - (near - 1), 0)
    state = lax.fori_loop(near_lo, i, functools.partial(body, with_bias=True), state)
    state = lax.fori_loop(0, near_lo, functools.partial(body, with_bias=False), state)
    _bounded_finish(state[1] - 1, ps_bufs, values_at(state[0]))
    acc = acc_sc[...]
    den = acc[:, LANES:]
    l = jnp.where(_own_half(0, (tq, LANES)), _lane_column(den, 0), _lane_column(den, 1))
    o_ref[0] = (acc[:, :LANES] / l * _silu(g_ref[0])).astype(o_ref.dtype)


def _moba_bounded(proj, rel_bias, bias_tiles, bound):
    b, t, _ = proj.shape
    tq = MOBA_WIDE
    assert t % tq == 0 and t // MOBA_BLOCK <= 16
    return pl.pallas_call(
        _moba_bounded_kernel,
        grid=(b, 2, t // tq),
        in_specs=[pl.BlockSpec(memory_space=pltpu.SMEM),
                  pl.BlockSpec(memory_space=pltpu.SMEM),
                  pl.BlockSpec((1, tq, LANES), lambda bb, p, i: (bb, i, COL_MQ + p)),
                  pl.BlockSpec((1, t, LANES), lambda bb, p, i: (bb, 0, COL_MK + p)),
                  pl.BlockSpec((1, t, LANES), lambda bb, p, i: (bb, 0, COL_MV + p)),
                  pl.BlockSpec((1, tq, LANES), lambda bb, p, i: (bb, i, COL_MG + p)),
                  pl.BlockSpec((2, MOBA_WIDE_NEAR_TILES, tq, tq), lambda bb, p, i: (p, 0, 0, 0))],
        out_specs=pl.BlockSpec((1, tq, LANES), lambda bb, p, i: (bb, i, p)),
        out_shape=jax.ShapeDtypeStruct((b, t, GROUP_WIDTH), _MXU_DTYPE),
        scratch_shapes=[pltpu.VMEM((2, t, LANES), _MXU_DTYPE),
                        pltpu.VMEM((2, t, 2 * LANES), _MXU_DTYPE),
                        pltpu.VMEM((LANES, LANES), _F32),
                        pltpu.VMEM((2, tq, tq), _MXU_DTYPE),
                        pltpu.VMEM((2, tq, tq), _MXU_DTYPE),
                        pltpu.VMEM((tq, 2 * LANES), _F32)],
        compiler_params=pltpu.CompilerParams(
            dimension_semantics=("arbitrary", "arbitrary", "arbitrary"),
            vmem_limit_bytes=_VMEM_LIMIT),
        name="moba_bounded",
    )(rel_bias, bound, proj, proj, proj, proj, bias_tiles)


def _moba_kernel(rb_ref, q_ref, k_ref, v_ref, g_ref, bias_ref, o_ref,
                 kaug_sc, v_sc, kmean_sc):
    p = pl.program_id(1)
    i = pl.program_id(2)
    t_len = k_ref.shape[1]
    blk = MOBA_BLOCK
    nblk = t_len // blk
    near = MOBA_NEAR_TILES

    @pl.when(i == 0)
    def _prep():
        kn = k_ref[0]
        _moba_block_means(kn, kmean_sc)
        lane = _lane_iota(kn.shape)
        row_blk = lax.broadcasted_iota(jnp.int32, kn.shape, 0) // blk
        for hh in range(2):
            a0 = HEAD_DIM if hh == 0 else 0
            onehot = jnp.where((lane - a0 == row_blk) | (lane - a0 - 16 == row_blk), 1.0, 0.0)
            kaug_sc[hh] = jnp.where(_own_half(hh, kn.shape), kn, onehot).astype(_MXU_DTYPE)
        v_sc[...] = v_ref[0].astype(_MXU_DTYPE)

    qn = q_ref[0]
    row0 = pl.multiple_of(i * blk, blk)
    lane = _lane_iota(qn.shape)
    lane_f = lane.astype(_F32)
    past = lane < i
    outs = []
    for hh in range(2):
        own = _own_half(hh, qn.shape)
        sel = _moba_select(jnp.where(own, qn, 0.0), kmean_sc[...], past)
        c_far = jnp.full((1, LANES), rb_ref[NUM_BUCKETS - 1, 2 * p + hh], _F32)
        far_hi = c_far.astype(_MXU_DTYPE).astype(_F32)
        far_lo = c_far - far_hi
        pen = jnp.where(past & (sel == 0.0), NEG_INF, 0.0)
        aug = jnp.where(lane < 16, pen + jnp.where(i - lane >= near, far_hi, 0.0),
                        jnp.where((lane < 32) & (i - (lane - 16) >= near), far_lo, 0.0))
        if hh == 0:
            aug = pltpu.roll(aug, HEAD_DIM, axis=1)
        q_aug = jnp.where(own, qn * ATTN_SCALE, aug).astype(_MXU_DTYPE)

        def scores(r, q_aug=q_aug, hh=hh):
            return lax.dot_general(q_aug, kaug_sc[hh, pl.ds(r, blk), :], _NT,
                                   preferred_element_type=_F32)

        s = scores(row0) + bias_ref[hh, 0]
        s = jnp.where(_causal(blk), s, NEG_INF)
        m = jnp.max(s, axis=-1, keepdims=True)
        pr = jnp.exp(s - m)
        l = jnp.sum(pr, axis=-1, keepdims=True)
        acc = jnp.dot(pr.astype(_MXU_DTYPE), v_sc[pl.ds(row0, blk), :], preferred_element_type=_F32)

        def near_body(j, carry, hh=hh, scores=scores):
            r = pl.multiple_of(j * blk, blk)
            return _flash_update(scores(r) + bias_ref[hh, i - j], v_sc[pl.ds(r, blk), :], *carry)

        def far_body(j, carry, scores=scores):
            r = pl.multiple_of(j * blk, blk)
            return _flash_update(scores(r), v_sc[pl.ds(r, blk), :], *carry)

        near_lo = jnp.maximum(i - (near - 1), 0)
        carry = lax.fori_loop(near_lo, i, near_body, (m, l, acc))
        m, l, acc = lax.fori_loop(0, near_lo, far_body, carry)
        outs.append(acc / l)
    o = jnp.where(_own_half(0, outs[0].shape), outs[0], outs[1])
    o_ref[0] = (o * _silu(g_ref[0])).astype(o_ref.dtype)


def _moba(proj, rel_bias, bias_tiles):
    b, t, _ = proj.shape
    blk = MOBA_BLOCK
    assert t % blk == 0 and t // blk <= 16
    return pl.pallas_call(
        _moba_kernel,
        grid=(b, 2, t // blk),
        in_specs=[pl.BlockSpec(memory_space=pltpu.SMEM),
                  pl.BlockSpec((1, blk, LANES), lambda bb, p, i: (bb, i, COL_MQ + p)),
                  pl.BlockSpec((1, t, LANES), lambda bb, p, i: (bb, 0, COL_MK + p)),
                  pl.BlockSpec((1, t, LANES), lambda bb, p, i: (bb, 0, COL_MV + p)),
                  pl.BlockSpec((1, blk, LANES), lambda bb, p, i: (bb, i, COL_MG + p)),
                  pl.BlockSpec((2, MOBA_NEAR_TILES, blk, blk), lambda bb, p, i: (p, 0, 0, 0))],
        out_specs=pl.BlockSpec((1, blk, LANES), lambda bb, p, i: (bb, i, p)),
        out_shape=jax.ShapeDtypeStruct((b, t, GROUP_WIDTH), _MXU_DTYPE),
        scratch_shapes=[pltpu.VMEM((2, t, LANES), _MXU_DTYPE),
                        pltpu.VMEM((t, LANES), _MXU_DTYPE),
                        pltpu.VMEM((LANES, LANES), _F32)],
        compiler_params=pltpu.CompilerParams(
            dimension_semantics=("arbitrary", "arbitrary", "arbitrary"),
            vmem_limit_bytes=_VMEM_LIMIT),
        name="moba",
    )(rel_bias, proj, proj, proj, proj, bias_tiles)


def _sb_kernel(q_ref, k_ref, v_ref, g_ref, o_ref, k_sc, v_sc, *, tq):
    i = pl.program_id(2)

    @pl.when(i == 0)
    def _prep():
        k_sc[...] = k_ref[0].astype(_MXU_DTYPE)
        v = v_ref[0]
        for hh in range(2):
            v_sc[hh] = jnp.where(_own_half(hh, v.shape), v, 0.0).astype(_MXU_DTYPE)

    subs = q_ref.shape[1] // tq
    q_heads = []
    for u in range(subs):
        q = q_ref[0, u * tq:(u + 1) * tq, :] * (ATTN_SCALE * LOG2E)
        q_heads.append([jnp.where(_own_half(hh, q.shape), q, 0.0).astype(_MXU_DTYPE)
                        for hh in range(2)])
    strict =(lax.broadcasted_iota(jnp.int32, (tq, tq), 0)
              > lax.broadcasted_iota(jnp.int32, (tq, tq), 1))
    after = strict.astype(_MXU_DTYPE)
    sign_bit = jnp.uint32(0x80000000)

    def pair(back, runs, diagonal):
        chains = [(u, t, hh) for u in range(subs) for t in range(2) for hh in range(2)]
        tile_idx = [[subs * i + u - back - t for t in range(2)] for u in range(subs)]
        valid = [[(j >= 0).astype(_F32) for j in js] for js in tile_idx]
        rows = [[pl.multiple_of(jnp.maximum(j, 0) * tq, tq) for j in js] for js in tile_idx]
        masked = [diagonal and t == 0 for _, t, _ in chains]
        zs = [lax.dot_general(q_heads[u][hh], k_sc[pl.ds(rows[u][t], tq), :], _NT,
                              preferred_element_type=_F32) for u, t, hh in chains]
        drops = []
        for z, msk in zip(zs, masked):
            neg_abs = lax.bitcast_convert_type(lax.bitcast_convert_type(z, jnp.uint32) | sign_bit, _F32)
            drop = jnp.maximum(z, 0.0) + jnp.log(1.0 + jnp.exp2(neg_abs)) * LOG2E
            drops.append(jnp.where(strict, drop, 0.0) if msk else drop)
        laters = []
        for drop in drops:
            hi = drop.astype(_MXU_DTYPE)
            lo = (drop - hi.astype(_F32)).astype(_MXU_DTYPE)
            laters.append(jnp.dot(hi, after, preferred_element_type=_F32)
                          + jnp.dot(lo, after, preferred_element_type=_F32))
        pvs = []
        for (u, t, hh), z, drop, later, msk in zip(chains, zs, drops, laters, masked):
            w = jnp.exp2(z - drop - later)
            if msk:
                w = jnp.where(strict, w, 0.0)
            pvs.append(jnp.dot(w.astype(_MXU_DTYPE), v_sc[hh, pl.ds(rows[u][t], tq), :],
                               preferred_element_type=_F32))
        sums = [jnp.sum(drop, axis=-1, keepdims=True) for drop in drops]
        new_runs, outs = [], []
        for u in range(subs):
            new_runs.append([])
            out = None
            for hh in range(2):
                run = runs[u][hh]
                for t in range(2):
                    c = chains.index((u, t, hh))
                    pv = pvs[c] * (jnp.exp2(run) * valid[u][t])
                    out = pv if out is None else out + pv
                    run = run - sums[c] * valid[u][t]
                new_runs[u].append(run)
            outs.append(out)
        return new_runs, outs

    def alive(runs):
        top = functools.reduce(jnp.maximum, [r for rs in runs for r in rs])
        return jnp.max(top) >= EXP_ZERO_CUTOFF * LOG2E

    def flat(runs):
        return [r for rs in runs for r in rs]

    def nested(flat_runs):
        return [flat_runs[2 * u:2 * u + 2] for u in range(subs)]

    zero = jnp.zeros((tq, 1), _F32)
    runs, accs = pair(0, [[zero, zero] for _ in range(subs)], True)

    def body(state):
        back = state[0]
        runs, outs = pair(back, nested(list(state[2:2 + 2 * subs])), False)
        accs = [a + o for a, o in zip(state[2 + 2 * subs:], outs)]
        return (back + 2, alive(runs), *flat(runs), *accs)

    state = lax.while_loop(lambda state: (subs * i + subs - 1 - state[0] >= 0) & state[1], body,
                           (jnp.int32(2), alive(runs), *flat(runs), *accs))
    for u in range(subs):
        o_ref[0, u * tq:(u + 1) * tq, :] = (
            state[2 + 2 * subs + u] * _silu(g_ref[0, u * tq:(u + 1) * tq, :])).astype(o_ref.dtype)


def _sb(proj, *, tq=256, subs=2):
    b, t, _ = proj.shape
    rows = tq * subs
    return pl.pallas_call(
        functools.partial(_sb_kernel, tq=tq),
        grid=(b, 2, t // rows),
        in_specs=[pl.BlockSpec((1, rows, LANES), lambda bb, p, i: (bb, i, COL_SQ + p)),
                  pl.BlockSpec((1, t, LANES), lambda bb, p, i: (bb, 0, COL_SK + p)),
                  pl.BlockSpec((1, t, LANES), lambda bb, p, i: (bb, 0, COL_SV + p)),
                  pl.BlockSpec((1, rows, LANES), lambda bb, p, i: (bb, i, COL_SG + p))],
        out_specs=pl.BlockSpec((1, rows, LANES), lambda bb, p, i: (bb, i, p)),
        out_shape=jax.ShapeDtypeStruct((b, t, GROUP_WIDTH), _MXU_DTYPE),
        scratch_shapes=[pltpu.VMEM((t, LANES), _MXU_DTYPE),
                        pltpu.VMEM((2, t, LANES), _MXU_DTYPE)],
        compiler_params=pltpu.CompilerParams(
            dimension_semantics=("arbitrary", "arbitrary", "arbitrary"),
            vmem_limit_bytes=_VMEM_LIMIT),
        name="stickbreak",
    )(proj, proj, proj, proj)


def _swa_kernel(sink_ref, q_ref, k_ref, v_ref, g_ref, bias_ref, o_ref,
                k_sc, v_sc, *, tq):
    kv = pl.program_id(1)
    i = pl.program_id(2)
    t_len = k_ref.shape[1]
    w = SWA_WINDOW

    @pl.when(i == 0)
    def _prep():
        kn = k_ref[0]
        v = v_ref[0]
        keep = (_lane_iota(kn.shape) < HEAD_DIM) == (kv == 0)
        k_sc[0:w, :] = jnp.zeros((w, LANES), _MXU_DTYPE)
        v_sc[0:w, :] = jnp.zeros((w, LANES), _MXU_DTYPE)
        k_sc[w:w + t_len, :] = jnp.where(keep, kn, pltpu.roll(kn, HEAD_DIM, axis=1)).astype(_MXU_DTYPE)
        v_sc[w:w + t_len, :] = jnp.where(keep, v, pltpu.roll(v, HEAD_DIM, axis=1)).astype(_MXU_DTYPE)

    q = q_ref[0] * ATTN_SCALE
    qi = lax.broadcasted_iota(jnp.int32, (w, 2 * w), 0)
    kj = lax.broadcasted_iota(jnp.int32, (w, 2 * w), 1)
    dist = qi + w - kj
    in_window = (dist >= 0) & (dist < w)
    for u in range(tq // w):
        nb = i * (tq // w) + u
        r = pl.multiple_of(nb * w, w)
        k_t = k_sc[pl.ds(r, 2 * w), :]
        v_t = v_sc[pl.ds(r, 2 * w), :]
        allowed = in_window & (kj + (nb - 1) * w >= 0)
        q_u = q[u * w:(u + 1) * w]
        outs = []
        for g in range(2):
            q_h = jnp.where(_own_half(g, q_u.shape), q_u, 0.0).astype(_MXU_DTYPE)
            s = lax.dot_general(q_h, k_t, _NT, preferred_element_type=_F32)
            s = jnp.where(allowed, s + bias_ref[g], NEG_INF)
            sink = sink_ref[2 * kv + g]
            m = jnp.maximum(jnp.max(s, axis=-1, keepdims=True), sink)
            e = jnp.exp(s - m)
            den = jnp.sum(e, axis=-1, keepdims=True) + jnp.exp(sink - m)
            outs.append(jnp.dot(e.astype(_MXU_DTYPE), v_t, preferred_element_type=_F32) / den)
        o = jnp.where(_own_half(0, outs[0].shape), outs[0], outs[1])
        o_ref[0, u * w:(u + 1) * w, :] = (o * _silu(g_ref[0, u * w:(u + 1) * w, :])).astype(o_ref.dtype)


def _swa(proj, sinks, bias_tiles, *, tq=512):
    b, t, _ = proj.shape
    w = SWA_WINDOW
    return pl.pallas_call(
        functools.partial(_swa_kernel, tq=tq),
        grid=(b, 2, t // tq),
        in_specs=[pl.BlockSpec(memory_space=pltpu.SMEM),
                  pl.BlockSpec((1, tq, LANES), lambda bb, kv, i: (bb, i, COL_WQ + kv)),
                  pl.BlockSpec((1, t, LANES), lambda bb, kv, i: (bb, 0, COL_WK)),
                  pl.BlockSpec((1, t, LANES), lambda bb, kv, i: (bb, 0, COL_WV)),
                  pl.BlockSpec((1, tq, LANES), lambda bb, kv, i: (bb, i, COL_WG + kv)),
                  pl.BlockSpec((2, w, 2 * w), lambda bb, kv, i: (kv, 0, 0))],
        out_specs=pl.BlockSpec((1, tq, LANES), lambda bb, kv, i: (bb, i, kv)),
        out_shape=jax.ShapeDtypeStruct((b, t, GROUP_WIDTH), _MXU_DTYPE),
        scratch_shapes=[pltpu.VMEM((t + w, LANES), _MXU_DTYPE),
                        pltpu.VMEM((t + w, LANES), _MXU_DTYPE)],
        compiler_params=pltpu.CompilerParams(
            dimension_semantics=("arbitrary", "arbitrary", "arbitrary"),
            vmem_limit_bytes=_VMEM_LIMIT),
        name="swa",
    )(sinks, proj, proj, proj, proj, bias_tiles)


def _outproj_kernel(x_ref, ya_ref, yb_ref, yc_ref, yd_ref, w_ref, o_ref):
    acc = x_ref[...]
    for g, y_ref in enumerate((ya_ref, yb_ref, yc_ref, yd_ref)):
        acc = acc + jnp.dot(y_ref[...], w_ref[g * GROUP_WIDTH:(g + 1) * GROUP_WIDTH, :],
                            preferred_element_type=_F32)
    o_ref[...] = acc


def _outproj(x2d, ys, w_all, layer):
    m, d = x2d.shape
    tm = 512
    y_spec = pl.BlockSpec((tm, GROUP_WIDTH), lambda i: (i, 0))
    return pl.pallas_call(
        _outproj_kernel,
        grid=(m // tm,),
        in_specs=[pl.BlockSpec((tm, d), lambda i: (i, 0)), y_spec, y_spec, y_spec, y_spec,
                  pl.BlockSpec((None,) + w_all.shape[1:], lambda i: (layer, 0, 0))],
        out_specs=pl.BlockSpec((tm, d), lambda i: (i, 0)),
        out_shape=jax.ShapeDtypeStruct((m, d), _F32),
        compiler_params=pltpu.CompilerParams(
            dimension_semantics=("arbitrary",), vmem_limit_bytes=_VMEM_LIMIT),
        name="outproj",
    )(x2d, *ys, w_all)


def _rearrange_w_in(w_in):
    off = 3 * GROUP_WIDTH
    main = jnp.concatenate([w_in[..., :off], w_in[..., off + FOX_FORGET_COLS:]], axis=-1)
    ff = w_in[..., off:off + FOX_FORGET_COLS]
    pad = PROJ_WIDTH - main.shape[-1] - FOX_FORGET_COLS
    return jnp.concatenate([main, ff, jnp.zeros(w_in.shape[:-1] + (pad,), w_in.dtype)], axis=-1)


def _qk_logit_bound(qk_gain):
    g = jnp.abs(qk_gain.astype(_F32))
    return ATTN_SCALE * HEAD_DIM * jnp.max(g[0]) * jnp.max(g[1])


def _qk_gain_row(fox_g, moba_g, swa_g):
    def tiles(g, n):
        return jnp.tile(g.astype(_F32), 2 * n)
    row = jnp.zeros((PROJ_WIDTH,), _F32)
    for col, g, n in ((COL_FQ, fox_g[0], 2), (COL_FK, fox_g[1], 2),
                      (COL_MQ, moba_g[0], 2), (COL_MK, moba_g[1], 2),
                      (COL_WQ, swa_g[0], 2), (COL_WK, swa_g[1], 1)):
        row = lax.dynamic_update_slice(row, tiles(g, n), (col * LANES,))
    return row[None, :]


def kernel(x, norm_gain, w_in, b_forget, fox_qk_gain, moba_qk_gain, swa_qk_gain, sinks, w_out, rel_bias):
    b, t, d = x.shape
    depth = w_in.shape[0]
    w_in_r = _rearrange_w_in(w_in).astype(_MXU_DTYPE)
    w_out_c = w_out.astype(_MXU_DTYPE)
    moba_bias, moba_wide_bias, swa_bias = _bias_tiles(rel_bias)
    moba_bias_max = jnp.max(jnp.abs(rel_bias[:, :rel_bias.shape[1] // 2].astype(_F32)))
    x2d = x.reshape(b * t, d)
    for layer in range(depth):
        qk_gain_row = _qk_gain_row(fox_qk_gain[layer], moba_qk_gain[layer], swa_qk_gain[layer])
        proj = _inproj(x2d, norm_gain[layer][None, :], w_in_r, layer, qk_gain_row)
        proj = proj.reshape(b, t, PROJ_WIDTH)
        bf_row = jnp.pad(b_forget[layer], (0, LANES - FOX_FORGET_COLS))[None, :]
        fox_bound = _qk_logit_bound(fox_qk_gain[layer])
        y_fox = lax.cond(2.0 * fox_bound <= BOUNDED_SOFTMAX_RANGE,
                         lambda: _fox_bounded(proj, bf_row, fox_bound.reshape(1)),
                         lambda: _fox(proj, bf_row))
        moba_bound = _qk_logit_bound(moba_qk_gain[layer]) + moba_bias_max
        y_moba = lax.cond(2.0 * moba_bound <= BOUNDED_SOFTMAX_RANGE,
                          lambda: _moba_bounded(proj, rel_bias, moba_wide_bias,
                                                moba_bound.reshape(1)),
                          lambda: _moba(proj, rel_bias, moba_bias))
        y_sb = _sb(proj)
        y_swa = _swa(proj, sinks[layer], swa_bias)
        ys = [y.reshape(b * t, GROUP_WIDTH) for y in (y_fox, y_moba, y_sb, y_swa)]
        x2d = _outproj(x2d, ys, w_out_c, layer)
    return x2d.reshape(b, t, d)
```

```python
import functools
import math

import jax
import jax.numpy as jnp
from jax import lax
from jax.experimental import pallas as pl
from jax.experimental.pallas import tpu as pltpu

HEAD_DIM = 64
LANES = 128
GROUP_WIDTH = 256
MOBA_BLOCK = 256
MOBA_TOPK = 3
SWA_WINDOW = 128
NUM_BUCKETS = 32
REL_MAX_DISTANCE = 1024
RMS_EPS = 1e-6
NEG_INF = -1e30
ATTN_SCALE = HEAD_DIM ** -0.5
LOG2E = math.log2(math.e)
BOUNDED_SOFTMAX_RANGE = 60.0
MOBA_NEAR_TILES = REL_MAX_DISTANCE // MOBA_BLOCK + 1
MOBA_WIDE = 2 * MOBA_BLOCK
MOBA_WIDE_NEAR_TILES = REL_MAX_DISTANCE // MOBA_WIDE + 1
FOX_FORGET_COLS = 4
EXP_ZERO_CUTOFF = -104.0

COL_FQ, COL_FK, COL_FV, COL_FG = 0, 2, 4, 6
COL_MQ, COL_MK, COL_MV, COL_MG = 8, 10, 12, 14
COL_SQ, COL_SK, COL_SV, COL_SG = 16, 18, 20, 22
COL_WQ, COL_WK, COL_WV, COL_WG = 24, 26, 27, 28
COL_FF = 30
PROJ_WIDTH = 32 * LANES
QK_NORM_COLS = (COL_FQ, COL_FQ + 1, COL_FK, COL_FK + 1, COL_MQ, COL_MQ + 1, COL_MK, COL_MK + 1,
                COL_WQ, COL_WQ + 1, COL_WK)

_MXU_DTYPE = jnp.bfloat16
_F32 = jnp.float32
_VMEM_LIMIT = 48 * 1024 * 1024
_VMEM_LIMIT_WIDE = 56 * 1024 * 1024

_NT = (((1,), (1,)), ((), ()))


def _lane_iota(shape):
    return lax.broadcasted_iota(jnp.int32, shape, len(shape) - 1)


def _pair_rms(x, gain_row):
    low = _lane_iota(x.shape) < HEAD_DIM
    sq = x * x
    ms_lo = jnp.sum(jnp.where(low, sq, 0.0), axis=-1, keepdims=True)
    ms_hi = jnp.sum(jnp.where(low, 0.0, sq), axis=-1, keepdims=True)
    ms = jnp.where(low, ms_lo, ms_hi) * (1.0 / HEAD_DIM)
    return x * lax.rsqrt(ms + RMS_EPS) * gain_row


def _silu(g):
    return g * (1.0 / (1.0 + jnp.exp(-g)))


def _split3(x):
    hi = x.astype(_MXU_DTYPE).astype(_F32)
    r = x - hi
    mid = r.astype(_MXU_DTYPE).astype(_F32)
    return hi, mid, r - mid


def _own_half(hh, shape):
    lane = _lane_iota(shape)
    return (lane < HEAD_DIM) if hh == 0 else (lane >= HEAD_DIM)


def _flash_update(s, v_tile, m, l, acc):
    m_new = jnp.maximum(m, jnp.max(s, axis=-1, keepdims=True))
    alpha = jnp.exp(m - m_new)
    p = jnp.exp(s - m_new)
    l = alpha * l + jnp.sum(p, axis=-1, keepdims=True)
    acc = alpha * acc + jnp.dot(p.astype(_MXU_DTYPE), v_tile, preferred_element_type=_F32)
    return m_new, l, acc


def _causal(tq):
    return (lax.broadcasted_iota(jnp.int32, (tq, tq), 0)
            >= lax.broadcasted_iota(jnp.int32, (tq, tq), 1))


def _inproj_kernel(x_ref, gain_ref, w_ref, qkg_ref, o_ref, *, tn):
    x = x_ref[...]
    ms = jnp.mean(x * x, axis=-1, keepdims=True)
    hn = (x * lax.rsqrt(ms + RMS_EPS) * gain_ref[...]).astype(_MXU_DTYPE)
    for n in range(o_ref.shape[1] // tn):
        acc = jnp.dot(hn, w_ref[:, n * tn:(n + 1) * tn], preferred_element_type=_F32)
        for c in range(n * tn // LANES, (n + 1) * tn // LANES):
            tile = acc[:, c * LANES - n * tn:(c + 1) * LANES - n * tn]
            if c in QK_NORM_COLS:
                tile = _pair_rms(tile, qkg_ref[:, c * LANES:(c + 1) * LANES])
            o_ref[:, c * LANES:(c + 1) * LANES] = tile


def _inproj(x2d, gain_row, w_all, layer, qk_gain_row):
    m, d = x2d.shape
    n = w_all.shape[2]
    tm = 256
    return pl.pallas_call(
        functools.partial(_inproj_kernel, tn=1024),
        grid=(m // tm,),
        in_specs=[pl.BlockSpec((tm, d), lambda i: (i, 0)),
                  pl.BlockSpec((1, d), lambda i: (0, 0)),
                  pl.BlockSpec((None, d, n), lambda i: (layer, 0, 0)),
                  pl.BlockSpec((1, n), lambda i: (0, 0))],
        out_specs=pl.BlockSpec((tm, n), lambda i: (i, 0)),
        out_shape=jax.ShapeDtypeStruct((m, n), _F32),
        compiler_params=pltpu.CompilerParams(
            dimension_semantics=("arbitrary",), vmem_limit_bytes=_VMEM_LIMIT),
        name="inproj",
    )(x2d, gain_row, w_all, qk_gain_row)


def _rel_bucket(dist):
    max_exact = NUM_BUCKETS // 2
    d = jnp.maximum(dist, 0)
    log_ratio = (jnp.log(jnp.maximum(d, 1).astype(_F32) / max_exact)
                 / math.log(REL_MAX_DISTANCE / max_exact))
    large = max_exact + (log_ratio * (NUM_BUCKETS - max_exact)).astype(jnp.int32)
    large = jnp.minimum(large, NUM_BUCKETS - 1)
    return jnp.where(d < max_exact, d, large)


def _bias_kernel(rb_ref, bm_ref, bw_ref, bs_ref, om_ref, ow_ref, os_ref, *, n_heads):
    h = pl.program_id(0)

    def lookup(buckets, col):
        acc = jnp.zeros(buckets.shape, _F32)
        for k in range(NUM_BUCKETS):
            acc = jnp.where(buckets == k, rb_ref[k, col], acc)
        return acc

    for d in range(bm_ref.shape[0]):
        om_ref[0, d] = lookup(bm_ref[d], h)
    for d in range(bw_ref.shape[0]):
        ow_ref[0, d] = lookup(bw_ref[d], h) * LOG2E
    os_ref[0] = lookup(bs_ref[...], n_heads + h)


def _bias_tiles(rel_bias):
    n_heads = rel_bias.shape[1] // 2

    def toeplitz(size, count):
        i = jnp.arange(size)[:, None]
        j = jnp.arange(size)[None, :]
        return jnp.stack([_rel_bucket(d * size + i - j) for d in range(count)])

    bm = toeplitz(MOBA_BLOCK, MOBA_NEAR_TILES)
    bw = toeplitz(MOBA_WIDE, MOBA_WIDE_NEAR_TILES)
    w = SWA_WINDOW
    bs = _rel_bucket(jnp.arange(w)[:, None] + w - jnp.arange(2 * w)[None, :])
    return pl.pallas_call(
        functools.partial(_bias_kernel, n_heads=n_heads),
        grid=(n_heads,),
        in_specs=[pl.BlockSpec(memory_space=pltpu.SMEM),
                  pl.BlockSpec(bm.shape, lambda h: (0, 0, 0)),
                  pl.BlockSpec(bw.shape, lambda h: (0, 0, 0)),
                  pl.BlockSpec(bs.shape, lambda h: (0, 0))],
        out_specs=[pl.BlockSpec((1,) + bm.shape, lambda h: (h, 0, 0, 0)),
                   pl.BlockSpec((1,) + bw.shape, lambda h: (h, 0, 0, 0)),
                   pl.BlockSpec((1,) + bs.shape, lambda h: (h, 0, 0))],
        out_shape=[jax.ShapeDtypeStruct((n_heads,) + bm.shape, _F32),
                   jax.ShapeDtypeStruct((n_heads,) + bw.shape, _F32),
                   jax.ShapeDtypeStruct((n_heads,) + bs.shape, _F32)],
        compiler_params=pltpu.CompilerParams(
            dimension_semantics=("arbitrary",), vmem_limit_bytes=_VMEM_LIMIT),
        name="bias_tiles",
    )(rel_bias, bm, bw, bs)


def _forget_cumsum(ff_ref, bf_ref, c_sc):
    t_len = c_sc.shape[0]
    ff = ff_ref[0] + bf_ref[...]
    log_f = jnp.minimum(ff, 0.0) - jnp.log(1.0 + jnp.exp(-jnp.abs(ff)))
    parts = [part.astype(_MXU_DTYPE) for part in _split3(log_f)]
    ch = 256
    tri = (lax.broadcasted_iota(jnp.int32, (ch, ch), 0)
           >= lax.broadcasted_iota(jnp.int32, (ch, ch), 1)).astype(_MXU_DTYPE)
    carry = jnp.zeros((1, LANES), _F32)
    for r in range(t_len // ch):
        inc = carry
        for part in reversed(parts):
            inc = inc + jnp.dot(tri, part[r * ch:(r + 1) * ch], preferred_element_type=_F32)
        c_sc[r * ch:(r + 1) * ch, :] = inc
        carry = inc[ch - 1:ch, :]


def _lane_column(x, col):
    return jnp.sum(jnp.where(_lane_iota(x.shape) == col, x, 0.0), axis=-1, keepdims=True)


def _lane_fields(shape, a0, fields):
    lane = _lane_iota(shape)
    out = jnp.zeros(shape, _F32)
    for n, f in enumerate(fields):
        out = jnp.where(lane == a0 + n, f, out)
    return out


def _bounded_weights(ps_ref, q_augs, kaug_sc, r, bias, diagonal):
    tq = q_augs[0].shape[0]
    for h in range(len(q_augs)):
        s = lax.dot_general(q_augs[h], kaug_sc[h, pl.ds(r, tq), :], _NT,
                            preferred_element_type=_F32)
        if bias is not None:
            s = s + bias(h)
        if diagonal:
            s = jnp.where(_causal(tq), s, NEG_INF)
        ps_ref[h] = jnp.exp2(s).astype(_MXU_DTYPE)


def _bounded_values(ps_ref, vext_sc, r, acc_sc):
    tq = ps_ref.shape[1]
    for p in range(acc_sc.shape[0]):
        acc_sc[p] += (
            jnp.dot(ps_ref[2 * p], vext_sc[2 * p, pl.ds(r, tq), :], preferred_element_type=_F32)
            + jnp.dot(ps_ref[2 * p + 1], vext_sc[2 * p + 1, pl.ds(r, tq), :],
                      preferred_element_type=_F32))


def _bounded_store(acc_sc, g_ref, o_ref):
    tq = acc_sc.shape[1]
    for p in range(acc_sc.shape[0]):
        acc = acc_sc[p]
        den = acc[:, LANES:]
        l = jnp.where(_own_half(0, (tq, LANES)), _lane_column(den, 0), _lane_column(den, 1))
        cols = slice(p * LANES, (p + 1) * LANES)
        o_ref[0, :, cols] = (acc[:, :LANES] / l * _silu(g_ref[0, :, cols])).astype(o_ref.dtype)


def _bounded_pipeline_step(n, ps_bufs, weights, values):
    def new_in(slot):
        def run():
            weights(ps_bufs[slot])
            values(ps_bufs[1 - slot])
        return run
    lax.cond(n % 2 == 0, new_in(0), new_in(1))


def _bounded_finish(n_done, ps_bufs, values):
    lax.cond(n_done % 2 == 0, lambda: values(ps_bufs[0]), lambda: values(ps_bufs[1]))


def _fox_bounded_kernel(bound_ref, q_ref, k_ref, v_ref, ff_ref, g_ref, bf_ref, o_ref,
                        kaug_sc, vext_sc, c_sc, ps_a, ps_b, acc_sc):
    i = pl.program_id(1)
    tq = q_ref.shape[1]
    n_heads = kaug_sc.shape[0]

    @pl.when(i == 0)
    def _prep():
        _forget_cumsum(ff_ref, bf_ref, c_sc)
        c_all = c_sc[...] * LOG2E
        lane = _lane_iota((c_all.shape[0], LANES))
        for h in range(n_heads):
            p, hh = divmod(h, 2)
            kn = k_ref[0, :, p * LANES:(p + 1) * LANES]
            v = v_ref[0, :, p * LANES:(p + 1) * LANES]
            a0 = HEAD_DIM if hh == 0 else 0
            hi, mid, lo = _split3(-_lane_column(c_all, h))
            aug = _lane_fields(kn.shape, a0, [hi, mid, lo, 1.0, 1.0, 1.0, 1.0, 1.0])
            own = _own_half(hh, kn.shape)
            kaug_sc[h] = jnp.where(own, kn, aug).astype(_MXU_DTYPE)
            vext_sc[h, :, 0:LANES] = jnp.where(own, v, 0.0).astype(_MXU_DTYPE)
            vext_sc[h, :, LANES:2 * LANES] = jnp.where(lane == hh, 1.0, 0.0).astype(_MXU_DTYPE)

    row0 = pl.multiple_of(i * tq, tq)
    c_t = c_sc[pl.ds(row0, tq), :] * LOG2E
    off = jnp.full((1, 1), -LOG2E, _F32) * bound_ref[0]
    off_hi = off.astype(_MXU_DTYPE).astype(_F32)
    q_augs = []
    for h in range(n_heads):
        p, hh = divmod(h, 2)
        q = q_ref[0, :, p * LANES:(p + 1) * LANES] * (ATTN_SCALE * LOG2E)
        a0 = HEAD_DIM if hh == 0 else 0
        hi, mid, lo = _split3(_lane_column(c_t, h))
        aug = _lane_fields(q.shape, a0, [1.0, 1.0, 1.0, hi, mid, lo, off_hi, off - off_hi])
        q_augs.append(jnp.where(_own_half(hh, q.shape), q, aug).astype(_MXU_DTYPE))

    head_lanes = _lane_iota((1, LANES)) < n_heads
    c_first = c_sc[pl.ds(row0, 1), :]
    cutoff = EXP_ZERO_CUTOFF - 2.0 * bound_ref[0]

    def live(j):
        c_last = c_sc[pl.ds(jnp.maximum(j, 0) * tq + (tq - 1), 1), :]
        gap = jnp.max(jnp.where(head_lanes, c_first - c_last, -jnp.inf))
        return (j >= 0) & (gap >= cutoff)

    ps_bufs = (ps_a, ps_b)

    def values_at(r):
        return lambda ps_ref: _bounded_values(ps_ref, vext_sc, pl.multiple_of(r, tq), acc_sc)

    def body(state):
        j, _, r_prev, n = state
        r = pl.multiple_of(j * tq, tq)
        _bounded_pipeline_step(
            n, ps_bufs,
            lambda ps_ref: _bounded_weights(ps_ref, q_augs, kaug_sc, r, None, False),
            values_at(r_prev))
        return j - 1, live(j - 1), r, n + 1

    acc_sc[...] = jnp.zeros(acc_sc.shape, _F32)
    _bounded_weights(ps_a, q_augs, kaug_sc, row0, None, True)
    state = lax.while_loop(lambda state: state[1], body,
                           (i - 1, live(i - 1), row0, jnp.int32(1)))
    _bounded_finish(state[3] - 1, ps_bufs, values_at(state[2]))
    _bounded_store(acc_sc, g_ref, o_ref)


def _fox_bounded(proj, bf_row, bound, *, tq=512):
    b, t, _ = proj.shape
    n_heads = GROUP_WIDTH // HEAD_DIM
    wide = GROUP_WIDTH // LANES
    return pl.pallas_call(
        _fox_bounded_kernel,
        grid=(b, t // tq),
        in_specs=[pl.BlockSpec(memory_space=pltpu.SMEM),
                  pl.BlockSpec((1, tq, GROUP_WIDTH), lambda bb, i: (bb, i, COL_FQ // wide)),
                  pl.BlockSpec((1, t, GROUP_WIDTH), lambda bb, i: (bb, 0, COL_FK // wide)),
                  pl.BlockSpec((1, t, GROUP_WIDTH), lambda bb, i: (bb, 0, COL_FV // wide)),
                  pl.BlockSpec((1, t, LANES), lambda bb, i: (bb, 0, COL_FF)),
                  pl.BlockSpec((1, tq, GROUP_WIDTH), lambda bb, i: (bb, i, COL_FG // wide)),
                  pl.BlockSpec((1, LANES), lambda bb, i: (0, 0))],
        out_specs=pl.BlockSpec((1, tq, GROUP_WIDTH), lambda bb, i: (bb, i, 0)),
        out_shape=jax.ShapeDtypeStruct((b, t, GROUP_WIDTH), _MXU_DTYPE),
        scratch_shapes=[pltpu.VMEM((n_heads, t, LANES), _MXU_DTYPE),
                        pltpu.VMEM((n_heads, t, 2 * LANES), _MXU_DTYPE),
                        pltpu.VMEM((t, LANES), _F32),
                        pltpu.VMEM((n_heads, tq, tq), _MXU_DTYPE),
                        pltpu.VMEM((n_heads, tq, tq), _MXU_DTYPE),
                        pltpu.VMEM((n_heads // 2, tq, 2 * LANES), _F32)],
        compiler_params=pltpu.CompilerParams(
            dimension_semantics=("arbitrary", "arbitrary"),
            vmem_limit_bytes=_VMEM_LIMIT_WIDE),
        name="fox_bounded",
    )(bound, proj, proj, proj, proj, proj, bf_row)


def _fox_kernel(q_ref, k_ref, v_ref, ff_ref, g_ref, bf_ref, o_ref,
                kaug_sc, v_sc, c_sc, *, tq):
    p = pl.program_id(1)
    i = pl.program_id(2)

    def head_column(x, hh):
        return _lane_column(x, 2 * p + hh)

    @pl.when(i == 0)
    def _prep():
        pl.when(p == 0)(lambda: _forget_cumsum(ff_ref, bf_ref, c_sc))
        kn = k_ref[0]
        c_all = c_sc[...]
        lane = _lane_iota(kn.shape)
        for hh in range(2):
            a0 = HEAD_DIM if hh == 0 else 0
            hi, mid, lo = _split3(-head_column(c_all, hh))
            aug = jnp.where(lane == a0, hi,
                            jnp.where(lane == a0 + 1, mid,
                                      jnp.where(lane == a0 + 2, lo,
                                                jnp.where((lane >= a0 + 3) & (lane < a0 + 6), 1.0, 0.0))))
            kaug_sc[hh] = jnp.where(_own_half(hh, kn.shape), kn, aug).astype(_MXU_DTYPE)
        v_sc[...] = v_ref[0].astype(_MXU_DTYPE)

    q = q_ref[0] * ATTN_SCALE
    row0 = pl.multiple_of(i * tq, tq)
    c_t = c_sc[pl.ds(row0, tq), :]
    lane = _lane_iota(q.shape)
    outs = []
    for hh in range(2):
        a0 = HEAD_DIM if hh == 0 else 0
        hi, mid, lo = _split3(head_column(c_t, hh))
        aug = jnp.where((lane >= a0) & (lane < a0 + 3), 1.0,
                        jnp.where(lane == a0 + 3, hi,
                                  jnp.where(lane == a0 + 4, mid,
                                            jnp.where(lane == a0 + 5, lo, 0.0))))
        q_aug = jnp.where(_own_half(hh, q.shape), q, aug).astype(_MXU_DTYPE)

        s = lax.dot_general(q_aug, kaug_sc[hh, pl.ds(row0, tq), :], _NT,
                            preferred_element_type=_F32)
        s = jnp.where(_causal(tq), s, NEG_INF)
        m = jnp.max(s, axis=-1, keepdims=True)
        pr = jnp.exp(s - m)
        l = jnp.sum(pr, axis=-1, keepdims=True)
        acc = jnp.dot(pr.astype(_MXU_DTYPE), v_sc[pl.ds(row0, tq), :], preferred_element_type=_F32)

        def body(j, carry, q_aug=q_aug, hh=hh):
            r = pl.multiple_of(j * tq, tq)
            s = lax.dot_general(q_aug, kaug_sc[hh, pl.ds(r, tq), :], _NT,
                                preferred_element_type=_F32)
            return _flash_update(s, v_sc[pl.ds(r, tq), :], *carry)

        m, l, acc = lax.fori_loop(0, i, body, (m, l, acc))
        outs.append(acc / l)
    o = jnp.where(_own_half(0, outs[0].shape), outs[0], outs[1])
    o_ref[0] = (o * _silu(g_ref[0])).astype(o_ref.dtype)


def _fox(proj, bf_row, *, tq=256):
    b, t, _ = proj.shape
    return pl.pallas_call(
        functools.partial(_fox_kernel, tq=tq),
        grid=(b, 2, t // tq),
        in_specs=[pl.BlockSpec((1, tq, LANES), lambda bb, p, i: (bb, i, COL_FQ + p)),
                  pl.BlockSpec((1, t, LANES), lambda bb, p, i: (bb, 0, COL_FK + p)),
                  pl.BlockSpec((1, t, LANES), lambda bb, p, i: (bb, 0, COL_FV + p)),
                  pl.BlockSpec((1, t, LANES), lambda bb, p, i: (bb, 0, COL_FF)),
                  pl.BlockSpec((1, tq, LANES), lambda bb, p, i: (bb, i, COL_FG + p)),
                  pl.BlockSpec((1, LANES), lambda bb, p, i: (0, 0))],
        out_specs=pl.BlockSpec((1, tq, LANES), lambda bb, p, i: (bb, i, p)),
        out_shape=jax.ShapeDtypeStruct((b, t, GROUP_WIDTH), _MXU_DTYPE),
        scratch_shapes=[pltpu.VMEM((2, t, LANES), _MXU_DTYPE),
                        pltpu.VMEM((t, LANES), _MXU_DTYPE),
                        pltpu.VMEM((t, LANES), _F32)],
        compiler_params=pltpu.CompilerParams(
            dimension_semantics=("arbitrary", "arbitrary", "arbitrary"),
            vmem_limit_bytes=_VMEM_LIMIT),
        name="fox",
    )(proj, proj, proj, proj, proj, bf_row)


def _moba_block_means(kn, kmean_sc):
    kmean_sc[...] = jnp.zeros(kmean_sc.shape, _F32)
    for n in range(kn.shape[0] // MOBA_BLOCK):
        kmean_sc[n:n + 1, :] = jnp.mean(kn[n * MOBA_BLOCK:(n + 1) * MOBA_BLOCK], axis=0, keepdims=True)


def _moba_select(q_head, kmean, past):
    gate = lax.dot_general(q_head, kmean, _NT, precision=lax.Precision.HIGHEST,
                           preferred_element_type=_F32)
    lane_f = _lane_iota(gate.shape).astype(_F32)
    cand = jnp.where(past, gate, -jnp.inf)
    sel = jnp.zeros(gate.shape, _F32)
    for _ in range(MOBA_TOPK):
        mx = jnp.max(cand, axis=-1, keepdims=True)
        is_max = (cand == mx) & (mx > -jnp.inf)
        first = jnp.min(jnp.where(is_max, lane_f, float(LANES)), axis=-1, keepdims=True)
        pick = lane_f == first
        sel = jnp.where(pick, 1.0, sel)
        cand = jnp.where(pick, -jnp.inf, cand)
    return sel


def _moba_bounded_kernel(rb_ref, bound_ref, q_ref, k_ref, v_ref, g_ref, bias_ref, o_ref,
                         kaug_sc, vext_sc, kmean_sc, ps_a, ps_b, acc_sc):
    i = pl.program_id(1)
    tq = q_ref.shape[1]
    n_heads = kaug_sc.shape[0]
    blk = MOBA_BLOCK
    near = MOBA_WIDE_NEAR_TILES

    @pl.when(i == 0)
    def _prep():
        lane = _lane_iota((k_ref.shape[1], LANES))
        row_blk = lax.broadcasted_iota(jnp.int32, lane.shape, 0) // blk
        for h in range(n_heads):
            p, hh = divmod(h, 2)
            kn = k_ref[0, :, p * LANES:(p + 1) * LANES]
            v = v_ref[0, :, p * LANES:(p + 1) * LANES]
            if hh == 0:
                _moba_block_means(kn, kmean_sc.at[p])
            a0 = HEAD_DIM if hh == 0 else 0
            aug = jnp.where((lane - a0 == row_blk) | (lane - a0 - 16 == row_blk)
                            | (lane - a0 == 32) | (lane - a0 == 33), 1.0, 0.0)
            own = _own_half(hh, kn.shape)
            kaug_sc[h] = jnp.where(own, kn, aug).astype(_MXU_DTYPE)
            vext_sc[h, :, 0:LANES] = jnp.where(own, v, 0.0).astype(_MXU_DTYPE)
            vext_sc[h, :, LANES:2 * LANES] = jnp.where(lane == hh, 1.0, 0.0).astype(_MXU_DTYPE)

    row0 = pl.multiple_of(i * tq, tq)
    off = jnp.full((1, 1), -LOG2E, _F32) * bound_ref[0]
    off_hi = off.astype(_MXU_DTYPE).astype(_F32)
    blk_n = lax.broadcasted_iota(jnp.int32, (16, tq), 0)
    blk_f = blk_n.astype(_F32)
    q_blk = i * (tq // blk) + lax.broadcasted_iota(jnp.int32, (16, tq), 1) // blk
    past = blk_n < q_blk
    far = i - blk_n // 2 >= near
    sub8 = lax.broadcasted_iota(jnp.int32, (8, tq), 0)
    off_rows = jnp.where(sub8 == 0, off_hi, jnp.where(sub8 == 1, off - off_hi, 0.0))
    q_augs = []
    for h in range(n_heads):
        p, hh = divmod(h, 2)
        qn = q_ref[0, :, p * LANES:(p + 1) * LANES]
        own = _own_half(hh, qn.shape)
        gate = lax.dot_general(kmean_sc[p, 0:16, :], jnp.where(own, qn, 0.0), _NT,
                               precision=lax.Precision.HIGHEST, preferred_element_type=_F32)
        cand = jnp.where(past, gate, -jnp.inf)
        sel = jnp.zeros((16, tq), _F32)
        for _ in range(MOBA_TOPK):
            mx = jnp.max(cand, axis=0, keepdims=True)
            is_max = (cand == mx) & (mx > -jnp.inf)
            first = jnp.min(jnp.where(is_max, blk_f, float(LANES)), axis=0, keepdims=True)
            pick = blk_f == first
            sel = jnp.where(pick, 1.0, sel)
            cand = jnp.where(pick, -jnp.inf, cand)
        c_far = jnp.full((1, 1), LOG2E, _F32) * rb_ref[NUM_BUCKETS - 1, h]
        far_hi = c_far.astype(_MXU_DTYPE).astype(_F32)
        pen = jnp.where((blk_n == q_blk) | (past & (sel != 0.0)), 0.0, NEG_INF)
        fields = jnp.concatenate([pen + jnp.where(far, far_hi, 0.0),
                                  jnp.where(far, c_far - far_hi, 0.0),
                                  off_rows,
                                  jnp.zeros((HEAD_DIM - 40, tq), _F32)], axis=0)
        blank = jnp.zeros((HEAD_DIM, tq), _F32)
        aug = jnp.concatenate([blank, fields] if hh == 0 else [fields, blank], axis=0).T
        q_augs.append(jnp.where(own, qn * (ATTN_SCALE * LOG2E), aug).astype(_MXU_DTYPE))

    ps_bufs = (ps_a, ps_b)

    def values_at(r):
        return lambda ps_ref: _bounded_values(ps_ref, vext_sc, pl.multiple_of(r, tq), acc_sc)

    def body(j, state, with_bias):
        r_prev, n = state
        r = pl.multiple_of(j * tq, tq)
        bias = (lambda h: bias_ref[h, i - j]) if with_bias else None
        _bounded_pipeline_step(
            n, ps_bufs,
            lambda ps_ref: _bounded_weights(ps_ref, q_augs, kaug_sc, r, bias, False),
            values_at(r_prev))
        return r, n + 1

    acc_sc[...] = jnp.zeros(acc_sc.shape, _F32)
    _bounded_weights(ps_a, q_augs, kaug_sc, row0, lambda h: bias_ref[h, 0], True)
    state = (row0, jnp.int32(1))
    near_lo = jnp.maximum(i - (near - 1), 0)
    state = lax.fori_loop(near_lo, i, functools.partial(body, with_bias=True), state)
    state = lax.fori_loop(0, near_lo, functools.partial(body, with_bias=False), state)
    _bounded_finish(state[1] - 1, ps_bufs, values_at(state[0]))
    _bounded_store(acc_sc, g_ref, o_ref)


def _moba_bounded(proj, rel_bias, bias_tiles, bound):
    b, t, _ = proj.shape
    tq = MOBA_WIDE
    assert t % tq == 0 and t // MOBA_BLOCK <= 16
    n_heads = GROUP_WIDTH // HEAD_DIM
    wide = GROUP_WIDTH // LANES
    resident = pl.Buffered(1)
    return pl.pallas_call(
        _moba_bounded_kernel,
        grid=(b, t // tq),
        in_specs=[pl.BlockSpec(memory_space=pltpu.SMEM),
                  pl.BlockSpec(memory_space=pltpu.SMEM),
                  pl.BlockSpec((1, tq, GROUP_WIDTH), lambda bb, i: (bb, i, COL_MQ // wide)),
                  pl.BlockSpec((1, t, GROUP_WIDTH), lambda bb, i: (bb, 0, COL_MK // wide),
                               pipeline_mode=resident),
                  pl.BlockSpec((1, t, GROUP_WIDTH), lambda bb, i: (bb, 0, COL_MV // wide),
                               pipeline_mode=resident),
                  pl.BlockSpec((1, tq, GROUP_WIDTH), lambda bb, i: (bb, i, COL_MG // wide)),
                  pl.BlockSpec((n_heads, MOBA_WIDE_NEAR_TILES, tq, tq), lambda bb, i: (0, 0, 0, 0),
                               pipeline_mode=resident)],
        out_specs=pl.BlockSpec((1, tq, GROUP_WIDTH), lambda bb, i: (bb, i, 0)),
        out_shape=jax.ShapeDtypeStruct((b, t, GROUP_WIDTH), _MXU_DTYPE),
        scratch_shapes=[pltpu.VMEM((n_heads, t, LANES), _MXU_DTYPE),
                        pltpu.VMEM((n_heads, t, 2 * LANES), _MXU_DTYPE),
                        pltpu.VMEM((n_heads // 2, LANES, LANES), _F32),
                        pltpu.VMEM((n_heads, tq, tq), _MXU_DTYPE),
                        pltpu.VMEM((n_heads, tq, tq), _MXU_DTYPE),
                        pltpu.VMEM((n_heads // 2, tq, 2 * LANES), _F32)],
        compiler_params=pltpu.CompilerParams(
            dimension_semantics=("arbitrary", "arbitrary"),
            vmem_limit_bytes=_VMEM_LIMIT_WIDE),
        name="moba_bounded",
    )(rel_bias, bound, proj, proj, proj, proj, bias_tiles)


def _moba_kernel(rb_ref, q_ref, k_ref, v_ref, g_ref, bias_ref, o_ref,
                 kaug_sc, v_sc, kmean_sc):
    p = pl.program_id(1)
    i = pl.program_id(2)
    t_len = k_ref.shape[1]
    blk = MOBA_BLOCK
    nblk = t_len // blk
    near = MOBA_NEAR_TILES

    @pl.when(i == 0)
    def _prep():
        kn = k_ref[0]
        _moba_block_means(kn, kmean_sc)
        lane = _lane_iota(kn.shape)
        row_blk = lax.broadcasted_iota(jnp.int32, kn.shape, 0) // blk
        for hh in range(2):
            a0 = HEAD_DIM if hh == 0 else 0
            onehot = jnp.where((lane - a0 == row_blk) | (lane - a0 - 16 == row_blk), 1.0, 0.0)
            kaug_sc[hh] = jnp.where(_own_half(hh, kn.shape), kn, onehot).astype(_MXU_DTYPE)
        v_sc[...] = v_ref[0].astype(_MXU_DTYPE)

    qn = q_ref[0]
    row0 = pl.multiple_of(i * blk, blk)
    lane = _lane_iota(qn.shape)
    lane_f = lane.astype(_F32)
    past = lane < i
    outs = []
    for hh in range(2):
        own = _own_half(hh, qn.shape)
        sel = _moba_select(jnp.where(own, qn, 0.0), kmean_sc[...], past)
        c_far = jnp.full((1, LANES), rb_ref[NUM_BUCKETS - 1, 2 * p + hh], _F32)
        far_hi = c_far.astype(_MXU_DTYPE).astype(_F32)
        far_lo = c_far - far_hi
        pen = jnp.where(past & (sel == 0.0), NEG_INF, 0.0)
        aug = jnp.where(lane < 16, pen + jnp.where(i - lane >= near, far_hi, 0.0),
                        jnp.where((lane < 32) & (i - (lane - 16) >= near), far_lo, 0.0))
        if hh == 0:
            aug = pltpu.roll(aug, HEAD_DIM, axis=1)
        q_aug = jnp.where(own, qn * ATTN_SCALE, aug).astype(_MXU_DTYPE)

        def scores(r, q_aug=q_aug, hh=hh):
            return lax.dot_general(q_aug, kaug_sc[hh, pl.ds(r, blk), :], _NT,
                                   preferred_element_type=_F32)

        s = scores(row0) + bias_ref[hh, 0]
        s = jnp.where(_causal(blk), s, NEG_INF)
        m = jnp.max(s, axis=-1, keepdims=True)
        pr = jnp.exp(s - m)
        l = jnp.sum(pr, axis=-1, keepdims=True)
        acc = jnp.dot(pr.astype(_MXU_DTYPE), v_sc[pl.ds(row0, blk), :], preferred_element_type=_F32)

        def near_body(j, carry, hh=hh, scores=scores):
            r = pl.multiple_of(j * blk, blk)
            return _flash_update(scores(r) + bias_ref[hh, i - j], v_sc[pl.ds(r, blk), :], *carry)

        def far_body(j, carry, scores=scores):
            r = pl.multiple_of(j * blk, blk)
            return _flash_update(scores(r), v_sc[pl.ds(r, blk), :], *carry)

        near_lo = jnp.maximum(i - (near - 1), 0)
        carry = lax.fori_loop(near_lo, i, near_body, (m, l, acc))
        m, l, acc = lax.fori_loop(0, near_lo, far_body, carry)
        outs.append(acc / l)
    o = jnp.where(_own_half(0, outs[0].shape), outs[0], outs[1])
    o_ref[0] = (o * _silu(g_ref[0])).astype(o_ref.dtype)


def _moba(proj, rel_bias, bias_tiles):
    b, t, _ = proj.shape
    blk = MOBA_BLOCK
    assert t % blk == 0 and t // blk <= 16
    return pl.pallas_call(
        _moba_kernel,
        grid=(b, 2, t // blk),
        in_specs=[pl.BlockSpec(memory_space=pltpu.SMEM),
                  pl.BlockSpec((1, blk, LANES), lambda bb, p, i: (bb, i, COL_MQ + p)),
                  pl.BlockSpec((1, t, LANES), lambda bb, p, i: (bb, 0, COL_MK + p)),
                  pl.BlockSpec((1, t, LANES), lambda bb, p, i: (bb, 0, COL_MV + p)),
                  pl.BlockSpec((1, blk, LANES), lambda bb, p, i: (bb, i, COL_MG + p)),
                  pl.BlockSpec((2, MOBA_NEAR_TILES, blk, blk), lambda bb, p, i: (p, 0, 0, 0))],
        out_specs=pl.BlockSpec((1, blk, LANES), lambda bb, p, i: (bb, i, p)),
        out_shape=jax.ShapeDtypeStruct((b, t, GROUP_WIDTH), _MXU_DTYPE),
        scratch_shapes=[pltpu.VMEM((2, t, LANES), _MXU_DTYPE),
                        pltpu.VMEM((t, LANES), _MXU_DTYPE),
                        pltpu.VMEM((LANES, LANES), _F32)],
        compiler_params=pltpu.CompilerParams(
            dimension_semantics=("arbitrary", "arbitrary", "arbitrary"),
            vmem_limit_bytes=_VMEM_LIMIT),
        name="moba",
    )(rel_bias, proj, proj, proj, proj, bias_tiles)


def _sb_kernel(q_ref, k_ref, v_ref, g_ref, o_ref, k_sc, v_sc, *, tq):
    i = pl.program_id(2)

    @pl.when(i == 0)
    def _prep():
        k_sc[...] = k_ref[0].astype(_MXU_DTYPE)
        v = v_ref[0]
        for hh in range(2):
            v_sc[hh] = jnp.where(_own_half(hh, v.shape), v, 0.0).astype(_MXU_DTYPE)

    subs = q_ref.shape[1] // tq
    q_heads = []
    for u in range(subs):
        q = q_ref[0, u * tq:(u + 1) * tq, :] * (ATTN_SCALE * LOG2E)
        q_heads.append([jnp.where(_own_half(hh, q.shape), q, 0.0).astype(_MXU_DTYPE)
                        for hh in range(2)])
    strict =(lax.broadcasted_iota(jnp.int32, (tq, tq), 0)
              > lax.broadcasted_iota(jnp.int32, (tq, tq), 1))
    after = strict.astype(_MXU_DTYPE)
    sign_bit = jnp.uint32(0x80000000)

    def pair(back, runs, diagonal):
        chains = [(u, t, hh) for u in range(subs) for t in range(2) for hh in range(2)]
        tile_idx = [[subs * i + u - back - t for t in range(2)] for u in range(subs)]
        valid = [[(j >= 0).astype(_F32) for j in js] for js in tile_idx]
        rows = [[pl.multiple_of(jnp.maximum(j, 0) * tq, tq) for j in js] for js in tile_idx]
        masked = [diagonal and t == 0 for _, t, _ in chains]
        zs = [lax.dot_general(q_heads[u][hh], k_sc[pl.ds(rows[u][t], tq), :], _NT,
                              preferred_element_type=_F32) for u, t, hh in chains]
        drops = []
        for z, msk in zip(zs, masked):
            neg_abs = lax.bitcast_convert_type(lax.bitcast_convert_type(z, jnp.uint32) | sign_bit, _F32)
            drop = jnp.maximum(z, 0.0) + jnp.log(1.0 + jnp.exp2(neg_abs)) * LOG2E
            drops.append(jnp.where(strict, drop, 0.0) if msk else drop)
        laters = []
        for drop in drops:
            hi = drop.astype(_MXU_DTYPE)
            lo = (drop - hi.astype(_F32)).astype(_MXU_DTYPE)
            laters.append(jnp.dot(hi, after, preferred_element_type=_F32)
                          + jnp.dot(lo, after, preferred_element_type=_F32))
        pvs = []
        for (u, t, hh), z, drop, later, msk in zip(chains, zs, drops, laters, masked):
            w = jnp.exp2(z - drop - later)
            if msk:
                w = jnp.where(strict, w, 0.0)
            pvs.append(jnp.dot(w.astype(_MXU_DTYPE), v_sc[hh, pl.ds(rows[u][t], tq), :],
                               preferred_element_type=_F32))
        sums = [jnp.sum(drop, axis=-1, keepdims=True) for drop in drops]
        new_runs, outs = [], []
        for u in range(subs):
            new_runs.append([])
            out = None
            for hh in range(2):
                run = runs[u][hh]
                for t in range(2):
                    c = chains.index((u, t, hh))
                    pv = pvs[c] * (jnp.exp2(run) * valid[u][t])
                    out = pv if out is None else out + pv
                    run = run - sums[c] * valid[u][t]
                new_runs[u].append(run)
            outs.append(out)
        return new_runs, outs

    def alive(runs):
        top = functools.reduce(jnp.maximum, [r for rs in runs for r in rs])
        return jnp.max(top) >= EXP_ZERO_CUTOFF * LOG2E

    def flat(runs):
        return [r for rs in runs for r in rs]

    def nested(flat_runs):
        return [flat_runs[2 * u:2 * u + 2] for u in range(subs)]

    zero = jnp.zeros((tq, 1), _F32)
    runs, accs = pair(0, [[zero, zero] for _ in range(subs)], True)

    def body(state):
        back = state[0]
        runs, outs = pair(back, nested(list(state[2:2 + 2 * subs])), False)
        accs = [a + o for a, o in zip(state[2 + 2 * subs:], outs)]
        return (back + 2, alive(runs), *flat(runs), *accs)

    state = lax.while_loop(lambda state: (subs * i + subs - 1 - state[0] >= 0) & state[1], body,
                           (jnp.int32(2), alive(runs), *flat(runs), *accs))
    for u in range(subs):
        o_ref[0, u * tq:(u + 1) * tq, :] = (
            state[2 + 2 * subs + u] * _silu(g_ref[0, u * tq:(u + 1) * tq, :])).astype(o_ref.dtype)


def _sb(proj, *, tq=256, subs=2):
    b, t, _ = proj.shape
    rows = tq * subs
    return pl.pallas_call(
        functools.partial(_sb_kernel, tq=tq),
        grid=(b, 2, t // rows),
        in_specs=[pl.BlockSpec((1, rows, LANES), lambda bb, p, i: (bb, i, COL_SQ + p)),
                  pl.BlockSpec((1, t, LANES), lambda bb, p, i: (bb, 0, COL_SK + p)),
                  pl.BlockSpec((1, t, LANES), lambda bb, p, i: (bb, 0, COL_SV + p)),
                  pl.BlockSpec((1, rows, LANES), lambda bb, p, i: (bb, i, COL_SG + p))],
        out_specs=pl.BlockSpec((1, rows, LANES), lambda bb, p, i: (bb, i, p)),
        out_shape=jax.ShapeDtypeStruct((b, t, GROUP_WIDTH), _MXU_DTYPE),
        scratch_shapes=[pltpu.VMEM((t, LANES), _MXU_DTYPE),
                        pltpu.VMEM((2, t, LANES), _MXU_DTYPE)],
        compiler_params=pltpu.CompilerParams(
            dimension_semantics=("arbitrary", "arbitrary", "arbitrary"),
            vmem_limit_bytes=_VMEM_LIMIT),
        name="stickbreak",
    )(proj, proj, proj, proj)


def _swa_kernel(sink_ref, q_ref, k_ref, v_ref, g_ref, bias_ref, o_ref,
                k_sc, v_sc, *, tq):
    kv = pl.program_id(1)
    i = pl.program_id(2)
    t_len = k_ref.shape[1]
    w = SWA_WINDOW

    @pl.when(i == 0)
    def _prep():
        kn = k_ref[0]
        v = v_ref[0]
        keep = (_lane_iota(kn.shape) < HEAD_DIM) == (kv == 0)
        k_sc[0:w, :] = jnp.zeros((w, LANES), _MXU_DTYPE)
        v_sc[0:w, :] = jnp.zeros((w, LANES), _MXU_DTYPE)
        k_sc[w:w + t_len, :] = jnp.where(keep, kn, pltpu.roll(kn, HEAD_DIM, axis=1)).astype(_MXU_DTYPE)
        v_sc[w:w + t_len, :] = jnp.where(keep, v, pltpu.roll(v, HEAD_DIM, axis=1)).astype(_MXU_DTYPE)

    q = q_ref[0] * ATTN_SCALE
    qi = lax.broadcasted_iota(jnp.int32, (w, 2 * w), 0)
    kj = lax.broadcasted_iota(jnp.int32, (w, 2 * w), 1)
    dist = qi + w - kj
    in_window = (dist >= 0) & (dist < w)
    for u in range(tq // w):
        nb = i * (tq // w) + u
        r = pl.multiple_of(nb * w, w)
        k_t = k_sc[pl.ds(r, 2 * w), :]
        v_t = v_sc[pl.ds(r, 2 * w), :]
        allowed = in_window & (kj + (nb - 1) * w >= 0)
        q_u = q[u * w:(u + 1) * w]
        outs = []
        for g in range(2):
            q_h = jnp.where(_own_half(g, q_u.shape), q_u, 0.0).astype(_MXU_DTYPE)
            s = lax.dot_general(q_h, k_t, _NT, preferred_element_type=_F32)
            s = jnp.where(allowed, s + bias_ref[g], NEG_INF)
            sink = sink_ref[2 * kv + g]
            m = jnp.maximum(jnp.max(s, axis=-1, keepdims=True), sink)
            e = jnp.exp(s - m)
            den = jnp.sum(e, axis=-1, keepdims=True) + jnp.exp(sink - m)
            outs.append(jnp.dot(e.astype(_MXU_DTYPE), v_t, preferred_element_type=_F32) / den)
        o = jnp.where(_own_half(0, outs[0].shape), outs[0], outs[1])
        o_ref[0, u * w:(u + 1) * w, :] = (o * _silu(g_ref[0, u * w:(u + 1) * w, :])).astype(o_ref.dtype)


def _swa(proj, sinks, bias_tiles, *, tq=512):
    b, t, _ = proj.shape
    w = SWA_WINDOW
    return pl.pallas_call(
        functools.partial(_swa_kernel, tq=tq),
        grid=(b, 2, t // tq),
        in_specs=[pl.BlockSpec(memory_space=pltpu.SMEM),
                  pl.BlockSpec((1, tq, LANES), lambda bb, kv, i: (bb, i, COL_WQ + kv)),
                  pl.BlockSpec((1, t, LANES), lambda bb, kv, i: (bb, 0, COL_WK)),
                  pl.BlockSpec((1, t, LANES), lambda bb, kv, i: (bb, 0, COL_WV)),
                  pl.BlockSpec((1, tq, LANES), lambda bb, kv, i: (bb, i, COL_WG + kv)),
                  pl.BlockSpec((2, w, 2 * w), lambda bb, kv, i: (kv, 0, 0))],
        out_specs=pl.BlockSpec((1, tq, LANES), lambda bb, kv, i: (bb, i, kv)),
        out_shape=jax.ShapeDtypeStruct((b, t, GROUP_WIDTH), _MXU_DTYPE),
        scratch_shapes=[pltpu.VMEM((t + w, LANES), _MXU_DTYPE),
                        pltpu.VMEM((t + w, LANES), _MXU_DTYPE)],
        compiler_params=pltpu.CompilerParams(
            dimension_semantics=("arbitrary", "arbitrary", "arbitrary"),
            vmem_limit_bytes=_VMEM_LIMIT),
        name="swa",
    )(sinks, proj, proj, proj, proj, bias_tiles)


def _outproj_kernel(x_ref, ya_ref, yb_ref, yc_ref, yd_ref, w_ref, o_ref):
    acc = x_ref[...]
    for g, y_ref in enumerate((ya_ref, yb_ref, yc_ref, yd_ref)):
        acc = acc + jnp.dot(y_ref[...], w_ref[g * GROUP_WIDTH:(g + 1) * GROUP_WIDTH, :],
                            preferred_element_type=_F32)
    o_ref[...] = acc


def _outproj(x2d, ys, w_all, layer):
    m, d = x2d.shape
    tm = 512
    y_spec = pl.BlockSpec((tm, GROUP_WIDTH), lambda i: (i, 0))
    return pl.pallas_call(
        _outproj_kernel,
        grid=(m // tm,),
        in_specs=[pl.BlockSpec((tm, d), lambda i: (i, 0)), y_spec, y_spec, y_spec, y_spec,
                  pl.BlockSpec((None,) + w_all.shape[1:], lambda i: (layer, 0, 0))],
        out_specs=pl.BlockSpec((tm, d), lambda i: (i, 0)),
        out_shape=jax.ShapeDtypeStruct((m, d), _F32),
        compiler_params=pltpu.CompilerParams(
            dimension_semantics=("arbitrary",), vmem_limit_bytes=_VMEM_LIMIT),
        name="outproj",
    )(x2d, *ys, w_all)


def _rearrange_w_in(w_in):
    off = 3 * GROUP_WIDTH
    main = jnp.concatenate([w_in[..., :off], w_in[..., off + FOX_FORGET_COLS:]], axis=-1)
    ff = w_in[..., off:off + FOX_FORGET_COLS]
    pad = PROJ_WIDTH - main.shape[-1] - FOX_FORGET_COLS
    return jnp.concatenate([main, ff, jnp.zeros(w_in.shape[:-1] + (pad,), w_in.dtype)], axis=-1)


def _qk_logit_bound(qk_gain):
    g = jnp.abs(qk_gain.astype(_F32))
    return ATTN_SCALE * HEAD_DIM * jnp.max(g[0]) * jnp.max(g[1])


def _qk_gain_row(fox_g, moba_g, swa_g):
    def tiles(g, n):
        return jnp.tile(g.astype(_F32), 2 * n)
    row = jnp.zeros((PROJ_WIDTH,), _F32)
    for col, g, n in ((COL_FQ, fox_g[0], 2), (COL_FK, fox_g[1], 2),
                      (COL_MQ, moba_g[0], 2), (COL_MK, moba_g[1], 2),
                      (COL_WQ, swa_g[0], 2), (COL_WK, swa_g[1], 1)):
        row = lax.dynamic_update_slice(row, tiles(g, n), (col * LANES,))
    return row[None, :]


def kernel(x, norm_gain, w_in, b_forget, fox_qk_gain, moba_qk_gain, swa_qk_gain, sinks, w_out, rel_bias):
    b, t, d = x.shape
    depth = w_in.shape[0]
    w_in_r = _rearrange_w_in(w_in).astype(_MXU_DTYPE)
    w_out_c = w_out.astype(_MXU_DTYPE)
    moba_bias, moba_wide_bias, swa_bias = _bias_tiles(rel_bias)
    moba_bias_max = jnp.max(jnp.abs(rel_bias[:, :rel_bias.shape[1] // 2].astype(_F32)))
    x2d = x.reshape(b * t, d)
    for layer in range(depth):
        qk_gain_row = _qk_gain_row(fox_qk_gain[layer], moba_qk_gain[layer], swa_qk_gain[layer])
        proj = _inproj(x2d, norm_gain[layer][None, :], w_in_r, layer, qk_gain_row)
        proj = proj.reshape(b, t, PROJ_WIDTH)
        bf_row = jnp.pad(b_forget[layer], (0, LANES - FOX_FORGET_COLS))[None, :]
        fox_bound = _qk_logit_bound(fox_qk_gain[layer])
        y_fox = lax.cond(2.0 * fox_bound <= BOUNDED_SOFTMAX_RANGE,
                         lambda: _fox_bounded(proj, bf_row, fox_bound.reshape(1)),
                         lambda: _fox(proj, bf_row))
        moba_bound = _qk_logit_bound(moba_qk_gain[layer]) + moba_bias_max
        y_moba = lax.cond(2.0 * moba_bound <= BOUNDED_SOFTMAX_RANGE,
                          lambda: _moba_bounded(proj, rel_bias, moba_wide_bias,
                                                moba_bound.reshape(1)),
                          lambda: _moba(proj, rel_bias, moba_bias))
        y_sb = _sb(proj)
        y_swa = _swa(proj, sinks[layer], swa_bias)
        ys = [y.reshape(b * t, GROUP_WIDTH) for y in (y_fox, y_moba, y_sb, y_swa)]
        x2d = _outproj(x2d, ys, w_out_c, layer)
    return x2d.reshape(b, t, d)
```

```python
import functools
import math

import jax
import jax.numpy as jnp
from jax import lax
from jax.experimental import pallas as pl
from jax.experimental.pallas import tpu as pltpu

HEAD_DIM = 64
LANES = 128
GROUP_WIDTH = 256
MOBA_BLOCK = 256
MOBA_TOPK = 3
SWA_WINDOW = 128
NUM_BUCKETS = 32
REL_MAX_DISTANCE = 1024
RMS_EPS = 1e-6
NEG_INF = -1e30
ATTN_SCALE = HEAD_DIM ** -0.5
LOG2E = math.log2(math.e)
BOUNDED_SOFTMAX_RANGE = 60.0
MOBA_NEAR_TILES = REL_MAX_DISTANCE // MOBA_BLOCK + 1
MOBA_WIDE = 2 * MOBA_BLOCK
MOBA_WIDE_NEAR_TILES = REL_MAX_DISTANCE // MOBA_WIDE + 1
FOX_FORGET_COLS = 4
EXP_ZERO_CUTOFF = -104.0

COL_FQ, COL_FK, COL_FV, COL_FG = 0, 2, 4, 6
COL_MQ, COL_MK, COL_MV, COL_MG = 8, 10, 12, 14
COL_SQ, COL_SK, COL_SV, COL_SG = 16, 18, 20, 22
COL_WQ, COL_WK, COL_WV, COL_WG = 24, 26, 27, 28
COL_FF = 30
PROJ_WIDTH = 32 * LANES
QK_NORM_COLS = (COL_FQ, COL_FQ + 1, COL_FK, COL_FK + 1, COL_MQ, COL_MQ + 1, COL_MK, COL_MK + 1,
                COL_WQ, COL_WQ + 1, COL_WK)

_MXU_DTYPE = jnp.bfloat16
_F32 = jnp.float32
_VMEM_LIMIT = 48 * 1024 * 1024
_VMEM_LIMIT_WIDE = 56 * 1024 * 1024

_NT = (((1,), (1,)), ((), ()))


def _lane_iota(shape):
    return lax.broadcasted_iota(jnp.int32, shape, len(shape) - 1)


def _pair_rms(x, gain_row):
    low = _lane_iota(x.shape) < HEAD_DIM
    sq = x * x
    ms_lo = jnp.sum(jnp.where(low, sq, 0.0), axis=-1, keepdims=True)
    ms_hi = jnp.sum(jnp.where(low, 0.0, sq), axis=-1, keepdims=True)
    ms = jnp.where(low, ms_lo, ms_hi) * (1.0 / HEAD_DIM)
    return x * lax.rsqrt(ms + RMS_EPS) * gain_row


def _silu(g):
    return g * (1.0 / (1.0 + jnp.exp(-g)))


def _split3(x):
    hi = x.astype(_MXU_DTYPE).astype(_F32)
    r = x - hi
    mid = r.astype(_MXU_DTYPE).astype(_F32)
    return hi, mid, r - mid


def _own_half(hh, shape):
    lane = _lane_iota(shape)
    return (lane < HEAD_DIM) if hh == 0 else (lane >= HEAD_DIM)


def _flash_update(s, v_tile, m, l, acc):
    m_new = jnp.maximum(m, jnp.max(s, axis=-1, keepdims=True))
    alpha = jnp.exp(m - m_new)
    p = jnp.exp(s - m_new)
    l = alpha * l + jnp.sum(p, axis=-1, keepdims=True)
    acc = alpha * acc + jnp.dot(p.astype(_MXU_DTYPE), v_tile, preferred_element_type=_F32)
    return m_new, l, acc


def _causal(tq):
    return (lax.broadcasted_iota(jnp.int32, (tq, tq), 0)
            >= lax.broadcasted_iota(jnp.int32, (tq, tq), 1))


def _inproj_kernel(x_ref, gain_ref, w_ref, qkg_ref, o_ref, *, tn):
    x = x_ref[...]
    ms = jnp.mean(x * x, axis=-1, keepdims=True)
    hn = (x * lax.rsqrt(ms + RMS_EPS) * gain_ref[...]).astype(_MXU_DTYPE)
    for n in range(o_ref.shape[1] // tn):
        acc = jnp.dot(hn, w_ref[:, n * tn:(n + 1) * tn], preferred_element_type=_F32)
        for c in range(n * tn // LANES, (n + 1) * tn // LANES):
            tile = acc[:, c * LANES - n * tn:(c + 1) * LANES - n * tn]
            if c in QK_NORM_COLS:
                tile = _pair_rms(tile, qkg_ref[:, c * LANES:(c + 1) * LANES])
            o_ref[:, c * LANES:(c + 1) * LANES] = tile


def _inproj(x2d, gain_row, w_all, layer, qk_gain_row):
    m, d = x2d.shape
    n = w_all.shape[2]
    tm = 512
    return pl.pallas_call(
        functools.partial(_inproj_kernel, tn=1024),
        grid=(m // tm,),
        in_specs=[pl.BlockSpec((tm, d), lambda i: (i, 0)),
                  pl.BlockSpec((1, d), lambda i: (0, 0)),
                  pl.BlockSpec((None, d, n), lambda i: (layer, 0, 0)),
                  pl.BlockSpec((1, n), lambda i: (0, 0))],
        out_specs=pl.BlockSpec((tm, n), lambda i: (i, 0)),
        out_shape=jax.ShapeDtypeStruct((m, n), _F32),
        compiler_params=pltpu.CompilerParams(
            dimension_semantics=("arbitrary",), vmem_limit_bytes=_VMEM_LIMIT),
        name="inproj",
    )(x2d, gain_row, w_all, qk_gain_row)


def _rel_bucket(dist):
    max_exact = NUM_BUCKETS // 2
    d = jnp.maximum(dist, 0)
    log_ratio = (jnp.log(jnp.maximum(d, 1).astype(_F32) / max_exact)
                 / math.log(REL_MAX_DISTANCE / max_exact))
    large = max_exact + (log_ratio * (NUM_BUCKETS - max_exact)).astype(jnp.int32)
    large = jnp.minimum(large, NUM_BUCKETS - 1)
    return jnp.where(d < max_exact, d, large)


def _bias_kernel(rb_ref, bm_ref, bw_ref, bs_ref, om_ref, ow_ref, os_ref, *, n_heads):
    h = pl.program_id(0)

    def lookup(buckets, col):
        acc = jnp.zeros(buckets.shape, _F32)
        for k in range(NUM_BUCKETS):
            acc = jnp.where(buckets == k, rb_ref[k, col], acc)
        return acc

    for d in range(bm_ref.shape[0]):
        om_ref[0, d] = lookup(bm_ref[d], h)
    for d in range(bw_ref.shape[0]):
        ow_ref[0, d] = lookup(bw_ref[d], h) * LOG2E
    os_ref[0] = lookup(bs_ref[...], n_heads + h)


def _bias_tiles(rel_bias):
    n_heads = rel_bias.shape[1] // 2

    def toeplitz(size, count):
        i = jnp.arange(size)[:, None]
        j = jnp.arange(size)[None, :]
        return jnp.stack([_rel_bucket(d * size + i - j) for d in range(count)])

    bm = toeplitz(MOBA_BLOCK, MOBA_NEAR_TILES)
    bw = toeplitz(MOBA_WIDE, MOBA_WIDE_NEAR_TILES)
    w = SWA_WINDOW
    bs = _rel_bucket(jnp.arange(w)[:, None] + w - jnp.arange(2 * w)[None, :])
    return pl.pallas_call(
        functools.partial(_bias_kernel, n_heads=n_heads),
        grid=(n_heads,),
        in_specs=[pl.BlockSpec(memory_space=pltpu.SMEM),
                  pl.BlockSpec(bm.shape, lambda h: (0, 0, 0)),
                  pl.BlockSpec(bw.shape, lambda h: (0, 0, 0)),
                  pl.BlockSpec(bs.shape, lambda h: (0, 0))],
        out_specs=[pl.BlockSpec((1,) + bm.shape, lambda h: (h, 0, 0, 0)),
                   pl.BlockSpec((1,) + bw.shape, lambda h: (h, 0, 0, 0)),
                   pl.BlockSpec((1,) + bs.shape, lambda h: (h, 0, 0))],
        out_shape=[jax.ShapeDtypeStruct((n_heads,) + bm.shape, _F32),
                   jax.ShapeDtypeStruct((n_heads,) + bw.shape, _F32),
                   jax.ShapeDtypeStruct((n_heads,) + bs.shape, _F32)],
        compiler_params=pltpu.CompilerParams(
            dimension_semantics=("arbitrary",), vmem_limit_bytes=_VMEM_LIMIT),
        name="bias_tiles",
    )(rel_bias, bm, bw, bs)


def _forget_cumsum(ff_ref, bf_ref, c_sc):
    t_len = c_sc.shape[0]
    ff = ff_ref[0] + bf_ref[...]
    log_f = jnp.minimum(ff, 0.0) - jnp.log(1.0 + jnp.exp(-jnp.abs(ff)))
    parts = [part.astype(_MXU_DTYPE) for part in _split3(log_f)]
    ch = 256
    tri = (lax.broadcasted_iota(jnp.int32, (ch, ch), 0)
           >= lax.broadcasted_iota(jnp.int32, (ch, ch), 1)).astype(_MXU_DTYPE)
    carry = jnp.zeros((1, LANES), _F32)
    for r in range(t_len // ch):
        inc = carry
        for part in reversed(parts):
            inc = inc + jnp.dot(tri, part[r * ch:(r + 1) * ch], preferred_element_type=_F32)
        c_sc[r * ch:(r + 1) * ch, :] = inc
        carry = inc[ch - 1:ch, :]


def _lane_column(x, col):
    return jnp.sum(jnp.where(_lane_iota(x.shape) == col, x, 0.0), axis=-1, keepdims=True)


def _lane_fields(shape, a0, fields):
    lane = _lane_iota(shape)
    out = jnp.zeros(shape, _F32)
    for n, f in enumerate(fields):
        out = jnp.where(lane == a0 + n, f, out)
    return out


def _bounded_weights(ps_ref, q_augs, kaug_sc, r, bias, diagonal):
    tq = q_augs[0].shape[0]
    for h in range(len(q_augs)):
        s = lax.dot_general(q_augs[h], kaug_sc[h, pl.ds(r, tq), :], _NT,
                            preferred_element_type=_F32)
        if bias is not None:
            s = s + bias(h)
        if diagonal:
            s = jnp.where(_causal(tq), s, NEG_INF)
        ps_ref[h] = jnp.exp2(s).astype(_MXU_DTYPE)


def _bounded_values(ps_ref, vext_sc, r, acc_sc):
    tq = ps_ref.shape[1]
    for p in range(acc_sc.shape[0]):
        acc_sc[p] += (
            jnp.dot(ps_ref[2 * p], vext_sc[2 * p, pl.ds(r, tq), :], preferred_element_type=_F32)
            + jnp.dot(ps_ref[2 * p + 1], vext_sc[2 * p + 1, pl.ds(r, tq), :],
                      preferred_element_type=_F32))


def _bounded_store(acc_sc, g_ref, o_ref):
    tq = acc_sc.shape[1]
    for p in range(acc_sc.shape[0]):
        acc = acc_sc[p]
        den = acc[:, LANES:]
        l = jnp.where(_own_half(0, (tq, LANES)), _lane_column(den, 0), _lane_column(den, 1))
        cols = slice(p * LANES, (p + 1) * LANES)
        o_ref[0, :, cols] = (acc[:, :LANES] / l * _silu(g_ref[0, :, cols])).astype(o_ref.dtype)


def _bounded_pipeline_step(n, ps_bufs, weights, values):
    def new_in(slot):
        def run():
            weights(ps_bufs[slot])
            values(ps_bufs[1 - slot])
        return run
    lax.cond(n % 2 == 0, new_in(0), new_in(1))


def _bounded_finish(n_done, ps_bufs, values):
    lax.cond(n_done % 2 == 0, lambda: values(ps_bufs[0]), lambda: values(ps_bufs[1]))


def _fox_extra_base(h):
    p, hh = divmod(h, 2)
    return (HEAD_DIM if hh == 0 else 0) + 8 * p


def _fox_extra_lanes(fields):
    rows = max(f.shape[1] for fs in fields for f in fs if hasattr(f, "shape"))
    sub = lax.broadcasted_iota(jnp.int32, (8, rows), 0)
    blocks = {}
    for h, fs in enumerate(fields):
        blk = jnp.zeros((8, rows), _F32)
        for n, f in enumerate(fs):
            blk = jnp.where(sub == n, f, blk)
        blocks[_fox_extra_base(h)] = blk
    pieces, at = [], 0
    for base in sorted(blocks):
        if base > at:
            pieces.append(jnp.zeros((base - at, rows), _F32))
        pieces.append(blocks[base])
        at = base + 8
    pieces.append(jnp.zeros((LANES - at, rows), _F32))
    return jnp.concatenate(pieces, axis=0).T


def _fox_bounded_kernel(bound_ref, q_ref, k_ref, v_ref, ff_ref, g_ref, bf_ref, o_ref,
                        kaug_sc, vext_sc, c_sc, ps_a, ps_b, acc_sc):
    i = pl.program_id(1)
    tq = q_ref.shape[1]
    n_heads = kaug_sc.shape[0]

    @pl.when(i == 0)
    def _prep():
        _forget_cumsum(ff_ref, bf_ref, c_sc)
        hi, mid, lo = _split3(-(c_sc[...] * LOG2E).T[0:8, :])
        extra = _fox_extra_lanes([[hi[h:h + 1], mid[h:h + 1], lo[h:h + 1], 1.0, 1.0, 1.0, 1.0, 1.0]
                                  for h in range(n_heads)])
        lane = _lane_iota(extra.shape)
        for h in range(n_heads):
            p, hh = divmod(h, 2)
            kn = k_ref[0, :, p * LANES:(p + 1) * LANES]
            v = v_ref[0, :, p * LANES:(p + 1) * LANES]
            a0 = _fox_extra_base(h)
            aug = jnp.where((lane >= a0) & (lane < a0 + 8), extra, 0.0)
            own = _own_half(hh, kn.shape)
            kaug_sc[h] = jnp.where(own, kn, aug).astype(_MXU_DTYPE)
            vext_sc[h, :, 0:LANES] = jnp.where(own, v, 0.0).astype(_MXU_DTYPE)
            vext_sc[h, :, LANES:2 * LANES] = jnp.where(lane == hh, 1.0, 0.0).astype(_MXU_DTYPE)

    row0 = pl.multiple_of(i * tq, tq)
    off = jnp.full((1, 1), -LOG2E, _F32) * bound_ref[0]
    off_hi = off.astype(_MXU_DTYPE).astype(_F32)
    hi, mid, lo = _split3((c_sc[pl.ds(row0, tq), :] * LOG2E).T[0:8, :])
    extra = _fox_extra_lanes([[1.0, 1.0, 1.0, hi[h:h + 1], mid[h:h + 1], lo[h:h + 1],
                               off_hi, off - off_hi] for h in range(n_heads)])
    lane = _lane_iota(extra.shape)
    q_augs = []
    for h in range(n_heads):
        p, hh = divmod(h, 2)
        q = q_ref[0, :, p * LANES:(p + 1) * LANES] * (ATTN_SCALE * LOG2E)
        a0 = _fox_extra_base(h)
        aug = jnp.where((lane >= a0) & (lane < a0 + 8), extra, 0.0)
        q_augs.append(jnp.where(_own_half(hh, q.shape), q, aug).astype(_MXU_DTYPE))

    head_lanes = _lane_iota((1, LANES)) < n_heads
    c_first = c_sc[pl.ds(row0, 1), :]
    cutoff = EXP_ZERO_CUTOFF - 2.0 * bound_ref[0]

    def live(j):
        c_last = c_sc[pl.ds(jnp.maximum(j, 0) * tq + (tq - 1), 1), :]
        gap = jnp.max(jnp.where(head_lanes, c_first - c_last, -jnp.inf))
        return (j >= 0) & (gap >= cutoff)

    ps_bufs = (ps_a, ps_b)

    def values_at(r):
        return lambda ps_ref: _bounded_values(ps_ref, vext_sc, pl.multiple_of(r, tq), acc_sc)

    def body(state):
        j, _, r_prev, n = state
        r = pl.multiple_of(j * tq, tq)
        _bounded_pipeline_step(
            n, ps_bufs,
            lambda ps_ref: _bounded_weights(ps_ref, q_augs, kaug_sc, r, None, False),
            values_at(r_prev))
        return j - 1, live(j - 1), r, n + 1

    acc_sc[...] = jnp.zeros(acc_sc.shape, _F32)
    _bounded_weights(ps_a, q_augs, kaug_sc, row0, None, True)
    state = lax.while_loop(lambda state: state[1], body,
                           (i - 1, live(i - 1), row0, jnp.int32(1)))
    _bounded_finish(state[3] - 1, ps_bufs, values_at(state[2]))
    _bounded_store(acc_sc, g_ref, o_ref)


def _fox_bounded(proj, bf_row, bound, *, tq=512):
    b, t, _ = proj.shape
    n_heads = GROUP_WIDTH // HEAD_DIM
    wide = GROUP_WIDTH // LANES
    return pl.pallas_call(
        _fox_bounded_kernel,
        grid=(b, t // tq),
        in_specs=[pl.BlockSpec(memory_space=pltpu.SMEM),
                  pl.BlockSpec((1, tq, GROUP_WIDTH), lambda bb, i: (bb, i, COL_FQ // wide)),
                  pl.BlockSpec((1, t, GROUP_WIDTH), lambda bb, i: (bb, 0, COL_FK // wide)),
                  pl.BlockSpec((1, t, GROUP_WIDTH), lambda bb, i: (bb, 0, COL_FV // wide)),
                  pl.BlockSpec((1, t, LANES), lambda bb, i: (bb, 0, COL_FF)),
                  pl.BlockSpec((1, tq, GROUP_WIDTH), lambda bb, i: (bb, i, COL_FG // wide)),
                  pl.BlockSpec((1, LANES), lambda bb, i: (0, 0))],
        out_specs=pl.BlockSpec((1, tq, GROUP_WIDTH), lambda bb, i: (bb, i, 0)),
        out_shape=jax.ShapeDtypeStruct((b, t, GROUP_WIDTH), _MXU_DTYPE),
        scratch_shapes=[pltpu.VMEM((n_heads, t, LANES), _MXU_DTYPE),
                        pltpu.VMEM((n_heads, t, 2 * LANES), _MXU_DTYPE),
                        pltpu.VMEM((t, LANES), _F32),
                        pltpu.VMEM((n_heads, tq, tq), _MXU_DTYPE),
                        pltpu.VMEM((n_heads, tq, tq), _MXU_DTYPE),
                        pltpu.VMEM((n_heads // 2, tq, 2 * LANES), _F32)],
        compiler_params=pltpu.CompilerParams(
            dimension_semantics=("arbitrary", "arbitrary"),
            vmem_limit_bytes=_VMEM_LIMIT_WIDE),
        name="fox_bounded",
    )(bound, proj, proj, proj, proj, proj, bf_row)


def _fox_kernel(q_ref, k_ref, v_ref, ff_ref, g_ref, bf_ref, o_ref,
                kaug_sc, v_sc, c_sc, *, tq):
    p = pl.program_id(1)
    i = pl.program_id(2)

    def head_column(x, hh):
        return _lane_column(x, 2 * p + hh)

    @pl.when(i == 0)
    def _prep():
        pl.when(p == 0)(lambda: _forget_cumsum(ff_ref, bf_ref, c_sc))
        kn = k_ref[0]
        c_all = c_sc[...]
        lane = _lane_iota(kn.shape)
        for hh in range(2):
            a0 = HEAD_DIM if hh == 0 else 0
            hi, mid, lo = _split3(-head_column(c_all, hh))
            aug = jnp.where(lane == a0, hi,
                            jnp.where(lane == a0 + 1, mid,
                                      jnp.where(lane == a0 + 2, lo,
                                                jnp.where((lane >= a0 + 3) & (lane < a0 + 6), 1.0, 0.0))))
            kaug_sc[hh] = jnp.where(_own_half(hh, kn.shape), kn, aug).astype(_MXU_DTYPE)
        v_sc[...] = v_ref[0].astype(_MXU_DTYPE)

    q = q_ref[0] * ATTN_SCALE
    row0 = pl.multiple_of(i * tq, tq)
    c_t = c_sc[pl.ds(row0, tq), :]
    lane = _lane_iota(q.shape)
    outs = []
    for hh in range(2):
        a0 = HEAD_DIM if hh == 0 else 0
        hi, mid, lo = _split3(head_column(c_t, hh))
        aug = jnp.where((lane >= a0) & (lane < a0 + 3), 1.0,
                        jnp.where(lane == a0 + 3, hi,
                                  jnp.where(lane == a0 + 4, mid,
                                            jnp.where(lane == a0 + 5, lo, 0.0))))
        q_aug = jnp.where(_own_half(hh, q.shape), q, aug).astype(_MXU_DTYPE)

        s = lax.dot_general(q_aug, kaug_sc[hh, pl.ds(row0, tq), :], _NT,
                            preferred_element_type=_F32)
        s = jnp.where(_causal(tq), s, NEG_INF)
        m = jnp.max(s, axis=-1, keepdims=True)
        pr = jnp.exp(s - m)
        l = jnp.sum(pr, axis=-1, keepdims=True)
        acc = jnp.dot(pr.astype(_MXU_DTYPE), v_sc[pl.ds(row0, tq), :], preferred_element_type=_F32)

        def body(j, carry, q_aug=q_aug, hh=hh):
            r = pl.multiple_of(j * tq, tq)
            s = lax.dot_general(q_aug, kaug_sc[hh, pl.ds(r, tq), :], _NT,
                                preferred_element_type=_F32)
            return _flash_update(s, v_sc[pl.ds(r, tq), :], *carry)

        m, l, acc = lax.fori_loop(0, i, body, (m, l, acc))
        outs.append(acc / l)
    o = jnp.where(_own_half(0, outs[0].shape), outs[0], outs[1])
    o_ref[0] = (o * _silu(g_ref[0])).astype(o_ref.dtype)


def _fox(proj, bf_row, *, tq=256):
    b, t, _ = proj.shape
    return pl.pallas_call(
        functools.partial(_fox_kernel, tq=tq),
        grid=(b, 2, t // tq),
        in_specs=[pl.BlockSpec((1, tq, LANES), lambda bb, p, i: (bb, i, COL_FQ + p)),
                  pl.BlockSpec((1, t, LANES), lambda bb, p, i: (bb, 0, COL_FK + p)),
                  pl.BlockSpec((1, t, LANES), lambda bb, p, i: (bb, 0, COL_FV + p)),
                  pl.BlockSpec((1, t, LANES), lambda bb, p, i: (bb, 0, COL_FF)),
                  pl.BlockSpec((1, tq, LANES), lambda bb, p, i: (bb, i, COL_FG + p)),
                  pl.BlockSpec((1, LANES), lambda bb, p, i: (0, 0))],
        out_specs=pl.BlockSpec((1, tq, LANES), lambda bb, p, i: (bb, i, p)),
        out_shape=jax.ShapeDtypeStruct((b, t, GROUP_WIDTH), _MXU_DTYPE),
        scratch_shapes=[pltpu.VMEM((2, t, LANES), _MXU_DTYPE),
                        pltpu.VMEM((t, LANES), _MXU_DTYPE),
                        pltpu.VMEM((t, LANES), _F32)],
        compiler_params=pltpu.CompilerParams(
            dimension_semantics=("arbitrary", "arbitrary", "arbitrary"),
            vmem_limit_bytes=_VMEM_LIMIT),
        name="fox",
    )(proj, proj, proj, proj, proj, bf_row)


def _moba_block_means(kn, kmean_sc):
    kmean_sc[...] = jnp.zeros(kmean_sc.shape, _F32)
    for n in range(kn.shape[0] // MOBA_BLOCK):
        kmean_sc[n:n + 1, :] = jnp.mean(kn[n * MOBA_BLOCK:(n + 1) * MOBA_BLOCK], axis=0, keepdims=True)


def _moba_select(q_head, kmean, past):
    gate = lax.dot_general(q_head, kmean, _NT, precision=lax.Precision.HIGHEST,
                           preferred_element_type=_F32)
    lane_f = _lane_iota(gate.shape).astype(_F32)
    cand = jnp.where(past, gate, -jnp.inf)
    sel = jnp.zeros(gate.shape, _F32)
    for _ in range(MOBA_TOPK):
        mx = jnp.max(cand, axis=-1, keepdims=True)
        is_max = (cand == mx) & (mx > -jnp.inf)
        first = jnp.min(jnp.where(is_max, lane_f, float(LANES)), axis=-1, keepdims=True)
        pick = lane_f == first
        sel = jnp.where(pick, 1.0, sel)
        cand = jnp.where(pick, -jnp.inf, cand)
    return sel


def _moba_bounded_kernel(rb_ref, bound_ref, q_ref, k_ref, v_ref, g_ref, bias_ref, o_ref,
                         kaug_sc, vext_sc, kmean_sc, ps_a, ps_b, acc_sc):
    i = pl.program_id(1)
    tq = q_ref.shape[1]
    n_heads = kaug_sc.shape[0]
    blk = MOBA_BLOCK
    near = MOBA_WIDE_NEAR_TILES

    @pl.when(i == 0)
    def _prep():
        lane = _lane_iota((k_ref.shape[1], LANES))
        row_blk = lax.broadcasted_iota(jnp.int32, lane.shape, 0) // blk
        for h in range(n_heads):
            p, hh = divmod(h, 2)
            kn = k_ref[0, :, p * LANES:(p + 1) * LANES]
            v = v_ref[0, :, p * LANES:(p + 1) * LANES]
            if hh == 0:
                _moba_block_means(kn, kmean_sc.at[p])
            a0 = HEAD_DIM if hh == 0 else 0
            aug = jnp.where((lane - a0 == row_blk) | (lane - a0 - 16 == row_blk)
                            | (lane - a0 == 32) | (lane - a0 == 33), 1.0, 0.0)
            own = _own_half(hh, kn.shape)
            kaug_sc[h] = jnp.where(own, kn, aug).astype(_MXU_DTYPE)
            vext_sc[h, :, 0:LANES] = jnp.where(own, v, 0.0).astype(_MXU_DTYPE)
            vext_sc[h, :, LANES:2 * LANES] = jnp.where(lane == hh, 1.0, 0.0).astype(_MXU_DTYPE)

    row0 = pl.multiple_of(i * tq, tq)
    off = jnp.full((1, 1), -LOG2E, _F32) * bound_ref[0]
    off_hi = off.astype(_MXU_DTYPE).astype(_F32)
    blk_n = lax.broadcasted_iota(jnp.int32, (16, tq), 0)
    blk_f = blk_n.astype(_F32)
    q_blk = i * (tq // blk) + lax.broadcasted_iota(jnp.int32, (16, tq), 1) // blk
    past = blk_n < q_blk
    far = i - blk_n // 2 >= near
    sub8 = lax.broadcasted_iota(jnp.int32, (8, tq), 0)
    off_rows = jnp.where(sub8 == 0, off_hi, jnp.where(sub8 == 1, off - off_hi, 0.0))
    q_augs = []
    for h in range(n_heads):
        p, hh = divmod(h, 2)
        qn = q_ref[0, :, p * LANES:(p + 1) * LANES]
        own = _own_half(hh, qn.shape)
        gate = lax.dot_general(kmean_sc[p, 0:16, :], jnp.where(own, qn, 0.0), _NT,
                               precision=lax.Precision.HIGHEST, preferred_element_type=_F32)
        cand = jnp.where(past, gate, -jnp.inf)
        sel = jnp.zeros((16, tq), _F32)
        for _ in range(MOBA_TOPK):
            mx = jnp.max(cand, axis=0, keepdims=True)
            is_max = (cand == mx) & (mx > -jnp.inf)
            first = jnp.min(jnp.where(is_max, blk_f, float(LANES)), axis=0, keepdims=True)
            pick = blk_f == first
            sel = jnp.where(pick, 1.0, sel)
            cand = jnp.where(pick, -jnp.inf, cand)
        c_far = jnp.full((1, 1), LOG2E, _F32) * rb_ref[NUM_BUCKETS - 1, h]
        far_hi = c_far.astype(_MXU_DTYPE).astype(_F32)
        pen = jnp.where((blk_n == q_blk) | (past & (sel != 0.0)), 0.0, NEG_INF)
        fields = jnp.concatenate([pen + jnp.where(far, far_hi, 0.0),
                                  jnp.where(far, c_far - far_hi, 0.0),
                                  off_rows,
                                  jnp.zeros((HEAD_DIM - 40, tq), _F32)], axis=0)
        blank = jnp.zeros((HEAD_DIM, tq), _F32)
        aug = jnp.concatenate([blank, fields] if hh == 0 else [fields, blank], axis=0).T
        q_augs.append(jnp.where(own, qn * (ATTN_SCALE * LOG2E), aug).astype(_MXU_DTYPE))

    ps_bufs = (ps_a, ps_b)

    def values_at(r):
        return lambda ps_ref: _bounded_values(ps_ref, vext_sc, pl.multiple_of(r, tq), acc_sc)

    def body(j, state, with_bias):
        r_prev, n = state
        r = pl.multiple_of(j * tq, tq)
        bias = (lambda h: bias_ref[h, i - j]) if with_bias else None
        _bounded_pipeline_step(
            n, ps_bufs,
            lambda ps_ref: _bounded_weights(ps_ref, q_augs, kaug_sc, r, bias, False),
            values_at(r_prev))
        return r, n + 1

    acc_sc[...] = jnp.zeros(acc_sc.shape, _F32)
    _bounded_weights(ps_a, q_augs, kaug_sc, row0, lambda h: bias_ref[h, 0], True)
    state = (row0, jnp.int32(1))
    near_lo = jnp.maximum(i - (near - 1), 0)
    state = lax.fori_loop(near_lo, i, functools.partial(body, with_bias=True), state)
    state = lax.fori_loop(0, near_lo, functools.partial(body, with_bias=False), state)
    _bounded_finish(state[1] - 1, ps_bufs, values_at(state[0]))
    _bounded_store(acc_sc, g_ref, o_ref)


def _moba_bounded(proj, rel_bias, bias_tiles, bound):
    b, t, _ = proj.shape
    tq = MOBA_WIDE
    assert t % tq == 0 and t // MOBA_BLOCK <= 16
    n_heads = GROUP_WIDTH // HEAD_DIM
    wide = GROUP_WIDTH // LANES
    resident = pl.Buffered(1)
    return pl.pallas_call(
        _moba_bounded_kernel,
        grid=(b, t // tq),
        in_specs=[pl.BlockSpec(memory_space=pltpu.SMEM),
                  pl.BlockSpec(memory_space=pltpu.SMEM),
                  pl.BlockSpec((1, tq, GROUP_WIDTH), lambda bb, i: (bb, i, COL_MQ // wide)),
                  pl.BlockSpec((1, t, GROUP_WIDTH), lambda bb, i: (bb, 0, COL_MK // wide),
                               pipeline_mode=resident),
                  pl.BlockSpec((1, t, GROUP_WIDTH), lambda bb, i: (bb, 0, COL_MV // wide),
                               pipeline_mode=resident),
                  pl.BlockSpec((1, tq, GROUP_WIDTH), lambda bb, i: (bb, i, COL_MG // wide)),
                  pl.BlockSpec((n_heads, MOBA_WIDE_NEAR_TILES, tq, tq), lambda bb, i: (0, 0, 0, 0),
                               pipeline_mode=resident)],
        out_specs=pl.BlockSpec((1, tq, GROUP_WIDTH), lambda bb, i: (bb, i, 0)),
        out_shape=jax.ShapeDtypeStruct((b, t, GROUP_WIDTH), _MXU_DTYPE),
        scratch_shapes=[pltpu.VMEM((n_heads, t, LANES), _MXU_DTYPE),
                        pltpu.VMEM((n_heads, t, 2 * LANES), _MXU_DTYPE),
                        pltpu.VMEM((n_heads // 2, LANES, LANES), _F32),
                        pltpu.VMEM((n_heads, tq, tq), _MXU_DTYPE),
                        pltpu.VMEM((n_heads, tq, tq), _MXU_DTYPE),
                        pltpu.VMEM((n_heads // 2, tq, 2 * LANES), _F32)],
        compiler_params=pltpu.CompilerParams(
            dimension_semantics=("arbitrary", "arbitrary"),
            vmem_limit_bytes=_VMEM_LIMIT_WIDE),
        name="moba_bounded",
    )(rel_bias, bound, proj, proj, proj, proj, bias_tiles)


def _moba_kernel(rb_ref, q_ref, k_ref, v_ref, g_ref, bias_ref, o_ref,
                 kaug_sc, v_sc, kmean_sc):
    p = pl.program_id(1)
    i = pl.program_id(2)
    t_len = k_ref.shape[1]
    blk = MOBA_BLOCK
    nblk = t_len // blk
    near = MOBA_NEAR_TILES

    @pl.when(i == 0)
    def _prep():
        kn = k_ref[0]
        _moba_block_means(kn, kmean_sc)
        lane = _lane_iota(kn.shape)
        row_blk = lax.broadcasted_iota(jnp.int32, kn.shape, 0) // blk
        for hh in range(2):
            a0 = HEAD_DIM if hh == 0 else 0
            onehot = jnp.where((lane - a0 == row_blk) | (lane - a0 - 16 == row_blk), 1.0, 0.0)
            kaug_sc[hh] = jnp.where(_own_half(hh, kn.shape), kn, onehot).astype(_MXU_DTYPE)
        v_sc[...] = v_ref[0].astype(_MXU_DTYPE)

    qn = q_ref[0]
    row0 = pl.multiple_of(i * blk, blk)
    lane = _lane_iota(qn.shape)
    lane_f = lane.astype(_F32)
    past = lane < i
    outs = []
    for hh in range(2):
        own = _own_half(hh, qn.shape)
        sel = _moba_select(jnp.where(own, qn, 0.0), kmean_sc[...], past)
        c_far = jnp.full((1, LANES), rb_ref[NUM_BUCKETS - 1, 2 * p + hh], _F32)
        far_hi = c_far.astype(_MXU_DTYPE).astype(_F32)
        far_lo = c_far - far_hi
        pen = jnp.where(past & (sel == 0.0), NEG_INF, 0.0)
        aug = jnp.where(lane < 16, pen + jnp.where(i - lane >= near, far_hi, 0.0),
                        jnp.where((lane < 32) & (i - (lane - 16) >= near), far_lo, 0.0))
        if hh == 0:
            aug = pltpu.roll(aug, HEAD_DIM, axis=1)
        q_aug = jnp.where(own, qn * ATTN_SCALE, aug).astype(_MXU_DTYPE)

        def scores(r, q_aug=q_aug, hh=hh):
            return lax.dot_general(q_aug, kaug_sc[hh, pl.ds(r, blk), :], _NT,
                                   preferred_element_type=_F32)

        s = scores(row0) + bias_ref[hh, 0]
        s = jnp.where(_causal(blk), s, NEG_INF)
        m = jnp.max(s, axis=-1, keepdims=True)
        pr = jnp.exp(s - m)
        l = jnp.sum(pr, axis=-1, keepdims=True)
        acc = jnp.dot(pr.astype(_MXU_DTYPE), v_sc[pl.ds(row0, blk), :], preferred_element_type=_F32)

        def near_body(j, carry, hh=hh, scores=scores):
            r = pl.multiple_of(j * blk, blk)
            return _flash_update(scores(r) + bias_ref[hh, i - j], v_sc[pl.ds(r, blk), :], *carry)

        def far_body(j, carry, scores=scores):
            r = pl.multiple_of(j * blk, blk)
            return _flash_update(scores(r), v_sc[pl.ds(r, blk), :], *carry)

        near_lo = jnp.maximum(i - (near - 1), 0)
        carry = lax.fori_loop(near_lo, i, near_body, (m, l, acc))
        m, l, acc = lax.fori_loop(0, near_lo, far_body, carry)
        outs.append(acc / l)
    o = jnp.where(_own_half(0, outs[0].shape), outs[0], outs[1])
    o_ref[0] = (o * _silu(g_ref[0])).astype(o_ref.dtype)


def _moba(proj, rel_bias, bias_tiles):
    b, t, _ = proj.shape
    blk = MOBA_BLOCK
    assert t % blk == 0 and t // blk <= 16
    return pl.pallas_call(
        _moba_kernel,
        grid=(b, 2, t // blk),
        in_specs=[pl.BlockSpec(memory_space=pltpu.SMEM),
                  pl.BlockSpec((1, blk, LANES), lambda bb, p, i: (bb, i, COL_MQ + p)),
                  pl.BlockSpec((1, t, LANES), lambda bb, p, i: (bb, 0, COL_MK + p)),
                  pl.BlockSpec((1, t, LANES), lambda bb, p, i: (bb, 0, COL_MV + p)),
                  pl.BlockSpec((1, blk, LANES), lambda bb, p, i: (bb, i, COL_MG + p)),
                  pl.BlockSpec((2, MOBA_NEAR_TILES, blk, blk), lambda bb, p, i: (p, 0, 0, 0))],
        out_specs=pl.BlockSpec((1, blk, LANES), lambda bb, p, i: (bb, i, p)),
        out_shape=jax.ShapeDtypeStruct((b, t, GROUP_WIDTH), _MXU_DTYPE),
        scratch_shapes=[pltpu.VMEM((2, t, LANES), _MXU_DTYPE),
                        pltpu.VMEM((t, LANES), _MXU_DTYPE),
                        pltpu.VMEM((LANES, LANES), _F32)],
        compiler_params=pltpu.CompilerParams(
            dimension_semantics=("arbitrary", "arbitrary", "arbitrary"),
            vmem_limit_bytes=_VMEM_LIMIT),
        name="moba",
    )(rel_bias, proj, proj, proj, proj, bias_tiles)


def _sb_kernel(q_ref, k_ref, v_ref, g_ref, o_ref, k_sc, v_sc, *, tq):
    i = pl.program_id(2)

    @pl.when(i == 0)
    def _prep():
        k_sc[...] = k_ref[0].astype(_MXU_DTYPE)
        v = v_ref[0]
        for hh in range(2):
            v_sc[hh] = jnp.where(_own_half(hh, v.shape), v, 0.0).astype(_MXU_DTYPE)

    subs = q_ref.shape[1] // tq
    q_heads = []
    for u in range(subs):
        q = q_ref[0, u * tq:(u + 1) * tq, :] * (ATTN_SCALE * LOG2E)
        q_heads.append([jnp.where(_own_half(hh, q.shape), q, 0.0).astype(_MXU_DTYPE)
                        for hh in range(2)])
    strict =(lax.broadcasted_iota(jnp.int32, (tq, tq), 0)
              > lax.broadcasted_iota(jnp.int32, (tq, tq), 1))
    after = strict.astype(_MXU_DTYPE)
    sign_bit = jnp.uint32(0x80000000)

    def pair(back, runs, diagonal):
        chains = [(u, t, hh) for u in range(subs) for t in range(2) for hh in range(2)]
        tile_idx = [[subs * i + u - back - t for t in range(2)] for u in range(subs)]
        valid = [[(j >= 0).astype(_F32) for j in js] for js in tile_idx]
        rows = [[pl.multiple_of(jnp.maximum(j, 0) * tq, tq) for j in js] for js in tile_idx]
        masked = [diagonal and t == 0 for _, t, _ in chains]
        zs = [lax.dot_general(q_heads[u][hh], k_sc[pl.ds(rows[u][t], tq), :], _NT,
                              preferred_element_type=_F32) for u, t, hh in chains]
        drops = []
        for z, msk in zip(zs, masked):
            neg_abs = lax.bitcast_convert_type(lax.bitcast_convert_type(z, jnp.uint32) | sign_bit, _F32)
            drop = jnp.maximum(z, 0.0) + jnp.log(1.0 + jnp.exp2(neg_abs)) * LOG2E
            drops.append(jnp.where(strict, drop, 0.0) if msk else drop)
        laters = []
        for drop in drops:
            hi = drop.astype(_MXU_DTYPE)
            lo = (drop - hi.astype(_F32)).astype(_MXU_DTYPE)
            laters.append(jnp.dot(hi, after, preferred_element_type=_F32)
                          + jnp.dot(lo, after, preferred_element_type=_F32))
        pvs = []
        for (u, t, hh), z, drop, later, msk in zip(chains, zs, drops, laters, masked):
            w = jnp.exp2(z - drop - later)
            if msk:
                w = jnp.where(strict, w, 0.0)
            pvs.append(jnp.dot(w.astype(_MXU_DTYPE), v_sc[hh, pl.ds(rows[u][t], tq), :],
                               preferred_element_type=_F32))
        sums = [jnp.sum(drop, axis=-1, keepdims=True) for drop in drops]
        new_runs, outs = [], []
        for u in range(subs):
            new_runs.append([])
            out = None
            for hh in range(2):
                run = runs[u][hh]
                for t in range(2):
                    c = chains.index((u, t, hh))
                    pv = pvs[c] * (jnp.exp2(run) * valid[u][t])
                    out = pv if out is None else out + pv
                    run = run - sums[c] * valid[u][t]
                new_runs[u].append(run)
            outs.append(out)
        return new_runs, outs

    def alive(runs):
        top = functools.reduce(jnp.maximum, [r for rs in runs for r in rs])
        return jnp.max(top) >= EXP_ZERO_CUTOFF * LOG2E

    def flat(runs):
        return [r for rs in runs for r in rs]

    def nested(flat_runs):
        return [flat_runs[2 * u:2 * u + 2] for u in range(subs)]

    zero = jnp.zeros((tq, 1), _F32)
    runs, accs = pair(0, [[zero, zero] for _ in range(subs)], True)

    def body(state):
        back = state[0]
        runs, outs = pair(back, nested(list(state[2:2 + 2 * subs])), False)
        accs = [a + o for a, o in zip(state[2 + 2 * subs:], outs)]
        return (back + 2, alive(runs), *flat(runs), *accs)

    state = lax.while_loop(lambda state: (subs * i + subs - 1 - state[0] >= 0) & state[1], body,
                           (jnp.int32(2), alive(runs), *flat(runs), *accs))
    for u in range(subs):
        o_ref[0, u * tq:(u + 1) * tq, :] = (
            state[2 + 2 * subs + u] * _silu(g_ref[0, u * tq:(u + 1) * tq, :])).astype(o_ref.dtype)


def _sb(proj, *, tq=256, subs=2):
    b, t, _ = proj.shape
    rows = tq * subs
    return pl.pallas_call(
        functools.partial(_sb_kernel, tq=tq),
        grid=(b, 2, t // rows),
        in_specs=[pl.BlockSpec((1, rows, LANES), lambda bb, p, i: (bb, i, COL_SQ + p)),
                  pl.BlockSpec((1, t, LANES), lambda bb, p, i: (bb, 0, COL_SK + p)),
                  pl.BlockSpec((1, t, LANES), lambda bb, p, i: (bb, 0, COL_SV + p)),
                  pl.BlockSpec((1, rows, LANES), lambda bb, p, i: (bb, i, COL_SG + p))],
        out_specs=pl.BlockSpec((1, rows, LANES), lambda bb, p, i: (bb, i, p)),
        out_shape=jax.ShapeDtypeStruct((b, t, GROUP_WIDTH), _MXU_DTYPE),
        scratch_shapes=[pltpu.VMEM((t, LANES), _MXU_DTYPE),
                        pltpu.VMEM((2, t, LANES), _MXU_DTYPE)],
        compiler_params=pltpu.CompilerParams(
            dimension_semantics=("arbitrary", "arbitrary", "arbitrary"),
            vmem_limit_bytes=_VMEM_LIMIT),
        name="stickbreak",
    )(proj, proj, proj, proj)


def _swa_kernel(sink_ref, q_ref, k_ref, v_ref, g_ref, bias_ref, o_ref,
                k_sc, v_sc, *, tq):
    kv = pl.program_id(1)
    i = pl.program_id(2)
    t_len = k_ref.shape[1]
    w = SWA_WINDOW

    @pl.when(i == 0)
    def _prep():
        kn = k_ref[0]
        v = v_ref[0]
        keep = (_lane_iota(kn.shape) < HEAD_DIM) == (kv == 0)
        k_sc[0:w, :] = jnp.zeros((w, LANES), _MXU_DTYPE)
        v_sc[0:w, :] = jnp.zeros((w, LANES), _MXU_DTYPE)
        k_sc[w:w + t_len, :] = jnp.where(keep, kn, pltpu.roll(kn, HEAD_DIM, axis=1)).astype(_MXU_DTYPE)
        v_sc[w:w + t_len, :] = jnp.where(keep, v, pltpu.roll(v, HEAD_DIM, axis=1)).astype(_MXU_DTYPE)

    q = q_ref[0] * ATTN_SCALE
    qi = lax.broadcasted_iota(jnp.int32, (w, 2 * w), 0)
    kj = lax.broadcasted_iota(jnp.int32, (w, 2 * w), 1)
    dist = qi + w - kj
    in_window = (dist >= 0) & (dist < w)
    n_sub = tq // w
    chains = [(u, g) for u in range(n_sub) for g in range(2)]
    rows = [pl.multiple_of((i * n_sub + u) * w, w) for u in range(n_sub)]
    ss = []
    for u, g in chains:
        q_u = q[u * w:(u + 1) * w]
        q_h = jnp.where(_own_half(g, q_u.shape), q_u, 0.0).astype(_MXU_DTYPE)
        ss.append(lax.dot_general(q_h, k_sc[pl.ds(rows[u], 2 * w), :], _NT,
                                  preferred_element_type=_F32))
    es, dens = [], []
    for (u, g), s in zip(chains, ss):
        allowed = in_window & (kj + (i * n_sub + u - 1) * w >= 0)
        s = jnp.where(allowed, s + bias_ref[g], NEG_INF)
        sink = sink_ref[2 * kv + g]
        m = jnp.maximum(jnp.max(s, axis=-1, keepdims=True), sink)
        e = jnp.exp(s - m)
        dens.append(jnp.sum(e, axis=-1, keepdims=True) + jnp.exp(sink - m))
        es.append(e.astype(_MXU_DTYPE))
    outs = [jnp.dot(e, v_sc[pl.ds(rows[u], 2 * w), :], preferred_element_type=_F32) / den
            for (u, g), e, den in zip(chains, es, dens)]
    for u in range(n_sub):
        o = jnp.where(_own_half(0, (w, LANES)), outs[2 * u], outs[2 * u + 1])
        o_ref[0, u * w:(u + 1) * w, :] = (o * _silu(g_ref[0, u * w:(u + 1) * w, :])).astype(o_ref.dtype)


def _swa(proj, sinks, bias_tiles, *, tq=512):
    b, t, _ = proj.shape
    w = SWA_WINDOW
    return pl.pallas_call(
        functools.partial(_swa_kernel, tq=tq),
        grid=(b, 2, t // tq),
        in_specs=[pl.BlockSpec(memory_space=pltpu.SMEM),
                  pl.BlockSpec((1, tq, LANES), lambda bb, kv, i: (bb, i, COL_WQ + kv)),
                  pl.BlockSpec((1, t, LANES), lambda bb, kv, i: (bb, 0, COL_WK)),
                  pl.BlockSpec((1, t, LANES), lambda bb, kv, i: (bb, 0, COL_WV)),
                  pl.BlockSpec((1, tq, LANES), lambda bb, kv, i: (bb, i, COL_WG + kv)),
                  pl.BlockSpec((2, w, 2 * w), lambda bb, kv, i: (kv, 0, 0))],
        out_specs=pl.BlockSpec((1, tq, LANES), lambda bb, kv, i: (bb, i, kv)),
        out_shape=jax.ShapeDtypeStruct((b, t, GROUP_WIDTH), _MXU_DTYPE),
        scratch_shapes=[pltpu.VMEM((t + w, LANES), _MXU_DTYPE),
                        pltpu.VMEM((t + w, LANES), _MXU_DTYPE)],
        compiler_params=pltpu.CompilerParams(
            dimension_semantics=("arbitrary", "arbitrary", "arbitrary"),
            vmem_limit_bytes=_VMEM_LIMIT),
        name="swa",
    )(sinks, proj, proj, proj, proj, bias_tiles)


def _outproj_kernel(x_ref, ya_ref, yb_ref, yc_ref, yd_ref, w_ref, o_ref):
    acc = x_ref[...]
    for g, y_ref in enumerate((ya_ref, yb_ref, yc_ref, yd_ref)):
        acc = acc + jnp.dot(y_ref[...], w_ref[g * GROUP_WIDTH:(g + 1) * GROUP_WIDTH, :],
                            preferred_element_type=_F32)
    o_ref[...] = acc


def _outproj(x2d, ys, w_all, layer):
    m, d = x2d.shape
    tm = 512
    y_spec = pl.BlockSpec((tm, GROUP_WIDTH), lambda i: (i, 0))
    return pl.pallas_call(
        _outproj_kernel,
        grid=(m // tm,),
        in_specs=[pl.BlockSpec((tm, d), lambda i: (i, 0)), y_spec, y_spec, y_spec, y_spec,
                  pl.BlockSpec((None,) + w_all.shape[1:], lambda i: (layer, 0, 0))],
        out_specs=pl.BlockSpec((tm, d), lambda i: (i, 0)),
        out_shape=jax.ShapeDtypeStruct((m, d), _F32),
        compiler_params=pltpu.CompilerParams(
            dimension_semantics=("arbitrary",), vmem_limit_bytes=_VMEM_LIMIT),
        name="outproj",
    )(x2d, *ys, w_all)


def _rearrange_w_in(w_in):
    off = 3 * GROUP_WIDTH
    main = jnp.concatenate([w_in[..., :off], w_in[..., off + FOX_FORGET_COLS:]], axis=-1)
    ff = w_in[..., off:off + FOX_FORGET_COLS]
    pad = PROJ_WIDTH - main.shape[-1] - FOX_FORGET_COLS
    return jnp.concatenate([main, ff, jnp.zeros(w_in.shape[:-1] + (pad,), w_in.dtype)], axis=-1)


def _qk_logit_bound(qk_gain):
    g = jnp.abs(qk_gain.astype(_F32))
    return ATTN_SCALE * HEAD_DIM * jnp.max(g[0]) * jnp.max(g[1])


def _qk_gain_row(fox_g, moba_g, swa_g):
    def tiles(g, n):
        return jnp.tile(g.astype(_F32), 2 * n)
    row = jnp.zeros((PROJ_WIDTH,), _F32)
    for col, g, n in ((COL_FQ, fox_g[0], 2), (COL_FK, fox_g[1], 2),
                      (COL_MQ, moba_g[0], 2), (COL_MK, moba_g[1], 2),
                      (COL_WQ, swa_g[0], 2), (COL_WK, swa_g[1], 1)):
        row = lax.dynamic_update_slice(row, tiles(g, n), (col * LANES,))
    return row[None, :]


def kernel(x, norm_gain, w_in, b_forget, fox_qk_gain, moba_qk_gain, swa_qk_gain, sinks, w_out, rel_bias):
    b, t, d = x.shape
    depth = w_in.shape[0]
    w_in_r = _rearrange_w_in(w_in).astype(_MXU_DTYPE)
    w_out_c = w_out.astype(_MXU_DTYPE)
    moba_bias, moba_wide_bias, swa_bias = _bias_tiles(rel_bias)
    moba_bias_max = jnp.max(jnp.abs(rel_bias[:, :rel_bias.shape[1] // 2].astype(_F32)))
    x2d = x.reshape(b * t, d)
    for layer in range(depth):
        qk_gain_row = _qk_gain_row(fox_qk_gain[layer], moba_qk_gain[layer], swa_qk_gain[layer])
        proj = _inproj(x2d, norm_gain[layer][None, :], w_in_r, layer, qk_gain_row)
        proj = proj.reshape(b, t, PROJ_WIDTH)
        bf_row = jnp.pad(b_forget[layer], (0, LANES - FOX_FORGET_COLS))[None, :]
        fox_bound = _qk_logit_bound(fox_qk_gain[layer])
        y_fox = lax.cond(2.0 * fox_bound <= BOUNDED_SOFTMAX_RANGE,
                         lambda: _fox_bounded(proj, bf_row, fox_bound.reshape(1)),
                         lambda: _fox(proj, bf_row))
        moba_bound = _qk_logit_bound(moba_qk_gain[layer]) + moba_bias_max
        y_moba = lax.cond(2.0 * moba_bound <= BOUNDED_SOFTMAX_RANGE,
                          lambda: _moba_bounded(proj, rel_bias, moba_wide_bias,
                                                moba_bound.reshape(1)),
                          lambda: _moba(proj, rel_bias, moba_bias))
        y_sb = _sb(proj)
        y_swa = _swa(proj, sinks[layer], swa_bias)
        ys = [y.reshape(b * t, GROUP_WIDTH) for y in (y_fox, y_moba, y_sb, y_swa)]
        x2d = _outproj(x2d, ys, w_out_c, layer)
    return x2d.reshape(b, t, d)
```

```python
import functools
import math

import jax
import jax.numpy as jnp
from jax import lax
from jax.experimental import pallas as pl
from jax.experimental.pallas import tpu as pltpu

HEAD_DIM = 64
LANES = 128
GROUP_WIDTH = 256
MOBA_BLOCK = 256
MOBA_TOPK = 3
SWA_WINDOW = 128
NUM_BUCKETS = 32
REL_MAX_DISTANCE = 1024
RMS_EPS = 1e-6
NEG_INF = -1e30
ATTN_SCALE = HEAD_DIM ** -0.5
LOG2E = math.log2(math.e)
BOUNDED_SOFTMAX_RANGE = 60.0
MOBA_NEAR_TILES = REL_MAX_DISTANCE // MOBA_BLOCK + 1
MOBA_WIDE = 2 * MOBA_BLOCK
MOBA_WIDE_NEAR_TILES = REL_MAX_DISTANCE // MOBA_WIDE + 1
FOX_FORGET_COLS = 4
EXP_ZERO_CUTOFF = -104.0

COL_FQ, COL_FK, COL_FV, COL_FG = 0, 2, 4, 6
COL_MQ, COL_MK, COL_MV, COL_MG = 8, 10, 12, 14
COL_SQ, COL_SK, COL_SV, COL_SG = 16, 18, 20, 22
COL_WQ, COL_WK, COL_WV, COL_WG = 24, 26, 27, 28
COL_FF = 30
PROJ_WIDTH = 32 * LANES
QK_NORM_COLS = (COL_FQ, COL_FQ + 1, COL_FK, COL_FK + 1, COL_MQ, COL_MQ + 1, COL_MK, COL_MK + 1,
                COL_WQ, COL_WQ + 1, COL_WK)

_MXU_DTYPE = jnp.bfloat16
_F32 = jnp.float32
_VMEM_LIMIT = 48 * 1024 * 1024
_VMEM_LIMIT_WIDE = 56 * 1024 * 1024

_NT = (((1,), (1,)), ((), ()))


def _lane_iota(shape):
    return lax.broadcasted_iota(jnp.int32, shape, len(shape) - 1)


def _pair_rms(x, gain_row):
    low = _lane_iota(x.shape) < HEAD_DIM
    sq = x * x
    ms_lo = jnp.sum(jnp.where(low, sq, 0.0), axis=-1, keepdims=True)
    ms_hi = jnp.sum(jnp.where(low, 0.0, sq), axis=-1, keepdims=True)
    ms = jnp.where(low, ms_lo, ms_hi) * (1.0 / HEAD_DIM)
    return x * lax.rsqrt(ms + RMS_EPS) * gain_row


def _silu(g):
    return g * (1.0 / (1.0 + jnp.exp(-g)))


def _split3(x):
    hi = x.astype(_MXU_DTYPE).astype(_F32)
    r = x - hi
    mid = r.astype(_MXU_DTYPE).astype(_F32)
    return hi, mid, r - mid


def _own_half(hh, shape):
    lane = _lane_iota(shape)
    return (lane < HEAD_DIM) if hh == 0 else (lane >= HEAD_DIM)


def _flash_update(s, v_tile, m, l, acc):
    m_new = jnp.maximum(m, jnp.max(s, axis=-1, keepdims=True))
    alpha = jnp.exp(m - m_new)
    p = jnp.exp(s - m_new)
    l = alpha * l + jnp.sum(p, axis=-1, keepdims=True)
    acc = alpha * acc + jnp.dot(p.astype(_MXU_DTYPE), v_tile, preferred_element_type=_F32)
    return m_new, l, acc


def _causal(tq):
    return (lax.broadcasted_iota(jnp.int32, (tq, tq), 0)
            >= lax.broadcasted_iota(jnp.int32, (tq, tq), 1))


def _inproj_kernel(x_ref, gain_ref, w_ref, qkg_ref, o_ref, *, tn):
    x = x_ref[...]
    ms = jnp.mean(x * x, axis=-1, keepdims=True)
    hn = (x * lax.rsqrt(ms + RMS_EPS) * gain_ref[...]).astype(_MXU_DTYPE)
    for n in range(o_ref.shape[1] // tn):
        acc = jnp.dot(hn, w_ref[:, n * tn:(n + 1) * tn], preferred_element_type=_F32)
        for c in range(n * tn // LANES, (n + 1) * tn // LANES):
            tile = acc[:, c * LANES - n * tn:(c + 1) * LANES - n * tn]
            if c in QK_NORM_COLS:
                tile = _pair_rms(tile, qkg_ref[:, c * LANES:(c + 1) * LANES])
            o_ref[:, c * LANES:(c + 1) * LANES] = tile


def _inproj(x2d, gain_row, w_all, layer, qk_gain_row):
    m, d = x2d.shape
    n = w_all.shape[2]
    tm = 512
    return pl.pallas_call(
        functools.partial(_inproj_kernel, tn=1024),
        grid=(m // tm,),
        in_specs=[pl.BlockSpec((tm, d), lambda i: (i, 0)),
                  pl.BlockSpec((1, d), lambda i: (0, 0)),
                  pl.BlockSpec((None, d, n), lambda i: (layer, 0, 0)),
                  pl.BlockSpec((1, n), lambda i: (0, 0))],
        out_specs=pl.BlockSpec((tm, n), lambda i: (i, 0)),
        out_shape=jax.ShapeDtypeStruct((m, n), _F32),
        compiler_params=pltpu.CompilerParams(
            dimension_semantics=("arbitrary",), vmem_limit_bytes=_VMEM_LIMIT),
        name="inproj",
    )(x2d, gain_row, w_all, qk_gain_row)


def _rel_bucket(dist):
    max_exact = NUM_BUCKETS // 2
    d = jnp.maximum(dist, 0)
    log_ratio = (jnp.log(jnp.maximum(d, 1).astype(_F32) / max_exact)
                 / math.log(REL_MAX_DISTANCE / max_exact))
    large = max_exact + (log_ratio * (NUM_BUCKETS - max_exact)).astype(jnp.int32)
    large = jnp.minimum(large, NUM_BUCKETS - 1)
    return jnp.where(d < max_exact, d, large)


def _bias_kernel(rb_ref, bm_ref, bw_ref, bs_ref, om_ref, ow_ref, os_ref, *, n_heads):
    h = pl.program_id(0)

    def lookup(buckets, col):
        acc = jnp.zeros(buckets.shape, _F32)
        for k in range(NUM_BUCKETS):
            acc = jnp.where(buckets == k, rb_ref[k, col], acc)
        return acc

    for d in range(bm_ref.shape[0]):
        om_ref[0, d] = lookup(bm_ref[d], h)
    for d in range(bw_ref.shape[0]):
        ow_ref[0, d] = lookup(bw_ref[d], h) * LOG2E
    os_ref[0] = lookup(bs_ref[...], n_heads + h)


def _bias_tiles(rel_bias):
    n_heads = rel_bias.shape[1] // 2

    def toeplitz(size, count):
        i = jnp.arange(size)[:, None]
        j = jnp.arange(size)[None, :]
        return jnp.stack([_rel_bucket(d * size + i - j) for d in range(count)])

    bm = toeplitz(MOBA_BLOCK, MOBA_NEAR_TILES)
    bw = toeplitz(MOBA_WIDE, MOBA_WIDE_NEAR_TILES)
    w = SWA_WINDOW
    bs = _rel_bucket(jnp.arange(w)[:, None] + w - jnp.arange(2 * w)[None, :])
    return pl.pallas_call(
        functools.partial(_bias_kernel, n_heads=n_heads),
        grid=(n_heads,),
        in_specs=[pl.BlockSpec(memory_space=pltpu.SMEM),
                  pl.BlockSpec(bm.shape, lambda h: (0, 0, 0)),
                  pl.BlockSpec(bw.shape, lambda h: (0, 0, 0)),
                  pl.BlockSpec(bs.shape, lambda h: (0, 0))],
        out_specs=[pl.BlockSpec((1,) + bm.shape, lambda h: (h, 0, 0, 0)),
                   pl.BlockSpec((1,) + bw.shape, lambda h: (h, 0, 0, 0)),
                   pl.BlockSpec((1,) + bs.shape, lambda h: (h, 0, 0))],
        out_shape=[jax.ShapeDtypeStruct((n_heads,) + bm.shape, _F32),
                   jax.ShapeDtypeStruct((n_heads,) + bw.shape, _F32),
                   jax.ShapeDtypeStruct((n_heads,) + bs.shape, _F32)],
        compiler_params=pltpu.CompilerParams(
            dimension_semantics=("arbitrary",), vmem_limit_bytes=_VMEM_LIMIT),
        name="bias_tiles",
    )(rel_bias, bm, bw, bs)


def _forget_cumsum(ff_ref, bf_ref, c_sc):
    t_len = c_sc.shape[0]
    ff = ff_ref[0] + bf_ref[...]
    log_f = jnp.minimum(ff, 0.0) - jnp.log(1.0 + jnp.exp(-jnp.abs(ff)))
    parts = [part.astype(_MXU_DTYPE) for part in _split3(log_f)]
    ch = 256
    tri = (lax.broadcasted_iota(jnp.int32, (ch, ch), 0)
           >= lax.broadcasted_iota(jnp.int32, (ch, ch), 1)).astype(_MXU_DTYPE)
    carry = jnp.zeros((1, LANES), _F32)
    for r in range(t_len // ch):
        inc = carry
        for part in reversed(parts):
            inc = inc + jnp.dot(tri, part[r * ch:(r + 1) * ch], preferred_element_type=_F32)
        c_sc[r * ch:(r + 1) * ch, :] = inc
        carry = inc[ch - 1:ch, :]


def _lane_column(x, col):
    return jnp.sum(jnp.where(_lane_iota(x.shape) == col, x, 0.0), axis=-1, keepdims=True)


def _lane_fields(shape, a0, fields):
    lane = _lane_iota(shape)
    out = jnp.zeros(shape, _F32)
    for n, f in enumerate(fields):
        out = jnp.where(lane == a0 + n, f, out)
    return out


def _write_ones_column(vext_sc, h, hh):
    @pl.when(pl.program_id(0) == 0)
    def _():
        lane = _lane_iota((vext_sc.shape[1], LANES))
        vext_sc[h, :, LANES:2 * LANES] = jnp.where(lane == hh, 1.0, 0.0).astype(_MXU_DTYPE)


def _bounded_weights(ps_ref, q_augs, kaug_sc, r, bias, diagonal):
    tq = q_augs[0].shape[0]
    if not diagonal:
        for h in range(len(q_augs)):
            s = lax.dot_general(q_augs[h], kaug_sc[h, pl.ds(r, tq), :], _NT,
                                preferred_element_type=_F32)
            if bias is not None:
                s = s + bias(h)
            ps_ref[h] = jnp.exp2(s).astype(_MXU_DTYPE)
        return
    half = tq // 2
    rows = lax.broadcasted_iota(jnp.int32, (tq, half), 0)
    cols = lax.broadcasted_iota(jnp.int32, (tq, half), 1)
    for h in range(len(q_augs)):
        b_h = None if bias is None else bias(h)
        s = lax.dot_general(q_augs[h], kaug_sc[h, pl.ds(r, half), :], _NT,
                            preferred_element_type=_F32)
        if b_h is not None:
            s = s + b_h[:, 0:half]
        ps_ref[h, :, 0:half] = jnp.exp2(jnp.where(rows >= cols, s, NEG_INF)).astype(_MXU_DTYPE)
        s = lax.dot_general(q_augs[h][half:, :], kaug_sc[h, pl.ds(r + half, half), :], _NT,
                            preferred_element_type=_F32)
        if b_h is not None:
            s = s + b_h[half:, half:]
        ps_ref[h, 0:half, half:] = jnp.zeros((half, half), _MXU_DTYPE)
        ps_ref[h, half:, half:] = jnp.exp2(jnp.where(_causal(half), s, NEG_INF)).astype(_MXU_DTYPE)


def _bounded_values(ps_ref, vext_sc, r, acc_sc):
    tq = ps_ref.shape[1]
    for p in range(acc_sc.shape[0]):
        acc_sc[p] += (
            jnp.dot(ps_ref[2 * p], vext_sc[2 * p, pl.ds(r, tq), :], preferred_element_type=_F32)
            + jnp.dot(ps_ref[2 * p + 1], vext_sc[2 * p + 1, pl.ds(r, tq), :],
                      preferred_element_type=_F32))


def _bounded_store(acc_sc, g_ref, o_ref):
    tq = acc_sc.shape[1]
    for p in range(acc_sc.shape[0]):
        acc = acc_sc[p]
        den = acc[:, LANES:]
        l = jnp.where(_own_half(0, (tq, LANES)), _lane_column(den, 0), _lane_column(den, 1))
        cols = slice(p * LANES, (p + 1) * LANES)
        o_ref[0, :, cols] = (acc[:, :LANES] / l * _silu(g_ref[0, :, cols])).astype(o_ref.dtype)


def _bounded_pipeline_step(n, ps_bufs, weights, values):
    def new_in(slot):
        def run():
            weights(ps_bufs[slot])
            values(ps_bufs[1 - slot])
        return run
    lax.cond(n % 2 == 0, new_in(0), new_in(1))


def _bounded_finish(n_done, ps_bufs, values):
    lax.cond(n_done % 2 == 0, lambda: values(ps_bufs[0]), lambda: values(ps_bufs[1]))


def _fox_extra_base(h):
    p, hh = divmod(h, 2)
    return (HEAD_DIM if hh == 0 else 0) + 8 * p


def _fox_extra_lanes(fields):
    rows = max(f.shape[1] for fs in fields for f in fs if hasattr(f, "shape"))
    sub = lax.broadcasted_iota(jnp.int32, (8, rows), 0)
    blocks = {}
    for h, fs in enumerate(fields):
        blk = jnp.zeros((8, rows), _F32)
        for n, f in enumerate(fs):
            blk = jnp.where(sub == n, f, blk)
        blocks[_fox_extra_base(h)] = blk
    pieces, at = [], 0
    for base in sorted(blocks):
        if base > at:
            pieces.append(jnp.zeros((base - at, rows), _F32))
        pieces.append(blocks[base])
        at = base + 8
    pieces.append(jnp.zeros((LANES - at, rows), _F32))
    return jnp.concatenate(pieces, axis=0).T


def _fox_bounded_kernel(bound_ref, q_ref, k_ref, v_ref, ff_ref, g_ref, bf_ref, o_ref,
                        kaug_sc, vext_sc, c_sc, ps_a, ps_b, acc_sc):
    i = pl.program_id(1)
    tq = q_ref.shape[1]
    n_heads = kaug_sc.shape[0]

    @pl.when(i == 0)
    def _prep():
        _forget_cumsum(ff_ref, bf_ref, c_sc)
        hi, mid, lo = _split3(-(c_sc[...] * LOG2E).T[0:8, :])
        extra = _fox_extra_lanes([[hi[h:h + 1], mid[h:h + 1], lo[h:h + 1], 1.0, 1.0, 1.0, 1.0, 1.0]
                                  for h in range(n_heads)])
        lane = _lane_iota(extra.shape)
        for h in range(n_heads):
            p, hh = divmod(h, 2)
            kn = k_ref[0, :, p * LANES:(p + 1) * LANES]
            v = v_ref[0, :, p * LANES:(p + 1) * LANES]
            a0 = _fox_extra_base(h)
            aug = jnp.where((lane >= a0) & (lane < a0 + 8), extra, 0.0)
            own = _own_half(hh, kn.shape)
            kaug_sc[h] = jnp.where(own, kn, aug).astype(_MXU_DTYPE)
            vext_sc[h, :, 0:LANES] = jnp.where(own, v, 0.0).astype(_MXU_DTYPE)
            _write_ones_column(vext_sc, h, hh)

    row0 = pl.multiple_of(i * tq, tq)
    off = jnp.full((1, 1), -LOG2E, _F32) * bound_ref[0]
    off_hi = off.astype(_MXU_DTYPE).astype(_F32)
    hi, mid, lo = _split3((c_sc[pl.ds(row0, tq), :] * LOG2E).T[0:8, :])
    extra = _fox_extra_lanes([[1.0, 1.0, 1.0, hi[h:h + 1], mid[h:h + 1], lo[h:h + 1],
                               off_hi, off - off_hi] for h in range(n_heads)])
    lane = _lane_iota(extra.shape)
    q_augs = []
    for h in range(n_heads):
        p, hh = divmod(h, 2)
        q = q_ref[0, :, p * LANES:(p + 1) * LANES] * (ATTN_SCALE * LOG2E)
        a0 = _fox_extra_base(h)
        aug = jnp.where((lane >= a0) & (lane < a0 + 8), extra, 0.0)
        q_augs.append(jnp.where(_own_half(hh, q.shape), q, aug).astype(_MXU_DTYPE))

    head_lanes = _lane_iota((1, LANES)) < n_heads
    c_first = c_sc[pl.ds(row0, 1), :]
    cutoff = EXP_ZERO_CUTOFF - 2.0 * bound_ref[0]

    def live(j):
        c_last = c_sc[pl.ds(jnp.maximum(j, 0) * tq + (tq - 1), 1), :]
        gap = jnp.max(jnp.where(head_lanes, c_first - c_last, -jnp.inf))
        return (j >= 0) & (gap >= cutoff)

    ps_bufs = (ps_a, ps_b)

    def values_at(r):
        return lambda ps_ref: _bounded_values(ps_ref, vext_sc, pl.multiple_of(r, tq), acc_sc)

    def body(state):
        j, _, r_prev, n = state
        r = pl.multiple_of(j * tq, tq)
        _bounded_pipeline_step(
            n, ps_bufs,
            lambda ps_ref: _bounded_weights(ps_ref, q_augs, kaug_sc, r, None, False),
            values_at(r_prev))
        return j - 1, live(j - 1), r, n + 1

    acc_sc[...] = jnp.zeros(acc_sc.shape, _F32)
    _bounded_weights(ps_a, q_augs, kaug_sc, row0, None, True)
    state = lax.while_loop(lambda state: state[1], body,
                           (i - 1, live(i - 1), row0, jnp.int32(1)))
    _bounded_finish(state[3] - 1, ps_bufs, values_at(state[2]))
    _bounded_store(acc_sc, g_ref, o_ref)


def _fox_bounded(proj, bf_row, bound, *, tq=512):
    b, t, _ = proj.shape
    n_heads = GROUP_WIDTH // HEAD_DIM
    wide = GROUP_WIDTH // LANES
    return pl.pallas_call(
        _fox_bounded_kernel,
        grid=(b, t // tq),
        in_specs=[pl.BlockSpec(memory_space=pltpu.SMEM),
                  pl.BlockSpec((1, tq, GROUP_WIDTH), lambda bb, i: (bb, i, COL_FQ // wide)),
                  pl.BlockSpec((1, t, GROUP_WIDTH), lambda bb, i: (bb, 0, COL_FK // wide)),
                  pl.BlockSpec((1, t, GROUP_WIDTH), lambda bb, i: (bb, 0, COL_FV // wide)),
                  pl.BlockSpec((1, t, LANES), lambda bb, i: (bb, 0, COL_FF)),
                  pl.BlockSpec((1, tq, GROUP_WIDTH), lambda bb, i: (bb, i, COL_FG // wide)),
                  pl.BlockSpec((1, LANES), lambda bb, i: (0, 0))],
        out_specs=pl.BlockSpec((1, tq, GROUP_WIDTH), lambda bb, i: (bb, i, 0)),
        out_shape=jax.ShapeDtypeStruct((b, t, GROUP_WIDTH), _MXU_DTYPE),
        scratch_shapes=[pltpu.VMEM((n_heads, t, LANES), _MXU_DTYPE),
                        pltpu.VMEM((n_heads, t, 2 * LANES), _MXU_DTYPE),
                        pltpu.VMEM((t, LANES), _F32),
                        pltpu.VMEM((n_heads, tq, tq), _MXU_DTYPE),
                        pltpu.VMEM((n_heads, tq, tq), _MXU_DTYPE),
                        pltpu.VMEM((n_heads // 2, tq, 2 * LANES), _F32)],
        compiler_params=pltpu.CompilerParams(
            dimension_semantics=("arbitrary", "arbitrary"),
            vmem_limit_bytes=_VMEM_LIMIT_WIDE),
        name="fox_bounded",
    )(bound, proj, proj, proj, proj, proj, bf_row)


def _fox_kernel(q_ref, k_ref, v_ref, ff_ref, g_ref, bf_ref, o_ref,
                kaug_sc, v_sc, c_sc, *, tq):
    p = pl.program_id(1)
    i = pl.program_id(2)

    def head_column(x, hh):
        return _lane_column(x, 2 * p + hh)

    @pl.when(i == 0)
    def _prep():
        pl.when(p == 0)(lambda: _forget_cumsum(ff_ref, bf_ref, c_sc))
        kn = k_ref[0]
        c_all = c_sc[...]
        lane = _lane_iota(kn.shape)
        for hh in range(2):
            a0 = HEAD_DIM if hh == 0 else 0
            hi, mid, lo = _split3(-head_column(c_all, hh))
            aug = jnp.where(lane == a0, hi,
                            jnp.where(lane == a0 + 1, mid,
                                      jnp.where(lane == a0 + 2, lo,
                                                jnp.where((lane >= a0 + 3) & (lane < a0 + 6), 1.0, 0.0))))
            kaug_sc[hh] = jnp.where(_own_half(hh, kn.shape), kn, aug).astype(_MXU_DTYPE)
        v_sc[...] = v_ref[0].astype(_MXU_DTYPE)

    q = q_ref[0] * ATTN_SCALE
    row0 = pl.multiple_of(i * tq, tq)
    c_t = c_sc[pl.ds(row0, tq), :]
    lane = _lane_iota(q.shape)
    outs = []
    for hh in range(2):
        a0 = HEAD_DIM if hh == 0 else 0
        hi, mid, lo = _split3(head_column(c_t, hh))
        aug = jnp.where((lane >= a0) & (lane < a0 + 3), 1.0,
                        jnp.where(lane == a0 + 3, hi,
                                  jnp.where(lane == a0 + 4, mid,
                                            jnp.where(lane == a0 + 5, lo, 0.0))))
        q_aug = jnp.where(_own_half(hh, q.shape), q, aug).astype(_MXU_DTYPE)

        s = lax.dot_general(q_aug, kaug_sc[hh, pl.ds(row0, tq), :], _NT,
                            preferred_element_type=_F32)
        s = jnp.where(_causal(tq), s, NEG_INF)
        m = jnp.max(s, axis=-1, keepdims=True)
        pr = jnp.exp(s - m)
        l = jnp.sum(pr, axis=-1, keepdims=True)
        acc = jnp.dot(pr.astype(_MXU_DTYPE), v_sc[pl.ds(row0, tq), :], preferred_element_type=_F32)

        def body(j, carry, q_aug=q_aug, hh=hh):
            r = pl.multiple_of(j * tq, tq)
            s = lax.dot_general(q_aug, kaug_sc[hh, pl.ds(r, tq), :], _NT,
                                preferred_element_type=_F32)
            return _flash_update(s, v_sc[pl.ds(r, tq), :], *carry)

        m, l, acc = lax.fori_loop(0, i, body, (m, l, acc))
        outs.append(acc / l)
    o = jnp.where(_own_half(0, outs[0].shape), outs[0], outs[1])
    o_ref[0] = (o * _silu(g_ref[0])).astype(o_ref.dtype)


def _fox(proj, bf_row, *, tq=256):
    b, t, _ = proj.shape
    return pl.pallas_call(
        functools.partial(_fox_kernel, tq=tq),
        grid=(b, 2, t // tq),
        in_specs=[pl.BlockSpec((1, tq, LANES), lambda bb, p, i: (bb, i, COL_FQ + p)),
                  pl.BlockSpec((1, t, LANES), lambda bb, p, i: (bb, 0, COL_FK + p)),
                  pl.BlockSpec((1, t, LANES), lambda bb, p, i: (bb, 0, COL_FV + p)),
                  pl.BlockSpec((1, t, LANES), lambda bb, p, i: (bb, 0, COL_FF)),
                  pl.BlockSpec((1, tq, LANES), lambda bb, p, i: (bb, i, COL_FG + p)),
                  pl.BlockSpec((1, LANES), lambda bb, p, i: (0, 0))],
        out_specs=pl.BlockSpec((1, tq, LANES), lambda bb, p, i: (bb, i, p)),
        out_shape=jax.ShapeDtypeStruct((b, t, GROUP_WIDTH), _MXU_DTYPE),
        scratch_shapes=[pltpu.VMEM((2, t, LANES), _MXU_DTYPE),
                        pltpu.VMEM((t, LANES), _MXU_DTYPE),
                        pltpu.VMEM((t, LANES), _F32)],
        compiler_params=pltpu.CompilerParams(
            dimension_semantics=("arbitrary", "arbitrary", "arbitrary"),
            vmem_limit_bytes=_VMEM_LIMIT),
        name="fox",
    )(proj, proj, proj, proj, proj, bf_row)


def _moba_block_means(kn, kmean_sc):
    kmean_sc[...] = jnp.zeros(kmean_sc.shape, _F32)
    for n in range(kn.shape[0] // MOBA_BLOCK):
        kmean_sc[n:n + 1, :] = jnp.mean(kn[n * MOBA_BLOCK:(n + 1) * MOBA_BLOCK], axis=0, keepdims=True)


def _moba_select(q_head, kmean, past):
    gate = lax.dot_general(q_head, kmean, _NT, precision=lax.Precision.HIGHEST,
                           preferred_element_type=_F32)
    lane_f = _lane_iota(gate.shape).astype(_F32)
    cand = jnp.where(past, gate, -jnp.inf)
    sel = jnp.zeros(gate.shape, _F32)
    for _ in range(MOBA_TOPK):
        mx = jnp.max(cand, axis=-1, keepdims=True)
        is_max = (cand == mx) & (mx > -jnp.inf)
        first = jnp.min(jnp.where(is_max, lane_f, float(LANES)), axis=-1, keepdims=True)
        pick = lane_f == first
        sel = jnp.where(pick, 1.0, sel)
        cand = jnp.where(pick, -jnp.inf, cand)
    return sel


def _moba_bounded_kernel(rb_ref, bound_ref, q_ref, k_ref, v_ref, g_ref, bias_ref, o_ref,
                         kaug_sc, vext_sc, kmean_sc, ps_a, ps_b, acc_sc):
    i = pl.program_id(1)
    tq = q_ref.shape[1]
    n_heads = kaug_sc.shape[0]
    blk = MOBA_BLOCK
    near = MOBA_WIDE_NEAR_TILES

    @pl.when(i == 0)
    def _prep():
        lane = _lane_iota((k_ref.shape[1], LANES))
        row_blk = lax.broadcasted_iota(jnp.int32, lane.shape, 0) // blk
        for h in range(n_heads):
            p, hh = divmod(h, 2)
            kn = k_ref[0, :, p * LANES:(p + 1) * LANES]
            v = v_ref[0, :, p * LANES:(p + 1) * LANES]
            if hh == 0:
                _moba_block_means(kn, kmean_sc.at[p])
            a0 = HEAD_DIM if hh == 0 else 0
            aug = jnp.where((lane - a0 == row_blk) | (lane - a0 - 16 == row_blk)
                            | (lane - a0 == 32) | (lane - a0 == 33), 1.0, 0.0)
            own = _own_half(hh, kn.shape)
            kaug_sc[h] = jnp.where(own, kn, aug).astype(_MXU_DTYPE)
            vext_sc[h, :, 0:LANES] = jnp.where(own, v, 0.0).astype(_MXU_DTYPE)
            _write_ones_column(vext_sc, h, hh)

    row0 = pl.multiple_of(i * tq, tq)
    off = jnp.full((1, 1), -LOG2E, _F32) * bound_ref[0]
    off_hi = off.astype(_MXU_DTYPE).astype(_F32)
    blk_n = lax.broadcasted_iota(jnp.int32, (16, tq), 0)
    blk_f = blk_n.astype(_F32)
    q_blk = i * (tq // blk) + lax.broadcasted_iota(jnp.int32, (16, tq), 1) // blk
    past = blk_n < q_blk
    far = i - blk_n // 2 >= near
    sub8 = lax.broadcasted_iota(jnp.int32, (8, tq), 0)
    off_rows = jnp.where(sub8 == 0, off_hi, jnp.where(sub8 == 1, off - off_hi, 0.0))
    q_augs = []
    for h in range(n_heads):
        p, hh = divmod(h, 2)
        qn = q_ref[0, :, p * LANES:(p + 1) * LANES]
        own = _own_half(hh, qn.shape)
        gate = lax.dot_general(kmean_sc[p, 0:16, :], jnp.where(own, qn, 0.0), _NT,
                               precision=lax.Precision.HIGHEST, preferred_element_type=_F32)
        cand = jnp.where(past, gate, -jnp.inf)
        sel = jnp.zeros((16, tq), _F32)
        for _ in range(MOBA_TOPK):
            mx = jnp.max(cand, axis=0, keepdims=True)
            is_max = (cand == mx) & (mx > -jnp.inf)
            first = jnp.min(jnp.where(is_max, blk_f, float(LANES)), axis=0, keepdims=True)
            pick = blk_f == first
            sel = jnp.where(pick, 1.0, sel)
            cand = jnp.where(pick, -jnp.inf, cand)
        c_far = jnp.full((1, 1), LOG2E, _F32) * rb_ref[NUM_BUCKETS - 1, h]
        far_hi = c_far.astype(_MXU_DTYPE).astype(_F32)
        pen = jnp.where((blk_n == q_blk) | (past & (sel != 0.0)), 0.0, NEG_INF)
        fields = jnp.concatenate([pen + jnp.where(far, far_hi, 0.0),
                                  jnp.where(far, c_far - far_hi, 0.0),
                                  off_rows,
                                  jnp.zeros((HEAD_DIM - 40, tq), _F32)], axis=0)
        blank = jnp.zeros((HEAD_DIM, tq), _F32)
        aug = jnp.concatenate([blank, fields] if hh == 0 else [fields, blank], axis=0).T
        q_augs.append(jnp.where(own, qn * (ATTN_SCALE * LOG2E), aug).astype(_MXU_DTYPE))

    ps_bufs = (ps_a, ps_b)

    def values_at(r):
        return lambda ps_ref: _bounded_values(ps_ref, vext_sc, pl.multiple_of(r, tq), acc_sc)

    def body(j, state, with_bias):
        r_prev, n = state
        r = pl.multiple_of(j * tq, tq)
        bias = (lambda h: bias_ref[h, i - j]) if with_bias else None
        _bounded_pipeline_step(
            n, ps_bufs,
            lambda ps_ref: _bounded_weights(ps_ref, q_augs, kaug_sc, r, bias, False),
            values_at(r_prev))
        return r, n + 1

    acc_sc[...] = jnp.zeros(acc_sc.shape, _F32)
    _bounded_weights(ps_a, q_augs, kaug_sc, row0, lambda h: bias_ref[h, 0], True)
    state = (row0, jnp.int32(1))
    near_lo = jnp.maximum(i - (near - 1), 0)
    state = lax.fori_loop(near_lo, i, functools.partial(body, with_bias=True), state)
    state = lax.fori_loop(0, near_lo, functools.partial(body, with_bias=False), state)
    _bounded_finish(state[1] - 1, ps_bufs, values_at(state[0]))
    _bounded_store(acc_sc, g_ref, o_ref)


def _moba_bounded(proj, rel_bias, bias_tiles, bound):
    b, t, _ = proj.shape
    tq = MOBA_WIDE
    assert t % tq == 0 and t // MOBA_BLOCK <= 16
    n_heads = GROUP_WIDTH // HEAD_DIM
    wide = GROUP_WIDTH // LANES
    resident = pl.Buffered(1)
    return pl.pallas_call(
        _moba_bounded_kernel,
        grid=(b, t // tq),
        in_specs=[pl.BlockSpec(memory_space=pltpu.SMEM),
                  pl.BlockSpec(memory_space=pltpu.SMEM),
                  pl.BlockSpec((1, tq, GROUP_WIDTH), lambda bb, i: (bb, i, COL_MQ // wide)),
                  pl.BlockSpec((1, t, GROUP_WIDTH), lambda bb, i: (bb, 0, COL_MK // wide),
                               pipeline_mode=resident),
                  pl.BlockSpec((1, t, GROUP_WIDTH), lambda bb, i: (bb, 0, COL_MV // wide),
                               pipeline_mode=resident),
                  pl.BlockSpec((1, tq, GROUP_WIDTH), lambda bb, i: (bb, i, COL_MG // wide)),
                  pl.BlockSpec((n_heads, MOBA_WIDE_NEAR_TILES, tq, tq), lambda bb, i: (0, 0, 0, 0),
                               pipeline_mode=resident)],
        out_specs=pl.BlockSpec((1, tq, GROUP_WIDTH), lambda bb, i: (bb, i, 0)),
        out_shape=jax.ShapeDtypeStruct((b, t, GROUP_WIDTH), _MXU_DTYPE),
        scratch_shapes=[pltpu.VMEM((n_heads, t, LANES), _MXU_DTYPE),
                        pltpu.VMEM((n_heads, t, 2 * LANES), _MXU_DTYPE),
                        pltpu.VMEM((n_heads // 2, LANES, LANES), _F32),
                        pltpu.VMEM((n_heads, tq, tq), _MXU_DTYPE),
                        pltpu.VMEM((n_heads, tq, tq), _MXU_DTYPE),
                        pltpu.VMEM((n_heads // 2, tq, 2 * LANES), _F32)],
        compiler_params=pltpu.CompilerParams(
            dimension_semantics=("arbitrary", "arbitrary"),
            vmem_limit_bytes=_VMEM_LIMIT_WIDE),
        name="moba_bounded",
    )(rel_bias, bound, proj, proj, proj, proj, bias_tiles)


def _moba_kernel(rb_ref, q_ref, k_ref, v_ref, g_ref, bias_ref, o_ref,
                 kaug_sc, v_sc, kmean_sc):
    p = pl.program_id(1)
    i = pl.program_id(2)
    t_len = k_ref.shape[1]
    blk = MOBA_BLOCK
    nblk = t_len // blk
    near = MOBA_NEAR_TILES

    @pl.when(i == 0)
    def _prep():
        kn = k_ref[0]
        _moba_block_means(kn, kmean_sc)
        lane = _lane_iota(kn.shape)
        row_blk = lax.broadcasted_iota(jnp.int32, kn.shape, 0) // blk
        for hh in range(2):
            a0 = HEAD_DIM if hh == 0 else 0
            onehot = jnp.where((lane - a0 == row_blk) | (lane - a0 - 16 == row_blk), 1.0, 0.0)
            kaug_sc[hh] = jnp.where(_own_half(hh, kn.shape), kn, onehot).astype(_MXU_DTYPE)
        v_sc[...] = v_ref[0].astype(_MXU_DTYPE)

    qn = q_ref[0]
    row0 = pl.multiple_of(i * blk, blk)
    lane = _lane_iota(qn.shape)
    lane_f = lane.astype(_F32)
    past = lane < i
    outs = []
    for hh in range(2):
        own = _own_half(hh, qn.shape)
        sel = _moba_select(jnp.where(own, qn, 0.0), kmean_sc[...], past)
        c_far = jnp.full((1, LANES), rb_ref[NUM_BUCKETS - 1, 2 * p + hh], _F32)
        far_hi = c_far.astype(_MXU_DTYPE).astype(_F32)
        far_lo = c_far - far_hi
        pen = jnp.where(past & (sel == 0.0), NEG_INF, 0.0)
        aug = jnp.where(lane < 16, pen + jnp.where(i - lane >= near, far_hi, 0.0),
                        jnp.where((lane < 32) & (i - (lane - 16) >= near), far_lo, 0.0))
        if hh == 0:
            aug = pltpu.roll(aug, HEAD_DIM, axis=1)
        q_aug = jnp.where(own, qn * ATTN_SCALE, aug).astype(_MXU_DTYPE)

        def scores(r, q_aug=q_aug, hh=hh):
            return lax.dot_general(q_aug, kaug_sc[hh, pl.ds(r, blk), :], _NT,
                                   preferred_element_type=_F32)

        s = scores(row0) + bias_ref[hh, 0]
        s = jnp.where(_causal(blk), s, NEG_INF)
        m = jnp.max(s, axis=-1, keepdims=True)
        pr = jnp.exp(s - m)
        l = jnp.sum(pr, axis=-1, keepdims=True)
        acc = jnp.dot(pr.astype(_MXU_DTYPE), v_sc[pl.ds(row0, blk), :], preferred_element_type=_F32)

        def near_body(j, carry, hh=hh, scores=scores):
            r = pl.multiple_of(j * blk, blk)
            return _flash_update(scores(r) + bias_ref[hh, i - j], v_sc[pl.ds(r, blk), :], *carry)

        def far_body(j, carry, scores=scores):
            r = pl.multiple_of(j * blk, blk)
            return _flash_update(scores(r), v_sc[pl.ds(r, blk), :], *carry)

        near_lo = jnp.maximum(i - (near - 1), 0)
        carry = lax.fori_loop(near_lo, i, near_body, (m, l, acc))
        m, l, acc = lax.fori_loop(0, near_lo, far_body, carry)
        outs.append(acc / l)
    o = jnp.where(_own_half(0, outs[0].shape), outs[0], outs[1])
    o_ref[0] = (o * _silu(g_ref[0])).astype(o_ref.dtype)


def _moba(proj, rel_bias, bias_tiles):
    b, t, _ = proj.shape
    blk = MOBA_BLOCK
    assert t % blk == 0 and t // blk <= 16
    return pl.pallas_call(
        _moba_kernel,
        grid=(b, 2, t // blk),
        in_specs=[pl.BlockSpec(memory_space=pltpu.SMEM),
                  pl.BlockSpec((1, blk, LANES), lambda bb, p, i: (bb, i, COL_MQ + p)),
                  pl.BlockSpec((1, t, LANES), lambda bb, p, i: (bb, 0, COL_MK + p)),
                  pl.BlockSpec((1, t, LANES), lambda bb, p, i: (bb, 0, COL_MV + p)),
                  pl.BlockSpec((1, blk, LANES), lambda bb, p, i: (bb, i, COL_MG + p)),
                  pl.BlockSpec((2, MOBA_NEAR_TILES, blk, blk), lambda bb, p, i: (p, 0, 0, 0))],
        out_specs=pl.BlockSpec((1, blk, LANES), lambda bb, p, i: (bb, i, p)),
        out_shape=jax.ShapeDtypeStruct((b, t, GROUP_WIDTH), _MXU_DTYPE),
        scratch_shapes=[pltpu.VMEM((2, t, LANES), _MXU_DTYPE),
                        pltpu.VMEM((t, LANES), _MXU_DTYPE),
                        pltpu.VMEM((LANES, LANES), _F32)],
        compiler_params=pltpu.CompilerParams(
            dimension_semantics=("arbitrary", "arbitrary", "arbitrary"),
            vmem_limit_bytes=_VMEM_LIMIT),
        name="moba",
    )(rel_bias, proj, proj, proj, proj, bias_tiles)


def _sb_kernel(q_ref, k_ref, v_ref, g_ref, o_ref, k_sc, v_sc, *, tq):
    i = pl.program_id(2)

    @pl.when(i == 0)
    def _prep():
        k_sc[...] = k_ref[0].astype(_MXU_DTYPE)
        v = v_ref[0]
        for hh in range(2):
            v_sc[hh] = jnp.where(_own_half(hh, v.shape), v, 0.0).astype(_MXU_DTYPE)

    subs = q_ref.shape[1] // tq
    q_heads = []
    for u in range(subs):
        q = q_ref[0, u * tq:(u + 1) * tq, :] * (ATTN_SCALE * LOG2E)
        q_heads.append([jnp.where(_own_half(hh, q.shape), q, 0.0).astype(_MXU_DTYPE)
                        for hh in range(2)])
    strict =(lax.broadcasted_iota(jnp.int32, (tq, tq), 0)
              > lax.broadcasted_iota(jnp.int32, (tq, tq), 1))
    after = strict.astype(_MXU_DTYPE)
    sign_bit = jnp.uint32(0x80000000)

    def pair(back, runs, diagonal):
        chains = [(u, t, hh) for u in range(subs) for t in range(2) for hh in range(2)]
        tile_idx = [[subs * i + u - back - t for t in range(2)] for u in range(subs)]
        valid = [[(j >= 0).astype(_F32) for j in js] for js in tile_idx]
        rows = [[pl.multiple_of(jnp.maximum(j, 0) * tq, tq) for j in js] for js in tile_idx]
        masked = [diagonal and t == 0 for _, t, _ in chains]
        zs = [lax.dot_general(q_heads[u][hh], k_sc[pl.ds(rows[u][t], tq), :], _NT,
                              preferred_element_type=_F32) for u, t, hh in chains]
        drops = []
        for z, msk in zip(zs, masked):
            neg_abs = lax.bitcast_convert_type(lax.bitcast_convert_type(z, jnp.uint32) | sign_bit, _F32)
            drop = jnp.maximum(z, 0.0) + jnp.log(1.0 + jnp.exp2(neg_abs)) * LOG2E
            drops.append(jnp.where(strict, drop, 0.0) if msk else drop)
        laters = []
        for drop in drops:
            hi = drop.astype(_MXU_DTYPE)
            lo = (drop - hi.astype(_F32)).astype(_MXU_DTYPE)
            laters.append(jnp.dot(hi, after, preferred_element_type=_F32)
                          + jnp.dot(lo, after, preferred_element_type=_F32))
        pvs = []
        for (u, t, hh), z, drop, later, msk in zip(chains, zs, drops, laters, masked):
            w = jnp.exp2(z - drop - later)
            if msk:
                w = jnp.where(strict, w, 0.0)
            pvs.append(jnp.dot(w.astype(_MXU_DTYPE), v_sc[hh, pl.ds(rows[u][t], tq), :],
                               preferred_element_type=_F32))
        sums = [jnp.sum(drop, axis=-1, keepdims=True) for drop in drops]
        new_runs, outs = [], []
        for u in range(subs):
            new_runs.append([])
            out = None
            for hh in range(2):
                run = runs[u][hh]
                for t in range(2):
                    c = chains.index((u, t, hh))
                    pv = pvs[c] * (jnp.exp2(run) * valid[u][t])
                    out = pv if out is None else out + pv
                    run = run - sums[c] * valid[u][t]
                new_runs[u].append(run)
            outs.append(out)
        return new_runs, outs

    def alive(runs):
        top = functools.reduce(jnp.maximum, [r for rs in runs for r in rs])
        return jnp.max(top) >= EXP_ZERO_CUTOFF * LOG2E

    def flat(runs):
        return [r for rs in runs for r in rs]

    def nested(flat_runs):
        return [flat_runs[2 * u:2 * u + 2] for u in range(subs)]

    zero = jnp.zeros((tq, 1), _F32)
    runs, accs = pair(0, [[zero, zero] for _ in range(subs)], True)

    def body(state):
        back = state[0]
        runs, outs = pair(back, nested(list(state[2:2 + 2 * subs])), False)
        accs = [a + o for a, o in zip(state[2 + 2 * subs:], outs)]
        return (back + 2, alive(runs), *flat(runs), *accs)

    state = lax.while_loop(lambda state: (subs * i + subs - 1 - state[0] >= 0) & state[1], body,
                           (jnp.int32(2), alive(runs), *flat(runs), *accs))
    for u in range(subs):
        o_ref[0, u * tq:(u + 1) * tq, :] = (
            state[2 + 2 * subs + u] * _silu(g_ref[0, u * tq:(u + 1) * tq, :])).astype(o_ref.dtype)


def _sb(proj, *, tq=256, subs=2):
    b, t, _ = proj.shape
    rows = tq * subs
    return pl.pallas_call(
        functools.partial(_sb_kernel, tq=tq),
        grid=(b, 2, t // rows),
        in_specs=[pl.BlockSpec((1, rows, LANES), lambda bb, p, i: (bb, i, COL_SQ + p)),
                  pl.BlockSpec((1, t, LANES), lambda bb, p, i: (bb, 0, COL_SK + p)),
                  pl.BlockSpec((1, t, LANES), lambda bb, p, i: (bb, 0, COL_SV + p)),
                  pl.BlockSpec((1, rows, LANES), lambda bb, p, i: (bb, i, COL_SG + p))],
        out_specs=pl.BlockSpec((1, rows, LANES), lambda bb, p, i: (bb, i, p)),
        out_shape=jax.ShapeDtypeStruct((b, t, GROUP_WIDTH), _MXU_DTYPE),
        scratch_shapes=[pltpu.VMEM((t, LANES), _MXU_DTYPE),
                        pltpu.VMEM((2, t, LANES), _MXU_DTYPE)],
        compiler_params=pltpu.CompilerParams(
            dimension_semantics=("arbitrary", "arbitrary", "arbitrary"),
            vmem_limit_bytes=_VMEM_LIMIT),
        name="stickbreak",
    )(proj, proj, proj, proj)


def _swa_kernel(sink_ref, q_ref, k_ref, v_ref, g_ref, bias_ref, o_ref,
                k_sc, v_sc, *, tq):
    i = pl.program_id(1)
    t_len = k_ref.shape[1]
    w = SWA_WINDOW
    n_kv = k_sc.shape[0]

    @pl.when(i == 0)
    def _prep():
        kn = k_ref[0]
        v = v_ref[0]
        kn_swapped = pltpu.roll(kn, HEAD_DIM, axis=1)
        v_swapped = pltpu.roll(v, HEAD_DIM, axis=1)
        for kv in range(n_kv):
            keep = _own_half(kv, kn.shape)
            k_sc[kv, 0:w, :] = jnp.zeros((w, LANES), _MXU_DTYPE)
            v_sc[kv, 0:w, :] = jnp.zeros((w, LANES), _MXU_DTYPE)
            k_sc[kv, w:w + t_len, :] = jnp.where(keep, kn, kn_swapped).astype(_MXU_DTYPE)
            v_sc[kv, w:w + t_len, :] = jnp.where(keep, v, v_swapped).astype(_MXU_DTYPE)

    qi =lax.broadcasted_iota(jnp.int32, (w, 2 * w), 0)
    kj = lax.broadcasted_iota(jnp.int32, (w, 2 * w), 1)
    dist = qi + w - kj
    in_window = (dist >= 0) & (dist < w)
    n_sub = tq // w
    chains = [(u, kv, g) for u in range(n_sub) for kv in range(n_kv) for g in range(2)]
    rows = [pl.multiple_of((i * n_sub + u) * w, w) for u in range(n_sub)]
    ss = []
    for u, kv, g in chains:
        q_u = q_ref[0, u * w:(u + 1) * w, kv * LANES:(kv + 1) * LANES] * ATTN_SCALE
        q_h = jnp.where(_own_half(g, q_u.shape), q_u, 0.0).astype(_MXU_DTYPE)
        ss.append(lax.dot_general(q_h, k_sc[kv, pl.ds(rows[u], 2 * w), :], _NT,
                                  preferred_element_type=_F32))
    es, dens = [], []
    for (u, kv, g), s in zip(chains, ss):
        allowed = in_window & (kj + (i * n_sub + u - 1) * w >= 0)
        s = jnp.where(allowed, s + bias_ref[2 * kv + g], NEG_INF)
        sink = sink_ref[2 * kv + g]
        m = jnp.maximum(jnp.max(s, axis=-1, keepdims=True), sink)
        e = jnp.exp(s - m)
        dens.append(jnp.sum(e, axis=-1, keepdims=True) + jnp.exp(sink - m))
        es.append(e.astype(_MXU_DTYPE))
    outs = [jnp.dot(e, v_sc[kv, pl.ds(rows[u], 2 * w), :], preferred_element_type=_F32) / den
            for (u, kv, g), e, den in zip(chains, es, dens)]
    for u in range(n_sub):
        for kv in range(n_kv):
            c = chains.index((u, kv, 0))
            o = jnp.where(_own_half(0, (w, LANES)), outs[c], outs[c + 1])
            sl = (0, slice(u * w, (u + 1) * w), slice(kv * LANES, (kv + 1) * LANES))
            o_ref[sl] = (o * _silu(g_ref[sl])).astype(o_ref.dtype)


def _swa(proj, sinks, bias_tiles, *, tq=512):
    b, t, _ = proj.shape
    w = SWA_WINDOW
    n_heads = GROUP_WIDTH // HEAD_DIM
    wide = GROUP_WIDTH // LANES
    return pl.pallas_call(
        functools.partial(_swa_kernel, tq=tq),
        grid=(b, t // tq),
        in_specs=[pl.BlockSpec(memory_space=pltpu.SMEM),
                  pl.BlockSpec((1, tq, GROUP_WIDTH), lambda bb, i: (bb, i, COL_WQ // wide)),
                  pl.BlockSpec((1, t, LANES), lambda bb, i: (bb, 0, COL_WK)),
                  pl.BlockSpec((1, t, LANES), lambda bb, i: (bb, 0, COL_WV)),
                  pl.BlockSpec((1, tq, GROUP_WIDTH), lambda bb, i: (bb, i, COL_WG // wide)),
                  pl.BlockSpec((n_heads, w, 2 * w), lambda bb, i: (0, 0, 0))],
        out_specs=pl.BlockSpec((1, tq, GROUP_WIDTH), lambda bb, i: (bb, i, 0)),
        out_shape=jax.ShapeDtypeStruct((b, t, GROUP_WIDTH), _MXU_DTYPE),
        scratch_shapes=[pltpu.VMEM((n_heads // 2, t + w, LANES), _MXU_DTYPE),
                        pltpu.VMEM((n_heads // 2, t + w, LANES), _MXU_DTYPE)],
        compiler_params=pltpu.CompilerParams(
            dimension_semantics=("arbitrary", "arbitrary"),
            vmem_limit_bytes=_VMEM_LIMIT),
        name="swa",
    )(sinks, proj, proj, proj, proj, bias_tiles)


def _outproj_kernel(x_ref, ya_ref, yb_ref, yc_ref, yd_ref, w_ref, o_ref):
    acc = x_ref[...]
    for g, y_ref in enumerate((ya_ref, yb_ref, yc_ref, yd_ref)):
        acc = acc + jnp.dot(y_ref[...], w_ref[g * GROUP_WIDTH:(g + 1) * GROUP_WIDTH, :],
                            preferred_element_type=_F32)
    o_ref[...] = acc


def _outproj(x2d, ys, w_all, layer):
    m, d = x2d.shape
    tm = 512
    y_spec = pl.BlockSpec((tm, GROUP_WIDTH), lambda i: (i, 0))
    return pl.pallas_call(
        _outproj_kernel,
        grid=(m // tm,),
        in_specs=[pl.BlockSpec((tm, d), lambda i: (i, 0)), y_spec, y_spec, y_spec, y_spec,
                  pl.BlockSpec((None,) + w_all.shape[1:], lambda i: (layer, 0, 0))],
        out_specs=pl.BlockSpec((tm, d), lambda i: (i, 0)),
        out_shape=jax.ShapeDtypeStruct((m, d), _F32),
        compiler_params=pltpu.CompilerParams(
            dimension_semantics=("arbitrary",), vmem_limit_bytes=_VMEM_LIMIT),
        name="outproj",
    )(x2d, *ys, w_all)


def _rearrange_w_in(w_in):
    off = 3 * GROUP_WIDTH
    main = jnp.concatenate([w_in[..., :off], w_in[..., off + FOX_FORGET_COLS:]], axis=-1)
    ff = w_in[..., off:off + FOX_FORGET_COLS]
    pad = PROJ_WIDTH - main.shape[-1] - FOX_FORGET_COLS
    return jnp.concatenate([main, ff, jnp.zeros(w_in.shape[:-1] + (pad,), w_in.dtype)], axis=-1)


def _qk_logit_bound(qk_gain):
    g = jnp.abs(qk_gain.astype(_F32))
    return ATTN_SCALE * HEAD_DIM * jnp.max(g[0]) * jnp.max(g[1])


def _qk_gain_row(fox_g, moba_g, swa_g):
    def tiles(g, n):
        return jnp.tile(g.astype(_F32), 2 * n)
    row = jnp.zeros((PROJ_WIDTH,), _F32)
    for col, g, n in ((COL_FQ, fox_g[0], 2), (COL_FK, fox_g[1], 2),
                      (COL_MQ, moba_g[0], 2), (COL_MK, moba_g[1], 2),
                      (COL_WQ, swa_g[0], 2), (COL_WK, swa_g[1], 1)):
        row = lax.dynamic_update_slice(row, tiles(g, n), (col * LANES,))
    return row[None, :]


def kernel(x, norm_gain, w_in, b_forget, fox_qk_gain, moba_qk_gain, swa_qk_gain, sinks, w_out, rel_bias):
    b, t, d = x.shape
    depth = w_in.shape[0]
    w_in_r = _rearrange_w_in(w_in).astype(_MXU_DTYPE)
    w_out_c = w_out.astype(_MXU_DTYPE)
    moba_bias, moba_wide_bias, swa_bias = _bias_tiles(rel_bias)
    moba_bias_max = jnp.max(jnp.abs(rel_bias[:, :rel_bias.shape[1] // 2].astype(_F32)))
    x2d = x.reshape(b * t, d)
    for layer in range(depth):
        qk_gain_row = _qk_gain_row(fox_qk_gain[layer], moba_qk_gain[layer], swa_qk_gain[layer])
        proj = _inproj(x2d, norm_gain[layer][None, :], w_in_r, layer, qk_gain_row)
        proj = proj.reshape(b, t, PROJ_WIDTH)
        bf_row = jnp.pad(b_forget[layer], (0, LANES - FOX_FORGET_COLS))[None, :]
        fox_bound = _qk_logit_bound(fox_qk_gain[layer])
        y_fox = lax.cond(2.0 * fox_bound <= BOUNDED_SOFTMAX_RANGE,
                         lambda: _fox_bounded(proj, bf_row, fox_bound.reshape(1)),
                         lambda: _fox(proj, bf_row))
        moba_bound = _qk_logit_bound(moba_qk_gain[layer]) + moba_bias_max
        y_moba = lax.cond(2.0 * moba_bound <= BOUNDED_SOFTMAX_RANGE,
                          lambda: _moba_bounded(proj, rel_bias, moba_wide_bias,
                                                moba_bound.reshape(1)),
                          lambda: _moba(proj, rel_bias, moba_bias))
        y_sb = _sb(proj)
        y_swa = _swa(proj, sinks[layer], swa_bias)
        ys = [y.reshape(b * t, GROUP_WIDTH) for y in (y_fox, y_moba, y_sb, y_swa)]
        x2d = _outproj(x2d, ys, w_out_c, layer)
    return x2d.reshape(b, t, d)
```

```python
import functools
import math

import jax
import jax.numpy as jnp
from jax import lax
from jax.experimental import pallas as pl
from jax.experimental.pallas import tpu as pltpu

HEAD_DIM = 64
LANES = 128
GROUP_WIDTH = 256
MOBA_BLOCK = 256
MOBA_TOPK = 3
SWA_WINDOW = 128
NUM_BUCKETS = 32
REL_MAX_DISTANCE = 1024
RMS_EPS = 1e-6
NEG_INF = -1e30
ATTN_SCALE = HEAD_DIM ** -0.5
LOG2E = math.log2(math.e)
BOUNDED_SOFTMAX_RANGE = 60.0
MOBA_NEAR_TILES = REL_MAX_DISTANCE // MOBA_BLOCK + 1
MOBA_WIDE = 2 * MOBA_BLOCK
MOBA_WIDE_NEAR_TILES = REL_MAX_DISTANCE // MOBA_WIDE + 1
FOX_FORGET_COLS = 4
EXP_ZERO_CUTOFF = -104.0

COL_FQ, COL_FK, COL_FV, COL_FG = 0, 2, 4, 6
COL_MQ, COL_MK, COL_MV, COL_MG = 8, 10, 12, 14
COL_SQ, COL_SK, COL_SV, COL_SG = 16, 18, 20, 22
COL_WQ, COL_WK, COL_WV, COL_WG = 24, 26, 27, 28
COL_FF = 30
PROJ_WIDTH = (COL_FF + 1) * LANES
QK_NORM_COLS = (COL_FQ, COL_FQ + 1, COL_FK, COL_FK + 1, COL_MQ, COL_MQ + 1, COL_MK, COL_MK + 1,
                COL_WQ, COL_WQ + 1, COL_WK)

_MXU_DTYPE = jnp.bfloat16
_F32 = jnp.float32
_VMEM_LIMIT = 48 * 1024 * 1024
_VMEM_LIMIT_WIDE = 56 * 1024 * 1024

_NT = (((1,), (1,)), ((), ()))


def _lane_iota(shape):
    return lax.broadcasted_iota(jnp.int32, shape, len(shape) - 1)


def _pair_rms(x, gain_row):
    low = _lane_iota(x.shape) < HEAD_DIM
    sq = x * x
    ms_lo = jnp.sum(jnp.where(low, sq, 0.0), axis=-1, keepdims=True)
    ms_hi = jnp.sum(jnp.where(low, 0.0, sq), axis=-1, keepdims=True)
    ms = jnp.where(low, ms_lo, ms_hi) * (1.0 / HEAD_DIM)
    return x * lax.rsqrt(ms + RMS_EPS) * gain_row


def _silu(g):
    return g * (1.0 / (1.0 + jnp.exp(-g)))


def _split3(x):
    hi = x.astype(_MXU_DTYPE).astype(_F32)
    r = x - hi
    mid = r.astype(_MXU_DTYPE).astype(_F32)
    return hi, mid, r - mid


def _own_half(hh, shape):
    lane = _lane_iota(shape)
    return (lane < HEAD_DIM) if hh == 0 else (lane >= HEAD_DIM)


def _flash_update(s, v_tile, m, l, acc):
    m_new = jnp.maximum(m, jnp.max(s, axis=-1, keepdims=True))
    alpha = jnp.exp(m - m_new)
    p = jnp.exp(s - m_new)
    l = alpha * l + jnp.sum(p, axis=-1, keepdims=True)
    acc = alpha * acc + jnp.dot(p.astype(_MXU_DTYPE), v_tile, preferred_element_type=_F32)
    return m_new, l, acc


def _causal(tq):
    return (lax.broadcasted_iota(jnp.int32, (tq, tq), 0)
            >= lax.broadcasted_iota(jnp.int32, (tq, tq), 1))


def _inproj_kernel(x_ref, gain_ref, w_ref, qkg_ref, o_ref, *, tn):
    x = x_ref[...]
    ms = jnp.mean(x * x, axis=-1, keepdims=True)
    hn = (x * lax.rsqrt(ms + RMS_EPS) * gain_ref[...]).astype(_MXU_DTYPE)
    width = o_ref.shape[1]
    for n in range(pl.cdiv(width, tn)):
        hi = min((n + 1) * tn, width)
        acc = jnp.dot(hn, w_ref[:, n * tn:hi], preferred_element_type=_F32)
        for c in range(n * tn // LANES, hi // LANES):
            tile = acc[:, c * LANES - n * tn:(c + 1) * LANES - n * tn]
            if c in QK_NORM_COLS:
                tile = _pair_rms(tile, qkg_ref[:, c * LANES:(c + 1) * LANES])
            o_ref[:, c * LANES:(c + 1) * LANES] = tile


def _inproj(x2d, gain_row, w_all, layer, qk_gain_row):
    m, d = x2d.shape
    n = w_all.shape[2]
    tm = 512
    return pl.pallas_call(
        functools.partial(_inproj_kernel, tn=1024),
        grid=(m // tm,),
        in_specs=[pl.BlockSpec((tm, d), lambda i: (i, 0)),
                  pl.BlockSpec((1, d), lambda i: (0, 0)),
                  pl.BlockSpec((None, d, n), lambda i: (layer, 0, 0)),
                  pl.BlockSpec((1, n), lambda i: (0, 0))],
        out_specs=pl.BlockSpec((tm, n), lambda i: (i, 0)),
        out_shape=jax.ShapeDtypeStruct((m, n), _F32),
        compiler_params=pltpu.CompilerParams(
            dimension_semantics=("arbitrary",), vmem_limit_bytes=_VMEM_LIMIT),
        name="inproj",
    )(x2d, gain_row, w_all, qk_gain_row)


def _rel_bucket(dist):
    max_exact = NUM_BUCKETS // 2
    d = jnp.maximum(dist, 0)
    log_ratio = (jnp.log(jnp.maximum(d, 1).astype(_F32) / max_exact)
                 / math.log(REL_MAX_DISTANCE / max_exact))
    large = max_exact + (log_ratio * (NUM_BUCKETS - max_exact)).astype(jnp.int32)
    large = jnp.minimum(large, NUM_BUCKETS - 1)
    return jnp.where(d < max_exact, d, large)


def _bias_kernel(rb_ref, bm_ref, bw_ref, bs_ref, om_ref, ow_ref, os_ref, *, n_heads):
    h = pl.program_id(0)

    def lookup(buckets, col):
        acc = jnp.zeros(buckets.shape, _F32)
        for k in range(NUM_BUCKETS):
            acc = jnp.where(buckets == k, rb_ref[k, col], acc)
        return acc

    for d in range(bm_ref.shape[0]):
        om_ref[0, d] = lookup(bm_ref[d], h)
    for d in range(bw_ref.shape[0]):
        ow_ref[0, d] = lookup(bw_ref[d], h) * LOG2E
    os_ref[0] = lookup(bs_ref[...], n_heads + h)


def _bias_tiles(rel_bias):
    n_heads = rel_bias.shape[1] // 2

    def toeplitz(size, count):
        i = jnp.arange(size)[:, None]
        j = jnp.arange(size)[None, :]
        return jnp.stack([_rel_bucket(d * size + i - j) for d in range(count)])

    bm = toeplitz(MOBA_BLOCK, MOBA_NEAR_TILES)
    bw = toeplitz(MOBA_WIDE, MOBA_WIDE_NEAR_TILES)
    w = SWA_WINDOW
    bs = _rel_bucket(jnp.arange(w)[:, None] + w - jnp.arange(2 * w)[None, :])
    return pl.pallas_call(
        functools.partial(_bias_kernel, n_heads=n_heads),
        grid=(n_heads,),
        in_specs=[pl.BlockSpec(memory_space=pltpu.SMEM),
                  pl.BlockSpec(bm.shape, lambda h: (0, 0, 0)),
                  pl.BlockSpec(bw.shape, lambda h: (0, 0, 0)),
                  pl.BlockSpec(bs.shape, lambda h: (0, 0))],
        out_specs=[pl.BlockSpec((1,) + bm.shape, lambda h: (h, 0, 0, 0)),
                   pl.BlockSpec((1,) + bw.shape, lambda h: (h, 0, 0, 0)),
                   pl.BlockSpec((1,) + bs.shape, lambda h: (h, 0, 0))],
        out_shape=[jax.ShapeDtypeStruct((n_heads,) + bm.shape, _F32),
                   jax.ShapeDtypeStruct((n_heads,) + bw.shape, _F32),
                   jax.ShapeDtypeStruct((n_heads,) + bs.shape, _F32)],
        compiler_params=pltpu.CompilerParams(
            dimension_semantics=("arbitrary",), vmem_limit_bytes=_VMEM_LIMIT),
        name="bias_tiles",
    )(rel_bias, bm, bw, bs)


def _forget_cumsum(ff_ref, bf_ref, c_sc):
    t_len = c_sc.shape[0]
    ff = ff_ref[0] + bf_ref[...]
    log_f = jnp.minimum(ff, 0.0) - jnp.log(1.0 + jnp.exp(-jnp.abs(ff)))
    parts = [part.astype(_MXU_DTYPE) for part in _split3(log_f)]
    ch = 256
    tri = (lax.broadcasted_iota(jnp.int32, (ch, ch), 0)
           >= lax.broadcasted_iota(jnp.int32, (ch, ch), 1)).astype(_MXU_DTYPE)
    carry = jnp.zeros((1, LANES), _F32)
    for r in range(t_len // ch):
        inc = carry
        for part in reversed(parts):
            inc = inc + jnp.dot(tri, part[r * ch:(r + 1) * ch], preferred_element_type=_F32)
        c_sc[r * ch:(r + 1) * ch, :] = inc
        carry = inc[ch - 1:ch, :]


def _lane_column(x, col):
    return jnp.sum(jnp.where(_lane_iota(x.shape) == col, x, 0.0), axis=-1, keepdims=True)


def _lane_fields(shape, a0, fields):
    lane = _lane_iota(shape)
    out = jnp.zeros(shape, _F32)
    for n, f in enumerate(fields):
        out = jnp.where(lane == a0 + n, f, out)
    return out


def _write_ones_column(vext_sc, h, hh):
    @pl.when(pl.program_id(0) == 0)
    def _():
        lane = _lane_iota((vext_sc.shape[1], LANES))
        vext_sc[h, :, LANES:2 * LANES] = jnp.where(lane == hh, 1.0, 0.0).astype(_MXU_DTYPE)


def _bounded_weights(ps_ref, q_augs, kaug_sc, r, bias, diagonal):
    tq = q_augs[0].shape[0]
    if not diagonal:
        for h in range(len(q_augs)):
            s = lax.dot_general(q_augs[h], kaug_sc[h, pl.ds(r, tq), :], _NT,
                                preferred_element_type=_F32)
            if bias is not None:
                s = s + bias(h)
            ps_ref[h] = jnp.exp2(s).astype(_MXU_DTYPE)
        return
    half = tq // 2
    rows = lax.broadcasted_iota(jnp.int32, (tq, half), 0)
    cols = lax.broadcasted_iota(jnp.int32, (tq, half), 1)
    for h in range(len(q_augs)):
        b_h = None if bias is None else bias(h)
        s = lax.dot_general(q_augs[h], kaug_sc[h, pl.ds(r, half), :], _NT,
                            preferred_element_type=_F32)
        if b_h is not None:
            s = s + b_h[:, 0:half]
        ps_ref[h, :, 0:half] = jnp.exp2(jnp.where(rows >= cols, s, NEG_INF)).astype(_MXU_DTYPE)
        s = lax.dot_general(q_augs[h][half:, :], kaug_sc[h, pl.ds(r + half, half), :], _NT,
                            preferred_element_type=_F32)
        if b_h is not None:
            s = s + b_h[half:, half:]
        ps_ref[h, 0:half, half:] = jnp.zeros((half, half), _MXU_DTYPE)
        ps_ref[h, half:, half:] = jnp.exp2(jnp.where(_causal(half), s, NEG_INF)).astype(_MXU_DTYPE)


def _bounded_values(ps_ref, vext_sc, r, acc_sc):
    tq = ps_ref.shape[1]
    for p in range(acc_sc.shape[0]):
        acc_sc[p] += (
            jnp.dot(ps_ref[2 * p], vext_sc[2 * p, pl.ds(r, tq), :], preferred_element_type=_F32)
            + jnp.dot(ps_ref[2 * p + 1], vext_sc[2 * p + 1, pl.ds(r, tq), :],
                      preferred_element_type=_F32))


def _bounded_store(acc_sc, g_ref, o_ref):
    tq = acc_sc.shape[1]
    for p in range(acc_sc.shape[0]):
        acc = acc_sc[p]
        den = acc[:, LANES:]
        l = jnp.where(_own_half(0, (tq, LANES)), _lane_column(den, 0), _lane_column(den, 1))
        cols = slice(p * LANES, (p + 1) * LANES)
        o_ref[0, :, cols] = (acc[:, :LANES] / l * _silu(g_ref[0, :, cols])).astype(o_ref.dtype)


def _bounded_pipeline_step(n, ps_bufs, weights, values):
    def new_in(slot):
        def run():
            weights(ps_bufs[slot])
            values(ps_bufs[1 - slot])
        return run
    lax.cond(n % 2 == 0, new_in(0), new_in(1))


def _bounded_finish(n_done, ps_bufs, values):
    lax.cond(n_done % 2 == 0, lambda: values(ps_bufs[0]), lambda: values(ps_bufs[1]))


def _fox_extra_base(h):
    p, hh = divmod(h, 2)
    return (HEAD_DIM if hh == 0 else 0) + 8 * p


def _fox_extra_lanes(fields):
    rows = max(f.shape[1] for fs in fields for f in fs if hasattr(f, "shape"))
    sub = lax.broadcasted_iota(jnp.int32, (8, rows), 0)
    blocks = {}
    for h, fs in enumerate(fields):
        blk = jnp.zeros((8, rows), _F32)
        for n, f in enumerate(fs):
            blk = jnp.where(sub == n, f, blk)
        blocks[_fox_extra_base(h)] = blk
    pieces, at = [], 0
    for base in sorted(blocks):
        if base > at:
            pieces.append(jnp.zeros((base - at, rows), _F32))
        pieces.append(blocks[base])
        at = base + 8
    pieces.append(jnp.zeros((LANES - at, rows), _F32))
    return jnp.concatenate(pieces, axis=0).T


def _fox_bounded_kernel(bound_ref, q_ref, k_ref, v_ref, ff_ref, g_ref, bf_ref, o_ref,
                        kaug_sc, vext_sc, c_sc, ps_a, ps_b, acc_sc):
    i = pl.program_id(1)
    tq = q_ref.shape[1]
    n_heads = kaug_sc.shape[0]

    @pl.when(i == 0)
    def _prep():
        _forget_cumsum(ff_ref, bf_ref, c_sc)
        hi, mid, lo = _split3(-(c_sc[...] * LOG2E).T[0:8, :])
        extra = _fox_extra_lanes([[hi[h:h + 1], mid[h:h + 1], lo[h:h + 1], 1.0, 1.0, 1.0, 1.0, 1.0]
                                  for h in range(n_heads)])
        lane = _lane_iota(extra.shape)
        for h in range(n_heads):
            p, hh = divmod(h, 2)
            kn = k_ref[0, :, p * LANES:(p + 1) * LANES]
            v = v_ref[0, :, p * LANES:(p + 1) * LANES]
            a0 = _fox_extra_base(h)
            aug = jnp.where((lane >= a0) & (lane < a0 + 8), extra, 0.0)
            own = _own_half(hh, kn.shape)
            kaug_sc[h] = jnp.where(own, kn, aug).astype(_MXU_DTYPE)
            vext_sc[h, :, 0:LANES] = jnp.where(own, v, 0.0).astype(_MXU_DTYPE)
            _write_ones_column(vext_sc, h, hh)

    row0 = pl.multiple_of(i * tq, tq)
    off = jnp.full((1, 1), -LOG2E, _F32) * bound_ref[0]
    off_hi = off.astype(_MXU_DTYPE).astype(_F32)
    hi, mid, lo = _split3((c_sc[pl.ds(row0, tq), :] * LOG2E).T[0:8, :])
    extra = _fox_extra_lanes([[1.0, 1.0, 1.0, hi[h:h + 1], mid[h:h + 1], lo[h:h + 1],
                               off_hi, off - off_hi] for h in range(n_heads)])
    lane = _lane_iota(extra.shape)
    q_augs = []
    for h in range(n_heads):
        p, hh = divmod(h, 2)
        q = q_ref[0, :, p * LANES:(p + 1) * LANES] * (ATTN_SCALE * LOG2E)
        a0 = _fox_extra_base(h)
        aug = jnp.where((lane >= a0) & (lane < a0 + 8), extra, 0.0)
        q_augs.append(jnp.where(_own_half(hh, q.shape), q, aug).astype(_MXU_DTYPE))

    head_lanes = _lane_iota((1, LANES)) < n_heads
    c_first = c_sc[pl.ds(row0, 1), :]
    cutoff = EXP_ZERO_CUTOFF - 2.0 * bound_ref[0]

    def live(j):
        c_last = c_sc[pl.ds(jnp.maximum(j, 0) * tq + (tq - 1), 1), :]
        gap = jnp.max(jnp.where(head_lanes, c_first - c_last, -jnp.inf))
        return (j >= 0) & (gap >= cutoff)

    ps_bufs = (ps_a, ps_b)

    def values_at(r):
        return lambda ps_ref: _bounded_values(ps_ref, vext_sc, pl.multiple_of(r, tq), acc_sc)

    def body(state):
        j, _, r_prev, n = state
        r = pl.multiple_of(j * tq, tq)
        _bounded_pipeline_step(
            n, ps_bufs,
            lambda ps_ref: _bounded_weights(ps_ref, q_augs, kaug_sc, r, None, False),
            values_at(r_prev))
        return j - 1, live(j - 1), r, n + 1

    acc_sc[...] = jnp.zeros(acc_sc.shape, _F32)
    _bounded_weights(ps_a, q_augs, kaug_sc, row0, None, True)
    state = lax.while_loop(lambda state: state[1], body,
                           (i - 1, live(i - 1), row0, jnp.int32(1)))
    _bounded_finish(state[3] - 1, ps_bufs, values_at(state[2]))
    _bounded_store(acc_sc, g_ref, o_ref)


def _fox_bounded(proj, bf_row, bound, *, tq=512):
    b, t, _ = proj.shape
    n_heads = GROUP_WIDTH // HEAD_DIM
    wide = GROUP_WIDTH // LANES
    return pl.pallas_call(
        _fox_bounded_kernel,
        grid=(b, t // tq),
        in_specs=[pl.BlockSpec(memory_space=pltpu.SMEM),
                  pl.BlockSpec((1, tq, GROUP_WIDTH), lambda bb, i: (bb, i, COL_FQ // wide)),
                  pl.BlockSpec((1, t, GROUP_WIDTH), lambda bb, i: (bb, 0, COL_FK // wide)),
                  pl.BlockSpec((1, t, GROUP_WIDTH), lambda bb, i: (bb, 0, COL_FV // wide)),
                  pl.BlockSpec((1, t, LANES), lambda bb, i: (bb, 0, COL_FF)),
                  pl.BlockSpec((1, tq, GROUP_WIDTH), lambda bb, i: (bb, i, COL_FG // wide)),
                  pl.BlockSpec((1, LANES), lambda bb, i: (0, 0))],
        out_specs=pl.BlockSpec((1, tq, GROUP_WIDTH), lambda bb, i: (bb, i, 0)),
        out_shape=jax.ShapeDtypeStruct((b, t, GROUP_WIDTH), _MXU_DTYPE),
        scratch_shapes=[pltpu.VMEM((n_heads, t, LANES), _MXU_DTYPE),
                        pltpu.VMEM((n_heads, t, 2 * LANES), _MXU_DTYPE),
                        pltpu.VMEM((t, LANES), _F32),
                        pltpu.VMEM((n_heads, tq, tq), _MXU_DTYPE),
                        pltpu.VMEM((n_heads, tq, tq), _MXU_DTYPE),
                        pltpu.VMEM((n_heads // 2, tq, 2 * LANES), _F32)],
        compiler_params=pltpu.CompilerParams(
            dimension_semantics=("arbitrary", "arbitrary"),
            vmem_limit_bytes=_VMEM_LIMIT_WIDE),
        name="fox_bounded",
    )(bound, proj, proj, proj, proj, proj, bf_row)


def _fox_kernel(q_ref, k_ref, v_ref, ff_ref, g_ref, bf_ref, o_ref,
                kaug_sc, v_sc, c_sc, *, tq):
    p = pl.program_id(1)
    i = pl.program_id(2)

    def head_column(x, hh):
        return _lane_column(x, 2 * p + hh)

    @pl.when(i == 0)
    def _prep():
        pl.when(p == 0)(lambda: _forget_cumsum(ff_ref, bf_ref, c_sc))
        kn = k_ref[0]
        c_all = c_sc[...]
        lane = _lane_iota(kn.shape)
        for hh in range(2):
            a0 = HEAD_DIM if hh == 0 else 0
            hi, mid, lo = _split3(-head_column(c_all, hh))
            aug = jnp.where(lane == a0, hi,
                            jnp.where(lane == a0 + 1, mid,
                                      jnp.where(lane == a0 + 2, lo,
                                                jnp.where((lane >= a0 + 3) & (lane < a0 + 6), 1.0, 0.0))))
            kaug_sc[hh] = jnp.where(_own_half(hh, kn.shape), kn, aug).astype(_MXU_DTYPE)
        v_sc[...] = v_ref[0].astype(_MXU_DTYPE)

    q = q_ref[0] * ATTN_SCALE
    row0 = pl.multiple_of(i * tq, tq)
    c_t = c_sc[pl.ds(row0, tq), :]
    lane = _lane_iota(q.shape)
    outs = []
    for hh in range(2):
        a0 = HEAD_DIM if hh == 0 else 0
        hi, mid, lo = _split3(head_column(c_t, hh))
        aug = jnp.where((lane >= a0) & (lane < a0 + 3), 1.0,
                        jnp.where(lane == a0 + 3, hi,
                                  jnp.where(lane == a0 + 4, mid,
                                            jnp.where(lane == a0 + 5, lo, 0.0))))
        q_aug = jnp.where(_own_half(hh, q.shape), q, aug).astype(_MXU_DTYPE)

        s = lax.dot_general(q_aug, kaug_sc[hh, pl.ds(row0, tq), :], _NT,
                            preferred_element_type=_F32)
        s = jnp.where(_causal(tq), s, NEG_INF)
        m = jnp.max(s, axis=-1, keepdims=True)
        pr = jnp.exp(s - m)
        l = jnp.sum(pr, axis=-1, keepdims=True)
        acc = jnp.dot(pr.astype(_MXU_DTYPE), v_sc[pl.ds(row0, tq), :], preferred_element_type=_F32)

        def body(j, carry, q_aug=q_aug, hh=hh):
            r = pl.multiple_of(j * tq, tq)
            s = lax.dot_general(q_aug, kaug_sc[hh, pl.ds(r, tq), :], _NT,
                                preferred_element_type=_F32)
            return _flash_update(s, v_sc[pl.ds(r, tq), :], *carry)

        m, l, acc = lax.fori_loop(0, i, body, (m, l, acc))
        outs.append(acc / l)
    o = jnp.where(_own_half(0, outs[0].shape), outs[0], outs[1])
    o_ref[0] = (o * _silu(g_ref[0])).astype(o_ref.dtype)


def _fox(proj, bf_row, *, tq=256):
    b, t, _ = proj.shape
    return pl.pallas_call(
        functools.partial(_fox_kernel, tq=tq),
        grid=(b, 2, t // tq),
        in_specs=[pl.BlockSpec((1, tq, LANES), lambda bb, p, i: (bb, i, COL_FQ + p)),
                  pl.BlockSpec((1, t, LANES), lambda bb, p, i: (bb, 0, COL_FK + p)),
                  pl.BlockSpec((1, t, LANES), lambda bb, p, i: (bb, 0, COL_FV + p)),
                  pl.BlockSpec((1, t, LANES), lambda bb, p, i: (bb, 0, COL_FF)),
                  pl.BlockSpec((1, tq, LANES), lambda bb, p, i: (bb, i, COL_FG + p)),
                  pl.BlockSpec((1, LANES), lambda bb, p, i: (0, 0))],
        out_specs=pl.BlockSpec((1, tq, LANES), lambda bb, p, i: (bb, i, p)),
        out_shape=jax.ShapeDtypeStruct((b, t, GROUP_WIDTH), _MXU_DTYPE),
        scratch_shapes=[pltpu.VMEM((2, t, LANES), _MXU_DTYPE),
                        pltpu.VMEM((t, LANES), _MXU_DTYPE),
                        pltpu.VMEM((t, LANES), _F32)],
        compiler_params=pltpu.CompilerParams(
            dimension_semantics=("arbitrary", "arbitrary", "arbitrary"),
            vmem_limit_bytes=_VMEM_LIMIT),
        name="fox",
    )(proj, proj, proj, proj, proj, bf_row)


def _moba_block_means(kn, kmean_sc):
    kmean_sc[...] = jnp.zeros(kmean_sc.shape, _F32)
    for n in range(kn.shape[0] // MOBA_BLOCK):
        kmean_sc[n:n + 1, :] = jnp.mean(kn[n * MOBA_BLOCK:(n + 1) * MOBA_BLOCK], axis=0, keepdims=True)


def _moba_select(q_head, kmean, past):
    gate = lax.dot_general(q_head, kmean, _NT, precision=lax.Precision.HIGHEST,
                           preferred_element_type=_F32)
    lane_f = _lane_iota(gate.shape).astype(_F32)
    cand = jnp.where(past, gate, -jnp.inf)
    sel = jnp.zeros(gate.shape, _F32)
    for _ in range(MOBA_TOPK):
        mx = jnp.max(cand, axis=-1, keepdims=True)
        is_max = (cand == mx) & (mx > -jnp.inf)
        first = jnp.min(jnp.where(is_max, lane_f, float(LANES)), axis=-1, keepdims=True)
        pick = lane_f == first
        sel = jnp.where(pick, 1.0, sel)
        cand = jnp.where(pick, -jnp.inf, cand)
    return sel


def _moba_bounded_kernel(rb_ref, bound_ref, q_ref, k_ref, v_ref, g_ref, bias_ref, o_ref,
                         kaug_sc, vext_sc, kmean_sc, ps_a, ps_b, acc_sc):
    i = pl.program_id(1)
    tq = q_ref.shape[1]
    n_heads = kaug_sc.shape[0]
    blk = MOBA_BLOCK
    near = MOBA_WIDE_NEAR_TILES

    @pl.when(i == 0)
    def _prep():
        lane = _lane_iota((k_ref.shape[1], LANES))
        row_blk = lax.broadcasted_iota(jnp.int32, lane.shape, 0) // blk
        for h in range(n_heads):
            p, hh = divmod(h, 2)
            kn = k_ref[0, :, p * LANES:(p + 1) * LANES]
            v = v_ref[0, :, p * LANES:(p + 1) * LANES]
            if hh == 0:
                _moba_block_means(kn, kmean_sc.at[p])
            a0 = HEAD_DIM if hh == 0 else 0
            aug = jnp.where((lane - a0 == row_blk) | (lane - a0 - 16 == row_blk)
                            | (lane - a0 == 32) | (lane - a0 == 33), 1.0, 0.0)
            own = _own_half(hh, kn.shape)
            kaug_sc[h] = jnp.where(own, kn, aug).astype(_MXU_DTYPE)
            vext_sc[h, :, 0:LANES] = jnp.where(own, v, 0.0).astype(_MXU_DTYPE)
            _write_ones_column(vext_sc, h, hh)

    row0 = pl.multiple_of(i * tq, tq)
    off = jnp.full((1, 1), -LOG2E, _F32) * bound_ref[0]
    off_hi = off.astype(_MXU_DTYPE).astype(_F32)
    blk_n = lax.broadcasted_iota(jnp.int32, (16, tq), 0)
    blk_f = blk_n.astype(_F32)
    q_blk = i * (tq // blk) + lax.broadcasted_iota(jnp.int32, (16, tq), 1) // blk
    past = blk_n < q_blk
    far = i - blk_n // 2 >= near
    sub8 = lax.broadcasted_iota(jnp.int32, (8, tq), 0)
    off_rows = jnp.where(sub8 == 0, off_hi, jnp.where(sub8 == 1, off - off_hi, 0.0))
    q_augs = []
    for h in range(n_heads):
        p, hh = divmod(h, 2)
        qn = q_ref[0, :, p * LANES:(p + 1) * LANES]
        own = _own_half(hh, qn.shape)
        gate = lax.dot_general(kmean_sc[p, 0:16, :], jnp.where(own, qn, 0.0), _NT,
                               precision=lax.Precision.HIGHEST, preferred_element_type=_F32)
        cand = jnp.where(past, gate, -jnp.inf)
        sel = jnp.zeros((16, tq), _F32)
        for _ in range(MOBA_TOPK):
            mx = jnp.max(cand, axis=0, keepdims=True)
            is_max = (cand == mx) & (mx > -jnp.inf)
            first = jnp.min(jnp.where(is_max, blk_f, float(LANES)), axis=0, keepdims=True)
            pick = blk_f == first
            sel = jnp.where(pick, 1.0, sel)
            cand = jnp.where(pick, -jnp.inf, cand)
        c_far = jnp.full((1, 1), LOG2E, _F32) * rb_ref[NUM_BUCKETS - 1, h]
        far_hi = c_far.astype(_MXU_DTYPE).astype(_F32)
        pen = jnp.where((blk_n == q_blk) | (past & (sel != 0.0)), 0.0, NEG_INF)
        fields = jnp.concatenate([pen + jnp.where(far, far_hi, 0.0),
                                  jnp.where(far, c_far - far_hi, 0.0),
                                  off_rows,
                                  jnp.zeros((HEAD_DIM - 40, tq), _F32)], axis=0)
        blank = jnp.zeros((HEAD_DIM, tq), _F32)
        aug = jnp.concatenate([blank, fields] if hh == 0 else [fields, blank], axis=0).T
        q_augs.append(jnp.where(own, qn * (ATTN_SCALE * LOG2E), aug).astype(_MXU_DTYPE))

    ps_bufs = (ps_a, ps_b)

    def values_at(r):
        return lambda ps_ref: _bounded_values(ps_ref, vext_sc, pl.multiple_of(r, tq), acc_sc)

    def body(j, state, with_bias):
        r_prev, n = state
        r = pl.multiple_of(j * tq, tq)
        bias = (lambda h: bias_ref[h, i - j]) if with_bias else None
        _bounded_pipeline_step(
            n, ps_bufs,
            lambda ps_ref: _bounded_weights(ps_ref, q_augs, kaug_sc, r, bias, False),
            values_at(r_prev))
        return r, n + 1

    acc_sc[...] = jnp.zeros(acc_sc.shape, _F32)
    _bounded_weights(ps_a, q_augs, kaug_sc, row0, lambda h: bias_ref[h, 0], True)
    state = (row0, jnp.int32(1))
    near_lo = jnp.maximum(i - (near - 1), 0)
    state = lax.fori_loop(near_lo, i, functools.partial(body, with_bias=True), state)
    state = lax.fori_loop(0, near_lo, functools.partial(body, with_bias=False), state)
    _bounded_finish(state[1] - 1, ps_bufs, values_at(state[0]))
    _bounded_store(acc_sc, g_ref, o_ref)


def _moba_bounded(proj, rel_bias, bias_tiles, bound):
    b, t, _ = proj.shape
    tq = MOBA_WIDE
    assert t % tq == 0 and t // MOBA_BLOCK <= 16
    n_heads = GROUP_WIDTH // HEAD_DIM
    wide = GROUP_WIDTH // LANES
    resident = pl.Buffered(1)
    return pl.pallas_call(
        _moba_bounded_kernel,
        grid=(b, t // tq),
        in_specs=[pl.BlockSpec(memory_space=pltpu.SMEM),
                  pl.BlockSpec(memory_space=pltpu.SMEM),
                  pl.BlockSpec((1, tq, GROUP_WIDTH), lambda bb, i: (bb, i, COL_MQ // wide)),
                  pl.BlockSpec((1, t, GROUP_WIDTH), lambda bb, i: (bb, 0, COL_MK // wide),
                               pipeline_mode=resident),
                  pl.BlockSpec((1, t, GROUP_WIDTH), lambda bb, i: (bb, 0, COL_MV // wide),
                               pipeline_mode=resident),
                  pl.BlockSpec((1, tq, GROUP_WIDTH), lambda bb, i: (bb, i, COL_MG // wide)),
                  pl.BlockSpec((n_heads, MOBA_WIDE_NEAR_TILES, tq, tq), lambda bb, i: (0, 0, 0, 0),
                               pipeline_mode=resident)],
        out_specs=pl.BlockSpec((1, tq, GROUP_WIDTH), lambda bb, i: (bb, i, 0)),
        out_shape=jax.ShapeDtypeStruct((b, t, GROUP_WIDTH), _MXU_DTYPE),
        scratch_shapes=[pltpu.VMEM((n_heads, t, LANES), _MXU_DTYPE),
                        pltpu.VMEM((n_heads, t, 2 * LANES), _MXU_DTYPE),
                        pltpu.VMEM((n_heads // 2, LANES, LANES), _F32),
                        pltpu.VMEM((n_heads, tq, tq), _MXU_DTYPE),
                        pltpu.VMEM((n_heads, tq, tq), _MXU_DTYPE),
                        pltpu.VMEM((n_heads // 2, tq, 2 * LANES), _F32)],
        compiler_params=pltpu.CompilerParams(
            dimension_semantics=("arbitrary", "arbitrary"),
            vmem_limit_bytes=_VMEM_LIMIT_WIDE),
        name="moba_bounded",
    )(rel_bias, bound, proj, proj, proj, proj, bias_tiles)


def _moba_kernel(rb_ref, q_ref, k_ref, v_ref, g_ref, bias_ref, o_ref,
                 kaug_sc, v_sc, kmean_sc):
    p = pl.program_id(1)
    i = pl.program_id(2)
    t_len = k_ref.shape[1]
    blk = MOBA_BLOCK
    nblk = t_len // blk
    near = MOBA_NEAR_TILES

    @pl.when(i == 0)
    def _prep():
        kn = k_ref[0]
        _moba_block_means(kn, kmean_sc)
        lane = _lane_iota(kn.shape)
        row_blk = lax.broadcasted_iota(jnp.int32, kn.shape, 0) // blk
        for hh in range(2):
            a0 = HEAD_DIM if hh == 0 else 0
            onehot = jnp.where((lane - a0 == row_blk) | (lane - a0 - 16 == row_blk), 1.0, 0.0)
            kaug_sc[hh] = jnp.where(_own_half(hh, kn.shape), kn, onehot).astype(_MXU_DTYPE)
        v_sc[...] = v_ref[0].astype(_MXU_DTYPE)

    qn = q_ref[0]
    row0 = pl.multiple_of(i * blk, blk)
    lane = _lane_iota(qn.shape)
    lane_f = lane.astype(_F32)
    past = lane < i
    outs = []
    for hh in range(2):
        own = _own_half(hh, qn.shape)
        sel = _moba_select(jnp.where(own, qn, 0.0), kmean_sc[...], past)
        c_far = jnp.full((1, LANES), rb_ref[NUM_BUCKETS - 1, 2 * p + hh], _F32)
        far_hi = c_far.astype(_MXU_DTYPE).astype(_F32)
        far_lo = c_far - far_hi
        pen = jnp.where(past & (sel == 0.0), NEG_INF, 0.0)
        aug = jnp.where(lane < 16, pen + jnp.where(i - lane >= near, far_hi, 0.0),
                        jnp.where((lane < 32) & (i - (lane - 16) >= near), far_lo, 0.0))
        if hh == 0:
            aug = pltpu.roll(aug, HEAD_DIM, axis=1)
        q_aug = jnp.where(own, qn * ATTN_SCALE, aug).astype(_MXU_DTYPE)

        def scores(r, q_aug=q_aug, hh=hh):
            return lax.dot_general(q_aug, kaug_sc[hh, pl.ds(r, blk), :], _NT,
                                   preferred_element_type=_F32)

        s = scores(row0) + bias_ref[hh, 0]
        s = jnp.where(_causal(blk), s, NEG_INF)
        m = jnp.max(s, axis=-1, keepdims=True)
        pr = jnp.exp(s - m)
        l = jnp.sum(pr, axis=-1, keepdims=True)
        acc = jnp.dot(pr.astype(_MXU_DTYPE), v_sc[pl.ds(row0, blk), :], preferred_element_type=_F32)

        def near_body(j, carry, hh=hh, scores=scores):
            r = pl.multiple_of(j * blk, blk)
            return _flash_update(scores(r) + bias_ref[hh, i - j], v_sc[pl.ds(r, blk), :], *carry)

        def far_body(j, carry, scores=scores):
            r = pl.multiple_of(j * blk, blk)
            return _flash_update(scores(r), v_sc[pl.ds(r, blk), :], *carry)

        near_lo = jnp.maximum(i - (near - 1), 0)
        carry = lax.fori_loop(near_lo, i, near_body, (m, l, acc))
        m, l, acc = lax.fori_loop(0, near_lo, far_body, carry)
        outs.append(acc / l)
    o = jnp.where(_own_half(0, outs[0].shape), outs[0], outs[1])
    o_ref[0] = (o * _silu(g_ref[0])).astype(o_ref.dtype)


def _moba(proj, rel_bias, bias_tiles):
    b, t, _ = proj.shape
    blk = MOBA_BLOCK
    assert t % blk == 0 and t // blk <= 16
    return pl.pallas_call(
        _moba_kernel,
        grid=(b, 2, t // blk),
        in_specs=[pl.BlockSpec(memory_space=pltpu.SMEM),
                  pl.BlockSpec((1, blk, LANES), lambda bb, p, i: (bb, i, COL_MQ + p)),
                  pl.BlockSpec((1, t, LANES), lambda bb, p, i: (bb, 0, COL_MK + p)),
                  pl.BlockSpec((1, t, LANES), lambda bb, p, i: (bb, 0, COL_MV + p)),
                  pl.BlockSpec((1, blk, LANES), lambda bb, p, i: (bb, i, COL_MG + p)),
                  pl.BlockSpec((2, MOBA_NEAR_TILES, blk, blk), lambda bb, p, i: (p, 0, 0, 0))],
        out_specs=pl.BlockSpec((1, blk, LANES), lambda bb, p, i: (bb, i, p)),
        out_shape=jax.ShapeDtypeStruct((b, t, GROUP_WIDTH), _MXU_DTYPE),
        scratch_shapes=[pltpu.VMEM((2, t, LANES), _MXU_DTYPE),
                        pltpu.VMEM((t, LANES), _MXU_DTYPE),
                        pltpu.VMEM((LANES, LANES), _F32)],
        compiler_params=pltpu.CompilerParams(
            dimension_semantics=("arbitrary", "arbitrary", "arbitrary"),
            vmem_limit_bytes=_VMEM_LIMIT),
        name="moba",
    )(rel_bias, proj, proj, proj, proj, bias_tiles)


def _sb_kernel(q_ref, k_ref, v_ref, g_ref, o_ref, k_sc, v_sc, *, tq):
    i = pl.program_id(1)
    n_heads = v_sc.shape[0]

    @pl.when(i == 0)
    def _prep():
        for p in range(n_heads // 2):
            k_sc[p] = k_ref[0, :, p * LANES:(p + 1) * LANES].astype(_MXU_DTYPE)
            v = v_ref[0, :, p * LANES:(p + 1) * LANES]
            for hh in range(2):
                v_sc[2 * p + hh] = jnp.where(_own_half(hh, v.shape), v, 0.0).astype(_MXU_DTYPE)

    subs = q_ref.shape[1] // tq
    q_heads = []
    for u in range(subs):
        q_heads.append([])
        for h in range(n_heads):
            p, hh = divmod(h, 2)
            q = q_ref[0, u * tq:(u + 1) * tq, p * LANES:(p + 1) * LANES] * (ATTN_SCALE * LOG2E)
            q_heads[u].append(jnp.where(_own_half(hh, q.shape), q, 0.0).astype(_MXU_DTYPE))
    strict = (lax.broadcasted_iota(jnp.int32, (tq, tq), 0)
              > lax.broadcasted_iota(jnp.int32, (tq, tq), 1))
    after = strict.astype(_MXU_DTYPE)
    sign_bit = jnp.uint32(0x80000000)

    def pair(back, runs, diagonal):
        chains = [(u, t, h) for u in range(subs) for t in range(2) for h in range(n_heads)]
        tile_idx = [[subs * i + u - back - t for t in range(2)] for u in range(subs)]
        valid = [[(j >= 0).astype(_F32) for j in js] for js in tile_idx]
        rows = [[pl.multiple_of(jnp.maximum(j, 0) * tq, tq) for j in js] for js in tile_idx]
        masked = [diagonal and t == 0 for _, t, _ in chains]
        zs = [lax.dot_general(q_heads[u][h], k_sc[h // 2, pl.ds(rows[u][t], tq), :], _NT,
                              preferred_element_type=_F32) for u, t, h in chains]
        drops = []
        for z, msk in zip(zs, masked):
            neg_abs = lax.bitcast_convert_type(lax.bitcast_convert_type(z, jnp.uint32) | sign_bit, _F32)
            drop = jnp.maximum(z, 0.0) + jnp.log(1.0 + jnp.exp2(neg_abs)) * LOG2E
            drops.append(jnp.where(strict, drop, 0.0) if msk else drop)
        laters = []
        for drop in drops:
            hi = drop.astype(_MXU_DTYPE)
            lo = (drop - hi.astype(_F32)).astype(_MXU_DTYPE)
            laters.append(jnp.dot(hi, after, preferred_element_type=_F32)
                          + jnp.dot(lo, after, preferred_element_type=_F32))
        pvs = []
        for (u, t, h), z, drop, later, msk in zip(chains, zs, drops, laters, masked):
            w = jnp.exp2(z - drop - later)
            if msk:
                w = jnp.where(strict, w, 0.0)
            pvs.append(jnp.dot(w.astype(_MXU_DTYPE), v_sc[h, pl.ds(rows[u][t], tq), :],
                               preferred_element_type=_F32))
        sums = [jnp.sum(drop, axis=-1, keepdims=True) for drop in drops]
        new_runs, outs = [], []
        for u in range(subs):
            new_runs.append([])
            for p in range(n_heads // 2):
                out = None
                for h in (2 * p, 2 * p + 1):
                    run = runs[u][h]
                    for t in range(2):
                        c = chains.index((u, t, h))
                        pv = pvs[c] * (jnp.exp2(run) * valid[u][t])
                        out = pv if out is None else out + pv
                        run = run - sums[c] * valid[u][t]
                    new_runs[u].append(run)
                outs.append(out)
        return new_runs, outs

    def alive(runs):
        top = functools.reduce(jnp.maximum, [r for rs in runs for r in rs])
        return jnp.max(top) >= EXP_ZERO_CUTOFF * LOG2E

    def flat(runs):
        return [r for rs in runs for r in rs]

    def nested(flat_runs):
        return [flat_runs[n_heads * u:n_heads * (u + 1)] for u in range(subs)]

    n_runs = n_heads * subs
    zero = jnp.zeros((tq, 1), _F32)
    runs, accs = pair(0, [[zero] * n_heads for _ in range(subs)], True)

    def body(state):
        back = state[0]
        runs, outs = pair(back, nested(list(state[2:2 + n_runs])), False)
        accs = [a + o for a, o in zip(state[2 + n_runs:], outs)]
        return (back + 2, alive(runs), *flat(runs), *accs)

    state = lax.while_loop(lambda state: (subs * i + subs - 1 - state[0] >= 0) & state[1], body,
                           (jnp.int32(2), alive(runs), *flat(runs), *accs))
    accs = state[2 + n_runs:]
    for u in range(subs):
        for p in range(n_heads // 2):
            sl = (0, slice(u * tq, (u + 1) * tq), slice(p * LANES, (p + 1) * LANES))
            o_ref[sl] = (accs[u * (n_heads // 2) + p] * _silu(g_ref[sl])).astype(o_ref.dtype)


def _sb(proj, *, tq=256, subs=2):
    b, t, _ = proj.shape
    rows = tq * subs
    n_heads = GROUP_WIDTH // HEAD_DIM
    wide = GROUP_WIDTH // LANES
    return pl.pallas_call(
        functools.partial(_sb_kernel, tq=tq),
        grid=(b, t // rows),
        in_specs=[pl.BlockSpec((1, rows, GROUP_WIDTH), lambda bb, i: (bb, i, COL_SQ // wide)),
                  pl.BlockSpec((1, t, GROUP_WIDTH), lambda bb, i: (bb, 0, COL_SK // wide)),
                  pl.BlockSpec((1, t, GROUP_WIDTH), lambda bb, i: (bb, 0, COL_SV // wide)),
                  pl.BlockSpec((1, rows, GROUP_WIDTH), lambda bb, i: (bb, i, COL_SG // wide))],
        out_specs=pl.BlockSpec((1, rows, GROUP_WIDTH), lambda bb, i: (bb, i, 0)),
        out_shape=jax.ShapeDtypeStruct((b, t, GROUP_WIDTH), _MXU_DTYPE),
        scratch_shapes=[pltpu.VMEM((n_heads // 2, t, LANES), _MXU_DTYPE),
                        pltpu.VMEM((n_heads, t, LANES), _MXU_DTYPE)],
        compiler_params=pltpu.CompilerParams(
            dimension_semantics=("arbitrary", "arbitrary"),
            vmem_limit_bytes=_VMEM_LIMIT),
        name="stickbreak",
    )(proj, proj, proj, proj)


def _swa_kernel(sink_ref, q_ref, k_ref, v_ref, g_ref, bias_ref, o_ref,
                k_sc, v_sc, *, tq):
    i = pl.program_id(1)
    t_len = k_ref.shape[1]
    w = SWA_WINDOW
    n_kv = k_sc.shape[0]

    @pl.when(i == 0)
    def _prep():
        kn = k_ref[0]
        v = v_ref[0]
        kn_swapped = pltpu.roll(kn, HEAD_DIM, axis=1)
        v_swapped = pltpu.roll(v, HEAD_DIM, axis=1)
        for kv in range(n_kv):
            keep = _own_half(kv, kn.shape)
            k_sc[kv, 0:w, :] = jnp.zeros((w, LANES), _MXU_DTYPE)
            v_sc[kv, 0:w, :] = jnp.zeros((w, LANES), _MXU_DTYPE)
            k_sc[kv, w:w + t_len, :] = jnp.where(keep, kn, kn_swapped).astype(_MXU_DTYPE)
            v_sc[kv, w:w + t_len, :] = jnp.where(keep, v, v_swapped).astype(_MXU_DTYPE)

    qi =lax.broadcasted_iota(jnp.int32, (w, 2 * w), 0)
    kj = lax.broadcasted_iota(jnp.int32, (w, 2 * w), 1)
    dist = qi + w - kj
    in_window = (dist >= 0) & (dist < w)
    n_sub = tq // w
    chains = [(u, kv, g) for u in range(n_sub) for kv in range(n_kv) for g in range(2)]
    rows = [pl.multiple_of((i * n_sub + u) * w, w) for u in range(n_sub)]
    ss = []
    for u, kv, g in chains:
        q_u = q_ref[0, u * w:(u + 1) * w, kv * LANES:(kv + 1) * LANES] * ATTN_SCALE
        q_h = jnp.where(_own_half(g, q_u.shape), q_u, 0.0).astype(_MXU_DTYPE)
        ss.append(lax.dot_general(q_h, k_sc[kv, pl.ds(rows[u], 2 * w), :], _NT,
                                  preferred_element_type=_F32))
    es, dens = [], []
    for (u, kv, g), s in zip(chains, ss):
        allowed = in_window & (kj + (i * n_sub + u - 1) * w >= 0)
        s = jnp.where(allowed, s + bias_ref[2 * kv + g], NEG_INF)
        sink = sink_ref[2 * kv + g]
        m = jnp.maximum(jnp.max(s, axis=-1, keepdims=True), sink)
        e = jnp.exp(s - m)
        dens.append(jnp.sum(e, axis=-1, keepdims=True) + jnp.exp(sink - m))
        es.append(e.astype(_MXU_DTYPE))
    outs = [jnp.dot(e, v_sc[kv, pl.ds(rows[u], 2 * w), :], preferred_element_type=_F32) / den
            for (u, kv, g), e, den in zip(chains, es, dens)]
    for u in range(n_sub):
        for kv in range(n_kv):
            c = chains.index((u, kv, 0))
            o = jnp.where(_own_half(0, (w, LANES)), outs[c], outs[c + 1])
            sl = (0, slice(u * w, (u + 1) * w), slice(kv * LANES, (kv + 1) * LANES))
            o_ref[sl] = (o * _silu(g_ref[sl])).astype(o_ref.dtype)


def _swa(proj, sinks, bias_tiles, *, tq=512):
    b, t, _ = proj.shape
    w = SWA_WINDOW
    n_heads = GROUP_WIDTH // HEAD_DIM
    wide = GROUP_WIDTH // LANES
    return pl.pallas_call(
        functools.partial(_swa_kernel, tq=tq),
        grid=(b, t // tq),
        in_specs=[pl.BlockSpec(memory_space=pltpu.SMEM),
                  pl.BlockSpec((1, tq, GROUP_WIDTH), lambda bb, i: (bb, i, COL_WQ // wide)),
                  pl.BlockSpec((1, t, LANES), lambda bb, i: (bb, 0, COL_WK)),
                  pl.BlockSpec((1, t, LANES), lambda bb, i: (bb, 0, COL_WV)),
                  pl.BlockSpec((1, tq, GROUP_WIDTH), lambda bb, i: (bb, i, COL_WG // wide)),
                  pl.BlockSpec((n_heads, w, 2 * w), lambda bb, i: (0, 0, 0))],
        out_specs=pl.BlockSpec((1, tq, GROUP_WIDTH), lambda bb, i: (bb, i, 0)),
        out_shape=jax.ShapeDtypeStruct((b, t, GROUP_WIDTH), _MXU_DTYPE),
        scratch_shapes=[pltpu.VMEM((n_heads // 2, t + w, LANES), _MXU_DTYPE),
                        pltpu.VMEM((n_heads // 2, t + w, LANES), _MXU_DTYPE)],
        compiler_params=pltpu.CompilerParams(
            dimension_semantics=("arbitrary", "arbitrary"),
            vmem_limit_bytes=_VMEM_LIMIT),
        name="swa",
    )(sinks, proj, proj, proj, proj, bias_tiles)


def _outproj_kernel(x_ref, ya_ref, yb_ref, yc_ref, yd_ref, w_ref, o_ref):
    acc = x_ref[...]
    for g, y_ref in enumerate((ya_ref, yb_ref, yc_ref, yd_ref)):
        acc = acc + jnp.dot(y_ref[...], w_ref[g * GROUP_WIDTH:(g + 1) * GROUP_WIDTH, :],
                            preferred_element_type=_F32)
    o_ref[...] = acc


def _outproj(x2d, ys, w_all, layer):
    m, d = x2d.shape
    tm = 512
    y_spec = pl.BlockSpec((tm, GROUP_WIDTH), lambda i: (i, 0))
    return pl.pallas_call(
        _outproj_kernel,
        grid=(m // tm,),
        in_specs=[pl.BlockSpec((tm, d), lambda i: (i, 0)), y_spec, y_spec, y_spec, y_spec,
                  pl.BlockSpec((None,) + w_all.shape[1:], lambda i: (layer, 0, 0))],
        out_specs=pl.BlockSpec((tm, d), lambda i: (i, 0)),
        out_shape=jax.ShapeDtypeStruct((m, d), _F32),
        compiler_params=pltpu.CompilerParams(
            dimension_semantics=("arbitrary",), vmem_limit_bytes=_VMEM_LIMIT),
        name="outproj",
    )(x2d, *ys, w_all)


def _rearrange_w_in(w_in):
    off = 3 * GROUP_WIDTH
    ff = w_in[..., off:off + FOX_FORGET_COLS]
    pad = jnp.zeros(w_in.shape[:-1] + (LANES - FOX_FORGET_COLS,), w_in.dtype)
    out = jnp.concatenate([w_in[..., :off], w_in[..., off + FOX_FORGET_COLS:], ff, pad], axis=-1)
    assert out.shape[-1] == PROJ_WIDTH
    return out


def _qk_logit_bound(qk_gain):
    g = jnp.abs(qk_gain.astype(_F32))
    return ATTN_SCALE * HEAD_DIM * jnp.max(g[0]) * jnp.max(g[1])


def _qk_gain_row(fox_g, moba_g, swa_g):
    def tiles(g, n):
        return jnp.tile(g.astype(_F32), 2 * n)
    row = jnp.zeros((PROJ_WIDTH,), _F32)
    for col, g, n in ((COL_FQ, fox_g[0], 2), (COL_FK, fox_g[1], 2),
                      (COL_MQ, moba_g[0], 2), (COL_MK, moba_g[1], 2),
                      (COL_WQ, swa_g[0], 2), (COL_WK, swa_g[1], 1)):
        row = lax.dynamic_update_slice(row, tiles(g, n), (col * LANES,))
    return row[None, :]


def kernel(x, norm_gain, w_in, b_forget, fox_qk_gain, moba_qk_gain, swa_qk_gain, sinks, w_out, rel_bias):
    b, t, d = x.shape
    depth = w_in.shape[0]
    w_in_r = _rearrange_w_in(w_in.astype(_MXU_DTYPE))
    w_out_c = w_out.astype(_MXU_DTYPE)
    moba_bias, moba_wide_bias, swa_bias = _bias_tiles(rel_bias)
    moba_bias_max = jnp.max(jnp.abs(rel_bias[:, :rel_bias.shape[1] // 2].astype(_F32)))
    x2d = x.reshape(b * t, d)
    for layer in range(depth):
        qk_gain_row = _qk_gain_row(fox_qk_gain[layer], moba_qk_gain[layer], swa_qk_gain[layer])
        proj = _inproj(x2d, norm_gain[layer][None, :], w_in_r, layer, qk_gain_row)
        proj = proj.reshape(b, t, PROJ_WIDTH)
        bf_row = jnp.pad(b_forget[layer], (0, LANES - FOX_FORGET_COLS))[None, :]
        fox_bound = _qk_logit_bound(fox_qk_gain[layer])
        y_fox = lax.cond(2.0 * fox_bound <= BOUNDED_SOFTMAX_RANGE,
                         lambda: _fox_bounded(proj, bf_row, fox_bound.reshape(1)),
                         lambda: _fox(proj, bf_row))
        moba_bound = _qk_logit_bound(moba_qk_gain[layer]) + moba_bias_max
        y_moba = lax.cond(2.0 * moba_bound <= BOUNDED_SOFTMAX_RANGE,
                          lambda: _moba_bounded(proj, rel_bias, moba_wide_bias,
                                                moba_bound.reshape(1)),
                          lambda: _moba(proj, rel_bias, moba_bias))
        y_sb = _sb(proj)
        y_swa = _swa(proj, sinks[layer], swa_bias)
        ys = [y.reshape(b * t, GROUP_WIDTH) for y in (y_fox, y_moba, y_sb, y_swa)]
        x2d = _outproj(x2d, ys, w_out_c, layer)
    return x2d.reshape(b, t, d)
```

```python
import functools
import math

import jax
import jax.numpy as jnp
from jax import lax
from jax.experimental import pallas as pl
from jax.experimental.pallas import tpu as pltpu

HEAD_DIM = 64
LANES = 128
GROUP_WIDTH = 256
MOBA_BLOCK = 256
MOBA_TOPK = 3
SWA_WINDOW = 128
NUM_BUCKETS = 32
REL_MAX_DISTANCE = 1024
RMS_EPS = 1e-6
NEG_INF = -1e30
ATTN_SCALE = HEAD_DIM ** -0.5
LOG2E = math.log2(math.e)
BOUNDED_SOFTMAX_RANGE = 60.0
MOBA_NEAR_TILES = REL_MAX_DISTANCE // MOBA_BLOCK + 1
MOBA_WIDE = 2 * MOBA_BLOCK
MOBA_WIDE_NEAR_TILES = REL_MAX_DISTANCE // MOBA_WIDE + 1
FOX_FORGET_COLS = 4
EXP_ZERO_CUTOFF = -104.0

COL_FQ, COL_FK, COL_FV, COL_FG = 0, 2, 4, 6
COL_MQ, COL_MK, COL_MV, COL_MG = 8, 10, 12, 14
COL_SQ, COL_SK, COL_SV, COL_SG = 16, 18, 20, 22
COL_WQ, COL_WK, COL_WV, COL_WG = 24, 26, 27, 28
COL_FF = 30
PROJ_WIDTH = (COL_FF + 1) * LANES
QK_NORM_COLS = (COL_FQ, COL_FQ + 1, COL_FK, COL_FK + 1, COL_MQ, COL_MQ + 1, COL_MK, COL_MK + 1,
                COL_WQ, COL_WQ + 1, COL_WK)

_MXU_DTYPE = jnp.bfloat16
_F32 = jnp.float32
_VMEM_LIMIT = 48 * 1024 * 1024
_VMEM_LIMIT_WIDE = 56 * 1024 * 1024

_NT = (((1,), (1,)), ((), ()))


def _lane_iota(shape):
    return lax.broadcasted_iota(jnp.int32, shape, len(shape) - 1)


def _pair_rms(x, gain_row):
    low = _lane_iota(x.shape) < HEAD_DIM
    sq = x * x
    ms_lo = jnp.sum(jnp.where(low, sq, 0.0), axis=-1, keepdims=True)
    ms_hi = jnp.sum(jnp.where(low, 0.0, sq), axis=-1, keepdims=True)
    ms = jnp.where(low, ms_lo, ms_hi) * (1.0 / HEAD_DIM)
    return x * lax.rsqrt(ms + RMS_EPS) * gain_row


def _silu(g):
    return g * (1.0 / (1.0 + jnp.exp(-g)))


def _split3(x):
    hi = x.astype(_MXU_DTYPE).astype(_F32)
    r = x - hi
    mid = r.astype(_MXU_DTYPE).astype(_F32)
    return hi, mid, r - mid


def _own_half(hh, shape):
    lane = _lane_iota(shape)
    return (lane < HEAD_DIM) if hh == 0 else (lane >= HEAD_DIM)


def _flash_update(s, v_tile, m, l, acc):
    m_new = jnp.maximum(m, jnp.max(s, axis=-1, keepdims=True))
    alpha = jnp.exp(m - m_new)
    p = jnp.exp(s - m_new)
    l = alpha * l + jnp.sum(p, axis=-1, keepdims=True)
    acc = alpha * acc + jnp.dot(p.astype(_MXU_DTYPE), v_tile, preferred_element_type=_F32)
    return m_new, l, acc


def _causal(tq):
    return (lax.broadcasted_iota(jnp.int32, (tq, tq), 0)
            >= lax.broadcasted_iota(jnp.int32, (tq, tq), 1))


def _inproj_kernel(x_ref, gain_ref, w_ref, qkg_ref, o_ref, *, tn):
    x = x_ref[...]
    ms = jnp.mean(x * x, axis=-1, keepdims=True)
    hn = (x * lax.rsqrt(ms + RMS_EPS) * gain_ref[...]).astype(_MXU_DTYPE)
    width = o_ref.shape[1]
    for n in range(pl.cdiv(width, tn)):
        hi = min((n + 1) * tn, width)
        acc = jnp.dot(hn, w_ref[:, n * tn:hi], preferred_element_type=_F32)
        for c in range(n * tn // LANES, hi // LANES):
            tile = acc[:, c * LANES - n * tn:(c + 1) * LANES - n * tn]
            if c in QK_NORM_COLS:
                tile = _pair_rms(tile, qkg_ref[:, c * LANES:(c + 1) * LANES])
            o_ref[:, c * LANES:(c + 1) * LANES] = tile


def _inproj(x2d, gain_row, w_all, layer, qk_gain_row):
    m, d = x2d.shape
    n = w_all.shape[2]
    tm = 512
    return pl.pallas_call(
        functools.partial(_inproj_kernel, tn=1024),
        grid=(m // tm,),
        in_specs=[pl.BlockSpec((tm, d), lambda i: (i, 0)),
                  pl.BlockSpec((1, d), lambda i: (0, 0)),
                  pl.BlockSpec((None, d, n), lambda i: (layer, 0, 0)),
                  pl.BlockSpec((1, n), lambda i: (0, 0))],
        out_specs=pl.BlockSpec((tm, n), lambda i: (i, 0)),
        out_shape=jax.ShapeDtypeStruct((m, n), _F32),
        compiler_params=pltpu.CompilerParams(
            dimension_semantics=("arbitrary",), vmem_limit_bytes=_VMEM_LIMIT),
        name="inproj",
    )(x2d, gain_row, w_all, qk_gain_row)


def _rel_bucket(dist):
    max_exact = NUM_BUCKETS // 2
    d = jnp.maximum(dist, 0)
    log_ratio = (jnp.log(jnp.maximum(d, 1).astype(_F32) / max_exact)
                 / math.log(REL_MAX_DISTANCE / max_exact))
    large = max_exact + (log_ratio * (NUM_BUCKETS - max_exact)).astype(jnp.int32)
    large = jnp.minimum(large, NUM_BUCKETS - 1)
    return jnp.where(d < max_exact, d, large)


def _bias_kernel(rb_ref, b_ref, o_ref, *, col0, scale):
    h = pl.program_id(0)
    for d in range(b_ref.shape[0]):
        buckets = b_ref[d]
        acc = jnp.zeros(buckets.shape, _F32)
        for k in range(NUM_BUCKETS):
            acc = jnp.where(buckets == k, rb_ref[k, col0 + h], acc)
        o_ref[0, d] = acc if scale == 1.0 else acc * scale


def _bias_lookup(rel_bias, buckets, col0, scale):
    n_heads = rel_bias.shape[1] // 2
    return pl.pallas_call(
        functools.partial(_bias_kernel, col0=col0, scale=scale),
        grid=(n_heads,),
        in_specs=[pl.BlockSpec(memory_space=pltpu.SMEM),
                  pl.BlockSpec(buckets.shape, lambda h: (0, 0, 0))],
        out_specs=pl.BlockSpec((1,) + buckets.shape, lambda h: (h, 0, 0, 0)),
        out_shape=jax.ShapeDtypeStruct((n_heads,) + buckets.shape, _F32),
        compiler_params=pltpu.CompilerParams(
            dimension_semantics=("arbitrary",), vmem_limit_bytes=_VMEM_LIMIT),
        name="bias_tiles",
    )(rel_bias, buckets)


def _toeplitz_buckets(size, count):
    i = jnp.arange(size)[:, None]
    j = jnp.arange(size)[None, :]
    return jnp.stack([_rel_bucket(d * size + i - j) for d in range(count)])


def _moba_bias_tiles(rel_bias):
    return _bias_lookup(rel_bias, _toeplitz_buckets(MOBA_BLOCK, MOBA_NEAR_TILES), 0, 1.0)


def _moba_wide_bias_tiles(rel_bias):
    return _bias_lookup(rel_bias, _toeplitz_buckets(MOBA_WIDE, MOBA_WIDE_NEAR_TILES), 0, LOG2E)


def _swa_bias_tiles(rel_bias):
    w = SWA_WINDOW
    buckets = _rel_bucket(jnp.arange(w)[:, None] + w - jnp.arange(2 * w)[None, :])
    return _bias_lookup(rel_bias, buckets[None], rel_bias.shape[1] // 2, 1.0)[:, 0]


def _forget_cumsum(ff_ref, bf_ref, c_sc):
    t_len = c_sc.shape[0]
    ff = ff_ref[0] + bf_ref[...]
    log_f = jnp.minimum(ff, 0.0) - jnp.log(1.0 + jnp.exp(-jnp.abs(ff)))
    parts = [part.astype(_MXU_DTYPE) for part in _split3(log_f)]
    ch = 256
    tri = (lax.broadcasted_iota(jnp.int32, (ch, ch), 0)
           >= lax.broadcasted_iota(jnp.int32, (ch, ch), 1)).astype(_MXU_DTYPE)
    carry = jnp.zeros((1, LANES), _F32)
    for r in range(t_len // ch):
        inc = carry
        for part in reversed(parts):
            inc = inc + jnp.dot(tri, part[r * ch:(r + 1) * ch], preferred_element_type=_F32)
        c_sc[r * ch:(r + 1) * ch, :] = inc
        carry = inc[ch - 1:ch, :]


def _lane_column(x, col):
    return jnp.sum(jnp.where(_lane_iota(x.shape) == col, x, 0.0), axis=-1, keepdims=True)


def _lane_fields(shape, a0, fields):
    lane = _lane_iota(shape)
    out = jnp.zeros(shape, _F32)
    for n, f in enumerate(fields):
        out = jnp.where(lane == a0 + n, f, out)
    return out


def _write_ones_column(vext_sc, h, hh):
    @pl.when(pl.program_id(0) == 0)
    def _():
        lane = _lane_iota((vext_sc.shape[1], LANES))
        vext_sc[h, :, LANES:2 * LANES] = jnp.where(lane == hh, 1.0, 0.0).astype(_MXU_DTYPE)


def _bounded_weights(ps_ref, q_augs, kaug_sc, r, bias, diagonal):
    tq = q_augs[0].shape[0]
    if not diagonal:
        for h in range(len(q_augs)):
            s = lax.dot_general(q_augs[h], kaug_sc[h, pl.ds(r, tq), :], _NT,
                                preferred_element_type=_F32)
            if bias is not None:
                s = s + bias(h)
            ps_ref[h] = jnp.exp2(s).astype(_MXU_DTYPE)
        return
    half = tq // 2
    rows = lax.broadcasted_iota(jnp.int32, (tq, half), 0)
    cols = lax.broadcasted_iota(jnp.int32, (tq, half), 1)
    for h in range(len(q_augs)):
        b_h = None if bias is None else bias(h)
        s = lax.dot_general(q_augs[h], kaug_sc[h, pl.ds(r, half), :], _NT,
                            preferred_element_type=_F32)
        if b_h is not None:
            s = s + b_h[:, 0:half]
        ps_ref[h, :, 0:half] = jnp.exp2(jnp.where(rows >= cols, s, NEG_INF)).astype(_MXU_DTYPE)
        s = lax.dot_general(q_augs[h][half:, :], kaug_sc[h, pl.ds(r + half, half), :], _NT,
                            preferred_element_type=_F32)
        if b_h is not None:
            s = s + b_h[half:, half:]
        ps_ref[h, 0:half, half:] = jnp.zeros((half, half), _MXU_DTYPE)
        ps_ref[h, half:, half:] = jnp.exp2(jnp.where(_causal(half), s, NEG_INF)).astype(_MXU_DTYPE)


def _bounded_values(ps_ref, vext_sc, r, acc_sc):
    tq = ps_ref.shape[1]
    for p in range(acc_sc.shape[0]):
        acc_sc[p] += (
            jnp.dot(ps_ref[2 * p], vext_sc[2 * p, pl.ds(r, tq), :], preferred_element_type=_F32)
            + jnp.dot(ps_ref[2 * p + 1], vext_sc[2 * p + 1, pl.ds(r, tq), :],
                      preferred_element_type=_F32))


def _bounded_store(acc_sc, g_ref, o_ref):
    tq = acc_sc.shape[1]
    for p in range(acc_sc.shape[0]):
        acc = acc_sc[p]
        den = acc[:, LANES:]
        l = jnp.where(_own_half(0, (tq, LANES)), _lane_column(den, 0), _lane_column(den, 1))
        cols = slice(p * LANES, (p + 1) * LANES)
        o_ref[0, :, cols] = (acc[:, :LANES] / l * _silu(g_ref[0, :, cols])).astype(o_ref.dtype)


def _bounded_pipeline_step(n, ps_bufs, weights, values):
    def new_in(slot):
        def run():
            weights(ps_bufs[slot])
            values(ps_bufs[1 - slot])
        return run
    lax.cond(n % 2 == 0, new_in(0), new_in(1))


def _bounded_finish(n_done, ps_bufs, values):
    lax.cond(n_done % 2 == 0, lambda: values(ps_bufs[0]), lambda: values(ps_bufs[1]))


def _fox_extra_base(h):
    p, hh = divmod(h, 2)
    return (HEAD_DIM if hh == 0 else 0) + 8 * p


def _fox_extra_lanes(fields):
    rows = max(f.shape[1] for fs in fields for f in fs if hasattr(f, "shape"))
    sub = lax.broadcasted_iota(jnp.int32, (8, rows), 0)
    blocks = {}
    for h, fs in enumerate(fields):
        blk = jnp.zeros((8, rows), _F32)
        for n, f in enumerate(fs):
            blk = jnp.where(sub == n, f, blk)
        blocks[_fox_extra_base(h)] = blk
    pieces, at = [], 0
    for base in sorted(blocks):
        if base > at:
            pieces.append(jnp.zeros((base - at, rows), _F32))
        pieces.append(blocks[base])
        at = base + 8
    pieces.append(jnp.zeros((LANES - at, rows), _F32))
    return jnp.concatenate(pieces, axis=0).T


def _fox_bounded_kernel(bound_ref, q_ref, k_ref, v_ref, ff_ref, g_ref, bf_ref, o_ref,
                        kaug_sc, vext_sc, c_sc, ps_a, ps_b, acc_sc):
    i = pl.program_id(1)
    tq = q_ref.shape[1]
    n_heads = kaug_sc.shape[0]

    @pl.when(i == 0)
    def _prep():
        _forget_cumsum(ff_ref, bf_ref, c_sc)
        hi, mid, lo = _split3(-(c_sc[...] * LOG2E).T[0:8, :])
        extra = _fox_extra_lanes([[hi[h:h + 1], mid[h:h + 1], lo[h:h + 1], 1.0, 1.0, 1.0, 1.0, 1.0]
                                  for h in range(n_heads)])
        lane = _lane_iota(extra.shape)
        for h in range(n_heads):
            p, hh = divmod(h, 2)
            kn = k_ref[0, :, p * LANES:(p + 1) * LANES]
            v = v_ref[0, :, p * LANES:(p + 1) * LANES]
            a0 = _fox_extra_base(h)
            aug = jnp.where((lane >= a0) & (lane < a0 + 8), extra, 0.0)
            own = _own_half(hh, kn.shape)
            kaug_sc[h] = jnp.where(own, kn, aug).astype(_MXU_DTYPE)
            vext_sc[h, :, 0:LANES] = jnp.where(own, v, 0.0).astype(_MXU_DTYPE)
            _write_ones_column(vext_sc, h, hh)

    row0 = pl.multiple_of(i * tq, tq)
    off = jnp.full((1, 1), -LOG2E, _F32) * bound_ref[0]
    off_hi = off.astype(_MXU_DTYPE).astype(_F32)
    hi, mid, lo = _split3((c_sc[pl.ds(row0, tq), :] * LOG2E).T[0:8, :])
    extra = _fox_extra_lanes([[1.0, 1.0, 1.0, hi[h:h + 1], mid[h:h + 1], lo[h:h + 1],
                               off_hi, off - off_hi] for h in range(n_heads)])
    lane = _lane_iota(extra.shape)
    q_augs = []
    for h in range(n_heads):
        p, hh = divmod(h, 2)
        q = q_ref[0, :, p * LANES:(p + 1) * LANES] * (ATTN_SCALE * LOG2E)
        a0 = _fox_extra_base(h)
        aug = jnp.where((lane >= a0) & (lane < a0 + 8), extra, 0.0)
        q_augs.append(jnp.where(_own_half(hh, q.shape), q, aug).astype(_MXU_DTYPE))

    head_lanes = _lane_iota((1, LANES)) < n_heads
    c_first = c_sc[pl.ds(row0, 1), :]
    cutoff = EXP_ZERO_CUTOFF - 2.0 * bound_ref[0]

    def live(j):
        c_last = c_sc[pl.ds(jnp.maximum(j, 0) * tq + (tq - 1), 1), :]
        gap = jnp.max(jnp.where(head_lanes, c_first - c_last, -jnp.inf))
        return (j >= 0) & (gap >= cutoff)

    ps_bufs = (ps_a, ps_b)

    def values_at(r):
        return lambda ps_ref: _bounded_values(ps_ref, vext_sc, pl.multiple_of(r, tq), acc_sc)

    def body(state):
        j, _, r_prev, n = state
        r = pl.multiple_of(j * tq, tq)
        _bounded_pipeline_step(
            n, ps_bufs,
            lambda ps_ref: _bounded_weights(ps_ref, q_augs, kaug_sc, r, None, False),
            values_at(r_prev))
        return j - 1, live(j - 1), r, n + 1

    acc_sc[...] = jnp.zeros(acc_sc.shape, _F32)
    _bounded_weights(ps_a, q_augs, kaug_sc, row0, None, True)
    state = lax.while_loop(lambda state: state[1], body,
                           (i - 1, live(i - 1), row0, jnp.int32(1)))
    _bounded_finish(state[3] - 1, ps_bufs, values_at(state[2]))
    _bounded_store(acc_sc, g_ref, o_ref)


def _fox_bounded(proj, bf_row, bound, *, tq=512):
    b, t, _ = proj.shape
    n_heads = GROUP_WIDTH // HEAD_DIM
    wide = GROUP_WIDTH // LANES
    return pl.pallas_call(
        _fox_bounded_kernel,
        grid=(b, t // tq),
        in_specs=[pl.BlockSpec(memory_space=pltpu.SMEM),
                  pl.BlockSpec((1, tq, GROUP_WIDTH), lambda bb, i: (bb, i, COL_FQ // wide)),
                  pl.BlockSpec((1, t, GROUP_WIDTH), lambda bb, i: (bb, 0, COL_FK // wide)),
                  pl.BlockSpec((1, t, GROUP_WIDTH), lambda bb, i: (bb, 0, COL_FV // wide)),
                  pl.BlockSpec((1, t, LANES), lambda bb, i: (bb, 0, COL_FF)),
                  pl.BlockSpec((1, tq, GROUP_WIDTH), lambda bb, i: (bb, i, COL_FG // wide)),
                  pl.BlockSpec((1, LANES), lambda bb, i: (0, 0))],
        out_specs=pl.BlockSpec((1, tq, GROUP_WIDTH), lambda bb, i: (bb, i, 0)),
        out_shape=jax.ShapeDtypeStruct((b, t, GROUP_WIDTH), _MXU_DTYPE),
        scratch_shapes=[pltpu.VMEM((n_heads, t, LANES), _MXU_DTYPE),
                        pltpu.VMEM((n_heads, t, 2 * LANES), _MXU_DTYPE),
                        pltpu.VMEM((t, LANES), _F32),
                        pltpu.VMEM((n_heads, tq, tq), _MXU_DTYPE),
                        pltpu.VMEM((n_heads, tq, tq), _MXU_DTYPE),
                        pltpu.VMEM((n_heads // 2, tq, 2 * LANES), _F32)],
        compiler_params=pltpu.CompilerParams(
            dimension_semantics=("arbitrary", "arbitrary"),
            vmem_limit_bytes=_VMEM_LIMIT_WIDE),
        name="fox_bounded",
    )(bound, proj, proj, proj, proj, proj, bf_row)


def _fox_kernel(q_ref, k_ref, v_ref, ff_ref, g_ref, bf_ref, o_ref,
                kaug_sc, v_sc, c_sc, *, tq):
    p = pl.program_id(1)
    i = pl.program_id(2)

    def head_column(x, hh):
        return _lane_column(x, 2 * p + hh)

    @pl.when(i == 0)
    def _prep():
        pl.when(p == 0)(lambda: _forget_cumsum(ff_ref, bf_ref, c_sc))
        kn = k_ref[0]
        c_all = c_sc[...]
        lane = _lane_iota(kn.shape)
        for hh in range(2):
            a0 = HEAD_DIM if hh == 0 else 0
            hi, mid, lo = _split3(-head_column(c_all, hh))
            aug = jnp.where(lane == a0, hi,
                            jnp.where(lane == a0 + 1, mid,
                                      jnp.where(lane == a0 + 2, lo,
                                                jnp.where((lane >= a0 + 3) & (lane < a0 + 6), 1.0, 0.0))))
            kaug_sc[hh] = jnp.where(_own_half(hh, kn.shape), kn, aug).astype(_MXU_DTYPE)
        v_sc[...] = v_ref[0].astype(_MXU_DTYPE)

    q = q_ref[0] * ATTN_SCALE
    row0 = pl.multiple_of(i * tq, tq)
    c_t = c_sc[pl.ds(row0, tq), :]
    lane = _lane_iota(q.shape)
    outs = []
    for hh in range(2):
        a0 = HEAD_DIM if hh == 0 else 0
        hi, mid, lo = _split3(head_column(c_t, hh))
        aug = jnp.where((lane >= a0) & (lane < a0 + 3), 1.0,
                        jnp.where(lane == a0 + 3, hi,
                                  jnp.where(lane == a0 + 4, mid,
                                            jnp.where(lane == a0 + 5, lo, 0.0))))
        q_aug = jnp.where(_own_half(hh, q.shape), q, aug).astype(_MXU_DTYPE)

        s = lax.dot_general(q_aug, kaug_sc[hh, pl.ds(row0, tq), :], _NT,
                            preferred_element_type=_F32)
        s = jnp.where(_causal(tq), s, NEG_INF)
        m = jnp.max(s, axis=-1, keepdims=True)
        pr = jnp.exp(s - m)
        l = jnp.sum(pr, axis=-1, keepdims=True)
        acc = jnp.dot(pr.astype(_MXU_DTYPE), v_sc[pl.ds(row0, tq), :], preferred_element_type=_F32)

        def body(j, carry, q_aug=q_aug, hh=hh):
            r = pl.multiple_of(j * tq, tq)
            s = lax.dot_general(q_aug, kaug_sc[hh, pl.ds(r, tq), :], _NT,
                                preferred_element_type=_F32)
            return _flash_update(s, v_sc[pl.ds(r, tq), :], *carry)

        m, l, acc = lax.fori_loop(0, i, body, (m, l, acc))
        outs.append(acc / l)
    o = jnp.where(_own_half(0, outs[0].shape), outs[0], outs[1])
    o_ref[0] = (o * _silu(g_ref[0])).astype(o_ref.dtype)


def _fox(proj, bf_row, *, tq=256):
    b, t, _ = proj.shape
    return pl.pallas_call(
        functools.partial(_fox_kernel, tq=tq),
        grid=(b, 2, t // tq),
        in_specs=[pl.BlockSpec((1, tq, LANES), lambda bb, p, i: (bb, i, COL_FQ + p)),
                  pl.BlockSpec((1, t, LANES), lambda bb, p, i: (bb, 0, COL_FK + p)),
                  pl.BlockSpec((1, t, LANES), lambda bb, p, i: (bb, 0, COL_FV + p)),
                  pl.BlockSpec((1, t, LANES), lambda bb, p, i: (bb, 0, COL_FF)),
                  pl.BlockSpec((1, tq, LANES), lambda bb, p, i: (bb, i, COL_FG + p)),
                  pl.BlockSpec((1, LANES), lambda bb, p, i: (0, 0))],
        out_specs=pl.BlockSpec((1, tq, LANES), lambda bb, p, i: (bb, i, p)),
        out_shape=jax.ShapeDtypeStruct((b, t, GROUP_WIDTH), _MXU_DTYPE),
        scratch_shapes=[pltpu.VMEM((2, t, LANES), _MXU_DTYPE),
                        pltpu.VMEM((t, LANES), _MXU_DTYPE),
                        pltpu.VMEM((t, LANES), _F32)],
        compiler_params=pltpu.CompilerParams(
            dimension_semantics=("arbitrary", "arbitrary", "arbitrary"),
            vmem_limit_bytes=_VMEM_LIMIT),
        name="fox",
    )(proj, proj, proj, proj, proj, bf_row)


def _moba_block_means(kn, kmean_sc):
    kmean_sc[...] = jnp.zeros(kmean_sc.shape, _F32)
    for n in range(kn.shape[0] // MOBA_BLOCK):
        kmean_sc[n:n + 1, :] = jnp.mean(kn[n * MOBA_BLOCK:(n + 1) * MOBA_BLOCK], axis=0, keepdims=True)


def _moba_select(q_head, kmean, past):
    gate = lax.dot_general(q_head, kmean, _NT, precision=lax.Precision.HIGHEST,
                           preferred_element_type=_F32)
    lane_f = _lane_iota(gate.shape).astype(_F32)
    cand = jnp.where(past, gate, -jnp.inf)
    sel = jnp.zeros(gate.shape, _F32)
    for _ in range(MOBA_TOPK):
        mx = jnp.max(cand, axis=-1, keepdims=True)
        is_max = (cand == mx) & (mx > -jnp.inf)
        first = jnp.min(jnp.where(is_max, lane_f, float(LANES)), axis=-1, keepdims=True)
        pick = lane_f == first
        sel = jnp.where(pick, 1.0, sel)
        cand = jnp.where(pick, -jnp.inf, cand)
    return sel


def _moba_bounded_kernel(rb_ref, bound_ref, q_ref, k_ref, v_ref, g_ref, bias_ref, o_ref,
                         kaug_sc, vext_sc, kmean_sc, ps_a, ps_b, acc_sc):
    i = pl.program_id(1)
    tq = q_ref.shape[1]
    n_heads = kaug_sc.shape[0]
    blk = MOBA_BLOCK
    near = MOBA_WIDE_NEAR_TILES

    @pl.when(i == 0)
    def _prep():
        lane = _lane_iota((k_ref.shape[1], LANES))
        row_blk = lax.broadcasted_iota(jnp.int32, lane.shape, 0) // blk
        for h in range(n_heads):
            p, hh = divmod(h, 2)
            kn = k_ref[0, :, p * LANES:(p + 1) * LANES]
            v = v_ref[0, :, p * LANES:(p + 1) * LANES]
            if hh == 0:
                _moba_block_means(kn, kmean_sc.at[p])
            a0 = HEAD_DIM if hh == 0 else 0
            aug = jnp.where((lane - a0 == row_blk) | (lane - a0 - 16 == row_blk)
                            | (lane - a0 == 32) | (lane - a0 == 33), 1.0, 0.0)
            own = _own_half(hh, kn.shape)
            kaug_sc[h] = jnp.where(own, kn, aug).astype(_MXU_DTYPE)
            vext_sc[h, :, 0:LANES] = jnp.where(own, v, 0.0).astype(_MXU_DTYPE)
            _write_ones_column(vext_sc, h, hh)

    row0 = pl.multiple_of(i * tq, tq)
    off = jnp.full((1, 1), -LOG2E, _F32) * bound_ref[0]
    off_hi = off.astype(_MXU_DTYPE).astype(_F32)
    blk_n = lax.broadcasted_iota(jnp.int32, (16, tq), 0)
    blk_f = blk_n.astype(_F32)
    q_blk = i * (tq // blk) + lax.broadcasted_iota(jnp.int32, (16, tq), 1) // blk
    past = blk_n < q_blk
    far = i - blk_n // 2 >= near
    sub8 = lax.broadcasted_iota(jnp.int32, (8, tq), 0)
    off_rows = jnp.where(sub8 == 0, off_hi, jnp.where(sub8 == 1, off - off_hi, 0.0))
    q_augs = []
    for h in range(n_heads):
        p, hh = divmod(h, 2)
        qn = q_ref[0, :, p * LANES:(p + 1) * LANES]
        own = _own_half(hh, qn.shape)
        gate = lax.dot_general(kmean_sc[p, 0:16, :], jnp.where(own, qn, 0.0), _NT,
                               precision=lax.Precision.HIGHEST, preferred_element_type=_F32)
        cand = jnp.where(past, gate, -jnp.inf)
        sel = jnp.zeros((16, tq), _F32)
        for _ in range(MOBA_TOPK):
            mx = jnp.max(cand, axis=0, keepdims=True)
            is_max = (cand == mx) & (mx > -jnp.inf)
            first = jnp.min(jnp.where(is_max, blk_f, float(LANES)), axis=0, keepdims=True)
            pick = blk_f == first
            sel = jnp.where(pick, 1.0, sel)
            cand = jnp.where(pick, -jnp.inf, cand)
        c_far = jnp.full((1, 1), LOG2E, _F32) * rb_ref[NUM_BUCKETS - 1, h]
        far_hi = c_far.astype(_MXU_DTYPE).astype(_F32)
        pen = jnp.where((blk_n == q_blk) | (past & (sel != 0.0)), 0.0, NEG_INF)
        fields = jnp.concatenate([pen + jnp.where(far, far_hi, 0.0),
                                  jnp.where(far, c_far - far_hi, 0.0),
                                  off_rows,
                                  jnp.zeros((HEAD_DIM - 40, tq), _F32)], axis=0)
        blank = jnp.zeros((HEAD_DIM, tq), _F32)
        aug = jnp.concatenate([blank, fields] if hh == 0 else [fields, blank], axis=0).T
        q_augs.append(jnp.where(own, qn * (ATTN_SCALE * LOG2E), aug).astype(_MXU_DTYPE))

    ps_bufs = (ps_a, ps_b)

    def values_at(r):
        return lambda ps_ref: _bounded_values(ps_ref, vext_sc, pl.multiple_of(r, tq), acc_sc)

    def body(j, state, with_bias):
        r_prev, n = state
        r = pl.multiple_of(j * tq, tq)
        bias = (lambda h: bias_ref[h, i - j]) if with_bias else None
        _bounded_pipeline_step(
            n, ps_bufs,
            lambda ps_ref: _bounded_weights(ps_ref, q_augs, kaug_sc, r, bias, False),
            values_at(r_prev))
        return r, n + 1

    acc_sc[...] = jnp.zeros(acc_sc.shape, _F32)
    _bounded_weights(ps_a, q_augs, kaug_sc, row0, lambda h: bias_ref[h, 0], True)
    state = (row0, jnp.int32(1))
    near_lo = jnp.maximum(i - (near - 1), 0)
    state = lax.fori_loop(near_lo, i, functools.partial(body, with_bias=True), state)
    state = lax.fori_loop(0, near_lo, functools.partial(body, with_bias=False), state)
    _bounded_finish(state[1] - 1, ps_bufs, values_at(state[0]))
    _bounded_store(acc_sc, g_ref, o_ref)


def _moba_bounded(proj, rel_bias, bias_tiles, bound):
    b, t, _ = proj.shape
    tq = MOBA_WIDE
    assert t % tq == 0 and t // MOBA_BLOCK <= 16
    n_heads = GROUP_WIDTH // HEAD_DIM
    wide = GROUP_WIDTH // LANES
    resident = pl.Buffered(1)
    return pl.pallas_call(
        _moba_bounded_kernel,
        grid=(b, t // tq),
        in_specs=[pl.BlockSpec(memory_space=pltpu.SMEM),
                  pl.BlockSpec(memory_space=pltpu.SMEM),
                  pl.BlockSpec((1, tq, GROUP_WIDTH), lambda bb, i: (bb, i, COL_MQ // wide)),
                  pl.BlockSpec((1, t, GROUP_WIDTH), lambda bb, i: (bb, 0, COL_MK // wide),
                               pipeline_mode=resident),
                  pl.BlockSpec((1, t, GROUP_WIDTH), lambda bb, i: (bb, 0, COL_MV // wide),
                               pipeline_mode=resident),
                  pl.BlockSpec((1, tq, GROUP_WIDTH), lambda bb, i: (bb, i, COL_MG // wide)),
                  pl.BlockSpec((n_heads, MOBA_WIDE_NEAR_TILES, tq, tq), lambda bb, i: (0, 0, 0, 0),
                               pipeline_mode=resident)],
        out_specs=pl.BlockSpec((1, tq, GROUP_WIDTH), lambda bb, i: (bb, i, 0)),
        out_shape=jax.ShapeDtypeStruct((b, t, GROUP_WIDTH), _MXU_DTYPE),
        scratch_shapes=[pltpu.VMEM((n_heads, t, LANES), _MXU_DTYPE),
                        pltpu.VMEM((n_heads, t, 2 * LANES), _MXU_DTYPE),
                        pltpu.VMEM((n_heads // 2, LANES, LANES), _F32),
                        pltpu.VMEM((n_heads, tq, tq), _MXU_DTYPE),
                        pltpu.VMEM((n_heads, tq, tq), _MXU_DTYPE),
                        pltpu.VMEM((n_heads // 2, tq, 2 * LANES), _F32)],
        compiler_params=pltpu.CompilerParams(
            dimension_semantics=("arbitrary", "arbitrary"),
            vmem_limit_bytes=_VMEM_LIMIT_WIDE),
        name="moba_bounded",
    )(rel_bias, bound, proj, proj, proj, proj, bias_tiles)


def _moba_kernel(rb_ref, q_ref, k_ref, v_ref, g_ref, bias_ref, o_ref,
                 kaug_sc, v_sc, kmean_sc):
    p = pl.program_id(1)
    i = pl.program_id(2)
    t_len = k_ref.shape[1]
    blk = MOBA_BLOCK
    nblk = t_len // blk
    near = MOBA_NEAR_TILES

    @pl.when(i == 0)
    def _prep():
        kn = k_ref[0]
        _moba_block_means(kn, kmean_sc)
        lane = _lane_iota(kn.shape)
        row_blk = lax.broadcasted_iota(jnp.int32, kn.shape, 0) // blk
        for hh in range(2):
            a0 = HEAD_DIM if hh == 0 else 0
            onehot = jnp.where((lane - a0 == row_blk) | (lane - a0 - 16 == row_blk), 1.0, 0.0)
            kaug_sc[hh] = jnp.where(_own_half(hh, kn.shape), kn, onehot).astype(_MXU_DTYPE)
        v_sc[...] = v_ref[0].astype(_MXU_DTYPE)

    qn = q_ref[0]
    row0 = pl.multiple_of(i * blk, blk)
    lane = _lane_iota(qn.shape)
    lane_f = lane.astype(_F32)
    past = lane < i
    outs = []
    for hh in range(2):
        own = _own_half(hh, qn.shape)
        sel = _moba_select(jnp.where(own, qn, 0.0), kmean_sc[...], past)
        c_far = jnp.full((1, LANES), rb_ref[NUM_BUCKETS - 1, 2 * p + hh], _F32)
        far_hi = c_far.astype(_MXU_DTYPE).astype(_F32)
        far_lo = c_far - far_hi
        pen = jnp.where(past & (sel == 0.0), NEG_INF, 0.0)
        aug = jnp.where(lane < 16, pen + jnp.where(i - lane >= near, far_hi, 0.0),
                        jnp.where((lane < 32) & (i - (lane - 16) >= near), far_lo, 0.0))
        if hh == 0:
            aug = pltpu.roll(aug, HEAD_DIM, axis=1)
        q_aug = jnp.where(own, qn * ATTN_SCALE, aug).astype(_MXU_DTYPE)

        def scores(r, q_aug=q_aug, hh=hh):
            return lax.dot_general(q_aug, kaug_sc[hh, pl.ds(r, blk), :], _NT,
                                   preferred_element_type=_F32)

        s = scores(row0) + bias_ref[hh, 0]
        s = jnp.where(_causal(blk), s, NEG_INF)
        m = jnp.max(s, axis=-1, keepdims=True)
        pr = jnp.exp(s - m)
        l = jnp.sum(pr, axis=-1, keepdims=True)
        acc = jnp.dot(pr.astype(_MXU_DTYPE), v_sc[pl.ds(row0, blk), :], preferred_element_type=_F32)

        def near_body(j, carry, hh=hh, scores=scores):
            r = pl.multiple_of(j * blk, blk)
            return _flash_update(scores(r) + bias_ref[hh, i - j], v_sc[pl.ds(r, blk), :], *carry)

        def far_body(j, carry, scores=scores):
            r = pl.multiple_of(j * blk, blk)
            return _flash_update(scores(r), v_sc[pl.ds(r, blk), :], *carry)

        near_lo = jnp.maximum(i - (near - 1), 0)
        carry = lax.fori_loop(near_lo, i, near_body, (m, l, acc))
        m, l, acc = lax.fori_loop(0, near_lo, far_body, carry)
        outs.append(acc / l)
    o = jnp.where(_own_half(0, outs[0].shape), outs[0], outs[1])
    o_ref[0] = (o * _silu(g_ref[0])).astype(o_ref.dtype)


def _moba(proj, rel_bias, bias_tiles):
    b, t, _ = proj.shape
    blk = MOBA_BLOCK
    assert t % blk == 0 and t // blk <= 16
    return pl.pallas_call(
        _moba_kernel,
        grid=(b, 2, t // blk),
        in_specs=[pl.BlockSpec(memory_space=pltpu.SMEM),
                  pl.BlockSpec((1, blk, LANES), lambda bb, p, i: (bb, i, COL_MQ + p)),
                  pl.BlockSpec((1, t, LANES), lambda bb, p, i: (bb, 0, COL_MK + p)),
                  pl.BlockSpec((1, t, LANES), lambda bb, p, i: (bb, 0, COL_MV + p)),
                  pl.BlockSpec((1, blk, LANES), lambda bb, p, i: (bb, i, COL_MG + p)),
                  pl.BlockSpec((2, MOBA_NEAR_TILES, blk, blk), lambda bb, p, i: (p, 0, 0, 0))],
        out_specs=pl.BlockSpec((1, blk, LANES), lambda bb, p, i: (bb, i, p)),
        out_shape=jax.ShapeDtypeStruct((b, t, GROUP_WIDTH), _MXU_DTYPE),
        scratch_shapes=[pltpu.VMEM((2, t, LANES), _MXU_DTYPE),
                        pltpu.VMEM((t, LANES), _MXU_DTYPE),
                        pltpu.VMEM((LANES, LANES), _F32)],
        compiler_params=pltpu.CompilerParams(
            dimension_semantics=("arbitrary", "arbitrary", "arbitrary"),
            vmem_limit_bytes=_VMEM_LIMIT),
        name="moba",
    )(rel_bias, proj, proj, proj, proj, bias_tiles)


def _sb_kernel(q_ref, k_ref, v_ref, g_ref, o_ref, k_sc, v_sc, *, tq):
    i = pl.program_id(1)
    n_heads = v_sc.shape[0]

    @pl.when(i == 0)
    def _prep():
        for p in range(n_heads // 2):
            k_sc[p] = k_ref[0, :, p * LANES:(p + 1) * LANES].astype(_MXU_DTYPE)
            v = v_ref[0, :, p * LANES:(p + 1) * LANES]
            for hh in range(2):
                v_sc[2 * p + hh] = jnp.where(_own_half(hh, v.shape), v, 0.0).astype(_MXU_DTYPE)

    subs = q_ref.shape[1] // tq
    q_heads = []
    for u in range(subs):
        q_heads.append([])
        for h in range(n_heads):
            p, hh = divmod(h, 2)
            q = q_ref[0, u * tq:(u + 1) * tq, p * LANES:(p + 1) * LANES] * (ATTN_SCALE * LOG2E)
            q_heads[u].append(jnp.where(_own_half(hh, q.shape), q, 0.0).astype(_MXU_DTYPE))
    strict = (lax.broadcasted_iota(jnp.int32, (tq, tq), 0)
              > lax.broadcasted_iota(jnp.int32, (tq, tq), 1))
    after = strict.astype(_MXU_DTYPE)
    sign_bit = jnp.uint32(0x80000000)

    def pair(back, runs, diagonal):
        chains = [(u, t, h) for u in range(subs) for t in range(2) for h in range(n_heads)]
        tile_idx = [[subs * i + u - back - t for t in range(2)] for u in range(subs)]
        valid = [[(j >= 0).astype(_F32) for j in js] for js in tile_idx]
        rows = [[pl.multiple_of(jnp.maximum(j, 0) * tq, tq) for j in js] for js in tile_idx]
        masked = [diagonal and t == 0 for _, t, _ in chains]
        zs = [lax.dot_general(q_heads[u][h], k_sc[h // 2, pl.ds(rows[u][t], tq), :], _NT,
                              preferred_element_type=_F32) for u, t, h in chains]
        drops = []
        for z, msk in zip(zs, masked):
            neg_abs = lax.bitcast_convert_type(lax.bitcast_convert_type(z, jnp.uint32) | sign_bit, _F32)
            drop = jnp.maximum(z, 0.0) + jnp.log(1.0 + jnp.exp2(neg_abs)) * LOG2E
            drops.append(jnp.where(strict, drop, 0.0) if msk else drop)
        laters = []
        for drop in drops:
            hi = drop.astype(_MXU_DTYPE)
            lo = (drop - hi.astype(_F32)).astype(_MXU_DTYPE)
            laters.append(jnp.dot(hi, after, preferred_element_type=_F32)
                          + jnp.dot(lo, after, preferred_element_type=_F32))
        pvs = []
        for (u, t, h), z, drop, later, msk in zip(chains, zs, drops, laters, masked):
            w = jnp.exp2(z - drop - later)
            if msk:
                w = jnp.where(strict, w, 0.0)
            pvs.append(jnp.dot(w.astype(_MXU_DTYPE), v_sc[h, pl.ds(rows[u][t], tq), :],
                               preferred_element_type=_F32))
        sums = [jnp.sum(drop, axis=-1, keepdims=True) for drop in drops]
        new_runs, outs = [], []
        for u in range(subs):
            new_runs.append([])
            for p in range(n_heads // 2):
                out = None
                for h in (2 * p, 2 * p + 1):
                    run = runs[u][h]
                    for t in range(2):
                        c = chains.index((u, t, h))
                        pv = pvs[c] * (jnp.exp2(run) * valid[u][t])
                        out = pv if out is None else out + pv
                        run = run - sums[c] * valid[u][t]
                    new_runs[u].append(run)
                outs.append(out)
        return new_runs, outs

    def alive(runs):
        top = functools.reduce(jnp.maximum, [r for rs in runs for r in rs])
        return jnp.max(top) >= EXP_ZERO_CUTOFF * LOG2E

    def flat(runs):
        return [r for rs in runs for r in rs]

    def nested(flat_runs):
        return [flat_runs[n_heads * u:n_heads * (u + 1)] for u in range(subs)]

    n_runs = n_heads * subs
    zero = jnp.zeros((tq, 1), _F32)
    runs, accs = pair(0, [[zero] * n_heads for _ in range(subs)], True)

    def body(state):
        back = state[0]
        runs, outs = pair(back, nested(list(state[2:2 + n_runs])), False)
        accs = [a + o for a, o in zip(state[2 + n_runs:], outs)]
        return (back + 2, alive(runs), *flat(runs), *accs)

    state = lax.while_loop(lambda state: (subs * i + subs - 1 - state[0] >= 0) & state[1], body,
                           (jnp.int32(2), alive(runs), *flat(runs), *accs))
    accs = state[2 + n_runs:]
    for u in range(subs):
        for p in range(n_heads // 2):
            sl = (0, slice(u * tq, (u + 1) * tq), slice(p * LANES, (p + 1) * LANES))
            o_ref[sl] = (accs[u * (n_heads // 2) + p] * _silu(g_ref[sl])).astype(o_ref.dtype)


def _sb(proj, *, tq=256, subs=2):
    b, t, _ = proj.shape
    rows = tq * subs
    n_heads = GROUP_WIDTH // HEAD_DIM
    wide = GROUP_WIDTH // LANES
    return pl.pallas_call(
        functools.partial(_sb_kernel, tq=tq),
        grid=(b, t // rows),
        in_specs=[pl.BlockSpec((1, rows, GROUP_WIDTH), lambda bb, i: (bb, i, COL_SQ // wide)),
                  pl.BlockSpec((1, t, GROUP_WIDTH), lambda bb, i: (bb, 0, COL_SK // wide)),
                  pl.BlockSpec((1, t, GROUP_WIDTH), lambda bb, i: (bb, 0, COL_SV // wide)),
                  pl.BlockSpec((1, rows, GROUP_WIDTH), lambda bb, i: (bb, i, COL_SG // wide))],
        out_specs=pl.BlockSpec((1, rows, GROUP_WIDTH), lambda bb, i: (bb, i, 0)),
        out_shape=jax.ShapeDtypeStruct((b, t, GROUP_WIDTH), _MXU_DTYPE),
        scratch_shapes=[pltpu.VMEM((n_heads // 2, t, LANES), _MXU_DTYPE),
                        pltpu.VMEM((n_heads, t, LANES), _MXU_DTYPE)],
        compiler_params=pltpu.CompilerParams(
            dimension_semantics=("arbitrary", "arbitrary"),
            vmem_limit_bytes=_VMEM_LIMIT),
        name="stickbreak",
    )(proj, proj, proj, proj)


def _swa_kernel(sink_ref, q_ref, k_ref, v_ref, g_ref, bias_ref, o_ref,
                k_sc, v_sc, *, tq):
    i = pl.program_id(1)
    t_len = k_ref.shape[1]
    w = SWA_WINDOW
    n_kv = k_sc.shape[0]

    @pl.when(i == 0)
    def _prep():
        kn = k_ref[0]
        v = v_ref[0]
        kn_swapped = pltpu.roll(kn, HEAD_DIM, axis=1)
        v_swapped = pltpu.roll(v, HEAD_DIM, axis=1)
        for kv in range(n_kv):
            keep = _own_half(kv, kn.shape)
            k_sc[kv, 0:w, :] = jnp.zeros((w, LANES), _MXU_DTYPE)
            v_sc[kv, 0:w, :] = jnp.zeros((w, LANES), _MXU_DTYPE)
            k_sc[kv, w:w + t_len, :] = jnp.where(keep, kn, kn_swapped).astype(_MXU_DTYPE)
            v_sc[kv, w:w + t_len, :] = jnp.where(keep, v, v_swapped).astype(_MXU_DTYPE)

    qi =lax.broadcasted_iota(jnp.int32, (w, 2 * w), 0)
    kj = lax.broadcasted_iota(jnp.int32, (w, 2 * w), 1)
    dist = qi + w - kj
    in_window = (dist >= 0) & (dist < w)
    n_sub = tq // w
    chains = [(u, kv, g) for u in range(n_sub) for kv in range(n_kv) for g in range(2)]
    rows = [pl.multiple_of((i * n_sub + u) * w, w) for u in range(n_sub)]
    ss = []
    for u, kv, g in chains:
        q_u = q_ref[0, u * w:(u + 1) * w, kv * LANES:(kv + 1) * LANES] * ATTN_SCALE
        q_h = jnp.where(_own_half(g, q_u.shape), q_u, 0.0).astype(_MXU_DTYPE)
        ss.append(lax.dot_general(q_h, k_sc[kv, pl.ds(rows[u], 2 * w), :], _NT,
                                  preferred_element_type=_F32))
    es, dens = [], []
    for (u, kv, g), s in zip(chains, ss):
        allowed = in_window & (kj + (i * n_sub + u - 1) * w >= 0)
        s = jnp.where(allowed, s + bias_ref[2 * kv + g], NEG_INF)
        sink = sink_ref[2 * kv + g]
        m = jnp.maximum(jnp.max(s, axis=-1, keepdims=True), sink)
        e = jnp.exp(s - m)
        dens.append(jnp.sum(e, axis=-1, keepdims=True) + jnp.exp(sink - m))
        es.append(e.astype(_MXU_DTYPE))
    outs = [jnp.dot(e, v_sc[kv, pl.ds(rows[u], 2 * w), :], preferred_element_type=_F32) / den
            for (u, kv, g), e, den in zip(chains, es, dens)]
    for u in range(n_sub):
        for kv in range(n_kv):
            c = chains.index((u, kv, 0))
            o = jnp.where(_own_half(0, (w, LANES)), outs[c], outs[c + 1])
            sl = (0, slice(u * w, (u + 1) * w), slice(kv * LANES, (kv + 1) * LANES))
            o_ref[sl] = (o * _silu(g_ref[sl])).astype(o_ref.dtype)


def _swa(proj, sinks, bias_tiles, *, tq=512):
    b, t, _ = proj.shape
    w = SWA_WINDOW
    n_heads = GROUP_WIDTH // HEAD_DIM
    wide = GROUP_WIDTH // LANES
    return pl.pallas_call(
        functools.partial(_swa_kernel, tq=tq),
        grid=(b, t // tq),
        in_specs=[pl.BlockSpec(memory_space=pltpu.SMEM),
                  pl.BlockSpec((1, tq, GROUP_WIDTH), lambda bb, i: (bb, i, COL_WQ // wide)),
                  pl.BlockSpec((1, t, LANES), lambda bb, i: (bb, 0, COL_WK)),
                  pl.BlockSpec((1, t, LANES), lambda bb, i: (bb, 0, COL_WV)),
                  pl.BlockSpec((1, tq, GROUP_WIDTH), lambda bb, i: (bb, i, COL_WG // wide)),
                  pl.BlockSpec((n_heads, w, 2 * w), lambda bb, i: (0, 0, 0))],
        out_specs=pl.BlockSpec((1, tq, GROUP_WIDTH), lambda bb, i: (bb, i, 0)),
        out_shape=jax.ShapeDtypeStruct((b, t, GROUP_WIDTH), _MXU_DTYPE),
        scratch_shapes=[pltpu.VMEM((n_heads // 2, t + w, LANES), _MXU_DTYPE),
                        pltpu.VMEM((n_heads // 2, t + w, LANES), _MXU_DTYPE)],
        compiler_params=pltpu.CompilerParams(
            dimension_semantics=("arbitrary", "arbitrary"),
            vmem_limit_bytes=_VMEM_LIMIT),
        name="swa",
    )(sinks, proj, proj, proj, proj, bias_tiles)


def _outproj_kernel(x_ref, ya_ref, yb_ref, yc_ref, yd_ref, w_ref, o_ref):
    acc = x_ref[...]
    for g, y_ref in enumerate((ya_ref, yb_ref, yc_ref, yd_ref)):
        acc = acc + jnp.dot(y_ref[...], w_ref[g * GROUP_WIDTH:(g + 1) * GROUP_WIDTH, :],
                            preferred_element_type=_F32)
    o_ref[...] = acc


def _outproj(x2d, ys, w_all, layer):
    m, d = x2d.shape
    tm = 512
    y_spec = pl.BlockSpec((tm, GROUP_WIDTH), lambda i: (i, 0))
    return pl.pallas_call(
        _outproj_kernel,
        grid=(m // tm,),
        in_specs=[pl.BlockSpec((tm, d), lambda i: (i, 0)), y_spec, y_spec, y_spec, y_spec,
                  pl.BlockSpec((None,) + w_all.shape[1:], lambda i: (layer, 0, 0))],
        out_specs=pl.BlockSpec((tm, d), lambda i: (i, 0)),
        out_shape=jax.ShapeDtypeStruct((m, d), _F32),
        compiler_params=pltpu.CompilerParams(
            dimension_semantics=("arbitrary",), vmem_limit_bytes=_VMEM_LIMIT),
        name="outproj",
    )(x2d, *ys, w_all)


def _rearrange_w_in(w_in):
    off = 3 * GROUP_WIDTH
    lead = (0,) * (w_in.ndim - 1)
    out = jnp.pad(w_in[..., off + FOX_FORGET_COLS:],
                  [(0, 0)] * (w_in.ndim - 1) + [(off, LANES)])
    assert out.shape[-1] == PROJ_WIDTH
    out = lax.dynamic_update_slice(out, w_in[..., :off], lead + (0,))
    return lax.dynamic_update_slice(out, w_in[..., off:off + FOX_FORGET_COLS],
                                    lead + (PROJ_WIDTH - LANES,))


def _qk_logit_bound(qk_gain):
    g = jnp.abs(qk_gain.astype(_F32))
    return ATTN_SCALE * HEAD_DIM * jnp.max(g[0]) * jnp.max(g[1])


def _qk_gain_row(fox_g, moba_g, swa_g):
    def tiles(g, n):
        return jnp.tile(g.astype(_F32), 2 * n)
    row = jnp.zeros((PROJ_WIDTH,), _F32)
    for col, g, n in ((COL_FQ, fox_g[0], 2), (COL_FK, fox_g[1], 2),
                      (COL_MQ, moba_g[0], 2), (COL_MK, moba_g[1], 2),
                      (COL_WQ, swa_g[0], 2), (COL_WK, swa_g[1], 1)):
        row = lax.dynamic_update_slice(row, tiles(g, n), (col * LANES,))
    return row[None, :]


def kernel(x, norm_gain, w_in, b_forget, fox_qk_gain, moba_qk_gain, swa_qk_gain, sinks, w_out, rel_bias):
    b, t, d = x.shape
    depth = w_in.shape[0]
    w_in_r = _rearrange_w_in(w_in.astype(_MXU_DTYPE))
    w_out_c = w_out.astype(_MXU_DTYPE)
    moba_wide_bias = _moba_wide_bias_tiles(rel_bias)
    swa_bias = _swa_bias_tiles(rel_bias)
    moba_bias_max = jnp.max(jnp.abs(rel_bias[:, :rel_bias.shape[1] // 2].astype(_F32)))
    x2d = x.reshape(b * t, d)
    for layer in range(depth):
        qk_gain_row = _qk_gain_row(fox_qk_gain[layer], moba_qk_gain[layer], swa_qk_gain[layer])
        proj = _inproj(x2d, norm_gain[layer][None, :], w_in_r, layer, qk_gain_row)
        proj = proj.reshape(b, t, PROJ_WIDTH)
        bf_row = jnp.pad(b_forget[layer], (0, LANES - FOX_FORGET_COLS))[None, :]
        fox_bound = _qk_logit_bound(fox_qk_gain[layer])
        y_fox = lax.cond(2.0 * fox_bound <= BOUNDED_SOFTMAX_RANGE,
                         lambda: _fox_bounded(proj, bf_row, fox_bound.reshape(1)),
                         lambda: _fox(proj, bf_row))
        moba_bound = _qk_logit_bound(moba_qk_gain[layer]) + moba_bias_max
        y_moba = lax.cond(2.0 * moba_bound <= BOUNDED_SOFTMAX_RANGE,
                          lambda: _moba_bounded(proj, rel_bias, moba_wide_bias,
                                                moba_bound.reshape(1)),
                          lambda: _moba(proj, rel_bias, _moba_bias_tiles(rel_bias)))
        y_sb = _sb(proj)
        y_swa = _swa(proj, sinks[layer], swa_bias)
        ys = [y.reshape(b * t, GROUP_WIDTH) for y in (y_fox, y_moba, y_sb, y_swa)]
        x2d = _outproj(x2d, ys, w_out_c, layer)
    return x2d.reshape(b, t, d)
```

```python
import functools
import math

import jax
import jax.numpy as jnp
from jax import lax
from jax.experimental import pallas as pl
from jax.experimental.pallas import tpu as pltpu

HEAD_DIM = 64
LANES = 128
GROUP_WIDTH = 256
MOBA_BLOCK = 256
MOBA_TOPK = 3
SWA_WINDOW = 128
NUM_BUCKETS = 32
REL_MAX_DISTANCE = 1024
RMS_EPS = 1e-6
NEG_INF = -1e30
ATTN_SCALE = HEAD_DIM ** -0.5
LOG2E = math.log2(math.e)
BOUNDED_SOFTMAX_RANGE = 60.0
MOBA_NEAR_TILES = REL_MAX_DISTANCE // MOBA_BLOCK + 1
MOBA_WIDE = 2 * MOBA_BLOCK
MOBA_WIDE_NEAR_TILES = REL_MAX_DISTANCE // MOBA_WIDE + 1
FOX_FORGET_COLS = 4
EXP_ZERO_CUTOFF = -104.0

COL_FQ, COL_FK, COL_FV, COL_FG = 0, 2, 4, 6
COL_MQ, COL_MK, COL_MV, COL_MG = 8, 10, 12, 14
COL_SQ, COL_SK, COL_SV, COL_SG = 16, 18, 20, 22
COL_WQ, COL_WK, COL_WV, COL_WG = 24, 26, 27, 28
COL_FF = 30
PROJ_WIDTH = (COL_FF + 1) * LANES
QK_NORM_COLS = (COL_FQ, COL_FQ + 1, COL_FK, COL_FK + 1, COL_MQ, COL_MQ + 1, COL_MK, COL_MK + 1,
                COL_WQ, COL_WQ + 1, COL_WK)

_MXU_DTYPE = jnp.bfloat16
_F32 = jnp.float32
_VMEM_LIMIT = 48 * 1024 * 1024
_VMEM_LIMIT_WIDE = 56 * 1024 * 1024

_NT = (((1,), (1,)), ((), ()))


def _lane_iota(shape):
    return lax.broadcasted_iota(jnp.int32, shape, len(shape) - 1)


def _pair_rms(x, gain_row):
    low = _lane_iota(x.shape) < HEAD_DIM
    sq = x * x
    ms_lo = jnp.sum(jnp.where(low, sq, 0.0), axis=-1, keepdims=True)
    ms_hi = jnp.sum(jnp.where(low, 0.0, sq), axis=-1, keepdims=True)
    ms = jnp.where(low, ms_lo, ms_hi) * (1.0 / HEAD_DIM)
    return x * lax.rsqrt(ms + RMS_EPS) * gain_row


def _silu(g):
    return g * (1.0 / (1.0 + jnp.exp(-g)))


def _split3(x):
    hi = x.astype(_MXU_DTYPE).astype(_F32)
    r = x - hi
    mid = r.astype(_MXU_DTYPE).astype(_F32)
    return hi, mid, r - mid


def _own_half(hh, shape):
    lane = _lane_iota(shape)
    return (lane < HEAD_DIM) if hh == 0 else (lane >= HEAD_DIM)


def _flash_update(s, v_tile, m, l, acc):
    m_new = jnp.maximum(m, jnp.max(s, axis=-1, keepdims=True))
    alpha = jnp.exp(m - m_new)
    p = jnp.exp(s - m_new)
    l = alpha * l + jnp.sum(p, axis=-1, keepdims=True)
    acc = alpha * acc + jnp.dot(p.astype(_MXU_DTYPE), v_tile, preferred_element_type=_F32)
    return m_new, l, acc


def _causal(tq):
    return (lax.broadcasted_iota(jnp.int32, (tq, tq), 0)
            >= lax.broadcasted_iota(jnp.int32, (tq, tq), 1))


def _inproj_kernel(x_ref, gain_ref, w_ref, qkg_ref, o_ref, *, tn):
    x = x_ref[...]
    ms = jnp.mean(x * x, axis=-1, keepdims=True)
    hn = (x * lax.rsqrt(ms + RMS_EPS) * gain_ref[...]).astype(_MXU_DTYPE)
    width = o_ref.shape[1]
    for n in range(pl.cdiv(width, tn)):
        hi = min((n + 1) * tn, width)
        acc = jnp.dot(hn, w_ref[:, n * tn:hi], preferred_element_type=_F32)
        for c in range(n * tn // LANES, hi // LANES):
            tile = acc[:, c * LANES - n * tn:(c + 1) * LANES - n * tn]
            if c in QK_NORM_COLS:
                tile = _pair_rms(tile, qkg_ref[:, c * LANES:(c + 1) * LANES])
            o_ref[:, c * LANES:(c + 1) * LANES] = tile


def _inproj(x2d, gain_row, w_all, layer, qk_gain_row):
    m, d = x2d.shape
    n = w_all.shape[2]
    tm = 512
    return pl.pallas_call(
        functools.partial(_inproj_kernel, tn=1024),
        grid=(m // tm,),
        in_specs=[pl.BlockSpec((tm, d), lambda i: (i, 0)),
                  pl.BlockSpec((1, d), lambda i: (0, 0)),
                  pl.BlockSpec((None, d, n), lambda i: (layer, 0, 0)),
                  pl.BlockSpec((1, n), lambda i: (0, 0))],
        out_specs=pl.BlockSpec((tm, n), lambda i: (i, 0)),
        out_shape=jax.ShapeDtypeStruct((m, n), _F32),
        compiler_params=pltpu.CompilerParams(
            dimension_semantics=("arbitrary",), vmem_limit_bytes=_VMEM_LIMIT),
        name="inproj",
    )(x2d, gain_row, w_all, qk_gain_row)


def _rel_bucket(dist):
    max_exact = NUM_BUCKETS // 2
    d = jnp.maximum(dist, 0)
    log_ratio = (jnp.log(jnp.maximum(d, 1).astype(_F32) / max_exact)
                 / math.log(REL_MAX_DISTANCE / max_exact))
    large = max_exact + (log_ratio * (NUM_BUCKETS - max_exact)).astype(jnp.int32)
    large = jnp.minimum(large, NUM_BUCKETS - 1)
    return jnp.where(d < max_exact, d, large)


def _bias_kernel(rb_ref, b_ref, o_ref, *, col0, scale):
    h = pl.program_id(0)
    for d in range(b_ref.shape[0]):
        buckets = b_ref[d]
        acc = jnp.zeros(buckets.shape, _F32)
        for k in range(NUM_BUCKETS):
            acc = jnp.where(buckets == k, rb_ref[k, col0 + h], acc)
        o_ref[0, d] = acc if scale == 1.0 else acc * scale


def _bias_lookup(rel_bias, buckets, col0, scale):
    n_heads = rel_bias.shape[1] // 2
    return pl.pallas_call(
        functools.partial(_bias_kernel, col0=col0, scale=scale),
        grid=(n_heads,),
        in_specs=[pl.BlockSpec(memory_space=pltpu.SMEM),
                  pl.BlockSpec(buckets.shape, lambda h: (0, 0, 0))],
        out_specs=pl.BlockSpec((1,) + buckets.shape, lambda h: (h, 0, 0, 0)),
        out_shape=jax.ShapeDtypeStruct((n_heads,) + buckets.shape, _F32),
        compiler_params=pltpu.CompilerParams(
            dimension_semantics=("arbitrary",), vmem_limit_bytes=_VMEM_LIMIT),
        name="bias_tiles",
    )(rel_bias, buckets)


def _toeplitz_buckets(size, count):
    i = jnp.arange(size)[:, None]
    j = jnp.arange(size)[None, :]
    return jnp.stack([_rel_bucket(d * size + i - j) for d in range(count)])


def _moba_bias_tiles(rel_bias):
    return _bias_lookup(rel_bias, _toeplitz_buckets(MOBA_BLOCK, MOBA_NEAR_TILES), 0, 1.0)


def _moba_wide_bias_tiles(rel_bias):
    return _bias_lookup(rel_bias, _toeplitz_buckets(MOBA_WIDE, MOBA_WIDE_NEAR_TILES), 0, LOG2E)


def _swa_bias_tiles(rel_bias):
    w = SWA_WINDOW
    buckets = _rel_bucket(jnp.arange(w)[:, None] + w - jnp.arange(2 * w)[None, :])
    return _bias_lookup(rel_bias, buckets[None], rel_bias.shape[1] // 2, 1.0)[:, 0]


def _forget_cumsum(ff_ref, bf_ref, c_sc):
    t_len = c_sc.shape[0]
    ff = ff_ref[0] + bf_ref[...]
    log_f = jnp.minimum(ff, 0.0) - jnp.log(1.0 + jnp.exp(-jnp.abs(ff)))
    parts = [part.astype(_MXU_DTYPE) for part in _split3(log_f)]
    ch = 256
    tri = (lax.broadcasted_iota(jnp.int32, (ch, ch), 0)
           >= lax.broadcasted_iota(jnp.int32, (ch, ch), 1)).astype(_MXU_DTYPE)
    carry = jnp.zeros((1, LANES), _F32)
    for r in range(t_len // ch):
        inc = carry
        for part in reversed(parts):
            inc = inc + jnp.dot(tri, part[r * ch:(r + 1) * ch], preferred_element_type=_F32)
        c_sc[r * ch:(r + 1) * ch, :] = inc
        carry = inc[ch - 1:ch, :]


def _lane_column(x, col):
    return jnp.sum(jnp.where(_lane_iota(x.shape) == col, x, 0.0), axis=-1, keepdims=True)


def _lane_fields(shape, a0, fields):
    lane = _lane_iota(shape)
    out = jnp.zeros(shape, _F32)
    for n, f in enumerate(fields):
        out = jnp.where(lane == a0 + n, f, out)
    return out


def _write_ones_column(vext_sc, h, hh):
    @pl.when(pl.program_id(0) == 0)
    def _():
        lane = _lane_iota((vext_sc.shape[1], LANES))
        vext_sc[h, :, LANES:2 * LANES] = jnp.where(lane == hh, 1.0, 0.0).astype(_MXU_DTYPE)


def _bounded_weights(ps_ref, q_augs, kaug_sc, r, bias, diagonal):
    tq = q_augs[0].shape[0]
    if not diagonal:
        for h in range(len(q_augs)):
            s = lax.dot_general(q_augs[h], kaug_sc[h, pl.ds(r, tq), :], _NT,
                                preferred_element_type=_F32)
            if bias is not None:
                s = s + bias(h)
            ps_ref[h] = jnp.exp2(s).astype(_MXU_DTYPE)
        return
    half = tq // 2
    rows = lax.broadcasted_iota(jnp.int32, (tq, half), 0)
    cols = lax.broadcasted_iota(jnp.int32, (tq, half), 1)
    for h in range(len(q_augs)):
        b_h = None if bias is None else bias(h)
        s = lax.dot_general(q_augs[h], kaug_sc[h, pl.ds(r, half), :], _NT,
                            preferred_element_type=_F32)
        if b_h is not None:
            s = s + b_h[:, 0:half]
        ps_ref[h, :, 0:half] = jnp.exp2(jnp.where(rows >= cols, s, NEG_INF)).astype(_MXU_DTYPE)
        s = lax.dot_general(q_augs[h][half:, :], kaug_sc[h, pl.ds(r + half, half), :], _NT,
                            preferred_element_type=_F32)
        if b_h is not None:
            s = s + b_h[half:, half:]
        ps_ref[h, 0:half, half:] = jnp.zeros((half, half), _MXU_DTYPE)
        ps_ref[h, half:, half:] = jnp.exp2(jnp.where(_causal(half), s, NEG_INF)).astype(_MXU_DTYPE)


def _bounded_values(ps_ref, vext_sc, r, acc_sc):
    tq = ps_ref.shape[1]
    for p in range(acc_sc.shape[0]):
        acc_sc[p] += (
            jnp.dot(ps_ref[2 * p], vext_sc[2 * p, pl.ds(r, tq), :], preferred_element_type=_F32)
            + jnp.dot(ps_ref[2 * p + 1], vext_sc[2 * p + 1, pl.ds(r, tq), :],
                      preferred_element_type=_F32))


def _bounded_store(acc_sc, g_ref, o_ref):
    tq = acc_sc.shape[1]
    for p in range(acc_sc.shape[0]):
        acc = acc_sc[p]
        den = acc[:, LANES:]
        l = jnp.where(_own_half(0, (tq, LANES)), _lane_column(den, 0), _lane_column(den, 1))
        cols = slice(p * LANES, (p + 1) * LANES)
        o_ref[0, :, cols] = (acc[:, :LANES] / l * _silu(g_ref[0, :, cols])).astype(o_ref.dtype)


def _bounded_pipeline_step(n, ps_bufs, weights, values):
    def new_in(slot):
        def run():
            weights(ps_bufs[slot])
            values(ps_bufs[1 - slot])
        return run
    lax.cond(n % 2 == 0, new_in(0), new_in(1))


def _bounded_finish(n_done, ps_bufs, values):
    lax.cond(n_done % 2 == 0, lambda: values(ps_bufs[0]), lambda: values(ps_bufs[1]))


def _fox_extra_base(h):
    p, hh = divmod(h, 2)
    return (HEAD_DIM if hh == 0 else 0) + 8 * p


def _fox_extra_lanes(fields):
    rows = max(f.shape[1] for fs in fields for f in fs if hasattr(f, "shape"))
    sub = lax.broadcasted_iota(jnp.int32, (8, rows), 0)
    blocks = {}
    for h, fs in enumerate(fields):
        blk = jnp.zeros((8, rows), _F32)
        for n, f in enumerate(fs):
            blk = jnp.where(sub == n, f, blk)
        blocks[_fox_extra_base(h)] = blk
    pieces, at = [], 0
    for base in sorted(blocks):
        if base > at:
            pieces.append(jnp.zeros((base - at, rows), _F32))
        pieces.append(blocks[base])
        at = base + 8
    pieces.append(jnp.zeros((LANES - at, rows), _F32))
    return jnp.concatenate(pieces, axis=0).T


def _fox_bounded_kernel(bound_ref, q_ref, k_ref, v_ref, ff_ref, g_ref, bf_ref, o_ref,
                        kaug_sc, vext_sc, c_sc, ps_a, ps_b, acc_sc):
    i = pl.program_id(1)
    tq = q_ref.shape[1]
    n_heads = kaug_sc.shape[0]

    @pl.when(i == 0)
    def _prep():
        _forget_cumsum(ff_ref, bf_ref, c_sc)
        hi, mid, lo = _split3(-(c_sc[...] * LOG2E).T[0:8, :])
        extra = _fox_extra_lanes([[hi[h:h + 1], mid[h:h + 1], lo[h:h + 1], 1.0, 1.0, 1.0, 1.0, 1.0]
                                  for h in range(n_heads)])
        lane = _lane_iota(extra.shape)
        for h in range(n_heads):
            p, hh = divmod(h, 2)
            kn = k_ref[0, :, p * LANES:(p + 1) * LANES]
            v = v_ref[0, :, p * LANES:(p + 1) * LANES]
            a0 = _fox_extra_base(h)
            aug = jnp.where((lane >= a0) & (lane < a0 + 8), extra, 0.0)
            own = _own_half(hh, kn.shape)
            kaug_sc[h] = jnp.where(own, kn, aug).astype(_MXU_DTYPE)
            vext_sc[h, :, 0:LANES] = jnp.where(own, v, 0.0).astype(_MXU_DTYPE)
            _write_ones_column(vext_sc, h, hh)

    row0 = pl.multiple_of(i * tq, tq)
    off = jnp.full((1, 1), -LOG2E, _F32) * bound_ref[0]
    off_hi = off.astype(_MXU_DTYPE).astype(_F32)
    hi, mid, lo = _split3((c_sc[pl.ds(row0, tq), :] * LOG2E).T[0:8, :])
    extra = _fox_extra_lanes([[1.0, 1.0, 1.0, hi[h:h + 1], mid[h:h + 1], lo[h:h + 1],
                               off_hi, off - off_hi] for h in range(n_heads)])
    lane = _lane_iota(extra.shape)
    q_augs = []
    for h in range(n_heads):
        p, hh = divmod(h, 2)
        q = q_ref[0, :, p * LANES:(p + 1) * LANES] * (ATTN_SCALE * LOG2E)
        a0 = _fox_extra_base(h)
        aug = jnp.where((lane >= a0) & (lane < a0 + 8), extra, 0.0)
        q_augs.append(jnp.where(_own_half(hh, q.shape), q, aug).astype(_MXU_DTYPE))

    head_lanes = _lane_iota((1, LANES)) < n_heads
    c_first = c_sc[pl.ds(row0, 1), :]
    cutoff = EXP_ZERO_CUTOFF - 2.0 * bound_ref[0]

    def live(j):
        c_last = c_sc[pl.ds(jnp.maximum(j, 0) * tq + (tq - 1), 1), :]
        gap = jnp.max(jnp.where(head_lanes, c_first - c_last, -jnp.inf))
        return (j >= 0) & (gap >= cutoff)

    ps_bufs = (ps_a, ps_b)

    def values_at(r):
        return lambda ps_ref: _bounded_values(ps_ref, vext_sc, pl.multiple_of(r, tq), acc_sc)

    def body(state):
        j, _, r_prev, n = state
        r = pl.multiple_of(j * tq, tq)
        _bounded_pipeline_step(
            n, ps_bufs,
            lambda ps_ref: _bounded_weights(ps_ref, q_augs, kaug_sc, r, None, False),
            values_at(r_prev))
        return j - 1, live(j - 1), r, n + 1

    acc_sc[...] = jnp.zeros(acc_sc.shape, _F32)
    _bounded_weights(ps_a, q_augs, kaug_sc, row0, None, True)
    state = lax.while_loop(lambda state: state[1], body,
                           (i - 1, live(i - 1), row0, jnp.int32(1)))
    _bounded_finish(state[3] - 1, ps_bufs, values_at(state[2]))
    _bounded_store(acc_sc, g_ref, o_ref)


def _fox_bounded(proj, bf_row, bound, *, tq=512):
    b, t, _ = proj.shape
    n_heads = GROUP_WIDTH // HEAD_DIM
    wide = GROUP_WIDTH // LANES
    return pl.pallas_call(
        _fox_bounded_kernel,
        grid=(b, t // tq),
        in_specs=[pl.BlockSpec(memory_space=pltpu.SMEM),
                  pl.BlockSpec((1, tq, GROUP_WIDTH), lambda bb, i: (bb, i, COL_FQ // wide)),
                  pl.BlockSpec((1, t, GROUP_WIDTH), lambda bb, i: (bb, 0, COL_FK // wide)),
                  pl.BlockSpec((1, t, GROUP_WIDTH), lambda bb, i: (bb, 0, COL_FV // wide)),
                  pl.BlockSpec((1, t, LANES), lambda bb, i: (bb, 0, COL_FF)),
                  pl.BlockSpec((1, tq, GROUP_WIDTH), lambda bb, i: (bb, i, COL_FG // wide)),
                  pl.BlockSpec((1, LANES), lambda bb, i: (0, 0))],
        out_specs=pl.BlockSpec((1, tq, GROUP_WIDTH), lambda bb, i: (bb, i, 0)),
        out_shape=jax.ShapeDtypeStruct((b, t, GROUP_WIDTH), _MXU_DTYPE),
        scratch_shapes=[pltpu.VMEM((n_heads, t, LANES), _MXU_DTYPE),
                        pltpu.VMEM((n_heads, t, 2 * LANES), _MXU_DTYPE),
                        pltpu.VMEM((t, LANES), _F32),
                        pltpu.VMEM((n_heads, tq, tq), _MXU_DTYPE),
                        pltpu.VMEM((n_heads, tq, tq), _MXU_DTYPE),
                        pltpu.VMEM((n_heads // 2, tq, 2 * LANES), _F32)],
        compiler_params=pltpu.CompilerParams(
            dimension_semantics=("arbitrary", "arbitrary"),
            vmem_limit_bytes=_VMEM_LIMIT_WIDE),
        name="fox_bounded",
    )(bound, proj, proj, proj, proj, proj, bf_row)


def _fox_kernel(q_ref, k_ref, v_ref, ff_ref, g_ref, bf_ref, o_ref,
                kaug_sc, v_sc, c_sc, *, tq):
    p = pl.program_id(1)
    i = pl.program_id(2)

    def head_column(x, hh):
        return _lane_column(x, 2 * p + hh)

    @pl.when(i == 0)
    def _prep():
        pl.when(p == 0)(lambda: _forget_cumsum(ff_ref, bf_ref, c_sc))
        kn = k_ref[0]
        c_all = c_sc[...]
        lane = _lane_iota(kn.shape)
        for hh in range(2):
            a0 = HEAD_DIM if hh == 0 else 0
            hi, mid, lo = _split3(-head_column(c_all, hh))
            aug = jnp.where(lane == a0, hi,
                            jnp.where(lane == a0 + 1, mid,
                                      jnp.where(lane == a0 + 2, lo,
                                                jnp.where((lane >= a0 + 3) & (lane < a0 + 6), 1.0, 0.0))))
            kaug_sc[hh] = jnp.where(_own_half(hh, kn.shape), kn, aug).astype(_MXU_DTYPE)
        v_sc[...] = v_ref[0].astype(_MXU_DTYPE)

    q = q_ref[0] * ATTN_SCALE
    row0 = pl.multiple_of(i * tq, tq)
    c_t = c_sc[pl.ds(row0, tq), :]
    lane = _lane_iota(q.shape)
    outs = []
    for hh in range(2):
        a0 = HEAD_DIM if hh == 0 else 0
        hi, mid, lo = _split3(head_column(c_t, hh))
        aug = jnp.where((lane >= a0) & (lane < a0 + 3), 1.0,
                        jnp.where(lane == a0 + 3, hi,
                                  jnp.where(lane == a0 + 4, mid,
                                            jnp.where(lane == a0 + 5, lo, 0.0))))
        q_aug = jnp.where(_own_half(hh, q.shape), q, aug).astype(_MXU_DTYPE)

        s = lax.dot_general(q_aug, kaug_sc[hh, pl.ds(row0, tq), :], _NT,
                            preferred_element_type=_F32)
        s = jnp.where(_causal(tq), s, NEG_INF)
        m = jnp.max(s, axis=-1, keepdims=True)
        pr = jnp.exp(s - m)
        l = jnp.sum(pr, axis=-1, keepdims=True)
        acc = jnp.dot(pr.astype(_MXU_DTYPE), v_sc[pl.ds(row0, tq), :], preferred_element_type=_F32)

        def body(j, carry, q_aug=q_aug, hh=hh):
            r = pl.multiple_of(j * tq, tq)
            s = lax.dot_general(q_aug, kaug_sc[hh, pl.ds(r, tq), :], _NT,
                                preferred_element_type=_F32)
            return _flash_update(s, v_sc[pl.ds(r, tq), :], *carry)

        m, l, acc = lax.fori_loop(0, i, body, (m, l, acc))
        outs.append(acc / l)
    o = jnp.where(_own_half(0, outs[0].shape), outs[0], outs[1])
    o_ref[0] = (o * _silu(g_ref[0])).astype(o_ref.dtype)


def _fox(proj, bf_row, *, tq=256):
    b, t, _ = proj.shape
    return pl.pallas_call(
        functools.partial(_fox_kernel, tq=tq),
        grid=(b, 2, t // tq),
        in_specs=[pl.BlockSpec((1, tq, LANES), lambda bb, p, i: (bb, i, COL_FQ + p)),
                  pl.BlockSpec((1, t, LANES), lambda bb, p, i: (bb, 0, COL_FK + p)),
                  pl.BlockSpec((1, t, LANES), lambda bb, p, i: (bb, 0, COL_FV + p)),
                  pl.BlockSpec((1, t, LANES), lambda bb, p, i: (bb, 0, COL_FF)),
                  pl.BlockSpec((1, tq, LANES), lambda bb, p, i: (bb, i, COL_FG + p)),
                  pl.BlockSpec((1, LANES), lambda bb, p, i: (0, 0))],
        out_specs=pl.BlockSpec((1, tq, LANES), lambda bb, p, i: (bb, i, p)),
        out_shape=jax.ShapeDtypeStruct((b, t, GROUP_WIDTH), _MXU_DTYPE),
        scratch_shapes=[pltpu.VMEM((2, t, LANES), _MXU_DTYPE),
                        pltpu.VMEM((t, LANES), _MXU_DTYPE),
                        pltpu.VMEM((t, LANES), _F32)],
        compiler_params=pltpu.CompilerParams(
            dimension_semantics=("arbitrary", "arbitrary", "arbitrary"),
            vmem_limit_bytes=_VMEM_LIMIT),
        name="fox",
    )(proj, proj, proj, proj, proj, bf_row)


def _moba_block_means(kn, kmean_sc):
    kmean_sc[...] = jnp.zeros(kmean_sc.shape, _F32)
    for n in range(kn.shape[0] // MOBA_BLOCK):
        kmean_sc[n:n + 1, :] = jnp.mean(kn[n * MOBA_BLOCK:(n + 1) * MOBA_BLOCK], axis=0, keepdims=True)


def _moba_select(q_head, kmean, past):
    gate = lax.dot_general(q_head, kmean, _NT, precision=lax.Precision.HIGHEST,
                           preferred_element_type=_F32)
    lane_f = _lane_iota(gate.shape).astype(_F32)
    cand = jnp.where(past, gate, -jnp.inf)
    sel = jnp.zeros(gate.shape, _F32)
    for _ in range(MOBA_TOPK):
        mx = jnp.max(cand, axis=-1, keepdims=True)
        is_max = (cand == mx) & (mx > -jnp.inf)
        first = jnp.min(jnp.where(is_max, lane_f, float(LANES)), axis=-1, keepdims=True)
        pick = lane_f == first
        sel = jnp.where(pick, 1.0, sel)
        cand = jnp.where(pick, -jnp.inf, cand)
    return sel


def _moba_bounded_kernel(rb_ref, bound_ref, q_ref, k_ref, v_ref, g_ref, bias_ref, o_ref,
                         kaug_sc, vext_sc, kmean_sc, ps_a, ps_b, acc_sc):
    i = pl.program_id(1)
    tq = q_ref.shape[1]
    n_heads = kaug_sc.shape[0]
    blk = MOBA_BLOCK
    near = MOBA_WIDE_NEAR_TILES

    @pl.when(i == 0)
    def _prep():
        lane = _lane_iota((k_ref.shape[1], LANES))
        row_blk = lax.broadcasted_iota(jnp.int32, lane.shape, 0) // blk
        for h in range(n_heads):
            p, hh = divmod(h, 2)
            kn = k_ref[0, :, p * LANES:(p + 1) * LANES]
            v = v_ref[0, :, p * LANES:(p + 1) * LANES]
            if hh == 0:
                _moba_block_means(kn, kmean_sc.at[p])
            a0 = HEAD_DIM if hh == 0 else 0
            aug = jnp.where((lane - a0 == row_blk) | (lane - a0 - 16 == row_blk)
                            | (lane - a0 == 32) | (lane - a0 == 33), 1.0, 0.0)
            own = _own_half(hh, kn.shape)
            kaug_sc[h] = jnp.where(own, kn, aug).astype(_MXU_DTYPE)
            vext_sc[h, :, 0:LANES] = jnp.where(own, v, 0.0).astype(_MXU_DTYPE)
            _write_ones_column(vext_sc, h, hh)

    row0 = pl.multiple_of(i * tq, tq)
    off = jnp.full((1, 1), -LOG2E, _F32) * bound_ref[0]
    off_hi = off.astype(_MXU_DTYPE).astype(_F32)
    blk_n = lax.broadcasted_iota(jnp.int32, (16, tq), 0)
    blk_f = blk_n.astype(_F32)
    q_blk = i * (tq // blk) + lax.broadcasted_iota(jnp.int32, (16, tq), 1) // blk
    past = blk_n < q_blk
    far = i - blk_n // 2 >= near
    sub8 = lax.broadcasted_iota(jnp.int32, (8, tq), 0)
    off_rows = jnp.where(sub8 == 0, off_hi, jnp.where(sub8 == 1, off - off_hi, 0.0))
    q_augs = []
    for h in range(n_heads):
        p, hh = divmod(h, 2)
        qn = q_ref[0, :, p * LANES:(p + 1) * LANES]
        own = _own_half(hh, qn.shape)
        gate = lax.dot_general(kmean_sc[p, 0:16, :], jnp.where(own, qn, 0.0), _NT,
                               precision=lax.Precision.HIGHEST, preferred_element_type=_F32)
        cand = jnp.where(past, gate, -jnp.inf)
        sel = jnp.zeros((16, tq), _F32)
        for _ in range(MOBA_TOPK):
            mx = jnp.max(cand, axis=0, keepdims=True)
            is_max = (cand == mx) & (mx > -jnp.inf)
            first = jnp.min(jnp.where(is_max, blk_f, float(LANES)), axis=0, keepdims=True)
            pick = blk_f == first
            sel = jnp.where(pick, 1.0, sel)
            cand = jnp.where(pick, -jnp.inf, cand)
        c_far = jnp.full((1, 1), LOG2E, _F32) * rb_ref[NUM_BUCKETS - 1, h]
        far_hi = c_far.astype(_MXU_DTYPE).astype(_F32)
        pen = jnp.where((blk_n == q_blk) | (past & (sel != 0.0)), 0.0, NEG_INF)
        fields = jnp.concatenate([pen + jnp.where(far, far_hi, 0.0),
                                  jnp.where(far, c_far - far_hi, 0.0),
                                  off_rows,
                                  jnp.zeros((HEAD_DIM - 40, tq), _F32)], axis=0)
        blank = jnp.zeros((HEAD_DIM, tq), _F32)
        aug = jnp.concatenate([blank, fields] if hh == 0 else [fields, blank], axis=0).T
        q_augs.append(jnp.where(own, qn * (ATTN_SCALE * LOG2E), aug).astype(_MXU_DTYPE))

    ps_bufs = (ps_a, ps_b)

    def values_at(r):
        return lambda ps_ref: _bounded_values(ps_ref, vext_sc, pl.multiple_of(r, tq), acc_sc)

    def body(j, state, with_bias):
        r_prev, n = state
        r = pl.multiple_of(j * tq, tq)
        bias = (lambda h: bias_ref[h, i - j]) if with_bias else None
        _bounded_pipeline_step(
            n, ps_bufs,
            lambda ps_ref: _bounded_weights(ps_ref, q_augs, kaug_sc, r, bias, False),
            values_at(r_prev))
        return r, n + 1

    acc_sc[...] = jnp.zeros(acc_sc.shape, _F32)
    _bounded_weights(ps_a, q_augs, kaug_sc, row0, lambda h: bias_ref[h, 0], True)
    state = (row0, jnp.int32(1))
    near_lo = jnp.maximum(i - (near - 1), 0)
    state = lax.fori_loop(near_lo, i, functools.partial(body, with_bias=True), state)
    state = lax.fori_loop(0, near_lo, functools.partial(body, with_bias=False), state)
    _bounded_finish(state[1] - 1, ps_bufs, values_at(state[0]))
    _bounded_store(acc_sc, g_ref, o_ref)


def _moba_bounded(proj, rel_bias, bias_tiles, bound):
    b, t, _ = proj.shape
    tq = MOBA_WIDE
    assert t % tq == 0 and t // MOBA_BLOCK <= 16
    n_heads = GROUP_WIDTH // HEAD_DIM
    wide = GROUP_WIDTH // LANES
    resident = pl.Buffered(1)
    return pl.pallas_call(
        _moba_bounded_kernel,
        grid=(b, t // tq),
        in_specs=[pl.BlockSpec(memory_space=pltpu.SMEM),
                  pl.BlockSpec(memory_space=pltpu.SMEM),
                  pl.BlockSpec((1, tq, GROUP_WIDTH), lambda bb, i: (bb, i, COL_MQ // wide)),
                  pl.BlockSpec((1, t, GROUP_WIDTH), lambda bb, i: (bb, 0, COL_MK // wide),
                               pipeline_mode=resident),
                  pl.BlockSpec((1, t, GROUP_WIDTH), lambda bb, i: (bb, 0, COL_MV // wide),
                               pipeline_mode=resident),
                  pl.BlockSpec((1, tq, GROUP_WIDTH), lambda bb, i: (bb, i, COL_MG // wide)),
                  pl.BlockSpec((n_heads, MOBA_WIDE_NEAR_TILES, tq, tq), lambda bb, i: (0, 0, 0, 0),
                               pipeline_mode=resident)],
        out_specs=pl.BlockSpec((1, tq, GROUP_WIDTH), lambda bb, i: (bb, i, 0)),
        out_shape=jax.ShapeDtypeStruct((b, t, GROUP_WIDTH), _MXU_DTYPE),
        scratch_shapes=[pltpu.VMEM((n_heads, t, LANES), _MXU_DTYPE),
                        pltpu.VMEM((n_heads, t, 2 * LANES), _MXU_DTYPE),
                        pltpu.VMEM((n_heads // 2, LANES, LANES), _F32),
                        pltpu.VMEM((n_heads, tq, tq), _MXU_DTYPE),
                        pltpu.VMEM((n_heads, tq, tq), _MXU_DTYPE),
                        pltpu.VMEM((n_heads // 2, tq, 2 * LANES), _F32)],
        compiler_params=pltpu.CompilerParams(
            dimension_semantics=("arbitrary", "arbitrary"),
            vmem_limit_bytes=_VMEM_LIMIT_WIDE),
        name="moba_bounded",
    )(rel_bias, bound, proj, proj, proj, proj, bias_tiles)


def _moba_kernel(rb_ref, q_ref, k_ref, v_ref, g_ref, bias_ref, o_ref,
                 kaug_sc, v_sc, kmean_sc):
    p = pl.program_id(1)
    i = pl.program_id(2)
    t_len = k_ref.shape[1]
    blk = MOBA_BLOCK
    nblk = t_len // blk
    near = MOBA_NEAR_TILES

    @pl.when(i == 0)
    def _prep():
        kn = k_ref[0]
        _moba_block_means(kn, kmean_sc)
        lane = _lane_iota(kn.shape)
        row_blk = lax.broadcasted_iota(jnp.int32, kn.shape, 0) // blk
        for hh in range(2):
            a0 = HEAD_DIM if hh == 0 else 0
            onehot = jnp.where((lane - a0 == row_blk) | (lane - a0 - 16 == row_blk), 1.0, 0.0)
            kaug_sc[hh] = jnp.where(_own_half(hh, kn.shape), kn, onehot).astype(_MXU_DTYPE)
        v_sc[...] = v_ref[0].astype(_MXU_DTYPE)

    qn = q_ref[0]
    row0 = pl.multiple_of(i * blk, blk)
    lane = _lane_iota(qn.shape)
    lane_f = lane.astype(_F32)
    past = lane < i
    outs = []
    for hh in range(2):
        own = _own_half(hh, qn.shape)
        sel = _moba_select(jnp.where(own, qn, 0.0), kmean_sc[...], past)
        c_far = jnp.full((1, LANES), rb_ref[NUM_BUCKETS - 1, 2 * p + hh], _F32)
        far_hi = c_far.astype(_MXU_DTYPE).astype(_F32)
        far_lo = c_far - far_hi
        pen = jnp.where(past & (sel == 0.0), NEG_INF, 0.0)
        aug = jnp.where(lane < 16, pen + jnp.where(i - lane >= near, far_hi, 0.0),
                        jnp.where((lane < 32) & (i - (lane - 16) >= near), far_lo, 0.0))
        if hh == 0:
            aug = pltpu.roll(aug, HEAD_DIM, axis=1)
        q_aug = jnp.where(own, qn * ATTN_SCALE, aug).astype(_MXU_DTYPE)

        def scores(r, q_aug=q_aug, hh=hh):
            return lax.dot_general(q_aug, kaug_sc[hh, pl.ds(r, blk), :], _NT,
                                   preferred_element_type=_F32)

        s = scores(row0) + bias_ref[hh, 0]
        s = jnp.where(_causal(blk), s, NEG_INF)
        m = jnp.max(s, axis=-1, keepdims=True)
        pr = jnp.exp(s - m)
        l = jnp.sum(pr, axis=-1, keepdims=True)
        acc = jnp.dot(pr.astype(_MXU_DTYPE), v_sc[pl.ds(row0, blk), :], preferred_element_type=_F32)

        def near_body(j, carry, hh=hh, scores=scores):
            r = pl.multiple_of(j * blk, blk)
            return _flash_update(scores(r) + bias_ref[hh, i - j], v_sc[pl.ds(r, blk), :], *carry)

        def far_body(j, carry, scores=scores):
            r = pl.multiple_of(j * blk, blk)
            return _flash_update(scores(r), v_sc[pl.ds(r, blk), :], *carry)

        near_lo = jnp.maximum(i - (near - 1), 0)
        carry = lax.fori_loop(near_lo, i, near_body, (m, l, acc))
        m, l, acc = lax.fori_loop(0, near_lo, far_body, carry)
        outs.append(acc / l)
    o = jnp.where(_own_half(0, outs[0].shape), outs[0], outs[1])
    o_ref[0] = (o * _silu(g_ref[0])).astype(o_ref.dtype)


def _moba(proj, rel_bias, bias_tiles):
    b, t, _ = proj.shape
    blk = MOBA_BLOCK
    assert t % blk == 0 and t // blk <= 16
    return pl.pallas_call(
        _moba_kernel,
        grid=(b, 2, t // blk),
        in_specs=[pl.BlockSpec(memory_space=pltpu.SMEM),
                  pl.BlockSpec((1, blk, LANES), lambda bb, p, i: (bb, i, COL_MQ + p)),
                  pl.BlockSpec((1, t, LANES), lambda bb, p, i: (bb, 0, COL_MK + p)),
                  pl.BlockSpec((1, t, LANES), lambda bb, p, i: (bb, 0, COL_MV + p)),
                  pl.BlockSpec((1, blk, LANES), lambda bb, p, i: (bb, i, COL_MG + p)),
                  pl.BlockSpec((2, MOBA_NEAR_TILES, blk, blk), lambda bb, p, i: (p, 0, 0, 0))],
        out_specs=pl.BlockSpec((1, blk, LANES), lambda bb, p, i: (bb, i, p)),
        out_shape=jax.ShapeDtypeStruct((b, t, GROUP_WIDTH), _MXU_DTYPE),
        scratch_shapes=[pltpu.VMEM((2, t, LANES), _MXU_DTYPE),
                        pltpu.VMEM((t, LANES), _MXU_DTYPE),
                        pltpu.VMEM((LANES, LANES), _F32)],
        compiler_params=pltpu.CompilerParams(
            dimension_semantics=("arbitrary", "arbitrary", "arbitrary"),
            vmem_limit_bytes=_VMEM_LIMIT),
        name="moba",
    )(rel_bias, proj, proj, proj, proj, bias_tiles)


def _sb_kernel(q_ref, k_ref, v_ref, g_ref, o_ref, k_sc, v_sc, *, tq):
    i = pl.program_id(1)
    n_heads = v_sc.shape[0]

    @pl.when(i == 0)
    def _prep():
        for p in range(n_heads // 2):
            k_sc[p] = k_ref[0, :, p * LANES:(p + 1) * LANES].astype(_MXU_DTYPE)
            v = v_ref[0, :, p * LANES:(p + 1) * LANES]
            for hh in range(2):
                v_sc[2 * p + hh] = jnp.where(_own_half(hh, v.shape), v, 0.0).astype(_MXU_DTYPE)

    subs = q_ref.shape[1] // tq
    q_heads = []
    for u in range(subs):
        q_heads.append([])
        for h in range(n_heads):
            p, hh = divmod(h, 2)
            q = q_ref[0, u * tq:(u + 1) * tq, p * LANES:(p + 1) * LANES] * (ATTN_SCALE * LOG2E)
            q_heads[u].append(jnp.where(_own_half(hh, q.shape), q, 0.0).astype(_MXU_DTYPE))
    strict = (lax.broadcasted_iota(jnp.int32, (tq, tq), 0)
              > lax.broadcasted_iota(jnp.int32, (tq, tq), 1))
    after = strict.astype(_MXU_DTYPE)
    sign_bit = jnp.uint32(0x80000000)

    def pair(back, runs, diagonal):
        chains = [(u, t, h) for u in range(subs) for t in range(2) for h in range(n_heads)]
        tile_idx = [[subs * i + u - back - t for t in range(2)] for u in range(subs)]
        valid = [[(j >= 0).astype(_F32) for j in js] for js in tile_idx]
        rows = [[pl.multiple_of(jnp.maximum(j, 0) * tq, tq) for j in js] for js in tile_idx]
        masked = [diagonal and t == 0 for _, t, _ in chains]
        zs = [lax.dot_general(q_heads[u][h], k_sc[h // 2, pl.ds(rows[u][t], tq), :], _NT,
                              preferred_element_type=_F32) for u, t, h in chains]
        drops = []
        for z, msk in zip(zs, masked):
            neg_abs = lax.bitcast_convert_type(lax.bitcast_convert_type(z, jnp.uint32) | sign_bit, _F32)
            drop = jnp.maximum(z, 0.0) + jnp.log(1.0 + jnp.exp2(neg_abs)) * LOG2E
            drops.append(jnp.where(strict, drop, 0.0) if msk else drop)
        laters = []
        for drop in drops:
            hi = drop.astype(_MXU_DTYPE)
            lo = (drop - hi.astype(_F32)).astype(_MXU_DTYPE)
            laters.append(jnp.dot(hi, after, preferred_element_type=_F32)
                          + jnp.dot(lo, after, preferred_element_type=_F32))
        pvs = []
        for (u, t, h), z, drop, later, msk in zip(chains, zs, drops, laters, masked):
            w = jnp.exp2(z - drop - later)
            if msk:
                w = jnp.where(strict, w, 0.0)
            pvs.append(jnp.dot(w.astype(_MXU_DTYPE), v_sc[h, pl.ds(rows[u][t], tq), :],
                               preferred_element_type=_F32))
        sums = [jnp.sum(drop, axis=-1, keepdims=True) for drop in drops]
        new_runs, outs = [], []
        for u in range(subs):
            new_runs.append([])
            for p in range(n_heads // 2):
                out = None
                for h in (2 * p, 2 * p + 1):
                    run = runs[u][h]
                    for t in range(2):
                        c = chains.index((u, t, h))
                        pv = pvs[c] * (jnp.exp2(run) * valid[u][t])
                        out = pv if out is None else out + pv
                        run = run - sums[c] * valid[u][t]
                    new_runs[u].append(run)
                outs.append(out)
        return new_runs, outs

    def alive(runs):
        top = functools.reduce(jnp.maximum, [r for rs in runs for r in rs])
        return jnp.max(top) >= EXP_ZERO_CUTOFF * LOG2E

    def flat(runs):
        return [r for rs in runs for r in rs]

    def nested(flat_runs):
        return [flat_runs[n_heads * u:n_heads * (u + 1)] for u in range(subs)]

    n_runs = n_heads * subs
    zero = jnp.zeros((tq, 1), _F32)
    runs, accs = pair(0, [[zero] * n_heads for _ in range(subs)], True)

    def body(state):
        back = state[0]
        runs, outs = pair(back, nested(list(state[2:2 + n_runs])), False)
        accs = [a + o for a, o in zip(state[2 + n_runs:], outs)]
        return (back + 2, alive(runs), *flat(runs), *accs)

    state = lax.while_loop(lambda state: (subs * i + subs - 1 - state[0] >= 0) & state[1], body,
                           (jnp.int32(2), alive(runs), *flat(runs), *accs))
    accs = state[2 + n_runs:]
    for u in range(subs):
        for p in range(n_heads // 2):
            sl = (0, slice(u * tq, (u + 1) * tq), slice(p * LANES, (p + 1) * LANES))
            o_ref[sl] = (accs[u * (n_heads // 2) + p] * _silu(g_ref[sl])).astype(o_ref.dtype)


def _sb(proj, *, tq=256, subs=2):
    b, t, _ = proj.shape
    rows = tq * subs
    n_heads = GROUP_WIDTH // HEAD_DIM
    wide = GROUP_WIDTH // LANES
    return pl.pallas_call(
        functools.partial(_sb_kernel, tq=tq),
        grid=(b, t // rows),
        in_specs=[pl.BlockSpec((1, rows, GROUP_WIDTH), lambda bb, i: (bb, i, COL_SQ // wide)),
                  pl.BlockSpec((1, t, GROUP_WIDTH), lambda bb, i: (bb, 0, COL_SK // wide)),
                  pl.BlockSpec((1, t, GROUP_WIDTH), lambda bb, i: (bb, 0, COL_SV // wide)),
                  pl.BlockSpec((1, rows, GROUP_WIDTH), lambda bb, i: (bb, i, COL_SG // wide))],
        out_specs=pl.BlockSpec((1, rows, GROUP_WIDTH), lambda bb, i: (bb, i, 0)),
        out_shape=jax.ShapeDtypeStruct((b, t, GROUP_WIDTH), _MXU_DTYPE),
        scratch_shapes=[pltpu.VMEM((n_heads // 2, t, LANES), _MXU_DTYPE),
                        pltpu.VMEM((n_heads, t, LANES), _MXU_DTYPE)],
        compiler_params=pltpu.CompilerParams(
            dimension_semantics=("arbitrary", "arbitrary"),
            vmem_limit_bytes=_VMEM_LIMIT),
        name="stickbreak",
    )(proj, proj, proj, proj)


def _swa_kernel(sink_ref, q_ref, k_ref, v_ref, g_ref, bias_ref, o_ref,
                k_sc, v_sc, *, tq):
    i = pl.program_id(1)
    t_len = k_ref.shape[1]
    w = SWA_WINDOW
    n_kv = k_sc.shape[0]

    @pl.when(i == 0)
    def _prep():
        kn = k_ref[0]
        v = v_ref[0]
        kn_swapped = pltpu.roll(kn, HEAD_DIM, axis=1)
        v_swapped = pltpu.roll(v, HEAD_DIM, axis=1)
        for kv in range(n_kv):
            keep = _own_half(kv, kn.shape)
            k_sc[kv, 0:w, :] = jnp.zeros((w, LANES), _MXU_DTYPE)
            v_sc[kv, 0:w, :] = jnp.zeros((w, LANES), _MXU_DTYPE)
            k_sc[kv, w:w + t_len, :] = jnp.where(keep, kn, kn_swapped).astype(_MXU_DTYPE)
            v_sc[kv, w:w + t_len, :] = jnp.where(keep, v, v_swapped).astype(_MXU_DTYPE)

    qi =lax.broadcasted_iota(jnp.int32, (w, 2 * w), 0)
    kj = lax.broadcasted_iota(jnp.int32, (w, 2 * w), 1)
    dist = qi + w - kj
    in_window = (dist >= 0) & (dist < w)
    n_sub = tq // w
    chains = [(u, kv, g) for u in range(n_sub) for kv in range(n_kv) for g in range(2)]
    rows = [pl.multiple_of((i * n_sub + u) * w, w) for u in range(n_sub)]
    ss = []
    for u, kv, g in chains:
        q_u = q_ref[0, u * w:(u + 1) * w, kv * LANES:(kv + 1) * LANES] * ATTN_SCALE
        q_h = jnp.where(_own_half(g, q_u.shape), q_u, 0.0).astype(_MXU_DTYPE)
        ss.append(lax.dot_general(q_h, k_sc[kv, pl.ds(rows[u], 2 * w), :], _NT,
                                  preferred_element_type=_F32))
    es, dens = [], []
    for (u, kv, g), s in zip(chains, ss):
        allowed = in_window & (kj + (i * n_sub + u - 1) * w >= 0)
        s = jnp.where(allowed, s + bias_ref[2 * kv + g], NEG_INF)
        sink = sink_ref[2 * kv + g]
        m = jnp.maximum(jnp.max(s, axis=-1, keepdims=True), sink)
        e = jnp.exp(s - m)
        dens.append(jnp.sum(e, axis=-1, keepdims=True) + jnp.exp(sink - m))
        es.append(e.astype(_MXU_DTYPE))
    outs = [jnp.dot(e, v_sc[kv, pl.ds(rows[u], 2 * w), :], preferred_element_type=_F32) / den
            for (u, kv, g), e, den in zip(chains, es, dens)]
    for u in range(n_sub):
        for kv in range(n_kv):
            c = chains.index((u, kv, 0))
            o = jnp.where(_own_half(0, (w, LANES)), outs[c], outs[c + 1])
            sl = (0, slice(u * w, (u + 1) * w), slice(kv * LANES, (kv + 1) * LANES))
            o_ref[sl] = (o * _silu(g_ref[sl])).astype(o_ref.dtype)


def _swa(proj, sinks, bias_tiles, *, tq=512):
    b, t, _ = proj.shape
    w = SWA_WINDOW
    n_heads = GROUP_WIDTH // HEAD_DIM
    wide = GROUP_WIDTH // LANES
    return pl.pallas_call(
        functools.partial(_swa_kernel, tq=tq),
        grid=(b, t // tq),
        in_specs=[pl.BlockSpec(memory_space=pltpu.SMEM),
                  pl.BlockSpec((1, tq, GROUP_WIDTH), lambda bb, i: (bb, i, COL_WQ // wide)),
                  pl.BlockSpec((1, t, LANES), lambda bb, i: (bb, 0, COL_WK)),
                  pl.BlockSpec((1, t, LANES), lambda bb, i: (bb, 0, COL_WV)),
                  pl.BlockSpec((1, tq, GROUP_WIDTH), lambda bb, i: (bb, i, COL_WG // wide)),
                  pl.BlockSpec((n_heads, w, 2 * w), lambda bb, i: (0, 0, 0))],
        out_specs=pl.BlockSpec((1, tq, GROUP_WIDTH), lambda bb, i: (bb, i, 0)),
        out_shape=jax.ShapeDtypeStruct((b, t, GROUP_WIDTH), _MXU_DTYPE),
        scratch_shapes=[pltpu.VMEM((n_heads // 2, t + w, LANES), _MXU_DTYPE),
                        pltpu.VMEM((n_heads // 2, t + w, LANES), _MXU_DTYPE)],
        compiler_params=pltpu.CompilerParams(
            dimension_semantics=("arbitrary", "arbitrary"),
            vmem_limit_bytes=_VMEM_LIMIT),
        name="swa",
    )(sinks, proj, proj, proj, proj, bias_tiles)


def _outproj_kernel(x_ref, ya_ref, yb_ref, yc_ref, yd_ref, w_ref, o_ref):
    acc = x_ref[...]
    for g, y_ref in enumerate((ya_ref, yb_ref, yc_ref, yd_ref)):
        acc = acc + jnp.dot(y_ref[...], w_ref[g * GROUP_WIDTH:(g + 1) * GROUP_WIDTH, :],
                            preferred_element_type=_F32)
    o_ref[...] = acc


def _outproj(x2d, ys, w_all, layer):
    m, d = x2d.shape
    tm = 512
    y_spec = pl.BlockSpec((tm, GROUP_WIDTH), lambda i: (i, 0))
    return pl.pallas_call(
        _outproj_kernel,
        grid=(m // tm,),
        in_specs=[pl.BlockSpec((tm, d), lambda i: (i, 0)), y_spec, y_spec, y_spec, y_spec,
                  pl.BlockSpec((None,) + w_all.shape[1:], lambda i: (layer, 0, 0))],
        out_specs=pl.BlockSpec((tm, d), lambda i: (i, 0)),
        out_shape=jax.ShapeDtypeStruct((m, d), _F32),
        compiler_params=pltpu.CompilerParams(
            dimension_semantics=("arbitrary",), vmem_limit_bytes=_VMEM_LIMIT),
        name="outproj",
    )(x2d, *ys, w_all)


def _rearrange_w_in(w_in):
    off = 3 * GROUP_WIDTH
    ff = w_in[..., off:off + FOX_FORGET_COLS]
    pad = jnp.zeros(w_in.shape[:-1] + (LANES - FOX_FORGET_COLS,), w_in.dtype)
    out = jnp.concatenate([w_in[..., :off], w_in[..., off + FOX_FORGET_COLS:], ff, pad], axis=-1)
    assert out.shape[-1] == PROJ_WIDTH
    return out


def _qk_logit_bound(qk_gain):
    g = jnp.abs(qk_gain.astype(_F32))
    return ATTN_SCALE * HEAD_DIM * jnp.max(g[0]) * jnp.max(g[1])


def _qk_gain_row(fox_g, moba_g, swa_g):
    def tiles(g, n):
        return jnp.tile(g.astype(_F32), 2 * n)
    row = jnp.zeros((PROJ_WIDTH,), _F32)
    for col, g, n in ((COL_FQ, fox_g[0], 2), (COL_FK, fox_g[1], 2),
                      (COL_MQ, moba_g[0], 2), (COL_MK, moba_g[1], 2),
                      (COL_WQ, swa_g[0], 2), (COL_WK, swa_g[1], 1)):
        row = lax.dynamic_update_slice(row, tiles(g, n), (col * LANES,))
    return row[None, :]


def kernel(x, norm_gain, w_in, b_forget, fox_qk_gain, moba_qk_gain, swa_qk_gain, sinks, w_out, rel_bias):
    b, t, d = x.shape
    depth = w_in.shape[0]
    w_in_r = _rearrange_w_in(w_in.astype(_MXU_DTYPE))
    w_out_c = w_out.astype(_MXU_DTYPE)
    moba_wide_bias = _moba_wide_bias_tiles(rel_bias)
    swa_bias = _swa_bias_tiles(rel_bias)
    moba_bias_max = jnp.max(jnp.abs(rel_bias[:, :rel_bias.shape[1] // 2].astype(_F32)))
    x2d = x.reshape(b * t, d)
    for layer in range(depth):
        qk_gain_row = _qk_gain_row(fox_qk_gain[layer], moba_qk_gain[layer], swa_qk_gain[layer])
        proj = _inproj(x2d, norm_gain[layer][None, :], w_in_r, layer, qk_gain_row)
        proj = proj.reshape(b, t, PROJ_WIDTH)
        bf_row = jnp.pad(b_forget[layer], (0, LANES - FOX_FORGET_COLS))[None, :]
        fox_bound = _qk_logit_bound(fox_qk_gain[layer])
        y_fox = lax.cond(2.0 * fox_bound <= BOUNDED_SOFTMAX_RANGE,
                         lambda: _fox_bounded(proj, bf_row, fox_bound.reshape(1)),
                         lambda: _fox(proj, bf_row))
        moba_bound = _qk_logit_bound(moba_qk_gain[layer]) + moba_bias_max
        y_moba = lax.cond(2.0 * moba_bound <= BOUNDED_SOFTMAX_RANGE,
                          lambda: _moba_bounded(proj, rel_bias, moba_wide_bias,
                                                moba_bound.reshape(1)),
                          lambda: _moba(proj, rel_bias, _moba_bias_tiles(rel_bias)))
        y_sb = _sb(proj)
        y_swa = _swa(proj, sinks[layer], swa_bias)
        ys = [y.reshape(b * t, GROUP_WIDTH) for y in (y_fox, y_moba, y_sb, y_swa)]
        x2d = _outproj(x2d, ys, w_out_c, layer)
    return x2d.reshape(b, t, d)
```

```python
import functools
import math

import jax
import jax.numpy as jnp
from jax import lax
from jax.experimental import pallas as pl
from jax.experimental.pallas import tpu as pltpu

HEAD_DIM = 64
LANES = 128
GROUP_WIDTH = 256
MOBA_BLOCK = 256
MOBA_TOPK = 3
SWA_WINDOW = 128
NUM_BUCKETS = 32
REL_MAX_DISTANCE = 1024
RMS_EPS = 1e-6
NEG_INF = -1e30
ATTN_SCALE = HEAD_DIM ** -0.5
LOG2E = math.log2(math.e)
BOUNDED_SOFTMAX_RANGE = 60.0
MOBA_NEAR_TILES = REL_MAX_DISTANCE // MOBA_BLOCK + 1
MOBA_WIDE = 2 * MOBA_BLOCK
MOBA_WIDE_NEAR_TILES = REL_MAX_DISTANCE // MOBA_WIDE + 1
FOX_FORGET_COLS = 4
EXP_ZERO_CUTOFF = -104.0

COL_FQ, COL_FK, COL_FV, COL_FG = 0, 2, 4, 6
COL_MQ, COL_MK, COL_MV, COL_MG = 8, 10, 12, 14
COL_SQ, COL_SK, COL_SV, COL_SG = 16, 18, 20, 22
COL_WQ, COL_WK, COL_WV, COL_WG = 24, 26, 27, 28
COL_FF = 30
PROJ_WIDTH = (COL_FF + 1) * LANES
QK_NORM_COLS = (COL_FQ, COL_FQ + 1, COL_FK, COL_FK + 1, COL_MQ, COL_MQ + 1, COL_MK, COL_MK + 1,
                COL_WQ, COL_WQ + 1, COL_WK)

_MXU_DTYPE = jnp.bfloat16
_F32 = jnp.float32
_VMEM_LIMIT = 48 * 1024 * 1024
_VMEM_LIMIT_WIDE = 56 * 1024 * 1024

_NT = (((1,), (1,)), ((), ()))


def _lane_iota(shape):
    return lax.broadcasted_iota(jnp.int32, shape, len(shape) - 1)


def _pair_rms(x, gain_row):
    low = _lane_iota(x.shape) < HEAD_DIM
    sq = x * x
    ms_lo = jnp.sum(jnp.where(low, sq, 0.0), axis=-1, keepdims=True)
    ms_hi = jnp.sum(jnp.where(low, 0.0, sq), axis=-1, keepdims=True)
    ms = jnp.where(low, ms_lo, ms_hi) * (1.0 / HEAD_DIM)
    return x * lax.rsqrt(ms + RMS_EPS) * gain_row


def _silu(g):
    return g * (1.0 / (1.0 + jnp.exp(-g)))


def _split3(x):
    hi = x.astype(_MXU_DTYPE).astype(_F32)
    r = x - hi
    mid = r.astype(_MXU_DTYPE).astype(_F32)
    return hi, mid, r - mid


def _own_half(hh, shape):
    lane = _lane_iota(shape)
    return (lane < HEAD_DIM) if hh == 0 else (lane >= HEAD_DIM)


def _flash_update(s, v_tile, m, l, acc):
    m_new = jnp.maximum(m, jnp.max(s, axis=-1, keepdims=True))
    alpha = jnp.exp(m - m_new)
    p = jnp.exp(s - m_new)
    l = alpha * l + jnp.sum(p, axis=-1, keepdims=True)
    acc = alpha * acc + jnp.dot(p.astype(_MXU_DTYPE), v_tile, preferred_element_type=_F32)
    return m_new, l, acc


def _causal(tq):
    return (lax.broadcasted_iota(jnp.int32, (tq, tq), 0)
            >= lax.broadcasted_iota(jnp.int32, (tq, tq), 1))


def _inproj_kernel(x_ref, gain_ref, w_ref, qkg_ref, o_ref, *, tn):
    x = x_ref[...]
    ms = jnp.mean(x * x, axis=-1, keepdims=True)
    hn = (x * lax.rsqrt(ms + RMS_EPS) * gain_ref[...]).astype(_MXU_DTYPE)
    width = o_ref.shape[1]
    for n in range(pl.cdiv(width, tn)):
        hi = min((n + 1) * tn, width)
        acc = jnp.dot(hn, w_ref[:, n * tn:hi], preferred_element_type=_F32)
        for c in range(n * tn // LANES, hi // LANES):
            tile = acc[:, c * LANES - n * tn:(c + 1) * LANES - n * tn]
            if c in QK_NORM_COLS:
                tile = _pair_rms(tile, qkg_ref[:, c * LANES:(c + 1) * LANES])
            o_ref[:, c * LANES:(c + 1) * LANES] = tile


def _inproj(x2d, gain_row, w_all, layer, qk_gain_row):
    m, d = x2d.shape
    n = w_all.shape[2]
    tm = 512
    return pl.pallas_call(
        functools.partial(_inproj_kernel, tn=1024),
        grid=(m // tm,),
        in_specs=[pl.BlockSpec((tm, d), lambda i: (i, 0)),
                  pl.BlockSpec((1, d), lambda i: (0, 0)),
                  pl.BlockSpec((None, d, n), lambda i: (layer, 0, 0)),
                  pl.BlockSpec((1, n), lambda i: (0, 0))],
        out_specs=pl.BlockSpec((tm, n), lambda i: (i, 0)),
        out_shape=jax.ShapeDtypeStruct((m, n), _F32),
        compiler_params=pltpu.CompilerParams(
            dimension_semantics=("arbitrary",), vmem_limit_bytes=_VMEM_LIMIT),
        name="inproj",
    )(x2d, gain_row, w_all, qk_gain_row)


def _rel_bucket(dist):
    max_exact = NUM_BUCKETS // 2
    d = jnp.maximum(dist, 0)
    log_ratio = (jnp.log(jnp.maximum(d, 1).astype(_F32) / max_exact)
                 / math.log(REL_MAX_DISTANCE / max_exact))
    large = max_exact + (log_ratio * (NUM_BUCKETS - max_exact)).astype(jnp.int32)
    large = jnp.minimum(large, NUM_BUCKETS - 1)
    return jnp.where(d < max_exact, d, large)


def _bias_kernel(rb_ref, b_ref, o_ref, *, col0, scale):
    h = pl.program_id(0)
    for d in range(b_ref.shape[0]):
        buckets = b_ref[d]
        acc = jnp.zeros(buckets.shape, _F32)
        for k in range(NUM_BUCKETS):
            acc = jnp.where(buckets == k, rb_ref[k, col0 + h], acc)
        o_ref[0, d] = acc if scale == 1.0 else acc * scale


def _bias_lookup(rel_bias, buckets, col0, scale):
    n_heads = rel_bias.shape[1] // 2
    return pl.pallas_call(
        functools.partial(_bias_kernel, col0=col0, scale=scale),
        grid=(n_heads,),
        in_specs=[pl.BlockSpec(memory_space=pltpu.SMEM),
                  pl.BlockSpec(buckets.shape, lambda h: (0, 0, 0))],
        out_specs=pl.BlockSpec((1,) + buckets.shape, lambda h: (h, 0, 0, 0)),
        out_shape=jax.ShapeDtypeStruct((n_heads,) + buckets.shape, _F32),
        compiler_params=pltpu.CompilerParams(
            dimension_semantics=("arbitrary",), vmem_limit_bytes=_VMEM_LIMIT),
        name="bias_tiles",
    )(rel_bias, buckets)


def _toeplitz_buckets(size, count):
    i = jnp.arange(size)[:, None]
    j = jnp.arange(size)[None, :]
    return jnp.stack([_rel_bucket(d * size + i - j) for d in range(count)])


def _moba_bias_tiles(rel_bias):
    return _bias_lookup(rel_bias, _toeplitz_buckets(MOBA_BLOCK, MOBA_NEAR_TILES), 0, 1.0)


def _moba_wide_bias_tiles(rel_bias):
    return _bias_lookup(rel_bias, _toeplitz_buckets(MOBA_WIDE, MOBA_WIDE_NEAR_TILES), 0, LOG2E)


def _swa_bias_tiles(rel_bias):
    w = SWA_WINDOW
    buckets = _rel_bucket(jnp.arange(w)[:, None] + w - jnp.arange(2 * w)[None, :])
    return _bias_lookup(rel_bias, buckets[None], rel_bias.shape[1] // 2, 1.0)[:, 0]


def _forget_cumsum(ff_ref, bf_ref, c_sc):
    t_len = c_sc.shape[0]
    ff = ff_ref[0] + bf_ref[...]
    log_f = jnp.minimum(ff, 0.0) - jnp.log(1.0 + jnp.exp(-jnp.abs(ff)))
    parts = [part.astype(_MXU_DTYPE) for part in _split3(log_f)]
    ch = 256
    tri = (lax.broadcasted_iota(jnp.int32, (ch, ch), 0)
           >= lax.broadcasted_iota(jnp.int32, (ch, ch), 1)).astype(_MXU_DTYPE)
    carry = jnp.zeros((1, LANES), _F32)
    for r in range(t_len // ch):
        inc = carry
        for part in reversed(parts):
            inc = inc + jnp.dot(tri, part[r * ch:(r + 1) * ch], preferred_element_type=_F32)
        c_sc[r * ch:(r + 1) * ch, :] = inc
        carry = inc[ch - 1:ch, :]


def _lane_column(x, col):
    return jnp.sum(jnp.where(_lane_iota(x.shape) == col, x, 0.0), axis=-1, keepdims=True)


def _lane_fields(shape, a0, fields):
    lane = _lane_iota(shape)
    out = jnp.zeros(shape, _F32)
    for n, f in enumerate(fields):
        out = jnp.where(lane == a0 + n, f, out)
    return out


def _write_ones_column(vext_sc, h, hh):
    @pl.when(pl.program_id(0) == 0)
    def _():
        lane = _lane_iota((vext_sc.shape[1], LANES))
        vext_sc[h, :, LANES:2 * LANES] = jnp.where(lane == hh, 1.0, 0.0).astype(_MXU_DTYPE)


def _bounded_weights(ps_ref, q_augs, kaug_sc, r, bias, diagonal):
    tq = q_augs[0].shape[0]
    if not diagonal:
        for h in range(len(q_augs)):
            s = lax.dot_general(q_augs[h], kaug_sc[h, pl.ds(r, tq), :], _NT,
                                preferred_element_type=_F32)
            if bias is not None:
                s = s + bias(h)
            ps_ref[h] = jnp.exp2(s).astype(_MXU_DTYPE)
        return
    half = tq // 2
    rows = lax.broadcasted_iota(jnp.int32, (tq, half), 0)
    cols = lax.broadcasted_iota(jnp.int32, (tq, half), 1)
    for h in range(len(q_augs)):
        b_h = None if bias is None else bias(h)
        s = lax.dot_general(q_augs[h], kaug_sc[h, pl.ds(r, half), :], _NT,
                            preferred_element_type=_F32)
        if b_h is not None:
            s = s + b_h[:, 0:half]
        ps_ref[h, :, 0:half] = jnp.exp2(jnp.where(rows >= cols, s, NEG_INF)).astype(_MXU_DTYPE)
        s = lax.dot_general(q_augs[h][half:, :], kaug_sc[h, pl.ds(r + half, half), :], _NT,
                            preferred_element_type=_F32)
        if b_h is not None:
            s = s + b_h[half:, half:]
        ps_ref[h, 0:half, half:] = jnp.zeros((half, half), _MXU_DTYPE)
        ps_ref[h, half:, half:] = jnp.exp2(jnp.where(_causal(half), s, NEG_INF)).astype(_MXU_DTYPE)


def _bounded_values(ps_ref, vext_sc, r, acc_sc):
    tq = ps_ref.shape[1]
    for p in range(acc_sc.shape[0]):
        acc_sc[p] += (
            jnp.dot(ps_ref[2 * p], vext_sc[2 * p, pl.ds(r, tq), :], preferred_element_type=_F32)
            + jnp.dot(ps_ref[2 * p + 1], vext_sc[2 * p + 1, pl.ds(r, tq), :],
                      preferred_element_type=_F32))


def _bounded_store(acc_sc, g_ref, o_ref):
    tq = acc_sc.shape[1]
    for p in range(acc_sc.shape[0]):
        acc = acc_sc[p]
        den = acc[:, LANES:]
        l = jnp.where(_own_half(0, (tq, LANES)), _lane_column(den, 0), _lane_column(den, 1))
        cols = slice(p * LANES, (p + 1) * LANES)
        o_ref[0, :, cols] = (acc[:, :LANES] / l * _silu(g_ref[0, :, cols])).astype(o_ref.dtype)


def _bounded_pipeline_step(n, ps_bufs, weights, values):
    def new_in(slot):
        def run():
            weights(ps_bufs[slot])
            values(ps_bufs[1 - slot])
        return run
    lax.cond(n % 2 == 0, new_in(0), new_in(1))


def _bounded_finish(n_done, ps_bufs, values):
    lax.cond(n_done % 2 == 0, lambda: values(ps_bufs[0]), lambda: values(ps_bufs[1]))


def _fox_extra_base(h):
    p, hh = divmod(h, 2)
    return (HEAD_DIM if hh == 0 else 0) + 8 * p


def _fox_extra_lanes(fields):
    rows = max(f.shape[1] for fs in fields for f in fs if hasattr(f, "shape"))
    sub = lax.broadcasted_iota(jnp.int32, (8, rows), 0)
    blocks = {}
    for h, fs in enumerate(fields):
        blk = jnp.zeros((8, rows), _F32)
        for n, f in enumerate(fs):
            blk = jnp.where(sub == n, f, blk)
        blocks[_fox_extra_base(h)] = blk
    pieces, at = [], 0
    for base in sorted(blocks):
        if base > at:
            pieces.append(jnp.zeros((base - at, rows), _F32))
        pieces.append(blocks[base])
        at = base + 8
    pieces.append(jnp.zeros((LANES - at, rows), _F32))
    return jnp.concatenate(pieces, axis=0).T


def _fox_bounded_kernel(bound_ref, q_ref, k_ref, v_ref, ff_ref, g_ref, bf_ref, o_ref,
                        kaug_sc, vext_sc, c_sc, ps_a, ps_b, acc_sc):
    i = pl.program_id(1)
    tq = q_ref.shape[1]
    n_heads = kaug_sc.shape[0]

    @pl.when(i == 0)
    def _prep():
        _forget_cumsum(ff_ref, bf_ref, c_sc)
        hi, mid, lo = _split3(-(c_sc[...] * LOG2E).T[0:8, :])
        extra = _fox_extra_lanes([[hi[h:h + 1], mid[h:h + 1], lo[h:h + 1], 1.0, 1.0, 1.0, 1.0, 1.0]
                                  for h in range(n_heads)])
        lane = _lane_iota(extra.shape)
        for h in range(n_heads):
            p, hh = divmod(h, 2)
            kn = k_ref[0, :, p * LANES:(p + 1) * LANES]
            v = v_ref[0, :, p * LANES:(p + 1) * LANES]
            a0 = _fox_extra_base(h)
            aug = jnp.where((lane >= a0) & (lane < a0 + 8), extra, 0.0)
            own = _own_half(hh, kn.shape)
            kaug_sc[h] = jnp.where(own, kn, aug).astype(_MXU_DTYPE)
            vext_sc[h, :, 0:LANES] = jnp.where(own, v, 0.0).astype(_MXU_DTYPE)
            _write_ones_column(vext_sc, h, hh)

    row0 = pl.multiple_of(i * tq, tq)
    off = jnp.full((1, 1), -LOG2E, _F32) * bound_ref[0]
    off_hi = off.astype(_MXU_DTYPE).astype(_F32)
    hi, mid, lo = _split3((c_sc[pl.ds(row0, tq), :] * LOG2E).T[0:8, :])
    extra = _fox_extra_lanes([[1.0, 1.0, 1.0, hi[h:h + 1], mid[h:h + 1], lo[h:h + 1],
                               off_hi, off - off_hi] for h in range(n_heads)])
    lane = _lane_iota(extra.shape)
    q_augs = []
    for h in range(n_heads):
        p, hh = divmod(h, 2)
        q = q_ref[0, :, p * LANES:(p + 1) * LANES] * (ATTN_SCALE * LOG2E)
        a0 = _fox_extra_base(h)
        aug = jnp.where((lane >= a0) & (lane < a0 + 8), extra, 0.0)
        q_augs.append(jnp.where(_own_half(hh, q.shape), q, aug).astype(_MXU_DTYPE))

    head_lanes = _lane_iota((1, LANES)) < n_heads
    c_first = c_sc[pl.ds(row0, 1), :]
    cutoff = EXP_ZERO_CUTOFF - 2.0 * bound_ref[0]

    def live(j):
        c_last = c_sc[pl.ds(jnp.maximum(j, 0) * tq + (tq - 1), 1), :]
        gap = jnp.max(jnp.where(head_lanes, c_first - c_last, -jnp.inf))
        return (j >= 0) & (gap >= cutoff)

    ps_bufs = (ps_a, ps_b)

    def values_at(r):
        return lambda ps_ref: _bounded_values(ps_ref, vext_sc, pl.multiple_of(r, tq), acc_sc)

    def body(state):
        j, _, r_prev, n = state
        r = pl.multiple_of(j * tq, tq)
        _bounded_pipeline_step(
            n, ps_bufs,
            lambda ps_ref: _bounded_weights(ps_ref, q_augs, kaug_sc, r, None, False),
            values_at(r_prev))
        return j - 1, live(j - 1), r, n + 1

    acc_sc[...] = jnp.zeros(acc_sc.shape, _F32)
    _bounded_weights(ps_a, q_augs, kaug_sc, row0, None, True)
    state = lax.while_loop(lambda state: state[1], body,
                           (i - 1, live(i - 1), row0, jnp.int32(1)))
    _bounded_finish(state[3] - 1, ps_bufs, values_at(state[2]))
    _bounded_store(acc_sc, g_ref, o_ref)


def _fox_bounded(proj, bf_row, bound, *, tq=512):
    b, t, _ = proj.shape
    n_heads = GROUP_WIDTH // HEAD_DIM
    wide = GROUP_WIDTH // LANES
    return pl.pallas_call(
        _fox_bounded_kernel,
        grid=(b, t // tq),
        in_specs=[pl.BlockSpec(memory_space=pltpu.SMEM),
                  pl.BlockSpec((1, tq, GROUP_WIDTH), lambda bb, i: (bb, i, COL_FQ // wide)),
                  pl.BlockSpec((1, t, GROUP_WIDTH), lambda bb, i: (bb, 0, COL_FK // wide)),
                  pl.BlockSpec((1, t, GROUP_WIDTH), lambda bb, i: (bb, 0, COL_FV // wide)),
                  pl.BlockSpec((1, t, LANES), lambda bb, i: (bb, 0, COL_FF)),
                  pl.BlockSpec((1, tq, GROUP_WIDTH), lambda bb, i: (bb, i, COL_FG // wide)),
                  pl.BlockSpec((1, LANES), lambda bb, i: (0, 0))],
        out_specs=pl.BlockSpec((1, tq, GROUP_WIDTH), lambda bb, i: (bb, i, 0)),
        out_shape=jax.ShapeDtypeStruct((b, t, GROUP_WIDTH), _MXU_DTYPE),
        scratch_shapes=[pltpu.VMEM((n_heads, t, LANES), _MXU_DTYPE),
                        pltpu.VMEM((n_heads, t, 2 * LANES), _MXU_DTYPE),
                        pltpu.VMEM((t, LANES), _F32),
                        pltpu.VMEM((n_heads, tq, tq), _MXU_DTYPE),
                        pltpu.VMEM((n_heads, tq, tq), _MXU_DTYPE),
                        pltpu.VMEM((n_heads // 2, tq, 2 * LANES), _F32)],
        compiler_params=pltpu.CompilerParams(
            dimension_semantics=("arbitrary", "arbitrary"),
            vmem_limit_bytes=_VMEM_LIMIT_WIDE),
        name="fox_bounded",
    )(bound, proj, proj, proj, proj, proj, bf_row)


def _fox_kernel(q_ref, k_ref, v_ref, ff_ref, g_ref, bf_ref, o_ref,
                kaug_sc, v_sc, c_sc, *, tq):
    p = pl.program_id(1)
    i = pl.program_id(2)

    def head_column(x, hh):
        return _lane_column(x, 2 * p + hh)

    @pl.when(i == 0)
    def _prep():
        pl.when(p == 0)(lambda: _forget_cumsum(ff_ref, bf_ref, c_sc))
        kn = k_ref[0]
        c_all = c_sc[...]
        for hh in range(2):
            a0 = HEAD_DIM if hh == 0 else 0
            hi, mid, lo = _split3(-head_column(c_all, hh))
            aug = _lane_fields(kn.shape, a0, [hi, mid, lo, 1.0, 1.0, 1.0])
            kaug_sc[hh] = jnp.where(_own_half(hh, kn.shape), kn, aug).astype(_MXU_DTYPE)
        v_sc[...] = v_ref[0].astype(_MXU_DTYPE)

    q = q_ref[0] * ATTN_SCALE
    row0 = pl.multiple_of(i * tq, tq)
    c_t = c_sc[pl.ds(row0, tq), :]
    outs = []
    for hh in range(2):
        a0 = HEAD_DIM if hh == 0 else 0
        hi, mid, lo = _split3(head_column(c_t, hh))
        aug = _lane_fields(q.shape, a0, [1.0, 1.0, 1.0, hi, mid, lo])
        q_aug = jnp.where(_own_half(hh, q.shape), q, aug).astype(_MXU_DTYPE)

        s = lax.dot_general(q_aug, kaug_sc[hh, pl.ds(row0, tq), :], _NT,
                            preferred_element_type=_F32)
        s = jnp.where(_causal(tq), s, NEG_INF)
        m = jnp.max(s, axis=-1, keepdims=True)
        pr = jnp.exp(s - m)
        l = jnp.sum(pr, axis=-1, keepdims=True)
        acc = jnp.dot(pr.astype(_MXU_DTYPE), v_sc[pl.ds(row0, tq), :], preferred_element_type=_F32)

        def body(j, carry, q_aug=q_aug, hh=hh):
            r = pl.multiple_of(j * tq, tq)
            s = lax.dot_general(q_aug, kaug_sc[hh, pl.ds(r, tq), :], _NT,
                                preferred_element_type=_F32)
            return _flash_update(s, v_sc[pl.ds(r, tq), :], *carry)

        m, l, acc = lax.fori_loop(0, i, body, (m, l, acc))
        outs.append(acc / l)
    o = jnp.where(_own_half(0, outs[0].shape), outs[0], outs[1])
    o_ref[0] = (o * _silu(g_ref[0])).astype(o_ref.dtype)


def _fox(proj, bf_row, *, tq=256):
    b, t, _ = proj.shape
    return pl.pallas_call(
        functools.partial(_fox_kernel, tq=tq),
        grid=(b, 2, t // tq),
        in_specs=[pl.BlockSpec((1, tq, LANES), lambda bb, p, i: (bb, i, COL_FQ + p)),
                  pl.BlockSpec((1, t, LANES), lambda bb, p, i: (bb, 0, COL_FK + p)),
                  pl.BlockSpec((1, t, LANES), lambda bb, p, i: (bb, 0, COL_FV + p)),
                  pl.BlockSpec((1, t, LANES), lambda bb, p, i: (bb, 0, COL_FF)),
                  pl.BlockSpec((1, tq, LANES), lambda bb, p, i: (bb, i, COL_FG + p)),
                  pl.BlockSpec((1, LANES), lambda bb, p, i: (0, 0))],
        out_specs=pl.BlockSpec((1, tq, LANES), lambda bb, p, i: (bb, i, p)),
        out_shape=jax.ShapeDtypeStruct((b, t, GROUP_WIDTH), _MXU_DTYPE),
        scratch_shapes=[pltpu.VMEM((2, t, LANES), _MXU_DTYPE),
                        pltpu.VMEM((t, LANES), _MXU_DTYPE),
                        pltpu.VMEM((t, LANES), _F32)],
        compiler_params=pltpu.CompilerParams(
            dimension_semantics=("arbitrary", "arbitrary", "arbitrary"),
            vmem_limit_bytes=_VMEM_LIMIT),
        name="fox",
    )(proj, proj, proj, proj, proj, bf_row)


def _moba_block_means(kn, kmean_sc):
    kmean_sc[...] = jnp.zeros(kmean_sc.shape, _F32)
    for n in range(kn.shape[0] // MOBA_BLOCK):
        kmean_sc[n:n + 1, :] = jnp.mean(kn[n * MOBA_BLOCK:(n + 1) * MOBA_BLOCK], axis=0, keepdims=True)


def _moba_select(q_head, kmean, past):
    gate = lax.dot_general(q_head, kmean, _NT, precision=lax.Precision.HIGHEST,
                           preferred_element_type=_F32)
    lane_f = _lane_iota(gate.shape).astype(_F32)
    cand = jnp.where(past, gate, -jnp.inf)
    sel = jnp.zeros(gate.shape, _F32)
    for _ in range(MOBA_TOPK):
        mx = jnp.max(cand, axis=-1, keepdims=True)
        is_max = (cand == mx) & (mx > -jnp.inf)
        first = jnp.min(jnp.where(is_max, lane_f, float(LANES)), axis=-1, keepdims=True)
        pick = lane_f == first
        sel = jnp.where(pick, 1.0, sel)
        cand = jnp.where(pick, -jnp.inf, cand)
    return sel


def _moba_bounded_kernel(rb_ref, bound_ref, q_ref, k_ref, v_ref, g_ref, bias_ref, o_ref,
                         kaug_sc, vext_sc, kmean_sc, ps_a, ps_b, acc_sc):
    i = pl.program_id(1)
    tq = q_ref.shape[1]
    n_heads = kaug_sc.shape[0]
    blk = MOBA_BLOCK
    near = MOBA_WIDE_NEAR_TILES

    @pl.when(i == 0)
    def _prep():
        lane = _lane_iota((k_ref.shape[1], LANES))
        row_blk = lax.broadcasted_iota(jnp.int32, lane.shape, 0) // blk
        for h in range(n_heads):
            p, hh = divmod(h, 2)
            kn = k_ref[0, :, p * LANES:(p + 1) * LANES]
            v = v_ref[0, :, p * LANES:(p + 1) * LANES]
            if hh == 0:
                _moba_block_means(kn, kmean_sc.at[p])
            a0 = HEAD_DIM if hh == 0 else 0
            aug = jnp.where((lane - a0 == row_blk) | (lane - a0 - 16 == row_blk)
                            | (lane - a0 == 32) | (lane - a0 == 33), 1.0, 0.0)
            own = _own_half(hh, kn.shape)
            kaug_sc[h] = jnp.where(own, kn, aug).astype(_MXU_DTYPE)
            vext_sc[h, :, 0:LANES] = jnp.where(own, v, 0.0).astype(_MXU_DTYPE)
            _write_ones_column(vext_sc, h, hh)

    row0 = pl.multiple_of(i * tq, tq)
    off = jnp.full((1, 1), -LOG2E, _F32) * bound_ref[0]
    off_hi = off.astype(_MXU_DTYPE).astype(_F32)
    blk_n = lax.broadcasted_iota(jnp.int32, (16, tq), 0)
    blk_f = blk_n.astype(_F32)
    q_blk = i * (tq // blk) + lax.broadcasted_iota(jnp.int32, (16, tq), 1) // blk
    past = blk_n < q_blk
    far = i - blk_n // 2 >= near
    sub8 = lax.broadcasted_iota(jnp.int32, (8, tq), 0)
    off_rows = jnp.where(sub8 == 0, off_hi, jnp.where(sub8 == 1, off - off_hi, 0.0))
    gates = []
    for p in range(n_heads // 2):
        means = kmean_sc[p, 0:16, :]
        first = _own_half(0, means.shape)
        stacked = jnp.concatenate([jnp.where(first, means, 0.0), jnp.where(first, 0.0, means)], axis=0)
        gates.append(lax.dot_general(stacked, q_ref[0, :, p * LANES:(p + 1) * LANES], _NT,
                                     precision=lax.Precision.HIGHEST, preferred_element_type=_F32))
    q_augs = []
    for h in range(n_heads):
        p, hh = divmod(h, 2)
        qn = q_ref[0, :, p * LANES:(p + 1) * LANES]
        own = _own_half(hh, qn.shape)
        cand = jnp.where(past, gates[p][16 * hh:16 * hh + 16, :], -jnp.inf)
        sel = jnp.zeros((16, tq), _F32)
        for _ in range(MOBA_TOPK):
            mx = jnp.max(cand, axis=0, keepdims=True)
            is_max = (cand == mx) & (mx > -jnp.inf)
            first = jnp.min(jnp.where(is_max, blk_f, float(LANES)), axis=0, keepdims=True)
            pick = blk_f == first
            sel = jnp.where(pick, 1.0, sel)
            cand = jnp.where(pick, -jnp.inf, cand)
        c_far = jnp.full((1, 1), LOG2E, _F32) * rb_ref[NUM_BUCKETS - 1, h]
        far_hi = c_far.astype(_MXU_DTYPE).astype(_F32)
        pen = jnp.where((blk_n == q_blk) | (past & (sel != 0.0)), 0.0, NEG_INF)
        fields = jnp.concatenate([pen + jnp.where(far, far_hi, 0.0),
                                  jnp.where(far, c_far - far_hi, 0.0),
                                  off_rows,
                                  jnp.zeros((HEAD_DIM - 40, tq), _F32)], axis=0)
        blank = jnp.zeros((HEAD_DIM, tq), _F32)
        aug = jnp.concatenate([blank, fields] if hh == 0 else [fields, blank], axis=0).T
        q_augs.append(jnp.where(own, qn * (ATTN_SCALE * LOG2E), aug).astype(_MXU_DTYPE))

    ps_bufs = (ps_a, ps_b)

    def values_at(r):
        return lambda ps_ref: _bounded_values(ps_ref, vext_sc, pl.multiple_of(r, tq), acc_sc)

    def body(j, state, with_bias):
        r_prev, n = state
        r = pl.multiple_of(j * tq, tq)
        bias = (lambda h: bias_ref[h, i - j]) if with_bias else None
        _bounded_pipeline_step(
            n, ps_bufs,
            lambda ps_ref: _bounded_weights(ps_ref, q_augs, kaug_sc, r, bias, False),
            values_at(r_prev))
        return r, n + 1

    acc_sc[...] = jnp.zeros(acc_sc.shape, _F32)
    _bounded_weights(ps_a, q_augs, kaug_sc, row0, lambda h: bias_ref[h, 0], True)
    state = (row0, jnp.int32(1))
    near_lo = jnp.maximum(i - (near - 1), 0)
    state = lax.fori_loop(near_lo, i, functools.partial(body, with_bias=True), state)
    state = lax.fori_loop(0, near_lo, functools.partial(body, with_bias=False), state)
    _bounded_finish(state[1] - 1, ps_bufs, values_at(state[0]))
    _bounded_store(acc_sc, g_ref, o_ref)


def _moba_bounded(proj, rel_bias, bias_tiles, bound):
    b, t, _ = proj.shape
    tq = MOBA_WIDE
    assert t % tq == 0 and t // MOBA_BLOCK <= 16
    n_heads = GROUP_WIDTH // HEAD_DIM
    wide = GROUP_WIDTH // LANES
    resident = pl.Buffered(1)
    return pl.pallas_call(
        _moba_bounded_kernel,
        grid=(b, t // tq),
        in_specs=[pl.BlockSpec(memory_space=pltpu.SMEM),
                  pl.BlockSpec(memory_space=pltpu.SMEM),
                  pl.BlockSpec((1, tq, GROUP_WIDTH), lambda bb, i: (bb, i, COL_MQ // wide)),
                  pl.BlockSpec((1, t, GROUP_WIDTH), lambda bb, i: (bb, 0, COL_MK // wide),
                               pipeline_mode=resident),
                  pl.BlockSpec((1, t, GROUP_WIDTH), lambda bb, i: (bb, 0, COL_MV // wide),
                               pipeline_mode=resident),
                  pl.BlockSpec((1, tq, GROUP_WIDTH), lambda bb, i: (bb, i, COL_MG // wide)),
                  pl.BlockSpec((n_heads, MOBA_WIDE_NEAR_TILES, tq, tq), lambda bb, i: (0, 0, 0, 0),
                               pipeline_mode=resident)],
        out_specs=pl.BlockSpec((1, tq, GROUP_WIDTH), lambda bb, i: (bb, i, 0)),
        out_shape=jax.ShapeDtypeStruct((b, t, GROUP_WIDTH), _MXU_DTYPE),
        scratch_shapes=[pltpu.VMEM((n_heads, t, LANES), _MXU_DTYPE),
                        pltpu.VMEM((n_heads, t, 2 * LANES), _MXU_DTYPE),
                        pltpu.VMEM((n_heads // 2, LANES, LANES), _F32),
                        pltpu.VMEM((n_heads, tq, tq), _MXU_DTYPE),
                        pltpu.VMEM((n_heads, tq, tq), _MXU_DTYPE),
                        pltpu.VMEM((n_heads // 2, tq, 2 * LANES), _F32)],
        compiler_params=pltpu.CompilerParams(
            dimension_semantics=("arbitrary", "arbitrary"),
            vmem_limit_bytes=_VMEM_LIMIT_WIDE),
        name="moba_bounded",
    )(rel_bias, bound, proj, proj, proj, proj, bias_tiles)


def _moba_kernel(rb_ref, q_ref, k_ref, v_ref, g_ref, bias_ref, o_ref,
                 kaug_sc, v_sc, kmean_sc):
    p = pl.program_id(1)
    i = pl.program_id(2)
    blk = MOBA_BLOCK
    near = MOBA_NEAR_TILES

    @pl.when(i == 0)
    def _prep():
        kn = k_ref[0]
        _moba_block_means(kn, kmean_sc)
        lane = _lane_iota(kn.shape)
        row_blk = lax.broadcasted_iota(jnp.int32, kn.shape, 0) // blk
        for hh in range(2):
            a0 = HEAD_DIM if hh == 0 else 0
            onehot = jnp.where((lane - a0 == row_blk) | (lane - a0 - 16 == row_blk), 1.0, 0.0)
            kaug_sc[hh] = jnp.where(_own_half(hh, kn.shape), kn, onehot).astype(_MXU_DTYPE)
        v_sc[...] = v_ref[0].astype(_MXU_DTYPE)

    qn = q_ref[0]
    row0 = pl.multiple_of(i * blk, blk)
    lane = _lane_iota(qn.shape)
    past = lane < i
    outs = []
    for hh in range(2):
        own = _own_half(hh, qn.shape)
        sel = _moba_select(jnp.where(own, qn, 0.0), kmean_sc[...], past)
        c_far = jnp.full((1, LANES), rb_ref[NUM_BUCKETS - 1, 2 * p + hh], _F32)
        far_hi = c_far.astype(_MXU_DTYPE).astype(_F32)
        far_lo = c_far - far_hi
        pen = jnp.where(past & (sel == 0.0), NEG_INF, 0.0)
        aug = jnp.where(lane < 16, pen + jnp.where(i - lane >= near, far_hi, 0.0),
                        jnp.where((lane < 32) & (i - (lane - 16) >= near), far_lo, 0.0))
        if hh == 0:
            aug = pltpu.roll(aug, HEAD_DIM, axis=1)
        q_aug = jnp.where(own, qn * ATTN_SCALE, aug).astype(_MXU_DTYPE)

        def scores(r, q_aug=q_aug, hh=hh):
            return lax.dot_general(q_aug, kaug_sc[hh, pl.ds(r, blk), :], _NT,
                                   preferred_element_type=_F32)

        s = scores(row0) + bias_ref[hh, 0]
        s = jnp.where(_causal(blk), s, NEG_INF)
        m = jnp.max(s, axis=-1, keepdims=True)
        pr = jnp.exp(s - m)
        l = jnp.sum(pr, axis=-1, keepdims=True)
        acc = jnp.dot(pr.astype(_MXU_DTYPE), v_sc[pl.ds(row0, blk), :], preferred_element_type=_F32)

        def near_body(j, carry, hh=hh, scores=scores):
            r = pl.multiple_of(j * blk, blk)
            return _flash_update(scores(r) + bias_ref[hh, i - j], v_sc[pl.ds(r, blk), :], *carry)

        def far_body(j, carry, scores=scores):
            r = pl.multiple_of(j * blk, blk)
            return _flash_update(scores(r), v_sc[pl.ds(r, blk), :], *carry)

        near_lo = jnp.maximum(i - (near - 1), 0)
        carry = lax.fori_loop(near_lo, i, near_body, (m, l, acc))
        m, l, acc = lax.fori_loop(0, near_lo, far_body, carry)
        outs.append(acc / l)
    o = jnp.where(_own_half(0, outs[0].shape), outs[0], outs[1])
    o_ref[0] = (o * _silu(g_ref[0])).astype(o_ref.dtype)


def _moba(proj, rel_bias, bias_tiles):
    b, t, _ = proj.shape
    blk = MOBA_BLOCK
    assert t % blk == 0 and t // blk <= 16
    return pl.pallas_call(
        _moba_kernel,
        grid=(b, 2, t // blk),
        in_specs=[pl.BlockSpec(memory_space=pltpu.SMEM),
                  pl.BlockSpec((1, blk, LANES), lambda bb, p, i: (bb, i, COL_MQ + p)),
                  pl.BlockSpec((1, t, LANES), lambda bb, p, i: (bb, 0, COL_MK + p)),
                  pl.BlockSpec((1, t, LANES), lambda bb, p, i: (bb, 0, COL_MV + p)),
                  pl.BlockSpec((1, blk, LANES), lambda bb, p, i: (bb, i, COL_MG + p)),
                  pl.BlockSpec((2, MOBA_NEAR_TILES, blk, blk), lambda bb, p, i: (p, 0, 0, 0))],
        out_specs=pl.BlockSpec((1, blk, LANES), lambda bb, p, i: (bb, i, p)),
        out_shape=jax.ShapeDtypeStruct((b, t, GROUP_WIDTH), _MXU_DTYPE),
        scratch_shapes=[pltpu.VMEM((2, t, LANES), _MXU_DTYPE),
                        pltpu.VMEM((t, LANES), _MXU_DTYPE),
                        pltpu.VMEM((LANES, LANES), _F32)],
        compiler_params=pltpu.CompilerParams(
            dimension_semantics=("arbitrary", "arbitrary", "arbitrary"),
            vmem_limit_bytes=_VMEM_LIMIT),
        name="moba",
    )(rel_bias, proj, proj, proj, proj, bias_tiles)


def _sb_kernel(q_ref, k_ref, v_ref, g_ref, o_ref, k_sc, v_sc, *, tq):
    i = pl.program_id(1)
    n_heads = v_sc.shape[0]

    @pl.when(i == 0)
    def _prep():
        for p in range(n_heads // 2):
            k_sc[p] = k_ref[0, :, p * LANES:(p + 1) * LANES].astype(_MXU_DTYPE)
            v = v_ref[0, :, p * LANES:(p + 1) * LANES]
            for hh in range(2):
                v_sc[2 * p + hh] = jnp.where(_own_half(hh, v.shape), v, 0.0).astype(_MXU_DTYPE)

    subs = q_ref.shape[1] // tq
    q_heads = []
    for u in range(subs):
        q_heads.append([])
        for h in range(n_heads):
            p, hh = divmod(h, 2)
            q = q_ref[0, u * tq:(u + 1) * tq, p * LANES:(p + 1) * LANES] * (ATTN_SCALE * LOG2E)
            q_heads[u].append(jnp.where(_own_half(hh, q.shape), q, 0.0).astype(_MXU_DTYPE))
    strict = (lax.broadcasted_iota(jnp.int32, (tq, tq), 0)
              > lax.broadcasted_iota(jnp.int32, (tq, tq), 1))
    after = strict.astype(_MXU_DTYPE)
    sign_bit = jnp.uint32(0x80000000)

    def pair(back, runs, diagonal):
        chains = [(u, t, h) for u in range(subs) for t in range(2) for h in range(n_heads)]
        tile_idx = [[subs * i + u - back - t for t in range(2)] for u in range(subs)]
        valid = [[(j >= 0).astype(_F32) for j in js] for js in tile_idx]
        rows = [[pl.multiple_of(jnp.maximum(j, 0) * tq, tq) for j in js] for js in tile_idx]
        masked = [diagonal and t == 0 for _, t, _ in chains]
        zs = [lax.dot_general(q_heads[u][h], k_sc[h // 2, pl.ds(rows[u][t], tq), :], _NT,
                              preferred_element_type=_F32) for u, t, h in chains]
        drops = []
        for z, msk in zip(zs, masked):
            neg_abs = lax.bitcast_convert_type(lax.bitcast_convert_type(z, jnp.uint32) | sign_bit, _F32)
            drop = jnp.maximum(z, 0.0) + jnp.log(1.0 + jnp.exp2(neg_abs)) * LOG2E
            drops.append(jnp.where(strict, drop, 0.0) if msk else drop)
        laters = []
        for drop in drops:
            hi = drop.astype(_MXU_DTYPE)
            lo = (drop - hi.astype(_F32)).astype(_MXU_DTYPE)
            laters.append(jnp.dot(hi, after, preferred_element_type=_F32)
                          + jnp.dot(lo, after, preferred_element_type=_F32))
        pvs = []
        for (u, t, h), z, drop, later, msk in zip(chains, zs, drops, laters, masked):
            w = jnp.exp2(z - drop - later)
            if msk:
                w = jnp.where(strict, w, 0.0)
            pvs.append(jnp.dot(w.astype(_MXU_DTYPE), v_sc[h, pl.ds(rows[u][t], tq), :],
                               preferred_element_type=_F32))
        sums = [jnp.sum(drop, axis=-1, keepdims=True) for drop in drops]
        new_runs, outs = [], []
        for u in range(subs):
            new_runs.append([])
            for p in range(n_heads // 2):
                out = None
                for h in (2 * p, 2 * p + 1):
                    run = runs[u][h]
                    for t in range(2):
                        c = chains.index((u, t, h))
                        pv = pvs[c] * (jnp.exp2(run) * valid[u][t])
                        out = pv if out is None else out + pv
                        run = run - sums[c] * valid[u][t]
                    new_runs[u].append(run)
                outs.append(out)
        return new_runs, outs

    def alive(runs):
        top = functools.reduce(jnp.maximum, [r for rs in runs for r in rs])
        return jnp.max(top) >= EXP_ZERO_CUTOFF * LOG2E

    def flat(runs):
        return [r for rs in runs for r in rs]

    def nested(flat_runs):
        return [flat_runs[n_heads * u:n_heads * (u + 1)] for u in range(subs)]

    n_runs = n_heads * subs
    zero = jnp.zeros((tq, 1), _F32)
    runs, accs = pair(0, [[zero] * n_heads for _ in range(subs)], True)

    def body(state):
        back = state[0]
        runs, outs = pair(back, nested(list(state[2:2 + n_runs])), False)
        accs = [a + o for a, o in zip(state[2 + n_runs:], outs)]
        return (back + 2, alive(runs), *flat(runs), *accs)

    state = lax.while_loop(lambda state: (subs * i + subs - 1 - state[0] >= 0) & state[1], body,
                           (jnp.int32(2), alive(runs), *flat(runs), *accs))
    accs = state[2 + n_runs:]
    for u in range(subs):
        for p in range(n_heads // 2):
            sl = (0, slice(u * tq, (u + 1) * tq), slice(p * LANES, (p + 1) * LANES))
            o_ref[sl] = (accs[u * (n_heads // 2) + p] * _silu(g_ref[sl])).astype(o_ref.dtype)


def _sb(proj, *, tq=256, subs=2):
    b, t, _ = proj.shape
    rows = tq * subs
    n_heads = GROUP_WIDTH // HEAD_DIM
    wide = GROUP_WIDTH // LANES
    return pl.pallas_call(
        functools.partial(_sb_kernel, tq=tq),
        grid=(b, t // rows),
        in_specs=[pl.BlockSpec((1, rows, GROUP_WIDTH), lambda bb, i: (bb, i, COL_SQ // wide)),
                  pl.BlockSpec((1, t, GROUP_WIDTH), lambda bb, i: (bb, 0, COL_SK // wide)),
                  pl.BlockSpec((1, t, GROUP_WIDTH), lambda bb, i: (bb, 0, COL_SV // wide)),
                  pl.BlockSpec((1, rows, GROUP_WIDTH), lambda bb, i: (bb, i, COL_SG // wide))],
        out_specs=pl.BlockSpec((1, rows, GROUP_WIDTH), lambda bb, i: (bb, i, 0)),
        out_shape=jax.ShapeDtypeStruct((b, t, GROUP_WIDTH), _MXU_DTYPE),
        scratch_shapes=[pltpu.VMEM((n_heads // 2, t, LANES), _MXU_DTYPE),
                        pltpu.VMEM((n_heads, t, LANES), _MXU_DTYPE)],
        compiler_params=pltpu.CompilerParams(
            dimension_semantics=("arbitrary", "arbitrary"),
            vmem_limit_bytes=_VMEM_LIMIT),
        name="stickbreak",
    )(proj, proj, proj, proj)


def _swa_kernel(sink_ref, q_ref, k_ref, v_ref, g_ref, bias_ref, o_ref,
                k_sc, v_sc, *, tq):
    i = pl.program_id(1)
    t_len = k_ref.shape[1]
    w = SWA_WINDOW
    n_kv = k_sc.shape[0]

    @pl.when(i == 0)
    def _prep():
        kn = k_ref[0]
        v = v_ref[0]
        kn_swapped = pltpu.roll(kn, HEAD_DIM, axis=1)
        v_swapped = pltpu.roll(v, HEAD_DIM, axis=1)
        for kv in range(n_kv):
            keep = _own_half(kv, kn.shape)
            k_sc[kv, 0:w, :] = jnp.zeros((w, LANES), _MXU_DTYPE)
            v_sc[kv, 0:w, :] = jnp.zeros((w, LANES), _MXU_DTYPE)
            k_sc[kv, w:w + t_len, :] = jnp.where(keep, kn, kn_swapped).astype(_MXU_DTYPE)
            v_sc[kv, w:w + t_len, :] = jnp.where(keep, v, v_swapped).astype(_MXU_DTYPE)

    qi =lax.broadcasted_iota(jnp.int32, (w, 2 * w), 0)
    kj = lax.broadcasted_iota(jnp.int32, (w, 2 * w), 1)
    dist = qi + w - kj
    in_window = (dist >= 0) & (dist < w)
    n_sub = tq // w
    chains = [(u, kv, g) for u in range(n_sub) for kv in range(n_kv) for g in range(2)]
    rows = [pl.multiple_of((i * n_sub + u) * w, w) for u in range(n_sub)]
    ss = []
    for u, kv, g in chains:
        q_u = q_ref[0, u * w:(u + 1) * w, kv * LANES:(kv + 1) * LANES] * ATTN_SCALE
        q_h = jnp.where(_own_half(g, q_u.shape), q_u, 0.0).astype(_MXU_DTYPE)
        ss.append(lax.dot_general(q_h, k_sc[kv, pl.ds(rows[u], 2 * w), :], _NT,
                                  preferred_element_type=_F32))
    es, dens = [], []
    for (u, kv, g), s in zip(chains, ss):
        allowed = in_window & (kj + (i * n_sub + u - 1) * w >= 0)
        s = jnp.where(allowed, s + bias_ref[2 * kv + g], NEG_INF)
        sink = sink_ref[2 * kv + g]
        m = jnp.maximum(jnp.max(s, axis=-1, keepdims=True), sink)
        e = jnp.exp(s - m)
        dens.append(jnp.sum(e, axis=-1, keepdims=True) + jnp.exp(sink - m))
        es.append(e.astype(_MXU_DTYPE))
    outs = [jnp.dot(e, v_sc[kv, pl.ds(rows[u], 2 * w), :], preferred_element_type=_F32) / den
            for (u, kv, g), e, den in zip(chains, es, dens)]
    for u in range(n_sub):
        for kv in range(n_kv):
            c = chains.index((u, kv, 0))
            o = jnp.where(_own_half(0, (w, LANES)), outs[c], outs[c + 1])
            sl = (0, slice(u * w, (u + 1) * w), slice(kv * LANES, (kv + 1) * LANES))
            o_ref[sl] = (o * _silu(g_ref[sl])).astype(o_ref.dtype)


def _swa(proj, sinks, bias_tiles, *, tq=512):
    b, t, _ = proj.shape
    w = SWA_WINDOW
    n_heads = GROUP_WIDTH // HEAD_DIM
    wide = GROUP_WIDTH // LANES
    return pl.pallas_call(
        functools.partial(_swa_kernel, tq=tq),
        grid=(b, t // tq),
        in_specs=[pl.BlockSpec(memory_space=pltpu.SMEM),
                  pl.BlockSpec((1, tq, GROUP_WIDTH), lambda bb, i: (bb, i, COL_WQ // wide)),
                  pl.BlockSpec((1, t, LANES), lambda bb, i: (bb, 0, COL_WK)),
                  pl.BlockSpec((1, t, LANES), lambda bb, i: (bb, 0, COL_WV)),
                  pl.BlockSpec((1, tq, GROUP_WIDTH), lambda bb, i: (bb, i, COL_WG // wide)),
                  pl.BlockSpec((n_heads, w, 2 * w), lambda bb, i: (0, 0, 0))],
        out_specs=pl.BlockSpec((1, tq, GROUP_WIDTH), lambda bb, i: (bb, i, 0)),
        out_shape=jax.ShapeDtypeStruct((b, t, GROUP_WIDTH), _MXU_DTYPE),
        scratch_shapes=[pltpu.VMEM((n_heads // 2, t + w, LANES), _MXU_DTYPE),
                        pltpu.VMEM((n_heads // 2, t + w, LANES), _MXU_DTYPE)],
        compiler_params=pltpu.CompilerParams(
            dimension_semantics=("arbitrary", "arbitrary"),
            vmem_limit_bytes=_VMEM_LIMIT),
        name="swa",
    )(sinks, proj, proj, proj, proj, bias_tiles)


def _outproj_kernel(x_ref, ya_ref, yb_ref, yc_ref, yd_ref, w_ref, o_ref):
    acc = x_ref[...]
    for g, y_ref in enumerate((ya_ref, yb_ref, yc_ref, yd_ref)):
        acc = acc + jnp.dot(y_ref[...], w_ref[g * GROUP_WIDTH:(g + 1) * GROUP_WIDTH, :],
                            preferred_element_type=_F32)
    o_ref[...] = acc


def _outproj(x2d, ys, w_all, layer):
    m, d = x2d.shape
    tm = 512
    y_spec = pl.BlockSpec((tm, GROUP_WIDTH), lambda i: (i, 0))
    return pl.pallas_call(
        _outproj_kernel,
        grid=(m // tm,),
        in_specs=[pl.BlockSpec((tm, d), lambda i: (i, 0)), y_spec, y_spec, y_spec, y_spec,
                  pl.BlockSpec((None,) + w_all.shape[1:], lambda i: (layer, 0, 0))],
        out_specs=pl.BlockSpec((tm, d), lambda i: (i, 0)),
        out_shape=jax.ShapeDtypeStruct((m, d), _F32),
        compiler_params=pltpu.CompilerParams(
            dimension_semantics=("arbitrary",), vmem_limit_bytes=_VMEM_LIMIT),
        name="outproj",
    )(x2d, *ys, w_all)


def _rearrange_w_in(w_in):
    off = 3 * GROUP_WIDTH
    ff = w_in[..., off:off + FOX_FORGET_COLS]
    pad = jnp.zeros(w_in.shape[:-1] + (LANES - FOX_FORGET_COLS,), w_in.dtype)
    out = jnp.concatenate([w_in[..., :off], w_in[..., off + FOX_FORGET_COLS:], ff, pad], axis=-1)
    assert out.shape[-1] == PROJ_WIDTH
    return out


def _qk_logit_bound(qk_gain):
    g = jnp.abs(qk_gain.astype(_F32))
    return ATTN_SCALE * HEAD_DIM * jnp.max(g[0]) * jnp.max(g[1])


def _qk_gain_row(fox_g, moba_g, swa_g):
    def tiles(g, n):
        return jnp.tile(g.astype(_F32), 2 * n)
    row = jnp.zeros((PROJ_WIDTH,), _F32)
    for col, g, n in ((COL_FQ, fox_g[0], 2), (COL_FK, fox_g[1], 2),
                      (COL_MQ, moba_g[0], 2), (COL_MK, moba_g[1], 2),
                      (COL_WQ, swa_g[0], 2), (COL_WK, swa_g[1], 1)):
        row = lax.dynamic_update_slice(row, tiles(g, n), (col * LANES,))
    return row[None, :]


def kernel(x, norm_gain, w_in, b_forget, fox_qk_gain, moba_qk_gain, swa_qk_gain, sinks, w_out, rel_bias):
    b, t, d = x.shape
    depth = w_in.shape[0]
    w_in_r = _rearrange_w_in(w_in.astype(_MXU_DTYPE))
    w_out_c = w_out.astype(_MXU_DTYPE)
    moba_wide_bias = _moba_wide_bias_tiles(rel_bias)
    swa_bias = _swa_bias_tiles(rel_bias)
    moba_bias_max = jnp.max(jnp.abs(rel_bias[:, :rel_bias.shape[1] // 2].astype(_F32)))
    x2d = x.reshape(b * t, d)
    for layer in range(depth):
        qk_gain_row = _qk_gain_row(fox_qk_gain[layer], moba_qk_gain[layer], swa_qk_gain[layer])
        proj = _inproj(x2d, norm_gain[layer][None, :], w_in_r, layer, qk_gain_row)
        proj = proj.reshape(b, t, PROJ_WIDTH)
        bf_row = jnp.pad(b_forget[layer], (0, LANES - FOX_FORGET_COLS))[None, :]
        fox_bound = _qk_logit_bound(fox_qk_gain[layer])
        y_fox = lax.cond(2.0 * fox_bound <= BOUNDED_SOFTMAX_RANGE,
                         lambda: _fox_bounded(proj, bf_row, fox_bound.reshape(1)),
                         lambda: _fox(proj, bf_row))
        moba_bound = _qk_logit_bound(moba_qk_gain[layer]) + moba_bias_max
        y_moba = lax.cond(2.0 * moba_bound <= BOUNDED_SOFTMAX_RANGE,
                          lambda: _moba_bounded(proj, rel_bias, moba_wide_bias,
                                                moba_bound.reshape(1)),
                          lambda: _moba(proj, rel_bias, _moba_bias_tiles(rel_bias)))
        y_sb = _sb(proj)
        y_swa = _swa(proj, sinks[layer], swa_bias)
        ys = [y.reshape(b * t, GROUP_WIDTH) for y in (y_fox, y_moba, y_sb, y_swa)]
        x2d = _outproj(x2d, ys, w_out_c, layer)
    return x2d.reshape(b, t, d)
```

```python
import functools
import math

import jax
import jax.numpy as jnp
from jax import lax
from jax.experimental import pallas as pl
from jax.experimental.pallas import tpu as pltpu

HEAD_DIM = 64
LANES = 128
GROUP_WIDTH = 256
MOBA_BLOCK = 256
MOBA_TOPK = 3
MOBA_MAX_BLOCKS = 16
SWA_WINDOW = 128
NUM_BUCKETS = 32
REL_MAX_DISTANCE = 1024
RMS_EPS = 1e-6
NEG_INF = -1e30
ATTN_SCALE = HEAD_DIM ** -0.5
LOG2E = math.log2(math.e)
BOUNDED_SOFTMAX_RANGE = 60.0
MOBA_NEAR_TILES = REL_MAX_DISTANCE // MOBA_BLOCK + 1
MOBA_WIDE = 2 * MOBA_BLOCK
MOBA_WIDE_NEAR_TILES = REL_MAX_DISTANCE // MOBA_WIDE + 1
FOX_FORGET_COLS = 4
FORGET_ROWS = 8
EXP_ZERO_CUTOFF = -104.0

COL_FQ, COL_FK, COL_FV, COL_FG = 0, 2, 4, 6
COL_MQ, COL_MK, COL_MV, COL_MG = 8, 10, 12, 14
COL_SQ, COL_SK, COL_SV, COL_SG = 16, 18, 20, 22
COL_WQ, COL_WK, COL_WV, COL_WG = 24, 26, 27, 28
COL_FF = 30
PROJ_WIDTH = (COL_FF + 1) * LANES
QK_NORM_COLS = (COL_FQ, COL_FQ + 1, COL_FK, COL_FK + 1, COL_MQ, COL_MQ + 1, COL_MK, COL_MK + 1,
                COL_WQ, COL_WQ + 1, COL_WK)

_MXU_DTYPE = jnp.bfloat16
_F32 = jnp.float32
_VMEM_LIMIT = 48 * 1024 * 1024
_VMEM_LIMIT_WIDE = 56 * 1024 * 1024

_NT = (((1,), (1,)), ((), ()))


def _lane_iota(shape):
    return lax.broadcasted_iota(jnp.int32, shape, len(shape) - 1)


def _pair_rms(x, gain_row):
    low = _lane_iota(x.shape) < HEAD_DIM
    sq = x * x
    ms_lo = jnp.sum(jnp.where(low, sq, 0.0), axis=-1, keepdims=True)
    ms_hi = jnp.sum(jnp.where(low, 0.0, sq), axis=-1, keepdims=True)
    ms = jnp.where(low, ms_lo, ms_hi) * (1.0 / HEAD_DIM)
    return x * lax.rsqrt(ms + RMS_EPS) * gain_row


def _silu(g):
    return g * (1.0 / (1.0 + jnp.exp(-g)))


def _split3(x):
    hi = x.astype(_MXU_DTYPE).astype(_F32)
    r = x - hi
    mid = r.astype(_MXU_DTYPE).astype(_F32)
    return hi, mid, r - mid


def _own_half(hh, shape):
    lane = _lane_iota(shape)
    return (lane < HEAD_DIM) if hh == 0 else (lane >= HEAD_DIM)


def _flash_update(s, v_tile, m, l, acc):
    m_new = jnp.maximum(m, jnp.max(s, axis=-1, keepdims=True))
    alpha = jnp.exp(m - m_new)
    p = jnp.exp(s - m_new)
    l = alpha * l + jnp.sum(p, axis=-1, keepdims=True)
    acc = alpha * acc + jnp.dot(p.astype(_MXU_DTYPE), v_tile, preferred_element_type=_F32)
    return m_new, l, acc


def _causal(tq):
    return (lax.broadcasted_iota(jnp.int32, (tq, tq), 0)
            >= lax.broadcasted_iota(jnp.int32, (tq, tq), 1))


def _inproj_kernel(x_ref, gain_ref, w_ref, qkg_ref, o_ref, *, tn):
    x = x_ref[...]
    ms = jnp.mean(x * x, axis=-1, keepdims=True)
    hn = (x * lax.rsqrt(ms + RMS_EPS) * gain_ref[...]).astype(_MXU_DTYPE)
    width = o_ref.shape[1]
    for n in range(pl.cdiv(width, tn)):
        hi = min((n + 1) * tn, width)
        acc = jnp.dot(hn, w_ref[:, n * tn:hi], preferred_element_type=_F32)
        for c in range(n * tn // LANES, hi // LANES):
            tile = acc[:, c * LANES - n * tn:(c + 1) * LANES - n * tn]
            if c in QK_NORM_COLS:
                tile = _pair_rms(tile, qkg_ref[:, c * LANES:(c + 1) * LANES])
            o_ref[:, c * LANES:(c + 1) * LANES] = tile


def _inproj(x2d, gain_row, w_all, layer, qk_gain_row):
    m, d = x2d.shape
    n = w_all.shape[2]
    tm = 512
    return pl.pallas_call(
        functools.partial(_inproj_kernel, tn=1024),
        grid=(m // tm,),
        in_specs=[pl.BlockSpec((tm, d), lambda i: (i, 0)),
                  pl.BlockSpec((1, d), lambda i: (0, 0)),
                  pl.BlockSpec((None, d, n), lambda i: (layer, 0, 0)),
                  pl.BlockSpec((1, n), lambda i: (0, 0))],
        out_specs=pl.BlockSpec((tm, n), lambda i: (i, 0)),
        out_shape=jax.ShapeDtypeStruct((m, n), _F32),
        compiler_params=pltpu.CompilerParams(
            dimension_semantics=("arbitrary",), vmem_limit_bytes=_VMEM_LIMIT),
        name="inproj",
    )(x2d, gain_row, w_all, qk_gain_row)


def _rel_bucket(dist):
    max_exact = NUM_BUCKETS // 2
    d = jnp.maximum(dist, 0)
    log_ratio = (jnp.log(jnp.maximum(d, 1).astype(_F32) / max_exact)
                 / math.log(REL_MAX_DISTANCE / max_exact))
    large = max_exact + (log_ratio * (NUM_BUCKETS - max_exact)).astype(jnp.int32)
    large = jnp.minimum(large, NUM_BUCKETS - 1)
    return jnp.where(d < max_exact, d, large)


def _bias_kernel(rb_ref, b_ref, o_ref, *, col0, scale):
    h = pl.program_id(0)
    for d in range(b_ref.shape[0]):
        buckets = b_ref[d]
        acc = jnp.zeros(buckets.shape, _F32)
        for k in range(NUM_BUCKETS):
            acc = jnp.where(buckets == k, rb_ref[k, col0 + h], acc)
        o_ref[0, d] = acc if scale == 1.0 else acc * scale


def _bias_lookup(rel_bias, buckets, col0, scale):
    n_heads = rel_bias.shape[1] // 2
    return pl.pallas_call(
        functools.partial(_bias_kernel, col0=col0, scale=scale),
        grid=(n_heads,),
        in_specs=[pl.BlockSpec(memory_space=pltpu.SMEM),
                  pl.BlockSpec(buckets.shape, lambda h: (0, 0, 0))],
        out_specs=pl.BlockSpec((1,) + buckets.shape, lambda h: (h, 0, 0, 0)),
        out_shape=jax.ShapeDtypeStruct((n_heads,) + buckets.shape, _F32),
        compiler_params=pltpu.CompilerParams(
            dimension_semantics=("arbitrary",), vmem_limit_bytes=_VMEM_LIMIT),
        name="bias_tiles",
    )(rel_bias, buckets)


def _toeplitz_buckets(size, count):
    i = jnp.arange(size)[:, None]
    j = jnp.arange(size)[None, :]
    return jnp.stack([_rel_bucket(d * size + i - j) for d in range(count)])


def _moba_bias_tiles(rel_bias):
    return _bias_lookup(rel_bias, _toeplitz_buckets(MOBA_BLOCK, MOBA_NEAR_TILES), 0, 1.0)


def _moba_wide_bias_tiles(rel_bias):
    return _bias_lookup(rel_bias, _toeplitz_buckets(MOBA_WIDE, MOBA_WIDE_NEAR_TILES), 0, LOG2E)


def _swa_bias_tiles(rel_bias):
    w = SWA_WINDOW
    buckets = _rel_bucket(jnp.arange(w)[:, None] + w - jnp.arange(2 * w)[None, :])
    return _bias_lookup(rel_bias, buckets[None], rel_bias.shape[1] // 2, 1.0)[:, 0]


def _forget_cumsum(ff_ref, bf_ref, c_sc):
    t_len = c_sc.shape[0]
    ff = ff_ref[0] + bf_ref[...]
    log_f = jnp.minimum(ff, 0.0) - jnp.log(1.0 + jnp.exp(-jnp.abs(ff)))
    parts = [part.astype(_MXU_DTYPE) for part in _split3(log_f)]
    ch = 256
    tri = (lax.broadcasted_iota(jnp.int32, (ch, ch), 0)
           >= lax.broadcasted_iota(jnp.int32, (ch, ch), 1)).astype(_MXU_DTYPE)
    carry = jnp.zeros((1, LANES), _F32)
    for r in range(t_len // ch):
        inc = carry
        for part in reversed(parts):
            inc = inc + jnp.dot(tri, part[r * ch:(r + 1) * ch], preferred_element_type=_F32)
        c_sc[r * ch:(r + 1) * ch, :] = inc
        carry = inc[ch - 1:ch, :]


def _forget_cumsum_rows(ff_ref, bf_ref, c_rows):
    n_rows, t_len = c_rows.shape
    z = ff_ref[0].T[0:n_rows, :] + bf_ref[...]
    log_f = jnp.minimum(z, 0.0) - jnp.log(1.0 + jnp.exp(-jnp.abs(z)))
    parts = [part.astype(_MXU_DTYPE) for part in _split3(log_f)]
    ch = 256
    tri = (lax.broadcasted_iota(jnp.int32, (ch, ch), 0)
           <= lax.broadcasted_iota(jnp.int32, (ch, ch), 1)).astype(_MXU_DTYPE)
    carry = jnp.zeros((n_rows, 1), _F32)
    for r in range(t_len // ch):
        inc = carry
        for part in reversed(parts):
            inc = inc + jnp.dot(part[:, r * ch:(r + 1) * ch], tri, preferred_element_type=_F32)
        c_rows[:, r * ch:(r + 1) * ch] = inc
        carry = inc[:, ch - 1:ch]


def _lane_column(x, col):
    return jnp.sum(jnp.where(_lane_iota(x.shape) == col, x, 0.0), axis=-1, keepdims=True)


def _lane_fields(shape, a0, fields):
    lane = _lane_iota(shape)
    out = jnp.zeros(shape, _F32)
    for n, f in enumerate(fields):
        out = jnp.where(lane == a0 + n, f, out)
    return out


def _write_ones_column(vext_sc, h, hh):
    @pl.when(pl.program_id(0) == 0)
    def _():
        lane = _lane_iota((vext_sc.shape[1], LANES))
        vext_sc[h, :, LANES:2 * LANES] = jnp.where(lane == hh, 1.0, 0.0).astype(_MXU_DTYPE)


def _bounded_weights(ps_ref, q_augs, kaug_sc, r, bias, diagonal):
    tq = q_augs[0].shape[0]
    if not diagonal:
        for h in range(len(q_augs)):
            s = lax.dot_general(q_augs[h], kaug_sc[h, pl.ds(r, tq), :], _NT,
                                preferred_element_type=_F32)
            if bias is not None:
                s = s + bias(h)
            ps_ref[h] = jnp.exp2(s).astype(_MXU_DTYPE)
        return
    half = tq // 2
    rows = lax.broadcasted_iota(jnp.int32, (tq, half), 0)
    cols = lax.broadcasted_iota(jnp.int32, (tq, half), 1)
    for h in range(len(q_augs)):
        b_h = None if bias is None else bias(h)
        s = lax.dot_general(q_augs[h], kaug_sc[h, pl.ds(r, half), :], _NT,
                            preferred_element_type=_F32)
        if b_h is not None:
            s = s + b_h[:, 0:half]
        ps_ref[h, :, 0:half] = jnp.exp2(jnp.where(rows >= cols, s, NEG_INF)).astype(_MXU_DTYPE)
        s = lax.dot_general(q_augs[h][half:, :], kaug_sc[h, pl.ds(r + half, half), :], _NT,
                            preferred_element_type=_F32)
        if b_h is not None:
            s = s + b_h[half:, half:]
        ps_ref[h, 0:half, half:] = jnp.zeros((half, half), _MXU_DTYPE)
        ps_ref[h, half:, half:] = jnp.exp2(jnp.where(_causal(half), s, NEG_INF)).astype(_MXU_DTYPE)


def _bounded_values(ps_ref, vext_sc, r, acc_sc):
    tq = ps_ref.shape[1]
    for p in range(acc_sc.shape[0]):
        acc_sc[p] += (
            jnp.dot(ps_ref[2 * p], vext_sc[2 * p, pl.ds(r, tq), :], preferred_element_type=_F32)
            + jnp.dot(ps_ref[2 * p + 1], vext_sc[2 * p + 1, pl.ds(r, tq), :],
                      preferred_element_type=_F32))


def _bounded_store(acc_sc, g_ref, o_ref):
    tq = acc_sc.shape[1]
    for p in range(acc_sc.shape[0]):
        acc = acc_sc[p]
        den = acc[:, LANES:]
        l = jnp.where(_own_half(0, (tq, LANES)), den[:, 0:1], den[:, 1:2])
        cols = slice(p * LANES, (p + 1) * LANES)
        o_ref[0, :, cols] = (acc[:, :LANES] / l * _silu(g_ref[0, :, cols])).astype(o_ref.dtype)


def _bounded_pipeline_step(n, ps_bufs, weights, values):
    def new_in(slot):
        def run():
            weights(ps_bufs[slot])
            values(ps_bufs[1 - slot])
        return run
    lax.cond(n % 2 == 0, new_in(0), new_in(1))


def _bounded_finish(n_done, ps_bufs, values):
    lax.cond(n_done % 2 == 0, lambda: values(ps_bufs[0]), lambda: values(ps_bufs[1]))


def _fox_extra_base(h):
    p, hh = divmod(h, 2)
    return (HEAD_DIM if hh == 0 else 0) + 8 * p


def _fox_extra_lanes(fields):
    rows = max(f.shape[1] for fs in fields for f in fs if hasattr(f, "shape"))
    sub = lax.broadcasted_iota(jnp.int32, (8, rows), 0)
    blocks = {}
    for h, fs in enumerate(fields):
        blk = jnp.zeros((8, rows), _F32)
        for n, f in enumerate(fs):
            blk = jnp.where(sub == n, f, blk)
        blocks[_fox_extra_base(h)] = blk
    pieces, at = [], 0
    for base in sorted(blocks):
        if base > at:
            pieces.append(jnp.zeros((base - at, rows), _F32))
        pieces.append(blocks[base])
        at = base + 8
    pieces.append(jnp.zeros((LANES - at, rows), _F32))
    return jnp.concatenate(pieces, axis=0).T


def _fox_bounded_kernel(bound_ref, q_ref, k_ref, v_ref, ff_ref, g_ref, bf_ref, o_ref,
                        kaug_sc, vext_sc, c_sc, ps_a, ps_b, acc_sc):
    i = pl.program_id(1)
    tq = q_ref.shape[1]
    n_heads = kaug_sc.shape[0]

    @pl.when(i == 0)
    def _prep():
        _forget_cumsum_rows(ff_ref, bf_ref, c_sc)
        hi, mid, lo = _split3(-(c_sc[...] * LOG2E))
        extra = _fox_extra_lanes([[hi[h:h + 1], mid[h:h + 1], lo[h:h + 1], 1.0, 1.0, 1.0, 1.0, 1.0]
                                  for h in range(n_heads)])
        lane = _lane_iota(extra.shape)
        for h in range(n_heads):
            p, hh = divmod(h, 2)
            kn = k_ref[0, :, p * LANES:(p + 1) * LANES]
            v = v_ref[0, :, p * LANES:(p + 1) * LANES]
            a0 = _fox_extra_base(h)
            aug = jnp.where((lane >= a0) & (lane < a0 + 8), extra, 0.0)
            own = _own_half(hh, kn.shape)
            kaug_sc[h] = jnp.where(own, kn, aug).astype(_MXU_DTYPE)
            vext_sc[h, :, 0:LANES] = jnp.where(own, v, 0.0).astype(_MXU_DTYPE)
            _write_ones_column(vext_sc, h, hh)

    row0 = pl.multiple_of(i * tq, tq)
    off = jnp.full((1, 1), -LOG2E, _F32) * bound_ref[0]
    off_hi = off.astype(_MXU_DTYPE).astype(_F32)
    hi, mid, lo = _split3(c_sc[:, pl.ds(row0, tq)] * LOG2E)
    extra = _fox_extra_lanes([[1.0, 1.0, 1.0, hi[h:h + 1], mid[h:h + 1], lo[h:h + 1],
                               off_hi, off - off_hi] for h in range(n_heads)])
    lane = _lane_iota(extra.shape)
    q_augs = []
    for h in range(n_heads):
        p, hh = divmod(h, 2)
        q = q_ref[0, :, p * LANES:(p + 1) * LANES] * (ATTN_SCALE * LOG2E)
        a0 = _fox_extra_base(h)
        aug = jnp.where((lane >= a0) & (lane < a0 + 8), extra, 0.0)
        q_augs.append(jnp.where(_own_half(hh, q.shape), q, aug).astype(_MXU_DTYPE))

    chunk_lane = _lane_iota((c_sc.shape[0], LANES))
    head_rows = lax.broadcasted_iota(jnp.int32, (c_sc.shape[0], 1), 0) < n_heads
    c_first = jnp.max(jnp.where(chunk_lane == 0, c_sc[:, pl.ds(row0, LANES)], -jnp.inf),
                      axis=-1, keepdims=True)
    cutoff = EXP_ZERO_CUTOFF - 2.0 * bound_ref[0]

    def live(j):
        start = pl.multiple_of(jnp.maximum(j, 0) * tq + (tq - LANES), LANES)
        c_last = jnp.max(jnp.where(chunk_lane == LANES - 1, c_sc[:, pl.ds(start, LANES)], -jnp.inf),
                         axis=-1, keepdims=True)
        gap = jnp.max(jnp.where(head_rows, c_first - c_last, -jnp.inf))
        return (j >= 0) & (gap >= cutoff)

    ps_bufs = (ps_a, ps_b)

    def values_at(r):
        return lambda ps_ref: _bounded_values(ps_ref, vext_sc, pl.multiple_of(r, tq), acc_sc)

    def body(state):
        j, _, r_prev, n = state
        r = pl.multiple_of(j * tq, tq)
        _bounded_pipeline_step(
            n, ps_bufs,
            lambda ps_ref: _bounded_weights(ps_ref, q_augs, kaug_sc, r, None, False),
            values_at(r_prev))
        return j - 1, live(j - 1), r, n + 1

    acc_sc[...] = jnp.zeros(acc_sc.shape, _F32)
    _bounded_weights(ps_a, q_augs, kaug_sc, row0, None, True)
    state = lax.while_loop(lambda state: state[1], body,
                           (i - 1, live(i - 1), row0, jnp.int32(1)))
    _bounded_finish(state[3] - 1, ps_bufs, values_at(state[2]))
    _bounded_store(acc_sc, g_ref, o_ref)


def _fox_bounded(proj, b_forget, bound, *, tq=512):
    b, t, _ = proj.shape
    bf_col = jnp.pad(b_forget.astype(_F32), (0, FORGET_ROWS - FOX_FORGET_COLS))[:, None]
    n_heads = GROUP_WIDTH // HEAD_DIM
    wide = GROUP_WIDTH // LANES
    return pl.pallas_call(
        _fox_bounded_kernel,
        grid=(b, t // tq),
        in_specs=[pl.BlockSpec(memory_space=pltpu.SMEM),
                  pl.BlockSpec((1, tq, GROUP_WIDTH), lambda bb, i: (bb, i, COL_FQ // wide)),
                  pl.BlockSpec((1, t, GROUP_WIDTH), lambda bb, i: (bb, 0, COL_FK // wide)),
                  pl.BlockSpec((1, t, GROUP_WIDTH), lambda bb, i: (bb, 0, COL_FV // wide)),
                  pl.BlockSpec((1, t, LANES), lambda bb, i: (bb, 0, COL_FF)),
                  pl.BlockSpec((1, tq, GROUP_WIDTH), lambda bb, i: (bb, i, COL_FG // wide)),
                  pl.BlockSpec((FORGET_ROWS, 1), lambda bb, i: (0, 0))],
        out_specs=pl.BlockSpec((1, tq, GROUP_WIDTH), lambda bb, i: (bb, i, 0)),
        out_shape=jax.ShapeDtypeStruct((b, t, GROUP_WIDTH), _MXU_DTYPE),
        scratch_shapes=[pltpu.VMEM((n_heads, t, LANES), _MXU_DTYPE),
                        pltpu.VMEM((n_heads, t, 2 * LANES), _MXU_DTYPE),
                        pltpu.VMEM((FORGET_ROWS, t), _F32),
                        pltpu.VMEM((n_heads, tq, tq), _MXU_DTYPE),
                        pltpu.VMEM((n_heads, tq, tq), _MXU_DTYPE),
                        pltpu.VMEM((n_heads // 2, tq, 2 * LANES), _F32)],
        compiler_params=pltpu.CompilerParams(
            dimension_semantics=("arbitrary", "arbitrary"),
            vmem_limit_bytes=_VMEM_LIMIT_WIDE),
        name="fox_bounded",
    )(bound, proj, proj, proj, proj, proj, bf_col)


def _fox_kernel(q_ref, k_ref, v_ref, ff_ref, g_ref, bf_ref, o_ref,
                kaug_sc, v_sc, c_sc, *, tq):
    p = pl.program_id(1)
    i = pl.program_id(2)

    def head_column(x, hh):
        return _lane_column(x, 2 * p + hh)

    @pl.when(i == 0)
    def _prep():
        pl.when(p == 0)(lambda: _forget_cumsum(ff_ref, bf_ref, c_sc))
        kn = k_ref[0]
        c_all = c_sc[...]
        for hh in range(2):
            a0 = HEAD_DIM if hh == 0 else 0
            hi, mid, lo = _split3(-head_column(c_all, hh))
            aug = _lane_fields(kn.shape, a0, [hi, mid, lo, 1.0, 1.0, 1.0])
            kaug_sc[hh] = jnp.where(_own_half(hh, kn.shape), kn, aug).astype(_MXU_DTYPE)
        v_sc[...] = v_ref[0].astype(_MXU_DTYPE)

    q = q_ref[0] * ATTN_SCALE
    row0 = pl.multiple_of(i * tq, tq)
    c_t = c_sc[pl.ds(row0, tq), :]
    outs = []
    for hh in range(2):
        a0 = HEAD_DIM if hh == 0 else 0
        hi, mid, lo = _split3(head_column(c_t, hh))
        aug = _lane_fields(q.shape, a0, [1.0, 1.0, 1.0, hi, mid, lo])
        q_aug = jnp.where(_own_half(hh, q.shape), q, aug).astype(_MXU_DTYPE)

        s = lax.dot_general(q_aug, kaug_sc[hh, pl.ds(row0, tq), :], _NT,
                            preferred_element_type=_F32)
        s = jnp.where(_causal(tq), s, NEG_INF)
        m = jnp.max(s, axis=-1, keepdims=True)
        pr = jnp.exp(s - m)
        l = jnp.sum(pr, axis=-1, keepdims=True)
        acc = jnp.dot(pr.astype(_MXU_DTYPE), v_sc[pl.ds(row0, tq), :], preferred_element_type=_F32)

        def body(j, carry, q_aug=q_aug, hh=hh):
            r = pl.multiple_of(j * tq, tq)
            s = lax.dot_general(q_aug, kaug_sc[hh, pl.ds(r, tq), :], _NT,
                                preferred_element_type=_F32)
            return _flash_update(s, v_sc[pl.ds(r, tq), :], *carry)

        m, l, acc = lax.fori_loop(0, i, body, (m, l, acc))
        outs.append(acc / l)
    o = jnp.where(_own_half(0, outs[0].shape), outs[0], outs[1])
    o_ref[0] = (o * _silu(g_ref[0])).astype(o_ref.dtype)


def _fox(proj, bf_row, *, tq=256):
    b, t, _ = proj.shape
    return pl.pallas_call(
        functools.partial(_fox_kernel, tq=tq),
        grid=(b, 2, t // tq),
        in_specs=[pl.BlockSpec((1, tq, LANES), lambda bb, p, i: (bb, i, COL_FQ + p)),
                  pl.BlockSpec((1, t, LANES), lambda bb, p, i: (bb, 0, COL_FK + p)),
                  pl.BlockSpec((1, t, LANES), lambda bb, p, i: (bb, 0, COL_FV + p)),
                  pl.BlockSpec((1, t, LANES), lambda bb, p, i: (bb, 0, COL_FF)),
                  pl.BlockSpec((1, tq, LANES), lambda bb, p, i: (bb, i, COL_FG + p)),
                  pl.BlockSpec((1, LANES), lambda bb, p, i: (0, 0))],
        out_specs=pl.BlockSpec((1, tq, LANES), lambda bb, p, i: (bb, i, p)),
        out_shape=jax.ShapeDtypeStruct((b, t, GROUP_WIDTH), _MXU_DTYPE),
        scratch_shapes=[pltpu.VMEM((2, t, LANES), _MXU_DTYPE),
                        pltpu.VMEM((t, LANES), _MXU_DTYPE),
                        pltpu.VMEM((t, LANES), _F32)],
        compiler_params=pltpu.CompilerParams(
            dimension_semantics=("arbitrary", "arbitrary", "arbitrary"),
            vmem_limit_bytes=_VMEM_LIMIT),
        name="fox",
    )(proj, proj, proj, proj, proj, bf_row)


def _moba_block_means(kn, kmean_sc):
    kmean_sc[...] = jnp.zeros(kmean_sc.shape, _F32)
    for n in range(kn.shape[0] // MOBA_BLOCK):
        kmean_sc[n:n + 1, :] = jnp.mean(kn[n * MOBA_BLOCK:(n + 1) * MOBA_BLOCK], axis=0, keepdims=True)


def _moba_select(q_head, kmean, past):
    gate = lax.dot_general(q_head, kmean, _NT, precision=lax.Precision.HIGHEST,
                           preferred_element_type=_F32)
    lane_f = _lane_iota(gate.shape).astype(_F32)
    cand = jnp.where(past, gate, -jnp.inf)
    sel = jnp.zeros(gate.shape, _F32)
    for _ in range(MOBA_TOPK):
        mx = jnp.max(cand, axis=-1, keepdims=True)
        is_max = (cand == mx) & (mx > -jnp.inf)
        first = jnp.min(jnp.where(is_max, lane_f, float(LANES)), axis=-1, keepdims=True)
        pick = lane_f == first
        sel = jnp.where(pick, 1.0, sel)
        cand = jnp.where(pick, -jnp.inf, cand)
    return sel


def _moba_bounded_kernel(rb_ref, bound_ref, q_ref, k_ref, v_ref, g_ref, bias_ref, o_ref,
                         kaug_sc, vext_sc, kmean_sc, ps_a, ps_b, acc_sc):
    i = pl.program_id(1)
    tq = q_ref.shape[1]
    n_heads = kaug_sc.shape[0]
    blk = MOBA_BLOCK
    near = MOBA_WIDE_NEAR_TILES

    @pl.when(i == 0)
    def _prep():
        lane = _lane_iota((k_ref.shape[1], LANES))
        row_blk = lax.broadcasted_iota(jnp.int32, lane.shape, 0) // blk
        for h in range(n_heads):
            p, hh = divmod(h, 2)
            kn = k_ref[0, :, p * LANES:(p + 1) * LANES]
            v = v_ref[0, :, p * LANES:(p + 1) * LANES]
            if hh == 0:
                _moba_block_means(kn, kmean_sc.at[p])
            a0 = HEAD_DIM if hh == 0 else 0
            aug = jnp.where((lane - a0 == row_blk) | (lane - a0 - MOBA_MAX_BLOCKS == row_blk)
                            | (lane - a0 == 2 * MOBA_MAX_BLOCKS)
                            | (lane - a0 == 2 * MOBA_MAX_BLOCKS + 1), 1.0, 0.0)
            own = _own_half(hh, kn.shape)
            kaug_sc[h] = jnp.where(own, kn, aug).astype(_MXU_DTYPE)
            vext_sc[h, :, 0:LANES] = jnp.where(own, v, 0.0).astype(_MXU_DTYPE)
            _write_ones_column(vext_sc, h, hh)

    row0 = pl.multiple_of(i * tq, tq)
    off = jnp.full((1, 1), -LOG2E, _F32) * bound_ref[0]
    off_hi = off.astype(_MXU_DTYPE).astype(_F32)
    blk_n = lax.broadcasted_iota(jnp.int32, (MOBA_MAX_BLOCKS, tq), 0)
    blk_f = blk_n.astype(_F32)
    q_blk = i * (tq // blk) + lax.broadcasted_iota(jnp.int32, (MOBA_MAX_BLOCKS, tq), 1) // blk
    past = blk_n < q_blk
    far = i - blk_n // 2 >= near
    sub8 = lax.broadcasted_iota(jnp.int32, (8, tq), 0)
    off_rows = jnp.where(sub8 == 0, off_hi, jnp.where(sub8 == 1, off - off_hi, 0.0))
    gates = []
    for p in range(n_heads // 2):
        means = kmean_sc[p, 0:MOBA_MAX_BLOCKS, :]
        first = _own_half(0, means.shape)
        stacked = jnp.concatenate([jnp.where(first, means, 0.0), jnp.where(first, 0.0, means)], axis=0)
        gates.append(lax.dot_general(stacked, q_ref[0, :, p * LANES:(p + 1) * LANES], _NT,
                                     precision=lax.Precision.HIGHEST, preferred_element_type=_F32))
    q_augs = []
    for h in range(n_heads):
        p, hh = divmod(h, 2)
        qn = q_ref[0, :, p * LANES:(p + 1) * LANES]
        own = _own_half(hh, qn.shape)
        cand = jnp.where(past, gates[p][MOBA_MAX_BLOCKS * hh:MOBA_MAX_BLOCKS * (hh + 1), :], -jnp.inf)
        sel = jnp.zeros((MOBA_MAX_BLOCKS, tq), _F32)
        for _ in range(MOBA_TOPK):
            mx = jnp.max(cand, axis=0, keepdims=True)
            is_max = (cand == mx) & (mx > -jnp.inf)
            first = jnp.min(jnp.where(is_max, blk_f, float(LANES)), axis=0, keepdims=True)
            pick = blk_f == first
            sel = jnp.where(pick, 1.0, sel)
            cand = jnp.where(pick, -jnp.inf, cand)
        c_far = jnp.full((1, 1), LOG2E, _F32) * rb_ref[NUM_BUCKETS - 1, h]
        far_hi = c_far.astype(_MXU_DTYPE).astype(_F32)
        pen = jnp.where((blk_n == q_blk) | (past & (sel != 0.0)), 0.0, NEG_INF)
        fields = jnp.concatenate([pen + jnp.where(far, far_hi, 0.0),
                                  jnp.where(far, c_far - far_hi, 0.0),
                                  off_rows,
                                  jnp.zeros((HEAD_DIM - 2 * MOBA_MAX_BLOCKS - 8, tq), _F32)],
                                 axis=0)
        blank = jnp.zeros((HEAD_DIM, tq), _F32)
        aug = jnp.concatenate([blank, fields] if hh == 0 else [fields, blank], axis=0).T
        q_augs.append(jnp.where(own, qn * (ATTN_SCALE * LOG2E), aug).astype(_MXU_DTYPE))

    ps_bufs = (ps_a, ps_b)

    def values_at(r):
        return lambda ps_ref: _bounded_values(ps_ref, vext_sc, pl.multiple_of(r, tq), acc_sc)

    def body(j, state, with_bias):
        r_prev, n = state
        r = pl.multiple_of(j * tq, tq)
        bias = (lambda h: bias_ref[h, i - j]) if with_bias else None
        _bounded_pipeline_step(
            n, ps_bufs,
            lambda ps_ref: _bounded_weights(ps_ref, q_augs, kaug_sc, r, bias, False),
            values_at(r_prev))
        return r, n + 1

    acc_sc[...] = jnp.zeros(acc_sc.shape, _F32)
    _bounded_weights(ps_a, q_augs, kaug_sc, row0, lambda h: bias_ref[h, 0], True)
    state = (row0, jnp.int32(1))
    near_lo = jnp.maximum(i - (near - 1), 0)
    state = lax.fori_loop(near_lo, i, functools.partial(body, with_bias=True), state)
    state = lax.fori_loop(0, near_lo, functools.partial(body, with_bias=False), state)
    _bounded_finish(state[1] - 1, ps_bufs, values_at(state[0]))
    _bounded_store(acc_sc, g_ref, o_ref)


def _moba_bounded(proj, rel_bias, bias_tiles, bound):
    b, t, _ = proj.shape
    tq = MOBA_WIDE
    assert t % tq == 0 and t // MOBA_BLOCK <= MOBA_MAX_BLOCKS
    n_heads = GROUP_WIDTH // HEAD_DIM
    wide = GROUP_WIDTH // LANES
    resident = pl.Buffered(1)
    return pl.pallas_call(
        _moba_bounded_kernel,
        grid=(b, t // tq),
        in_specs=[pl.BlockSpec(memory_space=pltpu.SMEM),
                  pl.BlockSpec(memory_space=pltpu.SMEM),
                  pl.BlockSpec((1, tq, GROUP_WIDTH), lambda bb, i: (bb, i, COL_MQ // wide)),
                  pl.BlockSpec((1, t, GROUP_WIDTH), lambda bb, i: (bb, 0, COL_MK // wide),
                               pipeline_mode=resident),
                  pl.BlockSpec((1, t, GROUP_WIDTH), lambda bb, i: (bb, 0, COL_MV // wide),
                               pipeline_mode=resident),
                  pl.BlockSpec((1, tq, GROUP_WIDTH), lambda bb, i: (bb, i, COL_MG // wide)),
                  pl.BlockSpec((n_heads, MOBA_WIDE_NEAR_TILES, tq, tq), lambda bb, i: (0, 0, 0, 0),
                               pipeline_mode=resident)],
        out_specs=pl.BlockSpec((1, tq, GROUP_WIDTH), lambda bb, i: (bb, i, 0)),
        out_shape=jax.ShapeDtypeStruct((b, t, GROUP_WIDTH), _MXU_DTYPE),
        scratch_shapes=[pltpu.VMEM((n_heads, t, LANES), _MXU_DTYPE),
                        pltpu.VMEM((n_heads, t, 2 * LANES), _MXU_DTYPE),
                        pltpu.VMEM((n_heads // 2, LANES, LANES), _F32),
                        pltpu.VMEM((n_heads, tq, tq), _MXU_DTYPE),
                        pltpu.VMEM((n_heads, tq, tq), _MXU_DTYPE),
                        pltpu.VMEM((n_heads // 2, tq, 2 * LANES), _F32)],
        compiler_params=pltpu.CompilerParams(
            dimension_semantics=("arbitrary", "arbitrary"),
            vmem_limit_bytes=_VMEM_LIMIT_WIDE),
        name="moba_bounded",
    )(rel_bias, bound, proj, proj, proj, proj, bias_tiles)


def _moba_kernel(rb_ref, q_ref, k_ref, v_ref, g_ref, bias_ref, o_ref,
                 kaug_sc, v_sc, kmean_sc):
    p = pl.program_id(1)
    i = pl.program_id(2)
    blk = MOBA_BLOCK
    near = MOBA_NEAR_TILES

    @pl.when(i == 0)
    def _prep():
        kn = k_ref[0]
        _moba_block_means(kn, kmean_sc)
        lane = _lane_iota(kn.shape)
        row_blk = lax.broadcasted_iota(jnp.int32, kn.shape, 0) // blk
        for hh in range(2):
            a0 = HEAD_DIM if hh == 0 else 0
            onehot = jnp.where((lane - a0 == row_blk) | (lane - a0 - MOBA_MAX_BLOCKS == row_blk),
                               1.0, 0.0)
            kaug_sc[hh] = jnp.where(_own_half(hh, kn.shape), kn, onehot).astype(_MXU_DTYPE)
        v_sc[...] = v_ref[0].astype(_MXU_DTYPE)

    qn = q_ref[0]
    row0 = pl.multiple_of(i * blk, blk)
    lane = _lane_iota(qn.shape)
    past = lane < i
    outs = []
    for hh in range(2):
        own = _own_half(hh, qn.shape)
        sel = _moba_select(jnp.where(own, qn, 0.0), kmean_sc[...], past)
        c_far = jnp.full((1, LANES), rb_ref[NUM_BUCKETS - 1, 2 * p + hh], _F32)
        far_hi = c_far.astype(_MXU_DTYPE).astype(_F32)
        far_lo = c_far - far_hi
        pen = jnp.where(past & (sel == 0.0), NEG_INF, 0.0)
        aug = jnp.where(lane < MOBA_MAX_BLOCKS, pen + jnp.where(i - lane >= near, far_hi, 0.0),
                        jnp.where((lane < 2 * MOBA_MAX_BLOCKS)
                                  & (i - (lane - MOBA_MAX_BLOCKS) >= near), far_lo, 0.0))
        if hh == 0:
            aug = pltpu.roll(aug, HEAD_DIM, axis=1)
        q_aug = jnp.where(own, qn * ATTN_SCALE, aug).astype(_MXU_DTYPE)

        def scores(r, q_aug=q_aug, hh=hh):
            return lax.dot_general(q_aug, kaug_sc[hh, pl.ds(r, blk), :], _NT,
                                   preferred_element_type=_F32)

        s = scores(row0) + bias_ref[hh, 0]
        s = jnp.where(_causal(blk), s, NEG_INF)
        m = jnp.max(s, axis=-1, keepdims=True)
        pr = jnp.exp(s - m)
        l = jnp.sum(pr, axis=-1, keepdims=True)
        acc = jnp.dot(pr.astype(_MXU_DTYPE), v_sc[pl.ds(row0, blk), :], preferred_element_type=_F32)

        def near_body(j, carry, hh=hh, scores=scores):
            r = pl.multiple_of(j * blk, blk)
            return _flash_update(scores(r) + bias_ref[hh, i - j], v_sc[pl.ds(r, blk), :], *carry)

        def far_body(j, carry, scores=scores):
            r = pl.multiple_of(j * blk, blk)
            return _flash_update(scores(r), v_sc[pl.ds(r, blk), :], *carry)

        near_lo = jnp.maximum(i - (near - 1), 0)
        carry = lax.fori_loop(near_lo, i, near_body, (m, l, acc))
        m, l, acc = lax.fori_loop(0, near_lo, far_body, carry)
        outs.append(acc / l)
    o = jnp.where(_own_half(0, outs[0].shape), outs[0], outs[1])
    o_ref[0] = (o * _silu(g_ref[0])).astype(o_ref.dtype)


def _moba(proj, rel_bias, bias_tiles):
    b, t, _ = proj.shape
    blk = MOBA_BLOCK
    assert t % blk == 0 and t // blk <= MOBA_MAX_BLOCKS
    return pl.pallas_call(
        _moba_kernel,
        grid=(b, 2, t // blk),
        in_specs=[pl.BlockSpec(memory_space=pltpu.SMEM),
                  pl.BlockSpec((1, blk, LANES), lambda bb, p, i: (bb, i, COL_MQ + p)),
                  pl.BlockSpec((1, t, LANES), lambda bb, p, i: (bb, 0, COL_MK + p)),
                  pl.BlockSpec((1, t, LANES), lambda bb, p, i: (bb, 0, COL_MV + p)),
                  pl.BlockSpec((1, blk, LANES), lambda bb, p, i: (bb, i, COL_MG + p)),
                  pl.BlockSpec((2, MOBA_NEAR_TILES, blk, blk), lambda bb, p, i: (p, 0, 0, 0))],
        out_specs=pl.BlockSpec((1, blk, LANES), lambda bb, p, i: (bb, i, p)),
        out_shape=jax.ShapeDtypeStruct((b, t, GROUP_WIDTH), _MXU_DTYPE),
        scratch_shapes=[pltpu.VMEM((2, t, LANES), _MXU_DTYPE),
                        pltpu.VMEM((t, LANES), _MXU_DTYPE),
                        pltpu.VMEM((LANES, LANES), _F32)],
        compiler_params=pltpu.CompilerParams(
            dimension_semantics=("arbitrary", "arbitrary", "arbitrary"),
            vmem_limit_bytes=_VMEM_LIMIT),
        name="moba",
    )(rel_bias, proj, proj, proj, proj, bias_tiles)


def _sb_kernel(q_ref, k_ref, v_ref, g_ref, o_ref, k_sc, v_sc, *, tq):
    i = pl.program_id(1)
    n_heads = v_sc.shape[0]

    @pl.when(i == 0)
    def _prep():
        for p in range(n_heads // 2):
            k_sc[p] = k_ref[0, :, p * LANES:(p + 1) * LANES].astype(_MXU_DTYPE)
            v = v_ref[0, :, p * LANES:(p + 1) * LANES]
            for hh in range(2):
                v_sc[2 * p + hh] = jnp.where(_own_half(hh, v.shape), v, 0.0).astype(_MXU_DTYPE)

    subs = q_ref.shape[1] // tq
    q_heads = []
    for u in range(subs):
        q_heads.append([])
        for h in range(n_heads):
            p, hh = divmod(h, 2)
            q = q_ref[0, u * tq:(u + 1) * tq, p * LANES:(p + 1) * LANES] * (ATTN_SCALE * LOG2E)
            q_heads[u].append(jnp.where(_own_half(hh, q.shape), q, 0.0).astype(_MXU_DTYPE))
    strict = (lax.broadcasted_iota(jnp.int32, (tq, tq), 0)
              > lax.broadcasted_iota(jnp.int32, (tq, tq), 1))
    after = strict.astype(_MXU_DTYPE)
    sign_bit = jnp.uint32(0x80000000)

    def pair(back, runs, diagonal):
        chains = [(u, t, h) for u in range(subs) for t in range(2) for h in range(n_heads)]
        tile_idx = [[subs * i + u - back - t for t in range(2)] for u in range(subs)]
        valid = [[(j >= 0).astype(_F32) for j in js] for js in tile_idx]
        rows = [[pl.multiple_of(jnp.maximum(j, 0) * tq, tq) for j in js] for js in tile_idx]
        masked = [diagonal and t == 0 for _, t, _ in chains]
        zs = [lax.dot_general(q_heads[u][h], k_sc[h // 2, pl.ds(rows[u][t], tq), :], _NT,
                              preferred_element_type=_F32) for u, t, h in chains]
        drops = []
        for z, msk in zip(zs, masked):
            neg_abs = lax.bitcast_convert_type(lax.bitcast_convert_type(z, jnp.uint32) | sign_bit, _F32)
            drop = jnp.maximum(z, 0.0) + jnp.log(1.0 + jnp.exp2(neg_abs)) * LOG2E
            drops.append(jnp.where(strict, drop, 0.0) if msk else drop)
        laters = []
        for drop in drops:
            hi = drop.astype(_MXU_DTYPE)
            lo = (drop - hi.astype(_F32)).astype(_MXU_DTYPE)
            laters.append(jnp.dot(hi, after, preferred_element_type=_F32)
                          + jnp.dot(lo, after, preferred_element_type=_F32))
        pvs = []
        for (u, t, h), z, drop, later, msk in zip(chains, zs, drops, laters, masked):
            w = jnp.exp2(z - drop - later)
            if msk:
                w = jnp.where(strict, w, 0.0)
            pvs.append(jnp.dot(w.astype(_MXU_DTYPE), v_sc[h, pl.ds(rows[u][t], tq), :],
                               preferred_element_type=_F32))
        sums = [jnp.sum(drop, axis=-1, keepdims=True) for drop in drops]
        new_runs, outs = [], []
        for u in range(subs):
            new_runs.append([])
            for p in range(n_heads // 2):
                out = None
                for h in (2 * p, 2 * p + 1):
                    run = runs[u][h]
                    for t in range(2):
                        c = chains.index((u, t, h))
                        pv = pvs[c] * (jnp.exp2(run) * valid[u][t])
                        out = pv if out is None else out + pv
                        run = run - sums[c] * valid[u][t]
                    new_runs[u].append(run)
                outs.append(out)
        return new_runs, outs

    def alive(runs):
        top = functools.reduce(jnp.maximum, [r for rs in runs for r in rs])
        return jnp.max(top) >= EXP_ZERO_CUTOFF * LOG2E

    def flat(runs):
        return [r for rs in runs for r in rs]

    def nested(flat_runs):
        return [flat_runs[n_heads * u:n_heads * (u + 1)] for u in range(subs)]

    n_runs = n_heads * subs
    zero = jnp.zeros((tq, 1), _F32)
    runs, accs = pair(0, [[zero] * n_heads for _ in range(subs)], True)

    def body(state):
        back = state[0]
        runs, outs = pair(back, nested(list(state[2:2 + n_runs])), False)
        accs = [a + o for a, o in zip(state[2 + n_runs:], outs)]
        return (back + 2, alive(runs), *flat(runs), *accs)

    state = lax.while_loop(lambda state: (subs * i + subs - 1 - state[0] >= 0) & state[1], body,
                           (jnp.int32(2), alive(runs), *flat(runs), *accs))
    accs = state[2 + n_runs:]
    for u in range(subs):
        for p in range(n_heads // 2):
            sl = (0, slice(u * tq, (u + 1) * tq), slice(p * LANES, (p + 1) * LANES))
            o_ref[sl] = (accs[u * (n_heads // 2) + p] * _silu(g_ref[sl])).astype(o_ref.dtype)


def _sb(proj, *, tq=256, subs=2):
    b, t, _ = proj.shape
    rows = tq * subs
    n_heads = GROUP_WIDTH // HEAD_DIM
    wide = GROUP_WIDTH // LANES
    return pl.pallas_call(
        functools.partial(_sb_kernel, tq=tq),
        grid=(b, t // rows),
        in_specs=[pl.BlockSpec((1, rows, GROUP_WIDTH), lambda bb, i: (bb, i, COL_SQ // wide)),
                  pl.BlockSpec((1, t, GROUP_WIDTH), lambda bb, i: (bb, 0, COL_SK // wide)),
                  pl.BlockSpec((1, t, GROUP_WIDTH), lambda bb, i: (bb, 0, COL_SV // wide)),
                  pl.BlockSpec((1, rows, GROUP_WIDTH), lambda bb, i: (bb, i, COL_SG // wide))],
        out_specs=pl.BlockSpec((1, rows, GROUP_WIDTH), lambda bb, i: (bb, i, 0)),
        out_shape=jax.ShapeDtypeStruct((b, t, GROUP_WIDTH), _MXU_DTYPE),
        scratch_shapes=[pltpu.VMEM((n_heads // 2, t, LANES), _MXU_DTYPE),
                        pltpu.VMEM((n_heads, t, LANES), _MXU_DTYPE)],
        compiler_params=pltpu.CompilerParams(
            dimension_semantics=("arbitrary", "arbitrary"),
            vmem_limit_bytes=_VMEM_LIMIT),
        name="stickbreak",
    )(proj, proj, proj, proj)


def _swa_kernel(sink_ref, q_ref, k_ref, v_ref, g_ref, bias_ref, o_ref,
                k_sc, v_sc, *, tq):
    i = pl.program_id(1)
    t_len = k_ref.shape[1]
    w = SWA_WINDOW
    n_kv = k_sc.shape[0]

    @pl.when(i == 0)
    def _prep():
        kn = k_ref[0]
        v = v_ref[0]
        kn_swapped = pltpu.roll(kn, HEAD_DIM, axis=1)
        v_swapped = pltpu.roll(v, HEAD_DIM, axis=1)
        for kv in range(n_kv):
            keep = _own_half(kv, kn.shape)
            k_sc[kv, 0:w, :] = jnp.zeros((w, LANES), _MXU_DTYPE)
            v_sc[kv, 0:w, :] = jnp.zeros((w, LANES), _MXU_DTYPE)
            k_sc[kv, w:w + t_len, :] = jnp.where(keep, kn, kn_swapped).astype(_MXU_DTYPE)
            v_sc[kv, w:w + t_len, :] = jnp.where(keep, v, v_swapped).astype(_MXU_DTYPE)

    qi =lax.broadcasted_iota(jnp.int32, (w, 2 * w), 0)
    kj = lax.broadcasted_iota(jnp.int32, (w, 2 * w), 1)
    dist = qi + w - kj
    in_window = (dist >= 0) & (dist < w)
    n_sub = tq // w
    chains = [(u, kv, g) for u in range(n_sub) for kv in range(n_kv) for g in range(2)]
    rows = [pl.multiple_of((i * n_sub + u) * w, w) for u in range(n_sub)]
    ss = []
    for u, kv, g in chains:
        q_u = q_ref[0, u * w:(u + 1) * w, kv * LANES:(kv + 1) * LANES] * ATTN_SCALE
        q_h = jnp.where(_own_half(g, q_u.shape), q_u, 0.0).astype(_MXU_DTYPE)
        ss.append(lax.dot_general(q_h, k_sc[kv, pl.ds(rows[u], 2 * w), :], _NT,
                                  preferred_element_type=_F32))
    es, dens = [], []
    for (u, kv, g), s in zip(chains, ss):
        allowed = in_window & (kj + (i * n_sub + u - 1) * w >= 0)
        s = jnp.where(allowed, s + bias_ref[2 * kv + g], NEG_INF)
        sink = sink_ref[2 * kv + g]
        m = jnp.maximum(jnp.max(s, axis=-1, keepdims=True), sink)
        e = jnp.exp(s - m)
        dens.append(jnp.sum(e, axis=-1, keepdims=True) + jnp.exp(sink - m))
        es.append(e.astype(_MXU_DTYPE))
    outs = [jnp.dot(e, v_sc[kv, pl.ds(rows[u], 2 * w), :], preferred_element_type=_F32) / den
            for (u, kv, g), e, den in zip(chains, es, dens)]
    for u in range(n_sub):
        for kv in range(n_kv):
            c = chains.index((u, kv, 0))
            o = jnp.where(_own_half(0, (w, LANES)), outs[c], outs[c + 1])
            sl = (0, slice(u * w, (u + 1) * w), slice(kv * LANES, (kv + 1) * LANES))
            o_ref[sl] = (o * _silu(g_ref[sl])).astype(o_ref.dtype)


def _swa(proj, sinks, bias_tiles, *, tq=512):
    b, t, _ = proj.shape
    w = SWA_WINDOW
    n_heads = GROUP_WIDTH // HEAD_DIM
    wide = GROUP_WIDTH // LANES
    return pl.pallas_call(
        functools.partial(_swa_kernel, tq=tq),
        grid=(b, t // tq),
        in_specs=[pl.BlockSpec(memory_space=pltpu.SMEM),
                  pl.BlockSpec((1, tq, GROUP_WIDTH), lambda bb, i: (bb, i, COL_WQ // wide)),
                  pl.BlockSpec((1, t, LANES), lambda bb, i: (bb, 0, COL_WK)),
                  pl.BlockSpec((1, t, LANES), lambda bb, i: (bb, 0, COL_WV)),
                  pl.BlockSpec((1, tq, GROUP_WIDTH), lambda bb, i: (bb, i, COL_WG // wide)),
                  pl.BlockSpec((n_heads, w, 2 * w), lambda bb, i: (0, 0, 0))],
        out_specs=pl.BlockSpec((1, tq, GROUP_WIDTH), lambda bb, i: (bb, i, 0)),
        out_shape=jax.ShapeDtypeStruct((b, t, GROUP_WIDTH), _MXU_DTYPE),
        scratch_shapes=[pltpu.VMEM((n_heads // 2, t + w, LANES), _MXU_DTYPE),
                        pltpu.VMEM((n_heads // 2, t + w, LANES), _MXU_DTYPE)],
        compiler_params=pltpu.CompilerParams(
            dimension_semantics=("arbitrary", "arbitrary"),
            vmem_limit_bytes=_VMEM_LIMIT),
        name="swa",
    )(sinks, proj, proj, proj, proj, bias_tiles)


def _outproj_kernel(x_ref, ya_ref, yb_ref, yc_ref, yd_ref, w_ref, o_ref):
    acc = x_ref[...]
    for g, y_ref in enumerate((ya_ref, yb_ref, yc_ref, yd_ref)):
        acc = acc + jnp.dot(y_ref[...], w_ref[g * GROUP_WIDTH:(g + 1) * GROUP_WIDTH, :],
                            preferred_element_type=_F32)
    o_ref[...] = acc


def _outproj(x2d, ys, w_all, layer):
    m, d = x2d.shape
    tm = 512
    y_spec = pl.BlockSpec((tm, GROUP_WIDTH), lambda i: (i, 0))
    return pl.pallas_call(
        _outproj_kernel,
        grid=(m // tm,),
        in_specs=[pl.BlockSpec((tm, d), lambda i: (i, 0)), y_spec, y_spec, y_spec, y_spec,
                  pl.BlockSpec((None,) + w_all.shape[1:], lambda i: (layer, 0, 0))],
        out_specs=pl.BlockSpec((tm, d), lambda i: (i, 0)),
        out_shape=jax.ShapeDtypeStruct((m, d), _F32),
        compiler_params=pltpu.CompilerParams(
            dimension_semantics=("arbitrary",), vmem_limit_bytes=_VMEM_LIMIT),
        name="outproj",
    )(x2d, *ys, w_all)


def _rearrange_w_in(w_in):
    off = 3 * GROUP_WIDTH
    ff = w_in[..., off:off + FOX_FORGET_COLS]
    pad = jnp.zeros(w_in.shape[:-1] + (LANES - FOX_FORGET_COLS,), w_in.dtype)
    out = jnp.concatenate([w_in[..., :off], w_in[..., off + FOX_FORGET_COLS:], ff, pad], axis=-1)
    assert out.shape[-1] == PROJ_WIDTH
    return out


def _qk_logit_bound(qk_gain):
    g = jnp.abs(qk_gain.astype(_F32))
    return ATTN_SCALE * HEAD_DIM * jnp.max(g[0]) * jnp.max(g[1])


def _qk_gain_row(fox_g, moba_g, swa_g):
    def tiles(g, n):
        return jnp.tile(g.astype(_F32), 2 * n)
    row = jnp.zeros((PROJ_WIDTH,), _F32)
    for col, g, n in ((COL_FQ, fox_g[0], 2), (COL_FK, fox_g[1], 2),
                      (COL_MQ, moba_g[0], 2), (COL_MK, moba_g[1], 2),
                      (COL_WQ, swa_g[0], 2), (COL_WK, swa_g[1], 1)):
        row = lax.dynamic_update_slice(row, tiles(g, n), (col * LANES,))
    return row[None, :]


def kernel(x, norm_gain, w_in, b_forget, fox_qk_gain, moba_qk_gain, swa_qk_gain, sinks, w_out, rel_bias):
    b, t, d = x.shape
    depth = w_in.shape[0]
    w_in_r = _rearrange_w_in(w_in.astype(_MXU_DTYPE))
    w_out_c = w_out.astype(_MXU_DTYPE)
    moba_wide_bias = _moba_wide_bias_tiles(rel_bias)
    swa_bias = _swa_bias_tiles(rel_bias)
    moba_bias_max = jnp.max(jnp.abs(rel_bias[:, :rel_bias.shape[1] // 2].astype(_F32)))
    x2d = x.reshape(b * t, d)
    for layer in range(depth):
        qk_gain_row = _qk_gain_row(fox_qk_gain[layer], moba_qk_gain[layer], swa_qk_gain[layer])
        proj = _inproj(x2d, norm_gain[layer][None, :], w_in_r, layer, qk_gain_row)
        proj = proj.reshape(b, t, PROJ_WIDTH)
        bf_row = jnp.pad(b_forget[layer], (0, LANES - FOX_FORGET_COLS))[None, :]
        fox_bound = _qk_logit_bound(fox_qk_gain[layer])
        y_fox = lax.cond(2.0 * fox_bound <= BOUNDED_SOFTMAX_RANGE,
                         lambda: _fox_bounded(proj, b_forget[layer], fox_bound.reshape(1)),
                         lambda: _fox(proj, bf_row))
        moba_bound = _qk_logit_bound(moba_qk_gain[layer]) + moba_bias_max
        y_moba = lax.cond(2.0 * moba_bound <= BOUNDED_SOFTMAX_RANGE,
                          lambda: _moba_bounded(proj, rel_bias, moba_wide_bias,
                                                moba_bound.reshape(1)),
                          lambda: _moba(proj, rel_bias, _moba_bias_tiles(rel_bias)))
        y_sb = _sb(proj)
        y_swa = _swa(proj, sinks[layer], swa_bias)
        ys = [y.reshape(b * t, GROUP_WIDTH) for y in (y_fox, y_moba, y_sb, y_swa)]
        x2d = _outproj(x2d, ys, w_out_c, layer)
    return x2d.reshape(b, t, d)
```

```python
import functools
import math

import jax
import jax.numpy as jnp
from jax import lax
from jax.experimental import pallas as pl
from jax.experimental.pallas import tpu as pltpu

HEAD_DIM = 64
LANES = 128
GROUP_WIDTH = 256
MOBA_BLOCK = 256
MOBA_TOPK = 3
MOBA_MAX_BLOCKS = 16
SWA_WINDOW = 128
NUM_BUCKETS = 32
REL_MAX_DISTANCE = 1024
RMS_EPS = 1e-6
NEG_INF = -1e30
ATTN_SCALE = HEAD_DIM ** -0.5
LOG2E = math.log2(math.e)
BOUNDED_SOFTMAX_RANGE = 60.0
MOBA_NEAR_TILES = REL_MAX_DISTANCE // MOBA_BLOCK + 1
MOBA_WIDE = 2 * MOBA_BLOCK
MOBA_WIDE_NEAR_TILES = REL_MAX_DISTANCE // MOBA_WIDE + 1
FOX_FORGET_COLS = 4
FORGET_ROWS = 8
EXP_ZERO_CUTOFF = -104.0

COL_FQ, COL_FK, COL_FV, COL_FG = 0, 2, 4, 6
COL_MQ, COL_MK, COL_MV, COL_MG = 8, 10, 12, 14
COL_SQ, COL_SK, COL_SV, COL_SG = 16, 18, 20, 22
COL_WQ, COL_WK, COL_WV, COL_WG = 24, 26, 27, 28
COL_FF = 30
PROJ_WIDTH = (COL_FF + 1) * LANES
QK_NORM_COLS = (COL_FQ, COL_FQ + 1, COL_FK, COL_FK + 1, COL_MQ, COL_MQ + 1, COL_MK, COL_MK + 1,
                COL_WQ, COL_WQ + 1, COL_WK)

_MXU_DTYPE = jnp.bfloat16
_F32 = jnp.float32
_VMEM_LIMIT = 48 * 1024 * 1024
_VMEM_LIMIT_WIDE = 56 * 1024 * 1024

_NT = (((1,), (1,)), ((), ()))


def _lane_iota(shape):
    return lax.broadcasted_iota(jnp.int32, shape, len(shape) - 1)


def _pair_rms(x, gain_row):
    low = _lane_iota(x.shape) < HEAD_DIM
    sq = x * x
    ms_lo = jnp.sum(jnp.where(low, sq, 0.0), axis=-1, keepdims=True)
    ms_hi = jnp.sum(jnp.where(low, 0.0, sq), axis=-1, keepdims=True)
    ms = jnp.where(low, ms_lo, ms_hi) * (1.0 / HEAD_DIM)
    return x * lax.rsqrt(ms + RMS_EPS) * gain_row


def _silu(g):
    return g * (1.0 / (1.0 + jnp.exp(-g)))


def _split3(x):
    hi = x.astype(_MXU_DTYPE).astype(_F32)
    r = x - hi
    mid = r.astype(_MXU_DTYPE).astype(_F32)
    return hi, mid, r - mid


def _own_half(hh, shape):
    lane = _lane_iota(shape)
    return (lane < HEAD_DIM) if hh == 0 else (lane >= HEAD_DIM)


def _flash_update(s, v_tile, m, l, acc):
    m_new = jnp.maximum(m, jnp.max(s, axis=-1, keepdims=True))
    alpha = jnp.exp(m - m_new)
    p = jnp.exp(s - m_new)
    l = alpha * l + jnp.sum(p, axis=-1, keepdims=True)
    acc = alpha * acc + jnp.dot(p.astype(_MXU_DTYPE), v_tile, preferred_element_type=_F32)
    return m_new, l, acc


def _causal(tq):
    return (lax.broadcasted_iota(jnp.int32, (tq, tq), 0)
            >= lax.broadcasted_iota(jnp.int32, (tq, tq), 1))


def _inproj_kernel(x_ref, gain_ref, w_ref, qkg_ref, o_ref, *, tn):
    x = x_ref[...]
    ms = jnp.mean(x * x, axis=-1, keepdims=True)
    hn = (x * lax.rsqrt(ms + RMS_EPS) * gain_ref[...]).astype(_MXU_DTYPE)
    width = o_ref.shape[1]
    for n in range(pl.cdiv(width, tn)):
        hi = min((n + 1) * tn, width)
        acc = jnp.dot(hn, w_ref[:, n * tn:hi], preferred_element_type=_F32)
        for c in range(n * tn // LANES, hi // LANES):
            tile = acc[:, c * LANES - n * tn:(c + 1) * LANES - n * tn]
            if c in QK_NORM_COLS:
                tile = _pair_rms(tile, qkg_ref[:, c * LANES:(c + 1) * LANES])
            o_ref[:, c * LANES:(c + 1) * LANES] = tile


def _inproj(x2d, gain_row, w_all, layer, qk_gain_row):
    m, d = x2d.shape
    n = w_all.shape[2]
    tm = 512
    return pl.pallas_call(
        functools.partial(_inproj_kernel, tn=1024),
        grid=(m // tm,),
        in_specs=[pl.BlockSpec((tm, d), lambda i: (i, 0)),
                  pl.BlockSpec((1, d), lambda i: (0, 0)),
                  pl.BlockSpec((None, d, n), lambda i: (layer, 0, 0)),
                  pl.BlockSpec((1, n), lambda i: (0, 0))],
        out_specs=pl.BlockSpec((tm, n), lambda i: (i, 0)),
        out_shape=jax.ShapeDtypeStruct((m, n), _F32),
        compiler_params=pltpu.CompilerParams(
            dimension_semantics=("arbitrary",), vmem_limit_bytes=_VMEM_LIMIT),
        name="inproj",
    )(x2d, gain_row, w_all, qk_gain_row)


def _rel_bucket(dist):
    max_exact = NUM_BUCKETS // 2
    d = jnp.maximum(dist, 0)
    log_ratio = (jnp.log(jnp.maximum(d, 1).astype(_F32) / max_exact)
                 / math.log(REL_MAX_DISTANCE / max_exact))
    large = max_exact + (log_ratio * (NUM_BUCKETS - max_exact)).astype(jnp.int32)
    large = jnp.minimum(large, NUM_BUCKETS - 1)
    return jnp.where(d < max_exact, d, large)


def _bias_kernel(rb_ref, b_ref, o_ref, *, col0, scale):
    h = pl.program_id(0)
    for d in range(b_ref.shape[0]):
        buckets = b_ref[d]
        acc = jnp.zeros(buckets.shape, _F32)
        for k in range(NUM_BUCKETS):
            acc = jnp.where(buckets == k, rb_ref[k, col0 + h], acc)
        o_ref[0, d] = acc if scale == 1.0 else acc * scale


def _bias_lookup(rel_bias, buckets, col0, scale):
    n_heads = rel_bias.shape[1] // 2
    return pl.pallas_call(
        functools.partial(_bias_kernel, col0=col0, scale=scale),
        grid=(n_heads,),
        in_specs=[pl.BlockSpec(memory_space=pltpu.SMEM),
                  pl.BlockSpec(buckets.shape, lambda h: (0, 0, 0))],
        out_specs=pl.BlockSpec((1,) + buckets.shape, lambda h: (h, 0, 0, 0)),
        out_shape=jax.ShapeDtypeStruct((n_heads,) + buckets.shape, _F32),
        compiler_params=pltpu.CompilerParams(
            dimension_semantics=("arbitrary",), vmem_limit_bytes=_VMEM_LIMIT),
        name="bias_tiles",
    )(rel_bias, buckets)


def _toeplitz_buckets(size, count):
    i = jnp.arange(size)[:, None]
    j = jnp.arange(size)[None, :]
    return jnp.stack([_rel_bucket(d * size + i - j) for d in range(count)])


def _moba_bias_tiles(rel_bias):
    return _bias_lookup(rel_bias, _toeplitz_buckets(MOBA_BLOCK, MOBA_NEAR_TILES), 0, 1.0)


def _moba_wide_bias_tiles(rel_bias):
    return _bias_lookup(rel_bias, _toeplitz_buckets(MOBA_WIDE, MOBA_WIDE_NEAR_TILES), 0, LOG2E)


def _swa_bias_tiles(rel_bias):
    w = SWA_WINDOW
    buckets = _rel_bucket(jnp.arange(w)[:, None] + w - jnp.arange(2 * w)[None, :])
    return _bias_lookup(rel_bias, buckets[None], rel_bias.shape[1] // 2, 1.0)[:, 0]


def _forget_cumsum(ff_ref, bf_ref, c_sc):
    t_len = c_sc.shape[0]
    ff = ff_ref[0] + bf_ref[...]
    log_f = jnp.minimum(ff, 0.0) - jnp.log(1.0 + jnp.exp(-jnp.abs(ff)))
    parts = [part.astype(_MXU_DTYPE) for part in _split3(log_f)]
    ch = 256
    tri = (lax.broadcasted_iota(jnp.int32, (ch, ch), 0)
           >= lax.broadcasted_iota(jnp.int32, (ch, ch), 1)).astype(_MXU_DTYPE)
    carry = jnp.zeros((1, LANES), _F32)
    for r in range(t_len // ch):
        inc = carry
        for part in reversed(parts):
            inc = inc + jnp.dot(tri, part[r * ch:(r + 1) * ch], preferred_element_type=_F32)
        c_sc[r * ch:(r + 1) * ch, :] = inc
        carry = inc[ch - 1:ch, :]


def _forget_cumsum_rows(ff_ref, bf_ref, c_rows):
    n_rows, t_len = c_rows.shape
    z = ff_ref[0].T[0:n_rows, :] + bf_ref[...]
    log_f = jnp.minimum(z, 0.0) - jnp.log(1.0 + jnp.exp(-jnp.abs(z)))
    parts = [part.astype(_MXU_DTYPE) for part in _split3(log_f)]
    ch = 256
    tri = (lax.broadcasted_iota(jnp.int32, (ch, ch), 0)
           <= lax.broadcasted_iota(jnp.int32, (ch, ch), 1)).astype(_MXU_DTYPE)
    carry = jnp.zeros((n_rows, 1), _F32)
    for r in range(t_len // ch):
        inc = carry
        for part in reversed(parts):
            inc = inc + jnp.dot(part[:, r * ch:(r + 1) * ch], tri, preferred_element_type=_F32)
        c_rows[:, r * ch:(r + 1) * ch] = inc
        carry = inc[:, ch - 1:ch]


def _lane_column(x, col):
    return jnp.sum(jnp.where(_lane_iota(x.shape) == col, x, 0.0), axis=-1, keepdims=True)


def _lane_fields(shape, a0, fields):
    lane = _lane_iota(shape)
    out = jnp.zeros(shape, _F32)
    for n, f in enumerate(fields):
        out = jnp.where(lane == a0 + n, f, out)
    return out


def _write_ones_column(vext_sc, h, hh):
    @pl.when(pl.program_id(0) == 0)
    def _():
        lane = _lane_iota((vext_sc.shape[1], LANES))
        vext_sc[h, :, LANES:2 * LANES] = jnp.where(lane == hh, 1.0, 0.0).astype(_MXU_DTYPE)


def _bounded_weights(ps_ref, q_augs, kaug_sc, r, bias, diagonal):
    tq = q_augs[0].shape[0]
    if not diagonal:
        for h in range(len(q_augs)):
            s = lax.dot_general(q_augs[h], kaug_sc[h, pl.ds(r, tq), :], _NT,
                                preferred_element_type=_F32)
            if bias is not None:
                s = s + bias(h)
            ps_ref[h] = jnp.exp2(s).astype(_MXU_DTYPE)
        return
    half = tq // 2
    rows = lax.broadcasted_iota(jnp.int32, (tq, half), 0)
    cols = lax.broadcasted_iota(jnp.int32, (tq, half), 1)
    for h in range(len(q_augs)):
        b_h = None if bias is None else bias(h)
        s = lax.dot_general(q_augs[h], kaug_sc[h, pl.ds(r, half), :], _NT,
                            preferred_element_type=_F32)
        if b_h is not None:
            s = s + b_h[:, 0:half]
        ps_ref[h, :, 0:half] = jnp.exp2(jnp.where(rows >= cols, s, NEG_INF)).astype(_MXU_DTYPE)
        s = lax.dot_general(q_augs[h][half:, :], kaug_sc[h, pl.ds(r + half, half), :], _NT,
                            preferred_element_type=_F32)
        if b_h is not None:
            s = s + b_h[half:, half:]
        ps_ref[h, 0:half, half:] = jnp.zeros((half, half), _MXU_DTYPE)
        ps_ref[h, half:, half:] = jnp.exp2(jnp.where(_causal(half), s, NEG_INF)).astype(_MXU_DTYPE)


def _bounded_values(ps_ref, vext_sc, r, acc_sc):
    tq = ps_ref.shape[1]
    for p in range(acc_sc.shape[0]):
        acc_sc[p] += (
            jnp.dot(ps_ref[2 * p], vext_sc[2 * p, pl.ds(r, tq), :], preferred_element_type=_F32)
            + jnp.dot(ps_ref[2 * p + 1], vext_sc[2 * p + 1, pl.ds(r, tq), :],
                      preferred_element_type=_F32))


def _bounded_store(acc_sc, g_ref, o_ref):
    tq = acc_sc.shape[1]
    for p in range(acc_sc.shape[0]):
        acc = acc_sc[p]
        den = acc[:, LANES:]
        l = jnp.where(_own_half(0, (tq, LANES)), den[:, 0:1], den[:, 1:2])
        cols = slice(p * LANES, (p + 1) * LANES)
        o_ref[0, :, cols] = (acc[:, :LANES] / l * _silu(g_ref[0, :, cols])).astype(o_ref.dtype)


def _bounded_pipeline_step(n, ps_bufs, weights, values):
    def new_in(slot):
        def run():
            weights(ps_bufs[slot])
            values(ps_bufs[1 - slot])
        return run
    lax.cond(n % 2 == 0, new_in(0), new_in(1))


def _bounded_finish(n_done, ps_bufs, values):
    lax.cond(n_done % 2 == 0, lambda: values(ps_bufs[0]), lambda: values(ps_bufs[1]))


def _fox_extra_base(h):
    p, hh = divmod(h, 2)
    return (HEAD_DIM if hh == 0 else 0) + 8 * p


def _fox_extra_lanes(fields):
    rows = max(f.shape[1] for fs in fields for f in fs if hasattr(f, "shape"))
    sub = lax.broadcasted_iota(jnp.int32, (8, rows), 0)
    blocks = {}
    for h, fs in enumerate(fields):
        blk = jnp.zeros((8, rows), _F32)
        for n, f in enumerate(fs):
            blk = jnp.where(sub == n, f, blk)
        blocks[_fox_extra_base(h)] = blk
    pieces, at = [], 0
    for base in sorted(blocks):
        if base > at:
            pieces.append(jnp.zeros((base - at, rows), _F32))
        pieces.append(blocks[base])
        at = base + 8
    pieces.append(jnp.zeros((LANES - at, rows), _F32))
    return jnp.concatenate(pieces, axis=0).T


def _fox_bounded_kernel(bound_ref, q_ref, k_ref, v_ref, ff_ref, g_ref, bf_ref, o_ref,
                        kaug_sc, vext_sc, c_sc, ps_a, ps_b, acc_sc):
    i = pl.program_id(1)
    tq = q_ref.shape[1]
    n_heads = kaug_sc.shape[0]

    @pl.when(i == 0)
    def _prep():
        _forget_cumsum_rows(ff_ref, bf_ref, c_sc)
        hi, mid, lo = _split3(-(c_sc[...] * LOG2E))
        extra = _fox_extra_lanes([[hi[h:h + 1], mid[h:h + 1], lo[h:h + 1], 1.0, 1.0, 1.0, 1.0, 1.0]
                                  for h in range(n_heads)])
        lane = _lane_iota(extra.shape)
        for h in range(n_heads):
            p, hh = divmod(h, 2)
            kn = k_ref[0, :, p * LANES:(p + 1) * LANES]
            v = v_ref[0, :, p * LANES:(p + 1) * LANES]
            a0 = _fox_extra_base(h)
            aug = jnp.where((lane >= a0) & (lane < a0 + 8), extra, 0.0)
            own = _own_half(hh, kn.shape)
            kaug_sc[h] = jnp.where(own, kn, aug).astype(_MXU_DTYPE)
            vext_sc[h, :, 0:LANES] = jnp.where(own, v, 0.0).astype(_MXU_DTYPE)
            _write_ones_column(vext_sc, h, hh)

    row0 = pl.multiple_of(i * tq, tq)
    off = jnp.full((1, 1), -LOG2E, _F32) * bound_ref[0]
    off_hi = off.astype(_MXU_DTYPE).astype(_F32)
    hi, mid, lo = _split3(c_sc[:, pl.ds(row0, tq)] * LOG2E)
    extra = _fox_extra_lanes([[1.0, 1.0, 1.0, hi[h:h + 1], mid[h:h + 1], lo[h:h + 1],
                               off_hi, off - off_hi] for h in range(n_heads)])
    lane = _lane_iota(extra.shape)
    q_augs = []
    for h in range(n_heads):
        p, hh = divmod(h, 2)
        q = q_ref[0, :, p * LANES:(p + 1) * LANES] * (ATTN_SCALE * LOG2E)
        a0 = _fox_extra_base(h)
        aug = jnp.where((lane >= a0) & (lane < a0 + 8), extra, 0.0)
        q_augs.append(jnp.where(_own_half(hh, q.shape), q, aug).astype(_MXU_DTYPE))

    chunk_lane = _lane_iota((c_sc.shape[0], LANES))
    head_rows = lax.broadcasted_iota(jnp.int32, (c_sc.shape[0], 1), 0) < n_heads
    c_first = jnp.max(jnp.where(chunk_lane == 0, c_sc[:, pl.ds(row0, LANES)], -jnp.inf),
                      axis=-1, keepdims=True)
    cutoff = EXP_ZERO_CUTOFF - 2.0 * bound_ref[0]

    def live(j):
        start = pl.multiple_of(jnp.maximum(j, 0) * tq + (tq - LANES), LANES)
        c_last = jnp.max(jnp.where(chunk_lane == LANES - 1, c_sc[:, pl.ds(start, LANES)], -jnp.inf),
                         axis=-1, keepdims=True)
        gap = jnp.max(jnp.where(head_rows, c_first - c_last, -jnp.inf))
        return (j >= 0) & (gap >= cutoff)

    ps_bufs = (ps_a, ps_b)

    def values_at(r):
        return lambda ps_ref: _bounded_values(ps_ref, vext_sc, pl.multiple_of(r, tq), acc_sc)

    def body(state):
        j, _, r_prev, n = state
        r = pl.multiple_of(j * tq, tq)
        _bounded_pipeline_step(
            n, ps_bufs,
            lambda ps_ref: _bounded_weights(ps_ref, q_augs, kaug_sc, r, None, False),
            values_at(r_prev))
        return j - 1, live(j - 1), r, n + 1

    acc_sc[...] = jnp.zeros(acc_sc.shape, _F32)
    _bounded_weights(ps_a, q_augs, kaug_sc, row0, None, True)
    state = lax.while_loop(lambda state: state[1], body,
                           (i - 1, live(i - 1), row0, jnp.int32(1)))
    _bounded_finish(state[3] - 1, ps_bufs, values_at(state[2]))
    _bounded_store(acc_sc, g_ref, o_ref)


def _fox_bounded(proj, b_forget, bound, *, tq=512):
    b, t, _ = proj.shape
    bf_col = jnp.pad(b_forget.astype(_F32), (0, FORGET_ROWS - FOX_FORGET_COLS))[:, None]
    n_heads = GROUP_WIDTH // HEAD_DIM
    wide = GROUP_WIDTH // LANES
    return pl.pallas_call(
        _fox_bounded_kernel,
        grid=(b, t // tq),
        in_specs=[pl.BlockSpec(memory_space=pltpu.SMEM),
                  pl.BlockSpec((1, tq, GROUP_WIDTH), lambda bb, i: (bb, i, COL_FQ // wide)),
                  pl.BlockSpec((1, t, GROUP_WIDTH), lambda bb, i: (bb, 0, COL_FK // wide)),
                  pl.BlockSpec((1, t, GROUP_WIDTH), lambda bb, i: (bb, 0, COL_FV // wide)),
                  pl.BlockSpec((1, t, LANES), lambda bb, i: (bb, 0, COL_FF)),
                  pl.BlockSpec((1, tq, GROUP_WIDTH), lambda bb, i: (bb, i, COL_FG // wide)),
                  pl.BlockSpec((FORGET_ROWS, 1), lambda bb, i: (0, 0))],
        out_specs=pl.BlockSpec((1, tq, GROUP_WIDTH), lambda bb, i: (bb, i, 0)),
        out_shape=jax.ShapeDtypeStruct((b, t, GROUP_WIDTH), _MXU_DTYPE),
        scratch_shapes=[pltpu.VMEM((n_heads, t, LANES), _MXU_DTYPE),
                        pltpu.VMEM((n_heads, t, 2 * LANES), _MXU_DTYPE),
                        pltpu.VMEM((FORGET_ROWS, t), _F32),
                        pltpu.VMEM((n_heads, tq, tq), _MXU_DTYPE),
                        pltpu.VMEM((n_heads, tq, tq), _MXU_DTYPE),
                        pltpu.VMEM((n_heads // 2, tq, 2 * LANES), _F32)],
        compiler_params=pltpu.CompilerParams(
            dimension_semantics=("arbitrary", "arbitrary"),
            vmem_limit_bytes=_VMEM_LIMIT_WIDE),
        name="fox_bounded",
    )(bound, proj, proj, proj, proj, proj, bf_col)


def _fox_kernel(q_ref, k_ref, v_ref, ff_ref, g_ref, bf_ref, o_ref,
                kaug_sc, v_sc, c_sc, *, tq):
    p = pl.program_id(1)
    i = pl.program_id(2)

    def head_column(x, hh):
        return _lane_column(x, 2 * p + hh)

    @pl.when(i == 0)
    def _prep():
        pl.when(p == 0)(lambda: _forget_cumsum(ff_ref, bf_ref, c_sc))
        kn = k_ref[0]
        c_all = c_sc[...]
        for hh in range(2):
            a0 = HEAD_DIM if hh == 0 else 0
            hi, mid, lo = _split3(-head_column(c_all, hh))
            aug = _lane_fields(kn.shape, a0, [hi, mid, lo, 1.0, 1.0, 1.0])
            kaug_sc[hh] = jnp.where(_own_half(hh, kn.shape), kn, aug).astype(_MXU_DTYPE)
        v_sc[...] = v_ref[0].astype(_MXU_DTYPE)

    q = q_ref[0] * ATTN_SCALE
    row0 = pl.multiple_of(i * tq, tq)
    c_t = c_sc[pl.ds(row0, tq), :]
    outs = []
    for hh in range(2):
        a0 = HEAD_DIM if hh == 0 else 0
        hi, mid, lo = _split3(head_column(c_t, hh))
        aug = _lane_fields(q.shape, a0, [1.0, 1.0, 1.0, hi, mid, lo])
        q_aug = jnp.where(_own_half(hh, q.shape), q, aug).astype(_MXU_DTYPE)

        s = lax.dot_general(q_aug, kaug_sc[hh, pl.ds(row0, tq), :], _NT,
                            preferred_element_type=_F32)
        s = jnp.where(_causal(tq), s, NEG_INF)
        m = jnp.max(s, axis=-1, keepdims=True)
        pr = jnp.exp(s - m)
        l = jnp.sum(pr, axis=-1, keepdims=True)
        acc = jnp.dot(pr.astype(_MXU_DTYPE), v_sc[pl.ds(row0, tq), :], preferred_element_type=_F32)

        def body(j, carry, q_aug=q_aug, hh=hh):
            r = pl.multiple_of(j * tq, tq)
            s = lax.dot_general(q_aug, kaug_sc[hh, pl.ds(r, tq), :], _NT,
                                preferred_element_type=_F32)
            return _flash_update(s, v_sc[pl.ds(r, tq), :], *carry)

        m, l, acc = lax.fori_loop(0, i, body, (m, l, acc))
        outs.append(acc / l)
    o = jnp.where(_own_half(0, outs[0].shape), outs[0], outs[1])
    o_ref[0] = (o * _silu(g_ref[0])).astype(o_ref.dtype)


def _fox(proj, bf_row, *, tq=256):
    b, t, _ = proj.shape
    return pl.pallas_call(
        functools.partial(_fox_kernel, tq=tq),
        grid=(b, 2, t // tq),
        in_specs=[pl.BlockSpec((1, tq, LANES), lambda bb, p, i: (bb, i, COL_FQ + p)),
                  pl.BlockSpec((1, t, LANES), lambda bb, p, i: (bb, 0, COL_FK + p)),
                  pl.BlockSpec((1, t, LANES), lambda bb, p, i: (bb, 0, COL_FV + p)),
                  pl.BlockSpec((1, t, LANES), lambda bb, p, i: (bb, 0, COL_FF)),
                  pl.BlockSpec((1, tq, LANES), lambda bb, p, i: (bb, i, COL_FG + p)),
                  pl.BlockSpec((1, LANES), lambda bb, p, i: (0, 0))],
        out_specs=pl.BlockSpec((1, tq, LANES), lambda bb, p, i: (bb, i, p)),
        out_shape=jax.ShapeDtypeStruct((b, t, GROUP_WIDTH), _MXU_DTYPE),
        scratch_shapes=[pltpu.VMEM((2, t, LANES), _MXU_DTYPE),
                        pltpu.VMEM((t, LANES), _MXU_DTYPE),
                        pltpu.VMEM((t, LANES), _F32)],
        compiler_params=pltpu.CompilerParams(
            dimension_semantics=("arbitrary", "arbitrary", "arbitrary"),
            vmem_limit_bytes=_VMEM_LIMIT),
        name="fox",
    )(proj, proj, proj, proj, proj, bf_row)


def _moba_block_means(kn, kmean_sc):
    kmean_sc[...] = jnp.zeros(kmean_sc.shape, _F32)
    for n in range(kn.shape[0] // MOBA_BLOCK):
        kmean_sc[n:n + 1, :] = jnp.mean(kn[n * MOBA_BLOCK:(n + 1) * MOBA_BLOCK], axis=0, keepdims=True)


def _moba_select(q_head, kmean, past):
    gate = lax.dot_general(q_head, kmean, _NT, precision=lax.Precision.HIGHEST,
                           preferred_element_type=_F32)
    lane_f = _lane_iota(gate.shape).astype(_F32)
    cand = jnp.where(past, gate, -jnp.inf)
    sel = jnp.zeros(gate.shape, _F32)
    for _ in range(MOBA_TOPK):
        mx = jnp.max(cand, axis=-1, keepdims=True)
        is_max = (cand == mx) & (mx > -jnp.inf)
        first = jnp.min(jnp.where(is_max, lane_f, float(LANES)), axis=-1, keepdims=True)
        pick = lane_f == first
        sel = jnp.where(pick, 1.0, sel)
        cand = jnp.where(pick, -jnp.inf, cand)
    return sel


def _moba_bounded_kernel(rb_ref, bound_ref, q_ref, k_ref, v_ref, g_ref, bias_ref, o_ref,
                         kaug_sc, vext_sc, kmean_sc, ps_a, ps_b, acc_sc):
    i = pl.program_id(1)
    tq = q_ref.shape[1]
    n_heads = kaug_sc.shape[0]
    blk = MOBA_BLOCK
    near = MOBA_WIDE_NEAR_TILES

    @pl.when(i == 0)
    def _prep():
        lane = _lane_iota((k_ref.shape[1], LANES))
        row_blk = lax.broadcasted_iota(jnp.int32, lane.shape, 0) // blk
        for h in range(n_heads):
            p, hh = divmod(h, 2)
            kn = k_ref[0, :, p * LANES:(p + 1) * LANES]
            v = v_ref[0, :, p * LANES:(p + 1) * LANES]
            if hh == 0:
                _moba_block_means(kn, kmean_sc.at[p])
            a0 = HEAD_DIM if hh == 0 else 0
            aug = jnp.where((lane - a0 == row_blk) | (lane - a0 - MOBA_MAX_BLOCKS == row_blk)
                            | (lane - a0 == 2 * MOBA_MAX_BLOCKS)
                            | (lane - a0 == 2 * MOBA_MAX_BLOCKS + 1), 1.0, 0.0)
            own = _own_half(hh, kn.shape)
            kaug_sc[h] = jnp.where(own, kn, aug).astype(_MXU_DTYPE)
            vext_sc[h, :, 0:LANES] = jnp.where(own, v, 0.0).astype(_MXU_DTYPE)
            _write_ones_column(vext_sc, h, hh)

    row0 = pl.multiple_of(i * tq, tq)
    off = jnp.full((1, 1), -LOG2E, _F32) * bound_ref[0]
    off_hi = off.astype(_MXU_DTYPE).astype(_F32)
    blk_n = lax.broadcasted_iota(jnp.int32, (MOBA_MAX_BLOCKS, tq), 0)
    blk_f = blk_n.astype(_F32)
    q_blk = i * (tq // blk) + lax.broadcasted_iota(jnp.int32, (MOBA_MAX_BLOCKS, tq), 1) // blk
    past = blk_n < q_blk
    far = i - blk_n // 2 >= near
    sub8 = lax.broadcasted_iota(jnp.int32, (8, tq), 0)
    off_rows = jnp.where(sub8 == 0, off_hi, jnp.where(sub8 == 1, off - off_hi, 0.0))
    gates = []
    for p in range(n_heads // 2):
        means = kmean_sc[p, 0:MOBA_MAX_BLOCKS, :]
        first = _own_half(0, means.shape)
        stacked = jnp.concatenate([jnp.where(first, means, 0.0), jnp.where(first, 0.0, means)], axis=0)
        gates.append(lax.dot_general(stacked, q_ref[0, :, p * LANES:(p + 1) * LANES], _NT,
                                     precision=lax.Precision.HIGHEST, preferred_element_type=_F32))
    q_augs = []
    for h in range(n_heads):
        p, hh = divmod(h, 2)
        qn = q_ref[0, :, p * LANES:(p + 1) * LANES]
        own = _own_half(hh, qn.shape)
        cand = jnp.where(past, gates[p][MOBA_MAX_BLOCKS * hh:MOBA_MAX_BLOCKS * (hh + 1), :], -jnp.inf)
        sel = jnp.zeros((MOBA_MAX_BLOCKS, tq), _F32)
        for _ in range(MOBA_TOPK):
            mx = jnp.max(cand, axis=0, keepdims=True)
            is_max = (cand == mx) & (mx > -jnp.inf)
            first = jnp.min(jnp.where(is_max, blk_f, float(LANES)), axis=0, keepdims=True)
            pick = blk_f == first
            sel = jnp.where(pick, 1.0, sel)
            cand = jnp.where(pick, -jnp.inf, cand)
        c_far = jnp.full((1, 1), LOG2E, _F32) * rb_ref[NUM_BUCKETS - 1, h]
        far_hi = c_far.astype(_MXU_DTYPE).astype(_F32)
        pen = jnp.where((blk_n == q_blk) | (past & (sel != 0.0)), 0.0, NEG_INF)
        fields = jnp.concatenate([pen + jnp.where(far, far_hi, 0.0),
                                  jnp.where(far, c_far - far_hi, 0.0),
                                  off_rows,
                                  jnp.zeros((HEAD_DIM - 2 * MOBA_MAX_BLOCKS - 8, tq), _F32)],
                                 axis=0)
        blank = jnp.zeros((HEAD_DIM, tq), _F32)
        aug = jnp.concatenate([blank, fields] if hh == 0 else [fields, blank], axis=0).T
        q_augs.append(jnp.where(own, qn * (ATTN_SCALE * LOG2E), aug).astype(_MXU_DTYPE))

    ps_bufs = (ps_a, ps_b)

    def values_at(r):
        return lambda ps_ref: _bounded_values(ps_ref, vext_sc, pl.multiple_of(r, tq), acc_sc)

    def body(j, state, with_bias):
        r_prev, n = state
        r = pl.multiple_of(j * tq, tq)
        bias = (lambda h: bias_ref[h, i - j]) if with_bias else None
        _bounded_pipeline_step(
            n, ps_bufs,
            lambda ps_ref: _bounded_weights(ps_ref, q_augs, kaug_sc, r, bias, False),
            values_at(r_prev))
        return r, n + 1

    acc_sc[...] = jnp.zeros(acc_sc.shape, _F32)
    _bounded_weights(ps_a, q_augs, kaug_sc, row0, lambda h: bias_ref[h, 0], True)
    state = (row0, jnp.int32(1))
    near_lo = jnp.maximum(i - (near - 1), 0)
    state = lax.fori_loop(near_lo, i, functools.partial(body, with_bias=True), state)
    state = lax.fori_loop(0, near_lo, functools.partial(body, with_bias=False), state)
    _bounded_finish(state[1] - 1, ps_bufs, values_at(state[0]))
    _bounded_store(acc_sc, g_ref, o_ref)


def _moba_bounded(proj, rel_bias, bias_tiles, bound):
    b, t, _ = proj.shape
    tq = MOBA_WIDE
    assert t % tq == 0 and t // MOBA_BLOCK <= MOBA_MAX_BLOCKS
    n_heads = GROUP_WIDTH // HEAD_DIM
    wide = GROUP_WIDTH // LANES
    resident = pl.Buffered(1)
    return pl.pallas_call(
        _moba_bounded_kernel,
        grid=(b, t // tq),
        in_specs=[pl.BlockSpec(memory_space=pltpu.SMEM),
                  pl.BlockSpec(memory_space=pltpu.SMEM),
                  pl.BlockSpec((1, tq, GROUP_WIDTH), lambda bb, i: (bb, i, COL_MQ // wide)),
                  pl.BlockSpec((1, t, GROUP_WIDTH), lambda bb, i: (bb, 0, COL_MK // wide)),
                  pl.BlockSpec((1, t, GROUP_WIDTH), lambda bb, i: (bb, 0, COL_MV // wide)),
                  pl.BlockSpec((1, tq, GROUP_WIDTH), lambda bb, i: (bb, i, COL_MG // wide)),
                  pl.BlockSpec((n_heads, MOBA_WIDE_NEAR_TILES, tq, tq), lambda bb, i: (0, 0, 0, 0),
                               pipeline_mode=resident)],
        out_specs=pl.BlockSpec((1, tq, GROUP_WIDTH), lambda bb, i: (bb, i, 0)),
        out_shape=jax.ShapeDtypeStruct((b, t, GROUP_WIDTH), _MXU_DTYPE),
        scratch_shapes=[pltpu.VMEM((n_heads, t, LANES), _MXU_DTYPE),
                        pltpu.VMEM((n_heads, t, 2 * LANES), _MXU_DTYPE),
                        pltpu.VMEM((n_heads // 2, LANES, LANES), _F32),
                        pltpu.VMEM((n_heads, tq, tq), _MXU_DTYPE),
                        pltpu.VMEM((n_heads, tq, tq), _MXU_DTYPE),
                        pltpu.VMEM((n_heads // 2, tq, 2 * LANES), _F32)],
        compiler_params=pltpu.CompilerParams(
            dimension_semantics=("arbitrary", "arbitrary"),
            vmem_limit_bytes=_VMEM_LIMIT_WIDE),
        name="moba_bounded",
    )(rel_bias, bound, proj, proj, proj, proj, bias_tiles)


def _moba_kernel(rb_ref, q_ref, k_ref, v_ref, g_ref, bias_ref, o_ref,
                 kaug_sc, v_sc, kmean_sc):
    p = pl.program_id(1)
    i = pl.program_id(2)
    blk = MOBA_BLOCK
    near = MOBA_NEAR_TILES

    @pl.when(i == 0)
    def _prep():
        kn = k_ref[0]
        _moba_block_means(kn, kmean_sc)
        lane = _lane_iota(kn.shape)
        row_blk = lax.broadcasted_iota(jnp.int32, kn.shape, 0) // blk
        for hh in range(2):
            a0 = HEAD_DIM if hh == 0 else 0
            onehot = jnp.where((lane - a0 == row_blk) | (lane - a0 - MOBA_MAX_BLOCKS == row_blk),
                               1.0, 0.0)
            kaug_sc[hh] = jnp.where(_own_half(hh, kn.shape), kn, onehot).astype(_MXU_DTYPE)
        v_sc[...] = v_ref[0].astype(_MXU_DTYPE)

    qn = q_ref[0]
    row0 = pl.multiple_of(i * blk, blk)
    lane = _lane_iota(qn.shape)
    past = lane < i
    outs = []
    for hh in range(2):
        own = _own_half(hh, qn.shape)
        sel = _moba_select(jnp.where(own, qn, 0.0), kmean_sc[...], past)
        c_far = jnp.full((1, LANES), rb_ref[NUM_BUCKETS - 1, 2 * p + hh], _F32)
        far_hi = c_far.astype(_MXU_DTYPE).astype(_F32)
        far_lo = c_far - far_hi
        pen = jnp.where(past & (sel == 0.0), NEG_INF, 0.0)
        aug = jnp.where(lane < MOBA_MAX_BLOCKS, pen + jnp.where(i - lane >= near, far_hi, 0.0),
                        jnp.where((lane < 2 * MOBA_MAX_BLOCKS)
                                  & (i - (lane - MOBA_MAX_BLOCKS) >= near), far_lo, 0.0))
        if hh == 0:
            aug = pltpu.roll(aug, HEAD_DIM, axis=1)
        q_aug = jnp.where(own, qn * ATTN_SCALE, aug).astype(_MXU_DTYPE)

        def scores(r, q_aug=q_aug, hh=hh):
            return lax.dot_general(q_aug, kaug_sc[hh, pl.ds(r, blk), :], _NT,
                                   preferred_element_type=_F32)

        s = scores(row0) + bias_ref[hh, 0]
        s = jnp.where(_causal(blk), s, NEG_INF)
        m = jnp.max(s, axis=-1, keepdims=True)
        pr = jnp.exp(s - m)
        l = jnp.sum(pr, axis=-1, keepdims=True)
        acc = jnp.dot(pr.astype(_MXU_DTYPE), v_sc[pl.ds(row0, blk), :], preferred_element_type=_F32)

        def near_body(j, carry, hh=hh, scores=scores):
            r = pl.multiple_of(j * blk, blk)
            return _flash_update(scores(r) + bias_ref[hh, i - j], v_sc[pl.ds(r, blk), :], *carry)

        def far_body(j, carry, scores=scores):
            r = pl.multiple_of(j * blk, blk)
            return _flash_update(scores(r), v_sc[pl.ds(r, blk), :], *carry)

        near_lo = jnp.maximum(i - (near - 1), 0)
        carry = lax.fori_loop(near_lo, i, near_body, (m, l, acc))
        m, l, acc = lax.fori_loop(0, near_lo, far_body, carry)
        outs.append(acc / l)
    o = jnp.where(_own_half(0, outs[0].shape), outs[0], outs[1])
    o_ref[0] = (o * _silu(g_ref[0])).astype(o_ref.dtype)


def _moba(proj, rel_bias, bias_tiles):
    b, t, _ = proj.shape
    blk = MOBA_BLOCK
    assert t % blk == 0 and t // blk <= MOBA_MAX_BLOCKS
    return pl.pallas_call(
        _moba_kernel,
        grid=(b, 2, t // blk),
        in_specs=[pl.BlockSpec(memory_space=pltpu.SMEM),
                  pl.BlockSpec((1, blk, LANES), lambda bb, p, i: (bb, i, COL_MQ + p)),
                  pl.BlockSpec((1, t, LANES), lambda bb, p, i: (bb, 0, COL_MK + p)),
                  pl.BlockSpec((1, t, LANES), lambda bb, p, i: (bb, 0, COL_MV + p)),
                  pl.BlockSpec((1, blk, LANES), lambda bb, p, i: (bb, i, COL_MG + p)),
                  pl.BlockSpec((2, MOBA_NEAR_TILES, blk, blk), lambda bb, p, i: (p, 0, 0, 0))],
        out_specs=pl.BlockSpec((1, blk, LANES), lambda bb, p, i: (bb, i, p)),
        out_shape=jax.ShapeDtypeStruct((b, t, GROUP_WIDTH), _MXU_DTYPE),
        scratch_shapes=[pltpu.VMEM((2, t, LANES), _MXU_DTYPE),
                        pltpu.VMEM((t, LANES), _MXU_DTYPE),
                        pltpu.VMEM((LANES, LANES), _F32)],
        compiler_params=pltpu.CompilerParams(
            dimension_semantics=("arbitrary", "arbitrary", "arbitrary"),
            vmem_limit_bytes=_VMEM_LIMIT),
        name="moba",
    )(rel_bias, proj, proj, proj, proj, bias_tiles)


def _sb_kernel(q_ref, k_ref, v_ref, g_ref, o_ref, k_sc, v_sc, *, tq):
    i = pl.program_id(1)
    n_heads = v_sc.shape[0]

    @pl.when(i == 0)
    def _prep():
        for p in range(n_heads // 2):
            k_sc[p] = k_ref[0, :, p * LANES:(p + 1) * LANES].astype(_MXU_DTYPE)
            v = v_ref[0, :, p * LANES:(p + 1) * LANES]
            for hh in range(2):
                v_sc[2 * p + hh] = jnp.where(_own_half(hh, v.shape), v, 0.0).astype(_MXU_DTYPE)

    subs = q_ref.shape[1] // tq
    q_heads = []
    for u in range(subs):
        q_heads.append([])
        for h in range(n_heads):
            p, hh = divmod(h, 2)
            q = q_ref[0, u * tq:(u + 1) * tq, p * LANES:(p + 1) * LANES] * (ATTN_SCALE * LOG2E)
            q_heads[u].append(jnp.where(_own_half(hh, q.shape), q, 0.0).astype(_MXU_DTYPE))
    strict = (lax.broadcasted_iota(jnp.int32, (tq, tq), 0)
              > lax.broadcasted_iota(jnp.int32, (tq, tq), 1))
    after = strict.astype(_MXU_DTYPE)
    sign_bit = jnp.uint32(0x80000000)

    def pair(back, runs, diagonal):
        chains = [(u, t, h) for u in range(subs) for t in range(2) for h in range(n_heads)]
        tile_idx = [[subs * i + u - back - t for t in range(2)] for u in range(subs)]
        valid = [[(j >= 0).astype(_F32) for j in js] for js in tile_idx]
        rows = [[pl.multiple_of(jnp.maximum(j, 0) * tq, tq) for j in js] for js in tile_idx]
        masked = [diagonal and t == 0 for _, t, _ in chains]
        zs = [lax.dot_general(q_heads[u][h], k_sc[h // 2, pl.ds(rows[u][t], tq), :], _NT,
                              preferred_element_type=_F32) for u, t, h in chains]
        drops = []
        for z, msk in zip(zs, masked):
            neg_abs = lax.bitcast_convert_type(lax.bitcast_convert_type(z, jnp.uint32) | sign_bit, _F32)
            drop = jnp.maximum(z, 0.0) + jnp.log(1.0 + jnp.exp2(neg_abs)) * LOG2E
            drops.append(jnp.where(strict, drop, 0.0) if msk else drop)
        laters = []
        for drop in drops:
            hi = drop.astype(_MXU_DTYPE)
            lo = (drop - hi.astype(_F32)).astype(_MXU_DTYPE)
            laters.append(jnp.dot(hi, after, preferred_element_type=_F32)
                          + jnp.dot(lo, after, preferred_element_type=_F32))
        pvs = []
        for (u, t, h), z, drop, later, msk in zip(chains, zs, drops, laters, masked):
            w = jnp.exp2(z - drop - later)
            if msk:
                w = jnp.where(strict, w, 0.0)
            pvs.append(jnp.dot(w.astype(_MXU_DTYPE), v_sc[h, pl.ds(rows[u][t], tq), :],
                               preferred_element_type=_F32))
        sums = [jnp.sum(drop, axis=-1, keepdims=True) for drop in drops]
        new_runs, outs = [], []
        for u in range(subs):
            new_runs.append([])
            for p in range(n_heads // 2):
                out = None
                for h in (2 * p, 2 * p + 1):
                    run = runs[u][h]
                    for t in range(2):
                        c = chains.index((u, t, h))
                        pv = pvs[c] * (jnp.exp2(run) * valid[u][t])
                        out = pv if out is None else out + pv
                        run = run - sums[c] * valid[u][t]
                    new_runs[u].append(run)
                outs.append(out)
        return new_runs, outs

    def alive(runs):
        top = functools.reduce(jnp.maximum, [r for rs in runs for r in rs])
        return jnp.max(top) >= EXP_ZERO_CUTOFF * LOG2E

    def flat(runs):
        return [r for rs in runs for r in rs]

    def nested(flat_runs):
        return [flat_runs[n_heads * u:n_heads * (u + 1)] for u in range(subs)]

    n_runs = n_heads * subs
    zero = jnp.zeros((tq, 1), _F32)
    runs, accs = pair(0, [[zero] * n_heads for _ in range(subs)], True)

    def body(state):
        back = state[0]
        runs, outs = pair(back, nested(list(state[2:2 + n_runs])), False)
        accs = [a + o for a, o in zip(state[2 + n_runs:], outs)]
        return (back + 2, alive(runs), *flat(runs), *accs)

    state = lax.while_loop(lambda state: (subs * i + subs - 1 - state[0] >= 0) & state[1], body,
                           (jnp.int32(2), alive(runs), *flat(runs), *accs))
    accs = state[2 + n_runs:]
    for u in range(subs):
        for p in range(n_heads // 2):
            sl = (0, slice(u * tq, (u + 1) * tq), slice(p * LANES, (p + 1) * LANES))
            o_ref[sl] = (accs[u * (n_heads // 2) + p] * _silu(g_ref[sl])).astype(o_ref.dtype)


def _sb(proj, *, tq=256, subs=2):
    b, t, _ = proj.shape
    rows = tq * subs
    n_heads = GROUP_WIDTH // HEAD_DIM
    wide = GROUP_WIDTH // LANES
    return pl.pallas_call(
        functools.partial(_sb_kernel, tq=tq),
        grid=(b, t // rows),
        in_specs=[pl.BlockSpec((1, rows, GROUP_WIDTH), lambda bb, i: (bb, i, COL_SQ // wide)),
                  pl.BlockSpec((1, t, GROUP_WIDTH), lambda bb, i: (bb, 0, COL_SK // wide)),
                  pl.BlockSpec((1, t, GROUP_WIDTH), lambda bb, i: (bb, 0, COL_SV // wide)),
                  pl.BlockSpec((1, rows, GROUP_WIDTH), lambda bb, i: (bb, i, COL_SG // wide))],
        out_specs=pl.BlockSpec((1, rows, GROUP_WIDTH), lambda bb, i: (bb, i, 0)),
        out_shape=jax.ShapeDtypeStruct((b, t, GROUP_WIDTH), _MXU_DTYPE),
        scratch_shapes=[pltpu.VMEM((n_heads // 2, t, LANES), _MXU_DTYPE),
                        pltpu.VMEM((n_heads, t, LANES), _MXU_DTYPE)],
        compiler_params=pltpu.CompilerParams(
            dimension_semantics=("arbitrary", "arbitrary"),
            vmem_limit_bytes=_VMEM_LIMIT),
        name="stickbreak",
    )(proj, proj, proj, proj)


def _swa_kernel(sink_ref, q_ref, k_ref, v_ref, g_ref, bias_ref, o_ref,
                k_sc, v_sc, *, tq):
    i = pl.program_id(1)
    t_len = k_ref.shape[1]
    w = SWA_WINDOW
    n_kv = k_sc.shape[0]

    @pl.when(i == 0)
    def _prep():
        kn = k_ref[0]
        v = v_ref[0]
        kn_swapped = pltpu.roll(kn, HEAD_DIM, axis=1)
        v_swapped = pltpu.roll(v, HEAD_DIM, axis=1)
        for kv in range(n_kv):
            keep = _own_half(kv, kn.shape)
            k_sc[kv, 0:w, :] = jnp.zeros((w, LANES), _MXU_DTYPE)
            v_sc[kv, 0:w, :] = jnp.zeros((w, LANES), _MXU_DTYPE)
            k_sc[kv, w:w + t_len, :] = jnp.where(keep, kn, kn_swapped).astype(_MXU_DTYPE)
            v_sc[kv, w:w + t_len, :] = jnp.where(keep, v, v_swapped).astype(_MXU_DTYPE)

    qi =lax.broadcasted_iota(jnp.int32, (w, 2 * w), 0)
    kj = lax.broadcasted_iota(jnp.int32, (w, 2 * w), 1)
    dist = qi + w - kj
    in_window = (dist >= 0) & (dist < w)
    n_sub = tq // w
    chains = [(u, kv, g) for u in range(n_sub) for kv in range(n_kv) for g in range(2)]
    rows = [pl.multiple_of((i * n_sub + u) * w, w) for u in range(n_sub)]
    ss = []
    for u, kv, g in chains:
        q_u = q_ref[0, u * w:(u + 1) * w, kv * LANES:(kv + 1) * LANES] * ATTN_SCALE
        q_h = jnp.where(_own_half(g, q_u.shape), q_u, 0.0).astype(_MXU_DTYPE)
        ss.append(lax.dot_general(q_h, k_sc[kv, pl.ds(rows[u], 2 * w), :], _NT,
                                  preferred_element_type=_F32))
    es, dens = [], []
    for (u, kv, g), s in zip(chains, ss):
        allowed = in_window & (kj + (i * n_sub + u - 1) * w >= 0)
        s = jnp.where(allowed, s + bias_ref[2 * kv + g], NEG_INF)
        sink = sink_ref[2 * kv + g]
        m = jnp.maximum(jnp.max(s, axis=-1, keepdims=True), sink)
        e = jnp.exp(s - m)
        dens.append(jnp.sum(e, axis=-1, keepdims=True) + jnp.exp(sink - m))
        es.append(e.astype(_MXU_DTYPE))
    outs = [jnp.dot(e, v_sc[kv, pl.ds(rows[u], 2 * w), :], preferred_element_type=_F32) / den
            for (u, kv, g), e, den in zip(chains, es, dens)]
    for u in range(n_sub):
        for kv in range(n_kv):
            c = chains.index((u, kv, 0))
            o = jnp.where(_own_half(0, (w, LANES)), outs[c], outs[c + 1])
            sl = (0, slice(u * w, (u + 1) * w), slice(kv * LANES, (kv + 1) * LANES))
            o_ref[sl] = (o * _silu(g_ref[sl])).astype(o_ref.dtype)


def _swa(proj, sinks, bias_tiles, *, tq=512):
    b, t, _ = proj.shape
    w = SWA_WINDOW
    n_heads = GROUP_WIDTH // HEAD_DIM
    wide = GROUP_WIDTH // LANES
    return pl.pallas_call(
        functools.partial(_swa_kernel, tq=tq),
        grid=(b, t // tq),
        in_specs=[pl.BlockSpec(memory_space=pltpu.SMEM),
                  pl.BlockSpec((1, tq, GROUP_WIDTH), lambda bb, i: (bb, i, COL_WQ // wide)),
                  pl.BlockSpec((1, t, LANES), lambda bb, i: (bb, 0, COL_WK)),
                  pl.BlockSpec((1, t, LANES), lambda bb, i: (bb, 0, COL_WV)),
                  pl.BlockSpec((1, tq, GROUP_WIDTH), lambda bb, i: (bb, i, COL_WG // wide)),
                  pl.BlockSpec((n_heads, w, 2 * w), lambda bb, i: (0, 0, 0))],
        out_specs=pl.BlockSpec((1, tq, GROUP_WIDTH), lambda bb, i: (bb, i, 0)),
        out_shape=jax.ShapeDtypeStruct((b, t, GROUP_WIDTH), _MXU_DTYPE),
        scratch_shapes=[pltpu.VMEM((n_heads // 2, t + w, LANES), _MXU_DTYPE),
                        pltpu.VMEM((n_heads // 2, t + w, LANES), _MXU_DTYPE)],
        compiler_params=pltpu.CompilerParams(
            dimension_semantics=("arbitrary", "arbitrary"),
            vmem_limit_bytes=_VMEM_LIMIT),
        name="swa",
    )(sinks, proj, proj, proj, proj, bias_tiles)


def _outproj_kernel(x_ref, ya_ref, yb_ref, yc_ref, yd_ref, w_ref, o_ref):
    acc = x_ref[...]
    for g, y_ref in enumerate((ya_ref, yb_ref, yc_ref, yd_ref)):
        acc = acc + jnp.dot(y_ref[...], w_ref[g * GROUP_WIDTH:(g + 1) * GROUP_WIDTH, :],
                            preferred_element_type=_F32)
    o_ref[...] = acc


def _outproj(x2d, ys, w_all, layer):
    m, d = x2d.shape
    tm = 512
    y_spec = pl.BlockSpec((tm, GROUP_WIDTH), lambda i: (i, 0))
    return pl.pallas_call(
        _outproj_kernel,
        grid=(m // tm,),
        in_specs=[pl.BlockSpec((tm, d), lambda i: (i, 0)), y_spec, y_spec, y_spec, y_spec,
                  pl.BlockSpec((None,) + w_all.shape[1:], lambda i: (layer, 0, 0))],
        out_specs=pl.BlockSpec((tm, d), lambda i: (i, 0)),
        out_shape=jax.ShapeDtypeStruct((m, d), _F32),
        compiler_params=pltpu.CompilerParams(
            dimension_semantics=("arbitrary",), vmem_limit_bytes=_VMEM_LIMIT),
        name="outproj",
    )(x2d, *ys, w_all)


def _rearrange_w_in(w_in):
    off = 3 * GROUP_WIDTH
    ff = w_in[..., off:off + FOX_FORGET_COLS]
    pad = jnp.zeros(w_in.shape[:-1] + (LANES - FOX_FORGET_COLS,), w_in.dtype)
    out = jnp.concatenate([w_in[..., :off], w_in[..., off + FOX_FORGET_COLS:], ff, pad], axis=-1)
    assert out.shape[-1] == PROJ_WIDTH
    return out


def _qk_logit_bound(qk_gain):
    g = jnp.abs(qk_gain.astype(_F32))
    return ATTN_SCALE * HEAD_DIM * jnp.max(g[0]) * jnp.max(g[1])


def _qk_gain_row(fox_g, moba_g, swa_g):
    def tiles(g, n):
        return jnp.tile(g.astype(_F32), 2 * n)
    row = jnp.zeros((PROJ_WIDTH,), _F32)
    for col, g, n in ((COL_FQ, fox_g[0], 2), (COL_FK, fox_g[1], 2),
                      (COL_MQ, moba_g[0], 2), (COL_MK, moba_g[1], 2),
                      (COL_WQ, swa_g[0], 2), (COL_WK, swa_g[1], 1)):
        row = lax.dynamic_update_slice(row, tiles(g, n), (col * LANES,))
    return row[None, :]


def kernel(x, norm_gain, w_in, b_forget, fox_qk_gain, moba_qk_gain, swa_qk_gain, sinks, w_out, rel_bias):
    b, t, d = x.shape
    depth = w_in.shape[0]
    w_in_r = _rearrange_w_in(w_in.astype(_MXU_DTYPE))
    w_out_c = w_out.astype(_MXU_DTYPE)
    moba_wide_bias = _moba_wide_bias_tiles(rel_bias)
    swa_bias = _swa_bias_tiles(rel_bias)
    moba_bias_max = jnp.max(jnp.abs(rel_bias[:, :rel_bias.shape[1] // 2].astype(_F32)))
    x2d = x.reshape(b * t, d)
    for layer in range(depth):
        qk_gain_row = _qk_gain_row(fox_qk_gain[layer], moba_qk_gain[layer], swa_qk_gain[layer])
        proj = _inproj(x2d, norm_gain[layer][None, :], w_in_r, layer, qk_gain_row)
        proj = proj.reshape(b, t, PROJ_WIDTH)
        bf_row = jnp.pad(b_forget[layer], (0, LANES - FOX_FORGET_COLS))[None, :]
        fox_bound = _qk_logit_bound(fox_qk_gain[layer])
        y_fox = lax.cond(2.0 * fox_bound <= BOUNDED_SOFTMAX_RANGE,
                         lambda: _fox_bounded(proj, b_forget[layer], fox_bound.reshape(1)),
                         lambda: _fox(proj, bf_row))
        moba_bound = _qk_logit_bound(moba_qk_gain[layer]) + moba_bias_max
        y_moba = lax.cond(2.0 * moba_bound <= BOUNDED_SOFTMAX_RANGE,
                          lambda: _moba_bounded(proj, rel_bias, moba_wide_bias,
                                                moba_bound.reshape(1)),
                          lambda: _moba(proj, rel_bias, _moba_bias_tiles(rel_bias)))
        y_sb = _sb(proj)
        y_swa = _swa(proj, sinks[layer], swa_bias)
        ys = [y.reshape(b * t, GROUP_WIDTH) for y in (y_fox, y_moba, y_sb, y_swa)]
        x2d = _outproj(x2d, ys, w_out_c, layer)
    return x2d.reshape(b, t, d)
```

```python
import functools
import math

import jax
import jax.numpy as jnp
from jax import lax
from jax.experimental import pallas as pl
from jax.experimental.pallas import tpu as pltpu

HEAD_DIM = 64
LANES = 128
GROUP_WIDTH = 256
MOBA_BLOCK = 256
MOBA_TOPK = 3
MOBA_MAX_BLOCKS = 16
SWA_WINDOW = 128
NUM_BUCKETS = 32
REL_MAX_DISTANCE = 1024
RMS_EPS = 1e-6
NEG_INF = -1e30
ATTN_SCALE = HEAD_DIM ** -0.5
LOG2E = math.log2(math.e)
BOUNDED_SOFTMAX_RANGE = 60.0
MOBA_NEAR_TILES = REL_MAX_DISTANCE // MOBA_BLOCK + 1
MOBA_WIDE = 2 * MOBA_BLOCK
MOBA_WIDE_NEAR_TILES = REL_MAX_DISTANCE // MOBA_WIDE + 1
FOX_FORGET_COLS = 4
FORGET_ROWS = 8
EXP_ZERO_CUTOFF = -104.0

COL_FQ, COL_FK, COL_FV, COL_FG = 0, 2, 4, 6
COL_MQ, COL_MK, COL_MV, COL_MG = 8, 10, 12, 14
COL_SQ, COL_SK, COL_SV, COL_SG = 16, 18, 20, 22
COL_WQ, COL_WK, COL_WV, COL_WG = 24, 26, 27, 28
COL_FF = 30
PROJ_WIDTH = (COL_FF + 1) * LANES
QK_NORM_COLS = (COL_FQ, COL_FQ + 1, COL_FK, COL_FK + 1, COL_MQ, COL_MQ + 1, COL_MK, COL_MK + 1,
                COL_WQ, COL_WQ + 1, COL_WK)

_MXU_DTYPE = jnp.bfloat16
_F32 = jnp.float32
_VMEM_LIMIT = 48 * 1024 * 1024
_VMEM_LIMIT_WIDE = 56 * 1024 * 1024

_NT = (((1,), (1,)), ((), ()))


def _lane_iota(shape):
    return lax.broadcasted_iota(jnp.int32, shape, len(shape) - 1)


def _pair_rms(x, gain_row):
    low = _lane_iota(x.shape) < HEAD_DIM
    sq = x * x
    ms_lo = jnp.sum(jnp.where(low, sq, 0.0), axis=-1, keepdims=True)
    ms_hi = jnp.sum(jnp.where(low, 0.0, sq), axis=-1, keepdims=True)
    ms = jnp.where(low, ms_lo, ms_hi) * (1.0 / HEAD_DIM)
    return x * lax.rsqrt(ms + RMS_EPS) * gain_row


def _silu(g):
    return g * (1.0 / (1.0 + jnp.exp(-g)))


def _split3(x):
    hi = x.astype(_MXU_DTYPE).astype(_F32)
    r = x - hi
    mid = r.astype(_MXU_DTYPE).astype(_F32)
    return hi, mid, r - mid


def _own_half(hh, shape):
    lane = _lane_iota(shape)
    return (lane < HEAD_DIM) if hh == 0 else (lane >= HEAD_DIM)


def _flash_update(s, v_tile, m, l, acc):
    m_new = jnp.maximum(m, jnp.max(s, axis=-1, keepdims=True))
    alpha = jnp.exp(m - m_new)
    p = jnp.exp(s - m_new)
    l = alpha * l + jnp.sum(p, axis=-1, keepdims=True)
    acc = alpha * acc + jnp.dot(p.astype(_MXU_DTYPE), v_tile, preferred_element_type=_F32)
    return m_new, l, acc


def _causal(tq):
    return (lax.broadcasted_iota(jnp.int32, (tq, tq), 0)
            >= lax.broadcasted_iota(jnp.int32, (tq, tq), 1))


def _inproj_kernel(x_ref, gain_ref, w_ref, qkg_ref, o_ref, *, tn):
    x = x_ref[...]
    ms = jnp.mean(x * x, axis=-1, keepdims=True)
    hn = (x * lax.rsqrt(ms + RMS_EPS) * gain_ref[...]).astype(_MXU_DTYPE)
    width = o_ref.shape[1]
    for n in range(pl.cdiv(width, tn)):
        hi = min((n + 1) * tn, width)
        acc = jnp.dot(hn, w_ref[:, n * tn:hi], preferred_element_type=_F32)
        for c in range(n * tn // LANES, hi // LANES):
            tile = acc[:, c * LANES - n * tn:(c + 1) * LANES - n * tn]
            if c in QK_NORM_COLS:
                tile = _pair_rms(tile, qkg_ref[:, c * LANES:(c + 1) * LANES])
            o_ref[:, c * LANES:(c + 1) * LANES] = tile


def _inproj(x2d, gain_row, w_all, layer, qk_gain_row):
    m, d = x2d.shape
    n = w_all.shape[2]
    tm = 512
    return pl.pallas_call(
        functools.partial(_inproj_kernel, tn=1024),
        grid=(m // tm,),
        in_specs=[pl.BlockSpec((tm, d), lambda i: (i, 0)),
                  pl.BlockSpec((1, d), lambda i: (0, 0)),
                  pl.BlockSpec((None, d, n), lambda i: (layer, 0, 0)),
                  pl.BlockSpec((1, n), lambda i: (0, 0))],
        out_specs=pl.BlockSpec((tm, n), lambda i: (i, 0)),
        out_shape=jax.ShapeDtypeStruct((m, n), _F32),
        compiler_params=pltpu.CompilerParams(
            dimension_semantics=("arbitrary",), vmem_limit_bytes=_VMEM_LIMIT),
        name="inproj",
    )(x2d, gain_row, w_all, qk_gain_row)


def _rel_bucket(dist):
    max_exact = NUM_BUCKETS // 2
    d = jnp.maximum(dist, 0)
    log_ratio = (jnp.log(jnp.maximum(d, 1).astype(_F32) / max_exact)
                 / math.log(REL_MAX_DISTANCE / max_exact))
    large = max_exact + (log_ratio * (NUM_BUCKETS - max_exact)).astype(jnp.int32)
    large = jnp.minimum(large, NUM_BUCKETS - 1)
    return jnp.where(d < max_exact, d, large)


def _bias_kernel(rb_ref, b_ref, o_ref, *, col0, scale):
    h = pl.program_id(0)
    for d in range(b_ref.shape[0]):
        buckets = b_ref[d]
        acc = jnp.zeros(buckets.shape, _F32)
        for k in range(NUM_BUCKETS):
            acc = jnp.where(buckets == k, rb_ref[k, col0 + h], acc)
        o_ref[0, d] = acc if scale == 1.0 else acc * scale


def _bias_lookup(rel_bias, buckets, col0, scale):
    n_heads = rel_bias.shape[1] // 2
    return pl.pallas_call(
        functools.partial(_bias_kernel, col0=col0, scale=scale),
        grid=(n_heads,),
        in_specs=[pl.BlockSpec(memory_space=pltpu.SMEM),
                  pl.BlockSpec(buckets.shape, lambda h: (0, 0, 0))],
        out_specs=pl.BlockSpec((1,) + buckets.shape, lambda h: (h, 0, 0, 0)),
        out_shape=jax.ShapeDtypeStruct((n_heads,) + buckets.shape, _F32),
        compiler_params=pltpu.CompilerParams(
            dimension_semantics=("arbitrary",), vmem_limit_bytes=_VMEM_LIMIT),
        name="bias_tiles",
    )(rel_bias, buckets)


def _toeplitz_buckets(size, count):
    i = jnp.arange(size)[:, None]
    j = jnp.arange(size)[None, :]
    return jnp.stack([_rel_bucket(d * size + i - j) for d in range(count)])


def _moba_bias_tiles(rel_bias):
    return _bias_lookup(rel_bias, _toeplitz_buckets(MOBA_BLOCK, MOBA_NEAR_TILES), 0, 1.0)


def _moba_wide_bias_tiles(rel_bias):
    return _bias_lookup(rel_bias, _toeplitz_buckets(MOBA_WIDE, MOBA_WIDE_NEAR_TILES), 0, LOG2E)


def _swa_bias_tiles(rel_bias):
    w = SWA_WINDOW
    buckets = _rel_bucket(jnp.arange(w)[:, None] + w - jnp.arange(2 * w)[None, :])
    return _bias_lookup(rel_bias, buckets[None], rel_bias.shape[1] // 2, 1.0)[:, 0]


def _forget_cumsum(ff_ref, bf_ref, c_sc):
    t_len = c_sc.shape[0]
    ff = ff_ref[0] + bf_ref[...]
    log_f = jnp.minimum(ff, 0.0) - jnp.log(1.0 + jnp.exp(-jnp.abs(ff)))
    parts = [part.astype(_MXU_DTYPE) for part in _split3(log_f)]
    ch = 256
    tri = (lax.broadcasted_iota(jnp.int32, (ch, ch), 0)
           >= lax.broadcasted_iota(jnp.int32, (ch, ch), 1)).astype(_MXU_DTYPE)
    carry = jnp.zeros((1, LANES), _F32)
    for r in range(t_len // ch):
        inc = carry
        for part in reversed(parts):
            inc = inc + jnp.dot(tri, part[r * ch:(r + 1) * ch], preferred_element_type=_F32)
        c_sc[r * ch:(r + 1) * ch, :] = inc
        carry = inc[ch - 1:ch, :]


def _forget_cumsum_rows(ff_ref, bf_ref, c_rows):
    n_rows, t_len = c_rows.shape
    z = ff_ref[0].T[0:n_rows, :] + bf_ref[...]
    log_f = jnp.minimum(z, 0.0) - jnp.log(1.0 + jnp.exp(-jnp.abs(z)))
    parts = [part.astype(_MXU_DTYPE) for part in _split3(log_f)]
    ch = 256
    tri = (lax.broadcasted_iota(jnp.int32, (ch, ch), 0)
           <= lax.broadcasted_iota(jnp.int32, (ch, ch), 1)).astype(_MXU_DTYPE)
    carry = jnp.zeros((n_rows, 1), _F32)
    for r in range(t_len // ch):
        inc = carry
        for part in reversed(parts):
            inc = inc + jnp.dot(part[:, r * ch:(r + 1) * ch], tri, preferred_element_type=_F32)
        c_rows[:, r * ch:(r + 1) * ch] = inc
        carry = inc[:, ch - 1:ch]


def _lane_column(x, col):
    return jnp.sum(jnp.where(_lane_iota(x.shape) == col, x, 0.0), axis=-1, keepdims=True)


def _lane_fields(shape, a0, fields):
    lane = _lane_iota(shape)
    out = jnp.zeros(shape, _F32)
    for n, f in enumerate(fields):
        out = jnp.where(lane == a0 + n, f, out)
    return out


def _write_ones_column(vext_sc, h, hh):
    @pl.when(pl.program_id(0) == 0)
    def _():
        lane = _lane_iota((vext_sc.shape[1], LANES))
        vext_sc[h, :, LANES:2 * LANES] = jnp.where(lane == hh, 1.0, 0.0).astype(_MXU_DTYPE)


def _bounded_weights(ps_ref, q_augs, kaug_sc, r, bias, diagonal):
    tq = q_augs[0].shape[0]
    if not diagonal:
        for h in range(len(q_augs)):
            s = lax.dot_general(q_augs[h], kaug_sc[h, pl.ds(r, tq), :], _NT,
                                preferred_element_type=_F32)
            if bias is not None:
                s = s + bias(h)
            ps_ref[h] = jnp.exp2(s).astype(_MXU_DTYPE)
        return
    half = tq // 2
    rows = lax.broadcasted_iota(jnp.int32, (tq, half), 0)
    cols = lax.broadcasted_iota(jnp.int32, (tq, half), 1)
    for h in range(len(q_augs)):
        b_h = None if bias is None else bias(h)
        s = lax.dot_general(q_augs[h], kaug_sc[h, pl.ds(r, half), :], _NT,
                            preferred_element_type=_F32)
        if b_h is not None:
            s = s + b_h[:, 0:half]
        ps_ref[h, :, 0:half] = jnp.exp2(jnp.where(rows >= cols, s, NEG_INF)).astype(_MXU_DTYPE)
        s = lax.dot_general(q_augs[h][half:, :], kaug_sc[h, pl.ds(r + half, half), :], _NT,
                            preferred_element_type=_F32)
        if b_h is not None:
            s = s + b_h[half:, half:]
        ps_ref[h, 0:half, half:] = jnp.zeros((half, half), _MXU_DTYPE)
        ps_ref[h, half:, half:] = jnp.exp2(jnp.where(_causal(half), s, NEG_INF)).astype(_MXU_DTYPE)


def _bounded_values(ps_ref, vext_sc, r, acc_sc):
    tq = ps_ref.shape[1]
    for p in range(acc_sc.shape[0]):
        acc_sc[p] += (
            jnp.dot(ps_ref[2 * p], vext_sc[2 * p, pl.ds(r, tq), :], preferred_element_type=_F32)
            + jnp.dot(ps_ref[2 * p + 1], vext_sc[2 * p + 1, pl.ds(r, tq), :],
                      preferred_element_type=_F32))


def _bounded_store(acc_sc, g_ref, o_ref):
    tq = acc_sc.shape[1]
    for p in range(acc_sc.shape[0]):
        acc = acc_sc[p]
        den = acc[:, LANES:]
        l = jnp.where(_own_half(0, (tq, LANES)), den[:, 0:1], den[:, 1:2])
        cols = slice(p * LANES, (p + 1) * LANES)
        o_ref[0, :, cols] = (acc[:, :LANES] / l * _silu(g_ref[0, :, cols])).astype(o_ref.dtype)


def _bounded_pipeline_step(n, ps_bufs, weights, values):
    def new_in(slot):
        def run():
            weights(ps_bufs[slot])
            values(ps_bufs[1 - slot])
        return run
    lax.cond(n % 2 == 0, new_in(0), new_in(1))


def _bounded_finish(n_done, ps_bufs, values):
    lax.cond(n_done % 2 == 0, lambda: values(ps_bufs[0]), lambda: values(ps_bufs[1]))


def _fox_extra_base(h):
    p, hh = divmod(h, 2)
    return (HEAD_DIM if hh == 0 else 0) + 8 * p


def _fox_extra_lanes(fields):
    rows = max(f.shape[1] for fs in fields for f in fs if hasattr(f, "shape"))
    sub = lax.broadcasted_iota(jnp.int32, (8, rows), 0)
    blocks = {}
    for h, fs in enumerate(fields):
        blk = jnp.zeros((8, rows), _F32)
        for n, f in enumerate(fs):
            blk = jnp.where(sub == n, f, blk)
        blocks[_fox_extra_base(h)] = blk
    pieces, at = [], 0
    for base in sorted(blocks):
        if base > at:
            pieces.append(jnp.zeros((base - at, rows), _F32))
        pieces.append(blocks[base])
        at = base + 8
    pieces.append(jnp.zeros((LANES - at, rows), _F32))
    return jnp.concatenate(pieces, axis=0).T


def _fox_bounded_kernel(bound_ref, q_ref, k_ref, v_ref, ff_ref, g_ref, bf_ref, o_ref,
                        kaug_sc, vext_sc, c_sc, ps_a, ps_b, acc_sc):
    i = pl.program_id(1)
    tq = q_ref.shape[1]
    n_heads = kaug_sc.shape[0]

    @pl.when(i == 0)
    def _prep():
        _forget_cumsum_rows(ff_ref, bf_ref, c_sc)
        hi, mid, lo = _split3(-(c_sc[...] * LOG2E))
        extra = _fox_extra_lanes([[hi[h:h + 1], mid[h:h + 1], lo[h:h + 1], 1.0, 1.0, 1.0, 1.0, 1.0]
                                  for h in range(n_heads)])
        lane = _lane_iota(extra.shape)
        for h in range(n_heads):
            p, hh = divmod(h, 2)
            kn = k_ref[0, :, p * LANES:(p + 1) * LANES]
            v = v_ref[0, :, p * LANES:(p + 1) * LANES]
            a0 = _fox_extra_base(h)
            aug = jnp.where((lane >= a0) & (lane < a0 + 8), extra, 0.0)
            own = _own_half(hh, kn.shape)
            kaug_sc[h] = jnp.where(own, kn, aug).astype(_MXU_DTYPE)
            vext_sc[h, :, 0:LANES] = jnp.where(own, v, 0.0).astype(_MXU_DTYPE)
            _write_ones_column(vext_sc, h, hh)

    row0 = pl.multiple_of(i * tq, tq)
    off = jnp.full((1, 1), -LOG2E, _F32) * bound_ref[0]
    off_hi = off.astype(_MXU_DTYPE).astype(_F32)
    hi, mid, lo = _split3(c_sc[:, pl.ds(row0, tq)] * LOG2E)
    extra = _fox_extra_lanes([[1.0, 1.0, 1.0, hi[h:h + 1], mid[h:h + 1], lo[h:h + 1],
                               off_hi, off - off_hi] for h in range(n_heads)])
    lane = _lane_iota(extra.shape)
    q_augs = []
    for h in range(n_heads):
        p, hh = divmod(h, 2)
        q = q_ref[0, :, p * LANES:(p + 1) * LANES] * (ATTN_SCALE * LOG2E)
        a0 = _fox_extra_base(h)
        aug = jnp.where((lane >= a0) & (lane < a0 + 8), extra, 0.0)
        q_augs.append(jnp.where(_own_half(hh, q.shape), q, aug).astype(_MXU_DTYPE))

    chunk_lane = _lane_iota((c_sc.shape[0], LANES))
    head_rows = lax.broadcasted_iota(jnp.int32, (c_sc.shape[0], 1), 0) < n_heads
    c_first = jnp.max(jnp.where(chunk_lane == 0, c_sc[:, pl.ds(row0, LANES)], -jnp.inf),
                      axis=-1, keepdims=True)
    cutoff = EXP_ZERO_CUTOFF - 2.0 * bound_ref[0]

    def live(j):
        start = pl.multiple_of(jnp.maximum(j, 0) * tq + (tq - LANES), LANES)
        c_last = jnp.max(jnp.where(chunk_lane == LANES - 1, c_sc[:, pl.ds(start, LANES)], -jnp.inf),
                         axis=-1, keepdims=True)
        gap = jnp.max(jnp.where(head_rows, c_first - c_last, -jnp.inf))
        return (j >= 0) & (gap >= cutoff)

    ps_bufs = (ps_a, ps_b)

    def values_at(r):
        return lambda ps_ref: _bounded_values(ps_ref, vext_sc, pl.multiple_of(r, tq), acc_sc)

    def body(state):
        j, _, r_prev, n = state
        r = pl.multiple_of(j * tq, tq)
        _bounded_pipeline_step(
            n, ps_bufs,
            lambda ps_ref: _bounded_weights(ps_ref, q_augs, kaug_sc, r, None, False),
            values_at(r_prev))
        return j - 1, live(j - 1), r, n + 1

    acc_sc[...] = jnp.zeros(acc_sc.shape, _F32)
    _bounded_weights(ps_a, q_augs, kaug_sc, row0, None, True)
    state = lax.while_loop(lambda state: state[1], body,
                           (i - 1, live(i - 1), row0, jnp.int32(1)))
    _bounded_finish(state[3] - 1, ps_bufs, values_at(state[2]))
    _bounded_store(acc_sc, g_ref, o_ref)


def _fox_bounded(proj, b_forget, bound, *, tq=512):
    b, t, _ = proj.shape
    bf_col = jnp.pad(b_forget.astype(_F32), (0, FORGET_ROWS - FOX_FORGET_COLS))[:, None]
    n_heads = GROUP_WIDTH // HEAD_DIM
    wide = GROUP_WIDTH // LANES
    return pl.pallas_call(
        _fox_bounded_kernel,
        grid=(b, t // tq),
        in_specs=[pl.BlockSpec(memory_space=pltpu.SMEM),
                  pl.BlockSpec((1, tq, GROUP_WIDTH), lambda bb, i: (bb, i, COL_FQ // wide)),
                  pl.BlockSpec((1, t, GROUP_WIDTH), lambda bb, i: (bb, 0, COL_FK // wide)),
                  pl.BlockSpec((1, t, GROUP_WIDTH), lambda bb, i: (bb, 0, COL_FV // wide)),
                  pl.BlockSpec((1, t, LANES), lambda bb, i: (bb, 0, COL_FF)),
                  pl.BlockSpec((1, tq, GROUP_WIDTH), lambda bb, i: (bb, i, COL_FG // wide)),
                  pl.BlockSpec((FORGET_ROWS, 1), lambda bb, i: (0, 0))],
        out_specs=pl.BlockSpec((1, tq, GROUP_WIDTH), lambda bb, i: (bb, i, 0)),
        out_shape=jax.ShapeDtypeStruct((b, t, GROUP_WIDTH), _MXU_DTYPE),
        scratch_shapes=[pltpu.VMEM((n_heads, t, LANES), _MXU_DTYPE),
                        pltpu.VMEM((n_heads, t, 2 * LANES), _MXU_DTYPE),
                        pltpu.VMEM((FORGET_ROWS, t), _F32),
                        pltpu.VMEM((n_heads, tq, tq), _MXU_DTYPE),
                        pltpu.VMEM((n_heads, tq, tq), _MXU_DTYPE),
                        pltpu.VMEM((n_heads // 2, tq, 2 * LANES), _F32)],
        compiler_params=pltpu.CompilerParams(
            dimension_semantics=("arbitrary", "arbitrary"),
            vmem_limit_bytes=_VMEM_LIMIT_WIDE),
        name="fox_bounded",
    )(bound, proj, proj, proj, proj, proj, bf_col)


def _fox_kernel(q_ref, k_ref, v_ref, ff_ref, g_ref, bf_ref, o_ref,
                kaug_sc, v_sc, c_sc, *, tq):
    p = pl.program_id(1)
    i = pl.program_id(2)

    def head_column(x, hh):
        return _lane_column(x, 2 * p + hh)

    @pl.when(i == 0)
    def _prep():
        pl.when(p == 0)(lambda: _forget_cumsum(ff_ref, bf_ref, c_sc))
        kn = k_ref[0]
        c_all = c_sc[...]
        for hh in range(2):
            a0 = HEAD_DIM if hh == 0 else 0
            hi, mid, lo = _split3(-head_column(c_all, hh))
            aug = _lane_fields(kn.shape, a0, [hi, mid, lo, 1.0, 1.0, 1.0])
            kaug_sc[hh] = jnp.where(_own_half(hh, kn.shape), kn, aug).astype(_MXU_DTYPE)
        v_sc[...] = v_ref[0].astype(_MXU_DTYPE)

    q = q_ref[0] * ATTN_SCALE
    row0 = pl.multiple_of(i * tq, tq)
    c_t = c_sc[pl.ds(row0, tq), :]
    outs = []
    for hh in range(2):
        a0 = HEAD_DIM if hh == 0 else 0
        hi, mid, lo = _split3(head_column(c_t, hh))
        aug = _lane_fields(q.shape, a0, [1.0, 1.0, 1.0, hi, mid, lo])
        q_aug = jnp.where(_own_half(hh, q.shape), q, aug).astype(_MXU_DTYPE)

        s = lax.dot_general(q_aug, kaug_sc[hh, pl.ds(row0, tq), :], _NT,
                            preferred_element_type=_F32)
        s = jnp.where(_causal(tq), s, NEG_INF)
        m = jnp.max(s, axis=-1, keepdims=True)
        pr = jnp.exp(s - m)
        l = jnp.sum(pr, axis=-1, keepdims=True)
        acc = jnp.dot(pr.astype(_MXU_DTYPE), v_sc[pl.ds(row0, tq), :], preferred_element_type=_F32)

        def body(j, carry, q_aug=q_aug, hh=hh):
            r = pl.multiple_of(j * tq, tq)
            s = lax.dot_general(q_aug, kaug_sc[hh, pl.ds(r, tq), :], _NT,
                                preferred_element_type=_F32)
            return _flash_update(s, v_sc[pl.ds(r, tq), :], *carry)

        m, l, acc = lax.fori_loop(0, i, body, (m, l, acc))
        outs.append(acc / l)
    o = jnp.where(_own_half(0, outs[0].shape), outs[0], outs[1])
    o_ref[0] = (o * _silu(g_ref[0])).astype(o_ref.dtype)


def _fox(proj, bf_row, *, tq=256):
    b, t, _ = proj.shape
    return pl.pallas_call(
        functools.partial(_fox_kernel, tq=tq),
        grid=(b, 2, t // tq),
        in_specs=[pl.BlockSpec((1, tq, LANES), lambda bb, p, i: (bb, i, COL_FQ + p)),
                  pl.BlockSpec((1, t, LANES), lambda bb, p, i: (bb, 0, COL_FK + p)),
                  pl.BlockSpec((1, t, LANES), lambda bb, p, i: (bb, 0, COL_FV + p)),
                  pl.BlockSpec((1, t, LANES), lambda bb, p, i: (bb, 0, COL_FF)),
                  pl.BlockSpec((1, tq, LANES), lambda bb, p, i: (bb, i, COL_FG + p)),
                  pl.BlockSpec((1, LANES), lambda bb, p, i: (0, 0))],
        out_specs=pl.BlockSpec((1, tq, LANES), lambda bb, p, i: (bb, i, p)),
        out_shape=jax.ShapeDtypeStruct((b, t, GROUP_WIDTH), _MXU_DTYPE),
        scratch_shapes=[pltpu.VMEM((2, t, LANES), _MXU_DTYPE),
                        pltpu.VMEM((t, LANES), _MXU_DTYPE),
                        pltpu.VMEM((t, LANES), _F32)],
        compiler_params=pltpu.CompilerParams(
            dimension_semantics=("arbitrary", "arbitrary", "arbitrary"),
            vmem_limit_bytes=_VMEM_LIMIT),
        name="fox",
    )(proj, proj, proj, proj, proj, bf_row)


def _moba_block_means(kn, kmean_sc):
    kmean_sc[...] = jnp.zeros(kmean_sc.shape, _F32)
    for n in range(kn.shape[0] // MOBA_BLOCK):
        kmean_sc[n:n + 1, :] = jnp.mean(kn[n * MOBA_BLOCK:(n + 1) * MOBA_BLOCK], axis=0, keepdims=True)


def _moba_select(q_head, kmean, past):
    gate = lax.dot_general(q_head, kmean, _NT, precision=lax.Precision.HIGHEST,
                           preferred_element_type=_F32)
    lane_f = _lane_iota(gate.shape).astype(_F32)
    cand = jnp.where(past, gate, -jnp.inf)
    sel = jnp.zeros(gate.shape, _F32)
    for _ in range(MOBA_TOPK):
        mx = jnp.max(cand, axis=-1, keepdims=True)
        is_max = (cand == mx) & (mx > -jnp.inf)
        first = jnp.min(jnp.where(is_max, lane_f, float(LANES)), axis=-1, keepdims=True)
        pick = lane_f == first
        sel = jnp.where(pick, 1.0, sel)
        cand = jnp.where(pick, -jnp.inf, cand)
    return sel


def _moba_bounded_kernel(rb_ref, bound_ref, q_ref, k_ref, v_ref, g_ref, bias_ref, o_ref,
                         kaug_sc, vext_sc, kmean_sc, ps_a, ps_b, acc_sc):
    i = pl.program_id(1)
    tq = q_ref.shape[1]
    n_heads = kaug_sc.shape[0]
    blk = MOBA_BLOCK
    near = MOBA_WIDE_NEAR_TILES

    @pl.when(i == 0)
    def _prep():
        lane = _lane_iota((k_ref.shape[1], LANES))
        row_blk = lax.broadcasted_iota(jnp.int32, lane.shape, 0) // blk
        for h in range(n_heads):
            p, hh = divmod(h, 2)
            kn = k_ref[0, :, p * LANES:(p + 1) * LANES]
            v = v_ref[0, :, p * LANES:(p + 1) * LANES]
            if hh == 0:
                _moba_block_means(kn, kmean_sc.at[p])
            a0 = HEAD_DIM if hh == 0 else 0
            aug = jnp.where((lane - a0 == row_blk) | (lane - a0 - MOBA_MAX_BLOCKS == row_blk)
                            | (lane - a0 == 2 * MOBA_MAX_BLOCKS)
                            | (lane - a0 == 2 * MOBA_MAX_BLOCKS + 1), 1.0, 0.0)
            own = _own_half(hh, kn.shape)
            kaug_sc[h] = jnp.where(own, kn, aug).astype(_MXU_DTYPE)
            vext_sc[h, :, 0:LANES] = jnp.where(own, v, 0.0).astype(_MXU_DTYPE)
            _write_ones_column(vext_sc, h, hh)

    row0 = pl.multiple_of(i * tq, tq)
    off = jnp.full((1, 1), -LOG2E, _F32) * bound_ref[0]
    off_hi = off.astype(_MXU_DTYPE).astype(_F32)
    blk_n = lax.broadcasted_iota(jnp.int32, (MOBA_MAX_BLOCKS, tq), 0)
    blk_f = blk_n.astype(_F32)
    q_blk = i * (tq // blk) + lax.broadcasted_iota(jnp.int32, (MOBA_MAX_BLOCKS, tq), 1) // blk
    past = blk_n < q_blk
    far = i - blk_n // 2 >= near
    sub8 = lax.broadcasted_iota(jnp.int32, (8, tq), 0)
    off_rows = jnp.where(sub8 == 0, off_hi, jnp.where(sub8 == 1, off - off_hi, 0.0))
    gates = []
    for p in range(n_heads // 2):
        means = kmean_sc[p, 0:MOBA_MAX_BLOCKS, :]
        first = _own_half(0, means.shape)
        stacked = jnp.concatenate([jnp.where(first, means, 0.0), jnp.where(first, 0.0, means)], axis=0)
        gates.append(lax.dot_general(stacked, q_ref[0, :, p * LANES:(p + 1) * LANES], _NT,
                                     precision=lax.Precision.HIGHEST, preferred_element_type=_F32))
    q_augs = []
    for h in range(n_heads):
        p, hh = divmod(h, 2)
        qn = q_ref[0, :, p * LANES:(p + 1) * LANES]
        own = _own_half(hh, qn.shape)
        cand = jnp.where(past, gates[p][MOBA_MAX_BLOCKS * hh:MOBA_MAX_BLOCKS * (hh + 1), :], -jnp.inf)
        sel = jnp.zeros((MOBA_MAX_BLOCKS, tq), _F32)
        for _ in range(MOBA_TOPK):
            mx = jnp.max(cand, axis=0, keepdims=True)
            is_max = (cand == mx) & (mx > -jnp.inf)
            first = jnp.min(jnp.where(is_max, blk_f, float(LANES)), axis=0, keepdims=True)
            pick = blk_f == first
            sel = jnp.where(pick, 1.0, sel)
            cand = jnp.where(pick, -jnp.inf, cand)
        c_far = jnp.full((1, 1), LOG2E, _F32) * rb_ref[NUM_BUCKETS - 1, h]
        far_hi = c_far.astype(_MXU_DTYPE).astype(_F32)
        pen = jnp.where((blk_n == q_blk) | (past & (sel != 0.0)), 0.0, NEG_INF)
        fields = jnp.concatenate([pen + jnp.where(far, far_hi, 0.0),
                                  jnp.where(far, c_far - far_hi, 0.0),
                                  off_rows,
                                  jnp.zeros((HEAD_DIM - 2 * MOBA_MAX_BLOCKS - 8, tq), _F32)],
                                 axis=0)
        blank = jnp.zeros((HEAD_DIM, tq), _F32)
        aug = jnp.concatenate([blank, fields] if hh == 0 else [fields, blank], axis=0).T
        q_augs.append(jnp.where(own, qn * (ATTN_SCALE * LOG2E), aug).astype(_MXU_DTYPE))

    ps_bufs = (ps_a, ps_b)

    def values_at(r):
        return lambda ps_ref: _bounded_values(ps_ref, vext_sc, pl.multiple_of(r, tq), acc_sc)

    def body(j, state, with_bias):
        r_prev, n = state
        r = pl.multiple_of(j * tq, tq)
        bias = (lambda h: bias_ref[h, i - j]) if with_bias else None
        _bounded_pipeline_step(
            n, ps_bufs,
            lambda ps_ref: _bounded_weights(ps_ref, q_augs, kaug_sc, r, bias, False),
            values_at(r_prev))
        return r, n + 1

    acc_sc[...] = jnp.zeros(acc_sc.shape, _F32)
    _bounded_weights(ps_a, q_augs, kaug_sc, row0, lambda h: bias_ref[h, 0], True)
    state = (row0, jnp.int32(1))
    near_lo = jnp.maximum(i - (near - 1), 0)
    state = lax.fori_loop(near_lo, i, functools.partial(body, with_bias=True), state)
    state = lax.fori_loop(0, near_lo, functools.partial(body, with_bias=False), state)
    _bounded_finish(state[1] - 1, ps_bufs, values_at(state[0]))
    _bounded_store(acc_sc, g_ref, o_ref)


def _moba_bounded(proj, rel_bias, bias_tiles, bound):
    b, t, _ = proj.shape
    tq = MOBA_WIDE
    assert t % tq == 0 and t // MOBA_BLOCK <= MOBA_MAX_BLOCKS
    n_heads = GROUP_WIDTH // HEAD_DIM
    wide = GROUP_WIDTH // LANES
    resident = pl.Buffered(1)
    return pl.pallas_call(
        _moba_bounded_kernel,
        grid=(b, t // tq),
        in_specs=[pl.BlockSpec(memory_space=pltpu.SMEM),
                  pl.BlockSpec(memory_space=pltpu.SMEM),
                  pl.BlockSpec((1, tq, GROUP_WIDTH), lambda bb, i: (bb, i, COL_MQ // wide)),
                  pl.BlockSpec((1, t, GROUP_WIDTH), lambda bb, i: (bb, 0, COL_MK // wide)),
                  pl.BlockSpec((1, t, GROUP_WIDTH), lambda bb, i: (bb, 0, COL_MV // wide)),
                  pl.BlockSpec((1, tq, GROUP_WIDTH), lambda bb, i: (bb, i, COL_MG // wide)),
                  pl.BlockSpec((n_heads, MOBA_WIDE_NEAR_TILES, tq, tq), lambda bb, i: (0, 0, 0, 0),
                               pipeline_mode=resident)],
        out_specs=pl.BlockSpec((1, tq, GROUP_WIDTH), lambda bb, i: (bb, i, 0)),
        out_shape=jax.ShapeDtypeStruct((b, t, GROUP_WIDTH), _MXU_DTYPE),
        scratch_shapes=[pltpu.VMEM((n_heads, t, LANES), _MXU_DTYPE),
                        pltpu.VMEM((n_heads, t, 2 * LANES), _MXU_DTYPE),
                        pltpu.VMEM((n_heads // 2, LANES, LANES), _F32),
                        pltpu.VMEM((n_heads, tq, tq), _MXU_DTYPE),
                        pltpu.VMEM((n_heads, tq, tq), _MXU_DTYPE),
                        pltpu.VMEM((n_heads // 2, tq, 2 * LANES), _F32)],
        compiler_params=pltpu.CompilerParams(
            dimension_semantics=("arbitrary", "arbitrary"),
            vmem_limit_bytes=_VMEM_LIMIT_WIDE),
        name="moba_bounded",
    )(rel_bias, bound, proj, proj, proj, proj, bias_tiles)


def _moba_kernel(rb_ref, q_ref, k_ref, v_ref, g_ref, bias_ref, o_ref,
                 kaug_sc, v_sc, kmean_sc):
    p = pl.program_id(1)
    i = pl.program_id(2)
    blk = MOBA_BLOCK
    near = MOBA_NEAR_TILES

    @pl.when(i == 0)
    def _prep():
        kn = k_ref[0]
        _moba_block_means(kn, kmean_sc)
        lane = _lane_iota(kn.shape)
        row_blk = lax.broadcasted_iota(jnp.int32, kn.shape, 0) // blk
        for hh in range(2):
            a0 = HEAD_DIM if hh == 0 else 0
            onehot = jnp.where((lane - a0 == row_blk) | (lane - a0 - MOBA_MAX_BLOCKS == row_blk),
                               1.0, 0.0)
            kaug_sc[hh] = jnp.where(_own_half(hh, kn.shape), kn, onehot).astype(_MXU_DTYPE)
        v_sc[...] = v_ref[0].astype(_MXU_DTYPE)

    qn = q_ref[0]
    row0 = pl.multiple_of(i * blk, blk)
    lane = _lane_iota(qn.shape)
    past = lane < i
    outs = []
    for hh in range(2):
        own = _own_half(hh, qn.shape)
        sel = _moba_select(jnp.where(own, qn, 0.0), kmean_sc[...], past)
        c_far = jnp.full((1, LANES), rb_ref[NUM_BUCKETS - 1, 2 * p + hh], _F32)
        far_hi = c_far.astype(_MXU_DTYPE).astype(_F32)
        far_lo = c_far - far_hi
        pen = jnp.where(past & (sel == 0.0), NEG_INF, 0.0)
        aug = jnp.where(lane < MOBA_MAX_BLOCKS, pen + jnp.where(i - lane >= near, far_hi, 0.0),
                        jnp.where((lane < 2 * MOBA_MAX_BLOCKS)
                                  & (i - (lane - MOBA_MAX_BLOCKS) >= near), far_lo, 0.0))
        if hh == 0:
            aug = pltpu.roll(aug, HEAD_DIM, axis=1)
        q_aug = jnp.where(own, qn * ATTN_SCALE, aug).astype(_MXU_DTYPE)

        def scores(r, q_aug=q_aug, hh=hh):
            return lax.dot_general(q_aug, kaug_sc[hh, pl.ds(r, blk), :], _NT,
                                   preferred_element_type=_F32)

        s = scores(row0) + bias_ref[hh, 0]
        s = jnp.where(_causal(blk), s, NEG_INF)
        m = jnp.max(s, axis=-1, keepdims=True)
        pr = jnp.exp(s - m)
        l = jnp.sum(pr, axis=-1, keepdims=True)
        acc = jnp.dot(pr.astype(_MXU_DTYPE), v_sc[pl.ds(row0, blk), :], preferred_element_type=_F32)

        def near_body(j, carry, hh=hh, scores=scores):
            r = pl.multiple_of(j * blk, blk)
            return _flash_update(scores(r) + bias_ref[hh, i - j], v_sc[pl.ds(r, blk), :], *carry)

        def far_body(j, carry, scores=scores):
            r = pl.multiple_of(j * blk, blk)
            return _flash_update(scores(r), v_sc[pl.ds(r, blk), :], *carry)

        near_lo = jnp.maximum(i - (near - 1), 0)
        carry = lax.fori_loop(near_lo, i, near_body, (m, l, acc))
        m, l, acc = lax.fori_loop(0, near_lo, far_body, carry)
        outs.append(acc / l)
    o = jnp.where(_own_half(0, outs[0].shape), outs[0], outs[1])
    o_ref[0] = (o * _silu(g_ref[0])).astype(o_ref.dtype)


def _moba(proj, rel_bias, bias_tiles):
    b, t, _ = proj.shape
    blk = MOBA_BLOCK
    assert t % blk == 0 and t // blk <= MOBA_MAX_BLOCKS
    return pl.pallas_call(
        _moba_kernel,
        grid=(b, 2, t // blk),
        in_specs=[pl.BlockSpec(memory_space=pltpu.SMEM),
                  pl.BlockSpec((1, blk, LANES), lambda bb, p, i: (bb, i, COL_MQ + p)),
                  pl.BlockSpec((1, t, LANES), lambda bb, p, i: (bb, 0, COL_MK + p)),
                  pl.BlockSpec((1, t, LANES), lambda bb, p, i: (bb, 0, COL_MV + p)),
                  pl.BlockSpec((1, blk, LANES), lambda bb, p, i: (bb, i, COL_MG + p)),
                  pl.BlockSpec((2, MOBA_NEAR_TILES, blk, blk), lambda bb, p, i: (p, 0, 0, 0))],
        out_specs=pl.BlockSpec((1, blk, LANES), lambda bb, p, i: (bb, i, p)),
        out_shape=jax.ShapeDtypeStruct((b, t, GROUP_WIDTH), _MXU_DTYPE),
        scratch_shapes=[pltpu.VMEM((2, t, LANES), _MXU_DTYPE),
                        pltpu.VMEM((t, LANES), _MXU_DTYPE),
                        pltpu.VMEM((LANES, LANES), _F32)],
        compiler_params=pltpu.CompilerParams(
            dimension_semantics=("arbitrary", "arbitrary", "arbitrary"),
            vmem_limit_bytes=_VMEM_LIMIT),
        name="moba",
    )(rel_bias, proj, proj, proj, proj, bias_tiles)


def _sb_kernel(q_ref, k_ref, v_ref, g_ref, o_ref, k_sc, v_sc, *, tq):
    i = pl.program_id(1)
    n_heads = v_sc.shape[0]

    @pl.when(i == 0)
    def _prep():
        for p in range(n_heads // 2):
            k_sc[p] = k_ref[0, :, p * LANES:(p + 1) * LANES].astype(_MXU_DTYPE)
            v = v_ref[0, :, p * LANES:(p + 1) * LANES]
            for hh in range(2):
                v_sc[2 * p + hh] = jnp.where(_own_half(hh, v.shape), v, 0.0).astype(_MXU_DTYPE)

    subs = q_ref.shape[1] // tq
    q_heads = []
    for u in range(subs):
        q_heads.append([])
        for h in range(n_heads):
            p, hh = divmod(h, 2)
            q = q_ref[0, u * tq:(u + 1) * tq, p * LANES:(p + 1) * LANES] * (ATTN_SCALE * LOG2E)
            q_heads[u].append(jnp.where(_own_half(hh, q.shape), q, 0.0).astype(_MXU_DTYPE))
    strict = (lax.broadcasted_iota(jnp.int32, (tq, tq), 0)
              > lax.broadcasted_iota(jnp.int32, (tq, tq), 1))
    after = strict.astype(_MXU_DTYPE)
    sign_bit = jnp.uint32(0x80000000)

    def pair(back, runs, diagonal):
        chains = [(u, t, h) for u in range(subs) for t in range(2) for h in range(n_heads)]
        tile_idx = [[subs * i + u - back - t for t in range(2)] for u in range(subs)]
        valid = [[(j >= 0).astype(_F32) for j in js] for js in tile_idx]
        rows = [[pl.multiple_of(jnp.maximum(j, 0) * tq, tq) for j in js] for js in tile_idx]
        masked = [diagonal and t == 0 for _, t, _ in chains]
        zs = [lax.dot_general(q_heads[u][h], k_sc[h // 2, pl.ds(rows[u][t], tq), :], _NT,
                              preferred_element_type=_F32) for u, t, h in chains]
        drops = []
        for z, msk in zip(zs, masked):
            neg_abs = lax.bitcast_convert_type(lax.bitcast_convert_type(z, jnp.uint32) | sign_bit, _F32)
            drop = jnp.maximum(z, 0.0) + jnp.log(1.0 + jnp.exp2(neg_abs)) * LOG2E
            drops.append(jnp.where(strict, drop, 0.0) if msk else drop)
        laters = []
        for drop in drops:
            hi = drop.astype(_MXU_DTYPE)
            lo = (drop - hi.astype(_F32)).astype(_MXU_DTYPE)
            laters.append(jnp.dot(hi, after, preferred_element_type=_F32)
                          + jnp.dot(lo, after, preferred_element_type=_F32))
        pvs = []
        for (u, t, h), z, drop, later, msk in zip(chains, zs, drops, laters, masked):
            w = jnp.exp2(z - drop - later)
            if msk:
                w = jnp.where(strict, w, 0.0)
            pvs.append(jnp.dot(w.astype(_MXU_DTYPE), v_sc[h, pl.ds(rows[u][t], tq), :],
                               preferred_element_type=_F32))
        sums = [jnp.sum(drop, axis=-1, keepdims=True) for drop in drops]
        new_runs, outs = [], []
        for u in range(subs):
            new_runs.append([])
            for p in range(n_heads // 2):
                out = None
                for h in (2 * p, 2 * p + 1):
                    run = runs[u][h]
                    for t in range(2):
                        c = chains.index((u, t, h))
                        pv = pvs[c] * (jnp.exp2(run) * valid[u][t])
                        out = pv if out is None else out + pv
                        run = run - sums[c] * valid[u][t]
                    new_runs[u].append(run)
                outs.append(out)
        return new_runs, outs

    def alive(runs):
        top = functools.reduce(jnp.maximum, [r for rs in runs for r in rs])
        return jnp.max(top) >= EXP_ZERO_CUTOFF * LOG2E

    def flat(runs):
        return [r for rs in runs for r in rs]

    def nested(flat_runs):
        return [flat_runs[n_heads * u:n_heads * (u + 1)] for u in range(subs)]

    n_runs = n_heads * subs
    zero = jnp.zeros((tq, 1), _F32)
    runs, accs = pair(0, [[zero] * n_heads for _ in range(subs)], True)

    def body(state):
        back = state[0]
        runs, outs = pair(back, nested(list(state[2:2 + n_runs])), False)
        accs = [a + o for a, o in zip(state[2 + n_runs:], outs)]
        return (back + 2, alive(runs), *flat(runs), *accs)

    state = lax.while_loop(lambda state: (subs * i + subs - 1 - state[0] >= 0) & state[1], body,
                           (jnp.int32(2), alive(runs), *flat(runs), *accs))
    accs = state[2 + n_runs:]
    for u in range(subs):
        for p in range(n_heads // 2):
            sl = (0, slice(u * tq, (u + 1) * tq), slice(p * LANES, (p + 1) * LANES))
            o_ref[sl] = (accs[u * (n_heads // 2) + p] * _silu(g_ref[sl])).astype(o_ref.dtype)


def _sb(proj, *, tq=256, subs=2):
    b, t, _ = proj.shape
    rows = tq * subs
    n_heads = GROUP_WIDTH // HEAD_DIM
    wide = GROUP_WIDTH // LANES
    return pl.pallas_call(
        functools.partial(_sb_kernel, tq=tq),
        grid=(b, t // rows),
        in_specs=[pl.BlockSpec((1, rows, GROUP_WIDTH), lambda bb, i: (bb, i, COL_SQ // wide)),
                  pl.BlockSpec((1, t, GROUP_WIDTH), lambda bb, i: (bb, 0, COL_SK // wide)),
                  pl.BlockSpec((1, t, GROUP_WIDTH), lambda bb, i: (bb, 0, COL_SV // wide)),
                  pl.BlockSpec((1, rows, GROUP_WIDTH), lambda bb, i: (bb, i, COL_SG // wide))],
        out_specs=pl.BlockSpec((1, rows, GROUP_WIDTH), lambda bb, i: (bb, i, 0)),
        out_shape=jax.ShapeDtypeStruct((b, t, GROUP_WIDTH), _MXU_DTYPE),
        scratch_shapes=[pltpu.VMEM((n_heads // 2, t, LANES), _MXU_DTYPE),
                        pltpu.VMEM((n_heads, t, LANES), _MXU_DTYPE)],
        compiler_params=pltpu.CompilerParams(
            dimension_semantics=("arbitrary", "arbitrary"),
            vmem_limit_bytes=_VMEM_LIMIT),
        name="stickbreak",
    )(proj, proj, proj, proj)


def _swa_kernel(sink_ref, q_ref, k_ref, v_ref, g_ref, bias_ref, o_ref,
                k_sc, v_sc, *, tq):
    i = pl.program_id(1)
    t_len = k_ref.shape[1]
    w = SWA_WINDOW
    n_kv = k_sc.shape[0]

    @pl.when(i == 0)
    def _prep():
        kn = k_ref[0]
        v = v_ref[0]
        kn_swapped = pltpu.roll(kn, HEAD_DIM, axis=1)
        v_swapped = pltpu.roll(v, HEAD_DIM, axis=1)
        for kv in range(n_kv):
            keep = _own_half(kv, kn.shape)
            k_sc[kv, 0:w, :] = jnp.zeros((w, LANES), _MXU_DTYPE)
            v_sc[kv, 0:w, :] = jnp.zeros((w, LANES), _MXU_DTYPE)
            k_sc[kv, w:w + t_len, :] = jnp.where(keep, kn, kn_swapped).astype(_MXU_DTYPE)
            v_sc[kv, w:w + t_len, :] = jnp.where(keep, v, v_swapped).astype(_MXU_DTYPE)

    qi =lax.broadcasted_iota(jnp.int32, (w, 2 * w), 0)
    kj = lax.broadcasted_iota(jnp.int32, (w, 2 * w), 1)
    dist = qi + w - kj
    in_window = (dist >= 0) & (dist < w)
    n_sub = tq // w
    chains = [(u, kv, g) for u in range(n_sub) for kv in range(n_kv) for g in range(2)]
    rows = [pl.multiple_of((i * n_sub + u) * w, w) for u in range(n_sub)]
    ss = []
    for u, kv, g in chains:
        q_u = q_ref[0, u * w:(u + 1) * w, kv * LANES:(kv + 1) * LANES] * ATTN_SCALE
        q_h = jnp.where(_own_half(g, q_u.shape), q_u, 0.0).astype(_MXU_DTYPE)
        ss.append(lax.dot_general(q_h, k_sc[kv, pl.ds(rows[u], 2 * w), :], _NT,
                                  preferred_element_type=_F32))
    es, dens = [], []
    for (u, kv, g), s in zip(chains, ss):
        allowed = in_window & (kj + (i * n_sub + u - 1) * w >= 0)
        s = jnp.where(allowed, s + bias_ref[2 * kv + g], NEG_INF)
        sink = sink_ref[2 * kv + g]
        m = jnp.maximum(jnp.max(s, axis=-1, keepdims=True), sink)
        e = jnp.exp(s - m)
        dens.append(jnp.sum(e, axis=-1, keepdims=True) + jnp.exp(sink - m))
        es.append(e.astype(_MXU_DTYPE))
    outs = [jnp.dot(e, v_sc[kv, pl.ds(rows[u], 2 * w), :], preferred_element_type=_F32) / den
            for (u, kv, g), e, den in zip(chains, es, dens)]
    for u in range(n_sub):
        for kv in range(n_kv):
            c = chains.index((u, kv, 0))
            o = jnp.where(_own_half(0, (w, LANES)), outs[c], outs[c + 1])
            sl = (0, slice(u * w, (u + 1) * w), slice(kv * LANES, (kv + 1) * LANES))
            o_ref[sl] = (o * _silu(g_ref[sl])).astype(o_ref.dtype)


def _swa(proj, sinks, bias_tiles, *, tq=512):
    b, t, _ = proj.shape
    w = SWA_WINDOW
    n_heads = GROUP_WIDTH // HEAD_DIM
    wide = GROUP_WIDTH // LANES
    return pl.pallas_call(
        functools.partial(_swa_kernel, tq=tq),
        grid=(b, t // tq),
        in_specs=[pl.BlockSpec(memory_space=pltpu.SMEM),
                  pl.BlockSpec((1, tq, GROUP_WIDTH), lambda bb, i: (bb, i, COL_WQ // wide)),
                  pl.BlockSpec((1, t, LANES), lambda bb, i: (bb, 0, COL_WK)),
                  pl.BlockSpec((1, t, LANES), lambda bb, i: (bb, 0, COL_WV)),
                  pl.BlockSpec((1, tq, GROUP_WIDTH), lambda bb, i: (bb, i, COL_WG // wide)),
                  pl.BlockSpec((n_heads, w, 2 * w), lambda bb, i: (0, 0, 0))],
        out_specs=pl.BlockSpec((1, tq, GROUP_WIDTH), lambda bb, i: (bb, i, 0)),
        out_shape=jax.ShapeDtypeStruct((b, t, GROUP_WIDTH), _MXU_DTYPE),
        scratch_shapes=[pltpu.VMEM((n_heads // 2, t + w, LANES), _MXU_DTYPE),
                        pltpu.VMEM((n_heads // 2, t + w, LANES), _MXU_DTYPE)],
        compiler_params=pltpu.CompilerParams(
            dimension_semantics=("arbitrary", "arbitrary"),
            vmem_limit_bytes=_VMEM_LIMIT),
        name="swa",
    )(sinks, proj, proj, proj, proj, bias_tiles)


def _outproj_kernel(x_hbm, ya_ref, yb_ref, yc_ref, yd_ref, w_ref, o_ref, x_ring, sems):
    s = pl.program_id(0)
    n = pl.num_programs(0)
    depth, tm, _ = x_ring.shape

    def rows_copy(step, slot):
        return pltpu.make_async_copy(x_hbm.at[pl.ds(step * tm, tm), :], x_ring.at[slot],
                                     sems.at[slot])

    @pl.when(s == 0)
    def _prefill():
        for k in range(depth):
            rows_copy(k, k).start()

    slot = s % depth
    rows_copy(s, slot).wait()
    acc = x_ring[slot]
    for g, y_ref in enumerate((ya_ref, yb_ref, yc_ref, yd_ref)):
        acc = acc + jnp.dot(y_ref[...], w_ref[g * GROUP_WIDTH:(g + 1) * GROUP_WIDTH, :],
                            preferred_element_type=_F32)
    o_ref[...] = acc

    @pl.when(s + depth < n)
    def _refill():
        rows_copy(s + depth, slot).start()


def _outproj(x2d, ys, w_all, layer):
    m, d = x2d.shape
    tm = 512
    depth = 3
    assert m // tm >= depth
    y_spec = pl.BlockSpec((tm, GROUP_WIDTH), lambda i: (i, 0))
    return pl.pallas_call(
        _outproj_kernel,
        grid=(m // tm,),
        in_specs=[pl.BlockSpec(memory_space=pl.ANY),
                  y_spec, y_spec, y_spec, y_spec,
                  pl.BlockSpec((None,) + w_all.shape[1:], lambda i: (layer, 0, 0))],
        out_specs=pl.BlockSpec((tm, d), lambda i: (i, 0)),
        out_shape=jax.ShapeDtypeStruct((m, d), _F32),
        scratch_shapes=[pltpu.VMEM((depth, tm, d), _F32),
                        pltpu.SemaphoreType.DMA((depth,))],
        compiler_params=pltpu.CompilerParams(
            dimension_semantics=("arbitrary",), vmem_limit_bytes=_VMEM_LIMIT),
        name="outproj",
    )(x2d, *ys, w_all)


def _rearrange_w_in(w_in):
    off = 3 * GROUP_WIDTH
    ff = w_in[..., off:off + FOX_FORGET_COLS]
    pad = jnp.zeros(w_in.shape[:-1] + (LANES - FOX_FORGET_COLS,), w_in.dtype)
    out = jnp.concatenate([w_in[..., :off], w_in[..., off + FOX_FORGET_COLS:], ff, pad], axis=-1)
    assert out.shape[-1] == PROJ_WIDTH
    return out


def _qk_logit_bound(qk_gain):
    g = jnp.abs(qk_gain.astype(_F32))
    return ATTN_SCALE * HEAD_DIM * jnp.max(g[0]) * jnp.max(g[1])


def _qk_gain_row(fox_g, moba_g, swa_g):
    def tiles(g, n):
        return jnp.tile(g.astype(_F32), 2 * n)
    row = jnp.zeros((PROJ_WIDTH,), _F32)
    for col, g, n in ((COL_FQ, fox_g[0], 2), (COL_FK, fox_g[1], 2),
                      (COL_MQ, moba_g[0], 2), (COL_MK, moba_g[1], 2),
                      (COL_WQ, swa_g[0], 2), (COL_WK, swa_g[1], 1)):
        row = lax.dynamic_update_slice(row, tiles(g, n), (col * LANES,))
    return row[None, :]


def kernel(x, norm_gain, w_in, b_forget, fox_qk_gain, moba_qk_gain, swa_qk_gain, sinks, w_out, rel_bias):
    b, t, d = x.shape
    depth = w_in.shape[0]
    w_in_r = _rearrange_w_in(w_in.astype(_MXU_DTYPE))
    w_out_c = w_out.astype(_MXU_DTYPE)
    moba_wide_bias = _moba_wide_bias_tiles(rel_bias)
    swa_bias = _swa_bias_tiles(rel_bias)
    moba_bias_max = jnp.max(jnp.abs(rel_bias[:, :rel_bias.shape[1] // 2].astype(_F32)))
    x2d = x.reshape(b * t, d)
    for layer in range(depth):
        qk_gain_row = _qk_gain_row(fox_qk_gain[layer], moba_qk_gain[layer], swa_qk_gain[layer])
        proj = _inproj(x2d, norm_gain[layer][None, :], w_in_r, layer, qk_gain_row)
        proj = proj.reshape(b, t, PROJ_WIDTH)
        bf_row = jnp.pad(b_forget[layer], (0, LANES - FOX_FORGET_COLS))[None, :]
        fox_bound = _qk_logit_bound(fox_qk_gain[layer])
        y_fox = lax.cond(2.0 * fox_bound <= BOUNDED_SOFTMAX_RANGE,
                         lambda: _fox_bounded(proj, b_forget[layer], fox_bound.reshape(1)),
                         lambda: _fox(proj, bf_row))
        moba_bound = _qk_logit_bound(moba_qk_gain[layer]) + moba_bias_max
        y_moba = lax.cond(2.0 * moba_bound <= BOUNDED_SOFTMAX_RANGE,
                          lambda: _moba_bounded(proj, rel_bias, moba_wide_bias,
                                                moba_bound.reshape(1)),
                          lambda: _moba(proj, rel_bias, _moba_bias_tiles(rel_bias)))
        y_sb = _sb(proj)
        y_swa = _swa(proj, sinks[layer], swa_bias)
        ys = [y.reshape(b * t, GROUP_WIDTH) for y in (y_fox, y_moba, y_sb, y_swa)]
        x2d = _outproj(x2d, ys, w_out_c, layer)
    return x2d.reshape(b, t, d)
```

```python
import functools
import math

import jax
import jax.numpy as jnp
from jax import lax
from jax.experimental import pallas as pl
from jax.experimental.pallas import tpu as pltpu

HEAD_DIM = 64
LANES = 128
GROUP_WIDTH = 256
MOBA_BLOCK = 256
MOBA_TOPK = 3
MOBA_MAX_BLOCKS = 16
SWA_WINDOW = 128
NUM_BUCKETS = 32
REL_MAX_DISTANCE = 1024
RMS_EPS = 1e-6
NEG_INF = -1e30
ATTN_SCALE = HEAD_DIM ** -0.5
LOG2E = math.log2(math.e)
BOUNDED_SOFTMAX_RANGE = 60.0
MOBA_NEAR_TILES = REL_MAX_DISTANCE // MOBA_BLOCK + 1
MOBA_WIDE = 2 * MOBA_BLOCK
MOBA_WIDE_NEAR_TILES = REL_MAX_DISTANCE // MOBA_WIDE + 1
FOX_FORGET_COLS = 4
FORGET_ROWS = 8
EXP_ZERO_CUTOFF = -104.0

COL_FQ, COL_FK, COL_FV, COL_FG = 0, 2, 4, 6
COL_MQ, COL_MK, COL_MV, COL_MG = 8, 10, 12, 14
COL_SQ, COL_SK, COL_SV, COL_SG = 16, 18, 20, 22
COL_WQ, COL_WK, COL_WV, COL_WG = 24, 26, 27, 28
COL_FF = 30
PROJ_WIDTH = (COL_FF + 1) * LANES
QK_NORM_COLS = (COL_FQ, COL_FQ + 1, COL_FK, COL_FK + 1, COL_MQ, COL_MQ + 1, COL_MK, COL_MK + 1,
                COL_WQ, COL_WQ + 1, COL_WK)

_MXU_DTYPE = jnp.bfloat16
_F32 = jnp.float32
_VMEM_LIMIT = 48 * 1024 * 1024
_VMEM_LIMIT_WIDE = 56 * 1024 * 1024

_NT = (((1,), (1,)), ((), ()))


def _lane_iota(shape):
    return lax.broadcasted_iota(jnp.int32, shape, len(shape) - 1)


def _pair_rms(x, gain_row):
    low = _lane_iota(x.shape) < HEAD_DIM
    sq = x * x
    ms_lo = jnp.sum(jnp.where(low, sq, 0.0), axis=-1, keepdims=True)
    ms_hi = jnp.sum(jnp.where(low, 0.0, sq), axis=-1, keepdims=True)
    ms = jnp.where(low, ms_lo, ms_hi) * (1.0 / HEAD_DIM)
    return x * lax.rsqrt(ms + RMS_EPS) * gain_row


def _silu(g):
    return g * (1.0 / (1.0 + jnp.exp(-g)))


def _split3(x):
    hi = x.astype(_MXU_DTYPE).astype(_F32)
    r = x - hi
    mid = r.astype(_MXU_DTYPE).astype(_F32)
    return hi, mid, r - mid


def _own_half(hh, shape):
    lane = _lane_iota(shape)
    return (lane < HEAD_DIM) if hh == 0 else (lane >= HEAD_DIM)


def _flash_update(s, v_tile, m, l, acc):
    m_new = jnp.maximum(m, jnp.max(s, axis=-1, keepdims=True))
    alpha = jnp.exp(m - m_new)
    p = jnp.exp(s - m_new)
    l = alpha * l + jnp.sum(p, axis=-1, keepdims=True)
    acc = alpha * acc + jnp.dot(p.astype(_MXU_DTYPE), v_tile, preferred_element_type=_F32)
    return m_new, l, acc


def _causal(tq):
    return (lax.broadcasted_iota(jnp.int32, (tq, tq), 0)
            >= lax.broadcasted_iota(jnp.int32, (tq, tq), 1))


def _inproj_kernel(x_ref, gain_ref, w_ref, qkg_ref, o_ref, *, tn):
    x = x_ref[...]
    ms = jnp.mean(x * x, axis=-1, keepdims=True)
    hn = (x * lax.rsqrt(ms + RMS_EPS) * gain_ref[...]).astype(_MXU_DTYPE)
    width = o_ref.shape[1]
    for n in range(pl.cdiv(width, tn)):
        hi = min((n + 1) * tn, width)
        acc = jnp.dot(hn, w_ref[:, n * tn:hi], preferred_element_type=_F32)
        for c in range(n * tn // LANES, hi // LANES):
            tile = acc[:, c * LANES - n * tn:(c + 1) * LANES - n * tn]
            if c in QK_NORM_COLS:
                tile = _pair_rms(tile, qkg_ref[:, c * LANES:(c + 1) * LANES])
            o_ref[:, c * LANES:(c + 1) * LANES] = tile


def _inproj(x2d, gain_row, w_all, layer, qk_gain_row):
    m, d = x2d.shape
    n = w_all.shape[2]
    tm = 512
    return pl.pallas_call(
        functools.partial(_inproj_kernel, tn=1024),
        grid=(m // tm,),
        in_specs=[pl.BlockSpec((tm, d), lambda i: (i, 0)),
                  pl.BlockSpec((1, d), lambda i: (0, 0)),
                  pl.BlockSpec((None, d, n), lambda i: (layer, 0, 0)),
                  pl.BlockSpec((1, n), lambda i: (0, 0))],
        out_specs=pl.BlockSpec((tm, n), lambda i: (i, 0)),
        out_shape=jax.ShapeDtypeStruct((m, n), _F32),
        compiler_params=pltpu.CompilerParams(
            dimension_semantics=("arbitrary",), vmem_limit_bytes=_VMEM_LIMIT),
        name="inproj",
    )(x2d, gain_row, w_all, qk_gain_row)


def _rel_bucket(dist):
    max_exact = NUM_BUCKETS // 2
    d = jnp.maximum(dist, 0)
    log_ratio = (jnp.log(jnp.maximum(d, 1).astype(_F32) / max_exact)
                 / math.log(REL_MAX_DISTANCE / max_exact))
    large = max_exact + (log_ratio * (NUM_BUCKETS - max_exact)).astype(jnp.int32)
    large = jnp.minimum(large, NUM_BUCKETS - 1)
    return jnp.where(d < max_exact, d, large)


def _bias_kernel(rb_ref, b_ref, o_ref, *, col0, scale):
    h = pl.program_id(0)
    for d in range(b_ref.shape[0]):
        buckets = b_ref[d]
        acc = jnp.zeros(buckets.shape, _F32)
        for k in range(NUM_BUCKETS):
            acc = jnp.where(buckets == k, rb_ref[k, col0 + h], acc)
        o_ref[0, d] = acc if scale == 1.0 else acc * scale


def _bias_lookup(rel_bias, buckets, col0, scale):
    n_heads = rel_bias.shape[1] // 2
    return pl.pallas_call(
        functools.partial(_bias_kernel, col0=col0, scale=scale),
        grid=(n_heads,),
        in_specs=[pl.BlockSpec(memory_space=pltpu.SMEM),
                  pl.BlockSpec(buckets.shape, lambda h: (0, 0, 0))],
        out_specs=pl.BlockSpec((1,) + buckets.shape, lambda h: (h, 0, 0, 0)),
        out_shape=jax.ShapeDtypeStruct((n_heads,) + buckets.shape, _F32),
        compiler_params=pltpu.CompilerParams(
            dimension_semantics=("arbitrary",), vmem_limit_bytes=_VMEM_LIMIT),
        name="bias_tiles",
    )(rel_bias, buckets)


def _toeplitz_buckets(size, count):
    i = jnp.arange(size)[:, None]
    j = jnp.arange(size)[None, :]
    return jnp.stack([_rel_bucket(d * size + i - j) for d in range(count)])


def _moba_bias_tiles(rel_bias):
    return _bias_lookup(rel_bias, _toeplitz_buckets(MOBA_BLOCK, MOBA_NEAR_TILES), 0, 1.0)


def _moba_wide_bias_tiles(rel_bias):
    return _bias_lookup(rel_bias, _toeplitz_buckets(MOBA_WIDE, MOBA_WIDE_NEAR_TILES), 0, LOG2E)


def _swa_bias_tiles(rel_bias):
    w = SWA_WINDOW
    buckets = _rel_bucket(jnp.arange(w)[:, None] + w - jnp.arange(2 * w)[None, :])
    return _bias_lookup(rel_bias, buckets[None], rel_bias.shape[1] // 2, LOG2E)[:, 0]


def _forget_cumsum(ff_ref, bf_ref, c_sc):
    t_len = c_sc.shape[0]
    ff = ff_ref[0] + bf_ref[...]
    log_f = jnp.minimum(ff, 0.0) - jnp.log(1.0 + jnp.exp(-jnp.abs(ff)))
    parts = [part.astype(_MXU_DTYPE) for part in _split3(log_f)]
    ch = 256
    tri = (lax.broadcasted_iota(jnp.int32, (ch, ch), 0)
           >= lax.broadcasted_iota(jnp.int32, (ch, ch), 1)).astype(_MXU_DTYPE)
    carry = jnp.zeros((1, LANES), _F32)
    for r in range(t_len // ch):
        inc = carry
        for part in reversed(parts):
            inc = inc + jnp.dot(tri, part[r * ch:(r + 1) * ch], preferred_element_type=_F32)
        c_sc[r * ch:(r + 1) * ch, :] = inc
        carry = inc[ch - 1:ch, :]


def _forget_cumsum_rows(ff_ref, bf_ref, c_rows):
    n_rows, t_len = c_rows.shape
    z = ff_ref[0].T[0:n_rows, :] + bf_ref[...]
    log_f = jnp.minimum(z, 0.0) - jnp.log(1.0 + jnp.exp(-jnp.abs(z)))
    parts = [part.astype(_MXU_DTYPE) for part in _split3(log_f)]
    ch = 256
    tri = (lax.broadcasted_iota(jnp.int32, (ch, ch), 0)
           <= lax.broadcasted_iota(jnp.int32, (ch, ch), 1)).astype(_MXU_DTYPE)
    carry = jnp.zeros((n_rows, 1), _F32)
    for r in range(t_len // ch):
        inc = carry
        for part in reversed(parts):
            inc = inc + jnp.dot(part[:, r * ch:(r + 1) * ch], tri, preferred_element_type=_F32)
        c_rows[:, r * ch:(r + 1) * ch] = inc
        carry = inc[:, ch - 1:ch]


def _lane_column(x, col):
    return jnp.sum(jnp.where(_lane_iota(x.shape) == col, x, 0.0), axis=-1, keepdims=True)


def _lane_fields(shape, a0, fields):
    lane = _lane_iota(shape)
    out = jnp.zeros(shape, _F32)
    for n, f in enumerate(fields):
        out = jnp.where(lane == a0 + n, f, out)
    return out


def _write_ones_column(vext_sc, h, hh):
    @pl.when(pl.program_id(0) == 0)
    def _():
        lane = _lane_iota((vext_sc.shape[1], LANES))
        vext_sc[h, :, LANES:2 * LANES] = jnp.where(lane == hh, 1.0, 0.0).astype(_MXU_DTYPE)


def _bounded_weights(ps_ref, q_augs, kaug_sc, r, bias, diagonal):
    tq = q_augs[0].shape[0]
    if not diagonal:
        for h in range(len(q_augs)):
            s = lax.dot_general(q_augs[h], kaug_sc[h, pl.ds(r, tq), :], _NT,
                                preferred_element_type=_F32)
            if bias is not None:
                s = s + bias(h)
            ps_ref[h] = jnp.exp2(s).astype(_MXU_DTYPE)
        return
    half = tq // 2
    rows = lax.broadcasted_iota(jnp.int32, (tq, half), 0)
    cols = lax.broadcasted_iota(jnp.int32, (tq, half), 1)
    for h in range(len(q_augs)):
        b_h = None if bias is None else bias(h)
        s = lax.dot_general(q_augs[h], kaug_sc[h, pl.ds(r, half), :], _NT,
                            preferred_element_type=_F32)
        if b_h is not None:
            s = s + b_h[:, 0:half]
        ps_ref[h, :, 0:half] = jnp.exp2(jnp.where(rows >= cols, s, NEG_INF)).astype(_MXU_DTYPE)
        s = lax.dot_general(q_augs[h][half:, :], kaug_sc[h, pl.ds(r + half, half), :], _NT,
                            preferred_element_type=_F32)
        if b_h is not None:
            s = s + b_h[half:, half:]
        ps_ref[h, 0:half, half:] = jnp.zeros((half, half), _MXU_DTYPE)
        ps_ref[h, half:, half:] = jnp.exp2(jnp.where(_causal(half), s, NEG_INF)).astype(_MXU_DTYPE)


def _bounded_values(ps_ref, vext_sc, r, acc_sc):
    tq = ps_ref.shape[1]
    for p in range(acc_sc.shape[0]):
        acc_sc[p] += (
            jnp.dot(ps_ref[2 * p], vext_sc[2 * p, pl.ds(r, tq), :], preferred_element_type=_F32)
            + jnp.dot(ps_ref[2 * p + 1], vext_sc[2 * p + 1, pl.ds(r, tq), :],
                      preferred_element_type=_F32))


def _bounded_store(acc_sc, g_ref, o_ref):
    tq = acc_sc.shape[1]
    for p in range(acc_sc.shape[0]):
        acc = acc_sc[p]
        den = acc[:, LANES:]
        l = jnp.where(_own_half(0, (tq, LANES)), den[:, 0:1], den[:, 1:2])
        cols = slice(p * LANES, (p + 1) * LANES)
        o_ref[0, :, cols] = (acc[:, :LANES] / l * _silu(g_ref[0, :, cols])).astype(o_ref.dtype)


def _bounded_pipeline_step(n, ps_bufs, weights, values):
    def new_in(slot):
        def run():
            weights(ps_bufs[slot])
            values(ps_bufs[1 - slot])
        return run
    lax.cond(n % 2 == 0, new_in(0), new_in(1))


def _bounded_finish(n_done, ps_bufs, values):
    lax.cond(n_done % 2 == 0, lambda: values(ps_bufs[0]), lambda: values(ps_bufs[1]))


def _fox_extra_base(h):
    p, hh = divmod(h, 2)
    return (HEAD_DIM if hh == 0 else 0) + 8 * p


def _fox_extra_lanes(fields):
    rows = max(f.shape[1] for fs in fields for f in fs if hasattr(f, "shape"))
    sub = lax.broadcasted_iota(jnp.int32, (8, rows), 0)
    blocks = {}
    for h, fs in enumerate(fields):
        blk = jnp.zeros((8, rows), _F32)
        for n, f in enumerate(fs):
            blk = jnp.where(sub == n, f, blk)
        blocks[_fox_extra_base(h)] = blk
    pieces, at = [], 0
    for base in sorted(blocks):
        if base > at:
            pieces.append(jnp.zeros((base - at, rows), _F32))
        pieces.append(blocks[base])
        at = base + 8
    pieces.append(jnp.zeros((LANES - at, rows), _F32))
    return jnp.concatenate(pieces, axis=0).T


def _fox_bounded_kernel(bound_ref, q_ref, k_ref, v_ref, ff_ref, g_ref, bf_ref, o_ref,
                        kaug_sc, vext_sc, c_sc, ps_a, ps_b, acc_sc):
    i = pl.program_id(1)
    tq = q_ref.shape[1]
    n_heads = kaug_sc.shape[0]

    @pl.when(i == 0)
    def _prep():
        _forget_cumsum_rows(ff_ref, bf_ref, c_sc)
        hi, mid, lo = _split3(-(c_sc[...] * LOG2E))
        extra = _fox_extra_lanes([[hi[h:h + 1], mid[h:h + 1], lo[h:h + 1], 1.0, 1.0, 1.0, 1.0, 1.0]
                                  for h in range(n_heads)])
        lane = _lane_iota(extra.shape)
        for h in range(n_heads):
            p, hh = divmod(h, 2)
            kn = k_ref[0, :, p * LANES:(p + 1) * LANES]
            v = v_ref[0, :, p * LANES:(p + 1) * LANES]
            a0 = _fox_extra_base(h)
            aug = jnp.where((lane >= a0) & (lane < a0 + 8), extra, 0.0)
            own = _own_half(hh, kn.shape)
            kaug_sc[h] = jnp.where(own, kn, aug).astype(_MXU_DTYPE)
            vext_sc[h, :, 0:LANES] = jnp.where(own, v, 0.0).astype(_MXU_DTYPE)
            _write_ones_column(vext_sc, h, hh)

    row0 = pl.multiple_of(i * tq, tq)
    off = jnp.full((1, 1), -LOG2E, _F32) * bound_ref[0]
    off_hi = off.astype(_MXU_DTYPE).astype(_F32)
    hi, mid, lo = _split3(c_sc[:, pl.ds(row0, tq)] * LOG2E)
    extra = _fox_extra_lanes([[1.0, 1.0, 1.0, hi[h:h + 1], mid[h:h + 1], lo[h:h + 1],
                               off_hi, off - off_hi] for h in range(n_heads)])
    lane = _lane_iota(extra.shape)
    q_augs = []
    for h in range(n_heads):
        p, hh = divmod(h, 2)
        q = q_ref[0, :, p * LANES:(p + 1) * LANES] * (ATTN_SCALE * LOG2E)
        a0 = _fox_extra_base(h)
        aug = jnp.where((lane >= a0) & (lane < a0 + 8), extra, 0.0)
        q_augs.append(jnp.where(_own_half(hh, q.shape), q, aug).astype(_MXU_DTYPE))

    chunk_lane = _lane_iota((c_sc.shape[0], LANES))
    head_rows = lax.broadcasted_iota(jnp.int32, (c_sc.shape[0], 1), 0) < n_heads
    c_first = jnp.max(jnp.where(chunk_lane == 0, c_sc[:, pl.ds(row0, LANES)], -jnp.inf),
                      axis=-1, keepdims=True)
    cutoff = EXP_ZERO_CUTOFF - 2.0 * bound_ref[0]

    def live(j):
        start = pl.multiple_of(jnp.maximum(j, 0) * tq + (tq - LANES), LANES)
        c_last = jnp.max(jnp.where(chunk_lane == LANES - 1, c_sc[:, pl.ds(start, LANES)], -jnp.inf),
                         axis=-1, keepdims=True)
        gap = jnp.max(jnp.where(head_rows, c_first - c_last, -jnp.inf))
        return (j >= 0) & (gap >= cutoff)

    ps_bufs = (ps_a, ps_b)

    def values_at(r):
        return lambda ps_ref: _bounded_values(ps_ref, vext_sc, pl.multiple_of(r, tq), acc_sc)

    def body(state):
        j, _, r_prev, n = state
        r = pl.multiple_of(j * tq, tq)
        _bounded_pipeline_step(
            n, ps_bufs,
            lambda ps_ref: _bounded_weights(ps_ref, q_augs, kaug_sc, r, None, False),
            values_at(r_prev))
        return j - 1, live(j - 1), r, n + 1

    acc_sc[...] = jnp.zeros(acc_sc.shape, _F32)
    _bounded_weights(ps_a, q_augs, kaug_sc, row0, None, True)
    state = lax.while_loop(lambda state: state[1], body,
                           (i - 1, live(i - 1), row0, jnp.int32(1)))
    _bounded_finish(state[3] - 1, ps_bufs, values_at(state[2]))
    _bounded_store(acc_sc, g_ref, o_ref)


def _fox_bounded(proj, b_forget, bound, *, tq=512):
    b, t, _ = proj.shape
    bf_col = jnp.pad(b_forget.astype(_F32), (0, FORGET_ROWS - FOX_FORGET_COLS))[:, None]
    n_heads = GROUP_WIDTH // HEAD_DIM
    wide = GROUP_WIDTH // LANES
    return pl.pallas_call(
        _fox_bounded_kernel,
        grid=(b, t // tq),
        in_specs=[pl.BlockSpec(memory_space=pltpu.SMEM),
                  pl.BlockSpec((1, tq, GROUP_WIDTH), lambda bb, i: (bb, i, COL_FQ // wide)),
                  pl.BlockSpec((1, t, GROUP_WIDTH), lambda bb, i: (bb, 0, COL_FK // wide)),
                  pl.BlockSpec((1, t, GROUP_WIDTH), lambda bb, i: (bb, 0, COL_FV // wide)),
                  pl.BlockSpec((1, t, LANES), lambda bb, i: (bb, 0, COL_FF)),
                  pl.BlockSpec((1, tq, GROUP_WIDTH), lambda bb, i: (bb, i, COL_FG // wide)),
                  pl.BlockSpec((FORGET_ROWS, 1), lambda bb, i: (0, 0))],
        out_specs=pl.BlockSpec((1, tq, GROUP_WIDTH), lambda bb, i: (bb, i, 0)),
        out_shape=jax.ShapeDtypeStruct((b, t, GROUP_WIDTH), _MXU_DTYPE),
        scratch_shapes=[pltpu.VMEM((n_heads, t, LANES), _MXU_DTYPE),
                        pltpu.VMEM((n_heads, t, 2 * LANES), _MXU_DTYPE),
                        pltpu.VMEM((FORGET_ROWS, t), _F32),
                        pltpu.VMEM((n_heads, tq, tq), _MXU_DTYPE),
                        pltpu.VMEM((n_heads, tq, tq), _MXU_DTYPE),
                        pltpu.VMEM((n_heads // 2, tq, 2 * LANES), _F32)],
        compiler_params=pltpu.CompilerParams(
            dimension_semantics=("arbitrary", "arbitrary"),
            vmem_limit_bytes=_VMEM_LIMIT_WIDE),
        name="fox_bounded",
    )(bound, proj, proj, proj, proj, proj, bf_col)


def _fox_kernel(q_ref, k_ref, v_ref, ff_ref, g_ref, bf_ref, o_ref,
                kaug_sc, v_sc, c_sc, *, tq):
    p = pl.program_id(1)
    i = pl.program_id(2)

    def head_column(x, hh):
        return _lane_column(x, 2 * p + hh)

    @pl.when(i == 0)
    def _prep():
        pl.when(p == 0)(lambda: _forget_cumsum(ff_ref, bf_ref, c_sc))
        kn = k_ref[0]
        c_all = c_sc[...]
        for hh in range(2):
            a0 = HEAD_DIM if hh == 0 else 0
            hi, mid, lo = _split3(-head_column(c_all, hh))
            aug = _lane_fields(kn.shape, a0, [hi, mid, lo, 1.0, 1.0, 1.0])
            kaug_sc[hh] = jnp.where(_own_half(hh, kn.shape), kn, aug).astype(_MXU_DTYPE)
        v_sc[...] = v_ref[0].astype(_MXU_DTYPE)

    q = q_ref[0] * ATTN_SCALE
    row0 = pl.multiple_of(i * tq, tq)
    c_t = c_sc[pl.ds(row0, tq), :]
    outs = []
    for hh in range(2):
        a0 = HEAD_DIM if hh == 0 else 0
        hi, mid, lo = _split3(head_column(c_t, hh))
        aug = _lane_fields(q.shape, a0, [1.0, 1.0, 1.0, hi, mid, lo])
        q_aug = jnp.where(_own_half(hh, q.shape), q, aug).astype(_MXU_DTYPE)

        s = lax.dot_general(q_aug, kaug_sc[hh, pl.ds(row0, tq), :], _NT,
                            preferred_element_type=_F32)
        s = jnp.where(_causal(tq), s, NEG_INF)
        m = jnp.max(s, axis=-1, keepdims=True)
        pr = jnp.exp(s - m)
        l = jnp.sum(pr, axis=-1, keepdims=True)
        acc = jnp.dot(pr.astype(_MXU_DTYPE), v_sc[pl.ds(row0, tq), :], preferred_element_type=_F32)

        def body(j, carry, q_aug=q_aug, hh=hh):
            r = pl.multiple_of(j * tq, tq)
            s = lax.dot_general(q_aug, kaug_sc[hh, pl.ds(r, tq), :], _NT,
                                preferred_element_type=_F32)
            return _flash_update(s, v_sc[pl.ds(r, tq), :], *carry)

        m, l, acc = lax.fori_loop(0, i, body, (m, l, acc))
        outs.append(acc / l)
    o = jnp.where(_own_half(0, outs[0].shape), outs[0], outs[1])
    o_ref[0] = (o * _silu(g_ref[0])).astype(o_ref.dtype)


def _fox(proj, bf_row, *, tq=256):
    b, t, _ = proj.shape
    return pl.pallas_call(
        functools.partial(_fox_kernel, tq=tq),
        grid=(b, 2, t // tq),
        in_specs=[pl.BlockSpec((1, tq, LANES), lambda bb, p, i: (bb, i, COL_FQ + p)),
                  pl.BlockSpec((1, t, LANES), lambda bb, p, i: (bb, 0, COL_FK + p)),
                  pl.BlockSpec((1, t, LANES), lambda bb, p, i: (bb, 0, COL_FV + p)),
                  pl.BlockSpec((1, t, LANES), lambda bb, p, i: (bb, 0, COL_FF)),
                  pl.BlockSpec((1, tq, LANES), lambda bb, p, i: (bb, i, COL_FG + p)),
                  pl.BlockSpec((1, LANES), lambda bb, p, i: (0, 0))],
        out_specs=pl.BlockSpec((1, tq, LANES), lambda bb, p, i: (bb, i, p)),
        out_shape=jax.ShapeDtypeStruct((b, t, GROUP_WIDTH), _MXU_DTYPE),
        scratch_shapes=[pltpu.VMEM((2, t, LANES), _MXU_DTYPE),
                        pltpu.VMEM((t, LANES), _MXU_DTYPE),
                        pltpu.VMEM((t, LANES), _F32)],
        compiler_params=pltpu.CompilerParams(
            dimension_semantics=("arbitrary", "arbitrary", "arbitrary"),
            vmem_limit_bytes=_VMEM_LIMIT),
        name="fox",
    )(proj, proj, proj, proj, proj, bf_row)


def _moba_block_means(kn, kmean_sc):
    kmean_sc[...] = jnp.zeros(kmean_sc.shape, _F32)
    for n in range(kn.shape[0] // MOBA_BLOCK):
        kmean_sc[n:n + 1, :] = jnp.mean(kn[n * MOBA_BLOCK:(n + 1) * MOBA_BLOCK], axis=0, keepdims=True)


def _moba_select(q_head, kmean, past):
    gate = lax.dot_general(q_head, kmean, _NT, precision=lax.Precision.HIGHEST,
                           preferred_element_type=_F32)
    lane_f = _lane_iota(gate.shape).astype(_F32)
    cand = jnp.where(past, gate, -jnp.inf)
    sel = jnp.zeros(gate.shape, _F32)
    for _ in range(MOBA_TOPK):
        mx = jnp.max(cand, axis=-1, keepdims=True)
        is_max = (cand == mx) & (mx > -jnp.inf)
        first = jnp.min(jnp.where(is_max, lane_f, float(LANES)), axis=-1, keepdims=True)
        pick = lane_f == first
        sel = jnp.where(pick, 1.0, sel)
        cand = jnp.where(pick, -jnp.inf, cand)
    return sel


def _moba_bounded_kernel(rb_ref, bound_ref, q_ref, k_ref, v_ref, g_ref, bias_ref, o_ref,
                         kaug_sc, vext_sc, kmean_sc, ps_a, ps_b, acc_sc):
    i = pl.program_id(1)
    tq = q_ref.shape[1]
    n_heads = kaug_sc.shape[0]
    blk = MOBA_BLOCK
    near = MOBA_WIDE_NEAR_TILES

    @pl.when(i == 0)
    def _prep():
        lane = _lane_iota((k_ref.shape[1], LANES))
        row_blk = lax.broadcasted_iota(jnp.int32, lane.shape, 0) // blk
        for h in range(n_heads):
            p, hh = divmod(h, 2)
            kn = k_ref[0, :, p * LANES:(p + 1) * LANES]
            v = v_ref[0, :, p * LANES:(p + 1) * LANES]
            if hh == 0:
                _moba_block_means(kn, kmean_sc.at[p])
            a0 = HEAD_DIM if hh == 0 else 0
            aug = jnp.where((lane - a0 == row_blk) | (lane - a0 - MOBA_MAX_BLOCKS == row_blk)
                            | (lane - a0 == 2 * MOBA_MAX_BLOCKS)
                            | (lane - a0 == 2 * MOBA_MAX_BLOCKS + 1), 1.0, 0.0)
            own = _own_half(hh, kn.shape)
            kaug_sc[h] = jnp.where(own, kn, aug).astype(_MXU_DTYPE)
            vext_sc[h, :, 0:LANES] = jnp.where(own, v, 0.0).astype(_MXU_DTYPE)
            _write_ones_column(vext_sc, h, hh)

    row0 = pl.multiple_of(i * tq, tq)
    off = jnp.full((1, 1), -LOG2E, _F32) * bound_ref[0]
    off_hi = off.astype(_MXU_DTYPE).astype(_F32)
    blk_n = lax.broadcasted_iota(jnp.int32, (MOBA_MAX_BLOCKS, tq), 0)
    blk_f = blk_n.astype(_F32)
    q_blk = i * (tq // blk) + lax.broadcasted_iota(jnp.int32, (MOBA_MAX_BLOCKS, tq), 1) // blk
    past = blk_n < q_blk
    far = i - blk_n // 2 >= near
    sub8 = lax.broadcasted_iota(jnp.int32, (8, tq), 0)
    off_rows = jnp.where(sub8 == 0, off_hi, jnp.where(sub8 == 1, off - off_hi, 0.0))
    gates = []
    for p in range(n_heads // 2):
        means = kmean_sc[p, 0:MOBA_MAX_BLOCKS, :]
        first = _own_half(0, means.shape)
        stacked = jnp.concatenate([jnp.where(first, means, 0.0), jnp.where(first, 0.0, means)], axis=0)
        gates.append(lax.dot_general(stacked, q_ref[0, :, p * LANES:(p + 1) * LANES], _NT,
                                     precision=lax.Precision.HIGHEST, preferred_element_type=_F32))
    q_augs = []
    for h in range(n_heads):
        p, hh = divmod(h, 2)
        qn = q_ref[0, :, p * LANES:(p + 1) * LANES]
        own = _own_half(hh, qn.shape)
        cand = jnp.where(past, gates[p][MOBA_MAX_BLOCKS * hh:MOBA_MAX_BLOCKS * (hh + 1), :], -jnp.inf)
        sel = jnp.zeros((MOBA_MAX_BLOCKS, tq), _F32)
        for _ in range(MOBA_TOPK):
            mx = jnp.max(cand, axis=0, keepdims=True)
            is_max = (cand == mx) & (mx > -jnp.inf)
            first = jnp.min(jnp.where(is_max, blk_f, float(LANES)), axis=0, keepdims=True)
            pick = blk_f == first
            sel = jnp.where(pick, 1.0, sel)
            cand = jnp.where(pick, -jnp.inf, cand)
        c_far = jnp.full((1, 1), LOG2E, _F32) * rb_ref[NUM_BUCKETS - 1, h]
        far_hi = c_far.astype(_MXU_DTYPE).astype(_F32)
        pen = jnp.where((blk_n == q_blk) | (past & (sel != 0.0)), 0.0, NEG_INF)
        fields = jnp.concatenate([pen + jnp.where(far, far_hi, 0.0),
                                  jnp.where(far, c_far - far_hi, 0.0),
                                  off_rows,
                                  jnp.zeros((HEAD_DIM - 2 * MOBA_MAX_BLOCKS - 8, tq), _F32)],
                                 axis=0)
        blank = jnp.zeros((HEAD_DIM, tq), _F32)
        aug = jnp.concatenate([blank, fields] if hh == 0 else [fields, blank], axis=0).T
        q_augs.append(jnp.where(own, qn * (ATTN_SCALE * LOG2E), aug).astype(_MXU_DTYPE))

    ps_bufs = (ps_a, ps_b)

    def values_at(r):
        return lambda ps_ref: _bounded_values(ps_ref, vext_sc, pl.multiple_of(r, tq), acc_sc)

    def body(j, state, with_bias):
        r_prev, n = state
        r = pl.multiple_of(j * tq, tq)
        bias = (lambda h: bias_ref[h, i - j]) if with_bias else None
        _bounded_pipeline_step(
            n, ps_bufs,
            lambda ps_ref: _bounded_weights(ps_ref, q_augs, kaug_sc, r, bias, False),
            values_at(r_prev))
        return r, n + 1

    acc_sc[...] = jnp.zeros(acc_sc.shape, _F32)
    _bounded_weights(ps_a, q_augs, kaug_sc, row0, lambda h: bias_ref[h, 0], True)
    state = (row0, jnp.int32(1))
    near_lo = jnp.maximum(i - (near - 1), 0)
    state = lax.fori_loop(near_lo, i, functools.partial(body, with_bias=True), state)
    state = lax.fori_loop(0, near_lo, functools.partial(body, with_bias=False), state)
    _bounded_finish(state[1] - 1, ps_bufs, values_at(state[0]))
    _bounded_store(acc_sc, g_ref, o_ref)


def _moba_bounded(proj, rel_bias, bias_tiles, bound):
    b, t, _ = proj.shape
    tq = MOBA_WIDE
    assert t % tq == 0 and t // MOBA_BLOCK <= MOBA_MAX_BLOCKS
    n_heads = GROUP_WIDTH // HEAD_DIM
    wide = GROUP_WIDTH // LANES
    resident = pl.Buffered(1)
    return pl.pallas_call(
        _moba_bounded_kernel,
        grid=(b, t // tq),
        in_specs=[pl.BlockSpec(memory_space=pltpu.SMEM),
                  pl.BlockSpec(memory_space=pltpu.SMEM),
                  pl.BlockSpec((1, tq, GROUP_WIDTH), lambda bb, i: (bb, i, COL_MQ // wide)),
                  pl.BlockSpec((1, t, GROUP_WIDTH), lambda bb, i: (bb, 0, COL_MK // wide)),
                  pl.BlockSpec((1, t, GROUP_WIDTH), lambda bb, i: (bb, 0, COL_MV // wide)),
                  pl.BlockSpec((1, tq, GROUP_WIDTH), lambda bb, i: (bb, i, COL_MG // wide)),
                  pl.BlockSpec((n_heads, MOBA_WIDE_NEAR_TILES, tq, tq), lambda bb, i: (0, 0, 0, 0),
                               pipeline_mode=resident)],
        out_specs=pl.BlockSpec((1, tq, GROUP_WIDTH), lambda bb, i: (bb, i, 0)),
        out_shape=jax.ShapeDtypeStruct((b, t, GROUP_WIDTH), _MXU_DTYPE),
        scratch_shapes=[pltpu.VMEM((n_heads, t, LANES), _MXU_DTYPE),
                        pltpu.VMEM((n_heads, t, 2 * LANES), _MXU_DTYPE),
                        pltpu.VMEM((n_heads // 2, LANES, LANES), _F32),
                        pltpu.VMEM((n_heads, tq, tq), _MXU_DTYPE),
                        pltpu.VMEM((n_heads, tq, tq), _MXU_DTYPE),
                        pltpu.VMEM((n_heads // 2, tq, 2 * LANES), _F32)],
        compiler_params=pltpu.CompilerParams(
            dimension_semantics=("arbitrary", "arbitrary"),
            vmem_limit_bytes=_VMEM_LIMIT_WIDE),
        name="moba_bounded",
    )(rel_bias, bound, proj, proj, proj, proj, bias_tiles)


def _moba_kernel(rb_ref, q_ref, k_ref, v_ref, g_ref, bias_ref, o_ref,
                 kaug_sc, v_sc, kmean_sc):
    p = pl.program_id(1)
    i = pl.program_id(2)
    blk = MOBA_BLOCK
    near = MOBA_NEAR_TILES

    @pl.when(i == 0)
    def _prep():
        kn = k_ref[0]
        _moba_block_means(kn, kmean_sc)
        lane = _lane_iota(kn.shape)
        row_blk = lax.broadcasted_iota(jnp.int32, kn.shape, 0) // blk
        for hh in range(2):
            a0 = HEAD_DIM if hh == 0 else 0
            onehot = jnp.where((lane - a0 == row_blk) | (lane - a0 - MOBA_MAX_BLOCKS == row_blk),
                               1.0, 0.0)
            kaug_sc[hh] = jnp.where(_own_half(hh, kn.shape), kn, onehot).astype(_MXU_DTYPE)
        v_sc[...] = v_ref[0].astype(_MXU_DTYPE)

    qn = q_ref[0]
    row0 = pl.multiple_of(i * blk, blk)
    lane = _lane_iota(qn.shape)
    past = lane < i
    outs = []
    for hh in range(2):
        own = _own_half(hh, qn.shape)
        sel = _moba_select(jnp.where(own, qn, 0.0), kmean_sc[...], past)
        c_far = jnp.full((1, LANES), rb_ref[NUM_BUCKETS - 1, 2 * p + hh], _F32)
        far_hi = c_far.astype(_MXU_DTYPE).astype(_F32)
        far_lo = c_far - far_hi
        pen = jnp.where(past & (sel == 0.0), NEG_INF, 0.0)
        aug = jnp.where(lane < MOBA_MAX_BLOCKS, pen + jnp.where(i - lane >= near, far_hi, 0.0),
                        jnp.where((lane < 2 * MOBA_MAX_BLOCKS)
                                  & (i - (lane - MOBA_MAX_BLOCKS) >= near), far_lo, 0.0))
        if hh == 0:
            aug = pltpu.roll(aug, HEAD_DIM, axis=1)
        q_aug = jnp.where(own, qn * ATTN_SCALE, aug).astype(_MXU_DTYPE)

        def scores(r, q_aug=q_aug, hh=hh):
            return lax.dot_general(q_aug, kaug_sc[hh, pl.ds(r, blk), :], _NT,
                                   preferred_element_type=_F32)

        s = scores(row0) + bias_ref[hh, 0]
        s = jnp.where(_causal(blk), s, NEG_INF)
        m = jnp.max(s, axis=-1, keepdims=True)
        pr = jnp.exp(s - m)
        l = jnp.sum(pr, axis=-1, keepdims=True)
        acc = jnp.dot(pr.astype(_MXU_DTYPE), v_sc[pl.ds(row0, blk), :], preferred_element_type=_F32)

        def near_body(j, carry, hh=hh, scores=scores):
            r = pl.multiple_of(j * blk, blk)
            return _flash_update(scores(r) + bias_ref[hh, i - j], v_sc[pl.ds(r, blk), :], *carry)

        def far_body(j, carry, scores=scores):
            r = pl.multiple_of(j * blk, blk)
            return _flash_update(scores(r), v_sc[pl.ds(r, blk), :], *carry)

        near_lo = jnp.maximum(i - (near - 1), 0)
        carry = lax.fori_loop(near_lo, i, near_body, (m, l, acc))
        m, l, acc = lax.fori_loop(0, near_lo, far_body, carry)
        outs.append(acc / l)
    o = jnp.where(_own_half(0, outs[0].shape), outs[0], outs[1])
    o_ref[0] = (o * _silu(g_ref[0])).astype(o_ref.dtype)


def _moba(proj, rel_bias, bias_tiles):
    b, t, _ = proj.shape
    blk = MOBA_BLOCK
    assert t % blk == 0 and t // blk <= MOBA_MAX_BLOCKS
    return pl.pallas_call(
        _moba_kernel,
        grid=(b, 2, t // blk),
        in_specs=[pl.BlockSpec(memory_space=pltpu.SMEM),
                  pl.BlockSpec((1, blk, LANES), lambda bb, p, i: (bb, i, COL_MQ + p)),
                  pl.BlockSpec((1, t, LANES), lambda bb, p, i: (bb, 0, COL_MK + p)),
                  pl.BlockSpec((1, t, LANES), lambda bb, p, i: (bb, 0, COL_MV + p)),
                  pl.BlockSpec((1, blk, LANES), lambda bb, p, i: (bb, i, COL_MG + p)),
                  pl.BlockSpec((2, MOBA_NEAR_TILES, blk, blk), lambda bb, p, i: (p, 0, 0, 0))],
        out_specs=pl.BlockSpec((1, blk, LANES), lambda bb, p, i: (bb, i, p)),
        out_shape=jax.ShapeDtypeStruct((b, t, GROUP_WIDTH), _MXU_DTYPE),
        scratch_shapes=[pltpu.VMEM((2, t, LANES), _MXU_DTYPE),
                        pltpu.VMEM((t, LANES), _MXU_DTYPE),
                        pltpu.VMEM((LANES, LANES), _F32)],
        compiler_params=pltpu.CompilerParams(
            dimension_semantics=("arbitrary", "arbitrary", "arbitrary"),
            vmem_limit_bytes=_VMEM_LIMIT),
        name="moba",
    )(rel_bias, proj, proj, proj, proj, bias_tiles)


def _sb_kernel(q_ref, k_ref, v_ref, g_ref, o_ref, k_sc, v_sc, *, tq):
    i = pl.program_id(1)
    n_heads = v_sc.shape[0]

    @pl.when(i == 0)
    def _prep():
        for p in range(n_heads // 2):
            k_sc[p] = k_ref[0, :, p * LANES:(p + 1) * LANES].astype(_MXU_DTYPE)
            v = v_ref[0, :, p * LANES:(p + 1) * LANES]
            for hh in range(2):
                v_sc[2 * p + hh] = jnp.where(_own_half(hh, v.shape), v, 0.0).astype(_MXU_DTYPE)

    subs = q_ref.shape[1] // tq
    q_heads = []
    for u in range(subs):
        q_heads.append([])
        for h in range(n_heads):
            p, hh = divmod(h, 2)
            q = q_ref[0, u * tq:(u + 1) * tq, p * LANES:(p + 1) * LANES] * (ATTN_SCALE * LOG2E)
            q_heads[u].append(jnp.where(_own_half(hh, q.shape), q, 0.0).astype(_MXU_DTYPE))
    strict = (lax.broadcasted_iota(jnp.int32, (tq, tq), 0)
              > lax.broadcasted_iota(jnp.int32, (tq, tq), 1))
    after = strict.astype(_MXU_DTYPE)
    sign_bit = jnp.uint32(0x80000000)

    def pair(back, runs, diagonal):
        chains = [(u, t, h) for u in range(subs) for t in range(2) for h in range(n_heads)]
        tile_idx = [[subs * i + u - back - t for t in range(2)] for u in range(subs)]
        valid = [[(j >= 0).astype(_F32) for j in js] for js in tile_idx]
        rows = [[pl.multiple_of(jnp.maximum(j, 0) * tq, tq) for j in js] for js in tile_idx]
        masked = [diagonal and t == 0 for _, t, _ in chains]
        zs = [lax.dot_general(q_heads[u][h], k_sc[h // 2, pl.ds(rows[u][t], tq), :], _NT,
                              preferred_element_type=_F32) for u, t, h in chains]
        drops = []
        for z, msk in zip(zs, masked):
            neg_abs = lax.bitcast_convert_type(lax.bitcast_convert_type(z, jnp.uint32) | sign_bit, _F32)
            drop = jnp.maximum(z, 0.0) + jnp.log(1.0 + jnp.exp2(neg_abs)) * LOG2E
            drops.append(jnp.where(strict, drop, 0.0) if msk else drop)
        laters = []
        for drop in drops:
            hi = drop.astype(_MXU_DTYPE)
            lo = (drop - hi.astype(_F32)).astype(_MXU_DTYPE)
            laters.append(jnp.dot(hi, after, preferred_element_type=_F32)
                          + jnp.dot(lo, after, preferred_element_type=_F32))
        pvs = []
        for (u, t, h), z, drop, later, msk in zip(chains, zs, drops, laters, masked):
            w = jnp.exp2(z - drop - later)
            if msk:
                w = jnp.where(strict, w, 0.0)
            pvs.append(jnp.dot(w.astype(_MXU_DTYPE), v_sc[h, pl.ds(rows[u][t], tq), :],
                               preferred_element_type=_F32))
        sums = [jnp.sum(drop, axis=-1, keepdims=True) for drop in drops]
        new_runs, outs = [], []
        for u in range(subs):
            new_runs.append([])
            for p in range(n_heads // 2):
                out = None
                for h in (2 * p, 2 * p + 1):
                    run = runs[u][h]
                    for t in range(2):
                        c = chains.index((u, t, h))
                        pv = pvs[c] * (jnp.exp2(run) * valid[u][t])
                        out = pv if out is None else out + pv
                        run = run - sums[c] * valid[u][t]
                    new_runs[u].append(run)
                outs.append(out)
        return new_runs, outs

    def alive(runs):
        top = functools.reduce(jnp.maximum, [r for rs in runs for r in rs])
        return jnp.max(top) >= EXP_ZERO_CUTOFF * LOG2E

    def flat(runs):
        return [r for rs in runs for r in rs]

    def nested(flat_runs):
        return [flat_runs[n_heads * u:n_heads * (u + 1)] for u in range(subs)]

    n_runs = n_heads * subs
    zero = jnp.zeros((tq, 1), _F32)
    runs, accs = pair(0, [[zero] * n_heads for _ in range(subs)], True)

    def body(state):
        back = state[0]
        runs, outs = pair(back, nested(list(state[2:2 + n_runs])), False)
        accs = [a + o for a, o in zip(state[2 + n_runs:], outs)]
        return (back + 2, alive(runs), *flat(runs), *accs)

    state = lax.while_loop(lambda state: (subs * i + subs - 1 - state[0] >= 0) & state[1], body,
                           (jnp.int32(2), alive(runs), *flat(runs), *accs))
    accs = state[2 + n_runs:]
    for u in range(subs):
        for p in range(n_heads // 2):
            sl = (0, slice(u * tq, (u + 1) * tq), slice(p * LANES, (p + 1) * LANES))
            o_ref[sl] = (accs[u * (n_heads // 2) + p] * _silu(g_ref[sl])).astype(o_ref.dtype)


def _sb(proj, *, tq=256, subs=2):
    b, t, _ = proj.shape
    rows = tq * subs
    n_heads = GROUP_WIDTH // HEAD_DIM
    wide = GROUP_WIDTH // LANES
    return pl.pallas_call(
        functools.partial(_sb_kernel, tq=tq),
        grid=(b, t // rows),
        in_specs=[pl.BlockSpec((1, rows, GROUP_WIDTH), lambda bb, i: (bb, i, COL_SQ // wide)),
                  pl.BlockSpec((1, t, GROUP_WIDTH), lambda bb, i: (bb, 0, COL_SK // wide)),
                  pl.BlockSpec((1, t, GROUP_WIDTH), lambda bb, i: (bb, 0, COL_SV // wide)),
                  pl.BlockSpec((1, rows, GROUP_WIDTH), lambda bb, i: (bb, i, COL_SG // wide))],
        out_specs=pl.BlockSpec((1, rows, GROUP_WIDTH), lambda bb, i: (bb, i, 0)),
        out_shape=jax.ShapeDtypeStruct((b, t, GROUP_WIDTH), _MXU_DTYPE),
        scratch_shapes=[pltpu.VMEM((n_heads // 2, t, LANES), _MXU_DTYPE),
                        pltpu.VMEM((n_heads, t, LANES), _MXU_DTYPE)],
        compiler_params=pltpu.CompilerParams(
            dimension_semantics=("arbitrary", "arbitrary"),
            vmem_limit_bytes=_VMEM_LIMIT),
        name="stickbreak",
    )(proj, proj, proj, proj)


def _swa_kernel(sink_ref, q_ref, k_ref, v_ref, g_ref, bias_ref, o_ref,
                k_sc, v_sc, *, tq):
    i = pl.program_id(1)
    t_len = k_ref.shape[1]
    w = SWA_WINDOW
    n_kv = k_sc.shape[0]

    @pl.when(i == 0)
    def _prep():
        kn = k_ref[0]
        v = v_ref[0]
        kn_swapped = pltpu.roll(kn, HEAD_DIM, axis=1)
        v_swapped = pltpu.roll(v, HEAD_DIM, axis=1)
        for kv in range(n_kv):
            keep = _own_half(kv, kn.shape)
            k_sc[kv, 0:w, :] = jnp.zeros((w, LANES), _MXU_DTYPE)
            v_sc[kv, 0:w, :] = jnp.zeros((w, LANES), _MXU_DTYPE)
            k_sc[kv, w:w + t_len, :] = jnp.where(keep, kn, kn_swapped).astype(_MXU_DTYPE)
            v_sc[kv, w:w + t_len, :] = jnp.where(keep, v, v_swapped).astype(_MXU_DTYPE)

    qi =lax.broadcasted_iota(jnp.int32, (w, 2 * w), 0)
    kj = lax.broadcasted_iota(jnp.int32, (w, 2 * w), 1)
    dist = qi + w - kj
    in_window = (dist >= 0) & (dist < w)
    n_sub = tq // w
    chains = [(u, kv, g) for u in range(n_sub) for kv in range(n_kv) for g in range(2)]
    rows = [pl.multiple_of((i * n_sub + u) * w, w) for u in range(n_sub)]
    ss = []
    for u, kv, g in chains:
        q_u = q_ref[0, u * w:(u + 1) * w, kv * LANES:(kv + 1) * LANES] * (ATTN_SCALE * LOG2E)
        q_h = jnp.where(_own_half(g, q_u.shape), q_u, 0.0).astype(_MXU_DTYPE)
        ss.append(lax.dot_general(q_h, k_sc[kv, pl.ds(rows[u], 2 * w), :], _NT,
                                  preferred_element_type=_F32))
    es, dens = [], []
    for (u, kv, g), s in zip(chains, ss):
        allowed = in_window & (kj + (i * n_sub + u - 1) * w >= 0)
        s = jnp.where(allowed, s + bias_ref[2 * kv + g], NEG_INF)
        sink = sink_ref[2 * kv + g] * LOG2E
        m = jnp.maximum(jnp.max(s, axis=-1, keepdims=True), sink)
        e = jnp.exp2(s - m)
        dens.append(jnp.sum(e, axis=-1, keepdims=True) + jnp.exp2(sink - m))
        es.append(e.astype(_MXU_DTYPE))
    outs = [jnp.dot(e, v_sc[kv, pl.ds(rows[u], 2 * w), :], preferred_element_type=_F32) / den
            for (u, kv, g), e, den in zip(chains, es, dens)]
    for u in range(n_sub):
        for kv in range(n_kv):
            c = chains.index((u, kv, 0))
            o = jnp.where(_own_half(0, (w, LANES)), outs[c], outs[c + 1])
            sl = (0, slice(u * w, (u + 1) * w), slice(kv * LANES, (kv + 1) * LANES))
            o_ref[sl] = (o * _silu(g_ref[sl])).astype(o_ref.dtype)


def _swa(proj, sinks, bias_tiles, *, tq=512):
    b, t, _ = proj.shape
    w = SWA_WINDOW
    n_heads = GROUP_WIDTH // HEAD_DIM
    wide = GROUP_WIDTH // LANES
    return pl.pallas_call(
        functools.partial(_swa_kernel, tq=tq),
        grid=(b, t // tq),
        in_specs=[pl.BlockSpec(memory_space=pltpu.SMEM),
                  pl.BlockSpec((1, tq, GROUP_WIDTH), lambda bb, i: (bb, i, COL_WQ // wide)),
                  pl.BlockSpec((1, t, LANES), lambda bb, i: (bb, 0, COL_WK)),
                  pl.BlockSpec((1, t, LANES), lambda bb, i: (bb, 0, COL_WV)),
                  pl.BlockSpec((1, tq, GROUP_WIDTH), lambda bb, i: (bb, i, COL_WG // wide)),
                  pl.BlockSpec((n_heads, w, 2 * w), lambda bb, i: (0, 0, 0))],
        out_specs=pl.BlockSpec((1, tq, GROUP_WIDTH), lambda bb, i: (bb, i, 0)),
        out_shape=jax.ShapeDtypeStruct((b, t, GROUP_WIDTH), _MXU_DTYPE),
        scratch_shapes=[pltpu.VMEM((n_heads // 2, t + w, LANES), _MXU_DTYPE),
                        pltpu.VMEM((n_heads // 2, t + w, LANES), _MXU_DTYPE)],
        compiler_params=pltpu.CompilerParams(
            dimension_semantics=("arbitrary", "arbitrary"),
            vmem_limit_bytes=_VMEM_LIMIT),
        name="swa",
    )(sinks, proj, proj, proj, proj, bias_tiles)


def _outproj_kernel(x_ref, ya_ref, yb_ref, yc_ref, yd_ref, w_ref, o_ref):
    acc = x_ref[...]
    for g, y_ref in enumerate((ya_ref, yb_ref, yc_ref, yd_ref)):
        acc = acc + jnp.dot(y_ref[...], w_ref[g * GROUP_WIDTH:(g + 1) * GROUP_WIDTH, :],
                            preferred_element_type=_F32)
    o_ref[...] = acc


def _outproj(x2d, ys, w_all, layer):
    m, d = x2d.shape
    tm = 512
    y_spec = pl.BlockSpec((tm, GROUP_WIDTH), lambda i: (i, 0))
    return pl.pallas_call(
        _outproj_kernel,
        grid=(m // tm,),
        in_specs=[pl.BlockSpec((tm, d), lambda i: (i, 0)), y_spec, y_spec, y_spec, y_spec,
                  pl.BlockSpec((None,) + w_all.shape[1:], lambda i: (layer, 0, 0))],
        out_specs=pl.BlockSpec((tm, d), lambda i: (i, 0)),
        out_shape=jax.ShapeDtypeStruct((m, d), _F32),
        compiler_params=pltpu.CompilerParams(
            dimension_semantics=("arbitrary",), vmem_limit_bytes=_VMEM_LIMIT),
        name="outproj",
    )(x2d, *ys, w_all)


def _rearrange_w_in(w_in):
    off = 3 * GROUP_WIDTH
    ff = w_in[..., off:off + FOX_FORGET_COLS]
    pad = jnp.zeros(w_in.shape[:-1] + (LANES - FOX_FORGET_COLS,), w_in.dtype)
    out = jnp.concatenate([w_in[..., :off], w_in[..., off + FOX_FORGET_COLS:], ff, pad], axis=-1)
    assert out.shape[-1] == PROJ_WIDTH
    return out


def _qk_logit_bound(qk_gain):
    g = jnp.abs(qk_gain.astype(_F32))
    return ATTN_SCALE * HEAD_DIM * jnp.max(g[0]) * jnp.max(g[1])


def _qk_gain_row(fox_g, moba_g, swa_g):
    def tiles(g, n):
        return jnp.tile(g.astype(_F32), 2 * n)
    row = jnp.zeros((PROJ_WIDTH,), _F32)
    for col, g, n in ((COL_FQ, fox_g[0], 2), (COL_FK, fox_g[1], 2),
                      (COL_MQ, moba_g[0], 2), (COL_MK, moba_g[1], 2),
                      (COL_WQ, swa_g[0], 2), (COL_WK, swa_g[1], 1)):
        row = lax.dynamic_update_slice(row, tiles(g, n), (col * LANES,))
    return row[None, :]


def kernel(x, norm_gain, w_in, b_forget, fox_qk_gain, moba_qk_gain, swa_qk_gain, sinks, w_out, rel_bias):
    b, t, d = x.shape
    depth = w_in.shape[0]
    w_in_r = _rearrange_w_in(w_in.astype(_MXU_DTYPE))
    w_out_c = w_out.astype(_MXU_DTYPE)
    moba_wide_bias = _moba_wide_bias_tiles(rel_bias)
    swa_bias = _swa_bias_tiles(rel_bias)
    moba_bias_max = jnp.max(jnp.abs(rel_bias[:, :rel_bias.shape[1] // 2].astype(_F32)))
    x2d = x.reshape(b * t, d)
    for layer in range(depth):
        qk_gain_row = _qk_gain_row(fox_qk_gain[layer], moba_qk_gain[layer], swa_qk_gain[layer])
        proj = _inproj(x2d, norm_gain[layer][None, :], w_in_r, layer, qk_gain_row)
        proj = proj.reshape(b, t, PROJ_WIDTH)
        bf_row = jnp.pad(b_forget[layer], (0, LANES - FOX_FORGET_COLS))[None, :]
        fox_bound = _qk_logit_bound(fox_qk_gain[layer])
        y_fox = lax.cond(2.0 * fox_bound <= BOUNDED_SOFTMAX_RANGE,
                         lambda: _fox_bounded(proj, b_forget[layer], fox_bound.reshape(1)),
                         lambda: _fox(proj, bf_row))
        moba_bound = _qk_logit_bound(moba_qk_gain[layer]) + moba_bias_max
        y_moba = lax.cond(2.0 * moba_bound <= BOUNDED_SOFTMAX_RANGE,
                          lambda: _moba_bounded(proj, rel_bias, moba_wide_bias,
                                                moba_bound.reshape(1)),
                          lambda: _moba(proj, rel_bias, _moba_bias_tiles(rel_bias)))
        y_sb = _sb(proj)
        y_swa = _swa(proj, sinks[layer], swa_bias)
        ys = [y.reshape(b * t, GROUP_WIDTH) for y in (y_fox, y_moba, y_sb, y_swa)]
        x2d = _outproj(x2d, ys, w_out_c, layer)
    return x2d.reshape(b, t, d)
```
